```python
import math
import jax, jax.numpy as jnp
from jax import lax
import numpy as np

D_MODEL = 1024
BATCH = 8
SEQ = 8192
DEPTH = 2

CHUNK = 64
N_LEFT_CHUNKS = 8
BAND_CHUNKS = N_LEFT_CHUNKS + 1
N_HEADS = 16
HEAD_DIM = D_MODEL // N_HEADS
MAX_REL = 2 * CHUNK
N_REL = 2 * MAX_REL + 1
CONV_WIDTH = 3
D_FF = ((8 * D_MODEL // 3 + 255) // 256) * 256
N_A = DEPTH // 2
N_B = DEPTH - N_A
EPS = 1e-6

kernel_name = "yoco_shortconv_chunkattn_sandwich_adaln"


def rms_norm(x, g):
    xf = x.astype(jnp.float32)
    y = xf * lax.rsqrt(jnp.mean(xf * xf, axis=-1, keepdims=True) + EPS)
    return (y * g.astype(jnp.float32)).astype(x.dtype)


def modulate(h, shift, scale):
    return h * (1.0 + scale[:, None, :]) + shift[:, None, :]


def short_conv_mixer(h, w_in, conv_k, w_out):
    S = h.shape[1]
    bcx = h @ w_in
    b_gate, c_gate, xin = jnp.split(bcx, 3, axis=-1)
    z = c_gate * xin
    zp = jnp.pad(z, ((0, 0), (CONV_WIDTH - 1, 0), (0, 0)))
    conv = sum(conv_k[k] * zp[:, k:k + S] for k in range(CONV_WIDTH))
    return (b_gate * conv) @ w_out


def gather_band(t):
    Bsz, S = t.shape[0], t.shape[1]
    nc = S // CHUNK
    tc = t.reshape(Bsz, nc, CHUNK, N_HEADS, HEAD_DIM)
    tp = jnp.pad(tc, ((0, 0), (N_LEFT_CHUNKS, 0), (0, 0), (0, 0), (0, 0)))
    idx = jnp.arange(nc)[:, None] + jnp.arange(BAND_CHUNKS)[None, :]
    band = tp[:, idx]
    return band.reshape(Bsz, nc, BAND_CHUNKS * CHUNK, N_HEADS, HEAD_DIM)


def chunk_band_attention(h, k_band, v_band, w_q, w_o, rel_bias):
    Bsz, S, _ = h.shape
    nc = S // CHUNK
    q = (h @ w_q).reshape(Bsz, nc, CHUNK, N_HEADS, HEAD_DIM)
    scores = jnp.einsum('bnqhd,bnkhd->bhnqk', q, k_band).astype(jnp.float32)
    scores = scores * (HEAD_DIM ** -0.5)
    a = jnp.arange(CHUNK)[:, None]
    kk = jnp.arange(BAND_CHUNKS * CHUNK)[None, :]
    j, b = kk // CHUNK, kk % CHUNK
    rel = (N_LEFT_CHUNKS - j) * CHUNK + a - b
    rel_idx = jnp.clip(rel, -MAX_REL, MAX_REL) + MAX_REL
    bias = rel_bias.astype(jnp.float32)[:, rel_idx]
    scores = scores + bias[None, :, None]
    key_chunk = jnp.arange(nc)[:, None] + (jnp.arange(BAND_CHUNKS * CHUNK)[None, :] // CHUNK) - N_LEFT_CHUNKS
    valid = key_chunk >= 0
    scores = jnp.where(valid[None, None, :, None, :], scores, jnp.finfo(jnp.float32).min)
    p = jax.nn.softmax(scores, axis=-1).astype(v_band.dtype)
    o = jnp.einsum('bhnqk,bnkhd->bnqhd', p, v_band)
    return o.reshape(Bsz, S, D_MODEL) @ w_o


def swiglu(h, w_in, w_out):
    gu = h @ w_in
    g, u = jnp.split(gu, 2, axis=-1)
    return (jax.nn.silu(g) * u) @ w_out


def _fwd_setup_inputs(seed: int = 0) -> dict:
    key = jax.random.key(seed)
    ks = jax.random.split(key, 20)
    nrm = lambda k, shape, fan: jax.random.normal(k, shape, jnp.float32) * (fan ** -0.5)
    D = D_MODEL
    return {
        "x": jax.random.normal(ks[0], (BATCH, SEQ, D), jnp.float32),
        "c": jax.random.normal(ks[1], (BATCH, D), jnp.float32),
        "mod_w": nrm(ks[2], (DEPTH, D, 6 * D), D) * 0.3,
        "mod_b": 0.05 * jax.random.normal(ks[3], (DEPTH, 6 * D), jnp.float32),
        "norm_g": 1.0 + 0.05 * jax.random.normal(ks[4], (DEPTH, 4, D), jnp.float32),
        "ffn_w_in": nrm(ks[5], (DEPTH, D, 2 * D_FF), D),
        "ffn_w_out": nrm(ks[6], (DEPTH, D_FF, D), D_FF),
        "conv_w_in": nrm(ks[7], (N_A, D, 3 * D), D),
        "conv_k": nrm(ks[8], (N_A, CONV_WIDTH, D), CONV_WIDTH),
        "conv_w_out": nrm(ks[9], (N_A, D, D), D),
        "kv_mod_w": nrm(ks[10], (D, 2 * D), D) * 0.3,
        "kv_mod_b": 0.05 * jax.random.normal(ks[11], (2 * D,), jnp.float32),
        "kv_norm_g": 1.0 + 0.05 * jax.random.normal(ks[12], (D,), jnp.float32),
        "w_kv": nrm(ks[13], (D, 2 * D), D),
        "attn_w_q": nrm(ks[14], (N_B, D, D), D),
        "attn_w_o": nrm(ks[15], (N_B, D, D), D),
        "rel_bias": 0.5 * jax.random.normal(ks[16], (N_B, N_HEADS, N_REL), jnp.float32),
    }


def _fwd_reference(x, c, mod_w, mod_b, norm_g, ffn_w_in, ffn_w_out, conv_w_in, conv_k,
              conv_w_out, kv_mod_w, kv_mod_b, kv_norm_g, w_kv, attn_w_q, attn_w_o,
              rel_bias):
    Bsz, S, _ = x.shape
    silu_c = jax.nn.silu(c)
    k_band = None
    v_band = None
    for layer in range(DEPTH):
        mod = silu_c @ mod_w[layer] + mod_b[layer]
        sh1, sc1, g1, sh2, sc2, g2 = jnp.split(mod, 6, axis=-1)
        h = modulate(rms_norm(x, norm_g[layer, 0]), sh1, sc1)
        if layer < N_A:
            y = short_conv_mixer(h, conv_w_in[layer], conv_k[layer], conv_w_out[layer])
        else:
            if layer == N_A:
                kv_sh, kv_sc = jnp.split(silu_c @ kv_mod_w + kv_mod_b, 2, axis=-1)
                hkv = modulate(rms_norm(x, kv_norm_g), kv_sh, kv_sc)
                k, v = jnp.split(hkv @ w_kv, 2, axis=-1)
                k_band = gather_band(k.reshape(Bsz, S, N_HEADS, HEAD_DIM))
                v_band = gather_band(v.reshape(Bsz, S, N_HEADS, HEAD_DIM))
            bi = layer - N_A
            y = chunk_band_attention(h, k_band, v_band, attn_w_q[bi], attn_w_o[bi], rel_bias[bi])
        x = x + g1[:, None, :] * rms_norm(y, norm_g[layer, 1])
        h = modulate(rms_norm(x, norm_g[layer, 2]), sh2, sc2)
        y = swiglu(h, ffn_w_in[layer], ffn_w_out[layer])
        x = x + g2[:, None, :] * rms_norm(y, norm_g[layer, 3])
    return x


import jax as _jax
import jax.numpy as _jnp

TWIN_FORMAT = 'train_step'
FWD_PARAMS = ['x', 'c', 'mod_w', 'mod_b', 'norm_g', 'ffn_w_in', 'ffn_w_out', 'conv_w_in', 'conv_k', 'conv_w_out', 'kv_mod_w', 'kv_mod_b', 'kv_norm_g', 'w_kv', 'attn_w_q', 'attn_w_o', 'rel_bias']
TWIN_WEIGHTS = ['mod_w', 'mod_b', 'norm_g', 'ffn_w_in', 'ffn_w_out', 'conv_w_in', 'conv_k', 'conv_w_out', 'kv_mod_w', 'kv_mod_b', 'kv_norm_g', 'w_kv', 'attn_w_q', 'attn_w_o', 'rel_bias']
TWIN_DIFF_INPUT = 'x'
TWIN_INPUTS = ['x', 'c', 'mod_w', 'mod_b', 'norm_g', 'ffn_w_in', 'ffn_w_out', 'conv_w_in', 'conv_k', 'conv_w_out', 'kv_mod_w', 'kv_mod_b', 'kv_norm_g', 'w_kv', 'attn_w_q', 'attn_w_o', 'rel_bias', 'loss_target', 'm_mod_w', 'm_mod_b', 'm_norm_g', 'm_ffn_w_in', 'm_ffn_w_out', 'm_conv_w_in', 'm_conv_k', 'm_conv_w_out', 'm_kv_mod_w', 'm_kv_mod_b', 'm_kv_norm_g', 'm_w_kv', 'm_attn_w_q', 'm_attn_w_o', 'm_rel_bias', 'v_mod_w', 'v_mod_b', 'v_norm_g', 'v_ffn_w_in', 'v_ffn_w_out', 'v_conv_w_in', 'v_conv_k', 'v_conv_w_out', 'v_kv_mod_w', 'v_kv_mod_b', 'v_kv_norm_g', 'v_w_kv', 'v_attn_w_q', 'v_attn_w_o', 'v_rel_bias']
TWIN_OUTPUTS = ['loss', 'grad_x', 'grad_mod_w', 'grad_mod_b', 'grad_norm_g', 'grad_ffn_w_in', 'grad_ffn_w_out', 'grad_conv_w_in', 'grad_conv_k', 'grad_conv_w_out', 'grad_kv_mod_w', 'grad_kv_mod_b', 'grad_kv_norm_g', 'grad_w_kv', 'grad_attn_w_q', 'grad_attn_w_o', 'grad_rel_bias', 'delta_mod_w', 'delta_mod_b', 'delta_norm_g', 'delta_ffn_w_in', 'delta_ffn_w_out', 'delta_conv_w_in', 'delta_conv_k', 'delta_conv_w_out', 'delta_kv_mod_w', 'delta_kv_mod_b', 'delta_kv_norm_g', 'delta_w_kv', 'delta_attn_w_q', 'delta_attn_w_o', 'delta_rel_bias', 'new_m_mod_w', 'new_m_mod_b', 'new_m_norm_g', 'new_m_ffn_w_in', 'new_m_ffn_w_out', 'new_m_conv_w_in', 'new_m_conv_k', 'new_m_conv_w_out', 'new_m_kv_mod_w', 'new_m_kv_mod_b', 'new_m_kv_norm_g', 'new_m_w_kv', 'new_m_attn_w_q', 'new_m_attn_w_o', 'new_m_rel_bias', 'new_v_mod_w', 'new_v_mod_b', 'new_v_norm_g', 'new_v_ffn_w_in', 'new_v_ffn_w_out', 'new_v_conv_w_in', 'new_v_conv_k', 'new_v_conv_w_out', 'new_v_kv_mod_w', 'new_v_kv_mod_b', 'new_v_kv_norm_g', 'new_v_w_kv', 'new_v_attn_w_q', 'new_v_attn_w_o', 'new_v_rel_bias']
TWIN_LEAF_KINDS = {'loss': 'loss', 'grad_x': 'grad_x', 'grad_mod_w': 'grad_w', 'grad_mod_b': 'grad_w', 'grad_norm_g': 'grad_w', 'grad_ffn_w_in': 'grad_w', 'grad_ffn_w_out': 'grad_w', 'grad_conv_w_in': 'grad_w', 'grad_conv_k': 'grad_w', 'grad_conv_w_out': 'grad_w', 'grad_kv_mod_w': 'grad_w', 'grad_kv_mod_b': 'grad_w', 'grad_kv_norm_g': 'grad_w', 'grad_w_kv': 'grad_w', 'grad_attn_w_q': 'grad_w', 'grad_attn_w_o': 'grad_w', 'grad_rel_bias': 'grad_w', 'delta_mod_w': 'delta_w', 'delta_mod_b': 'delta_w', 'delta_norm_g': 'delta_w', 'delta_ffn_w_in': 'delta_w', 'delta_ffn_w_out': 'delta_w', 'delta_conv_w_in': 'delta_w', 'delta_conv_k': 'delta_w', 'delta_conv_w_out': 'delta_w', 'delta_kv_mod_w': 'delta_w', 'delta_kv_mod_b': 'delta_w', 'delta_kv_norm_g': 'delta_w', 'delta_w_kv': 'delta_w', 'delta_attn_w_q': 'delta_w', 'delta_attn_w_o': 'delta_w', 'delta_rel_bias': 'delta_w', 'new_m_mod_w': 'new_m', 'new_m_mod_b': 'new_m', 'new_m_norm_g': 'new_m', 'new_m_ffn_w_in': 'new_m', 'new_m_ffn_w_out': 'new_m', 'new_m_conv_w_in': 'new_m', 'new_m_conv_k': 'new_m', 'new_m_conv_w_out': 'new_m', 'new_m_kv_mod_w': 'new_m', 'new_m_kv_mod_b': 'new_m', 'new_m_kv_norm_g': 'new_m', 'new_m_w_kv': 'new_m', 'new_m_attn_w_q': 'new_m', 'new_m_attn_w_o': 'new_m', 'new_m_rel_bias': 'new_m', 'new_v_mod_w': 'new_v', 'new_v_mod_b': 'new_v', 'new_v_norm_g': 'new_v', 'new_v_ffn_w_in': 'new_v', 'new_v_ffn_w_out': 'new_v', 'new_v_conv_w_in': 'new_v', 'new_v_conv_k': 'new_v', 'new_v_conv_w_out': 'new_v', 'new_v_kv_mod_w': 'new_v', 'new_v_kv_mod_b': 'new_v', 'new_v_kv_norm_g': 'new_v', 'new_v_w_kv': 'new_v', 'new_v_attn_w_q': 'new_v', 'new_v_attn_w_o': 'new_v', 'new_v_rel_bias': 'new_v'}


def _forward(args):
    return _fwd_reference(*[args[k] for k in FWD_PARAMS])


def _output_shape():
    def fwd():
        inp = _fwd_setup_inputs(0)
        return _fwd_reference(*[inp[k] for k in FWD_PARAMS])
    out = _jax.eval_shape(fwd)
    return out.shape, out.dtype

N_MICROBATCH = 1
ADAM_LR = 0.001
ADAM_B1 = 0.9
ADAM_B2 = 0.999
ADAM_EPS = 1e-08
ADAM_WD = 0.01
ADAM_STEP = 10
PER_EXAMPLE_BATCH_AXIS = {'x': 0, 'c': 0, 'loss_target': 0}
SHARED_INPUTS = []
_WEIGHT_DTYPES = {'mod_w': _jnp.float32, 'mod_b': _jnp.float32, 'norm_g': _jnp.float32, 'ffn_w_in': _jnp.float32, 'ffn_w_out': _jnp.float32, 'conv_w_in': _jnp.float32, 'conv_k': _jnp.float32, 'conv_w_out': _jnp.float32, 'kv_mod_w': _jnp.float32, 'kv_mod_b': _jnp.float32, 'kv_norm_g': _jnp.float32, 'w_kv': _jnp.float32, 'attn_w_q': _jnp.float32, 'attn_w_o': _jnp.float32, 'rel_bias': _jnp.float32}
MOMENT_SCALE = {'mod_w': 1.792848e+00, 'mod_b': 4.001963e+00, 'norm_g': 1.959490e+00, 'ffn_w_in': 5.405599e-02, 'ffn_w_out': 9.944594e-02, 'conv_w_in': 7.700781e-02, 'conv_k': 8.190272e-02, 'conv_w_out': 8.152902e-02, 'kv_mod_w': 1.810374e+00, 'kv_mod_b': 3.143529e+00, 'kv_norm_g': 1.087837e-01, 'w_kv': 6.535645e-01, 'attn_w_q': 4.902918e-02, 'attn_w_o': 8.675613e-01, 'rel_bias': 2.216366e-02}


def _to_microbatches(a, axis):
    t = _jnp.moveaxis(a, axis, 0)
    t = t.reshape((N_MICROBATCH, t.shape[0] // N_MICROBATCH) + t.shape[1:])
    return _jnp.moveaxis(t, 1, axis + 1)


def setup_inputs(seed: int = 0) -> dict:
    inp = _fwd_setup_inputs(seed)
    key = _jax.random.fold_in(_jax.random.key(seed), 7919)
    shape, _ = _output_shape()
    out = dict(inp)
    out["loss_target"] = _jax.random.normal(_jax.random.fold_in(key, 0), shape, _jnp.float32)
    for i, name in enumerate(TWIN_WEIGHTS):
        w = inp[name].astype(_jnp.float32)
        if MOMENT_SCALE is None:
            s = _jnp.sqrt(_jnp.mean(_jnp.square(w)) + 1e-30)
        else:
            s = MOMENT_SCALE[name]
        km, kv = _jax.random.split(_jax.random.fold_in(key, i + 1))
        out[name] = w
        out["m_" + name] = s * _jax.random.normal(km, w.shape, _jnp.float32)
        out["v_" + name] = (s * s) * _jax.random.uniform(kv, w.shape, _jnp.float32, 0.5, 1.5)
    if N_MICROBATCH > 1:
        for name, axis in PER_EXAMPLE_BATCH_AXIS.items():
            out[name] = _to_microbatches(out[name], axis)
    return {'x': out['x'], 'c': out['c'], 'mod_w': out['mod_w'], 'mod_b': out['mod_b'], 'norm_g': out['norm_g'], 'ffn_w_in': out['ffn_w_in'], 'ffn_w_out': out['ffn_w_out'], 'conv_w_in': out['conv_w_in'], 'conv_k': out['conv_k'], 'conv_w_out': out['conv_w_out'], 'kv_mod_w': out['kv_mod_w'], 'kv_mod_b': out['kv_mod_b'], 'kv_norm_g': out['kv_norm_g'], 'w_kv': out['w_kv'], 'attn_w_q': out['attn_w_q'], 'attn_w_o': out['attn_w_o'], 'rel_bias': out['rel_bias'], 'loss_target': out['loss_target'], 'm_mod_w': out['m_mod_w'], 'm_mod_b': out['m_mod_b'], 'm_norm_g': out['m_norm_g'], 'm_ffn_w_in': out['m_ffn_w_in'], 'm_ffn_w_out': out['m_ffn_w_out'], 'm_conv_w_in': out['m_conv_w_in'], 'm_conv_k': out['m_conv_k'], 'm_conv_w_out': out['m_conv_w_out'], 'm_kv_mod_w': out['m_kv_mod_w'], 'm_kv_mod_b': out['m_kv_mod_b'], 'm_kv_norm_g': out['m_kv_norm_g'], 'm_w_kv': out['m_w_kv'], 'm_attn_w_q': out['m_attn_w_q'], 'm_attn_w_o': out['m_attn_w_o'], 'm_rel_bias': out['m_rel_bias'], 'v_mod_w': out['v_mod_w'], 'v_mod_b': out['v_mod_b'], 'v_norm_g': out['v_norm_g'], 'v_ffn_w_in': out['v_ffn_w_in'], 'v_ffn_w_out': out['v_ffn_w_out'], 'v_conv_w_in': out['v_conv_w_in'], 'v_conv_k': out['v_conv_k'], 'v_conv_w_out': out['v_conv_w_out'], 'v_kv_mod_w': out['v_kv_mod_w'], 'v_kv_mod_b': out['v_kv_mod_b'], 'v_kv_norm_g': out['v_kv_norm_g'], 'v_w_kv': out['v_w_kv'], 'v_attn_w_q': out['v_attn_w_q'], 'v_attn_w_o': out['v_attn_w_o'], 'v_rel_bias': out['v_rel_bias']}


def _loss(weights, diff, rest, loss_target):
    with _jax.named_scope("forward"):
        args = {**rest, TWIN_DIFF_INPUT: diff, **{k: w.astype(_WEIGHT_DTYPES[k]) for k, w in weights.items()}}
        y = _forward(args)
    with _jax.named_scope("loss_head"):
        err = _jnp.square(y.astype(_jnp.float32) - loss_target)
        return 0.5 * _jnp.sum(_jnp.mean(err, axis=-1)) if err.ndim else 0.5 * err


def _adamw(w, g, m, v):
    m = ADAM_B1 * m + (1.0 - ADAM_B1) * g
    v = ADAM_B2 * v + (1.0 - ADAM_B2) * _jnp.square(g)
    m_hat = m / (1.0 - ADAM_B1 ** ADAM_STEP)
    v_hat = v / (1.0 - ADAM_B2 ** ADAM_STEP)
    delta = -ADAM_LR * (m_hat / (_jnp.sqrt(v_hat) + ADAM_EPS) + ADAM_WD * w)
    return delta, m, v


def reference(x, c, mod_w, mod_b, norm_g, ffn_w_in, ffn_w_out, conv_w_in, conv_k, conv_w_out, kv_mod_w, kv_mod_b, kv_norm_g, w_kv, attn_w_q, attn_w_o, rel_bias, loss_target, m_mod_w, m_mod_b, m_norm_g, m_ffn_w_in, m_ffn_w_out, m_conv_w_in, m_conv_k, m_conv_w_out, m_kv_mod_w, m_kv_mod_b, m_kv_norm_g, m_w_kv, m_attn_w_q, m_attn_w_o, m_rel_bias, v_mod_w, v_mod_b, v_norm_g, v_ffn_w_in, v_ffn_w_out, v_conv_w_in, v_conv_k, v_conv_w_out, v_kv_mod_w, v_kv_mod_b, v_kv_norm_g, v_w_kv, v_attn_w_q, v_attn_w_o, v_rel_bias):
    given = dict(x=x, c=c, mod_w=mod_w, mod_b=mod_b, norm_g=norm_g, ffn_w_in=ffn_w_in, ffn_w_out=ffn_w_out, conv_w_in=conv_w_in, conv_k=conv_k, conv_w_out=conv_w_out, kv_mod_w=kv_mod_w, kv_mod_b=kv_mod_b, kv_norm_g=kv_norm_g, w_kv=w_kv, attn_w_q=attn_w_q, attn_w_o=attn_w_o, rel_bias=rel_bias, loss_target=loss_target, m_mod_w=m_mod_w, m_mod_b=m_mod_b, m_norm_g=m_norm_g, m_ffn_w_in=m_ffn_w_in, m_ffn_w_out=m_ffn_w_out, m_conv_w_in=m_conv_w_in, m_conv_k=m_conv_k, m_conv_w_out=m_conv_w_out, m_kv_mod_w=m_kv_mod_w, m_kv_mod_b=m_kv_mod_b, m_kv_norm_g=m_kv_norm_g, m_w_kv=m_w_kv, m_attn_w_q=m_attn_w_q, m_attn_w_o=m_attn_w_o, m_rel_bias=m_rel_bias, v_mod_w=v_mod_w, v_mod_b=v_mod_b, v_norm_g=v_norm_g, v_ffn_w_in=v_ffn_w_in, v_ffn_w_out=v_ffn_w_out, v_conv_w_in=v_conv_w_in, v_conv_k=v_conv_k, v_conv_w_out=v_conv_w_out, v_kv_mod_w=v_kv_mod_w, v_kv_mod_b=v_kv_mod_b, v_kv_norm_g=v_kv_norm_g, v_w_kv=v_w_kv, v_attn_w_q=v_attn_w_q, v_attn_w_o=v_attn_w_o, v_rel_bias=v_rel_bias)
    weights = {n: given[n] for n in TWIN_WEIGHTS}
    shared = {n: given[n] for n in SHARED_INPUTS}
    per_example = {n: given[n] for n in ['x', 'c']}
    grad_fn = _jax.value_and_grad(_loss, argnums=(0, 1))

    def one_microbatch(ex, loss_target):
        ex = dict(ex)
        diff = ex.pop(TWIN_DIFF_INPUT)
        return grad_fn(weights, diff, {**shared, **ex}, loss_target)

    if N_MICROBATCH == 1:
        loss, (grad_w, grad_x) = one_microbatch(per_example, given["loss_target"])
    else:
        def body(carry, xs):
            loss_sum, grad_sum = carry
            l_k, (gw_k, gx_k) = one_microbatch(xs[0], xs[1])
            with _jax.named_scope("update"):
                return (loss_sum + l_k, _jax.tree.map(_jnp.add, grad_sum, gw_k)), gx_k

        init = (_jnp.zeros((), _jnp.float32), _jax.tree.map(_jnp.zeros_like, weights))
        (loss, grad_w), grad_x = _jax.lax.scan(body, init, (per_example, given["loss_target"]))
    with _jax.named_scope("update"):
        delta_w, new_m, new_v = {}, {}, {}
        for n in TWIN_WEIGHTS:
            delta_w[n], new_m[n], new_v[n] = _adamw(weights[n], grad_w[n], given["m_" + n], given["v_" + n])
    return (loss, grad_x, *[grad_w[n] for n in TWIN_WEIGHTS], *[delta_w[n] for n in TWIN_WEIGHTS],
            *[new_m[n] for n in TWIN_WEIGHTS], *[new_v[n] for n in TWIN_WEIGHTS])
```

```python
import functools
import math

import numpy as np
import jax
import jax.numpy as jnp
from jax import lax
from jax.experimental import pallas as pl
from jax.experimental.pallas import tpu as pltpu

CHUNK = 64
N_LEFT_CHUNKS = 8
N_HEADS = 16
MAX_REL = 2 * CHUNK
N_REL = 2 * MAX_REL + 1
EPS = 1e-6
ADAM_LR = 0.001
ADAM_B1 = 0.9
ADAM_B2 = 0.999
ADAM_EPS = 1e-08
ADAM_WD = 0.01
ADAM_STEP = 10

Q_CHUNKS = 4
BQ = Q_CHUNKS * CHUNK
N_WIN = 1 + N_LEFT_CHUNKS // Q_CHUNKS
HEADS_PER_STEP = 2
NEG = -1e30
N_DEV = 8
N_CHIP = 4

BF16 = jnp.bfloat16
F32 = jnp.float32
V7X_VMEM_LIMIT_BYTES = 56 * 1024 * 1024
MESH = pl.DeviceIdType.MESH


def _pick(n, pref, align):
    t = min(pref, n)
    t -= t % align
    while t >= align:
        if n % t == 0:
            return t
        t -= align
    return n


def _params(*sem):
    return pltpu.CompilerParams(dimension_semantics=sem, vmem_limit_bytes=V7X_VMEM_LIMIT_BYTES)


def _colsum8(v):
    r, d = v.shape
    return v.reshape(r // 8, 8, d).sum(axis=0)


_DIMS = {"nn": (((1,), (0,)), ((), ())), "nt": (((1,), (1,)), ((), ())), "tn": (((0,), (0,)), ((), ()))}


def _mm(a, b, mode, out_dtype, name, *, b_layer=None, tm=1024, tn=1024, tk=1024):
    bs = b.shape[1:] if b_layer is not None else b.shape
    if mode == "nn":
        (M, K), (K2, N) = a.shape, bs
    elif mode == "nt":
        (M, K), (N, K2) = a.shape, bs
    else:
        (K, M), (K2, N) = a.shape, bs
    assert K == K2, (name, a.shape, b.shape)
    tm = _pick(M, tm, 128 if mode == "tn" else 16)
    tn = _pick(N, tn, 128)
    tk = _pick(K, tk, 128 if mode != "tn" else 16)
    nk = K // tk
    dims = _DIMS[mode]

    def body(a_ref, b_ref, o_ref, *acc):
        p = lax.dot_general(a_ref[...].astype(BF16), b_ref[...].astype(BF16), dims,
                            preferred_element_type=F32)
        if nk == 1:
            o_ref[...] = p.astype(o_ref.dtype)
        else:
            k = pl.program_id(2)

            @pl.when(k == 0)
            def _():
                acc[0][...] = p

            @pl.when(k > 0)
            def _():
                acc[0][...] += p

            @pl.when(k == nk - 1)
            def _():
                o_ref[...] = acc[0][...].astype(o_ref.dtype)

    a_spec = (pl.BlockSpec((tk, tm), lambda i, j, k: (k, i)) if mode == "tn"
              else pl.BlockSpec((tm, tk), lambda i, j, k: (i, k)))
    if mode == "nt":
        b_blk, b_idx = (tn, tk), (lambda i, j, k: (j, k))
    else:
        b_blk, b_idx = (tk, tn), (lambda i, j, k: (k, j))
    if b_layer is not None:
        b_spec = pl.BlockSpec((None,) + b_blk, lambda i, j, k: (b_layer,) + b_idx(i, j, k))
    else:
        b_spec = pl.BlockSpec(b_blk, b_idx)
    return pl.pallas_call(
        body, name=name,
        grid=(M // tm, N // tn, nk),
        in_specs=[a_spec, b_spec],
        out_specs=pl.BlockSpec((tm, tn), lambda i, j, k: (i, j)),
        out_shape=jax.ShapeDtypeStruct((M, N), out_dtype),
        scratch_shapes=[pltpu.VMEM((tm, tn), F32)] if nk > 1 else [],
        compiler_params=_params("parallel", "parallel", "arbitrary"),
    )(a, b)


def _row_spec(tm, d):
    return pl.BlockSpec((tm, d), lambda i: (i, 0))


def _vec_spec(r, d):
    return pl.BlockSpec((r, d), lambda i: (0, 0))


def _norm_mod(x, scales, shifts, name):
    S, D = x.shape
    nb = scales.shape[0]
    tm = _pick(S, 512, 16)

    def body(x_ref, a_ref, b_ref, *o_refs):
        xv = x_ref[...]
        xh = xv * lax.rsqrt(jnp.mean(xv * xv, axis=-1, keepdims=True) + EPS)
        for n in range(nb):
            o_refs[n][...] = (xh * a_ref[n:n + 1, :] + b_ref[n:n + 1, :]).astype(BF16)

    return pl.pallas_call(
        body, name=name, grid=(S // tm,),
        in_specs=[_row_spec(tm, D), _vec_spec(nb, D), _vec_spec(nb, D)],
        out_specs=[_row_spec(tm, D)] * nb,
        out_shape=[jax.ShapeDtypeStruct((S, D), BF16)] * nb,
        compiler_params=_params("parallel"),
    )(x, scales, shifts)


def _post_norm(x, y, gate, name):
    S, D = x.shape
    tm = _pick(S, 512, 8)

    def body(x_ref, y_ref, g_ref, o_ref):
        yv = y_ref[...]
        yh = yv * lax.rsqrt(jnp.mean(yv * yv, axis=-1, keepdims=True) + EPS)
        o_ref[...] = x_ref[...] + yh * g_ref[...]

    return pl.pallas_call(
        body, name=name, grid=(S // tm,),
        in_specs=[_row_spec(tm, D), _row_spec(tm, D), _vec_spec(1, D)],
        out_specs=_row_spec(tm, D),
        out_shape=jax.ShapeDtypeStruct((S, D), F32),
        compiler_params=_params("parallel"),
    )(x, y, gate)


def _loss_grad(x, target, name):
    S, D = x.shape
    tm = _pick(S, 512, 8)

    def body(x_ref, t_ref, dx_ref, sq_ref):
        e = x_ref[...] - t_ref[...]
        dx_ref[...] = e / D

        @pl.when(pl.program_id(0) == 0)
        def _():
            sq_ref[...] = jnp.zeros_like(sq_ref)

        sq_ref[...] += _colsum8(e * e)

    return pl.pallas_call(
        body, name=name, grid=(S // tm,),
        in_specs=[_row_spec(tm, D), _row_spec(tm, D)],
        out_specs=[_row_spec(tm, D), _vec_spec(8, D)],
        out_shape=[jax.ShapeDtypeStruct((S, D), F32), jax.ShapeDtypeStruct((8, D), F32)],
        compiler_params=_params("arbitrary"),
    )(x, target)


def _post_norm_bwd(dxn, y, gate, name):
    S, D = y.shape
    tm = _pick(S, 512, 16)

    def body(d_ref, y_ref, g_ref, dy_ref, dg_ref):
        yv = y_ref[...]
        dv = d_ref[...]
        r = lax.rsqrt(jnp.mean(yv * yv, axis=-1, keepdims=True) + EPS)
        yh = yv * r
        dyh = dv * g_ref[...]
        dy_ref[...] = (r * (dyh - yh * jnp.mean(dyh * yh, axis=-1, keepdims=True))).astype(BF16)

        @pl.when(pl.program_id(0) == 0)
        def _():
            dg_ref[...] = jnp.zeros_like(dg_ref)

        dg_ref[...] += _colsum8(dv * yh)

    return pl.pallas_call(
        body, name=name, grid=(S // tm,),
        in_specs=[_row_spec(tm, D), _row_spec(tm, D), _vec_spec(1, D)],
        out_specs=[_row_spec(tm, D), _vec_spec(8, D)],
        out_shape=[jax.ShapeDtypeStruct((S, D), BF16), jax.ShapeDtypeStruct((8, D), F32)],
        compiler_params=_params("arbitrary"),
    )(dxn, y, gate)


def _pre_norm_bwd(x, dxn, dhs, scales, name):
    S, D = x.shape
    nb = len(dhs)
    tm = _pick(S, 512, 8)

    def body(x_ref, d_ref, a_ref, *rest):
        dh_refs, dx_ref, ds_ref, db_ref = rest[:nb], rest[nb], rest[nb + 1], rest[nb + 2]
        xv = x_ref[...]
        r = lax.rsqrt(jnp.mean(xv * xv, axis=-1, keepdims=True) + EPS)
        xh = xv * r

        @pl.when(pl.program_id(0) == 0)
        def _():
            ds_ref[...] = jnp.zeros_like(ds_ref)
            db_ref[...] = jnp.zeros_like(db_ref)

        dxh = jnp.zeros_like(xv)
        for n in range(nb):
            dh = dh_refs[n][...].astype(F32)
            dxh = dxh + dh * a_ref[n:n + 1, :]
            ds_ref[n] += _colsum8(dh * xh)
            db_ref[n] += _colsum8(dh)
        dx_ref[...] = d_ref[...] + r * (dxh - xh * jnp.mean(dxh * xh, axis=-1, keepdims=True))

    acc_spec = pl.BlockSpec((nb, 8, D), lambda i: (0, 0, 0))
    return pl.pallas_call(
        body, name=name, grid=(S // tm,),
        in_specs=[_row_spec(tm, D), _row_spec(tm, D), _vec_spec(nb, D)] + [_row_spec(tm, D)] * nb,
        out_specs=[_row_spec(tm, D), acc_spec, acc_spec],
        out_shape=[jax.ShapeDtypeStruct((S, D), F32), jax.ShapeDtypeStruct((nb, 8, D), F32),
                   jax.ShapeDtypeStruct((nb, 8, D), F32)],
        compiler_params=_params("arbitrary"),
    )(x, dxn, scales, *dhs)


def _swiglu(gu, name):
    S, F2 = gu.shape
    F = F2 // 2
    tm = _pick(S, 256, 16)

    def body(gu_ref, o_ref):
        g = gu_ref[:, 0:F]
        o_ref[...] = (g * jax.nn.sigmoid(g) * gu_ref[:, F:F2]).astype(BF16)

    return pl.pallas_call(
        body, name=name, grid=(S // tm,),
        in_specs=[_row_spec(tm, F2)], out_specs=_row_spec(tm, F),
        out_shape=jax.ShapeDtypeStruct((S, F), BF16),
        compiler_params=_params("parallel"),
    )(gu)


def _swiglu_bwd(da, gu, name):
    S, F2 = gu.shape
    F = F2 // 2
    tm = _pick(S, 128, 16)

    def body(da_ref, gu_ref, o_ref):
        g = gu_ref[:, 0:F]
        d = da_ref[...]
        sg = jax.nn.sigmoid(g)
        o_ref[:, 0:F] = (d * gu_ref[:, F:F2] * (sg * (1.0 + g * (1.0 - sg)))).astype(BF16)
        o_ref[:, F:F2] = (d * (g * sg)).astype(BF16)

    return pl.pallas_call(
        body, name=name, grid=(S // tm,),
        in_specs=[_row_spec(tm, F), _row_spec(tm, F2)], out_specs=_row_spec(tm, F2),
        out_shape=jax.ShapeDtypeStruct((S, F2), BF16),
        compiler_params=_params("parallel"),
    )(da, gu)


def _conv_terms(bcx_ref, prev_ref, i, tm, D):
    b = bcx_ref[:, 0:D]
    cg = bcx_ref[:, D:2 * D]
    xin = bcx_ref[:, 2 * D:3 * D]
    z = cg * xin
    zp = prev_ref[:, D:2 * D] * prev_ref[:, 2 * D:3 * D]
    zp = jnp.where(i > 0, zp, 0.0)
    row = lax.broadcasted_iota(jnp.int32, (tm, D), 0)
    z1 = jnp.where(row == 0, zp[7:8, :], pltpu.roll(z, 1, 0))
    z2 = jnp.where(row == 0, zp[6:7, :], jnp.where(row == 1, zp[7:8, :], pltpu.roll(z, 2, 0)))
    return b, cg, xin, z, z1, z2, row


def _conv_gate(bcx, ck, name):
    S, D3 = bcx.shape
    D = D3 // 3
    tm = _pick(S, 256, 16)
    hb = tm // 8

    def body(bcx_ref, prev_ref, ck_ref, o_ref):
        i = pl.program_id(0)
        b, _, _, z, z1, z2, _ = _conv_terms(bcx_ref, prev_ref, i, tm, D)
        conv = ck_ref[0:1, :] * z2 + ck_ref[1:2, :] * z1 + ck_ref[2:3, :] * z
        o_ref[...] = (b * conv).astype(BF16)

    return pl.pallas_call(
        body, name=name, grid=(S // tm,),
        in_specs=[_row_spec(tm, D3),
                  pl.BlockSpec((8, D3), lambda i: (jnp.maximum(i * hb - 1, 0), 0)),
                  _vec_spec(8, D)],
        out_specs=_row_spec(tm, D),
        out_shape=jax.ShapeDtypeStruct((S, D), BF16),
        compiler_params=_params("parallel"),
    )(bcx, bcx, ck)


def _conv_gate_bwd(du, bcx, ck, name):
    S, D3 = bcx.shape
    D = D3 // 3
    tm = _pick(S, 256, 16)
    hb = tm // 8
    nt = S // tm

    def body(du_ref, dun_ref, bcx_ref, prev_ref, next_ref, ck_ref, o_ref, dk_ref):
        i = pl.program_id(0)
        b, cg, xin, z, z1, z2, row = _conv_terms(bcx_ref, prev_ref, i, tm, D)
        k0, k1, k2 = ck_ref[0:1, :], ck_ref[1:2, :], ck_ref[2:3, :]
        conv = k0 * z2 + k1 * z1 + k2 * z
        d = du_ref[...]
        dconv = d * b
        dcn = jnp.where(i < nt - 1, dun_ref[...] * next_ref[:, 0:D], 0.0)
        d1 = jnp.where(row == tm - 1, dcn[0:1, :], pltpu.roll(dconv, tm - 1, 0))
        d2 = jnp.where(row == tm - 2, dcn[0:1, :],
                       jnp.where(row == tm - 1, dcn[1:2, :], pltpu.roll(dconv, tm - 2, 0)))
        dz = k2 * dconv + k1 * d1 + k0 * d2
        o_ref[:, 0:D] = (d * conv).astype(BF16)
        o_ref[:, D:2 * D] = (dz * xin).astype(BF16)
        o_ref[:, 2 * D:3 * D] = (dz * cg).astype(BF16)

        @pl.when(i == 0)
        def _():
            dk_ref[...] = jnp.zeros_like(dk_ref)

        dk_ref[0] += _colsum8(dconv * z2)
        dk_ref[1] += _colsum8(dconv * z1)
        dk_ref[2] += _colsum8(dconv * z)

    last = S // 8 - 1
    return pl.pallas_call(
        body, name=name, grid=(nt,),
        in_specs=[_row_spec(tm, D),
                  pl.BlockSpec((8, D), lambda i: (jnp.minimum((i + 1) * hb, last), 0)),
                  _row_spec(tm, D3),
                  pl.BlockSpec((8, D3), lambda i: (jnp.maximum(i * hb - 1, 0), 0)),
                  pl.BlockSpec((8, D3), lambda i: (jnp.minimum((i + 1) * hb, last), 0)),
                  _vec_spec(8, D)],
        out_specs=[_row_spec(tm, D3), pl.BlockSpec((3, 8, D), lambda i: (0, 0, 0))],
        out_shape=[jax.ShapeDtypeStruct((S, D3), BF16), jax.ShapeDtypeStruct((3, 8, D), F32)],
        compiler_params=_params("arbitrary"),
    )(du, du, bcx, bcx, bcx, ck)


def _rel_onehot():
    a = np.arange(CHUNK)[:, None]
    b = np.arange(CHUNK)[None, :]
    idx = np.stack([np.clip((N_LEFT_CHUNKS - dl) * CHUNK + a - b, -MAX_REL, MAX_REL) + MAX_REL
                    for dl in (6, 7, 8)]).reshape(-1)
    return (jnp.asarray(idx)[:, None] == jnp.arange(N_REL)[None, :]).astype(F32)


def _bias_table(rel_bias):
    H = rel_bias.shape[0]
    near = jnp.dot(rel_bias, _rel_onehot().T, precision=lax.Precision.HIGHEST).reshape(H, 3, CHUNK, CHUNK)
    far = jnp.broadcast_to(rel_bias[:, N_REL - 1][:, None, None], (H, CHUNK, CHUNK))
    neg = jnp.full((H, CHUNK, CHUNK), NEG, F32)

    def block(dl):
        if dl < 0 or dl > N_LEFT_CHUNKS:
            return neg
        return far if dl <= 5 else near[:, dl - 6]

    rows = [jnp.concatenate([block(jc - ic) for jc in range(N_WIN * Q_CHUNKS)], axis=-1)
            for ic in range(Q_CHUNKS)]
    return jnp.concatenate(rows, axis=-2)


def _bias_table_grad(dtab):
    H = dtab.shape[0]
    blk = lambda ic, jc: dtab[:, ic * CHUNK:(ic + 1) * CHUNK, jc * CHUNK:(jc + 1) * CHUNK]
    by_dl = [sum(blk(ic, ic + dl) for ic in range(Q_CHUNKS)) for dl in range(N_LEFT_CHUNKS + 1)]
    far = sum(jnp.sum(by_dl[dl], axis=(1, 2)) for dl in range(6))
    near = jnp.stack(by_dl[6:9], axis=1).reshape(H, 3 * CHUNK * CHUNK)
    g = jnp.dot(near, _rel_onehot(), precision=lax.Precision.HIGHEST)
    return g.at[:, N_REL - 1].add(far)


def _attn_specs(D, W):
    q_spec = pl.BlockSpec((BQ, W), lambda g, i: (i, g))

    def win(w, off):
        return pl.BlockSpec((BQ, W), lambda g, i: (jnp.maximum(i - (N_WIN - 1) + w, 0), off + g))

    k_specs = [win(w, 0) for w in range(N_WIN)]
    v_specs = [win(w, D // W) for w in range(N_WIN)]
    tab_spec = pl.BlockSpec((HEADS_PER_STEP, BQ, N_WIN * BQ), lambda g, i: (g, 0, 0))
    return q_spec, k_specs, v_specs, tab_spec


def _attn_probs(q_ref, kw, tab_ref, h, dh, i):
    qh = q_ref[:, h * dh:(h + 1) * dh]
    kh = kw[:, h * dh:(h + 1) * dh]
    s = lax.dot_general(qh, kh, _DIMS["nt"], preferred_element_type=F32) * (dh ** -0.5) + tab_ref[h]
    col = lax.broadcasted_iota(jnp.int32, s.shape, 1)
    s = jnp.where(col >= (N_WIN - 1 - i) * BQ, s, NEG)
    e = jnp.exp(s - jnp.max(s, axis=-1, keepdims=True))
    return e / jnp.sum(e, axis=-1, keepdims=True), qh, kh


def _attn_fwd(q, kv, tab, name):
    S, D = q.shape
    dh = D // N_HEADS
    W = HEADS_PER_STEP * dh
    q_spec, k_specs, v_specs, tab_spec = _attn_specs(D, W)

    def body(q_ref, *rest):
        k_refs, v_refs = rest[:N_WIN], rest[N_WIN:2 * N_WIN]
        tab_ref, o_ref = rest[2 * N_WIN], rest[2 * N_WIN + 1]
        i = pl.program_id(1)
        kw = jnp.concatenate([r[...] for r in k_refs], axis=0)
        vw = jnp.concatenate([r[...] for r in v_refs], axis=0)
        outs = []
        for h in range(HEADS_PER_STEP):
            p, _, _ = _attn_probs(q_ref, kw, tab_ref, h, dh, i)
            outs.append(jnp.dot(p.astype(BF16), vw[:, h * dh:(h + 1) * dh], preferred_element_type=F32))
        o_ref[...] = jnp.concatenate(outs, axis=1).astype(BF16)

    return pl.pallas_call(
        body, name=name, grid=(N_HEADS // HEADS_PER_STEP, S // BQ),
        in_specs=[q_spec] + k_specs + v_specs + [tab_spec],
        out_specs=q_spec,
        out_shape=jax.ShapeDtypeStruct((S, D), BF16),
        compiler_params=_params("parallel", "parallel"),
    )(q, *([kv] * (2 * N_WIN)), tab)


def _attn_bwd(q, kv, tab, do, name):
    S, D = q.shape
    dh = D // N_HEADS
    W = HEADS_PER_STEP * dh
    q_spec, k_specs, v_specs, tab_spec = _attn_specs(D, W)

    def body(q_ref, *rest):
        k_refs, v_refs = rest[:N_WIN], rest[N_WIN:2 * N_WIN]
        tab_ref, do_ref, dq_ref = rest[2 * N_WIN:2 * N_WIN + 3]
        dk_refs = rest[2 * N_WIN + 3:3 * N_WIN + 3]
        dv_refs = rest[3 * N_WIN + 3:4 * N_WIN + 3]
        dtab_ref = rest[4 * N_WIN + 3]
        i = pl.program_id(1)

        @pl.when(i == 0)
        def _():
            dtab_ref[...] = jnp.zeros_like(dtab_ref)

        kw = jnp.concatenate([r[...] for r in k_refs], axis=0)
        vw = jnp.concatenate([r[...] for r in v_refs], axis=0)
        dqs, dks, dvs = [], [], []
        for h in range(HEADS_PER_STEP):
            p, qh, kh = _attn_probs(q_ref, kw, tab_ref, h, dh, i)
            vh = vw[:, h * dh:(h + 1) * dh]
            doh = do_ref[:, h * dh:(h + 1) * dh]
            dp = lax.dot_general(doh, vh, _DIMS["nt"], preferred_element_type=F32)
            ds = p * (dp - jnp.sum(p * dp, axis=-1, keepdims=True))
            dtab_ref[h] += ds
            dsb = ds.astype(BF16)
            dqs.append(jnp.dot(dsb, kh, preferred_element_type=F32) * (dh ** -0.5))
            dks.append(lax.dot_general(dsb, qh, _DIMS["tn"], preferred_element_type=F32) * (dh ** -0.5))
            dvs.append(lax.dot_general(p.astype(BF16), doh, _DIMS["tn"], preferred_element_type=F32))
        dq_ref[...] = jnp.concatenate(dqs, axis=1).astype(BF16)
        dk = jnp.concatenate(dks, axis=1)
        dv = jnp.concatenate(dvs, axis=1)
        for w in range(N_WIN):
            dk_refs[w][...] = dk[w * BQ:(w + 1) * BQ, :]
            dv_refs[w][...] = dv[w * BQ:(w + 1) * BQ, :]

    part = jax.ShapeDtypeStruct((S, D), F32)
    outs = pl.pallas_call(
        body, name=name, grid=(N_HEADS // HEADS_PER_STEP, S // BQ),
        in_specs=[q_spec] + k_specs + v_specs + [tab_spec, q_spec],
        out_specs=[q_spec] + [q_spec] * (2 * N_WIN) + [tab_spec],
        out_shape=[jax.ShapeDtypeStruct((S, D), BF16)] + [part] * (2 * N_WIN)
        + [jax.ShapeDtypeStruct(tab.shape, F32)],
        compiler_params=_params("parallel", "arbitrary"),
    )(q, *([kv] * (2 * N_WIN)), tab, do)
    return outs[0], outs[1:1 + N_WIN], outs[1 + N_WIN:1 + 2 * N_WIN], outs[-1]


def _kv_grad_combine(dk_parts, dv_parts, name):
    S, D = dk_parts[0].shape
    nblk = S // BQ

    def body(*refs):
        o_ref = refs[2 * N_WIN]
        i = pl.program_id(0)
        for half, parts in enumerate((refs[:N_WIN], refs[N_WIN:2 * N_WIN])):
            acc = parts[N_WIN - 1][...]
            for w in range(N_WIN - 1):
                acc = acc + jnp.where(i + (N_WIN - 1 - w) < nblk, parts[w][...], 0.0)
            o_ref[:, half * D:(half + 1) * D] = acc.astype(BF16)

    specs = [pl.BlockSpec((BQ, D), functools.partial(
        lambda i, sh: (jnp.minimum(i + sh, nblk - 1), 0), sh=N_WIN - 1 - w)) for w in range(N_WIN)]
    return pl.pallas_call(
        body, name=name, grid=(nblk,),
        in_specs=specs + specs,
        out_specs=pl.BlockSpec((BQ, 2 * D), lambda i: (i, 0)),
        out_shape=jax.ShapeDtypeStruct((S, 2 * D), BF16),
        compiler_params=_params("parallel"),
    )(*dk_parts, *dv_parts)


def _adamw(w, g, m, v, name):
    shape = w.shape
    C = shape[-1]
    R = int(np.prod(shape[:-1])) if len(shape) > 1 else 1
    w2, g2, m2, v2 = (t.reshape(R, C) for t in (w, g, m, v))
    tr = _pick(R, max(8, (512 * 1024) // C // 8 * 8), 8)

    def body(w_ref, g_ref, m_ref, v_ref, d_ref, nm_ref, nv_ref):
        gv = g_ref[...]
        nm = ADAM_B1 * m_ref[...] + (1.0 - ADAM_B1) * gv
        nv = ADAM_B2 * v_ref[...] + (1.0 - ADAM_B2) * jnp.square(gv)
        m_hat = nm / (1.0 - ADAM_B1 ** ADAM_STEP)
        v_hat = nv / (1.0 - ADAM_B2 ** ADAM_STEP)
        d_ref[...] = -ADAM_LR * (m_hat / (jnp.sqrt(v_hat) + ADAM_EPS) + ADAM_WD * w_ref[...])
        nm_ref[...] = nm
        nv_ref[...] = nv

    spec = pl.BlockSpec((tr, C), lambda i: (i, 0))
    outs = pl.pallas_call(
        body, name=name, grid=(R // tr,),
        in_specs=[spec] * 4, out_specs=[spec] * 3,
        out_shape=[jax.ShapeDtypeStruct((R, C), F32)] * 3,
        compiler_params=_params("parallel"),
    )(w2, g2, m2, v2)
    return tuple(o.reshape(shape) for o in outs)


def _sum_rows(a, name):
    n, L = a.shape
    tl = _pick(L, 8192, 128)

    def body(a_ref, o_ref):
        acc = a_ref[0:1, :]
        for r in range(1, n):
            acc = acc + a_ref[r:r + 1, :]
        o_ref[...] = acc

    return pl.pallas_call(
        body, name=name, grid=(L // tl,),
        in_specs=[pl.BlockSpec((n, tl), lambda i: (0, i))],
        out_specs=pl.BlockSpec((1, tl), lambda i: (0, i)),
        out_shape=jax.ShapeDtypeStruct((1, L), F32),
        compiler_params=_params("parallel"),
    )(a)


def _scalar_call(body, name, scalar, grid, in_specs, out_spec, out_shape, args):
    return pl.pallas_call(
        body, name=name,
        grid_spec=pltpu.PrefetchScalarGridSpec(num_scalar_prefetch=1, grid=grid, in_specs=in_specs,
                                               out_specs=out_spec),
        out_shape=out_shape, compiler_params=_params("parallel"),
    )(jnp.reshape(scalar, (1,)).astype(jnp.int32), *args)


def _pair_sum(view, got, c, name):
    nb, _, rh, cols = view.shape
    tr = _pick(rh, max(16, (1 << 20) // cols // 16 * 16), 16)
    bpr = rh // tr

    def body(s_ref, a_ref, b_ref, o_ref):
        o_ref[...] = (a_ref[...].astype(F32) + b_ref[...].astype(F32)).astype(BF16)

    spec = pl.BlockSpec((tr, cols), lambda i, s: (i, 0))
    mine = pl.BlockSpec((tr, cols), lambda i, s: ((2 * (i // bpr) + s[0]) * bpr + i % bpr, 0))
    return _scalar_call(body, name, c, (nb * bpr,), [mine, spec], spec,
                        jax.ShapeDtypeStruct((nb * rh, cols), BF16),
                        (view.reshape(nb * 2 * rh, cols), got.reshape(nb * rh, cols)))


def _owner_sum(pair, recv, me, it, name):
    _, rh, bc = recv.shape
    tr = _pick(rh, max(16, (1 << 19) // bc // 16 * 16), 16)
    bpr = rh // tr

    def body(s_ref, a_ref, r0, r1, r2, o_ref):
        o_ref[...] = ((a_ref[...].astype(F32) + r0[...].astype(F32)) + r1[...].astype(F32)) + r2[...].astype(F32)

    if it.kind == "col":
        own = pl.BlockSpec((tr, bc), lambda i, s: (i, s[0]))
    else:
        own = pl.BlockSpec((tr, bc), lambda i, s: (s[0] * bpr + i, 0))
    slots = [pl.BlockSpec((None, tr, bc), functools.partial(lambda i, s, k: (k, i, 0), k=k)) for k in range(3)]
    return _scalar_call(body, name, me, (bpr,), [own] + slots, pl.BlockSpec((tr, bc), lambda i, s: (i, 0)),
                        jax.ShapeDtypeStruct((rh, bc), F32), (pair, recv, recv, recv))


def _place():
    x, y, c = lax.axis_index("x"), lax.axis_index("y"), lax.axis_index("c")
    chips = [(1 - x, y), (x, 1 - y), (1 - x, 1 - y)]
    return x, y, c, chips


def _chip_index(px, py):
    return 2 * px + py


def _all_gather_small(x_shard, name):
    m_per, n = x_shard.shape

    def body(x_ref, out_ref, send_sems, recv_sems, local_sem):
        x, y, c, chips = _place()
        me, sibling = (x, y, c), (x, y, 1 - c)

        def rows(px, py, pc):
            return out_ref.at[pl.ds((4 * px + 2 * py + pc) * m_per, m_per), :]

        def copy(k, block, to, src=None):
            return pltpu.make_async_remote_copy(
                src_ref=rows(*block) if src is None else src, dst_ref=rows(*block),
                send_sem=send_sems.at[k], recv_sem=recv_sems.at[k], device_id=to, device_id_type=MESH)

        mine = pltpu.make_async_copy(x_ref, rows(*me), local_sem)
        mine.start()
        first = [copy(0, me, sibling, src=x_ref)]
        first += [copy(1 + j, me, (*chip, c), src=x_ref) for j, chip in enumerate(chips)]
        for cp in first:
            cp.start()
        passed = [copy(4 + j, (*chip, c), sibling) for j, chip in enumerate(chips)]
        for j, chip in enumerate(chips):
            copy(1 + j, (*chip, c), me).wait_recv()
            passed[j].start()
        copy(0, sibling, me).wait_recv()
        for j, chip in enumerate(chips):
            copy(4 + j, (*chip, 1 - c), me).wait_recv()
        for cp in first + passed:
            cp.wait_send()
        mine.wait()

    return pl.pallas_call(
        body, name=name,
        out_shape=jax.ShapeDtypeStruct((N_DEV * m_per, n), x_shard.dtype),
        in_specs=[pl.BlockSpec(memory_space=pltpu.VMEM)],
        out_specs=pl.BlockSpec(memory_space=pltpu.VMEM),
        scratch_shapes=[pltpu.SemaphoreType.DMA((7,)), pltpu.SemaphoreType.DMA((7,)), pltpu.SemaphoreType.DMA],
    )(x_shard)


def _gather_flat(vec, name):
    L = vec.shape[0]
    Lp = -(-L // 1024) * 1024
    g = _all_gather_small(jnp.pad(vec, (0, Lp - L)).reshape(8, Lp // 8), name)
    return g.reshape(N_DEV, Lp)[:, :L]


class _Item:
    def __init__(self, kind, rows, cols, arg, layer=None):
        self.kind, self.rows, self.cols, self.arg, self.layer = kind, rows, cols, arg, layer

    def ref(self, refs):
        r = refs[self.arg]
        return r if self.layer is None else r.at[self.layer]


def _block(ref, it, j, half):
    if it.kind == "col":
        ns = it.cols // N_CHIP
        return ref.at[pl.ds(half * (it.rows // 2), it.rows // 2), pl.ds(j * ns, ns)]
    rs = it.rows // N_CHIP
    return ref.at[pl.ds(j * rs + half * (rs // 2), rs // 2), :]


def _shard_half(ref, it, half):
    r = it.rows if it.kind == "col" else it.rows // N_CHIP
    return ref.at[pl.ds(half * (r // 2), r // 2), :]


def _gather_weights(shards, items, out_shapes, name):
    n_arg, n_it = len(shards), len(items)

    def body(*refs):
        ins, outs = refs[:n_arg], refs[n_arg:2 * n_arg]
        send, recv, fsend, frecv, lsem = refs[2 * n_arg:]
        x, y, c, chips = _place()
        me = _chip_index(x, y)
        local = [pltpu.make_async_copy(it.ref(ins), _block_full(it.ref(outs), it, me), lsem.at[t])
                 for t, it in enumerate(items)]
        for cp in local:
            cp.start()
        sends, fwds = [], []
        for t, it in enumerate(items):
            for k, chip in enumerate(chips):
                cp = pltpu.make_async_remote_copy(
                    src_ref=_shard_half(it.ref(ins), it, c), dst_ref=_block(it.ref(outs), it, me, c),
                    send_sem=send.at[t, k], recv_sem=recv.at[t, k], device_id=(*chip, c), device_id_type=MESH)
                cp.start()
                sends.append(cp)
        for t, it in enumerate(items):
            for k, chip in enumerate(chips):
                landed = _block(it.ref(outs), it, _chip_index(*chip), c)
                pltpu.make_async_remote_copy(
                    src_ref=landed, dst_ref=landed, send_sem=send.at[t, k], recv_sem=recv.at[t, k],
                    device_id=(*chip, c), device_id_type=MESH).wait_recv()
                cp = pltpu.make_async_remote_copy(
                    src_ref=landed, dst_ref=landed, send_sem=fsend.at[t, k], recv_sem=frecv.at[t, k],
                    device_id=(x, y, 1 - c), device_id_type=MESH)
                cp.start()
                fwds.append(cp)
        for t, it in enumerate(items):
            for k, chip in enumerate(chips):
                other = _block(it.ref(outs), it, _chip_index(*chip), 1 - c)
                pltpu.make_async_remote_copy(
                    src_ref=other, dst_ref=other, send_sem=fsend.at[t, k], recv_sem=frecv.at[t, k],
                    device_id=(x, y, 1 - c), device_id_type=MESH).wait_recv()
        for cp in sends + fwds:
            cp.wait_send()
        for cp in local:
            cp.wait()

    any_spec = pl.BlockSpec(memory_space=pl.ANY)
    return pl.pallas_call(
        body, name=name,
        out_shape=[jax.ShapeDtypeStruct(s, BF16) for s in out_shapes],
        in_specs=[any_spec] * n_arg, out_specs=[any_spec] * n_arg,
        scratch_shapes=[pltpu.SemaphoreType.DMA((n_it, 3))] * 4 + [pltpu.SemaphoreType.DMA((n_it,))],
    )(*shards)


def _block_full(ref, it, j):
    if it.kind == "col":
        ns = it.cols // N_CHIP
        return ref.at[:, pl.ds(j * ns, ns)]
    rs = it.rows // N_CHIP
    return ref.at[pl.ds(j * rs, rs), :]


def _pair_view(g, it):
    if it.kind == "col":
        return g.reshape(1, 2, it.rows // 2, it.cols)
    return g.reshape(N_CHIP, 2, it.rows // (2 * N_CHIP), it.cols)


def _pair_exchange(views, name):
    n = len(views)

    def body(*refs):
        ins, outs, send, recv = refs[:n], refs[n:2 * n], refs[2 * n], refs[2 * n + 1]
        x, y, c, _ = _place()
        cps = []
        for t in range(n):
            cp = pltpu.make_async_remote_copy(
                src_ref=ins[t].at[:, pl.ds(1 - c, 1)], dst_ref=outs[t],
                send_sem=send.at[t], recv_sem=recv.at[t], device_id=(x, y, 1 - c), device_id_type=MESH)
            cp.start()
            cps.append(cp)
        for cp in cps:
            cp.wait()

    any_spec = pl.BlockSpec(memory_space=pl.ANY)
    return pl.pallas_call(
        body, name=name,
        out_shape=[jax.ShapeDtypeStruct((v.shape[0], 1) + v.shape[2:], v.dtype) for v in views],
        in_specs=[any_spec] * n, out_specs=[any_spec] * n,
        scratch_shapes=[pltpu.SemaphoreType.DMA((n,)), pltpu.SemaphoreType.DMA((n,))],
    )(*views)


def _owner_exchange(pairs, items, name):
    n = len(items)

    def blk(ref, it, j):
        if it.kind == "col":
            ns = it.cols // N_CHIP
            return ref.at[:, pl.ds(j * ns, ns)]
        return ref.at[j]

    def body(*refs):
        ins, outs, send, recv = refs[:n], refs[n:2 * n], refs[2 * n], refs[2 * n + 1]
        x, y, c, chips = _place()
        cps = []
        for t, it in enumerate(items):
            for k, chip in enumerate(chips):
                cp = pltpu.make_async_remote_copy(
                    src_ref=blk(ins[t], it, _chip_index(*chip)), dst_ref=outs[t].at[k],
                    send_sem=send.at[t, k], recv_sem=recv.at[t, k], device_id=(*chip, c), device_id_type=MESH)
                cp.start()
                cps.append(cp)
        for cp in cps:
            cp.wait()

    def slot_shape(it):
        if it.kind == "col":
            return (3, it.rows // 2, it.cols // N_CHIP)
        return (3, it.rows // (2 * N_CHIP), it.cols)

    any_spec = pl.BlockSpec(memory_space=pl.ANY)
    return pl.pallas_call(
        body, name=name,
        out_shape=[jax.ShapeDtypeStruct(slot_shape(it), BF16) for it in items],
        in_specs=[any_spec] * n, out_specs=[any_spec] * n,
        scratch_shapes=[pltpu.SemaphoreType.DMA((n, 3)), pltpu.SemaphoreType.DMA((n, 3))],
    )(*pairs)


def _half_exchange(halves, name):
    n = len(halves)

    def body(*refs):
        ins, outs, send, recv, lsem = refs[:n], refs[n:2 * n], refs[2 * n], refs[2 * n + 1], refs[2 * n + 2]
        x, y, c, _ = _place()
        cps, loc = [], []
        for t in range(n):
            r2 = ins[t].shape[0]
            mine = outs[t].at[pl.ds(c * r2, r2), :]
            lc = pltpu.make_async_copy(ins[t], mine, lsem.at[t])
            lc.start()
            loc.append(lc)
            cp = pltpu.make_async_remote_copy(
                src_ref=ins[t], dst_ref=mine, send_sem=send.at[t], recv_sem=recv.at[t],
                device_id=(x, y, 1 - c), device_id_type=MESH)
            cp.start()
            cps.append(cp)
        for t in range(n):
            r2 = ins[t].shape[0]
            theirs = outs[t].at[pl.ds((1 - c) * r2, r2), :]
            pltpu.make_async_remote_copy(
                src_ref=theirs, dst_ref=theirs, send_sem=send.at[t], recv_sem=recv.at[t],
                device_id=(x, y, 1 - c), device_id_type=MESH).wait_recv()
        for cp in cps:
            cp.wait_send()
        for lc in loc:
            lc.wait()

    any_spec = pl.BlockSpec(memory_space=pl.ANY)
    return pl.pallas_call(
        body, name=name,
        out_shape=[jax.ShapeDtypeStruct((2 * h.shape[0], h.shape[1]), h.dtype) for h in halves],
        in_specs=[any_spec] * n, out_specs=[any_spec] * n,
        scratch_shapes=[pltpu.SemaphoreType.DMA((n,)), pltpu.SemaphoreType.DMA((n,)), pltpu.SemaphoreType.DMA((n,))],
    )(*halves)


def _reduce_scatter(grads, items):
    x, y, c, _ = _place()
    me = _chip_index(x, y)
    views = [_pair_view(g, it) for g, it in zip(grads, items)]
    got = _pair_exchange(views, "rs_pair_exchange")
    pairs = [_pair_sum(v, r, c, f"rs_pair_sum_{t}") for t, (v, r) in enumerate(zip(views, got))]
    shaped = [p if it.kind == "col" else p.reshape(N_CHIP, p.shape[0] // N_CHIP, p.shape[1])
              for p, it in zip(pairs, items)]
    recv = _owner_exchange(shaped, items, "rs_owner_exchange")
    halves = [_owner_sum(p, r, me, it, f"rs_owner_sum_{t}") for t, (p, r, it) in enumerate(zip(pairs, recv, items))]
    return _half_exchange(halves, "rs_half_exchange")


def _silu(v):
    return v * jax.nn.sigmoid(v)


def _sum8(p):
    return jnp.sum(p, axis=-2)


def kernel(x, c, mod_w, mod_b, norm_g, ffn_w_in, ffn_w_out, conv_w_in, conv_k, conv_w_out, kv_mod_w, kv_mod_b, kv_norm_g, w_kv, attn_w_q, attn_w_o, rel_bias, loss_target, m_mod_w, m_mod_b, m_norm_g, m_ffn_w_in, m_ffn_w_out, m_conv_w_in, m_conv_k, m_conv_w_out, m_kv_mod_w, m_kv_mod_b, m_kv_norm_g, m_w_kv, m_attn_w_q, m_attn_w_o, m_rel_bias, v_mod_w, v_mod_b, v_norm_g, v_ffn_w_in, v_ffn_w_out, v_conv_w_in, v_conv_k, v_conv_w_out, v_kv_mod_w, v_kv_mod_b, v_kv_norm_g, v_w_kv, v_attn_w_q, v_attn_w_o, v_rel_bias):
    xi, yi, ci = lax.axis_index("x"), lax.axis_index("y"), lax.axis_index("c")
    chip = 2 * xi + yi
    dev = 2 * chip + ci
    _, S, D = x.shape
    F = ffn_w_out.shape[1] * N_CHIP
    x0 = x.reshape(S, D)
    target = loss_target.reshape(S, D)
    n_mod = mod_w.shape[2]
    n_kvm = kv_mod_w.shape[1]
    dsh = D // N_CHIP
    TF = F // 2

    c_all = _all_gather_small(c.reshape(8, D // 8), "ag_c").reshape(N_DEV, D)
    sc16 = jnp.pad(_silu(c_all), ((0, 8), (0, 0)))
    part = [_mm(sc16, mod_w, "nn", F32, f"mod_fwd_{l}", b_layer=l)[:8] for l in range(2)]
    part.append(_mm(sc16, kv_mod_w, "nn", F32, "mod_fwd_kv")[:8])
    fwd_vec = jnp.concatenate([p.reshape(-1) for p in part] + [norm_g.reshape(-1), conv_k.reshape(-1)])
    fwd_all = _gather_flat(fwd_vec, "ag_fwd_small")[0::2]
    o = 0
    mods = []
    for n in (n_mod, n_mod, n_kvm):
        blk = fwd_all[:, o:o + 8 * n].reshape(N_CHIP, 8, n)
        mods.append(lax.dynamic_index_in_dim(blk, dev, axis=1, keepdims=False).reshape(N_CHIP * n))
        o += 8 * n
    ng = fwd_all[:, o:o + 8 * dsh].reshape(N_CHIP, 2, 4, dsh).transpose(1, 2, 0, 3).reshape(2, 4, D)
    o += 8 * dsh
    ck = fwd_all[:, o:o + 3 * dsh].reshape(N_CHIP, 3, dsh).transpose(1, 0, 2).reshape(3, D)
    ck8 = jnp.pad(ck, ((0, 5), (0, 0)))
    mod = [mods[l] + mod_b[l] for l in range(2)]
    sh1, sc1, g1, sh2, sc2, g2 = zip(*[jnp.split(m, 6) for m in mod])
    kv_sh, kv_sc = jnp.split(mods[2] + kv_mod_b, 2)
    row = lambda v: v.reshape(1, D)

    items = [_Item("col", D, 3 * D, 0, 0), _Item("row", D, D, 1, 0),
             _Item("col", D, 2 * F, 2, 0), _Item("row", F, D, 3, 0),
             _Item("col", D, 2 * D, 4), _Item("row", D, D, 5, 0), _Item("row", D, D, 6, 0),
             _Item("col", D, 2 * F, 2, 1), _Item("row", F, D, 3, 1)]
    shards = [w.astype(BF16) for w in (conv_w_in, conv_w_out, ffn_w_in, ffn_w_out, w_kv, attn_w_q, attn_w_o)]
    full_shapes = [(1, D, 3 * D), (1, D, D), (2, D, 2 * F), (2, F, D), (D, 2 * D), (1, D, D), (1, D, D)]
    W_cin, W_cout, W_fin, W_fout, W_kv, W_q, W_o = _gather_weights(shards, items, full_shapes, "ag_weights")

    a1 = row(ng[0, 0] * (1.0 + sc1[0]))
    (h1,) = _norm_mod(x0, a1, row(sh1[0]), "l0_norm1")
    bcx = _mm(h1, W_cin, "nn", F32, "l0_conv_in", b_layer=0)
    ug = _conv_gate(bcx, ck8, "l0_conv_gate")
    y1 = _mm(ug, W_cout, "nn", F32, "l0_conv_out", b_layer=0)
    gt1 = row(g1[0] * ng[0, 1])
    x1 = _post_norm(x0, y1, gt1, "l0_post1")
    a2 = row(ng[0, 2] * (1.0 + sc2[0]))
    (h2,) = _norm_mod(x1, a2, row(sh2[0]), "l0_norm2")
    gu0 = _mm(h2, W_fin, "nn", F32, "l0_ffn_in", b_layer=0)
    act0 = _swiglu(gu0, "l0_swiglu")
    y2 = _mm(act0, W_fout, "nn", F32, "l0_ffn_out", b_layer=0, tk=TF)
    gt2 = row(g2[0] * ng[0, 3])
    x2 = _post_norm(x1, y2, gt2, "l0_post2")
    a3 = ng[1, 0] * (1.0 + sc1[1])
    akv = kv_norm_g * (1.0 + kv_sc)
    h3, hkv = _norm_mod(x2, jnp.stack([a3, akv]), jnp.stack([sh1[1], kv_sh]), "l1_norm1")
    kvp = _mm(hkv, W_kv, "nn", BF16, "l1_kv")
    qp = _mm(h3, W_q, "nn", BF16, "l1_q", b_layer=0)
    tab = _bias_table(rel_bias[0])
    oh = _attn_fwd(qp, kvp, tab, "l1_attn")
    y3 = _mm(oh, W_o, "nn", F32, "l1_attn_out", b_layer=0)
    gt3 = row(g1[1] * ng[1, 1])
    x3 = _post_norm(x2, y3, gt3, "l1_post1")
    a4 = row(ng[1, 2] * (1.0 + sc2[1]))
    (h4,) = _norm_mod(x3, a4, row(sh2[1]), "l1_norm2")
    gu1 = _mm(h4, W_fin, "nn", F32, "l1_ffn_in", b_layer=1)
    act1 = _swiglu(gu1, "l1_swiglu")
    y4 = _mm(act1, W_fout, "nn", F32, "l1_ffn_out", b_layer=1, tk=TF)
    gt4 = row(g2[1] * ng[1, 3])
    x4 = _post_norm(x3, y4, gt4, "l1_post2")
    dx4, sq = _loss_grad(x4, target, "loss")
    loss_part = 0.5 * jnp.sum(sq) / D

    def ffn_bwd(dxn, xin_, h, gu, act, y, gt, a, l, tag):
        dy, dgt = _post_norm_bwd(dxn, y, gt, f"{tag}_post2_bwd")
        da = _mm(dy, W_fout, "nt", F32, f"{tag}_ffn_out_dx", b_layer=l, tn=TF)
        g_fout = _mm(act, dy, "tn", BF16, f"{tag}_ffn_out_dw", tm=TF)
        dgu = _swiglu_bwd(da, gu, f"{tag}_swiglu_bwd")
        dh = _mm(dgu, W_fin, "nt", F32, f"{tag}_ffn_in_dx", b_layer=l)
        g_fin = _mm(h, dgu, "tn", BF16, f"{tag}_ffn_in_dw")
        dx, ds, db = _pre_norm_bwd(xin_, dxn, [dh], a, f"{tag}_norm2_bwd")
        return dx, _sum8(dgt), _sum8(ds)[0], _sum8(db)[0], g_fin, g_fout

    dx3, dgt4, da4, db4, G_fin1, G_fout1 = ffn_bwd(dx4, x3, h4, gu1, act1, y4, gt4, a4, 1, "l1")
    dy3, dgt3 = _post_norm_bwd(dx3, y3, gt3, "l1_post1_bwd")
    doh = _mm(dy3, W_o, "nt", BF16, "l1_attn_out_dx", b_layer=0)
    G_o = _mm(oh, dy3, "tn", BF16, "l1_attn_out_dw")
    dq, dk_parts, dv_parts, dtab = _attn_bwd(qp, kvp, tab, doh, "l1_attn_bwd")
    d_rel = _bias_table_grad(dtab)
    dkv = _kv_grad_combine(dk_parts, dv_parts, "l1_kv_grad")
    dh3 = _mm(dq, W_q, "nt", F32, "l1_q_dx", b_layer=0)
    G_q = _mm(h3, dq, "tn", BF16, "l1_q_dw")
    dhkv = _mm(dkv, W_kv, "nt", F32, "l1_kv_dx")
    G_kv = _mm(hkv, dkv, "tn", BF16, "l1_kv_dw")
    dx2, ds3, db3 = _pre_norm_bwd(x2, dx3, [dh3, dhkv], jnp.stack([a3, akv]), "l1_norm1_bwd")
    ds3, db3 = _sum8(ds3), _sum8(db3)

    dx1, dgt2, da2, db2, G_fin0, G_fout0 = ffn_bwd(dx2, x1, h2, gu0, act0, y2, gt2, a2, 0, "l0")
    dy1, dgt1 = _post_norm_bwd(dx1, y1, gt1, "l0_post1_bwd")
    dug = _mm(dy1, W_cout, "nt", F32, "l0_conv_out_dx", b_layer=0)
    G_cout = _mm(ug, dy1, "tn", BF16, "l0_conv_out_dw")
    dbcx, dck = _conv_gate_bwd(dug, bcx, ck8, "l0_conv_gate_bwd")
    dh1 = _mm(dbcx, W_cin, "nt", F32, "l0_conv_in_dx", b_layer=0)
    G_cin = _mm(h1, dbcx, "tn", BF16, "l0_conv_in_dw")
    dx0, ds1, db1 = _pre_norm_bwd(x0, dx1, [dh1], a1, "l0_norm1_bwd")
    ds1, db1 = _sum8(ds1)[0], _sum8(db1)[0]
    dgt1, dgt3 = _sum8(dgt1), _sum8(dgt3)

    def dmod_of(l, ds_a, db_a, dgt_a, ds_b, db_b, dgt_b):
        return jnp.concatenate([db_a, ds_a * ng[l, 0], dgt_a * ng[l, 1], db_b, ds_b * ng[l, 2], dgt_b * ng[l, 3]])

    dmod0 = dmod_of(0, ds1, db1, dgt1, da2, db2, dgt2)
    dmod1 = dmod_of(1, ds3[0], db3[0], dgt3, da4, db4, dgt4)
    dkvmod = jnp.concatenate([db3[1], ds3[1] * kv_norm_g])
    dng = jnp.stack([
        jnp.stack([ds1 * (1.0 + sc1[0]), dgt1 * g1[0], da2 * (1.0 + sc2[0]), dgt2 * g2[0]]),
        jnp.stack([ds3[0] * (1.0 + sc1[1]), dgt3 * g1[1], da4 * (1.0 + sc2[1]), dgt4 * g2[1]])])
    dkvng = ds3[1] * (1.0 + kv_sc)
    small = [dmod0, dmod1, dkvmod, dng.reshape(-1), dkvng, _sum8(dck).reshape(-1), d_rel.reshape(-1),
             loss_part.reshape(1)]
    sizes = [int(s.shape[0]) for s in small]
    offs = np.concatenate([[0], np.cumsum(sizes)])
    bwd_all = _gather_flat(jnp.concatenate(small), "ag_bwd_small")
    Lb = bwd_all.shape[1]
    Lp = -(-Lb // 128) * 128
    tot = _sum_rows(jnp.pad(bwd_all, ((0, 0), (0, Lp - Lb))), "sum_small")[0]
    seg = lambda i: tot[offs[i]:offs[i + 1]]
    g_mod_b = jnp.stack([seg(0), seg(1)])
    g_kv_mod_b = seg(2)
    g_norm_g = lax.dynamic_slice_in_dim(seg(3).reshape(2, 4, D), chip * dsh, dsh, axis=2)
    g_kv_norm_g = seg(4)
    g_conv_k = lax.dynamic_slice_in_dim(seg(5).reshape(1, 3, D), chip * dsh, dsh, axis=2)
    g_rel_bias = seg(6).reshape(rel_bias.shape)
    loss = seg(7)[0]

    def dmod_w(i, n, name):
        rows_ = lax.dynamic_slice_in_dim(bwd_all[:, offs[i]:offs[i + 1]], chip * n, n, axis=1)
        return _mm(sc16, jnp.pad(rows_, ((0, 8), (0, 0))), "tn", F32, name)

    g_mod_w = jnp.stack([dmod_w(0, n_mod, "mod_bwd_0"), dmod_w(1, n_mod, "mod_bwd_1")])
    g_kv_mod_w = dmod_w(2, n_kvm, "mod_bwd_kv")

    big = [G_cin, G_cout, G_fin0, G_fout0, G_kv, G_q, G_o, G_fin1, G_fout1]
    r_cin, r_cout, r_fin0, r_fout0, r_kv, r_q, r_o, r_fin1, r_fout1 = _reduce_scatter(big, items)
    grads = {
        "mod_w": g_mod_w, "mod_b": g_mod_b, "norm_g": g_norm_g,
        "ffn_w_in": jnp.stack([r_fin0, r_fin1]), "ffn_w_out": jnp.stack([r_fout0, r_fout1]),
        "conv_w_in": r_cin[None], "conv_k": g_conv_k, "conv_w_out": r_cout[None],
        "kv_mod_w": g_kv_mod_w, "kv_mod_b": g_kv_mod_b, "kv_norm_g": g_kv_norm_g, "w_kv": r_kv,
        "attn_w_q": r_q[None], "attn_w_o": r_o[None], "rel_bias": g_rel_bias,
    }
    weights = dict(mod_w=mod_w, mod_b=mod_b, norm_g=norm_g, ffn_w_in=ffn_w_in, ffn_w_out=ffn_w_out,
                   conv_w_in=conv_w_in, conv_k=conv_k, conv_w_out=conv_w_out, kv_mod_w=kv_mod_w,
                   kv_mod_b=kv_mod_b, kv_norm_g=kv_norm_g, w_kv=w_kv, attn_w_q=attn_w_q, attn_w_o=attn_w_o,
                   rel_bias=rel_bias)
    m_in = dict(mod_w=m_mod_w, mod_b=m_mod_b, norm_g=m_norm_g, ffn_w_in=m_ffn_w_in, ffn_w_out=m_ffn_w_out,
                conv_w_in=m_conv_w_in, conv_k=m_conv_k, conv_w_out=m_conv_w_out, kv_mod_w=m_kv_mod_w,
                kv_mod_b=m_kv_mod_b, kv_norm_g=m_kv_norm_g, w_kv=m_w_kv, attn_w_q=m_attn_w_q,
                attn_w_o=m_attn_w_o, rel_bias=m_rel_bias)
    v_in = dict(mod_w=v_mod_w, mod_b=v_mod_b, norm_g=v_norm_g, ffn_w_in=v_ffn_w_in, ffn_w_out=v_ffn_w_out,
                conv_w_in=v_conv_w_in, conv_k=v_conv_k, conv_w_out=v_conv_w_out, kv_mod_w=v_kv_mod_w,
                kv_mod_b=v_kv_mod_b, kv_norm_g=v_kv_norm_g, w_kv=v_w_kv, attn_w_q=v_attn_w_q,
                attn_w_o=v_attn_w_o, rel_bias=v_rel_bias)
    names = list(weights)
    g_out, d_out, m_out, v_out = [], [], [], []
    for n in names:
        g = grads[n].reshape(weights[n].shape)
        d, nm, nv = _adamw(weights[n], g, m_in[n], v_in[n], f"adamw_{n}")
        g_out.append(g)
        d_out.append(d)
        m_out.append(nm)
        v_out.append(nv)
    return (loss, dx0.reshape(x.shape), *g_out, *d_out, *m_out, *v_out)
```

```python
import functools
import math

import numpy as np
import jax
import jax.numpy as jnp
from jax import lax
from jax.experimental import pallas as pl
from jax.experimental.pallas import tpu as pltpu

CHUNK = 64
N_LEFT_CHUNKS = 8
N_HEADS = 16
MAX_REL = 2 * CHUNK
N_REL = 2 * MAX_REL + 1
EPS = 1e-6
ADAM_LR = 0.001
ADAM_B1 = 0.9
ADAM_B2 = 0.999
ADAM_EPS = 1e-08
ADAM_WD = 0.01
ADAM_STEP = 10

Q_CHUNKS = 4
BQ = Q_CHUNKS * CHUNK
N_WIN = 1 + N_LEFT_CHUNKS // Q_CHUNKS
HEADS_PER_STEP = 2
NEG = -1e30
N_DEV = 8
N_CHIP = 4

BF16 = jnp.bfloat16
F32 = jnp.float32
V7X_VMEM_LIMIT_BYTES = 56 * 1024 * 1024
MESH = pl.DeviceIdType.MESH


def _pick(n, pref, align):
    t = min(pref, n)
    t -= t % align
    while t >= align:
        if n % t == 0:
            return t
        t -= align
    return n


def _params(*sem):
    return pltpu.CompilerParams(dimension_semantics=sem, vmem_limit_bytes=V7X_VMEM_LIMIT_BYTES)


def _colsum8(v):
    r, d = v.shape
    return v.reshape(r // 8, 8, d).sum(axis=0)


_DIMS = {"nn": (((1,), (0,)), ((), ())), "nt": (((1,), (1,)), ((), ())), "tn": (((0,), (0,)), ((), ()))}


def _mm(a, b, mode, out_dtype, name, *, b_layer=None, tm=1024, tn=1024, tk=None, scale=None):
    if tk is None:
        tk = 2048 if mode == "tn" else 3072
    bs = b.shape[1:] if b_layer is not None else b.shape
    if mode == "nn":
        (M, K), (K2, N) = a.shape, bs
    elif mode == "nt":
        (M, K), (N, K2) = a.shape, bs
    else:
        (K, M), (K2, N) = a.shape, bs
    assert K == K2, (name, a.shape, b.shape)
    tm = _pick(M, tm, 128 if mode == "tn" else 16)
    tn = _pick(N, tn, 128)
    tk = _pick(K, tk, 128 if mode != "tn" else 16)
    nk = K // tk
    assert scale is None or nk == 1, name
    dims = _DIMS[mode]

    def body(a_ref, b_ref, o_ref, *acc):
        p = lax.dot_general(a_ref[...].astype(BF16), b_ref[...].astype(BF16), dims,
                            preferred_element_type=F32)
        if nk == 1:
            o_ref[...] = (p if scale is None else p * scale).astype(o_ref.dtype)
        else:
            k = pl.program_id(2)

            @pl.when(k == 0)
            def _():
                acc[0][...] = p

            @pl.when(k > 0)
            def _():
                acc[0][...] += p

            @pl.when(k == nk - 1)
            def _():
                o_ref[...] = acc[0][...].astype(o_ref.dtype)

    a_spec = (pl.BlockSpec((tk, tm), lambda i, j, k: (k, i)) if mode == "tn"
              else pl.BlockSpec((tm, tk), lambda i, j, k: (i, k)))
    if mode == "nt":
        b_blk, b_idx = (tn, tk), (lambda i, j, k: (j, k))
    else:
        b_blk, b_idx = (tk, tn), (lambda i, j, k: (k, j))
    if b_layer is not None:
        b_spec = pl.BlockSpec((None,) + b_blk, lambda i, j, k: (b_layer,) + b_idx(i, j, k))
    else:
        b_spec = pl.BlockSpec(b_blk, b_idx)
    return pl.pallas_call(
        body, name=name,
        grid=(M // tm, N // tn, nk),
        in_specs=[a_spec, b_spec],
        out_specs=pl.BlockSpec((tm, tn), lambda i, j, k: (i, j)),
        out_shape=jax.ShapeDtypeStruct((M, N), out_dtype),
        scratch_shapes=[pltpu.VMEM((tm, tn), F32)] if nk > 1 else [],
        compiler_params=_params("parallel", "parallel", "arbitrary"),
    )(a, b)


def _row_spec(tm, d):
    return pl.BlockSpec((tm, d), lambda i: (i, 0))


def _vec_spec(r, d):
    return pl.BlockSpec((r, d), lambda i: (0, 0))


def _norm_mod(x, scales, shifts, name):
    S, D = x.shape
    nb = scales.shape[0]
    tm = _pick(S, 512, 16)

    def body(x_ref, a_ref, b_ref, *o_refs):
        xv = x_ref[...]
        xh = xv * lax.rsqrt(jnp.mean(xv * xv, axis=-1, keepdims=True) + EPS)
        for n in range(nb):
            o_refs[n][...] = (xh * a_ref[n:n + 1, :] + b_ref[n:n + 1, :]).astype(BF16)

    return pl.pallas_call(
        body, name=name, grid=(S // tm,),
        in_specs=[_row_spec(tm, D), _vec_spec(nb, D), _vec_spec(nb, D)],
        out_specs=[_row_spec(tm, D)] * nb,
        out_shape=[jax.ShapeDtypeStruct((S, D), BF16)] * nb,
        compiler_params=_params("parallel"),
    )(x, scales, shifts)


def _post_norm(x, y, gate, name):
    S, D = x.shape
    tm = _pick(S, 512, 8)

    def body(x_ref, y_ref, g_ref, o_ref):
        yv = y_ref[...]
        yh = yv * lax.rsqrt(jnp.mean(yv * yv, axis=-1, keepdims=True) + EPS)
        o_ref[...] = x_ref[...] + yh * g_ref[...]

    return pl.pallas_call(
        body, name=name, grid=(S // tm,),
        in_specs=[_row_spec(tm, D), _row_spec(tm, D), _vec_spec(1, D)],
        out_specs=_row_spec(tm, D),
        out_shape=jax.ShapeDtypeStruct((S, D), F32),
        compiler_params=_params("parallel"),
    )(x, y, gate)


def _loss_grad(x, target, name):
    S, D = x.shape
    tm = _pick(S, 512, 8)

    def body(x_ref, t_ref, dx_ref, sq_ref):
        e = x_ref[...] - t_ref[...]
        dx_ref[...] = e / D

        @pl.when(pl.program_id(0) == 0)
        def _():
            sq_ref[...] = jnp.zeros_like(sq_ref)

        sq_ref[...] += _colsum8(e * e)

    return pl.pallas_call(
        body, name=name, grid=(S // tm,),
        in_specs=[_row_spec(tm, D), _row_spec(tm, D)],
        out_specs=[_row_spec(tm, D), _vec_spec(8, D)],
        out_shape=[jax.ShapeDtypeStruct((S, D), F32), jax.ShapeDtypeStruct((8, D), F32)],
        compiler_params=_params("arbitrary"),
    )(x, target)


def _post_norm_bwd(dxn, y, gate, name):
    S, D = y.shape
    tm = _pick(S, 512, 16)

    def body(d_ref, y_ref, g_ref, dy_ref, dg_ref):
        yv = y_ref[...]
        dv = d_ref[...]
        r = lax.rsqrt(jnp.mean(yv * yv, axis=-1, keepdims=True) + EPS)
        yh = yv * r
        dyh = dv * g_ref[...]
        dy_ref[...] = (r * (dyh - yh * jnp.mean(dyh * yh, axis=-1, keepdims=True))).astype(BF16)

        @pl.when(pl.program_id(0) == 0)
        def _():
            dg_ref[...] = jnp.zeros_like(dg_ref)

        dg_ref[...] += _colsum8(dv * yh)

    return pl.pallas_call(
        body, name=name, grid=(S // tm,),
        in_specs=[_row_spec(tm, D), _row_spec(tm, D), _vec_spec(1, D)],
        out_specs=[_row_spec(tm, D), _vec_spec(8, D)],
        out_shape=[jax.ShapeDtypeStruct((S, D), BF16), jax.ShapeDtypeStruct((8, D), F32)],
        compiler_params=_params("arbitrary"),
    )(dxn, y, gate)


def _pre_norm_bwd(x, dxn, dhs, scales, name):
    S, D = x.shape
    nb = len(dhs)
    tm = _pick(S, 512, 8)

    def body(x_ref, d_ref, a_ref, *rest):
        dh_refs, dx_ref, ds_ref, db_ref = rest[:nb], rest[nb], rest[nb + 1], rest[nb + 2]
        xv = x_ref[...]
        r = lax.rsqrt(jnp.mean(xv * xv, axis=-1, keepdims=True) + EPS)
        xh = xv * r

        @pl.when(pl.program_id(0) == 0)
        def _():
            ds_ref[...] = jnp.zeros_like(ds_ref)
            db_ref[...] = jnp.zeros_like(db_ref)

        dxh = jnp.zeros_like(xv)
        for n in range(nb):
            dh = dh_refs[n][...].astype(F32)
            dxh = dxh + dh * a_ref[n:n + 1, :]
            ds_ref[n] += _colsum8(dh * xh)
            db_ref[n] += _colsum8(dh)
        dx_ref[...] = d_ref[...] + r * (dxh - xh * jnp.mean(dxh * xh, axis=-1, keepdims=True))

    acc_spec = pl.BlockSpec((nb, 8, D), lambda i: (0, 0, 0))
    return pl.pallas_call(
        body, name=name, grid=(S // tm,),
        in_specs=[_row_spec(tm, D), _row_spec(tm, D), _vec_spec(nb, D)] + [_row_spec(tm, D)] * nb,
        out_specs=[_row_spec(tm, D), acc_spec, acc_spec],
        out_shape=[jax.ShapeDtypeStruct((S, D), F32), jax.ShapeDtypeStruct((nb, 8, D), F32),
                   jax.ShapeDtypeStruct((nb, 8, D), F32)],
        compiler_params=_params("arbitrary"),
    )(x, dxn, scales, *dhs)


FFN_PAIRS = 2
FFN_SUB_ROWS = 256


def _ffn_in_act(h, w, layer, name):
    S, D = h.shape
    F2 = w.shape[2]
    PW = F2 // (2 * FFN_PAIRS)
    tm = _pick(S, 512, 16)
    sub = _pick(tm, FFN_SUB_ROWS, 16)

    def body(h_ref, w_ref, gu_ref, a_ref):
        for r in range(tm // sub):
            rows = pl.ds(r * sub, sub)
            acc = jnp.dot(h_ref[rows, :], w_ref[...], preferred_element_type=F32)
            gu_ref[rows, :] = acc.astype(BF16)
            g = acc[:, :PW]
            a_ref[rows, :] = (g * jax.nn.sigmoid(g) * acc[:, PW:]).astype(BF16)

    return pl.pallas_call(
        body, name=name, grid=(FFN_PAIRS, S // tm),
        in_specs=[pl.BlockSpec((tm, D), lambda p, i: (i, 0)),
                  pl.BlockSpec((None, D, 2 * PW), lambda p, i: (layer, 0, p))],
        out_specs=[pl.BlockSpec((tm, 2 * PW), lambda p, i: (i, p)), pl.BlockSpec((tm, PW), lambda p, i: (i, p))],
        out_shape=[jax.ShapeDtypeStruct((S, F2), BF16), jax.ShapeDtypeStruct((S, F2 // 2), BF16)],
        compiler_params=_params("parallel", "parallel"),
    )(h, w)


def _ffn_out_dx_act(dy, w, layer, gu, name):
    S, D = dy.shape
    F2 = gu.shape[1]
    PW = F2 // (2 * FFN_PAIRS)
    tm = _pick(S, 512, 16)
    sub = _pick(tm, FFN_SUB_ROWS, 16)

    def body(dy_ref, w_ref, gu_ref, o_ref):
        for r in range(tm // sub):
            rows = pl.ds(r * sub, sub)
            da = lax.dot_general(dy_ref[rows, :], w_ref[...], _DIMS["nt"], preferred_element_type=F32)
            g = gu_ref[rows, 0:PW].astype(F32)
            u = gu_ref[rows, PW:2 * PW].astype(F32)
            sg = jax.nn.sigmoid(g)
            o_ref[rows, 0:PW] = (da * u * (sg * (1.0 + g * (1.0 - sg)))).astype(BF16)
            o_ref[rows, PW:2 * PW] = (da * (g * sg)).astype(BF16)

    return pl.pallas_call(
        body, name=name, grid=(FFN_PAIRS, S // tm),
        in_specs=[pl.BlockSpec((tm, D), lambda p, i: (i, 0)),
                  pl.BlockSpec((None, PW, D), lambda p, i: (layer, p, 0)),
                  pl.BlockSpec((tm, 2 * PW), lambda p, i: (i, p))],
        out_specs=pl.BlockSpec((tm, 2 * PW), lambda p, i: (i, p)),
        out_shape=jax.ShapeDtypeStruct((S, F2), BF16),
        compiler_params=_params("parallel", "parallel"),
    )(dy, w, gu)


HALO = 16


def _conv_terms(bcx_ref, prev_ref, i, tm, D):
    b = bcx_ref[:, 0:D].astype(F32)
    cg = bcx_ref[:, D:2 * D].astype(F32)
    xin = bcx_ref[:, 2 * D:3 * D].astype(F32)
    z = cg * xin
    zp = prev_ref[:, D:2 * D].astype(F32) * prev_ref[:, 2 * D:3 * D].astype(F32)
    zp = jnp.where(i > 0, zp, 0.0)
    row = lax.broadcasted_iota(jnp.int32, (tm, D), 0)
    p1, p2 = zp[HALO - 1:HALO, :], zp[HALO - 2:HALO - 1, :]
    z1 = jnp.where(row == 0, p1, pltpu.roll(z, 1, 0))
    z2 = jnp.where(row == 0, p2, jnp.where(row == 1, p1, pltpu.roll(z, 2, 0)))
    return b, cg, xin, z, z1, z2, row


def _conv_gate(bcx, ck, name):
    S, D3 = bcx.shape
    D = D3 // 3
    tm = _pick(S, 256, 16)
    hb = tm // HALO

    def body(bcx_ref, prev_ref, ck_ref, o_ref):
        i = pl.program_id(0)
        b, _, _, z, z1, z2, _ = _conv_terms(bcx_ref, prev_ref, i, tm, D)
        conv = ck_ref[0:1, :] * z2 + ck_ref[1:2, :] * z1 + ck_ref[2:3, :] * z
        o_ref[...] = (b * conv).astype(BF16)

    return pl.pallas_call(
        body, name=name, grid=(S // tm,),
        in_specs=[_row_spec(tm, D3),
                  pl.BlockSpec((HALO, D3), lambda i: (jnp.maximum(i * hb - 1, 0), 0)),
                  _vec_spec(8, D)],
        out_specs=_row_spec(tm, D),
        out_shape=jax.ShapeDtypeStruct((S, D), BF16),
        compiler_params=_params("parallel"),
    )(bcx, bcx, ck)


def _conv_gate_bwd(du, bcx, ck, name):
    S, D3 = bcx.shape
    D = D3 // 3
    tm = _pick(S, 256, 16)
    hb = tm // HALO
    nt = S // tm

    def body(du_ref, dun_ref, bcx_ref, prev_ref, next_ref, ck_ref, o_ref, dk_ref):
        i = pl.program_id(0)
        b, cg, xin, z, z1, z2, row = _conv_terms(bcx_ref, prev_ref, i, tm, D)
        k0, k1, k2 = ck_ref[0:1, :], ck_ref[1:2, :], ck_ref[2:3, :]
        conv = k0 * z2 + k1 * z1 + k2 * z
        d = du_ref[...].astype(F32)
        dconv = d * b
        dcn = jnp.where(i < nt - 1, dun_ref[...].astype(F32) * next_ref[:, 0:D].astype(F32), 0.0)
        d1 = jnp.where(row == tm - 1, dcn[0:1, :], pltpu.roll(dconv, tm - 1, 0))
        d2 = jnp.where(row == tm - 2, dcn[0:1, :],
                       jnp.where(row == tm - 1, dcn[1:2, :], pltpu.roll(dconv, tm - 2, 0)))
        dz = k2 * dconv + k1 * d1 + k0 * d2
        o_ref[:, 0:D] = (d * conv).astype(BF16)
        o_ref[:, D:2 * D] = (dz * xin).astype(BF16)
        o_ref[:, 2 * D:3 * D] = (dz * cg).astype(BF16)

        @pl.when(i == 0)
        def _():
            dk_ref[...] = jnp.zeros_like(dk_ref)

        dk_ref[0] += _colsum8(dconv * z2)
        dk_ref[1] += _colsum8(dconv * z1)
        dk_ref[2] += _colsum8(dconv * z)

    last = S // HALO - 1
    return pl.pallas_call(
        body, name=name, grid=(nt,),
        in_specs=[_row_spec(tm, D),
                  pl.BlockSpec((HALO, D), lambda i: (jnp.minimum((i + 1) * hb, last), 0)),
                  _row_spec(tm, D3),
                  pl.BlockSpec((HALO, D3), lambda i: (jnp.maximum(i * hb - 1, 0), 0)),
                  pl.BlockSpec((HALO, D3), lambda i: (jnp.minimum((i + 1) * hb, last), 0)),
                  _vec_spec(8, D)],
        out_specs=[_row_spec(tm, D3), pl.BlockSpec((3, 8, D), lambda i: (0, 0, 0))],
        out_shape=[jax.ShapeDtypeStruct((S, D3), BF16), jax.ShapeDtypeStruct((3, 8, D), F32)],
        compiler_params=_params("arbitrary"),
    )(du, du, bcx, bcx, bcx, ck)


def _rel_onehot():
    a = np.arange(CHUNK)[:, None]
    b = np.arange(CHUNK)[None, :]
    idx = np.stack([np.clip((N_LEFT_CHUNKS - dl) * CHUNK + a - b, -MAX_REL, MAX_REL) + MAX_REL
                    for dl in (6, 7, 8)]).reshape(-1)
    return (jnp.asarray(idx)[:, None] == jnp.arange(N_REL)[None, :]).astype(F32)


def _bias_table(rel_bias):
    H = rel_bias.shape[0]
    near = jnp.dot(rel_bias, _rel_onehot().T, precision=lax.Precision.HIGHEST).reshape(H, 3, CHUNK, CHUNK)
    far = jnp.broadcast_to(rel_bias[:, N_REL - 1][:, None, None], (H, CHUNK, CHUNK))
    neg = jnp.full((H, CHUNK, CHUNK), NEG, F32)

    def block(dl):
        if dl < 0 or dl > N_LEFT_CHUNKS:
            return neg
        return far if dl <= 5 else near[:, dl - 6]

    rows = [jnp.concatenate([block(jc - ic) for jc in range(N_WIN * Q_CHUNKS)], axis=-1)
            for ic in range(Q_CHUNKS)]
    return jnp.concatenate(rows, axis=-2)


def _bias_table_grad(dtab):
    H = dtab.shape[0]
    blk = lambda ic, jc: dtab[:, ic * CHUNK:(ic + 1) * CHUNK, jc * CHUNK:(jc + 1) * CHUNK]
    by_dl = [sum(blk(ic, ic + dl) for ic in range(Q_CHUNKS)) for dl in range(N_LEFT_CHUNKS + 1)]
    far = sum(jnp.sum(by_dl[dl], axis=(1, 2)) for dl in range(6))
    near = jnp.stack(by_dl[6:9], axis=1).reshape(H, 3 * CHUNK * CHUNK)
    g = jnp.dot(near, _rel_onehot(), precision=lax.Precision.HIGHEST)
    return g.at[:, N_REL - 1].add(far)


def _attn_specs(D, W):
    q_spec = pl.BlockSpec((BQ, W), lambda g, i: (i, g))

    def win(w, off):
        return pl.BlockSpec((BQ, W), lambda g, i: (jnp.maximum(i - (N_WIN - 1) + w, 0), off + g))

    k_specs = [win(w, 0) for w in range(N_WIN)]
    v_specs = [win(w, D // W) for w in range(N_WIN)]
    tab_spec = pl.BlockSpec((HEADS_PER_STEP, BQ, N_WIN * BQ), lambda g, i: (g, 0, 0))
    return q_spec, k_specs, v_specs, tab_spec


def _attn_exp(q_ref, kw, tab_ref, h, dh, i):
    qh = q_ref[:, h * dh:(h + 1) * dh]
    kh = kw[:, h * dh:(h + 1) * dh]
    s = lax.dot_general(qh, kh, _DIMS["nt"], preferred_element_type=F32) + tab_ref[h]

    def before_start(v):
        col = lax.broadcasted_iota(jnp.int32, v.shape, 1)
        return jnp.where(col >= (N_WIN - 1 - i) * BQ, v, NEG)

    s = lax.cond(i < N_WIN - 1, before_start, lambda v: v, s)
    e = jnp.exp(s - jnp.max(s, axis=-1, keepdims=True))
    return e, jnp.sum(e, axis=-1, keepdims=True), qh, kh


def _attn_fwd(q, kv, tab, name):
    S, D = q.shape
    dh = D // N_HEADS
    W = HEADS_PER_STEP * dh
    q_spec, k_specs, v_specs, tab_spec = _attn_specs(D, W)

    def body(q_ref, *rest):
        k_refs, v_refs = rest[:N_WIN], rest[N_WIN:2 * N_WIN]
        tab_ref, o_ref = rest[2 * N_WIN], rest[2 * N_WIN + 1]
        i = pl.program_id(1)
        kw = jnp.concatenate([r[...] for r in k_refs], axis=0)
        vw = jnp.concatenate([r[...] for r in v_refs], axis=0)
        outs = []
        for h in range(HEADS_PER_STEP):
            e, l, _, _ = _attn_exp(q_ref, kw, tab_ref, h, dh, i)
            outs.append(jnp.dot(e.astype(BF16), vw[:, h * dh:(h + 1) * dh], preferred_element_type=F32) / l)
        o_ref[...] = jnp.concatenate(outs, axis=1).astype(BF16)

    return pl.pallas_call(
        body, name=name, grid=(N_HEADS // HEADS_PER_STEP, S // BQ),
        in_specs=[q_spec] + k_specs + v_specs + [tab_spec],
        out_specs=q_spec,
        out_shape=jax.ShapeDtypeStruct((S, D), BF16),
        compiler_params=_params("parallel", "parallel"),
    )(q, *([kv] * (2 * N_WIN)), tab)


def _attn_bwd(q, kv, tab, do, name):
    S, D = q.shape
    dh = D // N_HEADS
    W = HEADS_PER_STEP * dh
    q_spec, k_specs, v_specs, tab_spec = _attn_specs(D, W)

    def body(q_ref, *rest):
        k_refs, v_refs = rest[:N_WIN], rest[N_WIN:2 * N_WIN]
        tab_ref, do_ref, dq_ref = rest[2 * N_WIN:2 * N_WIN + 3]
        dk_refs = rest[2 * N_WIN + 3:3 * N_WIN + 3]
        dv_refs = rest[3 * N_WIN + 3:4 * N_WIN + 3]
        dtab_ref = rest[4 * N_WIN + 3]
        i = pl.program_id(1)

        @pl.when(i == 0)
        def _():
            dtab_ref[...] = jnp.zeros_like(dtab_ref)

        kw = jnp.concatenate([r[...] for r in k_refs], axis=0)
        vw = jnp.concatenate([r[...] for r in v_refs], axis=0)
        dqs, dks, dvs = [], [], []
        for h in range(HEADS_PER_STEP):
            e, l, qh, kh = _attn_exp(q_ref, kw, tab_ref, h, dh, i)
            p = e * (1.0 / l)
            vh = vw[:, h * dh:(h + 1) * dh]
            doh = do_ref[:, h * dh:(h + 1) * dh]
            dp = lax.dot_general(doh, vh, _DIMS["nt"], preferred_element_type=F32)
            ds = p * (dp - jnp.sum(p * dp, axis=-1, keepdims=True))
            dtab_ref[h] += ds
            dsb = ds.astype(BF16)
            dqs.append(jnp.dot(dsb, kh, preferred_element_type=F32) * (dh ** -0.5))
            dks.append(lax.dot_general(dsb, qh, _DIMS["tn"], preferred_element_type=F32))
            dvs.append(lax.dot_general(p.astype(BF16), doh, _DIMS["tn"], preferred_element_type=F32))
        dq_ref[...] = jnp.concatenate(dqs, axis=1).astype(BF16)
        dk = jnp.concatenate(dks, axis=1)
        dv = jnp.concatenate(dvs, axis=1)
        for w in range(N_WIN):
            dk_refs[w][...] = dk[w * BQ:(w + 1) * BQ, :].astype(BF16)
            dv_refs[w][...] = dv[w * BQ:(w + 1) * BQ, :].astype(BF16)

    part = jax.ShapeDtypeStruct((S, D), BF16)
    outs = pl.pallas_call(
        body, name=name, grid=(N_HEADS // HEADS_PER_STEP, S // BQ),
        in_specs=[q_spec] + k_specs + v_specs + [tab_spec, q_spec],
        out_specs=[q_spec] + [q_spec] * (2 * N_WIN) + [tab_spec],
        out_shape=[jax.ShapeDtypeStruct((S, D), BF16)] + [part] * (2 * N_WIN)
        + [jax.ShapeDtypeStruct(tab.shape, F32)],
        compiler_params=_params("parallel", "arbitrary"),
    )(q, *([kv] * (2 * N_WIN)), tab, do)
    return outs[0], outs[1:1 + N_WIN], outs[1 + N_WIN:1 + 2 * N_WIN], outs[-1]


def _kv_grad_combine(dk_parts, dv_parts, name):
    S, D = dk_parts[0].shape
    nblk = S // BQ

    def body(*refs):
        o_ref = refs[2 * N_WIN]
        i = pl.program_id(0)
        for half, parts in enumerate((refs[:N_WIN], refs[N_WIN:2 * N_WIN])):
            acc = parts[N_WIN - 1][...].astype(F32)
            for w in range(N_WIN - 1):
                acc = acc + jnp.where(i + (N_WIN - 1 - w) < nblk, parts[w][...].astype(F32), 0.0)
            o_ref[:, half * D:(half + 1) * D] = acc.astype(BF16)

    specs = [pl.BlockSpec((BQ, D), functools.partial(
        lambda i, sh: (jnp.minimum(i + sh, nblk - 1), 0), sh=N_WIN - 1 - w)) for w in range(N_WIN)]
    return pl.pallas_call(
        body, name=name, grid=(nblk,),
        in_specs=specs + specs,
        out_specs=pl.BlockSpec((BQ, 2 * D), lambda i: (i, 0)),
        out_shape=jax.ShapeDtypeStruct((S, 2 * D), BF16),
        compiler_params=_params("parallel"),
    )(*dk_parts, *dv_parts)


def _adamw(w, g, m, v, name):
    shape = w.shape
    C = shape[-1]
    R = int(np.prod(shape[:-1])) if len(shape) > 1 else 1
    w2, g2, m2, v2 = (t.reshape(R, C) for t in (w, g, m, v))
    tr = _pick(R, max(8, (512 * 1024) // C // 8 * 8), 8)

    def body(w_ref, g_ref, m_ref, v_ref, d_ref, nm_ref, nv_ref):
        gv = g_ref[...]
        nm = ADAM_B1 * m_ref[...] + (1.0 - ADAM_B1) * gv
        nv = ADAM_B2 * v_ref[...] + (1.0 - ADAM_B2) * jnp.square(gv)
        m_hat = nm / (1.0 - ADAM_B1 ** ADAM_STEP)
        v_hat = nv / (1.0 - ADAM_B2 ** ADAM_STEP)
        d_ref[...] = -ADAM_LR * (m_hat / (jnp.sqrt(v_hat) + ADAM_EPS) + ADAM_WD * w_ref[...])
        nm_ref[...] = nm
        nv_ref[...] = nv

    spec = pl.BlockSpec((tr, C), lambda i: (i, 0))
    outs = pl.pallas_call(
        body, name=name, grid=(R // tr,),
        in_specs=[spec] * 4, out_specs=[spec] * 3,
        out_shape=[jax.ShapeDtypeStruct((R, C), F32)] * 3,
        compiler_params=_params("parallel"),
    )(w2, g2, m2, v2)
    return tuple(o.reshape(shape) for o in outs)


def _sum_rows(a, name):
    n, L = a.shape

    def body(a_ref, o_ref):
        acc = a_ref[0:1, :]
        for r in range(1, n):
            acc = acc + a_ref[r:r + 1, :]
        o_ref[...] = acc

    return pl.pallas_call(
        body, name=name, grid=(1,),
        in_specs=[pl.BlockSpec((n, L), lambda i: (0, 0))],
        out_specs=pl.BlockSpec((1, L), lambda i: (0, 0)),
        out_shape=jax.ShapeDtypeStruct((1, L), F32),
        compiler_params=_params("arbitrary"),
    )(a)


def _scalar_call(body, name, scalar, grid, in_specs, out_spec, out_shape, args):
    return pl.pallas_call(
        body, name=name,
        grid_spec=pltpu.PrefetchScalarGridSpec(num_scalar_prefetch=1, grid=grid, in_specs=in_specs,
                                               out_specs=out_spec),
        out_shape=out_shape, compiler_params=_params("parallel"),
    )(jnp.reshape(scalar, (-1,)).astype(jnp.int32), *args)


def _pair_sum(view, got, c, name):
    nb, _, rh, cols = view.shape
    tr = _pick(rh, max(16, (1 << 20) // cols // 16 * 16), 16)
    bpr = rh // tr

    def body(s_ref, a_ref, b_ref, o_ref):
        o_ref[...] = (a_ref[...].astype(F32) + b_ref[...].astype(F32)).astype(BF16)

    spec = pl.BlockSpec((tr, cols), lambda i, s: (i, 0))
    mine = pl.BlockSpec((tr, cols), lambda i, s: ((2 * (i // bpr) + s[0]) * bpr + i % bpr, 0))
    return _scalar_call(body, name, c, (nb * bpr,), [mine, spec], spec,
                        jax.ShapeDtypeStruct((nb * rh, cols), BF16),
                        (view.reshape(nb * 2 * rh, cols), got.reshape(nb * rh, cols)))


def _owner_sum(pair, recv, me, c, it, name):
    _, rh, bc = recv.shape
    tr = _pick(rh, max(16, (1 << 19) // bc // 16 * 16), 16)
    bpr = rh // tr

    def body(s_ref, a_ref, r0, r1, r2, o_ref):
        o_ref[...] = ((a_ref[...].astype(F32) + r0[...].astype(F32)) + r1[...].astype(F32)) + r2[...].astype(F32)

    if it.kind == "col":
        own = pl.BlockSpec((tr, bc), lambda i, s: (i, s[0]))
    else:
        own = pl.BlockSpec((tr, bc), lambda i, s: (s[0] * bpr + i, 0))
    slots = [pl.BlockSpec((None, tr, bc), functools.partial(lambda i, s, k: (k, i, 0), k=k)) for k in range(3)]
    return _scalar_call(body, name, jnp.stack([it.pos(me), c]), (bpr,), [own] + slots,
                        pl.BlockSpec((tr, bc), lambda i, s: (s[1] * bpr + i, 0)),
                        jax.ShapeDtypeStruct((2 * rh, bc), F32), (pair, recv, recv, recv))


def _place():
    x, y, c = lax.axis_index("x"), lax.axis_index("y"), lax.axis_index("c")
    chips = [(1 - x, y), (x, 1 - y), (1 - x, 1 - y)]
    return x, y, c, chips


def _chip_index(px, py):
    return 2 * px + py


def _all_gather_small(x_shard, name):
    m_per, n = x_shard.shape

    def body(x_ref, out_ref, send_sems, recv_sems, local_sem):
        x, y, c, chips = _place()
        me, sibling = (x, y, c), (x, y, 1 - c)

        def rows(px, py, pc):
            return out_ref.at[pl.ds((4 * px + 2 * py + pc) * m_per, m_per), :]

        def copy(k, block, to, src=None):
            return pltpu.make_async_remote_copy(
                src_ref=rows(*block) if src is None else src, dst_ref=rows(*block),
                send_sem=send_sems.at[k], recv_sem=recv_sems.at[k], device_id=to, device_id_type=MESH)

        mine = pltpu.make_async_copy(x_ref, rows(*me), local_sem)
        mine.start()
        first = [copy(0, me, sibling, src=x_ref)]
        first += [copy(1 + j, me, (*chip, c), src=x_ref) for j, chip in enumerate(chips)]
        for cp in first:
            cp.start()
        passed = [copy(4 + j, (*chip, c), sibling) for j, chip in enumerate(chips)]
        for j, chip in enumerate(chips):
            copy(1 + j, (*chip, c), me).wait_recv()
            passed[j].start()
        copy(0, sibling, me).wait_recv()
        for j, chip in enumerate(chips):
            copy(4 + j, (*chip, 1 - c), me).wait_recv()
        for cp in first + passed:
            cp.wait_send()
        mine.wait()

    return pl.pallas_call(
        body, name=name,
        out_shape=jax.ShapeDtypeStruct((N_DEV * m_per, n), x_shard.dtype),
        in_specs=[pl.BlockSpec(memory_space=pltpu.VMEM)],
        out_specs=pl.BlockSpec(memory_space=pltpu.VMEM),
        scratch_shapes=[pltpu.SemaphoreType.DMA((7,)), pltpu.SemaphoreType.DMA((7,)), pltpu.SemaphoreType.DMA],
    )(x_shard)


def _gather_flat(vec, name):
    L = vec.shape[0]
    Lp = -(-L // 1024) * 1024
    g = _all_gather_small(jnp.pad(vec, (0, Lp - L)).reshape(8, Lp // 8), name)
    return g.reshape(N_DEV, Lp)[:, :L]


class _Item:
    def __init__(self, kind, rows, cols, arg, layer, swap=False):
        self.kind, self.rows, self.cols, self.arg, self.layer, self.swap = kind, rows, cols, arg, layer, swap

    def ref(self, refs):
        return refs[self.arg].at[self.layer]

    def pos(self, j):
        return 2 * (j % 2) + j // 2 if self.swap else j


def _block(ref, it, j, half):
    if it.kind == "col":
        ns = it.cols // N_CHIP
        return ref.at[pl.ds(half * (it.rows // 2), it.rows // 2), pl.ds(it.pos(j) * ns, ns)]
    rs = it.rows // N_CHIP
    return ref.at[pl.ds(j * rs + half * (rs // 2), rs // 2), :]


def _cast_place(w, kind, pos, name):
    L, r, n = w.shape
    tr = _pick(r, max(16, (1 << 20) // n // 16 * 16), 16)
    bpr = r // tr

    def body(s_ref, w_ref, o_ref):
        o_ref[...] = w_ref[...].astype(BF16)

    if kind == "col":
        full, out_idx = (L, r, N_CHIP * n), (lambda l, i, s: (l, i, s[0]))
    else:
        full, out_idx = (L, N_CHIP * r, n), (lambda l, i, s: (l, s[0] * bpr + i, 0))
    return pl.pallas_call(
        body, name=name,
        grid_spec=pltpu.PrefetchScalarGridSpec(
            num_scalar_prefetch=1, grid=(L, bpr),
            in_specs=[pl.BlockSpec((None, tr, n), lambda l, i, s: (l, i, 0))],
            out_specs=pl.BlockSpec((None, tr, n), out_idx)),
        out_shape=jax.ShapeDtypeStruct(full, BF16),
        compiler_params=_params("parallel", "parallel"),
    )(jnp.reshape(pos, (1,)).astype(jnp.int32), w)


def _gather_weights(bufs, items, name):
    n_arg, n_it = len(bufs), len(items)

    def body(*refs):
        ins, outs = refs[:n_arg], refs[n_arg:2 * n_arg]
        send, recv, fsend, frecv = refs[2 * n_arg:]
        x, y, c, chips = _place()
        me = _chip_index(x, y)
        sends, fwds = [], []
        for t, it in enumerate(items):
            for k, chip in enumerate(chips):
                cp = pltpu.make_async_remote_copy(
                    src_ref=_block(it.ref(ins), it, me, c), dst_ref=_block(it.ref(outs), it, me, c),
                    send_sem=send.at[t, k], recv_sem=recv.at[t, k], device_id=(*chip, c), device_id_type=MESH)
                cp.start()
                sends.append(cp)
        for t, it in enumerate(items):
            for k, chip in enumerate(chips):
                landed = _block(it.ref(outs), it, _chip_index(*chip), c)
                pltpu.make_async_remote_copy(
                    src_ref=landed, dst_ref=landed, send_sem=send.at[t, k], recv_sem=recv.at[t, k],
                    device_id=(*chip, c), device_id_type=MESH).wait_recv()
                cp = pltpu.make_async_remote_copy(
                    src_ref=landed, dst_ref=landed, send_sem=fsend.at[t, k], recv_sem=frecv.at[t, k],
                    device_id=(x, y, 1 - c), device_id_type=MESH)
                cp.start()
                fwds.append(cp)
        for t, it in enumerate(items):
            for k, chip in enumerate(chips):
                other = _block(it.ref(outs), it, _chip_index(*chip), 1 - c)
                pltpu.make_async_remote_copy(
                    src_ref=other, dst_ref=other, send_sem=fsend.at[t, k], recv_sem=frecv.at[t, k],
                    device_id=(x, y, 1 - c), device_id_type=MESH).wait_recv()
        for cp in sends + fwds:
            cp.wait_send()

    any_spec = pl.BlockSpec(memory_space=pl.ANY)
    return pl.pallas_call(
        body, name=name,
        out_shape=[jax.ShapeDtypeStruct(b.shape, b.dtype) for b in bufs],
        in_specs=[any_spec] * n_arg, out_specs=[any_spec] * n_arg,
        input_output_aliases={t: t for t in range(n_arg)},
        scratch_shapes=[pltpu.SemaphoreType.DMA((n_it, 3))] * 4,
    )(*bufs)


def _pair_view(g, it):
    if it.kind == "col":
        return g.reshape(1, 2, it.rows // 2, it.cols)
    return g.reshape(N_CHIP, 2, it.rows // (2 * N_CHIP), it.cols)


def _pair_exchange(views, name):
    n = len(views)

    def body(*refs):
        ins, outs, send, recv = refs[:n], refs[n:2 * n], refs[2 * n], refs[2 * n + 1]
        x, y, c, _ = _place()
        cps = []
        for t in range(n):
            cp = pltpu.make_async_remote_copy(
                src_ref=ins[t].at[:, pl.ds(1 - c, 1)], dst_ref=outs[t],
                send_sem=send.at[t], recv_sem=recv.at[t], device_id=(x, y, 1 - c), device_id_type=MESH)
            cp.start()
            cps.append(cp)
        for cp in cps:
            cp.wait()

    any_spec = pl.BlockSpec(memory_space=pl.ANY)
    return pl.pallas_call(
        body, name=name,
        out_shape=[jax.ShapeDtypeStruct((v.shape[0], 1) + v.shape[2:], v.dtype) for v in views],
        in_specs=[any_spec] * n, out_specs=[any_spec] * n,
        scratch_shapes=[pltpu.SemaphoreType.DMA((n,)), pltpu.SemaphoreType.DMA((n,))],
    )(*views)


def _owner_exchange(pairs, items, name):
    n = len(items)

    def blk(ref, it, j):
        if it.kind == "col":
            ns = it.cols // N_CHIP
            return ref.at[:, pl.ds(it.pos(j) * ns, ns)]
        return ref.at[j]

    def body(*refs):
        ins, outs, send, recv = refs[:n], refs[n:2 * n], refs[2 * n], refs[2 * n + 1]
        x, y, c, chips = _place()
        cps = []
        for t, it in enumerate(items):
            for k, chip in enumerate(chips):
                cp = pltpu.make_async_remote_copy(
                    src_ref=blk(ins[t], it, _chip_index(*chip)), dst_ref=outs[t].at[k],
                    send_sem=send.at[t, k], recv_sem=recv.at[t, k], device_id=(*chip, c), device_id_type=MESH)
                cp.start()
                cps.append(cp)
        for cp in cps:
            cp.wait()

    def slot_shape(it):
        if it.kind == "col":
            return (3, it.rows // 2, it.cols // N_CHIP)
        return (3, it.rows // (2 * N_CHIP), it.cols)

    any_spec = pl.BlockSpec(memory_space=pl.ANY)
    return pl.pallas_call(
        body, name=name,
        out_shape=[jax.ShapeDtypeStruct(slot_shape(it), BF16) for it in items],
        in_specs=[any_spec] * n, out_specs=[any_spec] * n,
        scratch_shapes=[pltpu.SemaphoreType.DMA((n, 3)), pltpu.SemaphoreType.DMA((n, 3))],
    )(*pairs)


def _half_exchange(bufs, name):
    n = len(bufs)

    def body(*refs):
        ins, outs, send, recv = refs[:n], refs[n:2 * n], refs[2 * n], refs[2 * n + 1]
        x, y, c, _ = _place()
        cps = []
        for t in range(n):
            r2 = ins[t].shape[0] // 2
            cp = pltpu.make_async_remote_copy(
                src_ref=ins[t].at[pl.ds(c * r2, r2), :], dst_ref=outs[t].at[pl.ds(c * r2, r2), :],
                send_sem=send.at[t], recv_sem=recv.at[t], device_id=(x, y, 1 - c), device_id_type=MESH)
            cp.start()
            cps.append(cp)
        for t in range(n):
            r2 = ins[t].shape[0] // 2
            theirs = outs[t].at[pl.ds((1 - c) * r2, r2), :]
            pltpu.make_async_remote_copy(
                src_ref=theirs, dst_ref=theirs, send_sem=send.at[t], recv_sem=recv.at[t],
                device_id=(x, y, 1 - c), device_id_type=MESH).wait_recv()
        for cp in cps:
            cp.wait_send()

    any_spec = pl.BlockSpec(memory_space=pl.ANY)
    return pl.pallas_call(
        body, name=name,
        out_shape=[jax.ShapeDtypeStruct(b.shape, b.dtype) for b in bufs],
        in_specs=[any_spec] * n, out_specs=[any_spec] * n,
        input_output_aliases={t: t for t in range(n)},
        scratch_shapes=[pltpu.SemaphoreType.DMA((n,)), pltpu.SemaphoreType.DMA((n,))],
    )(*bufs)


def _reduce_scatter(grads, items):
    x, y, c, _ = _place()
    me = _chip_index(x, y)
    views = [_pair_view(g, it) for g, it in zip(grads, items)]
    got = _pair_exchange(views, "rs_pair_exchange")
    pairs = [_pair_sum(v, r, c, f"rs_pair_sum_{t}") for t, (v, r) in enumerate(zip(views, got))]
    shaped = [p if it.kind == "col" else p.reshape(N_CHIP, p.shape[0] // N_CHIP, p.shape[1])
              for p, it in zip(pairs, items)]
    recv = _owner_exchange(shaped, items, "rs_owner_exchange")
    halves = [_owner_sum(p, r, me, c, it, f"rs_owner_sum_{t}") for t, (p, r, it) in enumerate(zip(pairs, recv, items))]
    return _half_exchange(halves, "rs_half_exchange")


def _silu(v):
    return v * jax.nn.sigmoid(v)


def _sum8(p):
    return jnp.sum(p, axis=-2)


def kernel(x, c, mod_w, mod_b, norm_g, ffn_w_in, ffn_w_out, conv_w_in, conv_k, conv_w_out, kv_mod_w, kv_mod_b, kv_norm_g, w_kv, attn_w_q, attn_w_o, rel_bias, loss_target, m_mod_w, m_mod_b, m_norm_g, m_ffn_w_in, m_ffn_w_out, m_conv_w_in, m_conv_k, m_conv_w_out, m_kv_mod_w, m_kv_mod_b, m_kv_norm_g, m_w_kv, m_attn_w_q, m_attn_w_o, m_rel_bias, v_mod_w, v_mod_b, v_norm_g, v_ffn_w_in, v_ffn_w_out, v_conv_w_in, v_conv_k, v_conv_w_out, v_kv_mod_w, v_kv_mod_b, v_kv_norm_g, v_w_kv, v_attn_w_q, v_attn_w_o, v_rel_bias):
    xi, yi, ci = lax.axis_index("x"), lax.axis_index("y"), lax.axis_index("c")
    chip = 2 * xi + yi
    dev = 2 * chip + ci
    _, S, D = x.shape
    F = ffn_w_out.shape[1] * N_CHIP
    x0 = x.reshape(S, D)
    target = loss_target.reshape(S, D)
    n_mod = mod_w.shape[2]
    n_kvm = kv_mod_w.shape[1]
    dsh = D // N_CHIP
    TF = F // 2

    c_all = _all_gather_small(c.reshape(8, D // 8), "ag_c").reshape(N_DEV, D)
    sc16 = jnp.pad(_silu(c_all), ((0, 8), (0, 0)))
    part = [_mm(sc16, mod_w, "nn", F32, f"mod_fwd_{l}", b_layer=l)[:8] for l in range(2)]
    part.append(_mm(sc16, kv_mod_w, "nn", F32, "mod_fwd_kv")[:8])
    fwd_vec = jnp.concatenate([p.reshape(-1) for p in part] + [norm_g.reshape(-1), conv_k.reshape(-1)])
    fwd_all = _gather_flat(fwd_vec, "ag_fwd_small")[0::2]
    o = 0
    mods = []
    for n in (n_mod, n_mod, n_kvm):
        blk = fwd_all[:, o:o + 8 * n].reshape(N_CHIP, 8, n)
        mods.append(lax.dynamic_index_in_dim(blk, dev, axis=1, keepdims=False).reshape(N_CHIP * n))
        o += 8 * n
    ng = fwd_all[:, o:o + 8 * dsh].reshape(N_CHIP, 2, 4, dsh).transpose(1, 2, 0, 3).reshape(2, 4, D)
    o += 8 * dsh
    ck = fwd_all[:, o:o + 3 * dsh].reshape(N_CHIP, 3, dsh).transpose(1, 0, 2).reshape(3, D)
    ck8 = jnp.pad(ck, ((0, 5), (0, 0)))
    mod = [mods[l] + mod_b[l] for l in range(2)]
    sh1, sc1, g1, sh2, sc2, g2 = zip(*[jnp.split(m, 6) for m in mod])
    kv_sh, kv_sc = jnp.split(mods[2] + kv_mod_b, 2)
    row = lambda v: v.reshape(1, D)

    items = [_Item("col", D, 3 * D, 0, 0), _Item("row", D, D, 1, 0),
             _Item("col", D, 2 * F, 2, 0, swap=True), _Item("row", F, D, 3, 0),
             _Item("col", D, 2 * D, 4, 0), _Item("row", D, D, 5, 0), _Item("row", D, D, 6, 0),
             _Item("col", D, 2 * F, 2, 1, swap=True), _Item("row", F, D, 3, 1)]
    placed = [_cast_place(w, kind, _Item(kind, 0, 0, 0, 0, swap).pos(chip), f"place_{nm}")
              for w, kind, swap, nm in (
                  (conv_w_in, "col", False, "conv_w_in"), (conv_w_out, "row", False, "conv_w_out"),
                  (ffn_w_in, "col", True, "ffn_w_in"), (ffn_w_out, "row", False, "ffn_w_out"),
                  (w_kv[None], "col", False, "w_kv"), (attn_w_q, "row", False, "attn_w_q"),
                  (attn_w_o, "row", False, "attn_w_o"))]
    W_cin, W_cout, W_fin, W_fout, W_kv, W_q, W_o = _gather_weights(placed, items, "ag_weights")

    a1 = row(ng[0, 0] * (1.0 + sc1[0]))
    (h1,) = _norm_mod(x0, a1, row(sh1[0]), "l0_norm1")
    bcx = _mm(h1, W_cin, "nn", BF16, "l0_conv_in", b_layer=0)
    ug = _conv_gate(bcx, ck8, "l0_conv_gate")
    y1 = _mm(ug, W_cout, "nn", F32, "l0_conv_out", b_layer=0)
    gt1 = row(g1[0] * ng[0, 1])
    x1 = _post_norm(x0, y1, gt1, "l0_post1")
    a2 = row(ng[0, 2] * (1.0 + sc2[0]))
    (h2,) = _norm_mod(x1, a2, row(sh2[0]), "l0_norm2")
    gu0, act0 = _ffn_in_act(h2, W_fin, 0, "l0_ffn_in")
    y2 = _mm(act0, W_fout, "nn", F32, "l0_ffn_out", b_layer=0)
    gt2 = row(g2[0] * ng[0, 3])
    x2 = _post_norm(x1, y2, gt2, "l0_post2")
    a3 = ng[1, 0] * (1.0 + sc1[1])
    akv = kv_norm_g * (1.0 + kv_sc)
    h3, hkv = _norm_mod(x2, jnp.stack([a3, akv]), jnp.stack([sh1[1], kv_sh]), "l1_norm1")
    kvp = _mm(hkv, W_kv, "nn", BF16, "l1_kv", b_layer=0)
    att_scale = (D // N_HEADS) ** -0.5
    assert math.log2(att_scale) % 1 == 0, "scaling q before its bf16 cast is exact only for a power of two"
    qp = _mm(h3, W_q, "nn", BF16, "l1_q", b_layer=0, scale=att_scale)
    tab = _bias_table(rel_bias[0])
    oh = _attn_fwd(qp, kvp, tab, "l1_attn")
    y3 = _mm(oh, W_o, "nn", F32, "l1_attn_out", b_layer=0)
    gt3 = row(g1[1] * ng[1, 1])
    x3 = _post_norm(x2, y3, gt3, "l1_post1")
    a4 = row(ng[1, 2] * (1.0 + sc2[1]))
    (h4,) = _norm_mod(x3, a4, row(sh2[1]), "l1_norm2")
    gu1, act1 = _ffn_in_act(h4, W_fin, 1, "l1_ffn_in")
    y4 = _mm(act1, W_fout, "nn", F32, "l1_ffn_out", b_layer=1)
    gt4 = row(g2[1] * ng[1, 3])
    x4 = _post_norm(x3, y4, gt4, "l1_post2")
    dx4, sq = _loss_grad(x4, target, "loss")
    loss_part = 0.5 * jnp.sum(sq) / D

    def ffn_bwd(dxn, xin_, h, gu, act, y, gt, a, l, tag):
        dy, dgt = _post_norm_bwd(dxn, y, gt, f"{tag}_post2_bwd")
        dgu = _ffn_out_dx_act(dy, W_fout, l, gu, f"{tag}_ffn_out_dx")
        g_fout = _mm(act, dy, "tn", BF16, f"{tag}_ffn_out_dw", tm=TF)
        dh = _mm(dgu, W_fin, "nt", F32, f"{tag}_ffn_in_dx", b_layer=l)
        g_fin = _mm(h, dgu, "tn", BF16, f"{tag}_ffn_in_dw", tn=TF)
        dx, ds, db = _pre_norm_bwd(xin_, dxn, [dh], a, f"{tag}_norm2_bwd")
        return dx, _sum8(dgt), _sum8(ds)[0], _sum8(db)[0], g_fin, g_fout

    dx3, dgt4, da4, db4, G_fin1, G_fout1 = ffn_bwd(dx4, x3, h4, gu1, act1, y4, gt4, a4, 1, "l1")
    dy3, dgt3 = _post_norm_bwd(dx3, y3, gt3, "l1_post1_bwd")
    doh = _mm(dy3, W_o, "nt", BF16, "l1_attn_out_dx", b_layer=0)
    G_o = _mm(oh, dy3, "tn", BF16, "l1_attn_out_dw")
    dq, dk_parts, dv_parts, dtab = _attn_bwd(qp, kvp, tab, doh, "l1_attn_bwd")
    d_rel = _bias_table_grad(dtab)
    dkv = _kv_grad_combine(dk_parts, dv_parts, "l1_kv_grad")
    dh3 = _mm(dq, W_q, "nt", F32, "l1_q_dx", b_layer=0)
    G_q = _mm(h3, dq, "tn", BF16, "l1_q_dw")
    dhkv = _mm(dkv, W_kv, "nt", F32, "l1_kv_dx", b_layer=0)
    G_kv = _mm(hkv, dkv, "tn", BF16, "l1_kv_dw")
    dx2, ds3, db3 = _pre_norm_bwd(x2, dx3, [dh3, dhkv], jnp.stack([a3, akv]), "l1_norm1_bwd")
    ds3, db3 = _sum8(ds3), _sum8(db3)

    dx1, dgt2, da2, db2, G_fin0, G_fout0 = ffn_bwd(dx2, x1, h2, gu0, act0, y2, gt2, a2, 0, "l0")
    dy1, dgt1 = _post_norm_bwd(dx1, y1, gt1, "l0_post1_bwd")
    dug = _mm(dy1, W_cout, "nt", BF16, "l0_conv_out_dx", b_layer=0)
    G_cout = _mm(ug, dy1, "tn", BF16, "l0_conv_out_dw")
    dbcx, dck = _conv_gate_bwd(dug, bcx, ck8, "l0_conv_gate_bwd")
    dh1 = _mm(dbcx, W_cin, "nt", F32, "l0_conv_in_dx", b_layer=0)
    G_cin = _mm(h1, dbcx, "tn", BF16, "l0_conv_in_dw")
    dx0, ds1, db1 = _pre_norm_bwd(x0, dx1, [dh1], a1, "l0_norm1_bwd")
    ds1, db1 = _sum8(ds1)[0], _sum8(db1)[0]
    dgt1, dgt3 = _sum8(dgt1), _sum8(dgt3)

    def dmod_of(l, ds_a, db_a, dgt_a, ds_b, db_b, dgt_b):
        return jnp.concatenate([db_a, ds_a * ng[l, 0], dgt_a * ng[l, 1], db_b, ds_b * ng[l, 2], dgt_b * ng[l, 3]])

    dmod0 = dmod_of(0, ds1, db1, dgt1, da2, db2, dgt2)
    dmod1 = dmod_of(1, ds3[0], db3[0], dgt3, da4, db4, dgt4)
    dkvmod = jnp.concatenate([db3[1], ds3[1] * kv_norm_g])
    dng = jnp.stack([
        jnp.stack([ds1 * (1.0 + sc1[0]), dgt1 * g1[0], da2 * (1.0 + sc2[0]), dgt2 * g2[0]]),
        jnp.stack([ds3[0] * (1.0 + sc1[1]), dgt3 * g1[1], da4 * (1.0 + sc2[1]), dgt4 * g2[1]])])
    dkvng = ds3[1] * (1.0 + kv_sc)
    small = [dmod0, dmod1, dkvmod, dng.reshape(-1), dkvng, _sum8(dck).reshape(-1), d_rel.reshape(-1),
             loss_part.reshape(1)]
    sizes = [int(s.shape[0]) for s in small]
    offs = np.concatenate([[0], np.cumsum(sizes)])
    bwd_all = _gather_flat(jnp.concatenate(small), "ag_bwd_small")
    Lb = bwd_all.shape[1]
    Lp = -(-Lb // 128) * 128
    tot = _sum_rows(jnp.pad(bwd_all, ((0, 0), (0, Lp - Lb))), "sum_small")[0]
    seg = lambda i: tot[offs[i]:offs[i + 1]]
    g_mod_b = jnp.stack([seg(0), seg(1)])
    g_kv_mod_b = seg(2)
    g_norm_g = lax.dynamic_slice_in_dim(seg(3).reshape(2, 4, D), chip * dsh, dsh, axis=2)
    g_kv_norm_g = seg(4)
    g_conv_k = lax.dynamic_slice_in_dim(seg(5).reshape(1, 3, D), chip * dsh, dsh, axis=2)
    g_rel_bias = seg(6).reshape(rel_bias.shape)
    loss = seg(7)[0]

    def dmod_w(i, n, name):
        rows_ = lax.dynamic_slice_in_dim(bwd_all[:, offs[i]:offs[i + 1]], chip * n, n, axis=1)
        return _mm(sc16, jnp.pad(rows_, ((0, 8), (0, 0))), "tn", F32, name)

    g_mod_w = jnp.stack([dmod_w(0, n_mod, "mod_bwd_0"), dmod_w(1, n_mod, "mod_bwd_1")])
    g_kv_mod_w = dmod_w(2, n_kvm, "mod_bwd_kv")

    big = [G_cin, G_cout, G_fin0, G_fout0, G_kv, G_q, G_o, G_fin1, G_fout1]
    r_cin, r_cout, r_fin0, r_fout0, r_kv, r_q, r_o, r_fin1, r_fout1 = _reduce_scatter(big, items)
    grads = {
        "mod_w": g_mod_w, "mod_b": g_mod_b, "norm_g": g_norm_g,
        "ffn_w_in": jnp.stack([r_fin0, r_fin1]), "ffn_w_out": jnp.stack([r_fout0, r_fout1]),
        "conv_w_in": r_cin[None], "conv_k": g_conv_k, "conv_w_out": r_cout[None],
        "kv_mod_w": g_kv_mod_w, "kv_mod_b": g_kv_mod_b, "kv_norm_g": g_kv_norm_g, "w_kv": r_kv,
        "attn_w_q": r_q[None], "attn_w_o": r_o[None], "rel_bias": g_rel_bias,
    }
    weights = dict(mod_w=mod_w, mod_b=mod_b, norm_g=norm_g, ffn_w_in=ffn_w_in, ffn_w_out=ffn_w_out,
                   conv_w_in=conv_w_in, conv_k=conv_k, conv_w_out=conv_w_out, kv_mod_w=kv_mod_w,
                   kv_mod_b=kv_mod_b, kv_norm_g=kv_norm_g, w_kv=w_kv, attn_w_q=attn_w_q, attn_w_o=attn_w_o,
                   rel_bias=rel_bias)
    m_in = dict(mod_w=m_mod_w, mod_b=m_mod_b, norm_g=m_norm_g, ffn_w_in=m_ffn_w_in, ffn_w_out=m_ffn_w_out,
                conv_w_in=m_conv_w_in, conv_k=m_conv_k, conv_w_out=m_conv_w_out, kv_mod_w=m_kv_mod_w,
                kv_mod_b=m_kv_mod_b, kv_norm_g=m_kv_norm_g, w_kv=m_w_kv, attn_w_q=m_attn_w_q,
                attn_w_o=m_attn_w_o, rel_bias=m_rel_bias)
    v_in = dict(mod_w=v_mod_w, mod_b=v_mod_b, norm_g=v_norm_g, ffn_w_in=v_ffn_w_in, ffn_w_out=v_ffn_w_out,
                conv_w_in=v_conv_w_in, conv_k=v_conv_k, conv_w_out=v_conv_w_out, kv_mod_w=v_kv_mod_w,
                kv_mod_b=v_kv_mod_b, kv_norm_g=v_kv_norm_g, w_kv=v_w_kv, attn_w_q=v_attn_w_q,
                attn_w_o=v_attn_w_o, rel_bias=v_rel_bias)
    names = list(weights)
    g_out, d_out, m_out, v_out = [], [], [], []
    for n in names:
        g = grads[n].reshape(weights[n].shape)
        d, nm, nv = _adamw(weights[n], g, m_in[n], v_in[n], f"adamw_{n}")
        g_out.append(g)
        d_out.append(d)
        m_out.append(nm)
        v_out.append(nv)
    return (loss, dx0.reshape(x.shape), *g_out, *d_out, *m_out, *v_out)
```

```python
import functools
import math

import numpy as np
import jax
import jax.numpy as jnp
from jax import lax
from jax.experimental import pallas as pl
from jax.experimental.pallas import tpu as pltpu

CHUNK = 64
N_LEFT_CHUNKS = 8
N_HEADS = 16
MAX_REL = 2 * CHUNK
N_REL = 2 * MAX_REL + 1
EPS = 1e-6
ADAM_LR = 0.001
ADAM_B1 = 0.9
ADAM_B2 = 0.999
ADAM_EPS = 1e-08
ADAM_WD = 0.01
ADAM_STEP = 10

Q_CHUNKS = 4
BQ = Q_CHUNKS * CHUNK
N_WIN = 1 + N_LEFT_CHUNKS // Q_CHUNKS
HEADS_PER_STEP = 4
NEG = -1e30
N_DEV = 8
N_CHIP = 4

BF16 = jnp.bfloat16
F32 = jnp.float32
V7X_VMEM_LIMIT_BYTES = 56 * 1024 * 1024
MESH = pl.DeviceIdType.MESH


def _pick(n, pref, align):
    t = min(pref, n)
    t -= t % align
    while t >= align:
        if n % t == 0:
            return t
        t -= align
    return n


def _params(*sem):
    return pltpu.CompilerParams(dimension_semantics=sem, vmem_limit_bytes=V7X_VMEM_LIMIT_BYTES)


def _colsum8(v):
    r, d = v.shape
    return v.reshape(r // 8, 8, d).sum(axis=0)


_DIMS = {"nn": (((1,), (0,)), ((), ())), "nt": (((1,), (1,)), ((), ())), "tn": (((0,), (0,)), ((), ()))}


def _mm(a, b, mode, out_dtype, name, *, b_layer=None, tm=1024, tn=1024, tk=None, scale=None):
    if tk is None:
        tk = 2048 if mode == "tn" else 3072
    bs = b.shape[1:] if b_layer is not None else b.shape
    if mode == "nn":
        (M, K), (K2, N) = a.shape, bs
    elif mode == "nt":
        (M, K), (N, K2) = a.shape, bs
    else:
        (K, M), (K2, N) = a.shape, bs
    assert K == K2, (name, a.shape, b.shape)
    tm = _pick(M, tm, 128 if mode == "tn" else 16)
    tn = _pick(N, tn, 128)
    tk = _pick(K, tk, 128 if mode != "tn" else 16)
    nk = K // tk
    assert scale is None or nk == 1, name
    dims = _DIMS[mode]

    def body(a_ref, b_ref, o_ref, *acc):
        p = lax.dot_general(a_ref[...].astype(BF16), b_ref[...].astype(BF16), dims,
                            preferred_element_type=F32)
        if nk == 1:
            o_ref[...] = (p if scale is None else p * scale).astype(o_ref.dtype)
        else:
            k = pl.program_id(2)

            @pl.when(k == 0)
            def _():
                acc[0][...] = p

            @pl.when(k > 0)
            def _():
                acc[0][...] += p

            @pl.when(k == nk - 1)
            def _():
                o_ref[...] = acc[0][...].astype(o_ref.dtype)

    a_spec = (pl.BlockSpec((tk, tm), lambda i, j, k: (k, i)) if mode == "tn"
              else pl.BlockSpec((tm, tk), lambda i, j, k: (i, k)))
    if mode == "nt":
        b_blk, b_idx = (tn, tk), (lambda i, j, k: (j, k))
    else:
        b_blk, b_idx = (tk, tn), (lambda i, j, k: (k, j))
    if b_layer is not None:
        b_spec = pl.BlockSpec((None,) + b_blk, lambda i, j, k: (b_layer,) + b_idx(i, j, k))
    else:
        b_spec = pl.BlockSpec(b_blk, b_idx)
    return pl.pallas_call(
        body, name=name,
        grid=(M // tm, N // tn, nk),
        in_specs=[a_spec, b_spec],
        out_specs=pl.BlockSpec((tm, tn), lambda i, j, k: (i, j)),
        out_shape=jax.ShapeDtypeStruct((M, N), out_dtype),
        scratch_shapes=[pltpu.VMEM((tm, tn), F32)] if nk > 1 else [],
        compiler_params=_params("parallel", "parallel", "arbitrary"),
    )(a, b)


def _row_spec(tm, d):
    return pl.BlockSpec((tm, d), lambda i: (i, 0))


def _vec_spec(r, d):
    return pl.BlockSpec((r, d), lambda i: (0, 0))


def _norm_mod(x, scales, shifts, name):
    S, D = x.shape
    nb = scales.shape[0]
    tm = _pick(S, 512, 16)

    def body(x_ref, a_ref, b_ref, *o_refs):
        xv = x_ref[...]
        xh = xv * lax.rsqrt(jnp.mean(xv * xv, axis=-1, keepdims=True) + EPS)
        for n in range(nb):
            o_refs[n][...] = (xh * a_ref[n:n + 1, :] + b_ref[n:n + 1, :]).astype(BF16)

    return pl.pallas_call(
        body, name=name, grid=(S // tm,),
        in_specs=[_row_spec(tm, D), _vec_spec(nb, D), _vec_spec(nb, D)],
        out_specs=[_row_spec(tm, D)] * nb,
        out_shape=[jax.ShapeDtypeStruct((S, D), BF16)] * nb,
        compiler_params=_params("parallel"),
    )(x, scales, shifts)


def _post_norm(x, y, gate, name):
    S, D = x.shape
    tm = _pick(S, 512, 8)

    def body(x_ref, y_ref, g_ref, o_ref):
        yv = y_ref[...].astype(F32)
        yh = yv * lax.rsqrt(jnp.mean(yv * yv, axis=-1, keepdims=True) + EPS)
        o_ref[...] = x_ref[...] + yh * g_ref[...]

    return pl.pallas_call(
        body, name=name, grid=(S // tm,),
        in_specs=[_row_spec(tm, D), _row_spec(tm, D), _vec_spec(1, D)],
        out_specs=_row_spec(tm, D),
        out_shape=jax.ShapeDtypeStruct((S, D), F32),
        compiler_params=_params("parallel"),
    )(x, y, gate)


def _loss_grad(x, target, name):
    S, D = x.shape
    tm = _pick(S, 512, 8)

    def body(x_ref, t_ref, dx_ref, sq_ref):
        e = x_ref[...] - t_ref[...]
        dx_ref[...] = e / D

        @pl.when(pl.program_id(0) == 0)
        def _():
            sq_ref[...] = jnp.zeros_like(sq_ref)

        sq_ref[...] += _colsum8(e * e)

    return pl.pallas_call(
        body, name=name, grid=(S // tm,),
        in_specs=[_row_spec(tm, D), _row_spec(tm, D)],
        out_specs=[_row_spec(tm, D), _vec_spec(8, D)],
        out_shape=[jax.ShapeDtypeStruct((S, D), F32), jax.ShapeDtypeStruct((8, D), F32)],
        compiler_params=_params("arbitrary"),
    )(x, target)


def _post_norm_bwd(dxn, y, gate, name):
    S, D = y.shape
    tm = _pick(S, 512, 16)

    def body(d_ref, y_ref, g_ref, dy_ref, dg_ref):
        yv = y_ref[...].astype(F32)
        dv = d_ref[...]
        r = lax.rsqrt(jnp.mean(yv * yv, axis=-1, keepdims=True) + EPS)
        yh = yv * r
        dyh = dv * g_ref[...]
        dy_ref[...] = (r * (dyh - yh * jnp.mean(dyh * yh, axis=-1, keepdims=True))).astype(BF16)

        @pl.when(pl.program_id(0) == 0)
        def _():
            dg_ref[...] = jnp.zeros_like(dg_ref)

        dg_ref[...] += _colsum8(dv * yh)

    return pl.pallas_call(
        body, name=name, grid=(S // tm,),
        in_specs=[_row_spec(tm, D), _row_spec(tm, D), _vec_spec(1, D)],
        out_specs=[_row_spec(tm, D), _vec_spec(8, D)],
        out_shape=[jax.ShapeDtypeStruct((S, D), BF16), jax.ShapeDtypeStruct((8, D), F32)],
        compiler_params=_params("arbitrary"),
    )(dxn, y, gate)


def _pre_norm_bwd(x, dxn, dhs, scales, name):
    S, D = x.shape
    nb = len(dhs)
    tm = _pick(S, 512, 8)

    def body(x_ref, d_ref, a_ref, *rest):
        dh_refs, dx_ref, ds_ref, db_ref = rest[:nb], rest[nb], rest[nb + 1], rest[nb + 2]
        xv = x_ref[...]
        r = lax.rsqrt(jnp.mean(xv * xv, axis=-1, keepdims=True) + EPS)
        xh = xv * r

        @pl.when(pl.program_id(0) == 0)
        def _():
            ds_ref[...] = jnp.zeros_like(ds_ref)
            db_ref[...] = jnp.zeros_like(db_ref)

        dxh = jnp.zeros_like(xv)
        for n in range(nb):
            dh = dh_refs[n][...].astype(F32)
            dxh = dxh + dh * a_ref[n:n + 1, :]
            ds_ref[n] += _colsum8(dh * xh)
            db_ref[n] += _colsum8(dh)
        dx_ref[...] = d_ref[...] + r * (dxh - xh * jnp.mean(dxh * xh, axis=-1, keepdims=True))

    acc_spec = pl.BlockSpec((nb, 8, D), lambda i: (0, 0, 0))
    return pl.pallas_call(
        body, name=name, grid=(S // tm,),
        in_specs=[_row_spec(tm, D), _row_spec(tm, D), _vec_spec(nb, D)] + [_row_spec(tm, D)] * nb,
        out_specs=[_row_spec(tm, D), acc_spec, acc_spec],
        out_shape=[jax.ShapeDtypeStruct((S, D), F32), jax.ShapeDtypeStruct((nb, 8, D), F32),
                   jax.ShapeDtypeStruct((nb, 8, D), F32)],
        compiler_params=_params("arbitrary"),
    )(x, dxn, scales, *dhs)


FFN_PAIRS = 2
FFN_SUB_ROWS = 256


def _ffn_in_act(h, w, layer, name):
    S, D = h.shape
    F2 = w.shape[2]
    PW = F2 // (2 * FFN_PAIRS)
    tm = _pick(S, 512, 16)
    sub = _pick(tm, FFN_SUB_ROWS, 16)

    def body(h_ref, w_ref, gu_ref, a_ref):
        for r in range(tm // sub):
            rows = pl.ds(r * sub, sub)
            acc = jnp.dot(h_ref[rows, :], w_ref[...], preferred_element_type=F32)
            gu_ref[rows, :] = acc.astype(BF16)
            g = acc[:, :PW]
            a_ref[rows, :] = (g * jax.nn.sigmoid(g) * acc[:, PW:]).astype(BF16)

    return pl.pallas_call(
        body, name=name, grid=(FFN_PAIRS, S // tm),
        in_specs=[pl.BlockSpec((tm, D), lambda p, i: (i, 0)),
                  pl.BlockSpec((None, D, 2 * PW), lambda p, i: (layer, 0, p))],
        out_specs=[pl.BlockSpec((tm, 2 * PW), lambda p, i: (i, p)), pl.BlockSpec((tm, PW), lambda p, i: (i, p))],
        out_shape=[jax.ShapeDtypeStruct((S, F2), BF16), jax.ShapeDtypeStruct((S, F2 // 2), BF16)],
        compiler_params=_params("parallel", "parallel"),
    )(h, w)


def _ffn_out_dx_act(dy, w, layer, gu, name):
    S, D = dy.shape
    F2 = gu.shape[1]
    PW = F2 // (2 * FFN_PAIRS)
    tm = _pick(S, 512, 16)
    sub = _pick(tm, FFN_SUB_ROWS, 16)

    def body(dy_ref, w_ref, gu_ref, o_ref):
        for r in range(tm // sub):
            rows = pl.ds(r * sub, sub)
            da = lax.dot_general(dy_ref[rows, :], w_ref[...], _DIMS["nt"], preferred_element_type=F32)
            g = gu_ref[rows, 0:PW].astype(F32)
            u = gu_ref[rows, PW:2 * PW].astype(F32)
            sg = jax.nn.sigmoid(g)
            o_ref[rows, 0:PW] = (da * u * (sg * (1.0 + g * (1.0 - sg)))).astype(BF16)
            o_ref[rows, PW:2 * PW] = (da * (g * sg)).astype(BF16)

    return pl.pallas_call(
        body, name=name, grid=(FFN_PAIRS, S // tm),
        in_specs=[pl.BlockSpec((tm, D), lambda p, i: (i, 0)),
                  pl.BlockSpec((None, PW, D), lambda p, i: (layer, p, 0)),
                  pl.BlockSpec((tm, 2 * PW), lambda p, i: (i, p))],
        out_specs=pl.BlockSpec((tm, 2 * PW), lambda p, i: (i, p)),
        out_shape=jax.ShapeDtypeStruct((S, F2), BF16),
        compiler_params=_params("parallel", "parallel"),
    )(dy, w, gu)


HALO = 16


def _conv_terms(bcx_ref, prev_ref, i, tm, D):
    b = bcx_ref[:, 0:D].astype(F32)
    cg = bcx_ref[:, D:2 * D].astype(F32)
    xin = bcx_ref[:, 2 * D:3 * D].astype(F32)
    z = cg * xin
    zp = prev_ref[:, D:2 * D].astype(F32) * prev_ref[:, 2 * D:3 * D].astype(F32)
    zp = jnp.where(i > 0, zp, 0.0)
    row = lax.broadcasted_iota(jnp.int32, (tm, D), 0)
    p1, p2 = zp[HALO - 1:HALO, :], zp[HALO - 2:HALO - 1, :]
    z1 = jnp.where(row == 0, p1, pltpu.roll(z, 1, 0))
    z2 = jnp.where(row == 0, p2, jnp.where(row == 1, p1, pltpu.roll(z, 2, 0)))
    return b, cg, xin, z, z1, z2, row


def _conv_gate(bcx, ck, name):
    S, D3 = bcx.shape
    D = D3 // 3
    tm = _pick(S, 256, 16)
    hb = tm // HALO

    def body(bcx_ref, prev_ref, ck_ref, o_ref):
        i = pl.program_id(0)
        b, _, _, z, z1, z2, _ = _conv_terms(bcx_ref, prev_ref, i, tm, D)
        conv = ck_ref[0:1, :] * z2 + ck_ref[1:2, :] * z1 + ck_ref[2:3, :] * z
        o_ref[...] = (b * conv).astype(BF16)

    return pl.pallas_call(
        body, name=name, grid=(S // tm,),
        in_specs=[_row_spec(tm, D3),
                  pl.BlockSpec((HALO, D3), lambda i: (jnp.maximum(i * hb - 1, 0), 0)),
                  _vec_spec(8, D)],
        out_specs=_row_spec(tm, D),
        out_shape=jax.ShapeDtypeStruct((S, D), BF16),
        compiler_params=_params("parallel"),
    )(bcx, bcx, ck)


def _conv_gate_bwd(du, bcx, ck, name):
    S, D3 = bcx.shape
    D = D3 // 3
    tm = _pick(S, 256, 16)
    hb = tm // HALO
    nt = S // tm

    def body(du_ref, dun_ref, bcx_ref, prev_ref, next_ref, ck_ref, o_ref, dk_ref):
        i = pl.program_id(0)
        b, cg, xin, z, z1, z2, row = _conv_terms(bcx_ref, prev_ref, i, tm, D)
        k0, k1, k2 = ck_ref[0:1, :], ck_ref[1:2, :], ck_ref[2:3, :]
        conv = k0 * z2 + k1 * z1 + k2 * z
        d = du_ref[...].astype(F32)
        dconv = d * b
        dcn = jnp.where(i < nt - 1, dun_ref[...].astype(F32) * next_ref[:, 0:D].astype(F32), 0.0)
        d1 = jnp.where(row == tm - 1, dcn[0:1, :], pltpu.roll(dconv, tm - 1, 0))
        d2 = jnp.where(row == tm - 2, dcn[0:1, :],
                       jnp.where(row == tm - 1, dcn[1:2, :], pltpu.roll(dconv, tm - 2, 0)))
        dz = k2 * dconv + k1 * d1 + k0 * d2
        o_ref[:, 0:D] = (d * conv).astype(BF16)
        o_ref[:, D:2 * D] = (dz * xin).astype(BF16)
        o_ref[:, 2 * D:3 * D] = (dz * cg).astype(BF16)

        @pl.when(i == 0)
        def _():
            dk_ref[...] = jnp.zeros_like(dk_ref)

        dk_ref[0] += _colsum8(dconv * z2)
        dk_ref[1] += _colsum8(dconv * z1)
        dk_ref[2] += _colsum8(dconv * z)

    last = S // HALO - 1
    return pl.pallas_call(
        body, name=name, grid=(nt,),
        in_specs=[_row_spec(tm, D),
                  pl.BlockSpec((HALO, D), lambda i: (jnp.minimum((i + 1) * hb, last), 0)),
                  _row_spec(tm, D3),
                  pl.BlockSpec((HALO, D3), lambda i: (jnp.maximum(i * hb - 1, 0), 0)),
                  pl.BlockSpec((HALO, D3), lambda i: (jnp.minimum((i + 1) * hb, last), 0)),
                  _vec_spec(8, D)],
        out_specs=[_row_spec(tm, D3), pl.BlockSpec((3, 8, D), lambda i: (0, 0, 0))],
        out_shape=[jax.ShapeDtypeStruct((S, D3), BF16), jax.ShapeDtypeStruct((3, 8, D), F32)],
        compiler_params=_params("arbitrary"),
    )(du, du, bcx, bcx, bcx, ck)


def _rel_onehot():
    a = np.arange(CHUNK)[:, None]
    b = np.arange(CHUNK)[None, :]
    idx = np.stack([np.clip((N_LEFT_CHUNKS - dl) * CHUNK + a - b, -MAX_REL, MAX_REL) + MAX_REL
                    for dl in (6, 7, 8)]).reshape(-1)
    return (jnp.asarray(idx)[:, None] == jnp.arange(N_REL)[None, :]).astype(F32)


def _bias_table(rel_bias):
    H = rel_bias.shape[0]
    near = jnp.dot(rel_bias, _rel_onehot().T, precision=lax.Precision.HIGHEST).reshape(H, 3, CHUNK, CHUNK)
    far = jnp.broadcast_to(rel_bias[:, N_REL - 1][:, None, None], (H, CHUNK, CHUNK))
    neg = jnp.full((H, CHUNK, CHUNK), NEG, F32)

    def block(dl):
        if dl < 0 or dl > N_LEFT_CHUNKS:
            return neg
        return far if dl <= 5 else near[:, dl - 6]

    rows = [jnp.concatenate([block(jc - ic) for jc in range(N_WIN * Q_CHUNKS)], axis=-1)
            for ic in range(Q_CHUNKS)]
    return jnp.concatenate(rows, axis=-2)


def _bias_table_grad(dtab):
    H = dtab.shape[0]
    blk = lambda ic, jc: dtab[:, ic * CHUNK:(ic + 1) * CHUNK, jc * CHUNK:(jc + 1) * CHUNK]
    by_dl = [sum(blk(ic, ic + dl) for ic in range(Q_CHUNKS)) for dl in range(N_LEFT_CHUNKS + 1)]
    far = sum(jnp.sum(by_dl[dl], axis=(1, 2)) for dl in range(6))
    near = jnp.stack(by_dl[6:9], axis=1).reshape(H, 3 * CHUNK * CHUNK)
    g = jnp.dot(near, _rel_onehot(), precision=lax.Precision.HIGHEST)
    return g.at[:, N_REL - 1].add(far)


def _attn_specs(D, W):
    q_spec = pl.BlockSpec((BQ, W), lambda g, i: (i, g))

    def win(w, off):
        return pl.BlockSpec((BQ, W), lambda g, i: (jnp.maximum(i - (N_WIN - 1) + w, 0), off + g))

    k_specs = [win(w, 0) for w in range(N_WIN)]
    v_specs = [win(w, D // W) for w in range(N_WIN)]
    tab_spec = pl.BlockSpec((None, HEADS_PER_STEP, BQ, N_WIN * BQ),
                            lambda g, i: (jnp.minimum(i, N_WIN - 1), g, 0, 0))
    dtab_spec = pl.BlockSpec((HEADS_PER_STEP, BQ, N_WIN * BQ), lambda g, i: (g, 0, 0))
    return q_spec, k_specs, v_specs, tab_spec, dtab_spec


def _start_variants(tab):
    col = jnp.arange(N_WIN * BQ)
    return jnp.stack([jnp.where(col >= (N_WIN - 1 - v) * BQ, tab, NEG) for v in range(N_WIN)])


def _attn_exp(q_ref, kw, tab_ref, h, dh):
    qh = q_ref[:, h * dh:(h + 1) * dh]
    kh = kw[:, h * dh:(h + 1) * dh]
    s = lax.dot_general(qh, kh, _DIMS["nt"], preferred_element_type=F32) + tab_ref[h]
    e = jnp.exp(s - jnp.max(s, axis=-1, keepdims=True))
    return e, jnp.sum(e, axis=-1, keepdims=True), qh, kh


def _attn_fwd(q, kv, tab, name):
    S, D = q.shape
    dh = D // N_HEADS
    W = HEADS_PER_STEP * dh
    q_spec, k_specs, v_specs, tab_spec, _ = _attn_specs(D, W)

    def body(q_ref, *rest):
        k_refs, v_refs = rest[:N_WIN], rest[N_WIN:2 * N_WIN]
        tab_ref, o_ref = rest[2 * N_WIN], rest[2 * N_WIN + 1]
        kw = jnp.concatenate([r[...] for r in k_refs], axis=0)
        vw = jnp.concatenate([r[...] for r in v_refs], axis=0)
        outs = []
        for h in range(HEADS_PER_STEP):
            e, l, _, _ = _attn_exp(q_ref, kw, tab_ref, h, dh)
            outs.append(jnp.dot(e.astype(BF16), vw[:, h * dh:(h + 1) * dh], preferred_element_type=F32) / l)
        o_ref[...] = jnp.concatenate(outs, axis=1).astype(BF16)

    return pl.pallas_call(
        body, name=name, grid=(N_HEADS // HEADS_PER_STEP, S // BQ),
        in_specs=[q_spec] + k_specs + v_specs + [tab_spec],
        out_specs=q_spec,
        out_shape=jax.ShapeDtypeStruct((S, D), BF16),
        compiler_params=_params("parallel", "parallel"),
    )(q, *([kv] * (2 * N_WIN)), tab)


def _attn_bwd(q, kv, tab, do, name):
    S, D = q.shape
    dh = D // N_HEADS
    W = HEADS_PER_STEP * dh
    q_spec, k_specs, v_specs, tab_spec, dtab_spec = _attn_specs(D, W)

    def body(q_ref, *rest):
        k_refs, v_refs = rest[:N_WIN], rest[N_WIN:2 * N_WIN]
        tab_ref, do_ref, dq_ref = rest[2 * N_WIN:2 * N_WIN + 3]
        dk_refs = rest[2 * N_WIN + 3:3 * N_WIN + 3]
        dv_refs = rest[3 * N_WIN + 3:4 * N_WIN + 3]
        dtab_ref = rest[4 * N_WIN + 3]
        i = pl.program_id(1)

        @pl.when(i == 0)
        def _():
            dtab_ref[...] = jnp.zeros_like(dtab_ref)

        kw = jnp.concatenate([r[...] for r in k_refs], axis=0)
        vw = jnp.concatenate([r[...] for r in v_refs], axis=0)
        dqs, dks, dvs = [], [], []
        for h in range(HEADS_PER_STEP):
            e, l, qh, kh = _attn_exp(q_ref, kw, tab_ref, h, dh)
            p = e * (1.0 / l)
            vh = vw[:, h * dh:(h + 1) * dh]
            doh = do_ref[:, h * dh:(h + 1) * dh]
            dp = lax.dot_general(doh, vh, _DIMS["nt"], preferred_element_type=F32)
            ds = p * (dp - jnp.sum(p * dp, axis=-1, keepdims=True))
            dtab_ref[h] += ds
            dsb = ds.astype(BF16)
            dqs.append(jnp.dot(dsb, kh, preferred_element_type=F32) * (dh ** -0.5))
            dks.append(lax.dot_general(dsb, qh, _DIMS["tn"], preferred_element_type=F32))
            dvs.append(lax.dot_general(p.astype(BF16), doh, _DIMS["tn"], preferred_element_type=F32))
        dq_ref[...] = jnp.concatenate(dqs, axis=1).astype(BF16)
        dk = jnp.concatenate(dks, axis=1)
        dv = jnp.concatenate(dvs, axis=1)
        for w in range(N_WIN):
            dk_refs[w][...] = dk[w * BQ:(w + 1) * BQ, :].astype(BF16)
            dv_refs[w][...] = dv[w * BQ:(w + 1) * BQ, :].astype(BF16)

    part = jax.ShapeDtypeStruct((S, D), BF16)
    outs = pl.pallas_call(
        body, name=name, grid=(N_HEADS // HEADS_PER_STEP, S // BQ),
        in_specs=[q_spec] + k_specs + v_specs + [tab_spec, q_spec],
        out_specs=[q_spec] + [q_spec] * (2 * N_WIN) + [dtab_spec],
        out_shape=[jax.ShapeDtypeStruct((S, D), BF16)] + [part] * (2 * N_WIN)
        + [jax.ShapeDtypeStruct(tab.shape[1:], F32)],
        compiler_params=_params("parallel", "arbitrary"),
    )(q, *([kv] * (2 * N_WIN)), tab, do)
    return outs[0], outs[1:1 + N_WIN], outs[1 + N_WIN:1 + 2 * N_WIN], outs[-1]


def _kv_grad_combine(dk_parts, dv_parts, name):
    S, D = dk_parts[0].shape
    nblk = S // BQ

    def body(*refs):
        o_ref = refs[2 * N_WIN]
        i = pl.program_id(0)
        for half, parts in enumerate((refs[:N_WIN], refs[N_WIN:2 * N_WIN])):
            acc = parts[N_WIN - 1][...].astype(F32)
            for w in range(N_WIN - 1):
                acc = acc + jnp.where(i + (N_WIN - 1 - w) < nblk, parts[w][...].astype(F32), 0.0)
            o_ref[:, half * D:(half + 1) * D] = acc.astype(BF16)

    specs = [pl.BlockSpec((BQ, D), functools.partial(
        lambda i, sh: (jnp.minimum(i + sh, nblk - 1), 0), sh=N_WIN - 1 - w)) for w in range(N_WIN)]
    return pl.pallas_call(
        body, name=name, grid=(nblk,),
        in_specs=specs + specs,
        out_specs=pl.BlockSpec((BQ, 2 * D), lambda i: (i, 0)),
        out_shape=jax.ShapeDtypeStruct((S, 2 * D), BF16),
        compiler_params=_params("parallel"),
    )(*dk_parts, *dv_parts)


def _adamw(w, g, m, v, name):
    shape = w.shape
    C = shape[-1]
    R = int(np.prod(shape[:-1])) if len(shape) > 1 else 1
    w2, g2, m2, v2 = (t.reshape(R, C) for t in (w, g, m, v))
    tr = _pick(R, max(8, (512 * 1024) // C // 8 * 8), 8)

    def body(w_ref, g_ref, m_ref, v_ref, d_ref, nm_ref, nv_ref):
        gv = g_ref[...]
        nm = ADAM_B1 * m_ref[...] + (1.0 - ADAM_B1) * gv
        nv = ADAM_B2 * v_ref[...] + (1.0 - ADAM_B2) * jnp.square(gv)
        m_hat = nm / (1.0 - ADAM_B1 ** ADAM_STEP)
        v_hat = nv / (1.0 - ADAM_B2 ** ADAM_STEP)
        d_ref[...] = -ADAM_LR * (m_hat / (jnp.sqrt(v_hat) + ADAM_EPS) + ADAM_WD * w_ref[...])
        nm_ref[...] = nm
        nv_ref[...] = nv

    spec = pl.BlockSpec((tr, C), lambda i: (i, 0))
    outs = pl.pallas_call(
        body, name=name, grid=(R // tr,),
        in_specs=[spec] * 4, out_specs=[spec] * 3,
        out_shape=[jax.ShapeDtypeStruct((R, C), F32)] * 3,
        compiler_params=_params("parallel"),
    )(w2, g2, m2, v2)
    return tuple(o.reshape(shape) for o in outs)


def _sum_rows(a, name):
    n, L = a.shape

    def body(a_ref, o_ref):
        acc = a_ref[0:1, :]
        for r in range(1, n):
            acc = acc + a_ref[r:r + 1, :]
        o_ref[...] = acc

    return pl.pallas_call(
        body, name=name, grid=(1,),
        in_specs=[pl.BlockSpec((n, L), lambda i: (0, 0))],
        out_specs=pl.BlockSpec((1, L), lambda i: (0, 0)),
        out_shape=jax.ShapeDtypeStruct((1, L), F32),
        compiler_params=_params("arbitrary"),
    )(a)


def _scalar_call(body, name, scalar, grid, in_specs, out_spec, out_shape, args):
    return pl.pallas_call(
        body, name=name,
        grid_spec=pltpu.PrefetchScalarGridSpec(num_scalar_prefetch=1, grid=grid, in_specs=in_specs,
                                               out_specs=out_spec),
        out_shape=out_shape, compiler_params=_params("parallel"),
    )(jnp.reshape(scalar, (-1,)).astype(jnp.int32), *args)


def _pair_sum(view, got, c, name):
    nb, _, rh, cols = view.shape
    tr = _pick(rh, max(16, (1 << 20) // cols // 16 * 16), 16)
    bpr = rh // tr

    def body(s_ref, a_ref, b_ref, o_ref):
        o_ref[...] = (a_ref[...].astype(F32) + b_ref[...].astype(F32)).astype(BF16)

    spec = pl.BlockSpec((tr, cols), lambda i, s: (i, 0))
    mine = pl.BlockSpec((tr, cols), lambda i, s: ((2 * (i // bpr) + s[0]) * bpr + i % bpr, 0))
    return _scalar_call(body, name, c, (nb * bpr,), [mine, spec], spec,
                        jax.ShapeDtypeStruct((nb * rh, cols), BF16),
                        (view.reshape(nb * 2 * rh, cols), got.reshape(nb * rh, cols)))


def _owner_sum(pair, recv, me, c, it, name):
    _, rh, bc = recv.shape
    tr = _pick(rh, max(16, (1 << 19) // bc // 16 * 16), 16)
    bpr = rh // tr

    def body(s_ref, a_ref, r0, r1, r2, o_ref):
        o_ref[...] = ((a_ref[...].astype(F32) + r0[...].astype(F32)) + r1[...].astype(F32)) + r2[...].astype(F32)

    if it.kind == "col":
        own = pl.BlockSpec((tr, bc), lambda i, s: (i, s[0]))
    else:
        own = pl.BlockSpec((tr, bc), lambda i, s: (s[0] * bpr + i, 0))
    slots = [pl.BlockSpec((None, tr, bc), functools.partial(lambda i, s, k: (k, i, 0), k=k)) for k in range(3)]
    return _scalar_call(body, name, jnp.stack([it.pos(me), c]), (bpr,), [own] + slots,
                        pl.BlockSpec((tr, bc), lambda i, s: (s[1] * bpr + i, 0)),
                        jax.ShapeDtypeStruct((2 * rh, bc), F32), (pair, recv, recv, recv))


def _place():
    x, y, c = lax.axis_index("x"), lax.axis_index("y"), lax.axis_index("c")
    chips = [(1 - x, y), (x, 1 - y), (1 - x, 1 - y)]
    return x, y, c, chips


def _chip_index(px, py):
    return 2 * px + py


def _all_gather_small(x_shard, name):
    m_per, n = x_shard.shape

    def body(x_ref, out_ref, send_sems, recv_sems, local_sem):
        x, y, c, chips = _place()
        me, sibling = (x, y, c), (x, y, 1 - c)

        def rows(px, py, pc):
            return out_ref.at[pl.ds((4 * px + 2 * py + pc) * m_per, m_per), :]

        def copy(k, block, to, src=None):
            return pltpu.make_async_remote_copy(
                src_ref=rows(*block) if src is None else src, dst_ref=rows(*block),
                send_sem=send_sems.at[k], recv_sem=recv_sems.at[k], device_id=to, device_id_type=MESH)

        mine = pltpu.make_async_copy(x_ref, rows(*me), local_sem)
        mine.start()
        first = [copy(0, me, sibling, src=x_ref)]
        first += [copy(1 + j, me, (*chip, c), src=x_ref) for j, chip in enumerate(chips)]
        for cp in first:
            cp.start()
        passed = [copy(4 + j, (*chip, c), sibling) for j, chip in enumerate(chips)]
        for j, chip in enumerate(chips):
            copy(1 + j, (*chip, c), me).wait_recv()
            passed[j].start()
        copy(0, sibling, me).wait_recv()
        for j, chip in enumerate(chips):
            copy(4 + j, (*chip, 1 - c), me).wait_recv()
        for cp in first + passed:
            cp.wait_send()
        mine.wait()

    return pl.pallas_call(
        body, name=name,
        out_shape=jax.ShapeDtypeStruct((N_DEV * m_per, n), x_shard.dtype),
        in_specs=[pl.BlockSpec(memory_space=pltpu.VMEM)],
        out_specs=pl.BlockSpec(memory_space=pltpu.VMEM),
        scratch_shapes=[pltpu.SemaphoreType.DMA((7,)), pltpu.SemaphoreType.DMA((7,)), pltpu.SemaphoreType.DMA],
    )(x_shard)


def _gather_flat(vec, name):
    L = vec.shape[0]
    Lp = -(-L // 1024) * 1024
    g = _all_gather_small(jnp.pad(vec, (0, Lp - L)).reshape(8, Lp // 8), name)
    return g.reshape(N_DEV, Lp)[:, :L]


class _Item:
    def __init__(self, kind, rows, cols, arg, layer, swap=False):
        self.kind, self.rows, self.cols, self.arg, self.layer, self.swap = kind, rows, cols, arg, layer, swap

    def ref(self, refs):
        return refs[self.arg].at[self.layer]

    def pos(self, j):
        return 2 * (j % 2) + j // 2 if self.swap else j


def _block(ref, it, j, half):
    if it.kind == "col":
        ns = it.cols // N_CHIP
        return ref.at[pl.ds(half * (it.rows // 2), it.rows // 2), pl.ds(it.pos(j) * ns, ns)]
    rs = it.rows // N_CHIP
    return ref.at[pl.ds(j * rs + half * (rs // 2), rs // 2), :]


def _cast_place(w, kind, pos, name):
    L, r, n = w.shape
    tr = _pick(r, max(16, (1 << 20) // n // 16 * 16), 16)
    bpr = r // tr

    def body(s_ref, w_ref, o_ref):
        o_ref[...] = w_ref[...].astype(BF16)

    if kind == "col":
        full, out_idx = (L, r, N_CHIP * n), (lambda l, i, s: (l, i, s[0]))
    else:
        full, out_idx = (L, N_CHIP * r, n), (lambda l, i, s: (l, s[0] * bpr + i, 0))
    return pl.pallas_call(
        body, name=name,
        grid_spec=pltpu.PrefetchScalarGridSpec(
            num_scalar_prefetch=1, grid=(L, bpr),
            in_specs=[pl.BlockSpec((None, tr, n), lambda l, i, s: (l, i, 0))],
            out_specs=pl.BlockSpec((None, tr, n), out_idx)),
        out_shape=jax.ShapeDtypeStruct(full, BF16),
        compiler_params=_params("parallel", "parallel"),
    )(jnp.reshape(pos, (1,)).astype(jnp.int32), w)


def _gather_weights(bufs, items, name):
    n_arg, n_it = len(bufs), len(items)

    def body(*refs):
        ins, outs = refs[:n_arg], refs[n_arg:2 * n_arg]
        send, recv, fsend, frecv = refs[2 * n_arg:]
        x, y, c, chips = _place()
        me = _chip_index(x, y)
        sends, fwds = [], []
        for t, it in enumerate(items):
            for k, chip in enumerate(chips):
                cp = pltpu.make_async_remote_copy(
                    src_ref=_block(it.ref(ins), it, me, c), dst_ref=_block(it.ref(outs), it, me, c),
                    send_sem=send.at[t, k], recv_sem=recv.at[t, k], device_id=(*chip, c), device_id_type=MESH)
                cp.start()
                sends.append(cp)
        for t, it in enumerate(items):
            for k, chip in enumerate(chips):
                landed = _block(it.ref(outs), it, _chip_index(*chip), c)
                pltpu.make_async_remote_copy(
                    src_ref=landed, dst_ref=landed, send_sem=send.at[t, k], recv_sem=recv.at[t, k],
                    device_id=(*chip, c), device_id_type=MESH).wait_recv()
                cp = pltpu.make_async_remote_copy(
                    src_ref=landed, dst_ref=landed, send_sem=fsend.at[t, k], recv_sem=frecv.at[t, k],
                    device_id=(x, y, 1 - c), device_id_type=MESH)
                cp.start()
                fwds.append(cp)
        for t, it in enumerate(items):
            for k, chip in enumerate(chips):
                other = _block(it.ref(outs), it, _chip_index(*chip), 1 - c)
                pltpu.make_async_remote_copy(
                    src_ref=other, dst_ref=other, send_sem=fsend.at[t, k], recv_sem=frecv.at[t, k],
                    device_id=(x, y, 1 - c), device_id_type=MESH).wait_recv()
        for cp in sends + fwds:
            cp.wait_send()

    any_spec = pl.BlockSpec(memory_space=pl.ANY)
    return pl.pallas_call(
        body, name=name,
        out_shape=[jax.ShapeDtypeStruct(b.shape, b.dtype) for b in bufs],
        in_specs=[any_spec] * n_arg, out_specs=[any_spec] * n_arg,
        input_output_aliases={t: t for t in range(n_arg)},
        scratch_shapes=[pltpu.SemaphoreType.DMA((n_it, 3))] * 4,
    )(*bufs)


def _pair_view(g, it):
    if it.kind == "col":
        return g.reshape(1, 2, it.rows // 2, it.cols)
    return g.reshape(N_CHIP, 2, it.rows // (2 * N_CHIP), it.cols)


def _pair_exchange(views, name):
    n = len(views)

    def body(*refs):
        ins, outs, send, recv = refs[:n], refs[n:2 * n], refs[2 * n], refs[2 * n + 1]
        x, y, c, _ = _place()
        cps = []
        for t in range(n):
            cp = pltpu.make_async_remote_copy(
                src_ref=ins[t].at[:, pl.ds(1 - c, 1)], dst_ref=outs[t],
                send_sem=send.at[t], recv_sem=recv.at[t], device_id=(x, y, 1 - c), device_id_type=MESH)
            cp.start()
            cps.append(cp)
        for cp in cps:
            cp.wait()

    any_spec = pl.BlockSpec(memory_space=pl.ANY)
    return pl.pallas_call(
        body, name=name,
        out_shape=[jax.ShapeDtypeStruct((v.shape[0], 1) + v.shape[2:], v.dtype) for v in views],
        in_specs=[any_spec] * n, out_specs=[any_spec] * n,
        scratch_shapes=[pltpu.SemaphoreType.DMA((n,)), pltpu.SemaphoreType.DMA((n,))],
    )(*views)


def _owner_exchange(pairs, items, name):
    n = len(items)

    def blk(ref, it, j):
        if it.kind == "col":
            ns = it.cols // N_CHIP
            return ref.at[:, pl.ds(it.pos(j) * ns, ns)]
        return ref.at[j]

    def body(*refs):
        ins, outs, send, recv = refs[:n], refs[n:2 * n], refs[2 * n], refs[2 * n + 1]
        x, y, c, chips = _place()
        cps = []
        for t, it in enumerate(items):
            for k, chip in enumerate(chips):
                cp = pltpu.make_async_remote_copy(
                    src_ref=blk(ins[t], it, _chip_index(*chip)), dst_ref=outs[t].at[k],
                    send_sem=send.at[t, k], recv_sem=recv.at[t, k], device_id=(*chip, c), device_id_type=MESH)
                cp.start()
                cps.append(cp)
        for cp in cps:
            cp.wait()

    def slot_shape(it):
        if it.kind == "col":
            return (3, it.rows // 2, it.cols // N_CHIP)
        return (3, it.rows // (2 * N_CHIP), it.cols)

    any_spec = pl.BlockSpec(memory_space=pl.ANY)
    return pl.pallas_call(
        body, name=name,
        out_shape=[jax.ShapeDtypeStruct(slot_shape(it), BF16) for it in items],
        in_specs=[any_spec] * n, out_specs=[any_spec] * n,
        scratch_shapes=[pltpu.SemaphoreType.DMA((n, 3)), pltpu.SemaphoreType.DMA((n, 3))],
    )(*pairs)


def _half_exchange(bufs, name):
    n = len(bufs)

    def body(*refs):
        ins, outs, send, recv = refs[:n], refs[n:2 * n], refs[2 * n], refs[2 * n + 1]
        x, y, c, _ = _place()
        cps = []
        for t in range(n):
            r2 = ins[t].shape[0] // 2
            cp = pltpu.make_async_remote_copy(
                src_ref=ins[t].at[pl.ds(c * r2, r2), :], dst_ref=outs[t].at[pl.ds(c * r2, r2), :],
                send_sem=send.at[t], recv_sem=recv.at[t], device_id=(x, y, 1 - c), device_id_type=MESH)
            cp.start()
            cps.append(cp)
        for t in range(n):
            r2 = ins[t].shape[0] // 2
            theirs = outs[t].at[pl.ds((1 - c) * r2, r2), :]
            pltpu.make_async_remote_copy(
                src_ref=theirs, dst_ref=theirs, send_sem=send.at[t], recv_sem=recv.at[t],
                device_id=(x, y, 1 - c), device_id_type=MESH).wait_recv()
        for cp in cps:
            cp.wait_send()

    any_spec = pl.BlockSpec(memory_space=pl.ANY)
    return pl.pallas_call(
        body, name=name,
        out_shape=[jax.ShapeDtypeStruct(b.shape, b.dtype) for b in bufs],
        in_specs=[any_spec] * n, out_specs=[any_spec] * n,
        input_output_aliases={t: t for t in range(n)},
        scratch_shapes=[pltpu.SemaphoreType.DMA((n,)), pltpu.SemaphoreType.DMA((n,))],
    )(*bufs)


def _reduce_scatter(grads, items):
    x, y, c, _ = _place()
    me = _chip_index(x, y)
    views = [_pair_view(g, it) for g, it in zip(grads, items)]
    got = _pair_exchange(views, "rs_pair_exchange")
    pairs = [_pair_sum(v, r, c, f"rs_pair_sum_{t}") for t, (v, r) in enumerate(zip(views, got))]
    shaped = [p if it.kind == "col" else p.reshape(N_CHIP, p.shape[0] // N_CHIP, p.shape[1])
              for p, it in zip(pairs, items)]
    recv = _owner_exchange(shaped, items, "rs_owner_exchange")
    halves = [_owner_sum(p, r, me, c, it, f"rs_owner_sum_{t}") for t, (p, r, it) in enumerate(zip(pairs, recv, items))]
    return _half_exchange(halves, "rs_half_exchange")


def _silu(v):
    return v * jax.nn.sigmoid(v)


def _sum8(p):
    return jnp.sum(p, axis=-2)


def kernel(x, c, mod_w, mod_b, norm_g, ffn_w_in, ffn_w_out, conv_w_in, conv_k, conv_w_out, kv_mod_w, kv_mod_b, kv_norm_g, w_kv, attn_w_q, attn_w_o, rel_bias, loss_target, m_mod_w, m_mod_b, m_norm_g, m_ffn_w_in, m_ffn_w_out, m_conv_w_in, m_conv_k, m_conv_w_out, m_kv_mod_w, m_kv_mod_b, m_kv_norm_g, m_w_kv, m_attn_w_q, m_attn_w_o, m_rel_bias, v_mod_w, v_mod_b, v_norm_g, v_ffn_w_in, v_ffn_w_out, v_conv_w_in, v_conv_k, v_conv_w_out, v_kv_mod_w, v_kv_mod_b, v_kv_norm_g, v_w_kv, v_attn_w_q, v_attn_w_o, v_rel_bias):
    xi, yi, ci = lax.axis_index("x"), lax.axis_index("y"), lax.axis_index("c")
    chip = 2 * xi + yi
    dev = 2 * chip + ci
    _, S, D = x.shape
    F = ffn_w_out.shape[1] * N_CHIP
    x0 = x.reshape(S, D)
    target = loss_target.reshape(S, D)
    n_mod = mod_w.shape[2]
    n_kvm = kv_mod_w.shape[1]
    dsh = D // N_CHIP
    TF = F // 2

    c_all = _all_gather_small(c.reshape(8, D // 8), "ag_c").reshape(N_DEV, D)
    sc16 = jnp.pad(_silu(c_all), ((0, 8), (0, 0)))
    part = [_mm(sc16, mod_w, "nn", F32, f"mod_fwd_{l}", b_layer=l)[:8] for l in range(2)]
    part.append(_mm(sc16, kv_mod_w, "nn", F32, "mod_fwd_kv")[:8])
    fwd_vec = jnp.concatenate([p.reshape(-1) for p in part] + [norm_g.reshape(-1), conv_k.reshape(-1)])
    fwd_all = _gather_flat(fwd_vec, "ag_fwd_small")[0::2]
    o = 0
    mods = []
    for n in (n_mod, n_mod, n_kvm):
        blk = fwd_all[:, o:o + 8 * n].reshape(N_CHIP, 8, n)
        mods.append(lax.dynamic_index_in_dim(blk, dev, axis=1, keepdims=False).reshape(N_CHIP * n))
        o += 8 * n
    ng = fwd_all[:, o:o + 8 * dsh].reshape(N_CHIP, 2, 4, dsh).transpose(1, 2, 0, 3).reshape(2, 4, D)
    o += 8 * dsh
    ck = fwd_all[:, o:o + 3 * dsh].reshape(N_CHIP, 3, dsh).transpose(1, 0, 2).reshape(3, D)
    ck8 = jnp.pad(ck, ((0, 5), (0, 0)))
    mod = [mods[l] + mod_b[l] for l in range(2)]
    sh1, sc1, g1, sh2, sc2, g2 = zip(*[jnp.split(m, 6) for m in mod])
    kv_sh, kv_sc = jnp.split(mods[2] + kv_mod_b, 2)
    row = lambda v: v.reshape(1, D)

    items = [_Item("col", D, 3 * D, 0, 0), _Item("row", D, D, 1, 0),
             _Item("col", D, 2 * F, 2, 0, swap=True), _Item("row", F, D, 3, 0),
             _Item("col", D, 2 * D, 4, 0), _Item("row", D, D, 5, 0), _Item("row", D, D, 6, 0),
             _Item("col", D, 2 * F, 2, 1, swap=True), _Item("row", F, D, 3, 1)]
    placed = [_cast_place(w, kind, _Item(kind, 0, 0, 0, 0, swap).pos(chip), f"place_{nm}")
              for w, kind, swap, nm in (
                  (conv_w_in, "col", False, "conv_w_in"), (conv_w_out, "row", False, "conv_w_out"),
                  (ffn_w_in, "col", True, "ffn_w_in"), (ffn_w_out, "row", False, "ffn_w_out"),
                  (w_kv[None], "col", False, "w_kv"), (attn_w_q, "row", False, "attn_w_q"),
                  (attn_w_o, "row", False, "attn_w_o"))]
    W_cin, W_cout, W_fin, W_fout, W_kv, W_q, W_o = _gather_weights(placed, items, "ag_weights")

    a1 = row(ng[0, 0] * (1.0 + sc1[0]))
    (h1,) = _norm_mod(x0, a1, row(sh1[0]), "l0_norm1")
    bcx = _mm(h1, W_cin, "nn", BF16, "l0_conv_in", b_layer=0)
    ug = _conv_gate(bcx, ck8, "l0_conv_gate")
    y1 = _mm(ug, W_cout, "nn", BF16, "l0_conv_out", b_layer=0)
    gt1 = row(g1[0] * ng[0, 1])
    x1 = _post_norm(x0, y1, gt1, "l0_post1")
    a2 = row(ng[0, 2] * (1.0 + sc2[0]))
    (h2,) = _norm_mod(x1, a2, row(sh2[0]), "l0_norm2")
    gu0, act0 = _ffn_in_act(h2, W_fin, 0, "l0_ffn_in")
    y2 = _mm(act0, W_fout, "nn", BF16, "l0_ffn_out", b_layer=0)
    gt2 = row(g2[0] * ng[0, 3])
    x2 = _post_norm(x1, y2, gt2, "l0_post2")
    a3 = ng[1, 0] * (1.0 + sc1[1])
    akv = kv_norm_g * (1.0 + kv_sc)
    h3, hkv = _norm_mod(x2, jnp.stack([a3, akv]), jnp.stack([sh1[1], kv_sh]), "l1_norm1")
    kvp = _mm(hkv, W_kv, "nn", BF16, "l1_kv", b_layer=0)
    att_scale = (D // N_HEADS) ** -0.5
    assert math.log2(att_scale) % 1 == 0, "scaling q before its bf16 cast is exact only for a power of two"
    qp = _mm(h3, W_q, "nn", BF16, "l1_q", b_layer=0, scale=att_scale)
    tab = _start_variants(_bias_table(rel_bias[0]))
    oh = _attn_fwd(qp, kvp, tab, "l1_attn")
    y3 = _mm(oh, W_o, "nn", BF16, "l1_attn_out", b_layer=0)
    gt3 = row(g1[1] * ng[1, 1])
    x3 = _post_norm(x2, y3, gt3, "l1_post1")
    a4 = row(ng[1, 2] * (1.0 + sc2[1]))
    (h4,) = _norm_mod(x3, a4, row(sh2[1]), "l1_norm2")
    gu1, act1 = _ffn_in_act(h4, W_fin, 1, "l1_ffn_in")
    y4 = _mm(act1, W_fout, "nn", BF16, "l1_ffn_out", b_layer=1)
    gt4 = row(g2[1] * ng[1, 3])
    x4 = _post_norm(x3, y4, gt4, "l1_post2")
    dx4, sq = _loss_grad(x4, target, "loss")
    loss_part = 0.5 * jnp.sum(sq) / D

    def ffn_bwd(dxn, xin_, h, gu, act, y, gt, a, l, tag):
        dy, dgt = _post_norm_bwd(dxn, y, gt, f"{tag}_post2_bwd")
        dgu = _ffn_out_dx_act(dy, W_fout, l, gu, f"{tag}_ffn_out_dx")
        g_fout = _mm(act, dy, "tn", BF16, f"{tag}_ffn_out_dw", tm=TF)
        dh = _mm(dgu, W_fin, "nt", BF16, f"{tag}_ffn_in_dx", b_layer=l)
        g_fin = _mm(h, dgu, "tn", BF16, f"{tag}_ffn_in_dw", tn=TF)
        dx, ds, db = _pre_norm_bwd(xin_, dxn, [dh], a, f"{tag}_norm2_bwd")
        return dx, _sum8(dgt), _sum8(ds)[0], _sum8(db)[0], g_fin, g_fout

    dx3, dgt4, da4, db4, G_fin1, G_fout1 = ffn_bwd(dx4, x3, h4, gu1, act1, y4, gt4, a4, 1, "l1")
    dy3, dgt3 = _post_norm_bwd(dx3, y3, gt3, "l1_post1_bwd")
    doh = _mm(dy3, W_o, "nt", BF16, "l1_attn_out_dx", b_layer=0)
    G_o = _mm(oh, dy3, "tn", BF16, "l1_attn_out_dw")
    dq, dk_parts, dv_parts, dtab = _attn_bwd(qp, kvp, tab, doh, "l1_attn_bwd")
    d_rel = _bias_table_grad(dtab)
    dkv = _kv_grad_combine(dk_parts, dv_parts, "l1_kv_grad")
    dh3 = _mm(dq, W_q, "nt", BF16, "l1_q_dx", b_layer=0)
    G_q = _mm(h3, dq, "tn", BF16, "l1_q_dw")
    dhkv = _mm(dkv, W_kv, "nt", BF16, "l1_kv_dx", b_layer=0)
    G_kv = _mm(hkv, dkv, "tn", BF16, "l1_kv_dw")
    dx2, ds3, db3 = _pre_norm_bwd(x2, dx3, [dh3, dhkv], jnp.stack([a3, akv]), "l1_norm1_bwd")
    ds3, db3 = _sum8(ds3), _sum8(db3)

    dx1, dgt2, da2, db2, G_fin0, G_fout0 = ffn_bwd(dx2, x1, h2, gu0, act0, y2, gt2, a2, 0, "l0")
    dy1, dgt1 = _post_norm_bwd(dx1, y1, gt1, "l0_post1_bwd")
    dug = _mm(dy1, W_cout, "nt", BF16, "l0_conv_out_dx", b_layer=0)
    G_cout = _mm(ug, dy1, "tn", BF16, "l0_conv_out_dw")
    dbcx, dck = _conv_gate_bwd(dug, bcx, ck8, "l0_conv_gate_bwd")
    dh1 = _mm(dbcx, W_cin, "nt", BF16, "l0_conv_in_dx", b_layer=0)
    G_cin = _mm(h1, dbcx, "tn", BF16, "l0_conv_in_dw")
    dx0, ds1, db1 = _pre_norm_bwd(x0, dx1, [dh1], a1, "l0_norm1_bwd")
    ds1, db1 = _sum8(ds1)[0], _sum8(db1)[0]
    dgt1, dgt3 = _sum8(dgt1), _sum8(dgt3)

    def dmod_of(l, ds_a, db_a, dgt_a, ds_b, db_b, dgt_b):
        return jnp.concatenate([db_a, ds_a * ng[l, 0], dgt_a * ng[l, 1], db_b, ds_b * ng[l, 2], dgt_b * ng[l, 3]])

    dmod0 = dmod_of(0, ds1, db1, dgt1, da2, db2, dgt2)
    dmod1 = dmod_of(1, ds3[0], db3[0], dgt3, da4, db4, dgt4)
    dkvmod = jnp.concatenate([db3[1], ds3[1] * kv_norm_g])
    dng = jnp.stack([
        jnp.stack([ds1 * (1.0 + sc1[0]), dgt1 * g1[0], da2 * (1.0 + sc2[0]), dgt2 * g2[0]]),
        jnp.stack([ds3[0] * (1.0 + sc1[1]), dgt3 * g1[1], da4 * (1.0 + sc2[1]), dgt4 * g2[1]])])
    dkvng = ds3[1] * (1.0 + kv_sc)
    small = [dmod0, dmod1, dkvmod, dng.reshape(-1), dkvng, _sum8(dck).reshape(-1), d_rel.reshape(-1),
             loss_part.reshape(1)]
    sizes = [int(s.shape[0]) for s in small]
    offs = np.concatenate([[0], np.cumsum(sizes)])
    bwd_all = _gather_flat(jnp.concatenate(small), "ag_bwd_small")
    Lb = bwd_all.shape[1]
    Lp = -(-Lb // 128) * 128
    tot = _sum_rows(jnp.pad(bwd_all, ((0, 0), (0, Lp - Lb))), "sum_small")[0]
    seg = lambda i: tot[offs[i]:offs[i + 1]]
    g_mod_b = jnp.stack([seg(0), seg(1)])
    g_kv_mod_b = seg(2)
    g_norm_g = lax.dynamic_slice_in_dim(seg(3).reshape(2, 4, D), chip * dsh, dsh, axis=2)
    g_kv_norm_g = seg(4)
    g_conv_k = lax.dynamic_slice_in_dim(seg(5).reshape(1, 3, D), chip * dsh, dsh, axis=2)
    g_rel_bias = seg(6).reshape(rel_bias.shape)
    loss = seg(7)[0]

    def dmod_w(i, n, name):
        rows_ = lax.dynamic_slice_in_dim(bwd_all[:, offs[i]:offs[i + 1]], chip * n, n, axis=1)
        return _mm(sc16, jnp.pad(rows_, ((0, 8), (0, 0))), "tn", F32, name)

    g_mod_w = jnp.stack([dmod_w(0, n_mod, "mod_bwd_0"), dmod_w(1, n_mod, "mod_bwd_1")])
    g_kv_mod_w = dmod_w(2, n_kvm, "mod_bwd_kv")

    big = [G_cin, G_cout, G_fin0, G_fout0, G_kv, G_q, G_o, G_fin1, G_fout1]
    r_cin, r_cout, r_fin0, r_fout0, r_kv, r_q, r_o, r_fin1, r_fout1 = _reduce_scatter(big, items)
    grads = {
        "mod_w": g_mod_w, "mod_b": g_mod_b, "norm_g": g_norm_g,
        "ffn_w_in": jnp.stack([r_fin0, r_fin1]), "ffn_w_out": jnp.stack([r_fout0, r_fout1]),
        "conv_w_in": r_cin[None], "conv_k": g_conv_k, "conv_w_out": r_cout[None],
        "kv_mod_w": g_kv_mod_w, "kv_mod_b": g_kv_mod_b, "kv_norm_g": g_kv_norm_g, "w_kv": r_kv,
        "attn_w_q": r_q[None], "attn_w_o": r_o[None], "rel_bias": g_rel_bias,
    }
    weights = dict(mod_w=mod_w, mod_b=mod_b, norm_g=norm_g, ffn_w_in=ffn_w_in, ffn_w_out=ffn_w_out,
                   conv_w_in=conv_w_in, conv_k=conv_k, conv_w_out=conv_w_out, kv_mod_w=kv_mod_w,
                   kv_mod_b=kv_mod_b, kv_norm_g=kv_norm_g, w_kv=w_kv, attn_w_q=attn_w_q, attn_w_o=attn_w_o,
                   rel_bias=rel_bias)
    m_in = dict(mod_w=m_mod_w, mod_b=m_mod_b, norm_g=m_norm_g, ffn_w_in=m_ffn_w_in, ffn_w_out=m_ffn_w_out,
                conv_w_in=m_conv_w_in, conv_k=m_conv_k, conv_w_out=m_conv_w_out, kv_mod_w=m_kv_mod_w,
                kv_mod_b=m_kv_mod_b, kv_norm_g=m_kv_norm_g, w_kv=m_w_kv, attn_w_q=m_attn_w_q,
                attn_w_o=m_attn_w_o, rel_bias=m_rel_bias)
    v_in = dict(mod_w=v_mod_w, mod_b=v_mod_b, norm_g=v_norm_g, ffn_w_in=v_ffn_w_in, ffn_w_out=v_ffn_w_out,
                conv_w_in=v_conv_w_in, conv_k=v_conv_k, conv_w_out=v_conv_w_out, kv_mod_w=v_kv_mod_w,
                kv_mod_b=v_kv_mod_b, kv_norm_g=v_kv_norm_g, w_kv=v_w_kv, attn_w_q=v_attn_w_q,
                attn_w_o=v_attn_w_o, rel_bias=v_rel_bias)
    names = list(weights)
    g_out, d_out, m_out, v_out = [], [], [], []
    for n in names:
        g = grads[n].reshape(weights[n].shape)
        d, nm, nv = _adamw(weights[n], g, m_in[n], v_in[n], f"adamw_{n}")
        g_out.append(g)
        d_out.append(d)
        m_out.append(nm)
        v_out.append(nv)
    return (loss, dx0.reshape(x.shape), *g_out, *d_out, *m_out, *v_out)
```

```python
import functools
import math

import numpy as np
import jax
import jax.numpy as jnp
from jax import lax
from jax.experimental import pallas as pl
from jax.experimental.pallas import tpu as pltpu

CHUNK = 64
N_LEFT_CHUNKS = 8
N_HEADS = 16
MAX_REL = 2 * CHUNK
N_REL = 2 * MAX_REL + 1
EPS = 1e-6
ADAM_LR = 0.001
ADAM_B1 = 0.9
ADAM_B2 = 0.999
ADAM_EPS = 1e-08
ADAM_WD = 0.01
ADAM_STEP = 10

Q_CHUNKS = 4
BQ = Q_CHUNKS * CHUNK
N_WIN = 1 + N_LEFT_CHUNKS // Q_CHUNKS
HEADS_PER_STEP = 8
NEG = -1e30
N_DEV = 8
N_CHIP = 4

BF16 = jnp.bfloat16
F32 = jnp.float32
V7X_VMEM_LIMIT_BYTES = 56 * 1024 * 1024
MESH = pl.DeviceIdType.MESH


def _pick(n, pref, align):
    t = min(pref, n)
    t -= t % align
    while t >= align:
        if n % t == 0:
            return t
        t -= align
    return n


def _params(*sem):
    return pltpu.CompilerParams(dimension_semantics=sem, vmem_limit_bytes=V7X_VMEM_LIMIT_BYTES)


def _colsum8(v):
    r, d = v.shape
    return v.reshape(r // 8, 8, d).sum(axis=0)


_DIMS = {"nn": (((1,), (0,)), ((), ())), "nt": (((1,), (1,)), ((), ())), "tn": (((0,), (0,)), ((), ()))}


def _mm(a, b, mode, out_dtype, name, *, b_layer=None, tm=1024, tn=1024, tk=None, scale=None):
    if tk is None:
        tk = 2048 if mode == "tn" else 3072
    bs = b.shape[1:] if b_layer is not None else b.shape
    if mode == "nn":
        (M, K), (K2, N) = a.shape, bs
    elif mode == "nt":
        (M, K), (N, K2) = a.shape, bs
    else:
        (K, M), (K2, N) = a.shape, bs
    assert K == K2, (name, a.shape, b.shape)
    tm = _pick(M, tm, 128 if mode == "tn" else 16)
    tn = _pick(N, tn, 128)
    tk = _pick(K, tk, 128 if mode != "tn" else 16)
    nk = K // tk
    assert scale is None or nk == 1, name
    dims = _DIMS[mode]

    def body(a_ref, b_ref, o_ref, *acc):
        p = lax.dot_general(a_ref[...].astype(BF16), b_ref[...].astype(BF16), dims,
                            preferred_element_type=F32)
        if nk == 1:
            o_ref[...] = (p if scale is None else p * scale).astype(o_ref.dtype)
        else:
            k = pl.program_id(2)

            @pl.when(k == 0)
            def _():
                acc[0][...] = p

            @pl.when(k > 0)
            def _():
                acc[0][...] += p

            @pl.when(k == nk - 1)
            def _():
                o_ref[...] = acc[0][...].astype(o_ref.dtype)

    a_spec = (pl.BlockSpec((tk, tm), lambda i, j, k: (k, i)) if mode == "tn"
              else pl.BlockSpec((tm, tk), lambda i, j, k: (i, k)))
    if mode == "nt":
        b_blk, b_idx = (tn, tk), (lambda i, j, k: (j, k))
    else:
        b_blk, b_idx = (tk, tn), (lambda i, j, k: (k, j))
    if b_layer is not None:
        b_spec = pl.BlockSpec((None,) + b_blk, lambda i, j, k: (b_layer,) + b_idx(i, j, k))
    else:
        b_spec = pl.BlockSpec(b_blk, b_idx)
    return pl.pallas_call(
        body, name=name,
        grid=(M // tm, N // tn, nk),
        in_specs=[a_spec, b_spec],
        out_specs=pl.BlockSpec((tm, tn), lambda i, j, k: (i, j)),
        out_shape=jax.ShapeDtypeStruct((M, N), out_dtype),
        scratch_shapes=[pltpu.VMEM((tm, tn), F32)] if nk > 1 else [],
        compiler_params=_params("parallel", "parallel", "arbitrary"),
    )(a, b)


def _row_spec(tm, d):
    return pl.BlockSpec((tm, d), lambda i: (i, 0))


def _vec_spec(r, d):
    return pl.BlockSpec((r, d), lambda i: (0, 0))


def _norm_mod(x, scales, shifts, name):
    S, D = x.shape
    nb = scales.shape[0]
    tm = _pick(S, 512, 16)

    def body(x_ref, a_ref, b_ref, *o_refs):
        xv = x_ref[...]
        xh = xv * lax.rsqrt(jnp.mean(xv * xv, axis=-1, keepdims=True) + EPS)
        for n in range(nb):
            o_refs[n][...] = (xh * a_ref[n:n + 1, :] + b_ref[n:n + 1, :]).astype(BF16)

    return pl.pallas_call(
        body, name=name, grid=(S // tm,),
        in_specs=[_row_spec(tm, D), _vec_spec(nb, D), _vec_spec(nb, D)],
        out_specs=[_row_spec(tm, D)] * nb,
        out_shape=[jax.ShapeDtypeStruct((S, D), BF16)] * nb,
        compiler_params=_params("parallel"),
    )(x, scales, shifts)


def _post_norm(x, y, gate, name):
    S, D = x.shape
    tm = _pick(S, 512, 8)

    def body(x_ref, y_ref, g_ref, o_ref):
        yv = y_ref[...].astype(F32)
        yh = yv * lax.rsqrt(jnp.mean(yv * yv, axis=-1, keepdims=True) + EPS)
        o_ref[...] = x_ref[...] + yh * g_ref[...]

    return pl.pallas_call(
        body, name=name, grid=(S // tm,),
        in_specs=[_row_spec(tm, D), _row_spec(tm, D), _vec_spec(1, D)],
        out_specs=_row_spec(tm, D),
        out_shape=jax.ShapeDtypeStruct((S, D), F32),
        compiler_params=_params("parallel"),
    )(x, y, gate)


def _loss_grad(x, target, name):
    S, D = x.shape
    tm = _pick(S, 512, 8)

    def body(x_ref, t_ref, dx_ref, sq_ref):
        e = x_ref[...] - t_ref[...]
        dx_ref[...] = e / D

        @pl.when(pl.program_id(0) == 0)
        def _():
            sq_ref[...] = jnp.zeros_like(sq_ref)

        sq_ref[...] += _colsum8(e * e)

    return pl.pallas_call(
        body, name=name, grid=(S // tm,),
        in_specs=[_row_spec(tm, D), _row_spec(tm, D)],
        out_specs=[_row_spec(tm, D), _vec_spec(8, D)],
        out_shape=[jax.ShapeDtypeStruct((S, D), F32), jax.ShapeDtypeStruct((8, D), F32)],
        compiler_params=_params("arbitrary"),
    )(x, target)


def _post_norm_bwd(dxn, y, gate, name):
    S, D = y.shape
    tm = _pick(S, 512, 16)

    def body(d_ref, y_ref, g_ref, dy_ref, dg_ref):
        yv = y_ref[...].astype(F32)
        dv = d_ref[...]
        r = lax.rsqrt(jnp.mean(yv * yv, axis=-1, keepdims=True) + EPS)
        yh = yv * r
        dyh = dv * g_ref[...]
        dy_ref[...] = (r * (dyh - yh * jnp.mean(dyh * yh, axis=-1, keepdims=True))).astype(BF16)

        @pl.when(pl.program_id(0) == 0)
        def _():
            dg_ref[...] = jnp.zeros_like(dg_ref)

        dg_ref[...] += _colsum8(dv * yh)

    return pl.pallas_call(
        body, name=name, grid=(S // tm,),
        in_specs=[_row_spec(tm, D), _row_spec(tm, D), _vec_spec(1, D)],
        out_specs=[_row_spec(tm, D), _vec_spec(8, D)],
        out_shape=[jax.ShapeDtypeStruct((S, D), BF16), jax.ShapeDtypeStruct((8, D), F32)],
        compiler_params=_params("arbitrary"),
    )(dxn, y, gate)


def _pre_norm_bwd(x, dxn, dhs, scales, name):
    S, D = x.shape
    nb = len(dhs)
    tm = _pick(S, 512, 8)

    def body(x_ref, d_ref, a_ref, *rest):
        dh_refs, dx_ref, ds_ref, db_ref = rest[:nb], rest[nb], rest[nb + 1], rest[nb + 2]
        xv = x_ref[...]
        r = lax.rsqrt(jnp.mean(xv * xv, axis=-1, keepdims=True) + EPS)
        xh = xv * r

        @pl.when(pl.program_id(0) == 0)
        def _():
            ds_ref[...] = jnp.zeros_like(ds_ref)
            db_ref[...] = jnp.zeros_like(db_ref)

        dxh = jnp.zeros_like(xv)
        for n in range(nb):
            dh = dh_refs[n][...].astype(F32)
            dxh = dxh + dh * a_ref[n:n + 1, :]
            ds_ref[n] += _colsum8(dh * xh)
            db_ref[n] += _colsum8(dh)
        dx_ref[...] = d_ref[...] + r * (dxh - xh * jnp.mean(dxh * xh, axis=-1, keepdims=True))

    acc_spec = pl.BlockSpec((nb, 8, D), lambda i: (0, 0, 0))
    return pl.pallas_call(
        body, name=name, grid=(S // tm,),
        in_specs=[_row_spec(tm, D), _row_spec(tm, D), _vec_spec(nb, D)] + [_row_spec(tm, D)] * nb,
        out_specs=[_row_spec(tm, D), acc_spec, acc_spec],
        out_shape=[jax.ShapeDtypeStruct((S, D), F32), jax.ShapeDtypeStruct((nb, 8, D), F32),
                   jax.ShapeDtypeStruct((nb, 8, D), F32)],
        compiler_params=_params("arbitrary"),
    )(x, dxn, scales, *dhs)


FFN_PAIRS = 2
FFN_SUB_ROWS = 256


def _ffn_in_act(h, w, layer, name):
    S, D = h.shape
    F2 = w.shape[2]
    PW = F2 // (2 * FFN_PAIRS)
    tm = _pick(S, 512, 16)
    sub = _pick(tm, FFN_SUB_ROWS, 16)

    def body(h_ref, w_ref, gu_ref, a_ref):
        for r in range(tm // sub):
            rows = pl.ds(r * sub, sub)
            acc = jnp.dot(h_ref[rows, :], w_ref[...], preferred_element_type=F32)
            gu_ref[rows, :] = acc.astype(BF16)
            g = acc[:, :PW]
            a_ref[rows, :] = (g * jax.nn.sigmoid(g) * acc[:, PW:]).astype(BF16)

    return pl.pallas_call(
        body, name=name, grid=(FFN_PAIRS, S // tm),
        in_specs=[pl.BlockSpec((tm, D), lambda p, i: (i, 0)),
                  pl.BlockSpec((None, D, 2 * PW), lambda p, i: (layer, 0, p))],
        out_specs=[pl.BlockSpec((tm, 2 * PW), lambda p, i: (i, p)), pl.BlockSpec((tm, PW), lambda p, i: (i, p))],
        out_shape=[jax.ShapeDtypeStruct((S, F2), BF16), jax.ShapeDtypeStruct((S, F2 // 2), BF16)],
        compiler_params=_params("parallel", "parallel"),
    )(h, w)


def _ffn_out_dx_act(dy, w, layer, gu, name):
    S, D = dy.shape
    F2 = gu.shape[1]
    PW = F2 // (2 * FFN_PAIRS)
    tm = _pick(S, 512, 16)
    sub = _pick(tm, FFN_SUB_ROWS, 16)

    def body(dy_ref, w_ref, gu_ref, o_ref):
        for r in range(tm // sub):
            rows = pl.ds(r * sub, sub)
            da = lax.dot_general(dy_ref[rows, :], w_ref[...], _DIMS["nt"], preferred_element_type=F32)
            g = gu_ref[rows, 0:PW].astype(F32)
            u = gu_ref[rows, PW:2 * PW].astype(F32)
            sg = jax.nn.sigmoid(g)
            o_ref[rows, 0:PW] = (da * u * (sg * (1.0 + g * (1.0 - sg)))).astype(BF16)
            o_ref[rows, PW:2 * PW] = (da * (g * sg)).astype(BF16)

    return pl.pallas_call(
        body, name=name, grid=(FFN_PAIRS, S // tm),
        in_specs=[pl.BlockSpec((tm, D), lambda p, i: (i, 0)),
                  pl.BlockSpec((None, PW, D), lambda p, i: (layer, p, 0)),
                  pl.BlockSpec((tm, 2 * PW), lambda p, i: (i, p))],
        out_specs=pl.BlockSpec((tm, 2 * PW), lambda p, i: (i, p)),
        out_shape=jax.ShapeDtypeStruct((S, F2), BF16),
        compiler_params=_params("parallel", "parallel"),
    )(dy, w, gu)


HALO = 16


def _conv_terms(bcx_ref, prev_ref, i, tm, D):
    b = bcx_ref[:, 0:D].astype(F32)
    cg = bcx_ref[:, D:2 * D].astype(F32)
    xin = bcx_ref[:, 2 * D:3 * D].astype(F32)
    z = cg * xin
    zp = prev_ref[:, D:2 * D].astype(F32) * prev_ref[:, 2 * D:3 * D].astype(F32)
    zp = jnp.where(i > 0, zp, 0.0)
    row = lax.broadcasted_iota(jnp.int32, (tm, D), 0)
    p1, p2 = zp[HALO - 1:HALO, :], zp[HALO - 2:HALO - 1, :]
    z1 = jnp.where(row == 0, p1, pltpu.roll(z, 1, 0))
    z2 = jnp.where(row == 0, p2, jnp.where(row == 1, p1, pltpu.roll(z, 2, 0)))
    return b, cg, xin, z, z1, z2, row


def _conv_gate(bcx, ck, name):
    S, D3 = bcx.shape
    D = D3 // 3
    tm = _pick(S, 256, 16)
    hb = tm // HALO

    def body(bcx_ref, prev_ref, ck_ref, o_ref):
        i = pl.program_id(0)
        b, _, _, z, z1, z2, _ = _conv_terms(bcx_ref, prev_ref, i, tm, D)
        conv = ck_ref[0:1, :] * z2 + ck_ref[1:2, :] * z1 + ck_ref[2:3, :] * z
        o_ref[...] = (b * conv).astype(BF16)

    return pl.pallas_call(
        body, name=name, grid=(S // tm,),
        in_specs=[_row_spec(tm, D3),
                  pl.BlockSpec((HALO, D3), lambda i: (jnp.maximum(i * hb - 1, 0), 0)),
                  _vec_spec(8, D)],
        out_specs=_row_spec(tm, D),
        out_shape=jax.ShapeDtypeStruct((S, D), BF16),
        compiler_params=_params("parallel"),
    )(bcx, bcx, ck)


def _conv_gate_bwd(du, bcx, ck, name):
    S, D3 = bcx.shape
    D = D3 // 3
    tm = _pick(S, 256, 16)
    hb = tm // HALO
    nt = S // tm

    def body(du_ref, dun_ref, bcx_ref, prev_ref, next_ref, ck_ref, o_ref, dk_ref):
        i = pl.program_id(0)
        b, cg, xin, z, z1, z2, row = _conv_terms(bcx_ref, prev_ref, i, tm, D)
        k0, k1, k2 = ck_ref[0:1, :], ck_ref[1:2, :], ck_ref[2:3, :]
        conv = k0 * z2 + k1 * z1 + k2 * z
        d = du_ref[...].astype(F32)
        dconv = d * b
        dcn = jnp.where(i < nt - 1, dun_ref[...].astype(F32) * next_ref[:, 0:D].astype(F32), 0.0)
        d1 = jnp.where(row == tm - 1, dcn[0:1, :], pltpu.roll(dconv, tm - 1, 0))
        d2 = jnp.where(row == tm - 2, dcn[0:1, :],
                       jnp.where(row == tm - 1, dcn[1:2, :], pltpu.roll(dconv, tm - 2, 0)))
        dz = k2 * dconv + k1 * d1 + k0 * d2
        o_ref[:, 0:D] = (d * conv).astype(BF16)
        o_ref[:, D:2 * D] = (dz * xin).astype(BF16)
        o_ref[:, 2 * D:3 * D] = (dz * cg).astype(BF16)

        @pl.when(i == 0)
        def _():
            dk_ref[...] = jnp.zeros_like(dk_ref)

        dk_ref[0] += _colsum8(dconv * z2)
        dk_ref[1] += _colsum8(dconv * z1)
        dk_ref[2] += _colsum8(dconv * z)

    last = S // HALO - 1
    return pl.pallas_call(
        body, name=name, grid=(nt,),
        in_specs=[_row_spec(tm, D),
                  pl.BlockSpec((HALO, D), lambda i: (jnp.minimum((i + 1) * hb, last), 0)),
                  _row_spec(tm, D3),
                  pl.BlockSpec((HALO, D3), lambda i: (jnp.maximum(i * hb - 1, 0), 0)),
                  pl.BlockSpec((HALO, D3), lambda i: (jnp.minimum((i + 1) * hb, last), 0)),
                  _vec_spec(8, D)],
        out_specs=[_row_spec(tm, D3), pl.BlockSpec((3, 8, D), lambda i: (0, 0, 0))],
        out_shape=[jax.ShapeDtypeStruct((S, D3), BF16), jax.ShapeDtypeStruct((3, 8, D), F32)],
        compiler_params=_params("arbitrary"),
    )(du, du, bcx, bcx, bcx, ck)


def _rel_onehot():
    a = np.arange(CHUNK)[:, None]
    b = np.arange(CHUNK)[None, :]
    idx = np.stack([np.clip((N_LEFT_CHUNKS - dl) * CHUNK + a - b, -MAX_REL, MAX_REL) + MAX_REL
                    for dl in (6, 7, 8)]).reshape(-1)
    return (jnp.asarray(idx)[:, None] == jnp.arange(N_REL)[None, :]).astype(F32)


def _bias_table(rel_bias):
    H = rel_bias.shape[0]
    near = jnp.dot(rel_bias, _rel_onehot().T, precision=lax.Precision.HIGHEST).reshape(H, 3, CHUNK, CHUNK)
    far = jnp.broadcast_to(rel_bias[:, N_REL - 1][:, None, None], (H, CHUNK, CHUNK))
    neg = jnp.full((H, CHUNK, CHUNK), NEG, F32)

    def block(dl):
        if dl < 0 or dl > N_LEFT_CHUNKS:
            return neg
        return far if dl <= 5 else near[:, dl - 6]

    rows = [jnp.concatenate([block(jc - ic) for jc in range(N_WIN * Q_CHUNKS)], axis=-1)
            for ic in range(Q_CHUNKS)]
    return jnp.concatenate(rows, axis=-2)


def _bias_table_grad(dtab):
    H = dtab.shape[0]
    blk = lambda ic, jc: dtab[:, ic * CHUNK:(ic + 1) * CHUNK, jc * CHUNK:(jc + 1) * CHUNK]
    by_dl = [sum(blk(ic, ic + dl) for ic in range(Q_CHUNKS)) for dl in range(N_LEFT_CHUNKS + 1)]
    far = sum(jnp.sum(by_dl[dl], axis=(1, 2)) for dl in range(6))
    near = jnp.stack(by_dl[6:9], axis=1).reshape(H, 3 * CHUNK * CHUNK)
    g = jnp.dot(near, _rel_onehot(), precision=lax.Precision.HIGHEST)
    return g.at[:, N_REL - 1].add(far)


def _attn_specs(D, W):
    q_spec = pl.BlockSpec((BQ, W), lambda g, i: (i, g))

    def win(w, off):
        return pl.BlockSpec((BQ, W), lambda g, i: (jnp.maximum(i - (N_WIN - 1) + w, 0), off + g))

    k_specs = [win(w, 0) for w in range(N_WIN)]
    v_specs = [win(w, D // W) for w in range(N_WIN)]
    tab_spec = pl.BlockSpec((None, HEADS_PER_STEP, BQ, N_WIN * BQ),
                            lambda g, i: (jnp.minimum(i, N_WIN - 1), g, 0, 0))
    dtab_spec = pl.BlockSpec((HEADS_PER_STEP, BQ, N_WIN * BQ), lambda g, i: (g, 0, 0))
    return q_spec, k_specs, v_specs, tab_spec, dtab_spec


def _start_variants(tab):
    col = jnp.arange(N_WIN * BQ)
    return jnp.stack([jnp.where(col >= (N_WIN - 1 - v) * BQ, tab, NEG) for v in range(N_WIN)])


def _attn_exp(q_ref, kw, tab_ref, h, dh):
    qh = q_ref[:, h * dh:(h + 1) * dh]
    kh = kw[:, h * dh:(h + 1) * dh]
    s = lax.dot_general(qh, kh, _DIMS["nt"], preferred_element_type=F32) + tab_ref[h]
    e = jnp.exp(s - jnp.max(s, axis=-1, keepdims=True))
    return e, jnp.sum(e, axis=-1, keepdims=True), qh, kh


def _attn_fwd(q, kv, tab, name):
    S, D = q.shape
    dh = D // N_HEADS
    W = HEADS_PER_STEP * dh
    q_spec, k_specs, v_specs, tab_spec, _ = _attn_specs(D, W)

    def body(q_ref, *rest):
        k_refs, v_refs = rest[:N_WIN], rest[N_WIN:2 * N_WIN]
        tab_ref, o_ref = rest[2 * N_WIN], rest[2 * N_WIN + 1]
        kw = jnp.concatenate([r[...] for r in k_refs], axis=0)
        vw = jnp.concatenate([r[...] for r in v_refs], axis=0)
        outs = []
        for h in range(HEADS_PER_STEP):
            e, l, _, _ = _attn_exp(q_ref, kw, tab_ref, h, dh)
            outs.append(jnp.dot(e.astype(BF16), vw[:, h * dh:(h + 1) * dh], preferred_element_type=F32) / l)
        o_ref[...] = jnp.concatenate(outs, axis=1).astype(BF16)

    return pl.pallas_call(
        body, name=name, grid=(N_HEADS // HEADS_PER_STEP, S // BQ),
        in_specs=[q_spec] + k_specs + v_specs + [tab_spec],
        out_specs=q_spec,
        out_shape=jax.ShapeDtypeStruct((S, D), BF16),
        compiler_params=_params("parallel", "parallel"),
    )(q, *([kv] * (2 * N_WIN)), tab)


def _attn_bwd(q, kv, tab, do, name):
    S, D = q.shape
    dh = D // N_HEADS
    W = HEADS_PER_STEP * dh
    q_spec, k_specs, v_specs, tab_spec, dtab_spec = _attn_specs(D, W)

    def body(q_ref, *rest):
        k_refs, v_refs = rest[:N_WIN], rest[N_WIN:2 * N_WIN]
        tab_ref, do_ref, dq_ref = rest[2 * N_WIN:2 * N_WIN + 3]
        dk_refs = rest[2 * N_WIN + 3:3 * N_WIN + 3]
        dv_refs = rest[3 * N_WIN + 3:4 * N_WIN + 3]
        dtab_ref = rest[4 * N_WIN + 3]
        i = pl.program_id(1)

        @pl.when(i == 0)
        def _():
            dtab_ref[...] = jnp.zeros_like(dtab_ref)

        kw = jnp.concatenate([r[...] for r in k_refs], axis=0)
        vw = jnp.concatenate([r[...] for r in v_refs], axis=0)
        dqs, dks, dvs = [], [], []
        for h in range(HEADS_PER_STEP):
            e, l, qh, kh = _attn_exp(q_ref, kw, tab_ref, h, dh)
            p = e * (1.0 / l)
            vh = vw[:, h * dh:(h + 1) * dh]
            doh = do_ref[:, h * dh:(h + 1) * dh]
            dp = lax.dot_general(doh, vh, _DIMS["nt"], preferred_element_type=F32)
            ds = p * (dp - jnp.sum(p * dp, axis=-1, keepdims=True))
            dtab_ref[h] += ds
            dsb = ds.astype(BF16)
            dqs.append(jnp.dot(dsb, kh, preferred_element_type=F32) * (dh ** -0.5))
            dks.append(lax.dot_general(dsb, qh, _DIMS["tn"], preferred_element_type=F32))
            dvs.append(lax.dot_general(p.astype(BF16), doh, _DIMS["tn"], preferred_element_type=F32))
        dq_ref[...] = jnp.concatenate(dqs, axis=1).astype(BF16)
        dk = jnp.concatenate(dks, axis=1)
        dv = jnp.concatenate(dvs, axis=1)
        for w in range(N_WIN):
            dk_refs[w][...] = dk[w * BQ:(w + 1) * BQ, :].astype(BF16)
            dv_refs[w][...] = dv[w * BQ:(w + 1) * BQ, :].astype(BF16)

    part = jax.ShapeDtypeStruct((S, D), BF16)
    outs = pl.pallas_call(
        body, name=name, grid=(N_HEADS // HEADS_PER_STEP, S // BQ),
        in_specs=[q_spec] + k_specs + v_specs + [tab_spec, q_spec],
        out_specs=[q_spec] + [q_spec] * (2 * N_WIN) + [dtab_spec],
        out_shape=[jax.ShapeDtypeStruct((S, D), BF16)] + [part] * (2 * N_WIN)
        + [jax.ShapeDtypeStruct(tab.shape[1:], F32)],
        compiler_params=_params("parallel", "arbitrary"),
    )(q, *([kv] * (2 * N_WIN)), tab, do)
    return outs[0], outs[1:1 + N_WIN], outs[1 + N_WIN:1 + 2 * N_WIN], outs[-1]


def _kv_grad_combine(dk_parts, dv_parts, name):
    S, D = dk_parts[0].shape
    nblk = S // BQ

    def body(*refs):
        o_ref = refs[2 * N_WIN]
        i = pl.program_id(0)
        for half, parts in enumerate((refs[:N_WIN], refs[N_WIN:2 * N_WIN])):
            acc = parts[N_WIN - 1][...].astype(F32)
            for w in range(N_WIN - 1):
                acc = acc + jnp.where(i + (N_WIN - 1 - w) < nblk, parts[w][...].astype(F32), 0.0)
            o_ref[:, half * D:(half + 1) * D] = acc.astype(BF16)

    specs = [pl.BlockSpec((BQ, D), functools.partial(
        lambda i, sh: (jnp.minimum(i + sh, nblk - 1), 0), sh=N_WIN - 1 - w)) for w in range(N_WIN)]
    return pl.pallas_call(
        body, name=name, grid=(nblk,),
        in_specs=specs + specs,
        out_specs=pl.BlockSpec((BQ, 2 * D), lambda i: (i, 0)),
        out_shape=jax.ShapeDtypeStruct((S, 2 * D), BF16),
        compiler_params=_params("parallel"),
    )(*dk_parts, *dv_parts)


def _adamw(w, g, m, v, name):
    shape = w.shape
    C = shape[-1]
    R = int(np.prod(shape[:-1])) if len(shape) > 1 else 1
    w2, g2, m2, v2 = (t.reshape(R, C) for t in (w, g, m, v))
    tr = _pick(R, max(8, (512 * 1024) // C // 8 * 8), 8)

    def body(w_ref, g_ref, m_ref, v_ref, d_ref, nm_ref, nv_ref):
        gv = g_ref[...]
        nm = ADAM_B1 * m_ref[...] + (1.0 - ADAM_B1) * gv
        nv = ADAM_B2 * v_ref[...] + (1.0 - ADAM_B2) * jnp.square(gv)
        m_hat = nm / (1.0 - ADAM_B1 ** ADAM_STEP)
        v_hat = nv / (1.0 - ADAM_B2 ** ADAM_STEP)
        d_ref[...] = -ADAM_LR * (m_hat / (jnp.sqrt(v_hat) + ADAM_EPS) + ADAM_WD * w_ref[...])
        nm_ref[...] = nm
        nv_ref[...] = nv

    spec = pl.BlockSpec((tr, C), lambda i: (i, 0))
    outs = pl.pallas_call(
        body, name=name, grid=(R // tr,),
        in_specs=[spec] * 4, out_specs=[spec] * 3,
        out_shape=[jax.ShapeDtypeStruct((R, C), F32)] * 3,
        compiler_params=_params("parallel"),
    )(w2, g2, m2, v2)
    return tuple(o.reshape(shape) for o in outs)


def _sum_rows(a, name):
    n, L = a.shape

    def body(a_ref, o_ref):
        acc = a_ref[0:1, :]
        for r in range(1, n):
            acc = acc + a_ref[r:r + 1, :]
        o_ref[...] = acc

    return pl.pallas_call(
        body, name=name, grid=(1,),
        in_specs=[pl.BlockSpec((n, L), lambda i: (0, 0))],
        out_specs=pl.BlockSpec((1, L), lambda i: (0, 0)),
        out_shape=jax.ShapeDtypeStruct((1, L), F32),
        compiler_params=_params("arbitrary"),
    )(a)


def _scalar_call(body, name, scalar, grid, in_specs, out_spec, out_shape, args):
    return pl.pallas_call(
        body, name=name,
        grid_spec=pltpu.PrefetchScalarGridSpec(num_scalar_prefetch=1, grid=grid, in_specs=in_specs,
                                               out_specs=out_spec),
        out_shape=out_shape, compiler_params=_params("parallel"),
    )(jnp.reshape(scalar, (-1,)).astype(jnp.int32), *args)


def _pair_sum(view, got, c, name):
    nb, _, rh, cols = view.shape
    tr = _pick(rh, max(16, (1 << 20) // cols // 16 * 16), 16)
    bpr = rh // tr

    def body(s_ref, a_ref, b_ref, o_ref):
        o_ref[...] = (a_ref[...].astype(F32) + b_ref[...].astype(F32)).astype(BF16)

    spec = pl.BlockSpec((tr, cols), lambda i, s: (i, 0))
    mine = pl.BlockSpec((tr, cols), lambda i, s: ((2 * (i // bpr) + s[0]) * bpr + i % bpr, 0))
    return _scalar_call(body, name, c, (nb * bpr,), [mine, spec], spec,
                        jax.ShapeDtypeStruct((nb * rh, cols), BF16),
                        (view.reshape(nb * 2 * rh, cols), got.reshape(nb * rh, cols)))


def _owner_sum(pair, recv, me, c, it, name):
    _, rh, bc = recv.shape
    tr = _pick(rh, max(16, (1 << 19) // bc // 16 * 16), 16)
    bpr = rh // tr

    def body(s_ref, a_ref, r0, r1, r2, o_ref):
        o_ref[...] = ((a_ref[...].astype(F32) + r0[...].astype(F32)) + r1[...].astype(F32)) + r2[...].astype(F32)

    if it.kind == "col":
        own = pl.BlockSpec((tr, bc), lambda i, s: (i, s[0]))
    else:
        own = pl.BlockSpec((tr, bc), lambda i, s: (s[0] * bpr + i, 0))
    slots = [pl.BlockSpec((None, tr, bc), functools.partial(lambda i, s, k: (k, i, 0), k=k)) for k in range(3)]
    return _scalar_call(body, name, jnp.stack([it.pos(me), c]), (bpr,), [own] + slots,
                        pl.BlockSpec((tr, bc), lambda i, s: (s[1] * bpr + i, 0)),
                        jax.ShapeDtypeStruct((2 * rh, bc), F32), (pair, recv, recv, recv))


def _place():
    x, y, c = lax.axis_index("x"), lax.axis_index("y"), lax.axis_index("c")
    chips = [(1 - x, y), (x, 1 - y), (1 - x, 1 - y)]
    return x, y, c, chips


def _chip_index(px, py):
    return 2 * px + py


def _all_gather_small(x_shard, name):
    m_per, n = x_shard.shape

    def body(x_ref, out_ref, send_sems, recv_sems, local_sem):
        x, y, c, chips = _place()
        me, sibling = (x, y, c), (x, y, 1 - c)

        def rows(px, py, pc):
            return out_ref.at[pl.ds((4 * px + 2 * py + pc) * m_per, m_per), :]

        def copy(k, block, to, src=None):
            return pltpu.make_async_remote_copy(
                src_ref=rows(*block) if src is None else src, dst_ref=rows(*block),
                send_sem=send_sems.at[k], recv_sem=recv_sems.at[k], device_id=to, device_id_type=MESH)

        mine = pltpu.make_async_copy(x_ref, rows(*me), local_sem)
        mine.start()
        first = [copy(0, me, sibling, src=x_ref)]
        first += [copy(1 + j, me, (*chip, c), src=x_ref) for j, chip in enumerate(chips)]
        for cp in first:
            cp.start()
        passed = [copy(4 + j, (*chip, c), sibling) for j, chip in enumerate(chips)]
        for j, chip in enumerate(chips):
            copy(1 + j, (*chip, c), me).wait_recv()
            passed[j].start()
        copy(0, sibling, me).wait_recv()
        for j, chip in enumerate(chips):
            copy(4 + j, (*chip, 1 - c), me).wait_recv()
        for cp in first + passed:
            cp.wait_send()
        mine.wait()

    return pl.pallas_call(
        body, name=name,
        out_shape=jax.ShapeDtypeStruct((N_DEV * m_per, n), x_shard.dtype),
        in_specs=[pl.BlockSpec(memory_space=pltpu.VMEM)],
        out_specs=pl.BlockSpec(memory_space=pltpu.VMEM),
        scratch_shapes=[pltpu.SemaphoreType.DMA((7,)), pltpu.SemaphoreType.DMA((7,)), pltpu.SemaphoreType.DMA],
    )(x_shard)


def _gather_flat(vec, name):
    L = vec.shape[0]
    Lp = -(-L // 1024) * 1024
    g = _all_gather_small(jnp.pad(vec, (0, Lp - L)).reshape(8, Lp // 8), name)
    return g.reshape(N_DEV, Lp)[:, :L]


class _Item:
    def __init__(self, kind, rows, cols, arg, layer, swap=False):
        self.kind, self.rows, self.cols, self.arg, self.layer, self.swap = kind, rows, cols, arg, layer, swap

    def ref(self, refs):
        return refs[self.arg].at[self.layer]

    def pos(self, j):
        return 2 * (j % 2) + j // 2 if self.swap else j


def _block(ref, it, j, half):
    if it.kind == "col":
        ns = it.cols // N_CHIP
        return ref.at[pl.ds(half * (it.rows // 2), it.rows // 2), pl.ds(it.pos(j) * ns, ns)]
    rs = it.rows // N_CHIP
    return ref.at[pl.ds(j * rs + half * (rs // 2), rs // 2), :]


def _cast_place(w, layer, kind, pos, name):
    _, r, n = w.shape
    tr = _pick(r, max(16, (1 << 20) // n // 16 * 16), 16)
    bpr = r // tr

    def body(s_ref, w_ref, o_ref):
        o_ref[...] = w_ref[...].astype(BF16)

    if kind == "col":
        full, out_idx = (1, r, N_CHIP * n), (lambda i, s: (0, i, s[0]))
    else:
        full, out_idx = (1, N_CHIP * r, n), (lambda i, s: (0, s[0] * bpr + i, 0))
    return pl.pallas_call(
        body, name=name,
        grid_spec=pltpu.PrefetchScalarGridSpec(
            num_scalar_prefetch=1, grid=(bpr,),
            in_specs=[pl.BlockSpec((None, tr, n), lambda i, s: (layer, i, 0))],
            out_specs=pl.BlockSpec((None, tr, n), out_idx)),
        out_shape=jax.ShapeDtypeStruct(full, BF16),
        compiler_params=_params("parallel"),
    )(jnp.reshape(pos, (1,)).astype(jnp.int32), w)


HBM_SPEC = pl.BlockSpec(memory_space=pltpu.HBM)
SEM_SPEC = pl.BlockSpec(memory_space=pltpu.SEMAPHORE)
ANY_SPEC = pl.BlockSpec(memory_space=pl.ANY)
SPLIT_PARAMS = dict(has_side_effects=pltpu.SideEffectType.DATAFLOW_SIDE_EFFECTING)


def _in_hbm(a):
    return pltpu.with_memory_space_constraint(a, pltpu.HBM)


def _split_start(copies_of, bufs, n_sem, after, name):
    n = len(bufs)

    def body(*refs):
        ins, send, recv, token = refs[:n], refs[n + 1], refs[n + 2], refs[2 * n + 3]
        for cp in copies_of(ins, send, recv, False)[0]:
            cp.start()
        token[...] = jnp.zeros_like(token)

    outs = pl.pallas_call(
        body, name=name,
        out_shape=(pltpu.SemaphoreType.DMA(n_sem), pltpu.SemaphoreType.DMA(n_sem),
                   *[pltpu.HBM(b.shape, b.dtype) for b in bufs], jax.ShapeDtypeStruct((8, 128), F32)),
        in_specs=[HBM_SPEC] * n + [ANY_SPEC],
        out_specs=(SEM_SPEC, SEM_SPEC, *[HBM_SPEC] * n, pl.BlockSpec(memory_space=pltpu.VMEM)),
        input_output_aliases={t: 2 + t for t in range(n)},
        compiler_params=pltpu.CompilerParams(**SPLIT_PARAMS),
    )(*[_in_hbm(b) for b in bufs], after)
    return outs[0], outs[1], list(outs[2:2 + n]), outs[2 + n]


def _split_wait(copies_of, send, recv, bufs, after, name):
    n = len(bufs)

    def body(*refs):
        ins, send_ref, recv_ref = refs[:n], refs[n], refs[n + 1]
        sends, arrivals = copies_of(ins, send_ref, recv_ref, True)
        for cp in sends:
            cp.wait_send()
        for cp in arrivals:
            cp.wait_recv()

    return pl.pallas_call(
        body, name=name,
        out_shape=[pltpu.HBM(b.shape, b.dtype) for b in bufs],
        in_specs=[HBM_SPEC] * n + [SEM_SPEC, SEM_SPEC, ANY_SPEC],
        out_specs=[HBM_SPEC] * n,
        input_output_aliases={t: t for t in range(n)},
        compiler_params=pltpu.CompilerParams(**SPLIT_PARAMS),
    )(*bufs, send, recv, after)


def _gather_copies(items):
    def copies_of(refs, send, recv, with_arrivals):
        x, y, c, chips = _place()
        me = _chip_index(x, y)
        sends, arrivals = [], []
        for t, it in enumerate(items):
            for k, chip in enumerate(chips):
                for core in range(2):
                    mine = _block(it.ref(refs), it, me, c)
                    sends.append(pltpu.make_async_remote_copy(
                        src_ref=mine, dst_ref=mine, send_sem=send.at[6 * t + 2 * k + core],
                        recv_sem=recv.at[6 * t + 2 * k + c], device_id=(*chip, core), device_id_type=MESH))
                    if with_arrivals:
                        landed = _block(it.ref(refs), it, _chip_index(*chip), core)
                        arrivals.append(pltpu.make_async_remote_copy(
                            src_ref=landed, dst_ref=landed, send_sem=send.at[6 * t + 2 * k + core],
                            recv_sem=recv.at[6 * t + 2 * k + core], device_id=(*chip, core), device_id_type=MESH))
        return sends, arrivals

    return copies_of


def _owner_copies(items):
    n = len(items)

    def blk(ref, it, j):
        if it.kind == "col":
            ns = it.cols // N_CHIP
            return ref.at[:, pl.ds(it.pos(j) * ns, ns)]
        return ref.at[j]

    def copies_of(refs, send, recv, with_arrivals):
        x, y, c, chips = _place()
        sends, arrivals = [], []
        for t, it in enumerate(items):
            for k, chip in enumerate(chips):
                slot = refs[n + t].at[k]
                sends.append(pltpu.make_async_remote_copy(
                    src_ref=blk(refs[t], it, _chip_index(*chip)), dst_ref=slot, send_sem=send.at[3 * t + k],
                    recv_sem=recv.at[3 * t + k], device_id=(*chip, c), device_id_type=MESH))
                if with_arrivals:
                    arrivals.append(pltpu.make_async_remote_copy(
                        src_ref=slot, dst_ref=slot, send_sem=send.at[3 * t + k], recv_sem=recv.at[3 * t + k],
                        device_id=(*chip, c), device_id_type=MESH))
        return sends, arrivals

    return copies_of


def _owner_slot_shape(it):
    if it.kind == "col":
        return (3, it.rows // 2, it.cols // N_CHIP)
    return (3, it.rows // (2 * N_CHIP), it.cols)


def _pair_view(g, it):
    if it.kind == "col":
        return g.reshape(1, 2, it.rows // 2, it.cols)
    return g.reshape(N_CHIP, 2, it.rows // (2 * N_CHIP), it.cols)


def _pair_exchange(views, name):
    n = len(views)

    def body(*refs):
        ins, outs, send, recv = refs[:n], refs[n:2 * n], refs[2 * n], refs[2 * n + 1]
        x, y, c, _ = _place()
        cps = []
        for t in range(n):
            cp = pltpu.make_async_remote_copy(
                src_ref=ins[t].at[:, pl.ds(1 - c, 1)], dst_ref=outs[t],
                send_sem=send.at[t], recv_sem=recv.at[t], device_id=(x, y, 1 - c), device_id_type=MESH)
            cp.start()
            cps.append(cp)
        for cp in cps:
            cp.wait()

    any_spec = pl.BlockSpec(memory_space=pl.ANY)
    return pl.pallas_call(
        body, name=name,
        out_shape=[jax.ShapeDtypeStruct((v.shape[0], 1) + v.shape[2:], v.dtype) for v in views],
        in_specs=[any_spec] * n, out_specs=[any_spec] * n,
        scratch_shapes=[pltpu.SemaphoreType.DMA((n,)), pltpu.SemaphoreType.DMA((n,))],
    )(*views)


def _half_exchange(bufs, name):
    n = len(bufs)

    def body(*refs):
        ins, outs, send, recv = refs[:n], refs[n:2 * n], refs[2 * n], refs[2 * n + 1]
        x, y, c, _ = _place()
        cps = []
        for t in range(n):
            r2 = ins[t].shape[0] // 2
            cp = pltpu.make_async_remote_copy(
                src_ref=ins[t].at[pl.ds(c * r2, r2), :], dst_ref=outs[t].at[pl.ds(c * r2, r2), :],
                send_sem=send.at[t], recv_sem=recv.at[t], device_id=(x, y, 1 - c), device_id_type=MESH)
            cp.start()
            cps.append(cp)
        for t in range(n):
            r2 = ins[t].shape[0] // 2
            theirs = outs[t].at[pl.ds((1 - c) * r2, r2), :]
            pltpu.make_async_remote_copy(
                src_ref=theirs, dst_ref=theirs, send_sem=send.at[t], recv_sem=recv.at[t],
                device_id=(x, y, 1 - c), device_id_type=MESH).wait_recv()
        for cp in cps:
            cp.wait_send()

    any_spec = pl.BlockSpec(memory_space=pl.ANY)
    return pl.pallas_call(
        body, name=name,
        out_shape=[jax.ShapeDtypeStruct(b.shape, b.dtype) for b in bufs],
        in_specs=[any_spec] * n, out_specs=[any_spec] * n,
        input_output_aliases={t: t for t in range(n)},
        scratch_shapes=[pltpu.SemaphoreType.DMA((n,)), pltpu.SemaphoreType.DMA((n,))],
    )(*bufs)


class _Reduction:
    pass


def _reduce_start(grads, items, after, tag):
    x, y, c, _ = _place()
    views = [_pair_view(g, it) for g, it in zip(grads, items)]
    got = _pair_exchange(views, f"rs_pair_exchange_{tag}")
    pairs = [_pair_sum(v, r, c, f"rs_pair_sum_{tag}_{t}") for t, (v, r) in enumerate(zip(views, got))]
    shaped = [p if it.kind == "col" else p.reshape(N_CHIP, p.shape[0] // N_CHIP, p.shape[1])
              for p, it in zip(pairs, items)]
    lands = [lax.empty(_owner_slot_shape(it), BF16) for it in items]
    r = _Reduction()
    r.items, r.tag = items, tag
    r.send, r.recv, r.bufs, r.token = _split_start(
        _owner_copies(items), shaped + lands, (3 * len(items),), after, f"rs_owner_start_{tag}")
    return r


def _reduce_finish(groups, after):
    x, y, c, _ = _place()
    me = _chip_index(x, y)
    halves = []
    for r in groups:
        n = len(r.items)
        bufs = _split_wait(_owner_copies(r.items), r.send, r.recv, r.bufs, after, f"rs_owner_wait_{r.tag}")
        for t, it in enumerate(r.items):
            pair = bufs[t].reshape(-1, bufs[t].shape[-1])
            halves.append(_owner_sum(pair, bufs[n + t], me, c, it, f"rs_owner_sum_{r.tag}_{t}"))
    return _half_exchange(halves, "rs_half_exchange")


def _silu(v):
    return v * jax.nn.sigmoid(v)


def _sum8(p):
    return jnp.sum(p, axis=-2)


def kernel(x, c, mod_w, mod_b, norm_g, ffn_w_in, ffn_w_out, conv_w_in, conv_k, conv_w_out, kv_mod_w, kv_mod_b, kv_norm_g, w_kv, attn_w_q, attn_w_o, rel_bias, loss_target, m_mod_w, m_mod_b, m_norm_g, m_ffn_w_in, m_ffn_w_out, m_conv_w_in, m_conv_k, m_conv_w_out, m_kv_mod_w, m_kv_mod_b, m_kv_norm_g, m_w_kv, m_attn_w_q, m_attn_w_o, m_rel_bias, v_mod_w, v_mod_b, v_norm_g, v_ffn_w_in, v_ffn_w_out, v_conv_w_in, v_conv_k, v_conv_w_out, v_kv_mod_w, v_kv_mod_b, v_kv_norm_g, v_w_kv, v_attn_w_q, v_attn_w_o, v_rel_bias):
    xi, yi, ci = lax.axis_index("x"), lax.axis_index("y"), lax.axis_index("c")
    chip = 2 * xi + yi
    dev = 2 * chip + ci
    _, S, D = x.shape
    F = ffn_w_out.shape[1] * N_CHIP
    x0 = x.reshape(S, D)
    target = loss_target.reshape(S, D)
    n_mod = mod_w.shape[2]
    n_kvm = kv_mod_w.shape[1]
    dsh = D // N_CHIP
    TF = F // 2

    c_all = _all_gather_small(c.reshape(8, D // 8), "ag_c").reshape(N_DEV, D)
    sc16 = jnp.pad(_silu(c_all), ((0, 8), (0, 0)))
    part = [_mm(sc16, mod_w, "nn", F32, f"mod_fwd_{l}", b_layer=l)[:8] for l in range(2)]
    part.append(_mm(sc16, kv_mod_w, "nn", F32, "mod_fwd_kv")[:8])
    fwd_vec = jnp.concatenate([p.reshape(-1) for p in part] + [norm_g.reshape(-1), conv_k.reshape(-1)])
    fwd_all = _gather_flat(fwd_vec, "ag_fwd_small")[0::2]
    o = 0
    mods = []
    for n in (n_mod, n_mod, n_kvm):
        blk = fwd_all[:, o:o + 8 * n].reshape(N_CHIP, 8, n)
        mods.append(lax.dynamic_index_in_dim(blk, dev, axis=1, keepdims=False).reshape(N_CHIP * n))
        o += 8 * n
    ng = fwd_all[:, o:o + 8 * dsh].reshape(N_CHIP, 2, 4, dsh).transpose(1, 2, 0, 3).reshape(2, 4, D)
    o += 8 * dsh
    ck = fwd_all[:, o:o + 3 * dsh].reshape(N_CHIP, 3, dsh).transpose(1, 0, 2).reshape(3, D)
    ck8 = jnp.pad(ck, ((0, 5), (0, 0)))
    mod = [mods[l] + mod_b[l] for l in range(2)]
    sh1, sc1, g1, sh2, sc2, g2 = zip(*[jnp.split(m, 6) for m in mod])
    kv_sh, kv_sc = jnp.split(mods[2] + kv_mod_b, 2)
    row = lambda v: v.reshape(1, D)

    it_conv = [_Item("col", D, 3 * D, 0, 0), _Item("row", D, D, 1, 0)]
    it_ffn = [_Item("col", D, 2 * F, 0, 0, swap=True), _Item("row", F, D, 1, 0)]
    it_attn = [_Item("col", D, 2 * D, 0, 0), _Item("row", D, D, 1, 0), _Item("row", D, D, 2, 0)]

    def placed(w, layer, it, nm):
        return _cast_place(w, layer, it.kind, it.pos(chip), f"place_{nm}")

    groups = [
        ("conv", it_conv, [placed(conv_w_in, 0, it_conv[0], "conv_w_in"), placed(conv_w_out, 0, it_conv[1], "conv_w_out")]),
        ("ffn0", it_ffn, [placed(ffn_w_in, 0, it_ffn[0], "ffn_w_in0"), placed(ffn_w_out, 0, it_ffn[1], "ffn_w_out0")]),
        ("attn", it_attn, [placed(w_kv[None], 0, it_attn[0], "w_kv"), placed(attn_w_q, 0, it_attn[1], "attn_w_q"),
                           placed(attn_w_o, 0, it_attn[2], "attn_w_o")]),
        ("ffn1", it_ffn, [placed(ffn_w_in, 1, it_ffn[0], "ffn_w_in1"), placed(ffn_w_out, 1, it_ffn[1], "ffn_w_out1")]),
    ]
    token = jnp.zeros((8, 128), F32)
    flying = {}
    for tag, its, bufs in groups:
        send, recv, bufs, token = _split_start(_gather_copies(its), bufs, (6 * len(its),), token, f"ag_start_{tag}")
        flying[tag] = (its, send, recv, bufs)

    def arrived(tag, after):
        its, send, recv, bufs = flying[tag]
        return _split_wait(_gather_copies(its), send, recv, bufs, after, f"ag_wait_{tag}")

    a1 = row(ng[0, 0] * (1.0 + sc1[0])) + token[0, 0]
    W_cin, W_cout = arrived("conv", a1)
    (h1,) = _norm_mod(x0, a1, row(sh1[0]), "l0_norm1")
    bcx = _mm(h1, W_cin, "nn", BF16, "l0_conv_in", b_layer=0)
    ug = _conv_gate(bcx, ck8, "l0_conv_gate")
    y1 = _mm(ug, W_cout, "nn", BF16, "l0_conv_out", b_layer=0)
    gt1 = row(g1[0] * ng[0, 1])
    x1 = _post_norm(x0, y1, gt1, "l0_post1")
    a2 = row(ng[0, 2] * (1.0 + sc2[0]))
    (h2,) = _norm_mod(x1, a2, row(sh2[0]), "l0_norm2")
    W_fin0, W_fout0 = arrived("ffn0", h2)
    gu0, act0 = _ffn_in_act(h2, W_fin0, 0, "l0_ffn_in")
    y2 = _mm(act0, W_fout0, "nn", BF16, "l0_ffn_out", b_layer=0)
    gt2 = row(g2[0] * ng[0, 3])
    x2 = _post_norm(x1, y2, gt2, "l0_post2")
    a3 = ng[1, 0] * (1.0 + sc1[1])
    akv = kv_norm_g * (1.0 + kv_sc)
    h3, hkv = _norm_mod(x2, jnp.stack([a3, akv]), jnp.stack([sh1[1], kv_sh]), "l1_norm1")
    W_kv, W_q, W_o = arrived("attn", hkv)
    kvp = _mm(hkv, W_kv, "nn", BF16, "l1_kv", b_layer=0)
    att_scale = (D // N_HEADS) ** -0.5
    assert math.log2(att_scale) % 1 == 0, "scaling q before its bf16 cast is exact only for a power of two"
    qp = _mm(h3, W_q, "nn", BF16, "l1_q", b_layer=0, scale=att_scale)
    tab = _start_variants(_bias_table(rel_bias[0]))
    oh = _attn_fwd(qp, kvp, tab, "l1_attn")
    y3 = _mm(oh, W_o, "nn", BF16, "l1_attn_out", b_layer=0)
    gt3 = row(g1[1] * ng[1, 1])
    x3 = _post_norm(x2, y3, gt3, "l1_post1")
    a4 = row(ng[1, 2] * (1.0 + sc2[1]))
    (h4,) = _norm_mod(x3, a4, row(sh2[1]), "l1_norm2")
    W_fin1, W_fout1 = arrived("ffn1", h4)
    gu1, act1 = _ffn_in_act(h4, W_fin1, 0, "l1_ffn_in")
    y4 = _mm(act1, W_fout1, "nn", BF16, "l1_ffn_out", b_layer=0)
    gt4 = row(g2[1] * ng[1, 3])
    x4 = _post_norm(x3, y4, gt4, "l1_post2")
    dx4, sq = _loss_grad(x4, target, "loss")
    loss_part = 0.5 * jnp.sum(sq) / D

    def ffn_bwd(dxn, xin_, h, gu, act, y, gt, a, w_in, w_out, tag):
        dy, dgt = _post_norm_bwd(dxn, y, gt, f"{tag}_post2_bwd")
        dgu = _ffn_out_dx_act(dy, w_out, 0, gu, f"{tag}_ffn_out_dx")
        g_fout = _mm(act, dy, "tn", BF16, f"{tag}_ffn_out_dw", tm=TF)
        dh = _mm(dgu, w_in, "nt", BF16, f"{tag}_ffn_in_dx", b_layer=0)
        g_fin = _mm(h, dgu, "tn", BF16, f"{tag}_ffn_in_dw", tn=TF)
        dx, ds, db = _pre_norm_bwd(xin_, dxn, [dh], a, f"{tag}_norm2_bwd")
        return dx, _sum8(dgt), _sum8(ds)[0], _sum8(db)[0], g_fin, g_fout

    dx3, dgt4, da4, db4, G_fin1, G_fout1 = ffn_bwd(dx4, x3, h4, gu1, act1, y4, gt4, a4, W_fin1, W_fout1, "l1")
    red = [_reduce_start([G_fin1, G_fout1], it_ffn, token, "ffn1")]
    dy3, dgt3 = _post_norm_bwd(dx3, y3, gt3, "l1_post1_bwd")
    doh = _mm(dy3, W_o, "nt", BF16, "l1_attn_out_dx", b_layer=0)
    G_o = _mm(oh, dy3, "tn", BF16, "l1_attn_out_dw")
    dq, dk_parts, dv_parts, dtab = _attn_bwd(qp, kvp, tab, doh, "l1_attn_bwd")
    d_rel = _bias_table_grad(dtab)
    dkv = _kv_grad_combine(dk_parts, dv_parts, "l1_kv_grad")
    dh3 = _mm(dq, W_q, "nt", BF16, "l1_q_dx", b_layer=0)
    G_q = _mm(h3, dq, "tn", BF16, "l1_q_dw")
    dhkv = _mm(dkv, W_kv, "nt", BF16, "l1_kv_dx", b_layer=0)
    G_kv = _mm(hkv, dkv, "tn", BF16, "l1_kv_dw")
    red.append(_reduce_start([G_kv, G_q, G_o], it_attn, red[-1].token, "attn"))
    dx2, ds3, db3 = _pre_norm_bwd(x2, dx3, [dh3, dhkv], jnp.stack([a3, akv]), "l1_norm1_bwd")
    ds3, db3 = _sum8(ds3), _sum8(db3)

    dx1, dgt2, da2, db2, G_fin0, G_fout0 = ffn_bwd(dx2, x1, h2, gu0, act0, y2, gt2, a2, W_fin0, W_fout0, "l0")
    red.append(_reduce_start([G_fin0, G_fout0], it_ffn, red[-1].token, "ffn0"))
    dy1, dgt1 = _post_norm_bwd(dx1, y1, gt1, "l0_post1_bwd")
    dug = _mm(dy1, W_cout, "nt", BF16, "l0_conv_out_dx", b_layer=0)
    G_cout = _mm(ug, dy1, "tn", BF16, "l0_conv_out_dw")
    dbcx, dck = _conv_gate_bwd(dug, bcx, ck8, "l0_conv_gate_bwd")
    dh1 = _mm(dbcx, W_cin, "nt", BF16, "l0_conv_in_dx", b_layer=0)
    G_cin = _mm(h1, dbcx, "tn", BF16, "l0_conv_in_dw")
    red.append(_reduce_start([G_cin, G_cout], it_conv, red[-1].token, "conv"))
    dx0, ds1, db1 = _pre_norm_bwd(x0, dx1, [dh1], a1, "l0_norm1_bwd")
    ds1, db1 = _sum8(ds1)[0], _sum8(db1)[0]
    dgt1, dgt3 = _sum8(dgt1), _sum8(dgt3)

    def dmod_of(l, ds_a, db_a, dgt_a, ds_b, db_b, dgt_b):
        return jnp.concatenate([db_a, ds_a * ng[l, 0], dgt_a * ng[l, 1], db_b, ds_b * ng[l, 2], dgt_b * ng[l, 3]])

    dmod0 = dmod_of(0, ds1, db1, dgt1, da2, db2, dgt2)
    dmod1 = dmod_of(1, ds3[0], db3[0], dgt3, da4, db4, dgt4)
    dkvmod = jnp.concatenate([db3[1], ds3[1] * kv_norm_g])
    dng = jnp.stack([
        jnp.stack([ds1 * (1.0 + sc1[0]), dgt1 * g1[0], da2 * (1.0 + sc2[0]), dgt2 * g2[0]]),
        jnp.stack([ds3[0] * (1.0 + sc1[1]), dgt3 * g1[1], da4 * (1.0 + sc2[1]), dgt4 * g2[1]])])
    dkvng = ds3[1] * (1.0 + kv_sc)
    small = [dmod0, dmod1, dkvmod, dng.reshape(-1), dkvng, _sum8(dck).reshape(-1), d_rel.reshape(-1),
             loss_part.reshape(1)]
    sizes = [int(s.shape[0]) for s in small]
    offs = np.concatenate([[0], np.cumsum(sizes)])
    bwd_all = _gather_flat(jnp.concatenate(small), "ag_bwd_small")
    Lb = bwd_all.shape[1]
    Lp = -(-Lb // 128) * 128
    tot = _sum_rows(jnp.pad(bwd_all, ((0, 0), (0, Lp - Lb))), "sum_small")[0]
    seg = lambda i: tot[offs[i]:offs[i + 1]]
    g_mod_b = jnp.stack([seg(0), seg(1)])
    g_kv_mod_b = seg(2)
    g_norm_g = lax.dynamic_slice_in_dim(seg(3).reshape(2, 4, D), chip * dsh, dsh, axis=2)
    g_kv_norm_g = seg(4)
    g_conv_k = lax.dynamic_slice_in_dim(seg(5).reshape(1, 3, D), chip * dsh, dsh, axis=2)
    g_rel_bias = seg(6).reshape(rel_bias.shape)
    loss = seg(7)[0]

    def dmod_w(i, n, name):
        rows_ = lax.dynamic_slice_in_dim(bwd_all[:, offs[i]:offs[i + 1]], chip * n, n, axis=1)
        return _mm(sc16, jnp.pad(rows_, ((0, 8), (0, 0))), "tn", F32, name)

    g_mod_w = jnp.stack([dmod_w(0, n_mod, "mod_bwd_0"), dmod_w(1, n_mod, "mod_bwd_1")])
    g_kv_mod_w = dmod_w(2, n_kvm, "mod_bwd_kv")

    r_fin1, r_fout1, r_kv, r_q, r_o, r_fin0, r_fout0, r_cin, r_cout = _reduce_finish(red, dx0)
    grads = {
        "mod_w": g_mod_w, "mod_b": g_mod_b, "norm_g": g_norm_g,
        "ffn_w_in": jnp.stack([r_fin0, r_fin1]), "ffn_w_out": jnp.stack([r_fout0, r_fout1]),
        "conv_w_in": r_cin[None], "conv_k": g_conv_k, "conv_w_out": r_cout[None],
        "kv_mod_w": g_kv_mod_w, "kv_mod_b": g_kv_mod_b, "kv_norm_g": g_kv_norm_g, "w_kv": r_kv,
        "attn_w_q": r_q[None], "attn_w_o": r_o[None], "rel_bias": g_rel_bias,
    }
    weights = dict(mod_w=mod_w, mod_b=mod_b, norm_g=norm_g, ffn_w_in=ffn_w_in, ffn_w_out=ffn_w_out,
                   conv_w_in=conv_w_in, conv_k=conv_k, conv_w_out=conv_w_out, kv_mod_w=kv_mod_w,
                   kv_mod_b=kv_mod_b, kv_norm_g=kv_norm_g, w_kv=w_kv, attn_w_q=attn_w_q, attn_w_o=attn_w_o,
                   rel_bias=rel_bias)
    m_in = dict(mod_w=m_mod_w, mod_b=m_mod_b, norm_g=m_norm_g, ffn_w_in=m_ffn_w_in, ffn_w_out=m_ffn_w_out,
                conv_w_in=m_conv_w_in, conv_k=m_conv_k, conv_w_out=m_conv_w_out, kv_mod_w=m_kv_mod_w,
                kv_mod_b=m_kv_mod_b, kv_norm_g=m_kv_norm_g, w_kv=m_w_kv, attn_w_q=m_attn_w_q,
                attn_w_o=m_attn_w_o, rel_bias=m_rel_bias)
    v_in = dict(mod_w=v_mod_w, mod_b=v_mod_b, norm_g=v_norm_g, ffn_w_in=v_ffn_w_in, ffn_w_out=v_ffn_w_out,
                conv_w_in=v_conv_w_in, conv_k=v_conv_k, conv_w_out=v_conv_w_out, kv_mod_w=v_kv_mod_w,
                kv_mod_b=v_kv_mod_b, kv_norm_g=v_kv_norm_g, w_kv=v_w_kv, attn_w_q=v_attn_w_q,
                attn_w_o=v_attn_w_o, rel_bias=v_rel_bias)
    names = list(weights)
    g_out, d_out, m_out, v_out = [], [], [], []
    for n in names:
        g = grads[n].reshape(weights[n].shape)
        d, nm, nv = _adamw(weights[n], g, m_in[n], v_in[n], f"adamw_{n}")
        g_out.append(g)
        d_out.append(d)
        m_out.append(nm)
        v_out.append(nv)
    return (loss, dx0.reshape(x.shape), *g_out, *d_out, *m_out, *v_out)
```

```python
import functools
import math

import numpy as np
import jax
import jax.numpy as jnp
from jax import lax
from jax.experimental import pallas as pl
from jax.experimental.pallas import tpu as pltpu

CHUNK = 64
N_LEFT_CHUNKS = 8
N_HEADS = 16
MAX_REL = 2 * CHUNK
N_REL = 2 * MAX_REL + 1
EPS = 1e-6
ADAM_LR = 0.001
ADAM_B1 = 0.9
ADAM_B2 = 0.999
ADAM_EPS = 1e-08
ADAM_WD = 0.01
ADAM_STEP = 10

Q_CHUNKS = 4
BQ = Q_CHUNKS * CHUNK
N_WIN = 1 + N_LEFT_CHUNKS // Q_CHUNKS
HEADS_PER_STEP = 8
NEG = -1e30
N_DEV = 8
N_CHIP = 4

BF16 = jnp.bfloat16
F32 = jnp.float32
V7X_VMEM_LIMIT_BYTES = 56 * 1024 * 1024
MESH = pl.DeviceIdType.MESH


def _pick(n, pref, align):
    t = min(pref, n)
    t -= t % align
    while t >= align:
        if n % t == 0:
            return t
        t -= align
    return n


def _params(*sem):
    return pltpu.CompilerParams(dimension_semantics=sem, vmem_limit_bytes=V7X_VMEM_LIMIT_BYTES)


def _colsum8(v):
    r, d = v.shape
    return v.reshape(r // 8, 8, d).sum(axis=0)


_DIMS = {"nn": (((1,), (0,)), ((), ())), "nt": (((1,), (1,)), ((), ())), "tn": (((0,), (0,)), ((), ()))}


def _mm(a, b, mode, out_dtype, name, *, b_layer=None, tm=1024, tn=1024, tk=None, scale=None):
    if tk is None:
        tk = 2048 if mode == "tn" else 3072
    bs = b.shape[1:] if b_layer is not None else b.shape
    if mode == "nn":
        (M, K), (K2, N) = a.shape, bs
    elif mode == "nt":
        (M, K), (N, K2) = a.shape, bs
    else:
        (K, M), (K2, N) = a.shape, bs
    assert K == K2, (name, a.shape, b.shape)
    tm = _pick(M, tm, 128 if mode == "tn" else 16)
    tn = _pick(N, tn, 128)
    tk = _pick(K, tk, 128 if mode != "tn" else 16)
    nk = K // tk
    assert scale is None or nk == 1, name
    dims = _DIMS[mode]

    def body(a_ref, b_ref, o_ref, *acc):
        p = lax.dot_general(a_ref[...].astype(BF16), b_ref[...].astype(BF16), dims,
                            preferred_element_type=F32)
        if nk == 1:
            o_ref[...] = (p if scale is None else p * scale).astype(o_ref.dtype)
        else:
            k = pl.program_id(2)

            @pl.when(k == 0)
            def _():
                acc[0][...] = p

            @pl.when(k > 0)
            def _():
                acc[0][...] += p

            @pl.when(k == nk - 1)
            def _():
                o_ref[...] = acc[0][...].astype(o_ref.dtype)

    a_spec = (pl.BlockSpec((tk, tm), lambda i, j, k: (k, i)) if mode == "tn"
              else pl.BlockSpec((tm, tk), lambda i, j, k: (i, k)))
    if mode == "nt":
        b_blk, b_idx = (tn, tk), (lambda i, j, k: (j, k))
    else:
        b_blk, b_idx = (tk, tn), (lambda i, j, k: (k, j))
    if b_layer is not None:
        b_spec = pl.BlockSpec((None,) + b_blk, lambda i, j, k: (b_layer,) + b_idx(i, j, k))
    else:
        b_spec = pl.BlockSpec(b_blk, b_idx)
    return pl.pallas_call(
        body, name=name,
        grid=(M // tm, N // tn, nk),
        in_specs=[a_spec, b_spec],
        out_specs=pl.BlockSpec((tm, tn), lambda i, j, k: (i, j)),
        out_shape=jax.ShapeDtypeStruct((M, N), out_dtype),
        scratch_shapes=[pltpu.VMEM((tm, tn), F32)] if nk > 1 else [],
        compiler_params=_params("parallel", "parallel", "arbitrary"),
    )(a, b)


def _row_spec(tm, d):
    return pl.BlockSpec((tm, d), lambda i: (i, 0))


def _vec_spec(r, d):
    return pl.BlockSpec((r, d), lambda i: (0, 0))


def _norm_mod(x, scales, shifts, name):
    S, D = x.shape
    nb = scales.shape[0]
    tm = _pick(S, 512, 16)

    def body(x_ref, a_ref, b_ref, *o_refs):
        xv = x_ref[...]
        xh = xv * lax.rsqrt(jnp.mean(xv * xv, axis=-1, keepdims=True) + EPS)
        for n in range(nb):
            o_refs[n][...] = (xh * a_ref[n:n + 1, :] + b_ref[n:n + 1, :]).astype(BF16)

    return pl.pallas_call(
        body, name=name, grid=(S // tm,),
        in_specs=[_row_spec(tm, D), _vec_spec(nb, D), _vec_spec(nb, D)],
        out_specs=[_row_spec(tm, D)] * nb,
        out_shape=[jax.ShapeDtypeStruct((S, D), BF16)] * nb,
        compiler_params=_params("parallel"),
    )(x, scales, shifts)


def _mm_post(a, w, x, gate, name, *, scales=None, shifts=None, target=None):
    M, K = a.shape
    D = w.shape[2]
    tm = _pick(M, 512, 16)
    sub = _pick(tm, 256, 16)
    nb = 0 if scales is None else scales.shape[0]

    def body(a_ref, w_ref, x_ref, g_ref, *rest):
        if target is None:
            sc_ref, sh_ref, y_ref, xn_ref = rest[:4]
            h_refs = rest[4:]
        else:
            t_ref, y_ref, dx_ref, sq_ref = rest

            @pl.when(pl.program_id(0) == 0)
            def _():
                sq_ref[...] = jnp.zeros_like(sq_ref)

        for r in range(tm // sub):
            rows = pl.ds(r * sub, sub)
            yb = jnp.dot(a_ref[rows, :], w_ref[...], preferred_element_type=F32).astype(BF16)
            y_ref[rows, :] = yb
            yv = yb.astype(F32)
            yh = yv * lax.rsqrt(jnp.mean(yv * yv, axis=-1, keepdims=True) + EPS)
            xn = x_ref[rows, :] + yh * g_ref[...]
            if target is None:
                xn_ref[rows, :] = xn
                xh = xn * lax.rsqrt(jnp.mean(xn * xn, axis=-1, keepdims=True) + EPS)
                for n in range(nb):
                    h_refs[n][rows, :] = (xh * sc_ref[n:n + 1, :] + sh_ref[n:n + 1, :]).astype(BF16)
            else:
                e = xn - t_ref[rows, :]
                dx_ref[rows, :] = e / D
                sq_ref[...] += _colsum8(e * e)

    ins = [a, w, x, gate]
    in_specs = [_row_spec(tm, K), pl.BlockSpec((None, K, D), lambda i: (0, 0, 0)), _row_spec(tm, D), _vec_spec(1, D)]
    if target is None:
        ins += [scales, shifts]
        in_specs += [_vec_spec(nb, D), _vec_spec(nb, D)]
        out_specs = [_row_spec(tm, D)] * (2 + nb)
        out_shape = [jax.ShapeDtypeStruct((M, D), BF16), jax.ShapeDtypeStruct((M, D), F32)] \
            + [jax.ShapeDtypeStruct((M, D), BF16)] * nb
    else:
        ins += [target]
        in_specs += [_row_spec(tm, D)]
        out_specs = [_row_spec(tm, D), _row_spec(tm, D), _vec_spec(8, D)]
        out_shape = [jax.ShapeDtypeStruct((M, D), BF16), jax.ShapeDtypeStruct((M, D), F32),
                     jax.ShapeDtypeStruct((8, D), F32)]
    return pl.pallas_call(
        body, name=name, grid=(M // tm,), in_specs=in_specs, out_specs=out_specs, out_shape=out_shape,
        compiler_params=_params("arbitrary" if target is not None else "parallel"),
    )(*ins)


def _post_norm_bwd(dxn, y, gate, name):
    S, D = y.shape
    tm = _pick(S, 512, 16)

    def body(d_ref, y_ref, g_ref, dy_ref, dg_ref):
        yv = y_ref[...].astype(F32)
        dv = d_ref[...]
        r = lax.rsqrt(jnp.mean(yv * yv, axis=-1, keepdims=True) + EPS)
        yh = yv * r
        dyh = dv * g_ref[...]
        dy_ref[...] = (r * (dyh - yh * jnp.mean(dyh * yh, axis=-1, keepdims=True))).astype(BF16)

        @pl.when(pl.program_id(0) == 0)
        def _():
            dg_ref[...] = jnp.zeros_like(dg_ref)

        dg_ref[...] += _colsum8(dv * yh)

    return pl.pallas_call(
        body, name=name, grid=(S // tm,),
        in_specs=[_row_spec(tm, D), _row_spec(tm, D), _vec_spec(1, D)],
        out_specs=[_row_spec(tm, D), _vec_spec(8, D)],
        out_shape=[jax.ShapeDtypeStruct((S, D), BF16), jax.ShapeDtypeStruct((8, D), F32)],
        compiler_params=_params("arbitrary"),
    )(dxn, y, gate)


def _pre_norm_bwd(x, dxn, dhs, scales, name):
    S, D = x.shape
    nb = len(dhs)
    tm = _pick(S, 512, 8)

    def body(x_ref, d_ref, a_ref, *rest):
        dh_refs, dx_ref, ds_ref, db_ref = rest[:nb], rest[nb], rest[nb + 1], rest[nb + 2]
        xv = x_ref[...]
        r = lax.rsqrt(jnp.mean(xv * xv, axis=-1, keepdims=True) + EPS)
        xh = xv * r

        @pl.when(pl.program_id(0) == 0)
        def _():
            ds_ref[...] = jnp.zeros_like(ds_ref)
            db_ref[...] = jnp.zeros_like(db_ref)

        dxh = jnp.zeros_like(xv)
        for n in range(nb):
            dh = dh_refs[n][...].astype(F32)
            dxh = dxh + dh * a_ref[n:n + 1, :]
            ds_ref[n] += _colsum8(dh * xh)
            db_ref[n] += _colsum8(dh)
        dx_ref[...] = d_ref[...] + r * (dxh - xh * jnp.mean(dxh * xh, axis=-1, keepdims=True))

    acc_spec = pl.BlockSpec((nb, 8, D), lambda i: (0, 0, 0))
    return pl.pallas_call(
        body, name=name, grid=(S // tm,),
        in_specs=[_row_spec(tm, D), _row_spec(tm, D), _vec_spec(nb, D)] + [_row_spec(tm, D)] * nb,
        out_specs=[_row_spec(tm, D), acc_spec, acc_spec],
        out_shape=[jax.ShapeDtypeStruct((S, D), F32), jax.ShapeDtypeStruct((nb, 8, D), F32),
                   jax.ShapeDtypeStruct((nb, 8, D), F32)],
        compiler_params=_params("arbitrary"),
    )(x, dxn, scales, *dhs)


FFN_PAIRS = 2
FFN_SUB_ROWS = 256


def _ffn_in_act(h, w, layer, name):
    S, D = h.shape
    F2 = w.shape[2]
    PW = F2 // (2 * FFN_PAIRS)
    tm = _pick(S, 512, 16)
    sub = _pick(tm, FFN_SUB_ROWS, 16)

    def body(h_ref, w_ref, gu_ref, a_ref):
        for r in range(tm // sub):
            rows = pl.ds(r * sub, sub)
            acc = jnp.dot(h_ref[rows, :], w_ref[...], preferred_element_type=F32)
            gu_ref[rows, :] = acc.astype(BF16)
            g = acc[:, :PW]
            a_ref[rows, :] = (g * jax.nn.sigmoid(g) * acc[:, PW:]).astype(BF16)

    return pl.pallas_call(
        body, name=name, grid=(FFN_PAIRS, S // tm),
        in_specs=[pl.BlockSpec((tm, D), lambda p, i: (i, 0)),
                  pl.BlockSpec((None, D, 2 * PW), lambda p, i: (layer, 0, p))],
        out_specs=[pl.BlockSpec((tm, 2 * PW), lambda p, i: (i, p)), pl.BlockSpec((tm, PW), lambda p, i: (i, p))],
        out_shape=[jax.ShapeDtypeStruct((S, F2), BF16), jax.ShapeDtypeStruct((S, F2 // 2), BF16)],
        compiler_params=_params("parallel", "parallel"),
    )(h, w)


def _ffn_out_dx_act(dy, w, layer, gu, name):
    S, D = dy.shape
    F2 = gu.shape[1]
    PW = F2 // (2 * FFN_PAIRS)
    tm = _pick(S, 512, 16)
    sub = _pick(tm, FFN_SUB_ROWS, 16)

    def body(dy_ref, w_ref, gu_ref, o_ref):
        for r in range(tm // sub):
            rows = pl.ds(r * sub, sub)
            da = lax.dot_general(dy_ref[rows, :], w_ref[...], _DIMS["nt"], preferred_element_type=F32)
            g = gu_ref[rows, 0:PW].astype(F32)
            u = gu_ref[rows, PW:2 * PW].astype(F32)
            sg = jax.nn.sigmoid(g)
            o_ref[rows, 0:PW] = (da * u * (sg * (1.0 + g * (1.0 - sg)))).astype(BF16)
            o_ref[rows, PW:2 * PW] = (da * (g * sg)).astype(BF16)

    return pl.pallas_call(
        body, name=name, grid=(FFN_PAIRS, S // tm),
        in_specs=[pl.BlockSpec((tm, D), lambda p, i: (i, 0)),
                  pl.BlockSpec((None, PW, D), lambda p, i: (layer, p, 0)),
                  pl.BlockSpec((tm, 2 * PW), lambda p, i: (i, p))],
        out_specs=pl.BlockSpec((tm, 2 * PW), lambda p, i: (i, p)),
        out_shape=jax.ShapeDtypeStruct((S, F2), BF16),
        compiler_params=_params("parallel", "parallel"),
    )(dy, w, gu)


HALO = 16


def _conv_terms(bcx_ref, prev_ref, i, tm, D):
    b = bcx_ref[:, 0:D].astype(F32)
    cg = bcx_ref[:, D:2 * D].astype(F32)
    xin = bcx_ref[:, 2 * D:3 * D].astype(F32)
    z = cg * xin
    zp = prev_ref[:, D:2 * D].astype(F32) * prev_ref[:, 2 * D:3 * D].astype(F32)
    zp = jnp.where(i > 0, zp, 0.0)
    row = lax.broadcasted_iota(jnp.int32, (tm, D), 0)
    p1, p2 = zp[HALO - 1:HALO, :], zp[HALO - 2:HALO - 1, :]
    z1 = jnp.where(row == 0, p1, pltpu.roll(z, 1, 0))
    z2 = jnp.where(row == 0, p2, jnp.where(row == 1, p1, pltpu.roll(z, 2, 0)))
    return b, cg, xin, z, z1, z2, row


def _conv_gate(bcx, ck, name):
    S, D3 = bcx.shape
    D = D3 // 3
    tm = _pick(S, 256, 16)
    hb = tm // HALO

    def body(bcx_ref, prev_ref, ck_ref, o_ref):
        i = pl.program_id(0)
        b, _, _, z, z1, z2, _ = _conv_terms(bcx_ref, prev_ref, i, tm, D)
        conv = ck_ref[0:1, :] * z2 + ck_ref[1:2, :] * z1 + ck_ref[2:3, :] * z
        o_ref[...] = (b * conv).astype(BF16)

    return pl.pallas_call(
        body, name=name, grid=(S // tm,),
        in_specs=[_row_spec(tm, D3),
                  pl.BlockSpec((HALO, D3), lambda i: (jnp.maximum(i * hb - 1, 0), 0)),
                  _vec_spec(8, D)],
        out_specs=_row_spec(tm, D),
        out_shape=jax.ShapeDtypeStruct((S, D), BF16),
        compiler_params=_params("parallel"),
    )(bcx, bcx, ck)


def _conv_gate_bwd(du, bcx, ck, name):
    S, D3 = bcx.shape
    D = D3 // 3
    tm = _pick(S, 256, 16)
    hb = tm // HALO
    nt = S // tm

    def body(du_ref, dun_ref, bcx_ref, prev_ref, next_ref, ck_ref, o_ref, dk_ref):
        i = pl.program_id(0)
        b, cg, xin, z, z1, z2, row = _conv_terms(bcx_ref, prev_ref, i, tm, D)
        k0, k1, k2 = ck_ref[0:1, :], ck_ref[1:2, :], ck_ref[2:3, :]
        conv = k0 * z2 + k1 * z1 + k2 * z
        d = du_ref[...].astype(F32)
        dconv = d * b
        dcn = jnp.where(i < nt - 1, dun_ref[...].astype(F32) * next_ref[:, 0:D].astype(F32), 0.0)
        d1 = jnp.where(row == tm - 1, dcn[0:1, :], pltpu.roll(dconv, tm - 1, 0))
        d2 = jnp.where(row == tm - 2, dcn[0:1, :],
                       jnp.where(row == tm - 1, dcn[1:2, :], pltpu.roll(dconv, tm - 2, 0)))
        dz = k2 * dconv + k1 * d1 + k0 * d2
        o_ref[:, 0:D] = (d * conv).astype(BF16)
        o_ref[:, D:2 * D] = (dz * xin).astype(BF16)
        o_ref[:, 2 * D:3 * D] = (dz * cg).astype(BF16)

        @pl.when(i == 0)
        def _():
            dk_ref[...] = jnp.zeros_like(dk_ref)

        dk_ref[0] += _colsum8(dconv * z2)
        dk_ref[1] += _colsum8(dconv * z1)
        dk_ref[2] += _colsum8(dconv * z)

    last = S // HALO - 1
    return pl.pallas_call(
        body, name=name, grid=(nt,),
        in_specs=[_row_spec(tm, D),
                  pl.BlockSpec((HALO, D), lambda i: (jnp.minimum((i + 1) * hb, last), 0)),
                  _row_spec(tm, D3),
                  pl.BlockSpec((HALO, D3), lambda i: (jnp.maximum(i * hb - 1, 0), 0)),
                  pl.BlockSpec((HALO, D3), lambda i: (jnp.minimum((i + 1) * hb, last), 0)),
                  _vec_spec(8, D)],
        out_specs=[_row_spec(tm, D3), pl.BlockSpec((3, 8, D), lambda i: (0, 0, 0))],
        out_shape=[jax.ShapeDtypeStruct((S, D3), BF16), jax.ShapeDtypeStruct((3, 8, D), F32)],
        compiler_params=_params("arbitrary"),
    )(du, du, bcx, bcx, bcx, ck)


def _rel_onehot():
    a = np.arange(CHUNK)[:, None]
    b = np.arange(CHUNK)[None, :]
    idx = np.stack([np.clip((N_LEFT_CHUNKS - dl) * CHUNK + a - b, -MAX_REL, MAX_REL) + MAX_REL
                    for dl in (6, 7, 8)]).reshape(-1)
    return (jnp.asarray(idx)[:, None] == jnp.arange(N_REL)[None, :]).astype(F32)


def _bias_table(rel_bias, name):
    H = rel_bias.shape[0]
    near = jnp.dot(rel_bias, _rel_onehot().T, precision=lax.Precision.HIGHEST).reshape(H, 3, CHUNK, CHUNK)
    far = jnp.broadcast_to(rel_bias[:, N_REL - 1][:, None, None], (H, CHUNK, CHUNK))

    def body(near_ref, far_ref, o_ref):
        neg = jnp.full((CHUNK, CHUNK), NEG, F32)
        for v in range(N_WIN):
            for ic in range(Q_CHUNKS):
                for jc in range(N_WIN * Q_CHUNKS):
                    dl = jc - ic
                    if dl < 0 or dl > N_LEFT_CHUNKS or jc < (N_WIN - 1 - v) * Q_CHUNKS:
                        blk = neg
                    else:
                        blk = far_ref[...] if dl <= 5 else near_ref[dl - 6]
                    o_ref[v, ic * CHUNK:(ic + 1) * CHUNK, jc * CHUNK:(jc + 1) * CHUNK] = blk

    return pl.pallas_call(
        body, name=name, grid=(H,),
        in_specs=[pl.BlockSpec((None, 3, CHUNK, CHUNK), lambda h: (h, 0, 0, 0)),
                  pl.BlockSpec((None, CHUNK, CHUNK), lambda h: (h, 0, 0))],
        out_specs=pl.BlockSpec((N_WIN, None, BQ, N_WIN * BQ), lambda h: (0, h, 0, 0)),
        out_shape=jax.ShapeDtypeStruct((N_WIN, H, BQ, N_WIN * BQ), F32),
        compiler_params=_params("parallel"),
    )(near, far)


def _bias_table_grad(dtab):
    H = dtab.shape[0]
    blk = lambda ic, jc: dtab[:, ic * CHUNK:(ic + 1) * CHUNK, jc * CHUNK:(jc + 1) * CHUNK]
    by_dl = [sum(blk(ic, ic + dl) for ic in range(Q_CHUNKS)) for dl in range(N_LEFT_CHUNKS + 1)]
    far = sum(jnp.sum(by_dl[dl], axis=(1, 2)) for dl in range(6))
    near = jnp.stack(by_dl[6:9], axis=1).reshape(H, 3 * CHUNK * CHUNK)
    g = jnp.dot(near, _rel_onehot(), precision=lax.Precision.HIGHEST)
    return g.at[:, N_REL - 1].add(far)


def _attn_specs(D, W):
    q_spec = pl.BlockSpec((BQ, W), lambda g, i: (i, g))

    def win(w, off):
        return pl.BlockSpec((BQ, W), lambda g, i: (jnp.maximum(i - (N_WIN - 1) + w, 0), off + g))

    k_specs = [win(w, 0) for w in range(N_WIN)]
    v_specs = [win(w, D // W) for w in range(N_WIN)]
    tab_spec = pl.BlockSpec((None, HEADS_PER_STEP, BQ, N_WIN * BQ),
                            lambda g, i: (jnp.minimum(i, N_WIN - 1), g, 0, 0))
    dtab_spec = pl.BlockSpec((HEADS_PER_STEP, BQ, N_WIN * BQ), lambda g, i: (g, 0, 0))
    return q_spec, k_specs, v_specs, tab_spec, dtab_spec


def _attn_exp(q_ref, kw, tab_ref, h, dh):
    qh = q_ref[:, h * dh:(h + 1) * dh]
    kh = kw[:, h * dh:(h + 1) * dh]
    s = lax.dot_general(qh, kh, _DIMS["nt"], preferred_element_type=F32) + tab_ref[h]
    e = jnp.exp(s - jnp.max(s, axis=-1, keepdims=True))
    return e, jnp.sum(e, axis=-1, keepdims=True), qh, kh


def _attn_fwd(q, kv, tab, name):
    S, D = q.shape
    dh = D // N_HEADS
    W = HEADS_PER_STEP * dh
    q_spec, k_specs, v_specs, tab_spec, _ = _attn_specs(D, W)

    def body(q_ref, *rest):
        k_refs, v_refs = rest[:N_WIN], rest[N_WIN:2 * N_WIN]
        tab_ref, o_ref = rest[2 * N_WIN], rest[2 * N_WIN + 1]
        kw = jnp.concatenate([r[...] for r in k_refs], axis=0)
        vw = jnp.concatenate([r[...] for r in v_refs], axis=0)
        outs = []
        for h in range(HEADS_PER_STEP):
            e, l, _, _ = _attn_exp(q_ref, kw, tab_ref, h, dh)
            outs.append(jnp.dot(e.astype(BF16), vw[:, h * dh:(h + 1) * dh], preferred_element_type=F32) / l)
        o_ref[...] = jnp.concatenate(outs, axis=1).astype(BF16)

    return pl.pallas_call(
        body, name=name, grid=(N_HEADS // HEADS_PER_STEP, S // BQ),
        in_specs=[q_spec] + k_specs + v_specs + [tab_spec],
        out_specs=q_spec,
        out_shape=jax.ShapeDtypeStruct((S, D), BF16),
        compiler_params=_params("parallel", "parallel"),
    )(q, *([kv] * (2 * N_WIN)), tab)


def _attn_bwd(q, kv, tab, do, name):
    S, D = q.shape
    dh = D // N_HEADS
    W = HEADS_PER_STEP * dh
    q_spec, k_specs, v_specs, tab_spec, dtab_spec = _attn_specs(D, W)

    def body(q_ref, *rest):
        k_refs, v_refs = rest[:N_WIN], rest[N_WIN:2 * N_WIN]
        tab_ref, do_ref, dq_ref = rest[2 * N_WIN:2 * N_WIN + 3]
        dk_refs = rest[2 * N_WIN + 3:3 * N_WIN + 3]
        dv_refs = rest[3 * N_WIN + 3:4 * N_WIN + 3]
        dtab_ref = rest[4 * N_WIN + 3]
        i = pl.program_id(1)

        @pl.when(i == 0)
        def _():
            dtab_ref[...] = jnp.zeros_like(dtab_ref)

        kw = jnp.concatenate([r[...] for r in k_refs], axis=0)
        vw = jnp.concatenate([r[...] for r in v_refs], axis=0)
        dqs, dks, dvs = [], [], []
        for h in range(HEADS_PER_STEP):
            e, l, qh, kh = _attn_exp(q_ref, kw, tab_ref, h, dh)
            p = e * (1.0 / l)
            vh = vw[:, h * dh:(h + 1) * dh]
            doh = do_ref[:, h * dh:(h + 1) * dh]
            dp = lax.dot_general(doh, vh, _DIMS["nt"], preferred_element_type=F32)
            ds = p * (dp - jnp.sum(p * dp, axis=-1, keepdims=True))
            dtab_ref[h] += ds
            dsb = ds.astype(BF16)
            dqs.append(jnp.dot(dsb, kh, preferred_element_type=F32) * (dh ** -0.5))
            dks.append(lax.dot_general(dsb, qh, _DIMS["tn"], preferred_element_type=F32))
            dvs.append(lax.dot_general(p.astype(BF16), doh, _DIMS["tn"], preferred_element_type=F32))
        dq_ref[...] = jnp.concatenate(dqs, axis=1).astype(BF16)
        dk = jnp.concatenate(dks, axis=1)
        dv = jnp.concatenate(dvs, axis=1)
        for w in range(N_WIN):
            dk_refs[w][...] = dk[w * BQ:(w + 1) * BQ, :].astype(BF16)
            dv_refs[w][...] = dv[w * BQ:(w + 1) * BQ, :].astype(BF16)

    part = jax.ShapeDtypeStruct((S, D), BF16)
    outs = pl.pallas_call(
        body, name=name, grid=(N_HEADS // HEADS_PER_STEP, S // BQ),
        in_specs=[q_spec] + k_specs + v_specs + [tab_spec, q_spec],
        out_specs=[q_spec] + [q_spec] * (2 * N_WIN) + [dtab_spec],
        out_shape=[jax.ShapeDtypeStruct((S, D), BF16)] + [part] * (2 * N_WIN)
        + [jax.ShapeDtypeStruct(tab.shape[1:], F32)],
        compiler_params=_params("parallel", "arbitrary"),
    )(q, *([kv] * (2 * N_WIN)), tab, do)
    return outs[0], outs[1:1 + N_WIN], outs[1 + N_WIN:1 + 2 * N_WIN], outs[-1]


def _kv_grad_combine(dk_parts, dv_parts, name):
    S, D = dk_parts[0].shape
    nblk = S // BQ

    def body(*refs):
        o_ref = refs[2 * N_WIN]
        i = pl.program_id(0)
        for half, parts in enumerate((refs[:N_WIN], refs[N_WIN:2 * N_WIN])):
            acc = parts[N_WIN - 1][...].astype(F32)
            for w in range(N_WIN - 1):
                acc = acc + jnp.where(i + (N_WIN - 1 - w) < nblk, parts[w][...].astype(F32), 0.0)
            o_ref[:, half * D:(half + 1) * D] = acc.astype(BF16)

    specs = [pl.BlockSpec((BQ, D), functools.partial(
        lambda i, sh: (jnp.minimum(i + sh, nblk - 1), 0), sh=N_WIN - 1 - w)) for w in range(N_WIN)]
    return pl.pallas_call(
        body, name=name, grid=(nblk,),
        in_specs=specs + specs,
        out_specs=pl.BlockSpec((BQ, 2 * D), lambda i: (i, 0)),
        out_shape=jax.ShapeDtypeStruct((S, 2 * D), BF16),
        compiler_params=_params("parallel"),
    )(*dk_parts, *dv_parts)


def _adamw(w, g, m, v, name):
    shape = w.shape
    C = shape[-1]
    R = int(np.prod(shape[:-1])) if len(shape) > 1 else 1
    w2, g2, m2, v2 = (t.reshape(R, C) for t in (w, g, m, v))
    tr = _pick(R, max(8, (512 * 1024) // C // 8 * 8), 8)

    def body(w_ref, g_ref, m_ref, v_ref, d_ref, nm_ref, nv_ref):
        gv = g_ref[...]
        nm = ADAM_B1 * m_ref[...] + (1.0 - ADAM_B1) * gv
        nv = ADAM_B2 * v_ref[...] + (1.0 - ADAM_B2) * jnp.square(gv)
        m_hat = nm / (1.0 - ADAM_B1 ** ADAM_STEP)
        v_hat = nv / (1.0 - ADAM_B2 ** ADAM_STEP)
        d_ref[...] = -ADAM_LR * (m_hat / (jnp.sqrt(v_hat) + ADAM_EPS) + ADAM_WD * w_ref[...])
        nm_ref[...] = nm
        nv_ref[...] = nv

    spec = pl.BlockSpec((tr, C), lambda i: (i, 0))
    outs = pl.pallas_call(
        body, name=name, grid=(R // tr,),
        in_specs=[spec] * 4, out_specs=[spec] * 3,
        out_shape=[jax.ShapeDtypeStruct((R, C), F32)] * 3,
        compiler_params=_params("parallel"),
    )(w2, g2, m2, v2)
    return tuple(o.reshape(shape) for o in outs)


def _sum_rows(a, name):
    n, L = a.shape

    def body(a_ref, o_ref):
        acc = a_ref[0:1, :]
        for r in range(1, n):
            acc = acc + a_ref[r:r + 1, :]
        o_ref[...] = acc

    return pl.pallas_call(
        body, name=name, grid=(1,),
        in_specs=[pl.BlockSpec((n, L), lambda i: (0, 0))],
        out_specs=pl.BlockSpec((1, L), lambda i: (0, 0)),
        out_shape=jax.ShapeDtypeStruct((1, L), F32),
        compiler_params=_params("arbitrary"),
    )(a)


def _scalar_call(body, name, scalar, grid, in_specs, out_spec, out_shape, args):
    return pl.pallas_call(
        body, name=name,
        grid_spec=pltpu.PrefetchScalarGridSpec(num_scalar_prefetch=1, grid=grid, in_specs=in_specs,
                                               out_specs=out_spec),
        out_shape=out_shape, compiler_params=_params("parallel"),
    )(jnp.reshape(scalar, (-1,)).astype(jnp.int32), *args)


def _pair_sum(view, got, c, name):
    nb, _, rh, cols = view.shape
    tr = _pick(rh, max(16, (1 << 20) // cols // 16 * 16), 16)
    bpr = rh // tr

    def body(s_ref, a_ref, b_ref, o_ref):
        o_ref[...] = (a_ref[...].astype(F32) + b_ref[...].astype(F32)).astype(BF16)

    spec = pl.BlockSpec((tr, cols), lambda i, s: (i, 0))
    mine = pl.BlockSpec((tr, cols), lambda i, s: ((2 * (i // bpr) + s[0]) * bpr + i % bpr, 0))
    return _scalar_call(body, name, c, (nb * bpr,), [mine, spec], spec,
                        jax.ShapeDtypeStruct((nb * rh, cols), BF16),
                        (view.reshape(nb * 2 * rh, cols), got.reshape(nb * rh, cols)))


def _owner_sum(pair, recv, me, c, it, name):
    _, rh, bc = recv.shape
    tr = _pick(rh, max(16, (1 << 19) // bc // 16 * 16), 16)
    bpr = rh // tr

    def body(s_ref, a_ref, r0, r1, r2, o_ref):
        o_ref[...] = ((a_ref[...].astype(F32) + r0[...].astype(F32)) + r1[...].astype(F32)) + r2[...].astype(F32)

    if it.kind == "col":
        own = pl.BlockSpec((tr, bc), lambda i, s: (i, s[0]))
    else:
        own = pl.BlockSpec((tr, bc), lambda i, s: (s[0] * bpr + i, 0))
    slots = [pl.BlockSpec((None, tr, bc), functools.partial(lambda i, s, k: (k, i, 0), k=k)) for k in range(3)]
    return _scalar_call(body, name, jnp.stack([it.pos(me), c]), (bpr,), [own] + slots,
                        pl.BlockSpec((tr, bc), lambda i, s: (s[1] * bpr + i, 0)),
                        jax.ShapeDtypeStruct((2 * rh, bc), F32), (pair, recv, recv, recv))


def _place():
    x, y, c = lax.axis_index("x"), lax.axis_index("y"), lax.axis_index("c")
    chips = [(1 - x, y), (x, 1 - y), (1 - x, 1 - y)]
    return x, y, c, chips


def _chip_index(px, py):
    return 2 * px + py


def _all_gather_small(x_shard, name):
    m_per, n = x_shard.shape

    def body(x_ref, out_ref, send_sems, recv_sems, local_sem):
        x, y, c, chips = _place()
        me, sibling = (x, y, c), (x, y, 1 - c)

        def rows(px, py, pc):
            return out_ref.at[pl.ds((4 * px + 2 * py + pc) * m_per, m_per), :]

        def copy(k, block, to, src=None):
            return pltpu.make_async_remote_copy(
                src_ref=rows(*block) if src is None else src, dst_ref=rows(*block),
                send_sem=send_sems.at[k], recv_sem=recv_sems.at[k], device_id=to, device_id_type=MESH)

        mine = pltpu.make_async_copy(x_ref, rows(*me), local_sem)
        mine.start()
        first = [copy(0, me, sibling, src=x_ref)]
        first += [copy(1 + j, me, (*chip, c), src=x_ref) for j, chip in enumerate(chips)]
        for cp in first:
            cp.start()
        passed = [copy(4 + j, (*chip, c), sibling) for j, chip in enumerate(chips)]
        for j, chip in enumerate(chips):
            copy(1 + j, (*chip, c), me).wait_recv()
            passed[j].start()
        copy(0, sibling, me).wait_recv()
        for j, chip in enumerate(chips):
            copy(4 + j, (*chip, 1 - c), me).wait_recv()
        for cp in first + passed:
            cp.wait_send()
        mine.wait()

    return pl.pallas_call(
        body, name=name,
        out_shape=jax.ShapeDtypeStruct((N_DEV * m_per, n), x_shard.dtype),
        in_specs=[pl.BlockSpec(memory_space=pltpu.VMEM)],
        out_specs=pl.BlockSpec(memory_space=pltpu.VMEM),
        scratch_shapes=[pltpu.SemaphoreType.DMA((7,)), pltpu.SemaphoreType.DMA((7,)), pltpu.SemaphoreType.DMA],
    )(x_shard)


def _gather_flat(vec, name):
    L = vec.shape[0]
    Lp = -(-L // 1024) * 1024
    g = _all_gather_small(jnp.pad(vec, (0, Lp - L)).reshape(8, Lp // 8), name)
    return g.reshape(N_DEV, Lp)[:, :L]


class _Item:
    def __init__(self, kind, rows, cols, arg, layer, swap=False):
        self.kind, self.rows, self.cols, self.arg, self.layer, self.swap = kind, rows, cols, arg, layer, swap

    def ref(self, refs):
        return refs[self.arg].at[self.layer]

    def pos(self, j):
        return 2 * (j % 2) + j // 2 if self.swap else j


def _block(ref, it, j, half):
    if it.kind == "col":
        ns = it.cols // N_CHIP
        return ref.at[pl.ds(half * (it.rows // 2), it.rows // 2), pl.ds(it.pos(j) * ns, ns)]
    rs = it.rows // N_CHIP
    return ref.at[pl.ds(j * rs + half * (rs // 2), rs // 2), :]


def _cast_place(w, layer, kind, pos, name):
    _, r, n = w.shape
    tr = _pick(r, max(16, (1 << 20) // n // 16 * 16), 16)
    bpr = r // tr

    def body(s_ref, w_ref, o_ref):
        o_ref[...] = w_ref[...].astype(BF16)

    if kind == "col":
        full, out_idx = (1, r, N_CHIP * n), (lambda i, s: (0, i, s[0]))
    else:
        full, out_idx = (1, N_CHIP * r, n), (lambda i, s: (0, s[0] * bpr + i, 0))
    return pl.pallas_call(
        body, name=name,
        grid_spec=pltpu.PrefetchScalarGridSpec(
            num_scalar_prefetch=1, grid=(bpr,),
            in_specs=[pl.BlockSpec((None, tr, n), lambda i, s: (layer, i, 0))],
            out_specs=pl.BlockSpec((None, tr, n), out_idx)),
        out_shape=jax.ShapeDtypeStruct(full, BF16),
        compiler_params=_params("parallel"),
    )(jnp.reshape(pos, (1,)).astype(jnp.int32), w)


HBM_SPEC = pl.BlockSpec(memory_space=pltpu.HBM)
SEM_SPEC = pl.BlockSpec(memory_space=pltpu.SEMAPHORE)
ANY_SPEC = pl.BlockSpec(memory_space=pl.ANY)
SPLIT_PARAMS = dict(has_side_effects=pltpu.SideEffectType.DATAFLOW_SIDE_EFFECTING)


def _in_hbm(a):
    return pltpu.with_memory_space_constraint(a, pltpu.HBM)


def _split_start(copies_of, bufs, n_sem, after, name):
    n = len(bufs)

    def body(*refs):
        ins, send, recv, token = refs[:n], refs[n + 1], refs[n + 2], refs[2 * n + 3]
        for cp in copies_of(ins, send, recv, False)[0]:
            cp.start()
        token[...] = jnp.zeros_like(token)

    outs = pl.pallas_call(
        body, name=name,
        out_shape=(pltpu.SemaphoreType.DMA(n_sem), pltpu.SemaphoreType.DMA(n_sem),
                   *[pltpu.HBM(b.shape, b.dtype) for b in bufs], jax.ShapeDtypeStruct((8, 128), F32)),
        in_specs=[HBM_SPEC] * n + [ANY_SPEC],
        out_specs=(SEM_SPEC, SEM_SPEC, *[HBM_SPEC] * n, pl.BlockSpec(memory_space=pltpu.VMEM)),
        input_output_aliases={t: 2 + t for t in range(n)},
        compiler_params=pltpu.CompilerParams(**SPLIT_PARAMS),
    )(*[_in_hbm(b) for b in bufs], after)
    return outs[0], outs[1], list(outs[2:2 + n]), outs[2 + n]


def _split_wait(copies_of, send, recv, bufs, after, name):
    n = len(bufs)

    def body(*refs):
        ins, send_ref, recv_ref = refs[:n], refs[n], refs[n + 1]
        sends, arrivals = copies_of(ins, send_ref, recv_ref, True)
        for cp in sends:
            cp.wait_send()
        for cp in arrivals:
            cp.wait_recv()

    return pl.pallas_call(
        body, name=name,
        out_shape=[pltpu.HBM(b.shape, b.dtype) for b in bufs],
        in_specs=[HBM_SPEC] * n + [SEM_SPEC, SEM_SPEC, ANY_SPEC],
        out_specs=[HBM_SPEC] * n,
        input_output_aliases={t: t for t in range(n)},
        compiler_params=pltpu.CompilerParams(**SPLIT_PARAMS),
    )(*bufs, send, recv, after)


def _gather_copies(items):
    def copies_of(refs, send, recv, with_arrivals):
        x, y, c, chips = _place()
        me = _chip_index(x, y)
        sends, arrivals = [], []
        for t, it in enumerate(items):
            for k, chip in enumerate(chips):
                for core in range(2):
                    mine = _block(it.ref(refs), it, me, c)
                    sends.append(pltpu.make_async_remote_copy(
                        src_ref=mine, dst_ref=mine, send_sem=send.at[6 * t + 2 * k + core],
                        recv_sem=recv.at[6 * t + 2 * k + c], device_id=(*chip, core), device_id_type=MESH))
                    if with_arrivals:
                        landed = _block(it.ref(refs), it, _chip_index(*chip), core)
                        arrivals.append(pltpu.make_async_remote_copy(
                            src_ref=landed, dst_ref=landed, send_sem=send.at[6 * t + 2 * k + core],
                            recv_sem=recv.at[6 * t + 2 * k + core], device_id=(*chip, core), device_id_type=MESH))
        return sends, arrivals

    return copies_of


def _owner_copies(items):
    n = len(items)

    def blk(ref, it, j):
        if it.kind == "col":
            ns = it.cols // N_CHIP
            return ref.at[:, pl.ds(it.pos(j) * ns, ns)]
        return ref.at[j]

    def copies_of(refs, send, recv, with_arrivals):
        x, y, c, chips = _place()
        sends, arrivals = [], []
        for t, it in enumerate(items):
            for k, chip in enumerate(chips):
                slot = refs[n + t].at[k]
                sends.append(pltpu.make_async_remote_copy(
                    src_ref=blk(refs[t], it, _chip_index(*chip)), dst_ref=slot, send_sem=send.at[3 * t + k],
                    recv_sem=recv.at[3 * t + k], device_id=(*chip, c), device_id_type=MESH))
                if with_arrivals:
                    arrivals.append(pltpu.make_async_remote_copy(
                        src_ref=slot, dst_ref=slot, send_sem=send.at[3 * t + k], recv_sem=recv.at[3 * t + k],
                        device_id=(*chip, c), device_id_type=MESH))
        return sends, arrivals

    return copies_of


def _owner_slot_shape(it):
    if it.kind == "col":
        return (3, it.rows // 2, it.cols // N_CHIP)
    return (3, it.rows // (2 * N_CHIP), it.cols)


def _pair_view(g, it):
    if it.kind == "col":
        return g.reshape(1, 2, it.rows // 2, it.cols)
    return g.reshape(N_CHIP, 2, it.rows // (2 * N_CHIP), it.cols)


def _pair_exchange(views, name):
    n = len(views)

    def body(*refs):
        ins, outs, send, recv = refs[:n], refs[n:2 * n], refs[2 * n], refs[2 * n + 1]
        x, y, c, _ = _place()
        cps = []
        for t in range(n):
            cp = pltpu.make_async_remote_copy(
                src_ref=ins[t].at[:, pl.ds(1 - c, 1)], dst_ref=outs[t],
                send_sem=send.at[t], recv_sem=recv.at[t], device_id=(x, y, 1 - c), device_id_type=MESH)
            cp.start()
            cps.append(cp)
        for cp in cps:
            cp.wait()

    any_spec = pl.BlockSpec(memory_space=pl.ANY)
    return pl.pallas_call(
        body, name=name,
        out_shape=[jax.ShapeDtypeStruct((v.shape[0], 1) + v.shape[2:], v.dtype) for v in views],
        in_specs=[any_spec] * n, out_specs=[any_spec] * n,
        scratch_shapes=[pltpu.SemaphoreType.DMA((n,)), pltpu.SemaphoreType.DMA((n,))],
    )(*views)


def _half_exchange(bufs, name):
    n = len(bufs)

    def body(*refs):
        ins, outs, send, recv = refs[:n], refs[n:2 * n], refs[2 * n], refs[2 * n + 1]
        x, y, c, _ = _place()
        cps = []
        for t in range(n):
            r2 = ins[t].shape[0] // 2
            cp = pltpu.make_async_remote_copy(
                src_ref=ins[t].at[pl.ds(c * r2, r2), :], dst_ref=outs[t].at[pl.ds(c * r2, r2), :],
                send_sem=send.at[t], recv_sem=recv.at[t], device_id=(x, y, 1 - c), device_id_type=MESH)
            cp.start()
            cps.append(cp)
        for t in range(n):
            r2 = ins[t].shape[0] // 2
            theirs = outs[t].at[pl.ds((1 - c) * r2, r2), :]
            pltpu.make_async_remote_copy(
                src_ref=theirs, dst_ref=theirs, send_sem=send.at[t], recv_sem=recv.at[t],
                device_id=(x, y, 1 - c), device_id_type=MESH).wait_recv()
        for cp in cps:
            cp.wait_send()

    any_spec = pl.BlockSpec(memory_space=pl.ANY)
    return pl.pallas_call(
        body, name=name,
        out_shape=[jax.ShapeDtypeStruct(b.shape, b.dtype) for b in bufs],
        in_specs=[any_spec] * n, out_specs=[any_spec] * n,
        input_output_aliases={t: t for t in range(n)},
        scratch_shapes=[pltpu.SemaphoreType.DMA((n,)), pltpu.SemaphoreType.DMA((n,))],
    )(*bufs)


class _Reduction:
    pass


def _reduce_start(grads, items, after, tag):
    x, y, c, _ = _place()
    views = [_pair_view(g, it) for g, it in zip(grads, items)]
    got = _pair_exchange(views, f"rs_pair_exchange_{tag}")
    pairs = [_pair_sum(v, r, c, f"rs_pair_sum_{tag}_{t}") for t, (v, r) in enumerate(zip(views, got))]
    shaped = [p if it.kind == "col" else p.reshape(N_CHIP, p.shape[0] // N_CHIP, p.shape[1])
              for p, it in zip(pairs, items)]
    lands = [lax.empty(_owner_slot_shape(it), BF16) for it in items]
    r = _Reduction()
    r.items, r.tag = items, tag
    r.send, r.recv, r.bufs, r.token = _split_start(
        _owner_copies(items), shaped + lands, (3 * len(items),), after, f"rs_owner_start_{tag}")
    return r


def _reduce_finish(groups, after):
    x, y, c, _ = _place()
    me = _chip_index(x, y)
    halves = []
    for r in groups:
        n = len(r.items)
        bufs = _split_wait(_owner_copies(r.items), r.send, r.recv, r.bufs, after, f"rs_owner_wait_{r.tag}")
        for t, it in enumerate(r.items):
            pair = bufs[t].reshape(-1, bufs[t].shape[-1])
            halves.append(_owner_sum(pair, bufs[n + t], me, c, it, f"rs_owner_sum_{r.tag}_{t}"))
    return _half_exchange(halves, "rs_half_exchange")


def _silu(v):
    return v * jax.nn.sigmoid(v)


def _sum8(p):
    return jnp.sum(p, axis=-2)


def kernel(x, c, mod_w, mod_b, norm_g, ffn_w_in, ffn_w_out, conv_w_in, conv_k, conv_w_out, kv_mod_w, kv_mod_b, kv_norm_g, w_kv, attn_w_q, attn_w_o, rel_bias, loss_target, m_mod_w, m_mod_b, m_norm_g, m_ffn_w_in, m_ffn_w_out, m_conv_w_in, m_conv_k, m_conv_w_out, m_kv_mod_w, m_kv_mod_b, m_kv_norm_g, m_w_kv, m_attn_w_q, m_attn_w_o, m_rel_bias, v_mod_w, v_mod_b, v_norm_g, v_ffn_w_in, v_ffn_w_out, v_conv_w_in, v_conv_k, v_conv_w_out, v_kv_mod_w, v_kv_mod_b, v_kv_norm_g, v_w_kv, v_attn_w_q, v_attn_w_o, v_rel_bias):
    xi, yi, ci = lax.axis_index("x"), lax.axis_index("y"), lax.axis_index("c")
    chip = 2 * xi + yi
    dev = 2 * chip + ci
    _, S, D = x.shape
    F = ffn_w_out.shape[1] * N_CHIP
    x0 = x.reshape(S, D)
    target = loss_target.reshape(S, D)
    n_mod = mod_w.shape[2]
    n_kvm = kv_mod_w.shape[1]
    dsh = D // N_CHIP
    TF = F // 2

    c_all = _all_gather_small(c.reshape(8, D // 8), "ag_c").reshape(N_DEV, D)
    sc16 = jnp.pad(_silu(c_all), ((0, 8), (0, 0)))
    part = [_mm(sc16, mod_w, "nn", F32, f"mod_fwd_{l}", b_layer=l)[:8] for l in range(2)]
    part.append(_mm(sc16, kv_mod_w, "nn", F32, "mod_fwd_kv")[:8])
    fwd_vec = jnp.concatenate([p.reshape(-1) for p in part] + [norm_g.reshape(-1), conv_k.reshape(-1)])
    fwd_all = _gather_flat(fwd_vec, "ag_fwd_small")[0::2]
    o = 0
    mods = []
    for n in (n_mod, n_mod, n_kvm):
        blk = fwd_all[:, o:o + 8 * n].reshape(N_CHIP, 8, n)
        mods.append(lax.dynamic_index_in_dim(blk, dev, axis=1, keepdims=False).reshape(N_CHIP * n))
        o += 8 * n
    ng = fwd_all[:, o:o + 8 * dsh].reshape(N_CHIP, 2, 4, dsh).transpose(1, 2, 0, 3).reshape(2, 4, D)
    o += 8 * dsh
    ck = fwd_all[:, o:o + 3 * dsh].reshape(N_CHIP, 3, dsh).transpose(1, 0, 2).reshape(3, D)
    ck8 = jnp.pad(ck, ((0, 5), (0, 0)))
    mod = [mods[l] + mod_b[l] for l in range(2)]
    sh1, sc1, g1, sh2, sc2, g2 = zip(*[jnp.split(m, 6) for m in mod])
    kv_sh, kv_sc = jnp.split(mods[2] + kv_mod_b, 2)
    row = lambda v: v.reshape(1, D)

    it_conv = [_Item("col", D, 3 * D, 0, 0), _Item("row", D, D, 1, 0)]
    it_ffn = [_Item("col", D, 2 * F, 0, 0, swap=True), _Item("row", F, D, 1, 0)]
    it_attn = [_Item("col", D, 2 * D, 0, 0), _Item("row", D, D, 1, 0), _Item("row", D, D, 2, 0)]

    def placed(w, layer, it, nm, after=None):
        pos = it.pos(chip)
        if after is not None:
            pos = pos + after[0, 0].astype(jnp.int32)
        return _cast_place(w, layer, it.kind, pos, f"place_{nm}")

    flying = {}

    def start(tag, its, bufs, after):
        send, recv, bufs, tok = _split_start(_gather_copies(its), bufs, (6 * len(its),), after, f"ag_start_{tag}")
        flying[tag] = (its, send, recv, bufs)
        return tok

    def arrived(tag, after):
        its, send, recv, bufs = flying[tag]
        return _split_wait(_gather_copies(its), send, recv, bufs, after, f"ag_wait_{tag}")

    tok = start("conv", it_conv, [placed(conv_w_in, 0, it_conv[0], "conv_w_in"),
                                  placed(conv_w_out, 0, it_conv[1], "conv_w_out")], fwd_all)
    tok = start("ffn0", it_ffn, [placed(ffn_w_in, 0, it_ffn[0], "ffn_w_in0", tok),
                                 placed(ffn_w_out, 0, it_ffn[1], "ffn_w_out0", tok)], tok)
    tok = start("attn", it_attn, [placed(w_kv[None], 0, it_attn[0], "w_kv", tok),
                                  placed(attn_w_q, 0, it_attn[1], "attn_w_q", tok),
                                  placed(attn_w_o, 0, it_attn[2], "attn_w_o", tok)], tok)
    token = start("ffn1", it_ffn, [placed(ffn_w_in, 1, it_ffn[0], "ffn_w_in1", tok),
                                   placed(ffn_w_out, 1, it_ffn[1], "ffn_w_out1", tok)], tok)

    a1 = row(ng[0, 0] * (1.0 + sc1[0])) + token[0, 0]
    (h1,) = _norm_mod(x0, a1, row(sh1[0]), "l0_norm1")
    tab = _bias_table(rel_bias[0], "l1_bias_table")
    h1, tab = lax.optimization_barrier((h1, tab))
    W_cin, W_cout = arrived("conv", h1)
    bcx = _mm(h1, W_cin, "nn", BF16, "l0_conv_in", b_layer=0)
    ug = _conv_gate(bcx, ck8, "l0_conv_gate")
    gt1 = row(g1[0] * ng[0, 1])
    a2 = row(ng[0, 2] * (1.0 + sc2[0]))
    y1, x1, h2 = _mm_post(ug, W_cout, x0, gt1, "l0_conv_out", scales=a2, shifts=row(sh2[0]))
    W_fin0, W_fout0 = arrived("ffn0", h2)
    gu0, act0 = _ffn_in_act(h2, W_fin0, 0, "l0_ffn_in")
    gt2 = row(g2[0] * ng[0, 3])
    a3 = ng[1, 0] * (1.0 + sc1[1])
    akv = kv_norm_g * (1.0 + kv_sc)
    y2, x2, h3, hkv = _mm_post(act0, W_fout0, x1, gt2, "l0_ffn_out",
                               scales=jnp.stack([a3, akv]), shifts=jnp.stack([sh1[1], kv_sh]))
    W_kv, W_q, W_o = arrived("attn", hkv)
    kvp = _mm(hkv, W_kv, "nn", BF16, "l1_kv", b_layer=0)
    att_scale = (D // N_HEADS) ** -0.5
    assert math.log2(att_scale) % 1 == 0, "scaling q before its bf16 cast is exact only for a power of two"
    qp = _mm(h3, W_q, "nn", BF16, "l1_q", b_layer=0, scale=att_scale)
    oh = _attn_fwd(qp, kvp, tab, "l1_attn")
    gt3 = row(g1[1] * ng[1, 1])
    a4 = row(ng[1, 2] * (1.0 + sc2[1]))
    y3, x3, h4 = _mm_post(oh, W_o, x2, gt3, "l1_attn_out", scales=a4, shifts=row(sh2[1]))
    W_fin1, W_fout1 = arrived("ffn1", h4)
    gu1, act1 = _ffn_in_act(h4, W_fin1, 0, "l1_ffn_in")
    gt4 = row(g2[1] * ng[1, 3])
    y4, dx4, sq = _mm_post(act1, W_fout1, x3, gt4, "l1_ffn_out", target=target)
    loss_part = 0.5 * jnp.sum(sq) / D

    def ffn_bwd(dxn, xin_, h, gu, act, y, gt, a, w_in, w_out, tag):
        dy, dgt = _post_norm_bwd(dxn, y, gt, f"{tag}_post2_bwd")
        dgu = _ffn_out_dx_act(dy, w_out, 0, gu, f"{tag}_ffn_out_dx")
        g_fout = _mm(act, dy, "tn", BF16, f"{tag}_ffn_out_dw", tm=TF)
        dh = _mm(dgu, w_in, "nt", BF16, f"{tag}_ffn_in_dx", b_layer=0)
        g_fin = _mm(h, dgu, "tn", BF16, f"{tag}_ffn_in_dw", tn=TF)
        dx, ds, db = _pre_norm_bwd(xin_, dxn, [dh], a, f"{tag}_norm2_bwd")
        return dx, _sum8(dgt), _sum8(ds)[0], _sum8(db)[0], g_fin, g_fout

    dx3, dgt4, da4, db4, G_fin1, G_fout1 = ffn_bwd(dx4, x3, h4, gu1, act1, y4, gt4, a4, W_fin1, W_fout1, "l1")
    red = [_reduce_start([G_fin1, G_fout1], it_ffn, token, "ffn1")]
    dy3, dgt3 = _post_norm_bwd(dx3, y3, gt3, "l1_post1_bwd")
    doh = _mm(dy3, W_o, "nt", BF16, "l1_attn_out_dx", b_layer=0)
    G_o = _mm(oh, dy3, "tn", BF16, "l1_attn_out_dw")
    dq, dk_parts, dv_parts, dtab = _attn_bwd(qp, kvp, tab, doh, "l1_attn_bwd")
    d_rel = _bias_table_grad(dtab)
    dkv = _kv_grad_combine(dk_parts, dv_parts, "l1_kv_grad")
    dh3 = _mm(dq, W_q, "nt", BF16, "l1_q_dx", b_layer=0)
    G_q = _mm(h3, dq, "tn", BF16, "l1_q_dw")
    dhkv = _mm(dkv, W_kv, "nt", BF16, "l1_kv_dx", b_layer=0)
    G_kv = _mm(hkv, dkv, "tn", BF16, "l1_kv_dw")
    red.append(_reduce_start([G_kv, G_q, G_o], it_attn, red[-1].token, "attn"))
    dx2, ds3, db3 = _pre_norm_bwd(x2, dx3, [dh3, dhkv], jnp.stack([a3, akv]), "l1_norm1_bwd")
    ds3, db3 = _sum8(ds3), _sum8(db3)

    dx1, dgt2, da2, db2, G_fin0, G_fout0 = ffn_bwd(dx2, x1, h2, gu0, act0, y2, gt2, a2, W_fin0, W_fout0, "l0")
    red.append(_reduce_start([G_fin0, G_fout0], it_ffn, red[-1].token, "ffn0"))
    dy1, dgt1 = _post_norm_bwd(dx1, y1, gt1, "l0_post1_bwd")
    dug = _mm(dy1, W_cout, "nt", BF16, "l0_conv_out_dx", b_layer=0)
    G_cout = _mm(ug, dy1, "tn", BF16, "l0_conv_out_dw")
    dbcx, dck = _conv_gate_bwd(dug, bcx, ck8, "l0_conv_gate_bwd")
    dh1 = _mm(dbcx, W_cin, "nt", BF16, "l0_conv_in_dx", b_layer=0)
    G_cin = _mm(h1, dbcx, "tn", BF16, "l0_conv_in_dw")
    red.append(_reduce_start([G_cin, G_cout], it_conv, red[-1].token, "conv"))
    dx0, ds1, db1 = _pre_norm_bwd(x0, dx1, [dh1], a1, "l0_norm1_bwd")
    ds1, db1 = _sum8(ds1)[0], _sum8(db1)[0]
    dgt1, dgt3 = _sum8(dgt1), _sum8(dgt3)

    def dmod_of(l, ds_a, db_a, dgt_a, ds_b, db_b, dgt_b):
        return jnp.concatenate([db_a, ds_a * ng[l, 0], dgt_a * ng[l, 1], db_b, ds_b * ng[l, 2], dgt_b * ng[l, 3]])

    dmod0 = dmod_of(0, ds1, db1, dgt1, da2, db2, dgt2)
    dmod1 = dmod_of(1, ds3[0], db3[0], dgt3, da4, db4, dgt4)
    dkvmod = jnp.concatenate([db3[1], ds3[1] * kv_norm_g])
    dng = jnp.stack([
        jnp.stack([ds1 * (1.0 + sc1[0]), dgt1 * g1[0], da2 * (1.0 + sc2[0]), dgt2 * g2[0]]),
        jnp.stack([ds3[0] * (1.0 + sc1[1]), dgt3 * g1[1], da4 * (1.0 + sc2[1]), dgt4 * g2[1]])])
    dkvng = ds3[1] * (1.0 + kv_sc)
    small = [dmod0, dmod1, dkvmod, dng.reshape(-1), dkvng, _sum8(dck).reshape(-1), d_rel.reshape(-1),
             loss_part.reshape(1)]
    sizes = [int(s.shape[0]) for s in small]
    offs = np.concatenate([[0], np.cumsum(sizes)])
    bwd_all = _gather_flat(jnp.concatenate(small), "ag_bwd_small")
    Lb = bwd_all.shape[1]
    Lp = -(-Lb // 128) * 128
    tot = _sum_rows(jnp.pad(bwd_all, ((0, 0), (0, Lp - Lb))), "sum_small")[0]
    seg = lambda i: tot[offs[i]:offs[i + 1]]
    g_mod_b = jnp.stack([seg(0), seg(1)])
    g_kv_mod_b = seg(2)
    g_norm_g = lax.dynamic_slice_in_dim(seg(3).reshape(2, 4, D), chip * dsh, dsh, axis=2)
    g_kv_norm_g = seg(4)
    g_conv_k = lax.dynamic_slice_in_dim(seg(5).reshape(1, 3, D), chip * dsh, dsh, axis=2)
    g_rel_bias = seg(6).reshape(rel_bias.shape)
    loss = seg(7)[0]

    def dmod_w(i, n, name):
        rows_ = lax.dynamic_slice_in_dim(bwd_all[:, offs[i]:offs[i + 1]], chip * n, n, axis=1)
        return _mm(sc16, jnp.pad(rows_, ((0, 8), (0, 0))), "tn", F32, name)

    g_mod_w = jnp.stack([dmod_w(0, n_mod, "mod_bwd_0"), dmod_w(1, n_mod, "mod_bwd_1")])
    g_kv_mod_w = dmod_w(2, n_kvm, "mod_bwd_kv")

    r_fin1, r_fout1, r_kv, r_q, r_o, r_fin0, r_fout0, r_cin, r_cout = _reduce_finish(red, dx0)
    grads = {
        "mod_w": g_mod_w, "mod_b": g_mod_b, "norm_g": g_norm_g,
        "ffn_w_in": jnp.stack([r_fin0, r_fin1]), "ffn_w_out": jnp.stack([r_fout0, r_fout1]),
        "conv_w_in": r_cin[None], "conv_k": g_conv_k, "conv_w_out": r_cout[None],
        "kv_mod_w": g_kv_mod_w, "kv_mod_b": g_kv_mod_b, "kv_norm_g": g_kv_norm_g, "w_kv": r_kv,
        "attn_w_q": r_q[None], "attn_w_o": r_o[None], "rel_bias": g_rel_bias,
    }
    weights = dict(mod_w=mod_w, mod_b=mod_b, norm_g=norm_g, ffn_w_in=ffn_w_in, ffn_w_out=ffn_w_out,
                   conv_w_in=conv_w_in, conv_k=conv_k, conv_w_out=conv_w_out, kv_mod_w=kv_mod_w,
                   kv_mod_b=kv_mod_b, kv_norm_g=kv_norm_g, w_kv=w_kv, attn_w_q=attn_w_q, attn_w_o=attn_w_o,
                   rel_bias=rel_bias)
    m_in = dict(mod_w=m_mod_w, mod_b=m_mod_b, norm_g=m_norm_g, ffn_w_in=m_ffn_w_in, ffn_w_out=m_ffn_w_out,
                conv_w_in=m_conv_w_in, conv_k=m_conv_k, conv_w_out=m_conv_w_out, kv_mod_w=m_kv_mod_w,
                kv_mod_b=m_kv_mod_b, kv_norm_g=m_kv_norm_g, w_kv=m_w_kv, attn_w_q=m_attn_w_q,
                attn_w_o=m_attn_w_o, rel_bias=m_rel_bias)
    v_in = dict(mod_w=v_mod_w, mod_b=v_mod_b, norm_g=v_norm_g, ffn_w_in=v_ffn_w_in, ffn_w_out=v_ffn_w_out,
                conv_w_in=v_conv_w_in, conv_k=v_conv_k, conv_w_out=v_conv_w_out, kv_mod_w=v_kv_mod_w,
                kv_mod_b=v_kv_mod_b, kv_norm_g=v_kv_norm_g, w_kv=v_w_kv, attn_w_q=v_attn_w_q,
                attn_w_o=v_attn_w_o, rel_bias=v_rel_bias)
    names = list(weights)
    g_out, d_out, m_out, v_out = [], [], [], []
    for n in names:
        g = grads[n].reshape(weights[n].shape)
        d, nm, nv = _adamw(weights[n], g, m_in[n], v_in[n], f"adamw_{n}")
        g_out.append(g)
        d_out.append(d)
        m_out.append(nm)
        v_out.append(nv)
    return (loss, dx0.reshape(x.shape), *g_out, *d_out, *m_out, *v_out)
```

```python
import functools
import math

import numpy as np
import jax
import jax.numpy as jnp
from jax import lax
from jax.experimental import pallas as pl
from jax.experimental.pallas import tpu as pltpu

CHUNK = 64
N_LEFT_CHUNKS = 8
N_HEADS = 16
MAX_REL = 2 * CHUNK
N_REL = 2 * MAX_REL + 1
EPS = 1e-6
ADAM_LR = 0.001
ADAM_B1 = 0.9
ADAM_B2 = 0.999
ADAM_EPS = 1e-08
ADAM_WD = 0.01
ADAM_STEP = 10

Q_CHUNKS = 4
BQ = Q_CHUNKS * CHUNK
N_WIN = 1 + N_LEFT_CHUNKS // Q_CHUNKS
HEADS_PER_STEP = 8
NEG = -1e30
N_DEV = 8
N_CHIP = 4

BF16 = jnp.bfloat16
F32 = jnp.float32
V7X_VMEM_LIMIT_BYTES = 56 * 1024 * 1024
MESH = pl.DeviceIdType.MESH


def _pick(n, pref, align):
    t = min(pref, n)
    t -= t % align
    while t >= align:
        if n % t == 0:
            return t
        t -= align
    return n


def _params(*sem):
    return pltpu.CompilerParams(dimension_semantics=sem, vmem_limit_bytes=V7X_VMEM_LIMIT_BYTES)


def _colsum8(v):
    r, d = v.shape
    return v.reshape(r // 8, 8, d).sum(axis=0)


_DIMS = {"nn": (((1,), (0,)), ((), ())), "nt": (((1,), (1,)), ((), ())), "tn": (((0,), (0,)), ((), ()))}


def _mm(a, b, mode, out_dtype, name, *, b_layer=None, tm=1024, tn=1024, tk=None, scale=None):
    if tk is None:
        tk = 2048 if mode == "tn" else 3072
    bs = b.shape[1:] if b_layer is not None else b.shape
    if mode == "nn":
        (M, K), (K2, N) = a.shape, bs
    elif mode == "nt":
        (M, K), (N, K2) = a.shape, bs
    else:
        (K, M), (K2, N) = a.shape, bs
    assert K == K2, (name, a.shape, b.shape)
    tm = _pick(M, tm, 128 if mode == "tn" else 16)
    tn = _pick(N, tn, 128)
    tk = _pick(K, tk, 128 if mode != "tn" else 16)
    nk = K // tk
    assert scale is None or nk == 1, name
    dims = _DIMS[mode]

    def body(a_ref, b_ref, o_ref, *acc):
        p = lax.dot_general(a_ref[...].astype(BF16), b_ref[...].astype(BF16), dims,
                            preferred_element_type=F32)
        if nk == 1:
            o_ref[...] = (p if scale is None else p * scale).astype(o_ref.dtype)
        else:
            k = pl.program_id(2)

            @pl.when(k == 0)
            def _():
                acc[0][...] = p

            @pl.when(k > 0)
            def _():
                acc[0][...] += p

            @pl.when(k == nk - 1)
            def _():
                o_ref[...] = acc[0][...].astype(o_ref.dtype)

    a_spec = (pl.BlockSpec((tk, tm), lambda i, j, k: (k, i)) if mode == "tn"
              else pl.BlockSpec((tm, tk), lambda i, j, k: (i, k)))
    if mode == "nt":
        b_blk, b_idx = (tn, tk), (lambda i, j, k: (j, k))
    else:
        b_blk, b_idx = (tk, tn), (lambda i, j, k: (k, j))
    if b_layer is not None:
        b_spec = pl.BlockSpec((None,) + b_blk, lambda i, j, k: (b_layer,) + b_idx(i, j, k))
    else:
        b_spec = pl.BlockSpec(b_blk, b_idx)
    return pl.pallas_call(
        body, name=name,
        grid=(M // tm, N // tn, nk),
        in_specs=[a_spec, b_spec],
        out_specs=pl.BlockSpec((tm, tn), lambda i, j, k: (i, j)),
        out_shape=jax.ShapeDtypeStruct((M, N), out_dtype),
        scratch_shapes=[pltpu.VMEM((tm, tn), F32)] if nk > 1 else [],
        compiler_params=_params("parallel", "parallel", "arbitrary"),
    )(a, b)


def _row_spec(tm, d):
    return pl.BlockSpec((tm, d), lambda i: (i, 0))


def _vec_spec(r, d):
    return pl.BlockSpec((r, d), lambda i: (0, 0))


def _norm_mod(x, scales, shifts, name):
    S, D = x.shape
    nb = scales.shape[0]
    tm = _pick(S, 512, 16)

    def body(x_ref, a_ref, b_ref, *o_refs):
        xv = x_ref[...]
        xh = xv * lax.rsqrt(jnp.mean(xv * xv, axis=-1, keepdims=True) + EPS)
        for n in range(nb):
            o_refs[n][...] = (xh * a_ref[n:n + 1, :] + b_ref[n:n + 1, :]).astype(BF16)

    return pl.pallas_call(
        body, name=name, grid=(S // tm,),
        in_specs=[_row_spec(tm, D), _vec_spec(nb, D), _vec_spec(nb, D)],
        out_specs=[_row_spec(tm, D)] * nb,
        out_shape=[jax.ShapeDtypeStruct((S, D), BF16)] * nb,
        compiler_params=_params("parallel"),
    )(x, scales, shifts)


def _mm_post(a, w, x, gate, name, *, scales=None, shifts=None, target=None):
    M, K = a.shape
    D = w.shape[2]
    tm = _pick(M, 512, 16)
    sub = _pick(tm, 256, 16)
    nb = 0 if scales is None else scales.shape[0]

    def body(a_ref, w_ref, x_ref, g_ref, *rest):
        if target is None:
            sc_ref, sh_ref, y_ref, xn_ref = rest[:4]
            h_refs = rest[4:]
        else:
            t_ref, y_ref, dx_ref, sq_ref = rest

            @pl.when(pl.program_id(0) == 0)
            def _():
                sq_ref[...] = jnp.zeros_like(sq_ref)

        for r in range(tm // sub):
            rows = pl.ds(r * sub, sub)
            yb = jnp.dot(a_ref[rows, :], w_ref[...], preferred_element_type=F32).astype(BF16)
            y_ref[rows, :] = yb
            yv = yb.astype(F32)
            yh = yv * lax.rsqrt(jnp.mean(yv * yv, axis=-1, keepdims=True) + EPS)
            xn = x_ref[rows, :] + yh * g_ref[...]
            if target is None:
                xn_ref[rows, :] = xn
                xh = xn * lax.rsqrt(jnp.mean(xn * xn, axis=-1, keepdims=True) + EPS)
                for n in range(nb):
                    h_refs[n][rows, :] = (xh * sc_ref[n:n + 1, :] + sh_ref[n:n + 1, :]).astype(BF16)
            else:
                e = xn - t_ref[rows, :]
                dx_ref[rows, :] = e / D
                sq_ref[...] += _colsum8(e * e)

    ins = [a, w, x, gate]
    in_specs = [_row_spec(tm, K), pl.BlockSpec((None, K, D), lambda i: (0, 0, 0)), _row_spec(tm, D), _vec_spec(1, D)]
    if target is None:
        ins += [scales, shifts]
        in_specs += [_vec_spec(nb, D), _vec_spec(nb, D)]
        out_specs = [_row_spec(tm, D)] * (2 + nb)
        out_shape = [jax.ShapeDtypeStruct((M, D), BF16), jax.ShapeDtypeStruct((M, D), F32)] \
            + [jax.ShapeDtypeStruct((M, D), BF16)] * nb
    else:
        ins += [target]
        in_specs += [_row_spec(tm, D)]
        out_specs = [_row_spec(tm, D), _row_spec(tm, D), _vec_spec(8, D)]
        out_shape = [jax.ShapeDtypeStruct((M, D), BF16), jax.ShapeDtypeStruct((M, D), F32),
                     jax.ShapeDtypeStruct((8, D), F32)]
    return pl.pallas_call(
        body, name=name, grid=(M // tm,), in_specs=in_specs, out_specs=out_specs, out_shape=out_shape,
        compiler_params=_params("arbitrary" if target is not None else "parallel"),
    )(*ins)


def _post_norm_bwd(dxn, y, gate, name):
    S, D = y.shape
    tm = _pick(S, 512, 16)

    def body(d_ref, y_ref, g_ref, dy_ref, dg_ref):
        yv = y_ref[...].astype(F32)
        dv = d_ref[...]
        r = lax.rsqrt(jnp.mean(yv * yv, axis=-1, keepdims=True) + EPS)
        yh = yv * r
        dyh = dv * g_ref[...]
        dy_ref[...] = (r * (dyh - yh * jnp.mean(dyh * yh, axis=-1, keepdims=True))).astype(BF16)

        @pl.when(pl.program_id(0) == 0)
        def _():
            dg_ref[...] = jnp.zeros_like(dg_ref)

        dg_ref[...] += _colsum8(dv * yh)

    return pl.pallas_call(
        body, name=name, grid=(S // tm,),
        in_specs=[_row_spec(tm, D), _row_spec(tm, D), _vec_spec(1, D)],
        out_specs=[_row_spec(tm, D), _vec_spec(8, D)],
        out_shape=[jax.ShapeDtypeStruct((S, D), BF16), jax.ShapeDtypeStruct((8, D), F32)],
        compiler_params=_params("arbitrary"),
    )(dxn, y, gate)


def _pre_norm_bwd(x, dxn, dhs, scales, name):
    S, D = x.shape
    nb = len(dhs)
    tm = _pick(S, 512, 8)

    def body(x_ref, d_ref, a_ref, *rest):
        dh_refs, dx_ref, ds_ref, db_ref = rest[:nb], rest[nb], rest[nb + 1], rest[nb + 2]
        xv = x_ref[...]
        r = lax.rsqrt(jnp.mean(xv * xv, axis=-1, keepdims=True) + EPS)
        xh = xv * r

        @pl.when(pl.program_id(0) == 0)
        def _():
            ds_ref[...] = jnp.zeros_like(ds_ref)
            db_ref[...] = jnp.zeros_like(db_ref)

        dxh = jnp.zeros_like(xv)
        for n in range(nb):
            dh = dh_refs[n][...].astype(F32)
            dxh = dxh + dh * a_ref[n:n + 1, :]
            ds_ref[n] += _colsum8(dh * xh)
            db_ref[n] += _colsum8(dh)
        dx_ref[...] = d_ref[...] + r * (dxh - xh * jnp.mean(dxh * xh, axis=-1, keepdims=True))

    acc_spec = pl.BlockSpec((nb, 8, D), lambda i: (0, 0, 0))
    return pl.pallas_call(
        body, name=name, grid=(S // tm,),
        in_specs=[_row_spec(tm, D), _row_spec(tm, D), _vec_spec(nb, D)] + [_row_spec(tm, D)] * nb,
        out_specs=[_row_spec(tm, D), acc_spec, acc_spec],
        out_shape=[jax.ShapeDtypeStruct((S, D), F32), jax.ShapeDtypeStruct((nb, 8, D), F32),
                   jax.ShapeDtypeStruct((nb, 8, D), F32)],
        compiler_params=_params("arbitrary"),
    )(x, dxn, scales, *dhs)


FFN_PAIRS = 2
FFN_SUB_ROWS = 256


def _ffn_in_act(h, w, layer, name):
    S, D = h.shape
    F2 = w.shape[2]
    PW = F2 // (2 * FFN_PAIRS)
    tm = _pick(S, 512, 16)
    sub = _pick(tm, FFN_SUB_ROWS, 16)

    def body(h_ref, w_ref, gu_ref, a_ref):
        for r in range(tm // sub):
            rows = pl.ds(r * sub, sub)
            acc = jnp.dot(h_ref[rows, :], w_ref[...], preferred_element_type=F32)
            gu_ref[rows, :] = acc.astype(BF16)
            g = acc[:, :PW]
            a_ref[rows, :] = (g * jax.nn.sigmoid(g) * acc[:, PW:]).astype(BF16)

    return pl.pallas_call(
        body, name=name, grid=(FFN_PAIRS, S // tm),
        in_specs=[pl.BlockSpec((tm, D), lambda p, i: (i, 0)),
                  pl.BlockSpec((None, D, 2 * PW), lambda p, i: (layer, 0, p))],
        out_specs=[pl.BlockSpec((tm, 2 * PW), lambda p, i: (i, p)), pl.BlockSpec((tm, PW), lambda p, i: (i, p))],
        out_shape=[jax.ShapeDtypeStruct((S, F2), BF16), jax.ShapeDtypeStruct((S, F2 // 2), BF16)],
        compiler_params=_params("parallel", "parallel"),
    )(h, w)


def _ffn_out_dx_act(dy, w, layer, gu, name):
    S, D = dy.shape
    F2 = gu.shape[1]
    PW = F2 // (2 * FFN_PAIRS)
    tm = _pick(S, 512, 16)
    sub = _pick(tm, FFN_SUB_ROWS, 16)

    def body(dy_ref, w_ref, gu_ref, o_ref):
        for r in range(tm // sub):
            rows = pl.ds(r * sub, sub)
            da = lax.dot_general(dy_ref[rows, :], w_ref[...], _DIMS["nt"], preferred_element_type=F32)
            g = gu_ref[rows, 0:PW].astype(F32)
            u = gu_ref[rows, PW:2 * PW].astype(F32)
            sg = jax.nn.sigmoid(g)
            o_ref[rows, 0:PW] = (da * u * (sg * (1.0 + g * (1.0 - sg)))).astype(BF16)
            o_ref[rows, PW:2 * PW] = (da * (g * sg)).astype(BF16)

    return pl.pallas_call(
        body, name=name, grid=(FFN_PAIRS, S // tm),
        in_specs=[pl.BlockSpec((tm, D), lambda p, i: (i, 0)),
                  pl.BlockSpec((None, PW, D), lambda p, i: (layer, p, 0)),
                  pl.BlockSpec((tm, 2 * PW), lambda p, i: (i, p))],
        out_specs=pl.BlockSpec((tm, 2 * PW), lambda p, i: (i, p)),
        out_shape=jax.ShapeDtypeStruct((S, F2), BF16),
        compiler_params=_params("parallel", "parallel"),
    )(dy, w, gu)


HALO = 16


def _conv_terms(bcx_ref, prev_ref, i, tm, D):
    b = bcx_ref[:, 0:D].astype(F32)
    cg = bcx_ref[:, D:2 * D].astype(F32)
    xin = bcx_ref[:, 2 * D:3 * D].astype(F32)
    z = cg * xin
    zp = prev_ref[:, D:2 * D].astype(F32) * prev_ref[:, 2 * D:3 * D].astype(F32)
    zp = jnp.where(i > 0, zp, 0.0)
    z_ext = jnp.concatenate([zp, z], axis=0)
    z1 = pltpu.roll(z_ext, 1, 0)[HALO:, :]
    z2 = pltpu.roll(z_ext, 2, 0)[HALO:, :]
    return b, cg, xin, z, z1, z2


def _conv_gate(bcx, ck, name):
    S, D3 = bcx.shape
    D = D3 // 3
    tm = _pick(S, 256, 16)
    hb = tm // HALO

    def body(bcx_ref, prev_ref, ck_ref, o_ref):
        i = pl.program_id(0)
        b, _, _, z, z1, z2 = _conv_terms(bcx_ref, prev_ref, i, tm, D)
        conv = ck_ref[0:1, :] * z2 + ck_ref[1:2, :] * z1 + ck_ref[2:3, :] * z
        o_ref[...] = (b * conv).astype(BF16)

    return pl.pallas_call(
        body, name=name, grid=(S // tm,),
        in_specs=[_row_spec(tm, D3),
                  pl.BlockSpec((HALO, D3), lambda i: (jnp.maximum(i * hb - 1, 0), 0)),
                  _vec_spec(8, D)],
        out_specs=_row_spec(tm, D),
        out_shape=jax.ShapeDtypeStruct((S, D), BF16),
        compiler_params=_params("parallel"),
    )(bcx, bcx, ck)


def _conv_gate_bwd(du, bcx, ck, name):
    S, D3 = bcx.shape
    D = D3 // 3
    tm = _pick(S, 256, 16)
    hb = tm // HALO
    nt = S // tm

    def body(du_ref, dun_ref, bcx_ref, prev_ref, next_ref, ck_ref, o_ref, dk_ref):
        i = pl.program_id(0)
        b, cg, xin, z, z1, z2 = _conv_terms(bcx_ref, prev_ref, i, tm, D)
        k0, k1, k2 = ck_ref[0:1, :], ck_ref[1:2, :], ck_ref[2:3, :]
        conv = k0 * z2 + k1 * z1 + k2 * z
        d = du_ref[...].astype(F32)
        dconv = d * b
        dcn = jnp.where(i < nt - 1, dun_ref[...].astype(F32) * next_ref[:, 0:D].astype(F32), 0.0)
        d_ext = jnp.concatenate([dconv, dcn], axis=0)
        d1 = pltpu.roll(d_ext, tm + HALO - 1, 0)[:tm, :]
        d2 = pltpu.roll(d_ext, tm + HALO - 2, 0)[:tm, :]
        dz = k2 * dconv + k1 * d1 + k0 * d2
        o_ref[:, 0:D] = (d * conv).astype(BF16)
        o_ref[:, D:2 * D] = (dz * xin).astype(BF16)
        o_ref[:, 2 * D:3 * D] = (dz * cg).astype(BF16)

        @pl.when(i == 0)
        def _():
            dk_ref[...] = jnp.zeros_like(dk_ref)

        dk_ref[0] += _colsum8(dconv * z2)
        dk_ref[1] += _colsum8(dconv * z1)
        dk_ref[2] += _colsum8(dconv * z)

    last = S // HALO - 1
    return pl.pallas_call(
        body, name=name, grid=(nt,),
        in_specs=[_row_spec(tm, D),
                  pl.BlockSpec((HALO, D), lambda i: (jnp.minimum((i + 1) * hb, last), 0)),
                  _row_spec(tm, D3),
                  pl.BlockSpec((HALO, D3), lambda i: (jnp.maximum(i * hb - 1, 0), 0)),
                  pl.BlockSpec((HALO, D3), lambda i: (jnp.minimum((i + 1) * hb, last), 0)),
                  _vec_spec(8, D)],
        out_specs=[_row_spec(tm, D3), pl.BlockSpec((3, 8, D), lambda i: (0, 0, 0))],
        out_shape=[jax.ShapeDtypeStruct((S, D3), BF16), jax.ShapeDtypeStruct((3, 8, D), F32)],
        compiler_params=_params("arbitrary"),
    )(du, du, bcx, bcx, bcx, ck)


def _rel_onehot():
    a = np.arange(CHUNK)[:, None]
    b = np.arange(CHUNK)[None, :]
    idx = np.stack([np.clip((N_LEFT_CHUNKS - dl) * CHUNK + a - b, -MAX_REL, MAX_REL) + MAX_REL
                    for dl in (6, 7, 8)]).reshape(-1)
    return (jnp.asarray(idx)[:, None] == jnp.arange(N_REL)[None, :]).astype(F32)


def _bias_table(rel_bias, name):
    H = rel_bias.shape[0]
    near = jnp.dot(rel_bias, _rel_onehot().T, precision=lax.Precision.HIGHEST).reshape(H, 3, CHUNK, CHUNK)
    far = jnp.broadcast_to(rel_bias[:, N_REL - 1][:, None, None], (H, CHUNK, CHUNK))

    def body(near_ref, far_ref, o_ref):
        neg = jnp.full((CHUNK, CHUNK), NEG, F32)
        for v in range(N_WIN):
            for ic in range(Q_CHUNKS):
                for jc in range(N_WIN * Q_CHUNKS):
                    dl = jc - ic
                    if dl < 0 or dl > N_LEFT_CHUNKS or jc < (N_WIN - 1 - v) * Q_CHUNKS:
                        blk = neg
                    else:
                        blk = far_ref[...] if dl <= 5 else near_ref[dl - 6]
                    o_ref[v, ic * CHUNK:(ic + 1) * CHUNK, jc * CHUNK:(jc + 1) * CHUNK] = blk

    return pl.pallas_call(
        body, name=name, grid=(H,),
        in_specs=[pl.BlockSpec((None, 3, CHUNK, CHUNK), lambda h: (h, 0, 0, 0)),
                  pl.BlockSpec((None, CHUNK, CHUNK), lambda h: (h, 0, 0))],
        out_specs=pl.BlockSpec((N_WIN, None, BQ, N_WIN * BQ), lambda h: (0, h, 0, 0)),
        out_shape=jax.ShapeDtypeStruct((N_WIN, H, BQ, N_WIN * BQ), F32),
        compiler_params=_params("parallel"),
    )(near, far)


def _bias_table_grad(dtab):
    H = dtab.shape[0]
    blk = lambda ic, jc: dtab[:, ic * CHUNK:(ic + 1) * CHUNK, jc * CHUNK:(jc + 1) * CHUNK]
    by_dl = [sum(blk(ic, ic + dl) for ic in range(Q_CHUNKS)) for dl in range(N_LEFT_CHUNKS + 1)]
    far = sum(jnp.sum(by_dl[dl], axis=(1, 2)) for dl in range(6))
    near = jnp.stack(by_dl[6:9], axis=1).reshape(H, 3 * CHUNK * CHUNK)
    g = jnp.dot(near, _rel_onehot(), precision=lax.Precision.HIGHEST)
    return g.at[:, N_REL - 1].add(far)


def _attn_specs(D, W):
    q_spec = pl.BlockSpec((BQ, W), lambda g, i: (i, g))

    def win(w, off):
        return pl.BlockSpec((BQ, W), lambda g, i: (jnp.maximum(i - (N_WIN - 1) + w, 0), off + g))

    k_specs = [win(w, 0) for w in range(N_WIN)]
    v_specs = [win(w, D // W) for w in range(N_WIN)]
    tab_spec = pl.BlockSpec((None, HEADS_PER_STEP, BQ, N_WIN * BQ),
                            lambda g, i: (jnp.minimum(i, N_WIN - 1), g, 0, 0))
    dtab_spec = pl.BlockSpec((HEADS_PER_STEP, BQ, N_WIN * BQ), lambda g, i: (g, 0, 0))
    return q_spec, k_specs, v_specs, tab_spec, dtab_spec


def _attn_exp(q_ref, kT, tab_ref, h, dh):
    s = jnp.dot(q_ref[:, h * dh:(h + 1) * dh], kT[h * dh:(h + 1) * dh, :], preferred_element_type=F32) + tab_ref[h]
    e = jnp.exp(s - jnp.max(s, axis=-1, keepdims=True))
    return e, jnp.sum(e, axis=-1, keepdims=True)


def _attn_fwd(q, kv, tab, name):
    S, D = q.shape
    dh = D // N_HEADS
    W = HEADS_PER_STEP * dh
    q_spec, k_specs, v_specs, tab_spec, _ = _attn_specs(D, W)

    def body(q_ref, *rest):
        k_refs, v_refs = rest[:N_WIN], rest[N_WIN:2 * N_WIN]
        tab_ref, o_ref = rest[2 * N_WIN], rest[2 * N_WIN + 1]
        kT = jnp.concatenate([r[...] for r in k_refs], axis=0).T
        vw = jnp.concatenate([r[...] for r in v_refs], axis=0)
        outs = []
        for h in range(HEADS_PER_STEP):
            e, l = _attn_exp(q_ref, kT, tab_ref, h, dh)
            outs.append(jnp.dot(e.astype(BF16), vw[:, h * dh:(h + 1) * dh], preferred_element_type=F32) / l)
        o_ref[...] = jnp.concatenate(outs, axis=1).astype(BF16)

    return pl.pallas_call(
        body, name=name, grid=(N_HEADS // HEADS_PER_STEP, S // BQ),
        in_specs=[q_spec] + k_specs + v_specs + [tab_spec],
        out_specs=q_spec,
        out_shape=jax.ShapeDtypeStruct((S, D), BF16),
        compiler_params=_params("parallel", "parallel"),
    )(q, *([kv] * (2 * N_WIN)), tab)


def _attn_bwd(q, kv, tab, do, name):
    S, D = q.shape
    dh = D // N_HEADS
    W = HEADS_PER_STEP * dh
    q_spec, k_specs, v_specs, tab_spec, dtab_spec = _attn_specs(D, W)

    def body(q_ref, *rest):
        k_refs, v_refs = rest[:N_WIN], rest[N_WIN:2 * N_WIN]
        tab_ref, do_ref, dq_ref = rest[2 * N_WIN:2 * N_WIN + 3]
        dk_refs = rest[2 * N_WIN + 3:3 * N_WIN + 3]
        dv_refs = rest[3 * N_WIN + 3:4 * N_WIN + 3]
        dtab_ref = rest[4 * N_WIN + 3]
        i = pl.program_id(1)

        @pl.when(i == 0)
        def _():
            dtab_ref[...] = jnp.zeros_like(dtab_ref)

        kT = jnp.concatenate([r[...] for r in k_refs], axis=0).T
        vT = jnp.concatenate([r[...] for r in v_refs], axis=0).T
        qT = q_ref[...].T
        doT = do_ref[...].T
        dqs, dks, dvs = [], [], []
        for h in range(HEADS_PER_STEP):
            hd = slice(h * dh, (h + 1) * dh)
            e, l = _attn_exp(q_ref, kT, tab_ref, h, dh)
            p = e * (1.0 / l)
            dp = jnp.dot(do_ref[:, hd], vT[hd, :], preferred_element_type=F32)
            ds = p * (dp - jnp.sum(p * dp, axis=-1, keepdims=True))
            dtab_ref[h] += ds
            dsb = ds.astype(BF16)
            dqs.append(lax.dot_general(kT[hd, :], dsb, _DIMS["nt"], preferred_element_type=F32) * (dh ** -0.5))
            dks.append(jnp.dot(qT[hd, :], dsb, preferred_element_type=F32))
            dvs.append(jnp.dot(doT[hd, :], p.astype(BF16), preferred_element_type=F32))
        dq_ref[...] = jnp.concatenate(dqs, axis=0).T.astype(BF16)
        dk = jnp.concatenate(dks, axis=0).T
        dv = jnp.concatenate(dvs, axis=0).T
        for w in range(N_WIN):
            dk_refs[w][...] = dk[w * BQ:(w + 1) * BQ, :].astype(BF16)
            dv_refs[w][...] = dv[w * BQ:(w + 1) * BQ, :].astype(BF16)

    part = jax.ShapeDtypeStruct((S, D), BF16)
    outs = pl.pallas_call(
        body, name=name, grid=(N_HEADS // HEADS_PER_STEP, S // BQ),
        in_specs=[q_spec] + k_specs + v_specs + [tab_spec, q_spec],
        out_specs=[q_spec] + [q_spec] * (2 * N_WIN) + [dtab_spec],
        out_shape=[jax.ShapeDtypeStruct((S, D), BF16)] + [part] * (2 * N_WIN)
        + [jax.ShapeDtypeStruct(tab.shape[1:], F32)],
        compiler_params=_params("parallel", "arbitrary"),
    )(q, *([kv] * (2 * N_WIN)), tab, do)
    return outs[0], outs[1:1 + N_WIN], outs[1 + N_WIN:1 + 2 * N_WIN], outs[-1]


def _kv_grad_combine(dk_parts, dv_parts, name):
    S, D = dk_parts[0].shape
    nblk = S // BQ

    def body(*refs):
        o_ref = refs[2 * N_WIN]
        i = pl.program_id(0)
        for half, parts in enumerate((refs[:N_WIN], refs[N_WIN:2 * N_WIN])):
            acc = parts[N_WIN - 1][...].astype(F32)
            for w in range(N_WIN - 1):
                acc = acc + jnp.where(i + (N_WIN - 1 - w) < nblk, parts[w][...].astype(F32), 0.0)
            o_ref[:, half * D:(half + 1) * D] = acc.astype(BF16)

    specs = [pl.BlockSpec((BQ, D), functools.partial(
        lambda i, sh: (jnp.minimum(i + sh, nblk - 1), 0), sh=N_WIN - 1 - w)) for w in range(N_WIN)]
    return pl.pallas_call(
        body, name=name, grid=(nblk,),
        in_specs=specs + specs,
        out_specs=pl.BlockSpec((BQ, 2 * D), lambda i: (i, 0)),
        out_shape=jax.ShapeDtypeStruct((S, 2 * D), BF16),
        compiler_params=_params("parallel"),
    )(*dk_parts, *dv_parts)


def _adamw(w, g, m, v, name):
    shape = w.shape
    C = shape[-1]
    R = int(np.prod(shape[:-1])) if len(shape) > 1 else 1
    w2, g2, m2, v2 = (t.reshape(R, C) for t in (w, g, m, v))
    tr = _pick(R, max(8, (512 * 1024) // C // 8 * 8), 8)

    def body(w_ref, g_ref, m_ref, v_ref, d_ref, nm_ref, nv_ref):
        gv = g_ref[...]
        nm = ADAM_B1 * m_ref[...] + (1.0 - ADAM_B1) * gv
        nv = ADAM_B2 * v_ref[...] + (1.0 - ADAM_B2) * jnp.square(gv)
        m_hat = nm / (1.0 - ADAM_B1 ** ADAM_STEP)
        v_hat = nv / (1.0 - ADAM_B2 ** ADAM_STEP)
        d_ref[...] = -ADAM_LR * (m_hat / (jnp.sqrt(v_hat) + ADAM_EPS) + ADAM_WD * w_ref[...])
        nm_ref[...] = nm
        nv_ref[...] = nv

    spec = pl.BlockSpec((tr, C), lambda i: (i, 0))
    outs = pl.pallas_call(
        body, name=name, grid=(R // tr,),
        in_specs=[spec] * 4, out_specs=[spec] * 3,
        out_shape=[jax.ShapeDtypeStruct((R, C), F32)] * 3,
        compiler_params=_params("parallel"),
    )(w2, g2, m2, v2)
    return tuple(o.reshape(shape) for o in outs)


def _sum_rows(a, name):
    n, L = a.shape

    def body(a_ref, o_ref):
        acc = a_ref[0:1, :]
        for r in range(1, n):
            acc = acc + a_ref[r:r + 1, :]
        o_ref[...] = acc

    return pl.pallas_call(
        body, name=name, grid=(1,),
        in_specs=[pl.BlockSpec((n, L), lambda i: (0, 0))],
        out_specs=pl.BlockSpec((1, L), lambda i: (0, 0)),
        out_shape=jax.ShapeDtypeStruct((1, L), F32),
        compiler_params=_params("arbitrary"),
    )(a)


def _scalar_call(body, name, scalar, grid, in_specs, out_spec, out_shape, args):
    return pl.pallas_call(
        body, name=name,
        grid_spec=pltpu.PrefetchScalarGridSpec(num_scalar_prefetch=1, grid=grid, in_specs=in_specs,
                                               out_specs=out_spec),
        out_shape=out_shape, compiler_params=_params("parallel"),
    )(jnp.reshape(scalar, (-1,)).astype(jnp.int32), *args)


def _pair_sum(view, got, c, name):
    nb, _, rh, cols = view.shape
    tr = _pick(rh, max(16, (1 << 20) // cols // 16 * 16), 16)
    bpr = rh // tr

    def body(s_ref, a_ref, b_ref, o_ref):
        o_ref[...] = (a_ref[...].astype(F32) + b_ref[...].astype(F32)).astype(BF16)

    spec = pl.BlockSpec((tr, cols), lambda i, s: (i, 0))
    mine = pl.BlockSpec((tr, cols), lambda i, s: ((2 * (i // bpr) + s[0]) * bpr + i % bpr, 0))
    return _scalar_call(body, name, c, (nb * bpr,), [mine, spec], spec,
                        jax.ShapeDtypeStruct((nb * rh, cols), BF16),
                        (view.reshape(nb * 2 * rh, cols), got.reshape(nb * rh, cols)))


def _owner_sum(pair, recv, me, c, it, name):
    _, rh, bc = recv.shape
    tr = _pick(rh, max(16, (1 << 19) // bc // 16 * 16), 16)
    bpr = rh // tr

    def body(s_ref, a_ref, r0, r1, r2, o_ref):
        o_ref[...] = ((a_ref[...].astype(F32) + r0[...].astype(F32)) + r1[...].astype(F32)) + r2[...].astype(F32)

    if it.kind == "col":
        own = pl.BlockSpec((tr, bc), lambda i, s: (i, s[0]))
    else:
        own = pl.BlockSpec((tr, bc), lambda i, s: (s[0] * bpr + i, 0))
    slots = [pl.BlockSpec((None, tr, bc), functools.partial(lambda i, s, k: (k, i, 0), k=k)) for k in range(3)]
    return _scalar_call(body, name, jnp.stack([it.pos(me), c]), (bpr,), [own] + slots,
                        pl.BlockSpec((tr, bc), lambda i, s: (s[1] * bpr + i, 0)),
                        jax.ShapeDtypeStruct((2 * rh, bc), F32), (pair, recv, recv, recv))


def _place():
    x, y, c = lax.axis_index("x"), lax.axis_index("y"), lax.axis_index("c")
    chips = [(1 - x, y), (x, 1 - y), (1 - x, 1 - y)]
    return x, y, c, chips


def _chip_index(px, py):
    return 2 * px + py


def _all_gather_small(x_shard, name):
    m_per, n = x_shard.shape

    def body(x_ref, out_ref, send_sems, recv_sems, local_sem):
        x, y, c, chips = _place()
        me, sibling = (x, y, c), (x, y, 1 - c)

        def rows(px, py, pc):
            return out_ref.at[pl.ds((4 * px + 2 * py + pc) * m_per, m_per), :]

        def copy(k, block, to, src=None):
            return pltpu.make_async_remote_copy(
                src_ref=rows(*block) if src is None else src, dst_ref=rows(*block),
                send_sem=send_sems.at[k], recv_sem=recv_sems.at[k], device_id=to, device_id_type=MESH)

        mine = pltpu.make_async_copy(x_ref, rows(*me), local_sem)
        mine.start()
        first = [copy(0, me, sibling, src=x_ref)]
        first += [copy(1 + j, me, (*chip, c), src=x_ref) for j, chip in enumerate(chips)]
        for cp in first:
            cp.start()
        passed = [copy(4 + j, (*chip, c), sibling) for j, chip in enumerate(chips)]
        for j, chip in enumerate(chips):
            copy(1 + j, (*chip, c), me).wait_recv()
            passed[j].start()
        copy(0, sibling, me).wait_recv()
        for j, chip in enumerate(chips):
            copy(4 + j, (*chip, 1 - c), me).wait_recv()
        for cp in first + passed:
            cp.wait_send()
        mine.wait()

    return pl.pallas_call(
        body, name=name,
        out_shape=jax.ShapeDtypeStruct((N_DEV * m_per, n), x_shard.dtype),
        in_specs=[pl.BlockSpec(memory_space=pltpu.VMEM)],
        out_specs=pl.BlockSpec(memory_space=pltpu.VMEM),
        scratch_shapes=[pltpu.SemaphoreType.DMA((7,)), pltpu.SemaphoreType.DMA((7,)), pltpu.SemaphoreType.DMA],
    )(x_shard)


def _gather_flat(vec, name):
    L = vec.shape[0]
    Lp = -(-L // 1024) * 1024
    g = _all_gather_small(jnp.pad(vec, (0, Lp - L)).reshape(8, Lp // 8), name)
    return g.reshape(N_DEV, Lp)[:, :L]


class _Item:
    def __init__(self, kind, rows, cols, arg, layer, swap=False):
        self.kind, self.rows, self.cols, self.arg, self.layer, self.swap = kind, rows, cols, arg, layer, swap

    def ref(self, refs):
        return refs[self.arg].at[self.layer]

    def pos(self, j):
        return 2 * (j % 2) + j // 2 if self.swap else j


def _block(ref, it, j, half):
    if it.kind == "col":
        ns = it.cols // N_CHIP
        return ref.at[pl.ds(half * (it.rows // 2), it.rows // 2), pl.ds(it.pos(j) * ns, ns)]
    rs = it.rows // N_CHIP
    return ref.at[pl.ds(j * rs + half * (rs // 2), rs // 2), :]


def _cast_place(w, layer, kind, pos, name):
    _, r, n = w.shape
    tr = _pick(r, max(16, (1 << 20) // n // 16 * 16), 16)
    bpr = r // tr

    def body(s_ref, w_ref, o_ref):
        o_ref[...] = w_ref[...].astype(BF16)

    if kind == "col":
        full, out_idx = (1, r, N_CHIP * n), (lambda i, s: (0, i, s[0]))
    else:
        full, out_idx = (1, N_CHIP * r, n), (lambda i, s: (0, s[0] * bpr + i, 0))
    return pl.pallas_call(
        body, name=name,
        grid_spec=pltpu.PrefetchScalarGridSpec(
            num_scalar_prefetch=1, grid=(bpr,),
            in_specs=[pl.BlockSpec((None, tr, n), lambda i, s: (layer, i, 0))],
            out_specs=pl.BlockSpec((None, tr, n), out_idx)),
        out_shape=jax.ShapeDtypeStruct(full, BF16),
        compiler_params=_params("parallel"),
    )(jnp.reshape(pos, (1,)).astype(jnp.int32), w)


HBM_SPEC = pl.BlockSpec(memory_space=pltpu.HBM)
SEM_SPEC = pl.BlockSpec(memory_space=pltpu.SEMAPHORE)
ANY_SPEC = pl.BlockSpec(memory_space=pl.ANY)
SPLIT_PARAMS = dict(has_side_effects=pltpu.SideEffectType.DATAFLOW_SIDE_EFFECTING)


def _in_hbm(a):
    return pltpu.with_memory_space_constraint(a, pltpu.HBM)


def _split_start(copies_of, bufs, n_sem, after, name):
    n = len(bufs)

    def body(*refs):
        ins, send, recv, token = refs[:n], refs[n + 1], refs[n + 2], refs[2 * n + 3]
        for cp in copies_of(ins, send, recv, False)[0]:
            cp.start()
        token[...] = jnp.zeros_like(token)

    outs = pl.pallas_call(
        body, name=name,
        out_shape=(pltpu.SemaphoreType.DMA(n_sem), pltpu.SemaphoreType.DMA(n_sem),
                   *[pltpu.HBM(b.shape, b.dtype) for b in bufs], jax.ShapeDtypeStruct((8, 128), F32)),
        in_specs=[HBM_SPEC] * n + [ANY_SPEC],
        out_specs=(SEM_SPEC, SEM_SPEC, *[HBM_SPEC] * n, pl.BlockSpec(memory_space=pltpu.VMEM)),
        input_output_aliases={t: 2 + t for t in range(n)},
        compiler_params=pltpu.CompilerParams(**SPLIT_PARAMS),
    )(*[_in_hbm(b) for b in bufs], after)
    return outs[0], outs[1], list(outs[2:2 + n]), outs[2 + n]


def _split_wait(copies_of, send, recv, bufs, after, name):
    n = len(bufs)

    def body(*refs):
        ins, send_ref, recv_ref = refs[:n], refs[n], refs[n + 1]
        sends, arrivals = copies_of(ins, send_ref, recv_ref, True)
        for cp in sends:
            cp.wait_send()
        for cp in arrivals:
            cp.wait_recv()

    return pl.pallas_call(
        body, name=name,
        out_shape=[pltpu.HBM(b.shape, b.dtype) for b in bufs],
        in_specs=[HBM_SPEC] * n + [SEM_SPEC, SEM_SPEC, ANY_SPEC],
        out_specs=[HBM_SPEC] * n,
        input_output_aliases={t: t for t in range(n)},
        compiler_params=pltpu.CompilerParams(**SPLIT_PARAMS),
    )(*bufs, send, recv, after)


def _gather_copies(items):
    def copies_of(refs, send, recv, with_arrivals):
        x, y, c, chips = _place()
        me = _chip_index(x, y)
        sends, arrivals = [], []
        for t, it in enumerate(items):
            for k, chip in enumerate(chips):
                for core in range(2):
                    mine = _block(it.ref(refs), it, me, c)
                    sends.append(pltpu.make_async_remote_copy(
                        src_ref=mine, dst_ref=mine, send_sem=send.at[6 * t + 2 * k + core],
                        recv_sem=recv.at[6 * t + 2 * k + c], device_id=(*chip, core), device_id_type=MESH))
                    if with_arrivals:
                        landed = _block(it.ref(refs), it, _chip_index(*chip), core)
                        arrivals.append(pltpu.make_async_remote_copy(
                            src_ref=landed, dst_ref=landed, send_sem=send.at[6 * t + 2 * k + core],
                            recv_sem=recv.at[6 * t + 2 * k + core], device_id=(*chip, core), device_id_type=MESH))
        return sends, arrivals

    return copies_of


def _owner_copies(items):
    n = len(items)

    def blk(ref, it, j):
        if it.kind == "col":
            ns = it.cols // N_CHIP
            return ref.at[:, pl.ds(it.pos(j) * ns, ns)]
        return ref.at[j]

    def copies_of(refs, send, recv, with_arrivals):
        x, y, c, chips = _place()
        sends, arrivals = [], []
        for t, it in enumerate(items):
            for k, chip in enumerate(chips):
                slot = refs[n + t].at[k]
                sends.append(pltpu.make_async_remote_copy(
                    src_ref=blk(refs[t], it, _chip_index(*chip)), dst_ref=slot, send_sem=send.at[3 * t + k],
                    recv_sem=recv.at[3 * t + k], device_id=(*chip, c), device_id_type=MESH))
                if with_arrivals:
                    arrivals.append(pltpu.make_async_remote_copy(
                        src_ref=slot, dst_ref=slot, send_sem=send.at[3 * t + k], recv_sem=recv.at[3 * t + k],
                        device_id=(*chip, c), device_id_type=MESH))
        return sends, arrivals

    return copies_of


def _owner_slot_shape(it):
    if it.kind == "col":
        return (3, it.rows // 2, it.cols // N_CHIP)
    return (3, it.rows // (2 * N_CHIP), it.cols)


def _pair_view(g, it):
    if it.kind == "col":
        return g.reshape(1, 2, it.rows // 2, it.cols)
    return g.reshape(N_CHIP, 2, it.rows // (2 * N_CHIP), it.cols)


def _pair_exchange(views, name):
    n = len(views)

    def body(*refs):
        ins, outs, send, recv = refs[:n], refs[n:2 * n], refs[2 * n], refs[2 * n + 1]
        x, y, c, _ = _place()
        cps = []
        for t in range(n):
            cp = pltpu.make_async_remote_copy(
                src_ref=ins[t].at[:, pl.ds(1 - c, 1)], dst_ref=outs[t],
                send_sem=send.at[t], recv_sem=recv.at[t], device_id=(x, y, 1 - c), device_id_type=MESH)
            cp.start()
            cps.append(cp)
        for cp in cps:
            cp.wait()

    any_spec = pl.BlockSpec(memory_space=pl.ANY)
    return pl.pallas_call(
        body, name=name,
        out_shape=[jax.ShapeDtypeStruct((v.shape[0], 1) + v.shape[2:], v.dtype) for v in views],
        in_specs=[any_spec] * n, out_specs=[any_spec] * n,
        scratch_shapes=[pltpu.SemaphoreType.DMA((n,)), pltpu.SemaphoreType.DMA((n,))],
    )(*views)


def _half_exchange(bufs, name):
    n = len(bufs)

    def body(*refs):
        ins, outs, send, recv = refs[:n], refs[n:2 * n], refs[2 * n], refs[2 * n + 1]
        x, y, c, _ = _place()
        cps = []
        for t in range(n):
            r2 = ins[t].shape[0] // 2
            cp = pltpu.make_async_remote_copy(
                src_ref=ins[t].at[pl.ds(c * r2, r2), :], dst_ref=outs[t].at[pl.ds(c * r2, r2), :],
                send_sem=send.at[t], recv_sem=recv.at[t], device_id=(x, y, 1 - c), device_id_type=MESH)
            cp.start()
            cps.append(cp)
        for t in range(n):
            r2 = ins[t].shape[0] // 2
            theirs = outs[t].at[pl.ds((1 - c) * r2, r2), :]
            pltpu.make_async_remote_copy(
                src_ref=theirs, dst_ref=theirs, send_sem=send.at[t], recv_sem=recv.at[t],
                device_id=(x, y, 1 - c), device_id_type=MESH).wait_recv()
        for cp in cps:
            cp.wait_send()

    any_spec = pl.BlockSpec(memory_space=pl.ANY)
    return pl.pallas_call(
        body, name=name,
        out_shape=[jax.ShapeDtypeStruct(b.shape, b.dtype) for b in bufs],
        in_specs=[any_spec] * n, out_specs=[any_spec] * n,
        input_output_aliases={t: t for t in range(n)},
        scratch_shapes=[pltpu.SemaphoreType.DMA((n,)), pltpu.SemaphoreType.DMA((n,))],
    )(*bufs)


class _Reduction:
    pass


def _reduce_start(grads, items, after, tag):
    x, y, c, _ = _place()
    views = [_pair_view(g, it) for g, it in zip(grads, items)]
    got = _pair_exchange(views, f"rs_pair_exchange_{tag}")
    pairs = [_pair_sum(v, r, c, f"rs_pair_sum_{tag}_{t}") for t, (v, r) in enumerate(zip(views, got))]
    shaped = [p if it.kind == "col" else p.reshape(N_CHIP, p.shape[0] // N_CHIP, p.shape[1])
              for p, it in zip(pairs, items)]
    lands = [lax.empty(_owner_slot_shape(it), BF16) for it in items]
    r = _Reduction()
    r.items, r.tag = items, tag
    r.send, r.recv, r.bufs, r.token = _split_start(
        _owner_copies(items), shaped + lands, (3 * len(items),), after, f"rs_owner_start_{tag}")
    return r


def _reduce_finish(groups, after):
    x, y, c, _ = _place()
    me = _chip_index(x, y)
    halves = []
    for r in groups:
        n = len(r.items)
        bufs = _split_wait(_owner_copies(r.items), r.send, r.recv, r.bufs, after, f"rs_owner_wait_{r.tag}")
        for t, it in enumerate(r.items):
            pair = bufs[t].reshape(-1, bufs[t].shape[-1])
            halves.append(_owner_sum(pair, bufs[n + t], me, c, it, f"rs_owner_sum_{r.tag}_{t}"))
    return _half_exchange(halves, "rs_half_exchange")


def _silu(v):
    return v * jax.nn.sigmoid(v)


def _sum8(p):
    return jnp.sum(p, axis=-2)


def kernel(x, c, mod_w, mod_b, norm_g, ffn_w_in, ffn_w_out, conv_w_in, conv_k, conv_w_out, kv_mod_w, kv_mod_b, kv_norm_g, w_kv, attn_w_q, attn_w_o, rel_bias, loss_target, m_mod_w, m_mod_b, m_norm_g, m_ffn_w_in, m_ffn_w_out, m_conv_w_in, m_conv_k, m_conv_w_out, m_kv_mod_w, m_kv_mod_b, m_kv_norm_g, m_w_kv, m_attn_w_q, m_attn_w_o, m_rel_bias, v_mod_w, v_mod_b, v_norm_g, v_ffn_w_in, v_ffn_w_out, v_conv_w_in, v_conv_k, v_conv_w_out, v_kv_mod_w, v_kv_mod_b, v_kv_norm_g, v_w_kv, v_attn_w_q, v_attn_w_o, v_rel_bias):
    xi, yi, ci = lax.axis_index("x"), lax.axis_index("y"), lax.axis_index("c")
    chip = 2 * xi + yi
    dev = 2 * chip + ci
    _, S, D = x.shape
    F = ffn_w_out.shape[1] * N_CHIP
    x0 = x.reshape(S, D)
    target = loss_target.reshape(S, D)
    n_mod = mod_w.shape[2]
    n_kvm = kv_mod_w.shape[1]
    dsh = D // N_CHIP
    TF = F // 2

    c_all = _all_gather_small(c.reshape(8, D // 8), "ag_c").reshape(N_DEV, D)
    sc16 = jnp.pad(_silu(c_all), ((0, 8), (0, 0)))
    part = [_mm(sc16, mod_w, "nn", F32, f"mod_fwd_{l}", b_layer=l)[:8] for l in range(2)]
    part.append(_mm(sc16, kv_mod_w, "nn", F32, "mod_fwd_kv")[:8])
    fwd_vec = jnp.concatenate([p.reshape(-1) for p in part] + [norm_g.reshape(-1), conv_k.reshape(-1)])
    fwd_all = _gather_flat(fwd_vec, "ag_fwd_small")[0::2]
    o = 0
    mods = []
    for n in (n_mod, n_mod, n_kvm):
        blk = fwd_all[:, o:o + 8 * n].reshape(N_CHIP, 8, n)
        mods.append(lax.dynamic_index_in_dim(blk, dev, axis=1, keepdims=False).reshape(N_CHIP * n))
        o += 8 * n
    ng = fwd_all[:, o:o + 8 * dsh].reshape(N_CHIP, 2, 4, dsh).transpose(1, 2, 0, 3).reshape(2, 4, D)
    o += 8 * dsh
    ck = fwd_all[:, o:o + 3 * dsh].reshape(N_CHIP, 3, dsh).transpose(1, 0, 2).reshape(3, D)
    ck8 = jnp.pad(ck, ((0, 5), (0, 0)))
    mod = [mods[l] + mod_b[l] for l in range(2)]
    sh1, sc1, g1, sh2, sc2, g2 = zip(*[jnp.split(m, 6) for m in mod])
    kv_sh, kv_sc = jnp.split(mods[2] + kv_mod_b, 2)
    row = lambda v: v.reshape(1, D)

    it_conv = [_Item("col", D, 3 * D, 0, 0), _Item("row", D, D, 1, 0)]
    it_ffn = [_Item("col", D, 2 * F, 0, 0, swap=True), _Item("row", F, D, 1, 0)]
    it_attn = [_Item("col", D, 2 * D, 0, 0), _Item("row", D, D, 1, 0), _Item("row", D, D, 2, 0)]

    def placed(w, layer, it, nm, after=None):
        pos = it.pos(chip)
        if after is not None:
            pos = pos + after[0, 0].astype(jnp.int32)
        return _cast_place(w, layer, it.kind, pos, f"place_{nm}")

    flying = {}

    def start(tag, its, bufs, after):
        send, recv, bufs, tok = _split_start(_gather_copies(its), bufs, (6 * len(its),), after, f"ag_start_{tag}")
        flying[tag] = (its, send, recv, bufs)
        return tok

    def arrived(tag, after):
        its, send, recv, bufs = flying[tag]
        return _split_wait(_gather_copies(its), send, recv, bufs, after, f"ag_wait_{tag}")

    one = lambda it: [_Item(it.kind, it.rows, it.cols, 0, 0, it.swap)]
    tok = start("conv_in", one(it_conv[0]), [placed(conv_w_in, 0, it_conv[0], "conv_w_in")], fwd_all)
    tok = start("conv_out", one(it_conv[1]), [placed(conv_w_out, 0, it_conv[1], "conv_w_out", tok)], tok)
    tok = start("ffn0_in", one(it_ffn[0]), [placed(ffn_w_in, 0, it_ffn[0], "ffn_w_in0", tok)], tok)
    tok = start("ffn0_out", one(it_ffn[1]), [placed(ffn_w_out, 0, it_ffn[1], "ffn_w_out0", tok)], tok)
    tok = start("attn", it_attn, [placed(w_kv[None], 0, it_attn[0], "w_kv", tok),
                                  placed(attn_w_q, 0, it_attn[1], "attn_w_q", tok),
                                  placed(attn_w_o, 0, it_attn[2], "attn_w_o", tok)], tok)
    token = start("ffn1", it_ffn, [placed(ffn_w_in, 1, it_ffn[0], "ffn_w_in1", tok),
                                   placed(ffn_w_out, 1, it_ffn[1], "ffn_w_out1", tok)], tok)

    a1 = row(ng[0, 0] * (1.0 + sc1[0])) + token[0, 0]
    (h1,) = _norm_mod(x0, a1, row(sh1[0]), "l0_norm1")
    tab = _bias_table(rel_bias[0], "l1_bias_table")
    h1, tab = lax.optimization_barrier((h1, tab))
    (W_cin,) = arrived("conv_in", h1)
    bcx = _mm(h1, W_cin, "nn", BF16, "l0_conv_in", b_layer=0)
    ug = _conv_gate(bcx, ck8, "l0_conv_gate")
    gt1 = row(g1[0] * ng[0, 1])
    a2 = row(ng[0, 2] * (1.0 + sc2[0]))
    (W_cout,) = arrived("conv_out", ug)
    y1, x1, h2 = _mm_post(ug, W_cout, x0, gt1, "l0_conv_out", scales=a2, shifts=row(sh2[0]))
    (W_fin0,) = arrived("ffn0_in", h2)
    gu0, act0 = _ffn_in_act(h2, W_fin0, 0, "l0_ffn_in")
    (W_fout0,) = arrived("ffn0_out", act0)
    gt2 = row(g2[0] * ng[0, 3])
    a3 = ng[1, 0] * (1.0 + sc1[1])
    akv = kv_norm_g * (1.0 + kv_sc)
    y2, x2, h3, hkv = _mm_post(act0, W_fout0, x1, gt2, "l0_ffn_out",
                               scales=jnp.stack([a3, akv]), shifts=jnp.stack([sh1[1], kv_sh]))
    W_kv, W_q, W_o = arrived("attn", hkv)
    kvp = _mm(hkv, W_kv, "nn", BF16, "l1_kv", b_layer=0)
    att_scale = (D // N_HEADS) ** -0.5
    assert math.log2(att_scale) % 1 == 0, "scaling q before its bf16 cast is exact only for a power of two"
    qp = _mm(h3, W_q, "nn", BF16, "l1_q", b_layer=0, scale=att_scale)
    oh = _attn_fwd(qp, kvp, tab, "l1_attn")
    gt3 = row(g1[1] * ng[1, 1])
    a4 = row(ng[1, 2] * (1.0 + sc2[1]))
    y3, x3, h4 = _mm_post(oh, W_o, x2, gt3, "l1_attn_out", scales=a4, shifts=row(sh2[1]))
    W_fin1, W_fout1 = arrived("ffn1", h4)
    gu1, act1 = _ffn_in_act(h4, W_fin1, 0, "l1_ffn_in")
    gt4 = row(g2[1] * ng[1, 3])
    y4, dx4, sq = _mm_post(act1, W_fout1, x3, gt4, "l1_ffn_out", target=target)
    loss_part = 0.5 * jnp.sum(sq) / D

    def ffn_bwd(dxn, xin_, h, gu, act, y, gt, a, w_in, w_out, tag):
        dy, dgt = _post_norm_bwd(dxn, y, gt, f"{tag}_post2_bwd")
        dgu = _ffn_out_dx_act(dy, w_out, 0, gu, f"{tag}_ffn_out_dx")
        g_fout = _mm(act, dy, "tn", BF16, f"{tag}_ffn_out_dw", tm=TF)
        dh = _mm(dgu, w_in, "nt", BF16, f"{tag}_ffn_in_dx", b_layer=0)
        g_fin = _mm(h, dgu, "tn", BF16, f"{tag}_ffn_in_dw", tn=TF)
        dx, ds, db = _pre_norm_bwd(xin_, dxn, [dh], a, f"{tag}_norm2_bwd")
        return dx, _sum8(dgt), _sum8(ds)[0], _sum8(db)[0], g_fin, g_fout

    dx3, dgt4, da4, db4, G_fin1, G_fout1 = ffn_bwd(dx4, x3, h4, gu1, act1, y4, gt4, a4, W_fin1, W_fout1, "l1")
    red = [_reduce_start([G_fin1, G_fout1], it_ffn, token, "ffn1")]
    dy3, dgt3 = _post_norm_bwd(dx3, y3, gt3, "l1_post1_bwd")
    doh = _mm(dy3, W_o, "nt", BF16, "l1_attn_out_dx", b_layer=0)
    G_o = _mm(oh, dy3, "tn", BF16, "l1_attn_out_dw")
    dq, dk_parts, dv_parts, dtab = _attn_bwd(qp, kvp, tab, doh, "l1_attn_bwd")
    d_rel = _bias_table_grad(dtab)
    dkv = _kv_grad_combine(dk_parts, dv_parts, "l1_kv_grad")
    dh3 = _mm(dq, W_q, "nt", BF16, "l1_q_dx", b_layer=0)
    G_q = _mm(h3, dq, "tn", BF16, "l1_q_dw")
    dhkv = _mm(dkv, W_kv, "nt", BF16, "l1_kv_dx", b_layer=0)
    G_kv = _mm(hkv, dkv, "tn", BF16, "l1_kv_dw")
    red.append(_reduce_start([G_kv, G_q, G_o], it_attn, red[-1].token, "attn"))
    dx2, ds3, db3 = _pre_norm_bwd(x2, dx3, [dh3, dhkv], jnp.stack([a3, akv]), "l1_norm1_bwd")
    ds3, db3 = _sum8(ds3), _sum8(db3)

    dx1, dgt2, da2, db2, G_fin0, G_fout0 = ffn_bwd(dx2, x1, h2, gu0, act0, y2, gt2, a2, W_fin0, W_fout0, "l0")
    red.append(_reduce_start([G_fin0, G_fout0], it_ffn, red[-1].token, "ffn0"))
    dy1, dgt1 = _post_norm_bwd(dx1, y1, gt1, "l0_post1_bwd")
    dug = _mm(dy1, W_cout, "nt", BF16, "l0_conv_out_dx", b_layer=0)
    G_cout = _mm(ug, dy1, "tn", BF16, "l0_conv_out_dw")
    dbcx, dck = _conv_gate_bwd(dug, bcx, ck8, "l0_conv_gate_bwd")
    dh1 = _mm(dbcx, W_cin, "nt", BF16, "l0_conv_in_dx", b_layer=0)
    G_cin = _mm(h1, dbcx, "tn", BF16, "l0_conv_in_dw")
    red.append(_reduce_start([G_cin, G_cout], it_conv, red[-1].token, "conv"))
    dx0, ds1, db1 = _pre_norm_bwd(x0, dx1, [dh1], a1, "l0_norm1_bwd")
    ds1, db1 = _sum8(ds1)[0], _sum8(db1)[0]
    dgt1, dgt3 = _sum8(dgt1), _sum8(dgt3)

    def dmod_of(l, ds_a, db_a, dgt_a, ds_b, db_b, dgt_b):
        return jnp.concatenate([db_a, ds_a * ng[l, 0], dgt_a * ng[l, 1], db_b, ds_b * ng[l, 2], dgt_b * ng[l, 3]])

    dmod0 = dmod_of(0, ds1, db1, dgt1, da2, db2, dgt2)
    dmod1 = dmod_of(1, ds3[0], db3[0], dgt3, da4, db4, dgt4)
    dkvmod = jnp.concatenate([db3[1], ds3[1] * kv_norm_g])
    dng = jnp.stack([
        jnp.stack([ds1 * (1.0 + sc1[0]), dgt1 * g1[0], da2 * (1.0 + sc2[0]), dgt2 * g2[0]]),
        jnp.stack([ds3[0] * (1.0 + sc1[1]), dgt3 * g1[1], da4 * (1.0 + sc2[1]), dgt4 * g2[1]])])
    dkvng = ds3[1] * (1.0 + kv_sc)
    small = [dmod0, dmod1, dkvmod, dng.reshape(-1), dkvng, _sum8(dck).reshape(-1), d_rel.reshape(-1),
             loss_part.reshape(1)]
    sizes = [int(s.shape[0]) for s in small]
    offs = np.concatenate([[0], np.cumsum(sizes)])
    bwd_all = _gather_flat(jnp.concatenate(small), "ag_bwd_small")
    Lb = bwd_all.shape[1]
    Lp = -(-Lb // 128) * 128
    tot = _sum_rows(jnp.pad(bwd_all, ((0, 0), (0, Lp - Lb))), "sum_small")[0]
    seg = lambda i: tot[offs[i]:offs[i + 1]]
    g_mod_b = jnp.stack([seg(0), seg(1)])
    g_kv_mod_b = seg(2)
    g_norm_g = lax.dynamic_slice_in_dim(seg(3).reshape(2, 4, D), chip * dsh, dsh, axis=2)
    g_kv_norm_g = seg(4)
    g_conv_k = lax.dynamic_slice_in_dim(seg(5).reshape(1, 3, D), chip * dsh, dsh, axis=2)
    g_rel_bias = seg(6).reshape(rel_bias.shape)
    loss = seg(7)[0]

    def dmod_w(i, n, name):
        rows_ = lax.dynamic_slice_in_dim(bwd_all[:, offs[i]:offs[i + 1]], chip * n, n, axis=1)
        return _mm(sc16, jnp.pad(rows_, ((0, 8), (0, 0))), "tn", F32, name)

    g_mod_w = jnp.stack([dmod_w(0, n_mod, "mod_bwd_0"), dmod_w(1, n_mod, "mod_bwd_1")])
    g_kv_mod_w = dmod_w(2, n_kvm, "mod_bwd_kv")

    r_fin1, r_fout1, r_kv, r_q, r_o, r_fin0, r_fout0, r_cin, r_cout = _reduce_finish(red, dx0)
    grads = {
        "mod_w": g_mod_w, "mod_b": g_mod_b, "norm_g": g_norm_g,
        "ffn_w_in": jnp.stack([r_fin0, r_fin1]), "ffn_w_out": jnp.stack([r_fout0, r_fout1]),
        "conv_w_in": r_cin[None], "conv_k": g_conv_k, "conv_w_out": r_cout[None],
        "kv_mod_w": g_kv_mod_w, "kv_mod_b": g_kv_mod_b, "kv_norm_g": g_kv_norm_g, "w_kv": r_kv,
        "attn_w_q": r_q[None], "attn_w_o": r_o[None], "rel_bias": g_rel_bias,
    }
    weights = dict(mod_w=mod_w, mod_b=mod_b, norm_g=norm_g, ffn_w_in=ffn_w_in, ffn_w_out=ffn_w_out,
                   conv_w_in=conv_w_in, conv_k=conv_k, conv_w_out=conv_w_out, kv_mod_w=kv_mod_w,
                   kv_mod_b=kv_mod_b, kv_norm_g=kv_norm_g, w_kv=w_kv, attn_w_q=attn_w_q, attn_w_o=attn_w_o,
                   rel_bias=rel_bias)
    m_in = dict(mod_w=m_mod_w, mod_b=m_mod_b, norm_g=m_norm_g, ffn_w_in=m_ffn_w_in, ffn_w_out=m_ffn_w_out,
                conv_w_in=m_conv_w_in, conv_k=m_conv_k, conv_w_out=m_conv_w_out, kv_mod_w=m_kv_mod_w,
                kv_mod_b=m_kv_mod_b, kv_norm_g=m_kv_norm_g, w_kv=m_w_kv, attn_w_q=m_attn_w_q,
                attn_w_o=m_attn_w_o, rel_bias=m_rel_bias)
    v_in = dict(mod_w=v_mod_w, mod_b=v_mod_b, norm_g=v_norm_g, ffn_w_in=v_ffn_w_in, ffn_w_out=v_ffn_w_out,
                conv_w_in=v_conv_w_in, conv_k=v_conv_k, conv_w_out=v_conv_w_out, kv_mod_w=v_kv_mod_w,
                kv_mod_b=v_kv_mod_b, kv_norm_g=v_kv_norm_g, w_kv=v_w_kv, attn_w_q=v_attn_w_q,
                attn_w_o=v_attn_w_o, rel_bias=v_rel_bias)
    names = list(weights)
    g_out, d_out, m_out, v_out = [], [], [], []
    for n in names:
        g = grads[n].reshape(weights[n].shape)
        d, nm, nv = _adamw(weights[n], g, m_in[n], v_in[n], f"adamw_{n}")
        g_out.append(g)
        d_out.append(d)
        m_out.append(nm)
        v_out.append(nv)
    return (loss, dx0.reshape(x.shape), *g_out, *d_out, *m_out, *v_out)
```

```python
import functools
import math

import numpy as np
import jax
import jax.numpy as jnp
from jax import lax
from jax.experimental import pallas as pl
from jax.experimental.pallas import tpu as pltpu

CHUNK = 64
N_LEFT_CHUNKS = 8
N_HEADS = 16
MAX_REL = 2 * CHUNK
N_REL = 2 * MAX_REL + 1
EPS = 1e-6
ADAM_LR = 0.001
ADAM_B1 = 0.9
ADAM_B2 = 0.999
ADAM_EPS = 1e-08
ADAM_WD = 0.01
ADAM_STEP = 10

Q_CHUNKS = 4
BQ = Q_CHUNKS * CHUNK
N_WIN = 1 + N_LEFT_CHUNKS // Q_CHUNKS
HEADS_PER_STEP = 8
NEG = -1e30
N_DEV = 8
N_CHIP = 4

BF16 = jnp.bfloat16
F32 = jnp.float32
V7X_VMEM_LIMIT_BYTES = 56 * 1024 * 1024
MESH = pl.DeviceIdType.MESH


def _pick(n, pref, align):
    t = min(pref, n)
    t -= t % align
    while t >= align:
        if n % t == 0:
            return t
        t -= align
    return n


def _params(*sem):
    return pltpu.CompilerParams(dimension_semantics=sem, vmem_limit_bytes=V7X_VMEM_LIMIT_BYTES)


def _colsum8(v):
    r, d = v.shape
    return v.reshape(r // 8, 8, d).sum(axis=0)


_DIMS = {"nn": (((1,), (0,)), ((), ())), "nt": (((1,), (1,)), ((), ())), "tn": (((0,), (0,)), ((), ()))}


def _mm(a, b, mode, out_dtype, name, *, b_layer=None, tm=1024, tn=1024, tk=None, scale=None):
    if tk is None:
        tk = 2048 if mode == "tn" else 3072
    bs = b.shape[1:] if b_layer is not None else b.shape
    if mode == "nn":
        (M, K), (K2, N) = a.shape, bs
    elif mode == "nt":
        (M, K), (N, K2) = a.shape, bs
    else:
        (K, M), (K2, N) = a.shape, bs
    assert K == K2, (name, a.shape, b.shape)
    tm = _pick(M, tm, 128 if mode == "tn" else 16)
    tn = _pick(N, tn, 128)
    tk = _pick(K, tk, 128 if mode != "tn" else 16)
    nk = K // tk
    assert scale is None or nk == 1, name
    dims = _DIMS[mode]

    def body(a_ref, b_ref, o_ref, *acc):
        p = lax.dot_general(a_ref[...].astype(BF16), b_ref[...].astype(BF16), dims,
                            preferred_element_type=F32)
        if nk == 1:
            o_ref[...] = (p if scale is None else p * scale).astype(o_ref.dtype)
        else:
            k = pl.program_id(2)

            @pl.when(k == 0)
            def _():
                acc[0][...] = p

            @pl.when(k > 0)
            def _():
                acc[0][...] += p

            @pl.when(k == nk - 1)
            def _():
                o_ref[...] = acc[0][...].astype(o_ref.dtype)

    a_spec = (pl.BlockSpec((tk, tm), lambda i, j, k: (k, i)) if mode == "tn"
              else pl.BlockSpec((tm, tk), lambda i, j, k: (i, k)))
    if mode == "nt":
        b_blk, b_idx = (tn, tk), (lambda i, j, k: (j, k))
    else:
        b_blk, b_idx = (tk, tn), (lambda i, j, k: (k, j))
    if b_layer is not None:
        b_spec = pl.BlockSpec((None,) + b_blk, lambda i, j, k: (b_layer,) + b_idx(i, j, k))
    else:
        b_spec = pl.BlockSpec(b_blk, b_idx)
    return pl.pallas_call(
        body, name=name,
        grid=(M // tm, N // tn, nk),
        in_specs=[a_spec, b_spec],
        out_specs=pl.BlockSpec((tm, tn), lambda i, j, k: (i, j)),
        out_shape=jax.ShapeDtypeStruct((M, N), out_dtype),
        scratch_shapes=[pltpu.VMEM((tm, tn), F32)] if nk > 1 else [],
        compiler_params=_params("parallel", "parallel", "arbitrary"),
    )(a, b)


def _row_spec(tm, d):
    return pl.BlockSpec((tm, d), lambda i: (i, 0))


def _vec_spec(r, d):
    return pl.BlockSpec((r, d), lambda i: (0, 0))


def _norm_mod(x, scales, shifts, name):
    S, D = x.shape
    nb = scales.shape[0]
    tm = _pick(S, 512, 16)

    def body(x_ref, a_ref, b_ref, *o_refs):
        xv = x_ref[...]
        xh = xv * lax.rsqrt(jnp.mean(xv * xv, axis=-1, keepdims=True) + EPS)
        for n in range(nb):
            o_refs[n][...] = (xh * a_ref[n:n + 1, :] + b_ref[n:n + 1, :]).astype(BF16)

    return pl.pallas_call(
        body, name=name, grid=(S // tm,),
        in_specs=[_row_spec(tm, D), _vec_spec(nb, D), _vec_spec(nb, D)],
        out_specs=[_row_spec(tm, D)] * nb,
        out_shape=[jax.ShapeDtypeStruct((S, D), BF16)] * nb,
        compiler_params=_params("parallel"),
    )(x, scales, shifts)


def _mm_post(a, w, x, gate, name, *, scales=None, shifts=None, target=None):
    M, K = a.shape
    D = w.shape[2]
    tm = _pick(M, 512, 16)
    sub = _pick(tm, 256, 16)
    nb = 0 if scales is None else scales.shape[0]

    def body(a_ref, w_ref, x_ref, g_ref, *rest):
        if target is None:
            sc_ref, sh_ref, y_ref, xn_ref = rest[:4]
            h_refs = rest[4:]
        else:
            t_ref, y_ref, dx_ref, sq_ref = rest

            @pl.when(pl.program_id(0) == 0)
            def _():
                sq_ref[...] = jnp.zeros_like(sq_ref)

        for r in range(tm // sub):
            rows = pl.ds(r * sub, sub)
            yb = jnp.dot(a_ref[rows, :], w_ref[...], preferred_element_type=F32).astype(BF16)
            y_ref[rows, :] = yb
            yv = yb.astype(F32)
            yh = yv * lax.rsqrt(jnp.mean(yv * yv, axis=-1, keepdims=True) + EPS)
            xn = x_ref[rows, :] + yh * g_ref[...]
            if target is None:
                xn_ref[rows, :] = xn
                xh = xn * lax.rsqrt(jnp.mean(xn * xn, axis=-1, keepdims=True) + EPS)
                for n in range(nb):
                    h_refs[n][rows, :] = (xh * sc_ref[n:n + 1, :] + sh_ref[n:n + 1, :]).astype(BF16)
            else:
                e = xn - t_ref[rows, :]
                dx_ref[rows, :] = e / D
                sq_ref[...] += _colsum8(e * e)

    ins = [a, w, x, gate]
    in_specs = [_row_spec(tm, K), pl.BlockSpec((None, K, D), lambda i: (0, 0, 0)), _row_spec(tm, D), _vec_spec(1, D)]
    if target is None:
        ins += [scales, shifts]
        in_specs += [_vec_spec(nb, D), _vec_spec(nb, D)]
        out_specs = [_row_spec(tm, D)] * (2 + nb)
        out_shape = [jax.ShapeDtypeStruct((M, D), BF16), jax.ShapeDtypeStruct((M, D), F32)] \
            + [jax.ShapeDtypeStruct((M, D), BF16)] * nb
    else:
        ins += [target]
        in_specs += [_row_spec(tm, D)]
        out_specs = [_row_spec(tm, D), _row_spec(tm, D), _vec_spec(8, D)]
        out_shape = [jax.ShapeDtypeStruct((M, D), BF16), jax.ShapeDtypeStruct((M, D), F32),
                     jax.ShapeDtypeStruct((8, D), F32)]
    return pl.pallas_call(
        body, name=name, grid=(M // tm,), in_specs=in_specs, out_specs=out_specs, out_shape=out_shape,
        compiler_params=_params("arbitrary" if target is not None else "parallel"),
    )(*ins)


def _post_norm_bwd(dxn, y, gate, name):
    S, D = y.shape
    tm = _pick(S, 512, 16)

    def body(d_ref, y_ref, g_ref, dy_ref, dg_ref):
        yv = y_ref[...].astype(F32)
        dv = d_ref[...]
        r = lax.rsqrt(jnp.mean(yv * yv, axis=-1, keepdims=True) + EPS)
        yh = yv * r
        dyh = dv * g_ref[...]
        dy_ref[...] = (r * (dyh - yh * jnp.mean(dyh * yh, axis=-1, keepdims=True))).astype(BF16)

        @pl.when(pl.program_id(0) == 0)
        def _():
            dg_ref[...] = jnp.zeros_like(dg_ref)

        dg_ref[...] += _colsum8(dv * yh)

    return pl.pallas_call(
        body, name=name, grid=(S // tm,),
        in_specs=[_row_spec(tm, D), _row_spec(tm, D), _vec_spec(1, D)],
        out_specs=[_row_spec(tm, D), _vec_spec(8, D)],
        out_shape=[jax.ShapeDtypeStruct((S, D), BF16), jax.ShapeDtypeStruct((8, D), F32)],
        compiler_params=_params("arbitrary"),
    )(dxn, y, gate)


def _pre_norm_bwd(x, dxn, dhs, scales, name):
    S, D = x.shape
    nb = len(dhs)
    tm = _pick(S, 512, 8)

    def body(x_ref, d_ref, a_ref, *rest):
        dh_refs, dx_ref, ds_ref, db_ref = rest[:nb], rest[nb], rest[nb + 1], rest[nb + 2]
        xv = x_ref[...]
        r = lax.rsqrt(jnp.mean(xv * xv, axis=-1, keepdims=True) + EPS)
        xh = xv * r

        @pl.when(pl.program_id(0) == 0)
        def _():
            ds_ref[...] = jnp.zeros_like(ds_ref)
            db_ref[...] = jnp.zeros_like(db_ref)

        dxh = jnp.zeros_like(xv)
        for n in range(nb):
            dh = dh_refs[n][...].astype(F32)
            dxh = dxh + dh * a_ref[n:n + 1, :]
            ds_ref[n] += _colsum8(dh * xh)
            db_ref[n] += _colsum8(dh)
        dx_ref[...] = d_ref[...] + r * (dxh - xh * jnp.mean(dxh * xh, axis=-1, keepdims=True))

    acc_spec = pl.BlockSpec((nb, 8, D), lambda i: (0, 0, 0))
    return pl.pallas_call(
        body, name=name, grid=(S // tm,),
        in_specs=[_row_spec(tm, D), _row_spec(tm, D), _vec_spec(nb, D)] + [_row_spec(tm, D)] * nb,
        out_specs=[_row_spec(tm, D), acc_spec, acc_spec],
        out_shape=[jax.ShapeDtypeStruct((S, D), F32), jax.ShapeDtypeStruct((nb, 8, D), F32),
                   jax.ShapeDtypeStruct((nb, 8, D), F32)],
        compiler_params=_params("arbitrary"),
    )(x, dxn, scales, *dhs)


FFN_PAIRS = 2
FFN_SUB_ROWS = 256


def _ffn_in_act(h, w, layer, name):
    S, D = h.shape
    F2 = w.shape[2]
    PW = F2 // (2 * FFN_PAIRS)
    tm = _pick(S, 512, 16)
    sub = _pick(tm, FFN_SUB_ROWS, 16)

    def body(h_ref, w_ref, s_ref, a_ref):
        for r in range(tm // sub):
            rows = pl.ds(r * sub, sub)
            acc = jnp.dot(h_ref[rows, :], w_ref[...], preferred_element_type=F32)
            g, u = acc[:, :PW], acc[:, PW:]
            sg = jax.nn.sigmoid(g)
            sl = g * sg
            a_ref[rows, :] = (sl * u).astype(BF16)
            s_ref[rows, 0:PW] = (sg * (1.0 + g * (1.0 - sg))).astype(BF16)
            s_ref[rows, PW:2 * PW] = sl.astype(BF16)
            s_ref[rows, 2 * PW:3 * PW] = u.astype(BF16)

    return pl.pallas_call(
        body, name=name, grid=(FFN_PAIRS, S // tm),
        in_specs=[pl.BlockSpec((tm, D), lambda p, i: (i, 0)),
                  pl.BlockSpec((None, D, 2 * PW), lambda p, i: (layer, 0, p))],
        out_specs=[pl.BlockSpec((tm, 3 * PW), lambda p, i: (i, p)), pl.BlockSpec((tm, PW), lambda p, i: (i, p))],
        out_shape=[jax.ShapeDtypeStruct((S, 3 * FFN_PAIRS * PW), BF16), jax.ShapeDtypeStruct((S, F2 // 2), BF16)],
        compiler_params=_params("parallel", "parallel"),
    )(h, w)


def _ffn_out_dx_act(dy, w, layer, saved, name):
    S, D = dy.shape
    PW = saved.shape[1] // (3 * FFN_PAIRS)
    tm = _pick(S, 512, 16)
    sub = _pick(tm, FFN_SUB_ROWS, 16)

    def body(dy_ref, w_ref, s_ref, o_ref):
        for r in range(tm // sub):
            rows = pl.ds(r * sub, sub)
            da = lax.dot_general(dy_ref[rows, :], w_ref[...], _DIMS["nt"], preferred_element_type=F32)
            o_ref[rows, 0:PW] = (da * s_ref[rows, 2 * PW:3 * PW].astype(F32)
                                 * s_ref[rows, 0:PW].astype(F32)).astype(BF16)
            o_ref[rows, PW:2 * PW] = (da * s_ref[rows, PW:2 * PW].astype(F32)).astype(BF16)

    return pl.pallas_call(
        body, name=name, grid=(FFN_PAIRS, S // tm),
        in_specs=[pl.BlockSpec((tm, D), lambda p, i: (i, 0)),
                  pl.BlockSpec((None, PW, D), lambda p, i: (layer, p, 0)),
                  pl.BlockSpec((tm, 3 * PW), lambda p, i: (i, p))],
        out_specs=pl.BlockSpec((tm, 2 * PW), lambda p, i: (i, p)),
        out_shape=jax.ShapeDtypeStruct((S, 2 * FFN_PAIRS * PW), BF16),
        compiler_params=_params("parallel", "parallel"),
    )(dy, w, saved)


HALO = 16


def _conv_terms(bcx_ref, prev_ref, i, tm, D):
    b = bcx_ref[:, 0:D].astype(F32)
    cg = bcx_ref[:, D:2 * D].astype(F32)
    xin = bcx_ref[:, 2 * D:3 * D].astype(F32)
    z = cg * xin
    zp = prev_ref[:, D:2 * D].astype(F32) * prev_ref[:, 2 * D:3 * D].astype(F32)
    zp = jnp.where(i > 0, zp, 0.0)
    z_ext = jnp.concatenate([zp, z], axis=0)
    z1 = pltpu.roll(z_ext, 1, 0)[HALO:, :]
    z2 = pltpu.roll(z_ext, 2, 0)[HALO:, :]
    return b, cg, xin, z, z1, z2


def _conv_gate(bcx, ck, name):
    S, D3 = bcx.shape
    D = D3 // 3
    tm = _pick(S, 256, 16)
    hb = tm // HALO

    def body(bcx_ref, prev_ref, ck_ref, o_ref):
        i = pl.program_id(0)
        b, _, _, z, z1, z2 = _conv_terms(bcx_ref, prev_ref, i, tm, D)
        conv = ck_ref[0:1, :] * z2 + ck_ref[1:2, :] * z1 + ck_ref[2:3, :] * z
        o_ref[...] = (b * conv).astype(BF16)

    return pl.pallas_call(
        body, name=name, grid=(S // tm,),
        in_specs=[_row_spec(tm, D3),
                  pl.BlockSpec((HALO, D3), lambda i: (jnp.maximum(i * hb - 1, 0), 0)),
                  _vec_spec(8, D)],
        out_specs=_row_spec(tm, D),
        out_shape=jax.ShapeDtypeStruct((S, D), BF16),
        compiler_params=_params("parallel"),
    )(bcx, bcx, ck)


def _conv_gate_bwd(du, bcx, ck, name):
    S, D3 = bcx.shape
    D = D3 // 3
    tm = _pick(S, 256, 16)
    hb = tm // HALO
    nt = S // tm

    def body(du_ref, dun_ref, bcx_ref, prev_ref, next_ref, ck_ref, o_ref, dk_ref):
        i = pl.program_id(0)
        b, cg, xin, z, z1, z2 = _conv_terms(bcx_ref, prev_ref, i, tm, D)
        k0, k1, k2 = ck_ref[0:1, :], ck_ref[1:2, :], ck_ref[2:3, :]
        conv = k0 * z2 + k1 * z1 + k2 * z
        d = du_ref[...].astype(F32)
        dconv = d * b
        dcn = jnp.where(i < nt - 1, dun_ref[...].astype(F32) * next_ref[:, 0:D].astype(F32), 0.0)
        d_ext = jnp.concatenate([dconv, dcn], axis=0)
        d1 = pltpu.roll(d_ext, tm + HALO - 1, 0)[:tm, :]
        d2 = pltpu.roll(d_ext, tm + HALO - 2, 0)[:tm, :]
        dz = k2 * dconv + k1 * d1 + k0 * d2
        o_ref[:, 0:D] = (d * conv).astype(BF16)
        o_ref[:, D:2 * D] = (dz * xin).astype(BF16)
        o_ref[:, 2 * D:3 * D] = (dz * cg).astype(BF16)

        @pl.when(i == 0)
        def _():
            dk_ref[...] = jnp.zeros_like(dk_ref)

        dk_ref[0] += _colsum8(dconv * z2)
        dk_ref[1] += _colsum8(dconv * z1)
        dk_ref[2] += _colsum8(dconv * z)

    last = S // HALO - 1
    return pl.pallas_call(
        body, name=name, grid=(nt,),
        in_specs=[_row_spec(tm, D),
                  pl.BlockSpec((HALO, D), lambda i: (jnp.minimum((i + 1) * hb, last), 0)),
                  _row_spec(tm, D3),
                  pl.BlockSpec((HALO, D3), lambda i: (jnp.maximum(i * hb - 1, 0), 0)),
                  pl.BlockSpec((HALO, D3), lambda i: (jnp.minimum((i + 1) * hb, last), 0)),
                  _vec_spec(8, D)],
        out_specs=[_row_spec(tm, D3), pl.BlockSpec((3, 8, D), lambda i: (0, 0, 0))],
        out_shape=[jax.ShapeDtypeStruct((S, D3), BF16), jax.ShapeDtypeStruct((3, 8, D), F32)],
        compiler_params=_params("arbitrary"),
    )(du, du, bcx, bcx, bcx, ck)


def _rel_onehot():
    a = np.arange(CHUNK)[:, None]
    b = np.arange(CHUNK)[None, :]
    idx = np.stack([np.clip((N_LEFT_CHUNKS - dl) * CHUNK + a - b, -MAX_REL, MAX_REL) + MAX_REL
                    for dl in (6, 7, 8)]).reshape(-1)
    return (jnp.asarray(idx)[:, None] == jnp.arange(N_REL)[None, :]).astype(F32)


def _bias_table(rel_bias, name):
    H = rel_bias.shape[0]
    near = jnp.dot(rel_bias, _rel_onehot().T, precision=lax.Precision.HIGHEST).reshape(H, 3, CHUNK, CHUNK)
    far = jnp.broadcast_to(rel_bias[:, N_REL - 1][:, None, None], (H, CHUNK, CHUNK))

    def body(near_ref, far_ref, o_ref):
        neg = jnp.full((CHUNK, CHUNK), NEG, F32)
        for v in range(N_WIN):
            for ic in range(Q_CHUNKS):
                for jc in range(N_WIN * Q_CHUNKS):
                    dl = jc - ic
                    if dl < 0 or dl > N_LEFT_CHUNKS or jc < (N_WIN - 1 - v) * Q_CHUNKS:
                        blk = neg
                    else:
                        blk = far_ref[...] if dl <= 5 else near_ref[dl - 6]
                    o_ref[v, ic * CHUNK:(ic + 1) * CHUNK, jc * CHUNK:(jc + 1) * CHUNK] = blk

    return pl.pallas_call(
        body, name=name, grid=(H,),
        in_specs=[pl.BlockSpec((None, 3, CHUNK, CHUNK), lambda h: (h, 0, 0, 0)),
                  pl.BlockSpec((None, CHUNK, CHUNK), lambda h: (h, 0, 0))],
        out_specs=pl.BlockSpec((N_WIN, None, BQ, N_WIN * BQ), lambda h: (0, h, 0, 0)),
        out_shape=jax.ShapeDtypeStruct((N_WIN, H, BQ, N_WIN * BQ), F32),
        compiler_params=_params("parallel"),
    )(near, far)


def _bias_table_grad(dtab):
    H = dtab.shape[0]
    blk = lambda ic, jc: dtab[:, ic * CHUNK:(ic + 1) * CHUNK, jc * CHUNK:(jc + 1) * CHUNK]
    by_dl = [sum(blk(ic, ic + dl) for ic in range(Q_CHUNKS)) for dl in range(N_LEFT_CHUNKS + 1)]
    far = sum(jnp.sum(by_dl[dl], axis=(1, 2)) for dl in range(6))
    near = jnp.stack(by_dl[6:9], axis=1).reshape(H, 3 * CHUNK * CHUNK)
    g = jnp.dot(near, _rel_onehot(), precision=lax.Precision.HIGHEST)
    return g.at[:, N_REL - 1].add(far)


def _attn_specs(nblk, W):
    last = nblk - 1
    q_spec = pl.BlockSpec((BQ, W), lambda g, i: (jnp.minimum(i, last), g))
    kv_specs = [pl.BlockSpec((BQ, 2 * W), functools.partial(
        lambda g, i, w: (jnp.maximum(jnp.minimum(i, last) - (N_WIN - 1) + w, 0), g), w=w)) for w in range(N_WIN)]
    tab_spec = pl.BlockSpec((None, HEADS_PER_STEP, BQ, N_WIN * BQ),
                            lambda g, i: (jnp.minimum(i, N_WIN - 1), g, 0, 0))
    dtab_spec = pl.BlockSpec((HEADS_PER_STEP, BQ, N_WIN * BQ), lambda g, i: (g, 0, 0))
    return q_spec, kv_specs, tab_spec, dtab_spec


def _attn_exp(q_ref, kT, tab_ref, h, dh):
    s = jnp.dot(q_ref[:, h * dh:(h + 1) * dh], kT[h * dh:(h + 1) * dh, :], preferred_element_type=F32) + tab_ref[h]
    e = jnp.exp(s - jnp.max(s, axis=-1, keepdims=True))
    return e, jnp.sum(e, axis=-1, keepdims=True)


def _attn_fwd(q, kv, tab, name):
    S, D = q.shape
    dh = D // N_HEADS
    W = HEADS_PER_STEP * dh
    assert 2 * W == D, "the kv layout puts one head group's k beside its v: two head groups"
    q_spec, kv_specs, tab_spec, _ = _attn_specs(S // BQ, W)

    def body(q_ref, *rest):
        tab_ref, o_ref = rest[N_WIN], rest[N_WIN + 1]
        kvw = jnp.concatenate([r[...] for r in rest[:N_WIN]], axis=0)
        kT = kvw[:, :W].T
        vw = kvw[:, W:]
        outs = []
        for h in range(HEADS_PER_STEP):
            e, l = _attn_exp(q_ref, kT, tab_ref, h, dh)
            outs.append(jnp.dot(e.astype(BF16), vw[:, h * dh:(h + 1) * dh], preferred_element_type=F32) / l)
        o_ref[...] = jnp.concatenate(outs, axis=1).astype(BF16)

    return pl.pallas_call(
        body, name=name, grid=(N_HEADS // HEADS_PER_STEP, S // BQ),
        in_specs=[q_spec] + kv_specs + [tab_spec],
        out_specs=q_spec,
        out_shape=jax.ShapeDtypeStruct((S, D), BF16),
        compiler_params=_params("parallel", "parallel"),
    )(q, *([kv] * N_WIN), tab)


def _attn_bwd(q, kv, tab, do, name):
    S, D = q.shape
    dh = D // N_HEADS
    W = HEADS_PER_STEP * dh
    nblk = S // BQ
    q_spec, kv_specs, tab_spec, dtab_spec = _attn_specs(nblk, W)

    def body(q_ref, *rest):
        tab_ref, do_ref, dq_ref, dkv_ref, dtab_ref, ring = rest[N_WIN:]
        i = pl.program_id(1)

        @pl.when(i == 0)
        def _():
            dtab_ref[...] = jnp.zeros_like(dtab_ref)
            ring[...] = jnp.zeros_like(ring)

        @pl.when(i < nblk)
        def _():
            kvw = jnp.concatenate([r[...] for r in rest[:N_WIN]], axis=0)
            kT = kvw[:, :W].T
            vT = kvw[:, W:].T
            qT = q_ref[...].T
            doT = do_ref[...].T
            dqs, dks, dvs = [], [], []
            for h in range(HEADS_PER_STEP):
                hd = slice(h * dh, (h + 1) * dh)
                e, l = _attn_exp(q_ref, kT, tab_ref, h, dh)
                p = e * (1.0 / l)
                dp = jnp.dot(do_ref[:, hd], vT[hd, :], preferred_element_type=F32)
                ds = p * (dp - jnp.sum(p * dp, axis=-1, keepdims=True))
                dtab_ref[h] += ds
                dsb = ds.astype(BF16)
                dqs.append(lax.dot_general(kT[hd, :], dsb, _DIMS["nt"], preferred_element_type=F32) * (dh ** -0.5))
                dks.append(jnp.dot(qT[hd, :], dsb, preferred_element_type=F32))
                dvs.append(jnp.dot(doT[hd, :], p.astype(BF16), preferred_element_type=F32))
            dq_ref[...] = jnp.concatenate(dqs, axis=0).T.astype(BF16)
            dkv = jnp.concatenate(dks + dvs, axis=0).T
            for w in range(N_WIN):
                slot = lax.rem(i + 1 + w, N_WIN)
                part = dkv[w * BQ:(w + 1) * BQ, :]
                if w == N_WIN - 1:
                    ring[slot] = part
                else:
                    ring[slot] += part

        dkv_ref[...] = ring[lax.rem(i + 1, N_WIN)].astype(BF16)

    done_spec = pl.BlockSpec((BQ, 2 * W), lambda g, i: (jnp.maximum(i - (N_WIN - 1), 0), g))
    return pl.pallas_call(
        body, name=name, grid=(N_HEADS // HEADS_PER_STEP, nblk + N_WIN - 1),
        in_specs=[q_spec] + kv_specs + [tab_spec, q_spec],
        out_specs=[q_spec, done_spec, dtab_spec],
        out_shape=[jax.ShapeDtypeStruct((S, D), BF16), jax.ShapeDtypeStruct((S, 2 * D), BF16),
                   jax.ShapeDtypeStruct(tab.shape[1:], F32)],
        scratch_shapes=[pltpu.VMEM((N_WIN, BQ, 2 * W), F32)],
        compiler_params=_params("parallel", "arbitrary"),
    )(q, *([kv] * N_WIN), tab, do)


def _adamw(w, g, m, v, name):
    shape = w.shape
    C = shape[-1]
    R = int(np.prod(shape[:-1])) if len(shape) > 1 else 1
    w2, g2, m2, v2 = (t.reshape(R, C) for t in (w, g, m, v))
    tr = _pick(R, max(8, (512 * 1024) // C // 8 * 8), 8)

    def body(w_ref, g_ref, m_ref, v_ref, d_ref, nm_ref, nv_ref):
        gv = g_ref[...]
        nm = ADAM_B1 * m_ref[...] + (1.0 - ADAM_B1) * gv
        nv = ADAM_B2 * v_ref[...] + (1.0 - ADAM_B2) * jnp.square(gv)
        m_hat = nm / (1.0 - ADAM_B1 ** ADAM_STEP)
        v_hat = nv / (1.0 - ADAM_B2 ** ADAM_STEP)
        d_ref[...] = -ADAM_LR * (m_hat / (jnp.sqrt(v_hat) + ADAM_EPS) + ADAM_WD * w_ref[...])
        nm_ref[...] = nm
        nv_ref[...] = nv

    spec = pl.BlockSpec((tr, C), lambda i: (i, 0))
    outs = pl.pallas_call(
        body, name=name, grid=(R // tr,),
        in_specs=[spec] * 4, out_specs=[spec] * 3,
        out_shape=[jax.ShapeDtypeStruct((R, C), F32)] * 3,
        compiler_params=_params("parallel"),
    )(w2, g2, m2, v2)
    return tuple(o.reshape(shape) for o in outs)


def _sum_rows(a, name):
    n, L = a.shape

    def body(a_ref, o_ref):
        acc = a_ref[0:1, :]
        for r in range(1, n):
            acc = acc + a_ref[r:r + 1, :]
        o_ref[...] = acc

    return pl.pallas_call(
        body, name=name, grid=(1,),
        in_specs=[pl.BlockSpec((n, L), lambda i: (0, 0))],
        out_specs=pl.BlockSpec((1, L), lambda i: (0, 0)),
        out_shape=jax.ShapeDtypeStruct((1, L), F32),
        compiler_params=_params("arbitrary"),
    )(a)


def _scalar_call(body, name, scalar, grid, in_specs, out_spec, out_shape, args):
    return pl.pallas_call(
        body, name=name,
        grid_spec=pltpu.PrefetchScalarGridSpec(num_scalar_prefetch=1, grid=grid, in_specs=in_specs,
                                               out_specs=out_spec),
        out_shape=out_shape, compiler_params=_params("parallel"),
    )(jnp.reshape(scalar, (-1,)).astype(jnp.int32), *args)


def _pair_sum(view, got, c, name):
    nb, _, rh, cols = view.shape
    tr = _pick(rh, max(16, (1 << 20) // cols // 16 * 16), 16)
    bpr = rh // tr

    def body(s_ref, a_ref, b_ref, o_ref):
        o_ref[...] = (a_ref[...].astype(F32) + b_ref[...].astype(F32)).astype(BF16)

    spec = pl.BlockSpec((tr, cols), lambda i, s: (i, 0))
    mine = pl.BlockSpec((tr, cols), lambda i, s: ((2 * (i // bpr) + s[0]) * bpr + i % bpr, 0))
    return _scalar_call(body, name, c, (nb * bpr,), [mine, spec], spec,
                        jax.ShapeDtypeStruct((nb * rh, cols), BF16),
                        (view.reshape(nb * 2 * rh, cols), got.reshape(nb * rh, cols)))


def _owner_sum(pair, recv, me, c, it, name):
    _, rh, bc = recv.shape
    tr = _pick(rh, max(16, (1 << 19) // bc // 16 * 16), 16)
    bpr = rh // tr

    def body(s_ref, a_ref, r0, r1, r2, o_ref):
        o_ref[...] = ((a_ref[...].astype(F32) + r0[...].astype(F32)) + r1[...].astype(F32)) + r2[...].astype(F32)

    if it.kind == "col":
        own = pl.BlockSpec((tr, bc), lambda i, s: (i, s[0]))
    else:
        own = pl.BlockSpec((tr, bc), lambda i, s: (s[0] * bpr + i, 0))
    slots = [pl.BlockSpec((None, tr, bc), functools.partial(lambda i, s, k: (k, i, 0), k=k)) for k in range(3)]
    return _scalar_call(body, name, jnp.stack([it.pos(me), c]), (bpr,), [own] + slots,
                        pl.BlockSpec((tr, bc), lambda i, s: (s[1] * bpr + i, 0)),
                        jax.ShapeDtypeStruct((2 * rh, bc), F32), (pair, recv, recv, recv))


def _place():
    x, y, c = lax.axis_index("x"), lax.axis_index("y"), lax.axis_index("c")
    chips = [(1 - x, y), (x, 1 - y), (1 - x, 1 - y)]
    return x, y, c, chips


def _chip_index(px, py):
    return 2 * px + py


def _all_gather_small(x_shard, name):
    m_per, n = x_shard.shape

    def body(x_ref, out_ref, send_sems, recv_sems, local_sem):
        x, y, c, chips = _place()
        me, sibling = (x, y, c), (x, y, 1 - c)

        def rows(px, py, pc):
            return out_ref.at[pl.ds((4 * px + 2 * py + pc) * m_per, m_per), :]

        def copy(k, block, to, src=None):
            return pltpu.make_async_remote_copy(
                src_ref=rows(*block) if src is None else src, dst_ref=rows(*block),
                send_sem=send_sems.at[k], recv_sem=recv_sems.at[k], device_id=to, device_id_type=MESH)

        mine = pltpu.make_async_copy(x_ref, rows(*me), local_sem)
        mine.start()
        first = [copy(0, me, sibling, src=x_ref)]
        first += [copy(1 + j, me, (*chip, c), src=x_ref) for j, chip in enumerate(chips)]
        for cp in first:
            cp.start()
        passed = [copy(4 + j, (*chip, c), sibling) for j, chip in enumerate(chips)]
        for j, chip in enumerate(chips):
            copy(1 + j, (*chip, c), me).wait_recv()
            passed[j].start()
        copy(0, sibling, me).wait_recv()
        for j, chip in enumerate(chips):
            copy(4 + j, (*chip, 1 - c), me).wait_recv()
        for cp in first + passed:
            cp.wait_send()
        mine.wait()

    return pl.pallas_call(
        body, name=name,
        out_shape=jax.ShapeDtypeStruct((N_DEV * m_per, n), x_shard.dtype),
        in_specs=[pl.BlockSpec(memory_space=pltpu.VMEM)],
        out_specs=pl.BlockSpec(memory_space=pltpu.VMEM),
        scratch_shapes=[pltpu.SemaphoreType.DMA((7,)), pltpu.SemaphoreType.DMA((7,)), pltpu.SemaphoreType.DMA],
    )(x_shard)


def _gather_flat(vec, name):
    L = vec.shape[0]
    Lp = -(-L // 1024) * 1024
    g = _all_gather_small(jnp.pad(vec, (0, Lp - L)).reshape(8, Lp // 8), name)
    return g.reshape(N_DEV, Lp)[:, :L]


class _Item:
    def __init__(self, kind, rows, cols, arg, layer, swap=False):
        self.kind, self.rows, self.cols, self.arg, self.layer, self.swap = kind, rows, cols, arg, layer, swap

    def ref(self, refs):
        return refs[self.arg].at[self.layer]

    def pos(self, j):
        return 2 * (j % 2) + j // 2 if self.swap else j


def _block(ref, it, j, half):
    if it.kind == "col":
        ns = it.cols // N_CHIP
        return ref.at[pl.ds(half * (it.rows // 2), it.rows // 2), pl.ds(it.pos(j) * ns, ns)]
    rs = it.rows // N_CHIP
    return ref.at[pl.ds(j * rs + half * (rs // 2), rs // 2), :]


def _cast_place(w, layer, kind, pos, name):
    _, r, n = w.shape
    tr = _pick(r, max(16, (1 << 20) // n // 16 * 16), 16)
    bpr = r // tr

    def body(s_ref, w_ref, o_ref):
        o_ref[...] = w_ref[...].astype(BF16)

    if kind == "col":
        full, out_idx = (1, r, N_CHIP * n), (lambda i, s: (0, i, s[0]))
    else:
        full, out_idx = (1, N_CHIP * r, n), (lambda i, s: (0, s[0] * bpr + i, 0))
    return pl.pallas_call(
        body, name=name,
        grid_spec=pltpu.PrefetchScalarGridSpec(
            num_scalar_prefetch=1, grid=(bpr,),
            in_specs=[pl.BlockSpec((None, tr, n), lambda i, s: (layer, i, 0))],
            out_specs=pl.BlockSpec((None, tr, n), out_idx)),
        out_shape=jax.ShapeDtypeStruct(full, BF16),
        compiler_params=_params("parallel"),
    )(jnp.reshape(pos, (1,)).astype(jnp.int32), w)


HBM_SPEC = pl.BlockSpec(memory_space=pltpu.HBM)
SEM_SPEC = pl.BlockSpec(memory_space=pltpu.SEMAPHORE)
ANY_SPEC = pl.BlockSpec(memory_space=pl.ANY)
SPLIT_PARAMS = dict(has_side_effects=pltpu.SideEffectType.DATAFLOW_SIDE_EFFECTING)


def _in_hbm(a):
    return pltpu.with_memory_space_constraint(a, pltpu.HBM)


def _split_start(copies_of, bufs, n_sem, after, name):
    n = len(bufs)

    def body(*refs):
        ins, send, recv, token = refs[:n], refs[n + 1], refs[n + 2], refs[2 * n + 3]
        for cp in copies_of(ins, send, recv, False)[0]:
            cp.start()
        token[...] = jnp.zeros_like(token)

    outs = pl.pallas_call(
        body, name=name,
        out_shape=(pltpu.SemaphoreType.DMA(n_sem), pltpu.SemaphoreType.DMA(n_sem),
                   *[pltpu.HBM(b.shape, b.dtype) for b in bufs], jax.ShapeDtypeStruct((8, 128), F32)),
        in_specs=[HBM_SPEC] * n + [ANY_SPEC],
        out_specs=(SEM_SPEC, SEM_SPEC, *[HBM_SPEC] * n, pl.BlockSpec(memory_space=pltpu.VMEM)),
        input_output_aliases={t: 2 + t for t in range(n)},
        compiler_params=pltpu.CompilerParams(**SPLIT_PARAMS),
    )(*[_in_hbm(b) for b in bufs], after)
    return outs[0], outs[1], list(outs[2:2 + n]), outs[2 + n]


def _split_wait(copies_of, send, recv, bufs, after, name):
    n = len(bufs)

    def body(*refs):
        ins, send_ref, recv_ref = refs[:n], refs[n], refs[n + 1]
        sends, arrivals = copies_of(ins, send_ref, recv_ref, True)
        for cp in sends:
            cp.wait_send()
        for cp in arrivals:
            cp.wait_recv()

    return pl.pallas_call(
        body, name=name,
        out_shape=[pltpu.HBM(b.shape, b.dtype) for b in bufs],
        in_specs=[HBM_SPEC] * n + [SEM_SPEC, SEM_SPEC, ANY_SPEC],
        out_specs=[HBM_SPEC] * n,
        input_output_aliases={t: t for t in range(n)},
        compiler_params=pltpu.CompilerParams(**SPLIT_PARAMS),
    )(*bufs, send, recv, after)


def _gather_copies(items):
    def copies_of(refs, send, recv, with_arrivals):
        x, y, c, chips = _place()
        me = _chip_index(x, y)
        sends, arrivals = [], []
        for t, it in enumerate(items):
            for k, chip in enumerate(chips):
                for core in range(2):
                    mine = _block(it.ref(refs), it, me, c)
                    sends.append(pltpu.make_async_remote_copy(
                        src_ref=mine, dst_ref=mine, send_sem=send.at[6 * t + 2 * k + core],
                        recv_sem=recv.at[6 * t + 2 * k + c], device_id=(*chip, core), device_id_type=MESH))
                    if with_arrivals:
                        landed = _block(it.ref(refs), it, _chip_index(*chip), core)
                        arrivals.append(pltpu.make_async_remote_copy(
                            src_ref=landed, dst_ref=landed, send_sem=send.at[6 * t + 2 * k + core],
                            recv_sem=recv.at[6 * t + 2 * k + core], device_id=(*chip, core), device_id_type=MESH))
        return sends, arrivals

    return copies_of


def _owner_copies(items):
    n = len(items)

    def blk(ref, it, j):
        if it.kind == "col":
            ns = it.cols // N_CHIP
            return ref.at[:, pl.ds(it.pos(j) * ns, ns)]
        return ref.at[j]

    def copies_of(refs, send, recv, with_arrivals):
        x, y, c, chips = _place()
        sends, arrivals = [], []
        for t, it in enumerate(items):
            for k, chip in enumerate(chips):
                slot = refs[n + t].at[k]
                sends.append(pltpu.make_async_remote_copy(
                    src_ref=blk(refs[t], it, _chip_index(*chip)), dst_ref=slot, send_sem=send.at[3 * t + k],
                    recv_sem=recv.at[3 * t + k], device_id=(*chip, c), device_id_type=MESH))
                if with_arrivals:
                    arrivals.append(pltpu.make_async_remote_copy(
                        src_ref=slot, dst_ref=slot, send_sem=send.at[3 * t + k], recv_sem=recv.at[3 * t + k],
                        device_id=(*chip, c), device_id_type=MESH))
        return sends, arrivals

    return copies_of


def _owner_slot_shape(it):
    if it.kind == "col":
        return (3, it.rows // 2, it.cols // N_CHIP)
    return (3, it.rows // (2 * N_CHIP), it.cols)


def _pair_view(g, it):
    if it.kind == "col":
        return g.reshape(1, 2, it.rows // 2, it.cols)
    return g.reshape(N_CHIP, 2, it.rows // (2 * N_CHIP), it.cols)


def _pair_copies(n):
    def copies_of(refs, send, recv, with_arrivals):
        x, y, c, _ = _place()
        sends, arrivals = [], []
        for t in range(n):
            land = refs[n + t]
            sends.append(pltpu.make_async_remote_copy(
                src_ref=refs[t].at[:, pl.ds(1 - c, 1)], dst_ref=land, send_sem=send.at[t], recv_sem=recv.at[t],
                device_id=(x, y, 1 - c), device_id_type=MESH))
            if with_arrivals:
                arrivals.append(pltpu.make_async_remote_copy(
                    src_ref=land, dst_ref=land, send_sem=send.at[t], recv_sem=recv.at[t],
                    device_id=(x, y, 1 - c), device_id_type=MESH))
        return sends, arrivals

    return copies_of


def _half_copies(n):
    def copies_of(refs, send, recv, with_arrivals):
        x, y, c, _ = _place()
        sends, arrivals = [], []
        for t in range(n):
            r2 = refs[t].shape[0] // 2
            mine = refs[t].at[pl.ds(c * r2, r2), :]
            sends.append(pltpu.make_async_remote_copy(
                src_ref=mine, dst_ref=mine, send_sem=send.at[t], recv_sem=recv.at[t],
                device_id=(x, y, 1 - c), device_id_type=MESH))
            if with_arrivals:
                theirs = refs[t].at[pl.ds((1 - c) * r2, r2), :]
                arrivals.append(pltpu.make_async_remote_copy(
                    src_ref=theirs, dst_ref=theirs, send_sem=send.at[t], recv_sem=recv.at[t],
                    device_id=(x, y, 1 - c), device_id_type=MESH))
        return sends, arrivals

    return copies_of


class _Reduction:
    pass


def _pair_start(grads, items, after, tag):
    n = len(items)
    views = [_pair_view(g, it) for g, it in zip(grads, items)]
    lands = [lax.empty((v.shape[0], 1) + v.shape[2:], v.dtype) for v in views]
    r = _Reduction()
    r.items, r.tag = items, tag
    r.send, r.recv, r.bufs, r.token = _split_start(_pair_copies(n), views + lands, (n,), after, f"rs_pair_start_{tag}")
    return r


def _owner_start(r, after):
    x, y, c, _ = _place()
    n = len(r.items)
    bufs = _split_wait(_pair_copies(n), r.send, r.recv, r.bufs, after, f"rs_pair_wait_{r.tag}")
    pairs = [_pair_sum(bufs[t], bufs[n + t], c, f"rs_pair_sum_{r.tag}_{t}") for t in range(n)]
    shaped = [p if it.kind == "col" else p.reshape(N_CHIP, p.shape[0] // N_CHIP, p.shape[1])
              for p, it in zip(pairs, r.items)]
    lands = [lax.empty(_owner_slot_shape(it), BF16) for it in r.items]
    r.send, r.recv, r.bufs, r.token = _split_start(
        _owner_copies(r.items), shaped + lands, (3 * n,), r.token, f"rs_owner_start_{r.tag}")
    return r


def _reduce_finish(groups, after):
    x, y, c, _ = _place()
    me = _chip_index(x, y)
    halves = []
    for r in groups:
        n = len(r.items)
        bufs = _split_wait(_owner_copies(r.items), r.send, r.recv, r.bufs, after, f"rs_owner_wait_{r.tag}")
        for t, it in enumerate(r.items):
            pair = bufs[t].reshape(-1, bufs[t].shape[-1])
            halves.append(_owner_sum(pair, bufs[n + t], me, c, it, f"rs_owner_sum_{r.tag}_{t}"))
    n = len(halves)
    return _split_start(_half_copies(n), halves, (n,), after, "rs_half_start")


def _silu(v):
    return v * jax.nn.sigmoid(v)


def _sum8(p):
    return jnp.sum(p, axis=-2)


def kernel(x, c, mod_w, mod_b, norm_g, ffn_w_in, ffn_w_out, conv_w_in, conv_k, conv_w_out, kv_mod_w, kv_mod_b, kv_norm_g, w_kv, attn_w_q, attn_w_o, rel_bias, loss_target, m_mod_w, m_mod_b, m_norm_g, m_ffn_w_in, m_ffn_w_out, m_conv_w_in, m_conv_k, m_conv_w_out, m_kv_mod_w, m_kv_mod_b, m_kv_norm_g, m_w_kv, m_attn_w_q, m_attn_w_o, m_rel_bias, v_mod_w, v_mod_b, v_norm_g, v_ffn_w_in, v_ffn_w_out, v_conv_w_in, v_conv_k, v_conv_w_out, v_kv_mod_w, v_kv_mod_b, v_kv_norm_g, v_w_kv, v_attn_w_q, v_attn_w_o, v_rel_bias):
    xi, yi, ci = lax.axis_index("x"), lax.axis_index("y"), lax.axis_index("c")
    chip = 2 * xi + yi
    dev = 2 * chip + ci
    _, S, D = x.shape
    F = ffn_w_out.shape[1] * N_CHIP
    x0 = x.reshape(S, D)
    target = loss_target.reshape(S, D)
    n_mod = mod_w.shape[2]
    n_kvm = kv_mod_w.shape[1]
    dsh = D // N_CHIP
    TF = F // 2

    c_all = _all_gather_small(c.reshape(8, D // 8), "ag_c").reshape(N_DEV, D)
    sc16 = jnp.pad(_silu(c_all), ((0, 8), (0, 0)))
    part = [_mm(sc16, mod_w, "nn", F32, f"mod_fwd_{l}", b_layer=l)[:8] for l in range(2)]
    part.append(_mm(sc16, kv_mod_w, "nn", F32, "mod_fwd_kv")[:8])
    fwd_vec = jnp.concatenate([p.reshape(-1) for p in part] + [norm_g.reshape(-1), conv_k.reshape(-1)])
    fwd_all = _gather_flat(fwd_vec, "ag_fwd_small")[0::2]
    o = 0
    mods = []
    for n in (n_mod, n_mod, n_kvm):
        blk = fwd_all[:, o:o + 8 * n].reshape(N_CHIP, 8, n)
        mods.append(lax.dynamic_index_in_dim(blk, dev, axis=1, keepdims=False).reshape(N_CHIP * n))
        o += 8 * n
    ng = fwd_all[:, o:o + 8 * dsh].reshape(N_CHIP, 2, 4, dsh).transpose(1, 2, 0, 3).reshape(2, 4, D)
    o += 8 * dsh
    ck = fwd_all[:, o:o + 3 * dsh].reshape(N_CHIP, 3, dsh).transpose(1, 0, 2).reshape(3, D)
    ck8 = jnp.pad(ck, ((0, 5), (0, 0)))
    mod = [mods[l] + mod_b[l] for l in range(2)]
    sh1, sc1, g1, sh2, sc2, g2 = zip(*[jnp.split(m, 6) for m in mod])
    kv_sh, kv_sc = jnp.split(mods[2] + kv_mod_b, 2)
    row = lambda v: v.reshape(1, D)

    it_conv = [_Item("col", D, 3 * D, 0, 0), _Item("row", D, D, 1, 0)]
    it_ffn = [_Item("col", D, 2 * F, 0, 0, swap=True), _Item("row", F, D, 1, 0)]
    it_attn = [_Item("col", D, 2 * D, 0, 0, swap=True), _Item("row", D, D, 1, 0), _Item("row", D, D, 2, 0)]

    def placed(w, layer, it, nm, after=None):
        pos = it.pos(chip)
        if after is not None:
            pos = pos + after[0, 0].astype(jnp.int32)
        return _cast_place(w, layer, it.kind, pos, f"place_{nm}")

    flying = {}

    def start(tag, its, bufs, after):
        send, recv, bufs, tok = _split_start(_gather_copies(its), bufs, (6 * len(its),), after, f"ag_start_{tag}")
        flying[tag] = (its, send, recv, bufs)
        return tok

    def arrived(tag, after):
        its, send, recv, bufs = flying[tag]
        return _split_wait(_gather_copies(its), send, recv, bufs, after, f"ag_wait_{tag}")

    one = lambda it: [_Item(it.kind, it.rows, it.cols, 0, 0, it.swap)]
    tok = start("conv_in", one(it_conv[0]), [placed(conv_w_in, 0, it_conv[0], "conv_w_in")], fwd_all)
    tok = start("conv_out", one(it_conv[1]), [placed(conv_w_out, 0, it_conv[1], "conv_w_out", tok)], tok)
    tok = start("ffn0_in", one(it_ffn[0]), [placed(ffn_w_in, 0, it_ffn[0], "ffn_w_in0", tok)], tok)
    tok = start("ffn0_out", one(it_ffn[1]), [placed(ffn_w_out, 0, it_ffn[1], "ffn_w_out0", tok)], tok)
    tok = start("attn", it_attn, [placed(w_kv[None], 0, it_attn[0], "w_kv", tok),
                                  placed(attn_w_q, 0, it_attn[1], "attn_w_q", tok),
                                  placed(attn_w_o, 0, it_attn[2], "attn_w_o", tok)], tok)
    token = start("ffn1", it_ffn, [placed(ffn_w_in, 1, it_ffn[0], "ffn_w_in1", tok),
                                   placed(ffn_w_out, 1, it_ffn[1], "ffn_w_out1", tok)], tok)

    a1 = row(ng[0, 0] * (1.0 + sc1[0])) + token[0, 0]
    (h1,) = _norm_mod(x0, a1, row(sh1[0]), "l0_norm1")
    tab = _bias_table(rel_bias[0], "l1_bias_table")
    h1, tab = lax.optimization_barrier((h1, tab))
    (W_cin,) = arrived("conv_in", h1)
    bcx = _mm(h1, W_cin, "nn", BF16, "l0_conv_in", b_layer=0)
    ug = _conv_gate(bcx, ck8, "l0_conv_gate")
    gt1 = row(g1[0] * ng[0, 1])
    a2 = row(ng[0, 2] * (1.0 + sc2[0]))
    (W_cout,) = arrived("conv_out", ug)
    y1, x1, h2 = _mm_post(ug, W_cout, x0, gt1, "l0_conv_out", scales=a2, shifts=row(sh2[0]))
    (W_fin0,) = arrived("ffn0_in", h2)
    gu0, act0 = _ffn_in_act(h2, W_fin0, 0, "l0_ffn_in")
    (W_fout0,) = arrived("ffn0_out", act0)
    gt2 = row(g2[0] * ng[0, 3])
    a3 = ng[1, 0] * (1.0 + sc1[1])
    akv = kv_norm_g * (1.0 + kv_sc)
    y2, x2, h3, hkv = _mm_post(act0, W_fout0, x1, gt2, "l0_ffn_out",
                               scales=jnp.stack([a3, akv]), shifts=jnp.stack([sh1[1], kv_sh]))
    W_kv, W_q, W_o = arrived("attn", hkv)
    kvp = _mm(hkv, W_kv, "nn", BF16, "l1_kv", b_layer=0)
    att_scale = (D // N_HEADS) ** -0.5
    assert math.log2(att_scale) % 1 == 0, "scaling q before its bf16 cast is exact only for a power of two"
    qp = _mm(h3, W_q, "nn", BF16, "l1_q", b_layer=0, scale=att_scale)
    oh = _attn_fwd(qp, kvp, tab, "l1_attn")
    gt3 = row(g1[1] * ng[1, 1])
    a4 = row(ng[1, 2] * (1.0 + sc2[1]))
    y3, x3, h4 = _mm_post(oh, W_o, x2, gt3, "l1_attn_out", scales=a4, shifts=row(sh2[1]))
    W_fin1, W_fout1 = arrived("ffn1", h4)
    gu1, act1 = _ffn_in_act(h4, W_fin1, 0, "l1_ffn_in")
    gt4 = row(g2[1] * ng[1, 3])
    y4, dx4, sq = _mm_post(act1, W_fout1, x3, gt4, "l1_ffn_out", target=target)
    loss_part = 0.5 * jnp.sum(sq) / D

    def ffn_bwd(dxn, xin_, h, gu, act, y, gt, a, w_in, w_out, tag):
        dy, dgt = _post_norm_bwd(dxn, y, gt, f"{tag}_post2_bwd")
        dgu = _ffn_out_dx_act(dy, w_out, 0, gu, f"{tag}_ffn_out_dx")
        g_fout = _mm(act, dy, "tn", BF16, f"{tag}_ffn_out_dw", tm=TF)
        dh = _mm(dgu, w_in, "nt", BF16, f"{tag}_ffn_in_dx", b_layer=0)
        g_fin = _mm(h, dgu, "tn", BF16, f"{tag}_ffn_in_dw", tn=TF)
        dx, ds, db = _pre_norm_bwd(xin_, dxn, [dh], a, f"{tag}_norm2_bwd")
        return dx, _sum8(dgt), _sum8(ds)[0], _sum8(db)[0], g_fin, g_fout

    dx3, dgt4, da4, db4, G_fin1, G_fout1 = ffn_bwd(dx4, x3, h4, gu1, act1, y4, gt4, a4, W_fin1, W_fout1, "l1")
    red = [_pair_start([G_fin1, G_fout1], it_ffn, token, "ffn1")]
    dy3, dgt3 = _post_norm_bwd(dx3, y3, gt3, "l1_post1_bwd")
    doh = _mm(dy3, W_o, "nt", BF16, "l1_attn_out_dx", b_layer=0)
    G_o = _mm(oh, dy3, "tn", BF16, "l1_attn_out_dw")
    _owner_start(red[0], G_o)
    dq, dkv, dtab = _attn_bwd(qp, kvp, tab, doh, "l1_attn_bwd")
    d_rel = _bias_table_grad(dtab)
    dh3 = _mm(dq, W_q, "nt", BF16, "l1_q_dx", b_layer=0)
    G_q = _mm(h3, dq, "tn", BF16, "l1_q_dw")
    dhkv = _mm(dkv, W_kv, "nt", BF16, "l1_kv_dx", b_layer=0)
    G_kv = _mm(hkv, dkv, "tn", BF16, "l1_kv_dw")
    red.append(_pair_start([G_kv, G_q, G_o], it_attn, red[-1].token, "attn"))
    dx2, ds3, db3 = _pre_norm_bwd(x2, dx3, [dh3, dhkv], jnp.stack([a3, akv]), "l1_norm1_bwd")
    _owner_start(red[1], dx2)
    ds3, db3 = _sum8(ds3), _sum8(db3)

    dx1, dgt2, da2, db2, G_fin0, G_fout0 = ffn_bwd(dx2, x1, h2, gu0, act0, y2, gt2, a2, W_fin0, W_fout0, "l0")
    red.append(_pair_start([G_fin0, G_fout0], it_ffn, red[-1].token, "ffn0"))
    dy1, dgt1 = _post_norm_bwd(dx1, y1, gt1, "l0_post1_bwd")
    _owner_start(red[2], dy1)
    dug = _mm(dy1, W_cout, "nt", BF16, "l0_conv_out_dx", b_layer=0)
    G_cout = _mm(ug, dy1, "tn", BF16, "l0_conv_out_dw")
    dbcx, dck = _conv_gate_bwd(dug, bcx, ck8, "l0_conv_gate_bwd")
    dh1 = _mm(dbcx, W_cin, "nt", BF16, "l0_conv_in_dx", b_layer=0)
    G_cin = _mm(h1, dbcx, "tn", BF16, "l0_conv_in_dw")
    red.append(_pair_start([G_cin, G_cout], it_conv, red[-1].token, "conv"))
    dx0, ds1, db1 = _pre_norm_bwd(x0, dx1, [dh1], a1, "l0_norm1_bwd")
    _owner_start(red[3], dx0)
    ds1, db1 = _sum8(ds1)[0], _sum8(db1)[0]
    dgt1, dgt3 = _sum8(dgt1), _sum8(dgt3)

    def dmod_of(l, ds_a, db_a, dgt_a, ds_b, db_b, dgt_b):
        return jnp.concatenate([db_a, ds_a * ng[l, 0], dgt_a * ng[l, 1], db_b, ds_b * ng[l, 2], dgt_b * ng[l, 3]])

    dmod0 = dmod_of(0, ds1, db1, dgt1, da2, db2, dgt2)
    dmod1 = dmod_of(1, ds3[0], db3[0], dgt3, da4, db4, dgt4)
    dkvmod = jnp.concatenate([db3[1], ds3[1] * kv_norm_g])
    dng = jnp.stack([
        jnp.stack([ds1 * (1.0 + sc1[0]), dgt1 * g1[0], da2 * (1.0 + sc2[0]), dgt2 * g2[0]]),
        jnp.stack([ds3[0] * (1.0 + sc1[1]), dgt3 * g1[1], da4 * (1.0 + sc2[1]), dgt4 * g2[1]])])
    dkvng = ds3[1] * (1.0 + kv_sc)
    small = [dmod0, dmod1, dkvmod, dng.reshape(-1), dkvng, _sum8(dck).reshape(-1), d_rel.reshape(-1),
             loss_part.reshape(1)]
    sizes = [int(s.shape[0]) for s in small]
    offs = np.concatenate([[0], np.cumsum(sizes)])
    bwd_all = _gather_flat(jnp.concatenate(small), "ag_bwd_small")
    Lb = bwd_all.shape[1]
    Lp = -(-Lb // 128) * 128
    tot = _sum_rows(jnp.pad(bwd_all, ((0, 0), (0, Lp - Lb))), "sum_small")[0]
    seg = lambda i: tot[offs[i]:offs[i + 1]]
    g_mod_b = jnp.stack([seg(0), seg(1)])
    g_kv_mod_b = seg(2)
    g_norm_g = lax.dynamic_slice_in_dim(seg(3).reshape(2, 4, D), chip * dsh, dsh, axis=2)
    g_kv_norm_g = seg(4)
    g_conv_k = lax.dynamic_slice_in_dim(seg(5).reshape(1, 3, D), chip * dsh, dsh, axis=2)
    g_rel_bias = seg(6).reshape(rel_bias.shape)
    loss = seg(7)[0]

    def dmod_w(i, n, name):
        rows_ = lax.dynamic_slice_in_dim(bwd_all[:, offs[i]:offs[i + 1]], chip * n, n, axis=1)
        return _mm(sc16, jnp.pad(rows_, ((0, 8), (0, 0))), "tn", F32, name)

    g_mod_w = jnp.stack([dmod_w(0, n_mod, "mod_bwd_0"), dmod_w(1, n_mod, "mod_bwd_1")])
    g_kv_mod_w = dmod_w(2, n_kvm, "mod_bwd_kv")

    half_send, half_recv, half_bufs, _ = _reduce_finish(red, g_kv_mod_w)
    grads = {
        "mod_w": g_mod_w, "mod_b": g_mod_b, "norm_g": g_norm_g, "conv_k": g_conv_k,
        "kv_mod_w": g_kv_mod_w, "kv_mod_b": g_kv_mod_b, "kv_norm_g": g_kv_norm_g, "rel_bias": g_rel_bias,
    }
    weights = dict(mod_w=mod_w, mod_b=mod_b, norm_g=norm_g, ffn_w_in=ffn_w_in, ffn_w_out=ffn_w_out,
                   conv_w_in=conv_w_in, conv_k=conv_k, conv_w_out=conv_w_out, kv_mod_w=kv_mod_w,
                   kv_mod_b=kv_mod_b, kv_norm_g=kv_norm_g, w_kv=w_kv, attn_w_q=attn_w_q, attn_w_o=attn_w_o,
                   rel_bias=rel_bias)
    m_in = dict(mod_w=m_mod_w, mod_b=m_mod_b, norm_g=m_norm_g, ffn_w_in=m_ffn_w_in, ffn_w_out=m_ffn_w_out,
                conv_w_in=m_conv_w_in, conv_k=m_conv_k, conv_w_out=m_conv_w_out, kv_mod_w=m_kv_mod_w,
                kv_mod_b=m_kv_mod_b, kv_norm_g=m_kv_norm_g, w_kv=m_w_kv, attn_w_q=m_attn_w_q,
                attn_w_o=m_attn_w_o, rel_bias=m_rel_bias)
    v_in = dict(mod_w=v_mod_w, mod_b=v_mod_b, norm_g=v_norm_g, ffn_w_in=v_ffn_w_in, ffn_w_out=v_ffn_w_out,
                conv_w_in=v_conv_w_in, conv_k=v_conv_k, conv_w_out=v_conv_w_out, kv_mod_w=v_kv_mod_w,
                kv_mod_b=v_kv_mod_b, kv_norm_g=v_kv_norm_g, w_kv=v_w_kv, attn_w_q=v_attn_w_q,
                attn_w_o=v_attn_w_o, rel_bias=v_rel_bias)
    names = list(weights)
    step = {}

    def update(n):
        g = grads[n].reshape(weights[n].shape)
        step[n] = (g, *_adamw(weights[n], g, m_in[n], v_in[n], f"adamw_{n}"))

    for n in list(grads):
        update(n)
    r_fin1, r_fout1, r_kv, r_q, r_o, r_fin0, r_fout0, r_cin, r_cout = _split_wait(
        _half_copies(len(half_bufs)), half_send, half_recv, half_bufs, step["mod_w"][1], "rs_half_wait")
    grads.update({
        "ffn_w_in": jnp.stack([r_fin0, r_fin1]), "ffn_w_out": jnp.stack([r_fout0, r_fout1]),
        "conv_w_in": r_cin[None], "conv_w_out": r_cout[None], "w_kv": r_kv,
        "attn_w_q": r_q[None], "attn_w_o": r_o[None],
    })
    for n in names:
        if n not in step:
            update(n)
    return (loss, dx0.reshape(x.shape), *[step[n][k] for k in range(4) for n in names])
```

```python
import functools
import math

import numpy as np
import jax
import jax.numpy as jnp
from jax import lax
from jax.experimental import pallas as pl
from jax.experimental.pallas import tpu as pltpu

CHUNK = 64
N_LEFT_CHUNKS = 8
N_HEADS = 16
MAX_REL = 2 * CHUNK
N_REL = 2 * MAX_REL + 1
EPS = 1e-6
ADAM_LR = 0.001
ADAM_B1 = 0.9
ADAM_B2 = 0.999
ADAM_EPS = 1e-08
ADAM_WD = 0.01
ADAM_STEP = 10

Q_CHUNKS = 4
BQ = Q_CHUNKS * CHUNK
N_WIN = 1 + N_LEFT_CHUNKS // Q_CHUNKS
HEADS_PER_STEP = 8
NEG = -1e30
N_DEV = 8
N_CHIP = 4

BF16 = jnp.bfloat16
F32 = jnp.float32
V7X_VMEM_LIMIT_BYTES = 56 * 1024 * 1024
MESH = pl.DeviceIdType.MESH


def _pick(n, pref, align):
    t = min(pref, n)
    t -= t % align
    while t >= align:
        if n % t == 0:
            return t
        t -= align
    return n


def _params(*sem):
    return pltpu.CompilerParams(dimension_semantics=sem, vmem_limit_bytes=V7X_VMEM_LIMIT_BYTES)


def _colsum8(v):
    r, d = v.shape
    return v.reshape(r // 8, 8, d).sum(axis=0)


_DIMS = {"nn": (((1,), (0,)), ((), ())), "nt": (((1,), (1,)), ((), ())), "tn": (((0,), (0,)), ((), ()))}


def _mm(a, b, mode, out_dtype, name, *, b_layer=None, tm=1024, tn=1024, tk=None, scale=None):
    if tk is None:
        tk = 2048 if mode == "tn" else 3072
    bs = b.shape[1:] if b_layer is not None else b.shape
    if mode == "nn":
        (M, K), (K2, N) = a.shape, bs
    elif mode == "nt":
        (M, K), (N, K2) = a.shape, bs
    else:
        (K, M), (K2, N) = a.shape, bs
    assert K == K2, (name, a.shape, b.shape)
    tm = _pick(M, tm, 128 if mode == "tn" else 16)
    tn = _pick(N, tn, 128)
    tk = _pick(K, tk, 128 if mode != "tn" else 16)
    nk = K // tk
    assert scale is None or nk == 1, name
    dims = _DIMS[mode]

    def body(a_ref, b_ref, o_ref, *acc):
        p = lax.dot_general(a_ref[...].astype(BF16), b_ref[...].astype(BF16), dims,
                            preferred_element_type=F32)
        if nk == 1:
            o_ref[...] = (p if scale is None else p * scale).astype(o_ref.dtype)
        else:
            k = pl.program_id(2)

            @pl.when(k == 0)
            def _():
                acc[0][...] = p

            @pl.when(k > 0)
            def _():
                acc[0][...] += p

            @pl.when(k == nk - 1)
            def _():
                o_ref[...] = acc[0][...].astype(o_ref.dtype)

    a_spec = (pl.BlockSpec((tk, tm), lambda i, j, k: (k, i)) if mode == "tn"
              else pl.BlockSpec((tm, tk), lambda i, j, k: (i, k)))
    if mode == "nt":
        b_blk, b_idx = (tn, tk), (lambda i, j, k: (j, k))
    else:
        b_blk, b_idx = (tk, tn), (lambda i, j, k: (k, j))
    if b_layer is not None:
        b_spec = pl.BlockSpec((None,) + b_blk, lambda i, j, k: (b_layer,) + b_idx(i, j, k))
    else:
        b_spec = pl.BlockSpec(b_blk, b_idx)
    return pl.pallas_call(
        body, name=name,
        grid=(M // tm, N // tn, nk),
        in_specs=[a_spec, b_spec],
        out_specs=pl.BlockSpec((tm, tn), lambda i, j, k: (i, j)),
        out_shape=jax.ShapeDtypeStruct((M, N), out_dtype),
        scratch_shapes=[pltpu.VMEM((tm, tn), F32)] if nk > 1 else [],
        compiler_params=_params("parallel", "parallel", "arbitrary"),
    )(a, b)


def _row_spec(tm, d):
    return pl.BlockSpec((tm, d), lambda i: (i, 0))


def _vec_spec(r, d):
    return pl.BlockSpec((r, d), lambda i: (0, 0))


def _norm_mod(x, scales, shifts, name):
    S, D = x.shape
    nb = scales.shape[0]
    tm = _pick(S, 512, 16)

    def body(x_ref, a_ref, b_ref, *o_refs):
        xv = x_ref[...]
        xh = xv * lax.rsqrt(jnp.mean(xv * xv, axis=-1, keepdims=True) + EPS)
        for n in range(nb):
            o_refs[n][...] = (xh * a_ref[n:n + 1, :] + b_ref[n:n + 1, :]).astype(BF16)

    return pl.pallas_call(
        body, name=name, grid=(S // tm,),
        in_specs=[_row_spec(tm, D), _vec_spec(nb, D), _vec_spec(nb, D)],
        out_specs=[_row_spec(tm, D)] * nb,
        out_shape=[jax.ShapeDtypeStruct((S, D), BF16)] * nb,
        compiler_params=_params("parallel"),
    )(x, scales, shifts)


def _mm_post(a, w, x, gate, name, *, scales=None, shifts=None, target=None):
    M, K = a.shape
    D = w.shape[2]
    tm = _pick(M, 512, 16)
    sub = _pick(tm, 256, 16)
    nb = 0 if scales is None else scales.shape[0]

    def body(a_ref, w_ref, x_ref, g_ref, *rest):
        if target is None:
            sc_ref, sh_ref, y_ref, xn_ref = rest[:4]
            h_refs = rest[4:]
        else:
            t_ref, y_ref, dx_ref, sq_ref = rest

            @pl.when(pl.program_id(0) == 0)
            def _():
                sq_ref[...] = jnp.zeros_like(sq_ref)

        for r in range(tm // sub):
            rows = pl.ds(r * sub, sub)
            yb = jnp.dot(a_ref[rows, :], w_ref[...], preferred_element_type=F32).astype(BF16)
            y_ref[rows, :] = yb
            yv = yb.astype(F32)
            yh = yv * lax.rsqrt(jnp.mean(yv * yv, axis=-1, keepdims=True) + EPS)
            xn = x_ref[rows, :] + yh * g_ref[...]
            if target is None:
                xn_ref[rows, :] = xn
                xh = xn * lax.rsqrt(jnp.mean(xn * xn, axis=-1, keepdims=True) + EPS)
                for n in range(nb):
                    h_refs[n][rows, :] = (xh * sc_ref[n:n + 1, :] + sh_ref[n:n + 1, :]).astype(BF16)
            else:
                e = xn - t_ref[rows, :]
                dx_ref[rows, :] = e / D
                sq_ref[...] += _colsum8(e * e)

    ins = [a, w, x, gate]
    in_specs = [_row_spec(tm, K), pl.BlockSpec((None, K, D), lambda i: (0, 0, 0)), _row_spec(tm, D), _vec_spec(1, D)]
    if target is None:
        ins += [scales, shifts]
        in_specs += [_vec_spec(nb, D), _vec_spec(nb, D)]
        out_specs = [_row_spec(tm, D)] * (2 + nb)
        out_shape = [jax.ShapeDtypeStruct((M, D), BF16), jax.ShapeDtypeStruct((M, D), F32)] \
            + [jax.ShapeDtypeStruct((M, D), BF16)] * nb
    else:
        ins += [target]
        in_specs += [_row_spec(tm, D)]
        out_specs = [_row_spec(tm, D), _row_spec(tm, D), _vec_spec(8, D)]
        out_shape = [jax.ShapeDtypeStruct((M, D), BF16), jax.ShapeDtypeStruct((M, D), F32),
                     jax.ShapeDtypeStruct((8, D), F32)]
    return pl.pallas_call(
        body, name=name, grid=(M // tm,), in_specs=in_specs, out_specs=out_specs, out_shape=out_shape,
        compiler_params=_params("arbitrary" if target is not None else "parallel"),
    )(*ins)


def _post_norm_bwd(dxn, y, gate, name):
    S, D = y.shape
    tm = _pick(S, 512, 16)

    def body(d_ref, y_ref, g_ref, dy_ref, dg_ref):
        yv = y_ref[...].astype(F32)
        dv = d_ref[...]
        r = lax.rsqrt(jnp.mean(yv * yv, axis=-1, keepdims=True) + EPS)
        yh = yv * r
        dyh = dv * g_ref[...]
        dy_ref[...] = (r * (dyh - yh * jnp.mean(dyh * yh, axis=-1, keepdims=True))).astype(BF16)

        @pl.when(pl.program_id(0) == 0)
        def _():
            dg_ref[...] = jnp.zeros_like(dg_ref)

        dg_ref[...] += _colsum8(dv * yh)

    return pl.pallas_call(
        body, name=name, grid=(S // tm,),
        in_specs=[_row_spec(tm, D), _row_spec(tm, D), _vec_spec(1, D)],
        out_specs=[_row_spec(tm, D), _vec_spec(8, D)],
        out_shape=[jax.ShapeDtypeStruct((S, D), BF16), jax.ShapeDtypeStruct((8, D), F32)],
        compiler_params=_params("arbitrary"),
    )(dxn, y, gate)


def _pre_norm_bwd(x, dxn, dhs, scales, name):
    S, D = x.shape
    nb = len(dhs)
    tm = _pick(S, 512, 8)

    def body(x_ref, d_ref, a_ref, *rest):
        dh_refs, dx_ref, ds_ref, db_ref = rest[:nb], rest[nb], rest[nb + 1], rest[nb + 2]
        xv = x_ref[...]
        r = lax.rsqrt(jnp.mean(xv * xv, axis=-1, keepdims=True) + EPS)
        xh = xv * r

        @pl.when(pl.program_id(0) == 0)
        def _():
            ds_ref[...] = jnp.zeros_like(ds_ref)
            db_ref[...] = jnp.zeros_like(db_ref)

        dxh = jnp.zeros_like(xv)
        for n in range(nb):
            dh = dh_refs[n][...].astype(F32)
            dxh = dxh + dh * a_ref[n:n + 1, :]
            ds_ref[n] += _colsum8(dh * xh)
            db_ref[n] += _colsum8(dh)
        dx_ref[...] = d_ref[...] + r * (dxh - xh * jnp.mean(dxh * xh, axis=-1, keepdims=True))

    acc_spec = pl.BlockSpec((nb, 8, D), lambda i: (0, 0, 0))
    return pl.pallas_call(
        body, name=name, grid=(S // tm,),
        in_specs=[_row_spec(tm, D), _row_spec(tm, D), _vec_spec(nb, D)] + [_row_spec(tm, D)] * nb,
        out_specs=[_row_spec(tm, D), acc_spec, acc_spec],
        out_shape=[jax.ShapeDtypeStruct((S, D), F32), jax.ShapeDtypeStruct((nb, 8, D), F32),
                   jax.ShapeDtypeStruct((nb, 8, D), F32)],
        compiler_params=_params("arbitrary"),
    )(x, dxn, scales, *dhs)


FFN_PAIRS = 2
FFN_SUB_ROWS = 256


def _ffn_in_act(h, w, layer, name):
    S, D = h.shape
    F2 = w.shape[2]
    PW = F2 // (2 * FFN_PAIRS)
    tm = _pick(S, 512, 16)
    sub = _pick(tm, FFN_SUB_ROWS, 16)

    def body(h_ref, w_ref, gu_ref, a_ref):
        for r in range(tm // sub):
            rows = pl.ds(r * sub, sub)
            acc = jnp.dot(h_ref[rows, :], w_ref[...], preferred_element_type=F32)
            gu_ref[rows, :] = acc.astype(BF16)
            g = acc[:, :PW]
            a_ref[rows, :] = (g * jax.nn.sigmoid(g) * acc[:, PW:]).astype(BF16)

    return pl.pallas_call(
        body, name=name, grid=(FFN_PAIRS, S // tm),
        in_specs=[pl.BlockSpec((tm, D), lambda p, i: (i, 0)),
                  pl.BlockSpec((None, D, 2 * PW), lambda p, i: (layer, 0, p))],
        out_specs=[pl.BlockSpec((tm, 2 * PW), lambda p, i: (i, p)), pl.BlockSpec((tm, PW), lambda p, i: (i, p))],
        out_shape=[jax.ShapeDtypeStruct((S, F2), BF16), jax.ShapeDtypeStruct((S, F2 // 2), BF16)],
        compiler_params=_params("parallel", "parallel"),
    )(h, w)


def _ffn_out_dx_act(dy, w, layer, gu, name):
    S, D = dy.shape
    F2 = gu.shape[1]
    PW = F2 // (2 * FFN_PAIRS)
    tm = _pick(S, 512, 16)
    sub = _pick(tm, FFN_SUB_ROWS, 16)

    def body(dy_ref, w_ref, gu_ref, o_ref):
        for r in range(tm // sub):
            rows = pl.ds(r * sub, sub)
            da = lax.dot_general(dy_ref[rows, :], w_ref[...], _DIMS["nt"], preferred_element_type=F32)
            g = gu_ref[rows, 0:PW].astype(F32)
            u = gu_ref[rows, PW:2 * PW].astype(F32)
            sg = jax.nn.sigmoid(g)
            o_ref[rows, 0:PW] = (da * u * (sg * (1.0 + g * (1.0 - sg)))).astype(BF16)
            o_ref[rows, PW:2 * PW] = (da * (g * sg)).astype(BF16)

    return pl.pallas_call(
        body, name=name, grid=(FFN_PAIRS, S // tm),
        in_specs=[pl.BlockSpec((tm, D), lambda p, i: (i, 0)),
                  pl.BlockSpec((None, PW, D), lambda p, i: (layer, p, 0)),
                  pl.BlockSpec((tm, 2 * PW), lambda p, i: (i, p))],
        out_specs=pl.BlockSpec((tm, 2 * PW), lambda p, i: (i, p)),
        out_shape=jax.ShapeDtypeStruct((S, F2), BF16),
        compiler_params=_params("parallel", "parallel"),
    )(dy, w, gu)


HALO = 16


def _conv_terms(bcx_ref, prev_ref, i, tm, D):
    b = bcx_ref[:, 0:D].astype(F32)
    cg = bcx_ref[:, D:2 * D].astype(F32)
    xin = bcx_ref[:, 2 * D:3 * D].astype(F32)
    z = cg * xin
    zp = prev_ref[:, D:2 * D].astype(F32) * prev_ref[:, 2 * D:3 * D].astype(F32)
    zp = jnp.where(i > 0, zp, 0.0)
    z_ext = jnp.concatenate([zp, z], axis=0)
    z1 = pltpu.roll(z_ext, 1, 0)[HALO:, :]
    z2 = pltpu.roll(z_ext, 2, 0)[HALO:, :]
    return b, cg, xin, z, z1, z2


def _conv_gate(bcx, ck, name):
    S, D3 = bcx.shape
    D = D3 // 3
    tm = _pick(S, 256, 16)
    hb = tm // HALO

    def body(bcx_ref, prev_ref, ck_ref, o_ref):
        i = pl.program_id(0)
        b, _, _, z, z1, z2 = _conv_terms(bcx_ref, prev_ref, i, tm, D)
        conv = ck_ref[0:1, :] * z2 + ck_ref[1:2, :] * z1 + ck_ref[2:3, :] * z
        o_ref[...] = (b * conv).astype(BF16)

    return pl.pallas_call(
        body, name=name, grid=(S // tm,),
        in_specs=[_row_spec(tm, D3),
                  pl.BlockSpec((HALO, D3), lambda i: (jnp.maximum(i * hb - 1, 0), 0)),
                  _vec_spec(8, D)],
        out_specs=_row_spec(tm, D),
        out_shape=jax.ShapeDtypeStruct((S, D), BF16),
        compiler_params=_params("parallel"),
    )(bcx, bcx, ck)


def _conv_gate_bwd(du, bcx, ck, name):
    S, D3 = bcx.shape
    D = D3 // 3
    tm = _pick(S, 256, 16)
    hb = tm // HALO
    nt = S // tm

    def body(du_ref, dun_ref, bcx_ref, prev_ref, next_ref, ck_ref, o_ref, dk_ref):
        i = pl.program_id(0)
        b, cg, xin, z, z1, z2 = _conv_terms(bcx_ref, prev_ref, i, tm, D)
        k0, k1, k2 = ck_ref[0:1, :], ck_ref[1:2, :], ck_ref[2:3, :]
        conv = k0 * z2 + k1 * z1 + k2 * z
        d = du_ref[...].astype(F32)
        dconv = d * b
        dcn = jnp.where(i < nt - 1, dun_ref[...].astype(F32) * next_ref[:, 0:D].astype(F32), 0.0)
        d_ext = jnp.concatenate([dconv, dcn], axis=0)
        d1 = pltpu.roll(d_ext, tm + HALO - 1, 0)[:tm, :]
        d2 = pltpu.roll(d_ext, tm + HALO - 2, 0)[:tm, :]
        dz = k2 * dconv + k1 * d1 + k0 * d2
        o_ref[:, 0:D] = (d * conv).astype(BF16)
        o_ref[:, D:2 * D] = (dz * xin).astype(BF16)
        o_ref[:, 2 * D:3 * D] = (dz * cg).astype(BF16)

        @pl.when(i == 0)
        def _():
            dk_ref[...] = jnp.zeros_like(dk_ref)

        dk_ref[0] += _colsum8(dconv * z2)
        dk_ref[1] += _colsum8(dconv * z1)
        dk_ref[2] += _colsum8(dconv * z)

    last = S // HALO - 1
    return pl.pallas_call(
        body, name=name, grid=(nt,),
        in_specs=[_row_spec(tm, D),
                  pl.BlockSpec((HALO, D), lambda i: (jnp.minimum((i + 1) * hb, last), 0)),
                  _row_spec(tm, D3),
                  pl.BlockSpec((HALO, D3), lambda i: (jnp.maximum(i * hb - 1, 0), 0)),
                  pl.BlockSpec((HALO, D3), lambda i: (jnp.minimum((i + 1) * hb, last), 0)),
                  _vec_spec(8, D)],
        out_specs=[_row_spec(tm, D3), pl.BlockSpec((3, 8, D), lambda i: (0, 0, 0))],
        out_shape=[jax.ShapeDtypeStruct((S, D3), BF16), jax.ShapeDtypeStruct((3, 8, D), F32)],
        compiler_params=_params("arbitrary"),
    )(du, du, bcx, bcx, bcx, ck)


def _rel_onehot():
    a = np.arange(CHUNK)[:, None]
    b = np.arange(CHUNK)[None, :]
    idx = np.stack([np.clip((N_LEFT_CHUNKS - dl) * CHUNK + a - b, -MAX_REL, MAX_REL) + MAX_REL
                    for dl in (6, 7, 8)]).reshape(-1)
    return (jnp.asarray(idx)[:, None] == jnp.arange(N_REL)[None, :]).astype(F32)


def _bias_table(rel_bias, name):
    H = rel_bias.shape[0]
    near = jnp.dot(rel_bias, _rel_onehot().T, precision=lax.Precision.HIGHEST).reshape(H, 3, CHUNK, CHUNK)
    far = jnp.broadcast_to(rel_bias[:, N_REL - 1][:, None, None], (H, CHUNK, CHUNK))

    def body(near_ref, far_ref, o_ref):
        neg = jnp.full((CHUNK, CHUNK), NEG, F32)
        for v in range(N_WIN):
            for ic in range(Q_CHUNKS):
                for jc in range(N_WIN * Q_CHUNKS):
                    dl = jc - ic
                    if dl < 0 or dl > N_LEFT_CHUNKS or jc < (N_WIN - 1 - v) * Q_CHUNKS:
                        blk = neg
                    else:
                        blk = far_ref[...] if dl <= 5 else near_ref[dl - 6]
                    o_ref[v, ic * CHUNK:(ic + 1) * CHUNK, jc * CHUNK:(jc + 1) * CHUNK] = blk

    return pl.pallas_call(
        body, name=name, grid=(H,),
        in_specs=[pl.BlockSpec((None, 3, CHUNK, CHUNK), lambda h: (h, 0, 0, 0)),
                  pl.BlockSpec((None, CHUNK, CHUNK), lambda h: (h, 0, 0))],
        out_specs=pl.BlockSpec((N_WIN, None, BQ, N_WIN * BQ), lambda h: (0, h, 0, 0)),
        out_shape=jax.ShapeDtypeStruct((N_WIN, H, BQ, N_WIN * BQ), F32),
        compiler_params=_params("parallel"),
    )(near, far)


def _bias_table_grad(dtab):
    H = dtab.shape[0]
    blk = lambda ic, jc: dtab[:, ic * CHUNK:(ic + 1) * CHUNK, jc * CHUNK:(jc + 1) * CHUNK]
    by_dl = [sum(blk(ic, ic + dl) for ic in range(Q_CHUNKS)) for dl in range(N_LEFT_CHUNKS + 1)]
    far = sum(jnp.sum(by_dl[dl], axis=(1, 2)) for dl in range(6))
    near = jnp.stack(by_dl[6:9], axis=1).reshape(H, 3 * CHUNK * CHUNK)
    g = jnp.dot(near, _rel_onehot(), precision=lax.Precision.HIGHEST)
    return g.at[:, N_REL - 1].add(far)


def _attn_specs(nblk, W):
    last = nblk - 1
    q_spec = pl.BlockSpec((BQ, W), lambda g, i: (jnp.minimum(i, last), g))
    kv_specs = [pl.BlockSpec((BQ, 2 * W), functools.partial(
        lambda g, i, w: (jnp.maximum(jnp.minimum(i, last) - (N_WIN - 1) + w, 0), g), w=w)) for w in range(N_WIN)]
    tab_spec = pl.BlockSpec((None, HEADS_PER_STEP, BQ, N_WIN * BQ),
                            lambda g, i: (jnp.minimum(i, N_WIN - 1), g, 0, 0))
    dtab_spec = pl.BlockSpec((HEADS_PER_STEP, BQ, N_WIN * BQ), lambda g, i: (g, 0, 0))
    return q_spec, kv_specs, tab_spec, dtab_spec


def _attn_exp(q_ref, kT, tab_ref, h, dh):
    s = jnp.dot(q_ref[:, h * dh:(h + 1) * dh], kT[h * dh:(h + 1) * dh, :], preferred_element_type=F32) + tab_ref[h]
    e = jnp.exp(s - jnp.max(s, axis=-1, keepdims=True))
    return e, jnp.sum(e, axis=-1, keepdims=True)


def _attn_fwd(q, kv, tab, name):
    S, D = q.shape
    dh = D // N_HEADS
    W = HEADS_PER_STEP * dh
    assert 2 * W == D, "the kv layout puts one head group's k beside its v: two head groups"
    q_spec, kv_specs, tab_spec, _ = _attn_specs(S // BQ, W)

    def body(q_ref, *rest):
        tab_ref, o_ref = rest[N_WIN], rest[N_WIN + 1]
        kvw = jnp.concatenate([r[...] for r in rest[:N_WIN]], axis=0)
        kT = kvw[:, :W].T
        vw = kvw[:, W:]
        outs = []
        for h in range(HEADS_PER_STEP):
            e, l = _attn_exp(q_ref, kT, tab_ref, h, dh)
            outs.append(jnp.dot(e.astype(BF16), vw[:, h * dh:(h + 1) * dh], preferred_element_type=F32) / l)
        o_ref[...] = jnp.concatenate(outs, axis=1).astype(BF16)

    return pl.pallas_call(
        body, name=name, grid=(N_HEADS // HEADS_PER_STEP, S // BQ),
        in_specs=[q_spec] + kv_specs + [tab_spec],
        out_specs=q_spec,
        out_shape=jax.ShapeDtypeStruct((S, D), BF16),
        compiler_params=_params("parallel", "parallel"),
    )(q, *([kv] * N_WIN), tab)


def _attn_bwd(q, kv, tab, do, name):
    S, D = q.shape
    dh = D // N_HEADS
    W = HEADS_PER_STEP * dh
    nblk = S // BQ
    q_spec, kv_specs, tab_spec, dtab_spec = _attn_specs(nblk, W)

    def body(q_ref, *rest):
        tab_ref, do_ref, dq_ref, dkv_ref, dtab_ref, ring = rest[N_WIN:]
        i = pl.program_id(1)

        @pl.when(i == 0)
        def _():
            dtab_ref[...] = jnp.zeros_like(dtab_ref)
            ring[...] = jnp.zeros_like(ring)

        @pl.when(i < nblk)
        def _():
            kvw = jnp.concatenate([r[...] for r in rest[:N_WIN]], axis=0)
            kT = kvw[:, :W].T
            vT = kvw[:, W:].T
            qT = q_ref[...].T
            doT = do_ref[...].T
            dqs, dks, dvs = [], [], []
            for h in range(HEADS_PER_STEP):
                hd = slice(h * dh, (h + 1) * dh)
                e, l = _attn_exp(q_ref, kT, tab_ref, h, dh)
                p = e * (1.0 / l)
                dp = jnp.dot(do_ref[:, hd], vT[hd, :], preferred_element_type=F32)
                ds = p * (dp - jnp.sum(p * dp, axis=-1, keepdims=True))
                dtab_ref[h] += ds
                dsb = ds.astype(BF16)
                dqs.append(lax.dot_general(kT[hd, :], dsb, _DIMS["nt"], preferred_element_type=F32) * (dh ** -0.5))
                dks.append(jnp.dot(qT[hd, :], dsb, preferred_element_type=F32))
                dvs.append(jnp.dot(doT[hd, :], p.astype(BF16), preferred_element_type=F32))
            dq_ref[...] = jnp.concatenate(dqs, axis=0).T.astype(BF16)
            dkv = jnp.concatenate(dks + dvs, axis=0).T
            for w in range(N_WIN):
                slot = lax.rem(i + 1 + w, N_WIN)
                part = dkv[w * BQ:(w + 1) * BQ, :]
                if w == N_WIN - 1:
                    ring[slot] = part
                else:
                    ring[slot] += part

        dkv_ref[...] = ring[lax.rem(i + 1, N_WIN)].astype(BF16)

    done_spec = pl.BlockSpec((BQ, 2 * W), lambda g, i: (jnp.maximum(i - (N_WIN - 1), 0), g))
    return pl.pallas_call(
        body, name=name, grid=(N_HEADS // HEADS_PER_STEP, nblk + N_WIN - 1),
        in_specs=[q_spec] + kv_specs + [tab_spec, q_spec],
        out_specs=[q_spec, done_spec, dtab_spec],
        out_shape=[jax.ShapeDtypeStruct((S, D), BF16), jax.ShapeDtypeStruct((S, 2 * D), BF16),
                   jax.ShapeDtypeStruct(tab.shape[1:], F32)],
        scratch_shapes=[pltpu.VMEM((N_WIN, BQ, 2 * W), F32)],
        compiler_params=_params("parallel", "arbitrary"),
    )(q, *([kv] * N_WIN), tab, do)


def _adamw(w, g, m, v, name):
    shape = w.shape
    C = shape[-1]
    R = int(np.prod(shape[:-1])) if len(shape) > 1 else 1
    w2, g2, m2, v2 = (t.reshape(R, C) for t in (w, g, m, v))
    tr = _pick(R, max(8, (512 * 1024) // C // 8 * 8), 8)

    def body(w_ref, g_ref, m_ref, v_ref, d_ref, nm_ref, nv_ref):
        gv = g_ref[...]
        nm = ADAM_B1 * m_ref[...] + (1.0 - ADAM_B1) * gv
        nv = ADAM_B2 * v_ref[...] + (1.0 - ADAM_B2) * jnp.square(gv)
        m_hat = nm / (1.0 - ADAM_B1 ** ADAM_STEP)
        v_hat = nv / (1.0 - ADAM_B2 ** ADAM_STEP)
        d_ref[...] = -ADAM_LR * (m_hat / (jnp.sqrt(v_hat) + ADAM_EPS) + ADAM_WD * w_ref[...])
        nm_ref[...] = nm
        nv_ref[...] = nv

    spec = pl.BlockSpec((tr, C), lambda i: (i, 0))
    outs = pl.pallas_call(
        body, name=name, grid=(R // tr,),
        in_specs=[spec] * 4, out_specs=[spec] * 3,
        out_shape=[jax.ShapeDtypeStruct((R, C), F32)] * 3,
        compiler_params=_params("parallel"),
    )(w2, g2, m2, v2)
    return tuple(o.reshape(shape) for o in outs)


def _sum_rows(a, name):
    n, L = a.shape

    def body(a_ref, o_ref):
        acc = a_ref[0:1, :]
        for r in range(1, n):
            acc = acc + a_ref[r:r + 1, :]
        o_ref[...] = acc

    return pl.pallas_call(
        body, name=name, grid=(1,),
        in_specs=[pl.BlockSpec((n, L), lambda i: (0, 0))],
        out_specs=pl.BlockSpec((1, L), lambda i: (0, 0)),
        out_shape=jax.ShapeDtypeStruct((1, L), F32),
        compiler_params=_params("arbitrary"),
    )(a)


def _scalar_call(body, name, scalar, grid, in_specs, out_spec, out_shape, args):
    return pl.pallas_call(
        body, name=name,
        grid_spec=pltpu.PrefetchScalarGridSpec(num_scalar_prefetch=1, grid=grid, in_specs=in_specs,
                                               out_specs=out_spec),
        out_shape=out_shape, compiler_params=_params("parallel"),
    )(jnp.reshape(scalar, (-1,)).astype(jnp.int32), *args)


def _pair_sum(view, got, c, name):
    nb, _, rh, cols = view.shape
    tr = _pick(rh, max(16, (1 << 20) // cols // 16 * 16), 16)
    bpr = rh // tr

    def body(s_ref, a_ref, b_ref, o_ref):
        o_ref[...] = (a_ref[...].astype(F32) + b_ref[...].astype(F32)).astype(BF16)

    spec = pl.BlockSpec((tr, cols), lambda i, s: (i, 0))
    mine = pl.BlockSpec((tr, cols), lambda i, s: ((2 * (i // bpr) + s[0]) * bpr + i % bpr, 0))
    return _scalar_call(body, name, c, (nb * bpr,), [mine, spec], spec,
                        jax.ShapeDtypeStruct((nb * rh, cols), BF16),
                        (view.reshape(nb * 2 * rh, cols), got.reshape(nb * rh, cols)))


def _owner_sum(pair, recv, me, c, it, name):
    _, rh, bc = recv.shape
    tr = _pick(rh, max(16, (1 << 19) // bc // 16 * 16), 16)
    bpr = rh // tr

    def body(s_ref, a_ref, r0, r1, r2, o_ref):
        o_ref[...] = ((a_ref[...].astype(F32) + r0[...].astype(F32)) + r1[...].astype(F32)) + r2[...].astype(F32)

    if it.kind == "col":
        own = pl.BlockSpec((tr, bc), lambda i, s: (i, s[0]))
    else:
        own = pl.BlockSpec((tr, bc), lambda i, s: (s[0] * bpr + i, 0))
    slots = [pl.BlockSpec((None, tr, bc), functools.partial(lambda i, s, k: (k, i, 0), k=k)) for k in range(3)]
    return _scalar_call(body, name, jnp.stack([it.pos(me), c]), (bpr,), [own] + slots,
                        pl.BlockSpec((tr, bc), lambda i, s: (s[1] * bpr + i, 0)),
                        jax.ShapeDtypeStruct((2 * rh, bc), F32), (pair, recv, recv, recv))


def _place():
    x, y, c = lax.axis_index("x"), lax.axis_index("y"), lax.axis_index("c")
    chips = [(1 - x, y), (x, 1 - y), (1 - x, 1 - y)]
    return x, y, c, chips


def _chip_index(px, py):
    return 2 * px + py


def _all_gather_small(x_shard, name):
    m_per, n = x_shard.shape

    def body(x_ref, out_ref, send_sems, recv_sems, local_sem):
        x, y, c, chips = _place()
        me, sibling = (x, y, c), (x, y, 1 - c)

        def rows(px, py, pc):
            return out_ref.at[pl.ds((4 * px + 2 * py + pc) * m_per, m_per), :]

        def copy(k, block, to, src=None):
            return pltpu.make_async_remote_copy(
                src_ref=rows(*block) if src is None else src, dst_ref=rows(*block),
                send_sem=send_sems.at[k], recv_sem=recv_sems.at[k], device_id=to, device_id_type=MESH)

        mine = pltpu.make_async_copy(x_ref, rows(*me), local_sem)
        mine.start()
        first = [copy(0, me, sibling, src=x_ref)]
        first += [copy(1 + j, me, (*chip, c), src=x_ref) for j, chip in enumerate(chips)]
        for cp in first:
            cp.start()
        passed = [copy(4 + j, (*chip, c), sibling) for j, chip in enumerate(chips)]
        for j, chip in enumerate(chips):
            copy(1 + j, (*chip, c), me).wait_recv()
            passed[j].start()
        copy(0, sibling, me).wait_recv()
        for j, chip in enumerate(chips):
            copy(4 + j, (*chip, 1 - c), me).wait_recv()
        for cp in first + passed:
            cp.wait_send()
        mine.wait()

    return pl.pallas_call(
        body, name=name,
        out_shape=jax.ShapeDtypeStruct((N_DEV * m_per, n), x_shard.dtype),
        in_specs=[pl.BlockSpec(memory_space=pltpu.VMEM)],
        out_specs=pl.BlockSpec(memory_space=pltpu.VMEM),
        scratch_shapes=[pltpu.SemaphoreType.DMA((7,)), pltpu.SemaphoreType.DMA((7,)), pltpu.SemaphoreType.DMA],
    )(x_shard)


def _gather_flat(vec, name):
    L = vec.shape[0]
    Lp = -(-L // 1024) * 1024
    g = _all_gather_small(jnp.pad(vec, (0, Lp - L)).reshape(8, Lp // 8), name)
    return g.reshape(N_DEV, Lp)[:, :L]


class _Item:
    def __init__(self, kind, rows, cols, arg, layer, swap=False):
        self.kind, self.rows, self.cols, self.arg, self.layer, self.swap = kind, rows, cols, arg, layer, swap

    def ref(self, refs):
        return refs[self.arg].at[self.layer]

    def pos(self, j):
        return 2 * (j % 2) + j // 2 if self.swap else j


def _block(ref, it, j, half):
    if it.kind == "col":
        ns = it.cols // N_CHIP
        return ref.at[pl.ds(half * (it.rows // 2), it.rows // 2), pl.ds(it.pos(j) * ns, ns)]
    rs = it.rows // N_CHIP
    return ref.at[pl.ds(j * rs + half * (rs // 2), rs // 2), :]


def _cast_place(w, layer, kind, pos, name):
    _, r, n = w.shape
    tr = _pick(r, max(16, (1 << 20) // n // 16 * 16), 16)
    bpr = r // tr

    def body(s_ref, w_ref, o_ref):
        o_ref[...] = w_ref[...].astype(BF16)

    if kind == "col":
        full, out_idx = (1, r, N_CHIP * n), (lambda i, s: (0, i, s[0]))
    else:
        full, out_idx = (1, N_CHIP * r, n), (lambda i, s: (0, s[0] * bpr + i, 0))
    return pl.pallas_call(
        body, name=name,
        grid_spec=pltpu.PrefetchScalarGridSpec(
            num_scalar_prefetch=1, grid=(bpr,),
            in_specs=[pl.BlockSpec((None, tr, n), lambda i, s: (layer, i, 0))],
            out_specs=pl.BlockSpec((None, tr, n), out_idx)),
        out_shape=jax.ShapeDtypeStruct(full, BF16),
        compiler_params=_params("parallel"),
    )(jnp.reshape(pos, (1,)).astype(jnp.int32), w)


HBM_SPEC = pl.BlockSpec(memory_space=pltpu.HBM)
SEM_SPEC = pl.BlockSpec(memory_space=pltpu.SEMAPHORE)
ANY_SPEC = pl.BlockSpec(memory_space=pl.ANY)
SPLIT_PARAMS = dict(has_side_effects=pltpu.SideEffectType.DATAFLOW_SIDE_EFFECTING)


def _in_hbm(a):
    return pltpu.with_memory_space_constraint(a, pltpu.HBM)


def _split_start(copies_of, bufs, n_sem, after, name):
    n = len(bufs)

    def body(*refs):
        ins, send, recv, token = refs[:n], refs[n + 1], refs[n + 2], refs[2 * n + 3]
        for cp in copies_of(ins, send, recv, False)[0]:
            cp.start()
        token[...] = jnp.zeros_like(token)

    outs = pl.pallas_call(
        body, name=name,
        out_shape=(pltpu.SemaphoreType.DMA(n_sem), pltpu.SemaphoreType.DMA(n_sem),
                   *[pltpu.HBM(b.shape, b.dtype) for b in bufs], jax.ShapeDtypeStruct((8, 128), F32)),
        in_specs=[HBM_SPEC] * n + [ANY_SPEC],
        out_specs=(SEM_SPEC, SEM_SPEC, *[HBM_SPEC] * n, pl.BlockSpec(memory_space=pltpu.VMEM)),
        input_output_aliases={t: 2 + t for t in range(n)},
        compiler_params=pltpu.CompilerParams(**SPLIT_PARAMS),
    )(*[_in_hbm(b) for b in bufs], after)
    return outs[0], outs[1], list(outs[2:2 + n]), outs[2 + n]


def _split_wait(copies_of, send, recv, bufs, after, name):
    n = len(bufs)

    def body(*refs):
        ins, send_ref, recv_ref = refs[:n], refs[n], refs[n + 1]
        sends, arrivals = copies_of(ins, send_ref, recv_ref, True)
        for cp in sends:
            cp.wait_send()
        for cp in arrivals:
            cp.wait_recv()

    return pl.pallas_call(
        body, name=name,
        out_shape=[pltpu.HBM(b.shape, b.dtype) for b in bufs],
        in_specs=[HBM_SPEC] * n + [SEM_SPEC, SEM_SPEC, ANY_SPEC],
        out_specs=[HBM_SPEC] * n,
        input_output_aliases={t: t for t in range(n)},
        compiler_params=pltpu.CompilerParams(**SPLIT_PARAMS),
    )(*bufs, send, recv, after)


def _gather_copies(items):
    def copies_of(refs, send, recv, with_arrivals):
        x, y, c, chips = _place()
        me = _chip_index(x, y)
        sends, arrivals = [], []
        for t, it in enumerate(items):
            for k, chip in enumerate(chips):
                for core in range(2):
                    mine = _block(it.ref(refs), it, me, c)
                    sends.append(pltpu.make_async_remote_copy(
                        src_ref=mine, dst_ref=mine, send_sem=send.at[6 * t + 2 * k + core],
                        recv_sem=recv.at[6 * t + 2 * k + c], device_id=(*chip, core), device_id_type=MESH))
                    if with_arrivals:
                        landed = _block(it.ref(refs), it, _chip_index(*chip), core)
                        arrivals.append(pltpu.make_async_remote_copy(
                            src_ref=landed, dst_ref=landed, send_sem=send.at[6 * t + 2 * k + core],
                            recv_sem=recv.at[6 * t + 2 * k + core], device_id=(*chip, core), device_id_type=MESH))
        return sends, arrivals

    return copies_of


def _owner_copies(items):
    n = len(items)

    def blk(ref, it, j):
        if it.kind == "col":
            ns = it.cols // N_CHIP
            return ref.at[:, pl.ds(it.pos(j) * ns, ns)]
        return ref.at[j]

    def copies_of(refs, send, recv, with_arrivals):
        x, y, c, chips = _place()
        sends, arrivals = [], []
        for t, it in enumerate(items):
            for k, chip in enumerate(chips):
                slot = refs[n + t].at[k]
                sends.append(pltpu.make_async_remote_copy(
                    src_ref=blk(refs[t], it, _chip_index(*chip)), dst_ref=slot, send_sem=send.at[3 * t + k],
                    recv_sem=recv.at[3 * t + k], device_id=(*chip, c), device_id_type=MESH))
                if with_arrivals:
                    arrivals.append(pltpu.make_async_remote_copy(
                        src_ref=slot, dst_ref=slot, send_sem=send.at[3 * t + k], recv_sem=recv.at[3 * t + k],
                        device_id=(*chip, c), device_id_type=MESH))
        return sends, arrivals

    return copies_of


def _owner_slot_shape(it):
    if it.kind == "col":
        return (3, it.rows // 2, it.cols // N_CHIP)
    return (3, it.rows // (2 * N_CHIP), it.cols)


def _pair_view(g, it):
    if it.kind == "col":
        return g.reshape(1, 2, it.rows // 2, it.cols)
    return g.reshape(N_CHIP, 2, it.rows // (2 * N_CHIP), it.cols)


def _pair_copies(n):
    def copies_of(refs, send, recv, with_arrivals):
        x, y, c, _ = _place()
        sends, arrivals = [], []
        for t in range(n):
            land = refs[n + t]
            sends.append(pltpu.make_async_remote_copy(
                src_ref=refs[t].at[:, pl.ds(1 - c, 1)], dst_ref=land, send_sem=send.at[t], recv_sem=recv.at[t],
                device_id=(x, y, 1 - c), device_id_type=MESH))
            if with_arrivals:
                arrivals.append(pltpu.make_async_remote_copy(
                    src_ref=land, dst_ref=land, send_sem=send.at[t], recv_sem=recv.at[t],
                    device_id=(x, y, 1 - c), device_id_type=MESH))
        return sends, arrivals

    return copies_of


def _half_copies(n):
    def copies_of(refs, send, recv, with_arrivals):
        x, y, c, _ = _place()
        sends, arrivals = [], []
        for t in range(n):
            r2 = refs[t].shape[0] // 2
            mine = refs[t].at[pl.ds(c * r2, r2), :]
            sends.append(pltpu.make_async_remote_copy(
                src_ref=mine, dst_ref=mine, send_sem=send.at[t], recv_sem=recv.at[t],
                device_id=(x, y, 1 - c), device_id_type=MESH))
            if with_arrivals:
                theirs = refs[t].at[pl.ds((1 - c) * r2, r2), :]
                arrivals.append(pltpu.make_async_remote_copy(
                    src_ref=theirs, dst_ref=theirs, send_sem=send.at[t], recv_sem=recv.at[t],
                    device_id=(x, y, 1 - c), device_id_type=MESH))
        return sends, arrivals

    return copies_of


class _Reduction:
    pass


def _pair_start(grads, items, after, tag):
    n = len(items)
    views = [_pair_view(g, it) for g, it in zip(grads, items)]
    lands = [lax.empty((v.shape[0], 1) + v.shape[2:], v.dtype) for v in views]
    r = _Reduction()
    r.items, r.tag = items, tag
    r.send, r.recv, r.bufs, r.token = _split_start(_pair_copies(n), views + lands, (n,), after, f"rs_pair_start_{tag}")
    return r


def _owner_start(r, after):
    x, y, c, _ = _place()
    n = len(r.items)
    bufs = _split_wait(_pair_copies(n), r.send, r.recv, r.bufs, after, f"rs_pair_wait_{r.tag}")
    pairs = [_pair_sum(bufs[t], bufs[n + t], c, f"rs_pair_sum_{r.tag}_{t}") for t in range(n)]
    shaped = [p if it.kind == "col" else p.reshape(N_CHIP, p.shape[0] // N_CHIP, p.shape[1])
              for p, it in zip(pairs, r.items)]
    lands = [lax.empty(_owner_slot_shape(it), BF16) for it in r.items]
    r.send, r.recv, r.bufs, r.token = _split_start(
        _owner_copies(r.items), shaped + lands, (3 * n,), r.token, f"rs_owner_start_{r.tag}")
    return r


def _reduce_finish(groups, after):
    x, y, c, _ = _place()
    me = _chip_index(x, y)
    halves = []
    for r in groups:
        n = len(r.items)
        bufs = _split_wait(_owner_copies(r.items), r.send, r.recv, r.bufs, after, f"rs_owner_wait_{r.tag}")
        for t, it in enumerate(r.items):
            pair = bufs[t].reshape(-1, bufs[t].shape[-1])
            halves.append(_owner_sum(pair, bufs[n + t], me, c, it, f"rs_owner_sum_{r.tag}_{t}"))
    n = len(halves)
    return _split_start(_half_copies(n), halves, (n,), after, "rs_half_start")


def _silu(v):
    return v * jax.nn.sigmoid(v)


def _sum8(p):
    return jnp.sum(p, axis=-2)


def kernel(x, c, mod_w, mod_b, norm_g, ffn_w_in, ffn_w_out, conv_w_in, conv_k, conv_w_out, kv_mod_w, kv_mod_b, kv_norm_g, w_kv, attn_w_q, attn_w_o, rel_bias, loss_target, m_mod_w, m_mod_b, m_norm_g, m_ffn_w_in, m_ffn_w_out, m_conv_w_in, m_conv_k, m_conv_w_out, m_kv_mod_w, m_kv_mod_b, m_kv_norm_g, m_w_kv, m_attn_w_q, m_attn_w_o, m_rel_bias, v_mod_w, v_mod_b, v_norm_g, v_ffn_w_in, v_ffn_w_out, v_conv_w_in, v_conv_k, v_conv_w_out, v_kv_mod_w, v_kv_mod_b, v_kv_norm_g, v_w_kv, v_attn_w_q, v_attn_w_o, v_rel_bias):
    xi, yi, ci = lax.axis_index("x"), lax.axis_index("y"), lax.axis_index("c")
    chip = 2 * xi + yi
    dev = 2 * chip + ci
    _, S, D = x.shape
    F = ffn_w_out.shape[1] * N_CHIP
    x0 = x.reshape(S, D)
    target = loss_target.reshape(S, D)
    n_mod = mod_w.shape[2]
    n_kvm = kv_mod_w.shape[1]
    dsh = D // N_CHIP
    TF = F // 2

    c_all = _all_gather_small(c.reshape(8, D // 8), "ag_c").reshape(N_DEV, D)
    sc16 = jnp.pad(_silu(c_all), ((0, 8), (0, 0)))
    part = [_mm(sc16, mod_w, "nn", F32, f"mod_fwd_{l}", b_layer=l)[:8] for l in range(2)]
    part.append(_mm(sc16, kv_mod_w, "nn", F32, "mod_fwd_kv")[:8])
    fwd_vec = jnp.concatenate([p.reshape(-1) for p in part] + [norm_g.reshape(-1), conv_k.reshape(-1)])
    fwd_all = _gather_flat(fwd_vec, "ag_fwd_small")[0::2]
    o = 0
    mods = []
    for n in (n_mod, n_mod, n_kvm):
        blk = fwd_all[:, o:o + 8 * n].reshape(N_CHIP, 8, n)
        mods.append(lax.dynamic_index_in_dim(blk, dev, axis=1, keepdims=False).reshape(N_CHIP * n))
        o += 8 * n
    ng = fwd_all[:, o:o + 8 * dsh].reshape(N_CHIP, 2, 4, dsh).transpose(1, 2, 0, 3).reshape(2, 4, D)
    o += 8 * dsh
    ck = fwd_all[:, o:o + 3 * dsh].reshape(N_CHIP, 3, dsh).transpose(1, 0, 2).reshape(3, D)
    ck8 = jnp.pad(ck, ((0, 5), (0, 0)))
    mod = [mods[l] + mod_b[l] for l in range(2)]
    sh1, sc1, g1, sh2, sc2, g2 = zip(*[jnp.split(m, 6) for m in mod])
    kv_sh, kv_sc = jnp.split(mods[2] + kv_mod_b, 2)
    row = lambda v: v.reshape(1, D)

    it_conv = [_Item("col", D, 3 * D, 0, 0), _Item("row", D, D, 1, 0)]
    it_ffn = [_Item("col", D, 2 * F, 0, 0, swap=True), _Item("row", F, D, 1, 0)]
    it_attn = [_Item("col", D, 2 * D, 0, 0, swap=True), _Item("row", D, D, 1, 0), _Item("row", D, D, 2, 0)]

    def placed(w, layer, it, nm, after=None):
        pos = it.pos(chip)
        if after is not None:
            pos = pos + after[0, 0].astype(jnp.int32)
        return _cast_place(w, layer, it.kind, pos, f"place_{nm}")

    flying = {}

    def start(tag, its, bufs, after):
        send, recv, bufs, tok = _split_start(_gather_copies(its), bufs, (6 * len(its),), after, f"ag_start_{tag}")
        flying[tag] = (its, send, recv, bufs)
        return tok

    def arrived(tag, after):
        its, send, recv, bufs = flying[tag]
        return _split_wait(_gather_copies(its), send, recv, bufs, after, f"ag_wait_{tag}")

    one = lambda it: [_Item(it.kind, it.rows, it.cols, 0, 0, it.swap)]
    tok = start("conv_in", one(it_conv[0]), [placed(conv_w_in, 0, it_conv[0], "conv_w_in")], fwd_all)
    tok = start("conv_out", one(it_conv[1]), [placed(conv_w_out, 0, it_conv[1], "conv_w_out", tok)], tok)
    tok = start("ffn0_in", one(it_ffn[0]), [placed(ffn_w_in, 0, it_ffn[0], "ffn_w_in0", tok)], tok)
    tok = start("ffn0_out", one(it_ffn[1]), [placed(ffn_w_out, 0, it_ffn[1], "ffn_w_out0", tok)], tok)
    tok = start("attn", it_attn, [placed(w_kv[None], 0, it_attn[0], "w_kv", tok),
                                  placed(attn_w_q, 0, it_attn[1], "attn_w_q", tok),
                                  placed(attn_w_o, 0, it_attn[2], "attn_w_o", tok)], tok)
    token = start("ffn1", it_ffn, [placed(ffn_w_in, 1, it_ffn[0], "ffn_w_in1", tok),
                                   placed(ffn_w_out, 1, it_ffn[1], "ffn_w_out1", tok)], tok)

    a1 = row(ng[0, 0] * (1.0 + sc1[0])) + token[0, 0]
    (h1,) = _norm_mod(x0, a1, row(sh1[0]), "l0_norm1")
    tab = _bias_table(rel_bias[0], "l1_bias_table")
    h1, tab = lax.optimization_barrier((h1, tab))
    (W_cin,) = arrived("conv_in", h1)
    bcx = _mm(h1, W_cin, "nn", BF16, "l0_conv_in", b_layer=0)
    ug = _conv_gate(bcx, ck8, "l0_conv_gate")
    gt1 = row(g1[0] * ng[0, 1])
    a2 = row(ng[0, 2] * (1.0 + sc2[0]))
    (W_cout,) = arrived("conv_out", ug)
    y1, x1, h2 = _mm_post(ug, W_cout, x0, gt1, "l0_conv_out", scales=a2, shifts=row(sh2[0]))
    (W_fin0,) = arrived("ffn0_in", h2)
    gu0, act0 = _ffn_in_act(h2, W_fin0, 0, "l0_ffn_in")
    (W_fout0,) = arrived("ffn0_out", act0)
    gt2 = row(g2[0] * ng[0, 3])
    a3 = ng[1, 0] * (1.0 + sc1[1])
    akv = kv_norm_g * (1.0 + kv_sc)
    y2, x2, h3, hkv = _mm_post(act0, W_fout0, x1, gt2, "l0_ffn_out",
                               scales=jnp.stack([a3, akv]), shifts=jnp.stack([sh1[1], kv_sh]))
    W_kv, W_q, W_o = arrived("attn", hkv)
    kvp = _mm(hkv, W_kv, "nn", BF16, "l1_kv", b_layer=0)
    att_scale = (D // N_HEADS) ** -0.5
    assert math.log2(att_scale) % 1 == 0, "scaling q before its bf16 cast is exact only for a power of two"
    qp = _mm(h3, W_q, "nn", BF16, "l1_q", b_layer=0, scale=att_scale)
    oh = _attn_fwd(qp, kvp, tab, "l1_attn")
    gt3 = row(g1[1] * ng[1, 1])
    a4 = row(ng[1, 2] * (1.0 + sc2[1]))
    y3, x3, h4 = _mm_post(oh, W_o, x2, gt3, "l1_attn_out", scales=a4, shifts=row(sh2[1]))
    W_fin1, W_fout1 = arrived("ffn1", h4)
    gu1, act1 = _ffn_in_act(h4, W_fin1, 0, "l1_ffn_in")
    gt4 = row(g2[1] * ng[1, 3])
    y4, dx4, sq = _mm_post(act1, W_fout1, x3, gt4, "l1_ffn_out", target=target)
    loss_part = 0.5 * jnp.sum(sq) / D

    def ffn_bwd(dxn, xin_, h, gu, act, y, gt, a, w_in, w_out, tag):
        dy, dgt = _post_norm_bwd(dxn, y, gt, f"{tag}_post2_bwd")
        dgu = _ffn_out_dx_act(dy, w_out, 0, gu, f"{tag}_ffn_out_dx")
        g_fout = _mm(act, dy, "tn", BF16, f"{tag}_ffn_out_dw", tm=TF)
        dh = _mm(dgu, w_in, "nt", BF16, f"{tag}_ffn_in_dx", b_layer=0)
        g_fin = _mm(h, dgu, "tn", BF16, f"{tag}_ffn_in_dw", tn=TF)
        dx, ds, db = _pre_norm_bwd(xin_, dxn, [dh], a, f"{tag}_norm2_bwd")
        return dx, _sum8(dgt), _sum8(ds)[0], _sum8(db)[0], g_fin, g_fout

    dx3, dgt4, da4, db4, G_fin1, G_fout1 = ffn_bwd(dx4, x3, h4, gu1, act1, y4, gt4, a4, W_fin1, W_fout1, "l1")
    red = [_pair_start([G_fin1, G_fout1], it_ffn, token, "ffn1")]
    dy3, dgt3 = _post_norm_bwd(dx3, y3, gt3 + red[0].token[0, 0], "l1_post1_bwd")
    doh = _mm(dy3, W_o, "nt", BF16, "l1_attn_out_dx", b_layer=0)
    G_o = _mm(oh, dy3, "tn", BF16, "l1_attn_out_dw")
    _owner_start(red[0], G_o)
    dq, dkv, dtab = _attn_bwd(qp, kvp, tab, doh, "l1_attn_bwd")
    d_rel = _bias_table_grad(dtab)
    dh3 = _mm(dq, W_q, "nt", BF16, "l1_q_dx", b_layer=0)
    G_q = _mm(h3, dq, "tn", BF16, "l1_q_dw")
    dhkv = _mm(dkv, W_kv, "nt", BF16, "l1_kv_dx", b_layer=0)
    G_kv = _mm(hkv, dkv, "tn", BF16, "l1_kv_dw")
    red.append(_pair_start([G_kv, G_q, G_o], it_attn, red[-1].token, "attn"))
    dx2, ds3, db3 = _pre_norm_bwd(x2, dx3, [dh3, dhkv], jnp.stack([a3, akv]) + red[1].token[0, 0], "l1_norm1_bwd")
    _owner_start(red[1], dx2)
    ds3, db3 = _sum8(ds3), _sum8(db3)

    dx1, dgt2, da2, db2, G_fin0, G_fout0 = ffn_bwd(dx2, x1, h2, gu0, act0, y2, gt2, a2, W_fin0, W_fout0, "l0")
    red.append(_pair_start([G_fin0, G_fout0], it_ffn, red[-1].token, "ffn0"))
    dy1, dgt1 = _post_norm_bwd(dx1, y1, gt1 + red[2].token[0, 0], "l0_post1_bwd")
    _owner_start(red[2], dy1)
    dug = _mm(dy1, W_cout, "nt", BF16, "l0_conv_out_dx", b_layer=0)
    G_cout = _mm(ug, dy1, "tn", BF16, "l0_conv_out_dw")
    dbcx, dck = _conv_gate_bwd(dug, bcx, ck8, "l0_conv_gate_bwd")
    dh1 = _mm(dbcx, W_cin, "nt", BF16, "l0_conv_in_dx", b_layer=0)
    G_cin = _mm(h1, dbcx, "tn", BF16, "l0_conv_in_dw")
    red.append(_pair_start([G_cin, G_cout], it_conv, red[-1].token, "conv"))
    dx0, ds1, db1 = _pre_norm_bwd(x0, dx1, [dh1], a1 + red[3].token[0, 0], "l0_norm1_bwd")
    ds1, db1 = _sum8(ds1)[0], _sum8(db1)[0]
    dgt1, dgt3 = _sum8(dgt1), _sum8(dgt3)

    def dmod_of(l, ds_a, db_a, dgt_a, ds_b, db_b, dgt_b):
        return jnp.concatenate([db_a, ds_a * ng[l, 0], dgt_a * ng[l, 1], db_b, ds_b * ng[l, 2], dgt_b * ng[l, 3]])

    dmod0 = dmod_of(0, ds1, db1, dgt1, da2, db2, dgt2)
    dmod1 = dmod_of(1, ds3[0], db3[0], dgt3, da4, db4, dgt4)
    dkvmod = jnp.concatenate([db3[1], ds3[1] * kv_norm_g])
    dng = jnp.stack([
        jnp.stack([ds1 * (1.0 + sc1[0]), dgt1 * g1[0], da2 * (1.0 + sc2[0]), dgt2 * g2[0]]),
        jnp.stack([ds3[0] * (1.0 + sc1[1]), dgt3 * g1[1], da4 * (1.0 + sc2[1]), dgt4 * g2[1]])])
    dkvng = ds3[1] * (1.0 + kv_sc)
    small = [dmod0, dmod1, dkvmod, dng.reshape(-1), dkvng, _sum8(dck).reshape(-1), d_rel.reshape(-1),
             loss_part.reshape(1)]
    sizes = [int(s.shape[0]) for s in small]
    offs = np.concatenate([[0], np.cumsum(sizes)])
    bwd_all = _gather_flat(jnp.concatenate(small), "ag_bwd_small")
    _owner_start(red[3], bwd_all)
    Lb = bwd_all.shape[1]
    Lp = -(-Lb // 128) * 128
    tot = _sum_rows(jnp.pad(bwd_all, ((0, 0), (0, Lp - Lb))), "sum_small")[0]
    seg = lambda i: tot[offs[i]:offs[i + 1]]
    g_mod_b = jnp.stack([seg(0), seg(1)])
    g_kv_mod_b = seg(2)
    g_norm_g = lax.dynamic_slice_in_dim(seg(3).reshape(2, 4, D), chip * dsh, dsh, axis=2)
    g_kv_norm_g = seg(4)
    g_conv_k = lax.dynamic_slice_in_dim(seg(5).reshape(1, 3, D), chip * dsh, dsh, axis=2)
    g_rel_bias = seg(6).reshape(rel_bias.shape)
    loss = seg(7)[0]

    def dmod_w(i, n, name):
        rows_ = lax.dynamic_slice_in_dim(bwd_all[:, offs[i]:offs[i + 1]], chip * n, n, axis=1)
        return _mm(sc16, jnp.pad(rows_, ((0, 8), (0, 0))), "tn", F32, name)

    g_mod_w = jnp.stack([dmod_w(0, n_mod, "mod_bwd_0"), dmod_w(1, n_mod, "mod_bwd_1")])
    g_kv_mod_w = dmod_w(2, n_kvm, "mod_bwd_kv")

    half_send, half_recv, half_bufs, _ = _reduce_finish(red, g_kv_mod_w)
    grads = {
        "mod_w": g_mod_w, "mod_b": g_mod_b, "norm_g": g_norm_g, "conv_k": g_conv_k,
        "kv_mod_w": g_kv_mod_w, "kv_mod_b": g_kv_mod_b, "kv_norm_g": g_kv_norm_g, "rel_bias": g_rel_bias,
    }
    weights = dict(mod_w=mod_w, mod_b=mod_b, norm_g=norm_g, ffn_w_in=ffn_w_in, ffn_w_out=ffn_w_out,
                   conv_w_in=conv_w_in, conv_k=conv_k, conv_w_out=conv_w_out, kv_mod_w=kv_mod_w,
                   kv_mod_b=kv_mod_b, kv_norm_g=kv_norm_g, w_kv=w_kv, attn_w_q=attn_w_q, attn_w_o=attn_w_o,
                   rel_bias=rel_bias)
    m_in = dict(mod_w=m_mod_w, mod_b=m_mod_b, norm_g=m_norm_g, ffn_w_in=m_ffn_w_in, ffn_w_out=m_ffn_w_out,
                conv_w_in=m_conv_w_in, conv_k=m_conv_k, conv_w_out=m_conv_w_out, kv_mod_w=m_kv_mod_w,
                kv_mod_b=m_kv_mod_b, kv_norm_g=m_kv_norm_g, w_kv=m_w_kv, attn_w_q=m_attn_w_q,
                attn_w_o=m_attn_w_o, rel_bias=m_rel_bias)
    v_in = dict(mod_w=v_mod_w, mod_b=v_mod_b, norm_g=v_norm_g, ffn_w_in=v_ffn_w_in, ffn_w_out=v_ffn_w_out,
                conv_w_in=v_conv_w_in, conv_k=v_conv_k, conv_w_out=v_conv_w_out, kv_mod_w=v_kv_mod_w,
                kv_mod_b=v_kv_mod_b, kv_norm_g=v_kv_norm_g, w_kv=v_w_kv, attn_w_q=v_attn_w_q,
                attn_w_o=v_attn_w_o, rel_bias=v_rel_bias)
    names = list(weights)
    step = {}

    def update(n):
        g = grads[n].reshape(weights[n].shape)
        step[n] = (g, *_adamw(weights[n], g, m_in[n], v_in[n], f"adamw_{n}"))

    for n in list(grads):
        update(n)
    r_fin1, r_fout1, r_kv, r_q, r_o, r_fin0, r_fout0, r_cin, r_cout = _split_wait(
        _half_copies(len(half_bufs)), half_send, half_recv, half_bufs, step["mod_w"][1], "rs_half_wait")
    grads.update({
        "ffn_w_in": jnp.stack([r_fin0, r_fin1]), "ffn_w_out": jnp.stack([r_fout0, r_fout1]),
        "conv_w_in": r_cin[None], "conv_w_out": r_cout[None], "w_kv": r_kv,
        "attn_w_q": r_q[None], "attn_w_o": r_o[None],
    })
    for n in names:
        if n not in step:
            update(n)
    return (loss, dx0.reshape(x.shape), *[step[n][k] for k in range(4) for n in names])
```

```python
import functools
import math

import numpy as np
import jax
import jax.numpy as jnp
from jax import lax
from jax.experimental import pallas as pl
from jax.experimental.pallas import tpu as pltpu

CHUNK = 64
N_LEFT_CHUNKS = 8
N_HEADS = 16
MAX_REL = 2 * CHUNK
N_REL = 2 * MAX_REL + 1
EPS = 1e-6
ADAM_LR = 0.001
ADAM_B1 = 0.9
ADAM_B2 = 0.999
ADAM_EPS = 1e-08
ADAM_WD = 0.01
ADAM_STEP = 10

Q_CHUNKS = 4
BQ = Q_CHUNKS * CHUNK
N_WIN = 1 + N_LEFT_CHUNKS // Q_CHUNKS
HEADS_PER_STEP = 8
NEG = -1e30
N_DEV = 8
N_CHIP = 4

BF16 = jnp.bfloat16
F32 = jnp.float32
V7X_VMEM_LIMIT_BYTES = 56 * 1024 * 1024
MESH = pl.DeviceIdType.MESH


def _pick(n, pref, align):
    t = min(pref, n)
    t -= t % align
    while t >= align:
        if n % t == 0:
            return t
        t -= align
    return n


def _params(*sem):
    return pltpu.CompilerParams(dimension_semantics=sem, vmem_limit_bytes=V7X_VMEM_LIMIT_BYTES)


def _colsum8(v):
    r, d = v.shape
    return v.reshape(r // 8, 8, d).sum(axis=0)


_DIMS = {"nn": (((1,), (0,)), ((), ())), "nt": (((1,), (1,)), ((), ())), "tn": (((0,), (0,)), ((), ()))}


def _mm(a, b, mode, out_dtype, name, *, b_layer=None, tm=1024, tn=1024, tk=None, scale=None):
    if tk is None:
        tk = 2048 if mode == "tn" else 3072
    bs = b.shape[1:] if b_layer is not None else b.shape
    if mode == "nn":
        (M, K), (K2, N) = a.shape, bs
    elif mode == "nt":
        (M, K), (N, K2) = a.shape, bs
    else:
        (K, M), (K2, N) = a.shape, bs
    assert K == K2, (name, a.shape, b.shape)
    tm = _pick(M, tm, 128 if mode == "tn" else 16)
    tn = _pick(N, tn, 128)
    tk = _pick(K, tk, 128 if mode != "tn" else 16)
    nk = K // tk
    assert scale is None or nk == 1, name
    dims = _DIMS[mode]

    def body(a_ref, b_ref, o_ref, *acc):
        p = lax.dot_general(a_ref[...].astype(BF16), b_ref[...].astype(BF16), dims,
                            preferred_element_type=F32)
        if nk == 1:
            o_ref[...] = (p if scale is None else p * scale).astype(o_ref.dtype)
        else:
            k = pl.program_id(2)

            @pl.when(k == 0)
            def _():
                acc[0][...] = p

            @pl.when(k > 0)
            def _():
                acc[0][...] += p

            @pl.when(k == nk - 1)
            def _():
                o_ref[...] = acc[0][...].astype(o_ref.dtype)

    a_spec = (pl.BlockSpec((tk, tm), lambda i, j, k: (k, i)) if mode == "tn"
              else pl.BlockSpec((tm, tk), lambda i, j, k: (i, k)))
    if mode == "nt":
        b_blk, b_idx = (tn, tk), (lambda i, j, k: (j, k))
    else:
        b_blk, b_idx = (tk, tn), (lambda i, j, k: (k, j))
    if b_layer is not None:
        b_spec = pl.BlockSpec((None,) + b_blk, lambda i, j, k: (b_layer,) + b_idx(i, j, k))
    else:
        b_spec = pl.BlockSpec(b_blk, b_idx)
    return pl.pallas_call(
        body, name=name,
        grid=(M // tm, N // tn, nk),
        in_specs=[a_spec, b_spec],
        out_specs=pl.BlockSpec((tm, tn), lambda i, j, k: (i, j)),
        out_shape=jax.ShapeDtypeStruct((M, N), out_dtype),
        scratch_shapes=[pltpu.VMEM((tm, tn), F32)] if nk > 1 else [],
        compiler_params=_params("parallel", "parallel", "arbitrary"),
    )(a, b)


def _row_spec(tm, d):
    return pl.BlockSpec((tm, d), lambda i: (i, 0))


def _vec_spec(r, d):
    return pl.BlockSpec((r, d), lambda i: (0, 0))


def _norm_mod(x, scales, shifts, name):
    S, D = x.shape
    nb = scales.shape[0]
    tm = _pick(S, 512, 16)

    def body(x_ref, a_ref, b_ref, *o_refs):
        xv = x_ref[...]
        xh = xv * lax.rsqrt(jnp.mean(xv * xv, axis=-1, keepdims=True) + EPS)
        for n in range(nb):
            o_refs[n][...] = (xh * a_ref[n:n + 1, :] + b_ref[n:n + 1, :]).astype(BF16)

    return pl.pallas_call(
        body, name=name, grid=(S // tm,),
        in_specs=[_row_spec(tm, D), _vec_spec(nb, D), _vec_spec(nb, D)],
        out_specs=[_row_spec(tm, D)] * nb,
        out_shape=[jax.ShapeDtypeStruct((S, D), BF16)] * nb,
        compiler_params=_params("parallel"),
    )(x, scales, shifts)


def _mm_post(a, w, x, gate, name, *, scales=None, shifts=None, target=None):
    M, K = a.shape
    D = w.shape[2]
    tm = _pick(M, 512, 16)
    sub = _pick(tm, 256, 16)
    nb = 0 if scales is None else scales.shape[0]

    def body(a_ref, w_ref, x_ref, g_ref, *rest):
        if target is None:
            sc_ref, sh_ref, y_ref, xn_ref = rest[:4]
            h_refs = rest[4:]
        else:
            t_ref, dx_ref, sq_ref, dy_ref, dg_ref = rest

            @pl.when(pl.program_id(0) == 0)
            def _():
                sq_ref[...] = jnp.zeros_like(sq_ref)
                dg_ref[...] = jnp.zeros_like(dg_ref)

        for r in range(tm // sub):
            rows = pl.ds(r * sub, sub)
            yb = jnp.dot(a_ref[rows, :], w_ref[...], preferred_element_type=F32).astype(BF16)
            yv = yb.astype(F32)
            yh = yv * lax.rsqrt(jnp.mean(yv * yv, axis=-1, keepdims=True) + EPS)
            xn = x_ref[rows, :] + yh * g_ref[...]
            if target is None:
                y_ref[rows, :] = yb
                xn_ref[rows, :] = xn
                xh = xn * lax.rsqrt(jnp.mean(xn * xn, axis=-1, keepdims=True) + EPS)
                for n in range(nb):
                    h_refs[n][rows, :] = (xh * sc_ref[n:n + 1, :] + sh_ref[n:n + 1, :]).astype(BF16)
            else:
                e = xn - t_ref[rows, :]
                dx = e / D
                dx_ref[rows, :] = dx
                sq_ref[...] += _colsum8(e * e)
                dy, dxy = _post_norm_grad(dx, yb, g_ref[...])
                dy_ref[rows, :] = dy.astype(BF16)
                dg_ref[...] += _colsum8(dxy)

    ins = [a, w, x, gate]
    in_specs = [_row_spec(tm, K), pl.BlockSpec((None, K, D), lambda i: (0, 0, 0)), _row_spec(tm, D), _vec_spec(1, D)]
    if target is None:
        ins += [scales, shifts]
        in_specs += [_vec_spec(nb, D), _vec_spec(nb, D)]
        out_specs = [_row_spec(tm, D)] * (2 + nb)
        out_shape = [jax.ShapeDtypeStruct((M, D), BF16), jax.ShapeDtypeStruct((M, D), F32)] \
            + [jax.ShapeDtypeStruct((M, D), BF16)] * nb
    else:
        ins += [target]
        in_specs += [_row_spec(tm, D)]
        out_specs = [_row_spec(tm, D), _vec_spec(8, D), _row_spec(tm, D), _vec_spec(8, D)]
        out_shape = [jax.ShapeDtypeStruct((M, D), F32), jax.ShapeDtypeStruct((8, D), F32),
                     jax.ShapeDtypeStruct((M, D), BF16), jax.ShapeDtypeStruct((8, D), F32)]
    return pl.pallas_call(
        body, name=name, grid=(M // tm,), in_specs=in_specs, out_specs=out_specs, out_shape=out_shape,
        compiler_params=_params("arbitrary" if target is not None else "parallel"),
    )(*ins)


def _post_norm_grad(dxn, yb, gate):
    yv = yb.astype(F32)
    r = lax.rsqrt(jnp.mean(yv * yv, axis=-1, keepdims=True) + EPS)
    yh = yv * r
    dyh = dxn * gate
    return r * (dyh - yh * jnp.mean(dyh * yh, axis=-1, keepdims=True)), dxn * yh


def _mm_pre_bwd(pairs, x, dxn, scales, name, post=None):
    S, D = x.shape
    nb = len(pairs)
    tm = _pick(S, 256 if max(a.shape[1] for a, _ in pairs) > 4 * D else 512, 16)
    sub = _pick(tm, 256, 16)

    def body(*refs):
        a_refs, w_refs = refs[0:2 * nb:2], refs[1:2 * nb:2]
        x_ref, d_ref, sc_ref = refs[2 * nb:2 * nb + 3]
        rest = refs[2 * nb + 3:]
        if post is not None:
            y_ref, g_ref, dx_ref, ds_ref, db_ref, dy_ref, dg_ref = rest
        else:
            dx_ref, ds_ref, db_ref = rest

        @pl.when(pl.program_id(0) == 0)
        def _():
            ds_ref[...] = jnp.zeros_like(ds_ref)
            db_ref[...] = jnp.zeros_like(db_ref)
            if post is not None:
                dg_ref[...] = jnp.zeros_like(dg_ref)

        for r in range(tm // sub):
            rows = pl.ds(r * sub, sub)
            xv = x_ref[rows, :]
            rr = lax.rsqrt(jnp.mean(xv * xv, axis=-1, keepdims=True) + EPS)
            xh = xv * rr
            dxh = jnp.zeros_like(xv)
            for n in range(nb):
                dh = lax.dot_general(a_refs[n][rows, :], w_refs[n][...], _DIMS["nt"], preferred_element_type=F32)
                dxh = dxh + dh * sc_ref[n:n + 1, :]
                ds_ref[n] += _colsum8(dh * xh)
                db_ref[n] += _colsum8(dh)
            dx = d_ref[rows, :] + rr * (dxh - xh * jnp.mean(dxh * xh, axis=-1, keepdims=True))
            dx_ref[rows, :] = dx
            if post is not None:
                dy, dxy = _post_norm_grad(dx, y_ref[rows, :], g_ref[...])
                dy_ref[rows, :] = dy.astype(BF16)
                dg_ref[...] += _colsum8(dxy)

    ins, in_specs = [], []
    for a, w in pairs:
        ins += [a, w]
        in_specs += [_row_spec(tm, a.shape[1]), pl.BlockSpec((None, D, a.shape[1]), lambda i: (0, 0, 0))]
    ins += [x, dxn, scales]
    in_specs += [_row_spec(tm, D), _row_spec(tm, D), _vec_spec(nb, D)]
    acc_spec = pl.BlockSpec((nb, 8, D), lambda i: (0, 0, 0))
    out_specs = [_row_spec(tm, D), acc_spec, acc_spec]
    out_shape = [jax.ShapeDtypeStruct((S, D), F32), jax.ShapeDtypeStruct((nb, 8, D), F32),
                 jax.ShapeDtypeStruct((nb, 8, D), F32)]
    if post is not None:
        ins += list(post)
        in_specs += [_row_spec(tm, D), _vec_spec(1, D)]
        out_specs += [_row_spec(tm, D), _vec_spec(8, D)]
        out_shape += [jax.ShapeDtypeStruct((S, D), BF16), jax.ShapeDtypeStruct((8, D), F32)]
    return pl.pallas_call(
        body, name=name, grid=(S // tm,), in_specs=in_specs, out_specs=out_specs, out_shape=out_shape,
        compiler_params=_params("arbitrary"),
    )(*ins)


FFN_PAIRS = 2
FFN_SUB_ROWS = 256


def _ffn_in_act(h, w, layer, name):
    S, D = h.shape
    F2 = w.shape[2]
    PW = F2 // (2 * FFN_PAIRS)
    tm = _pick(S, 512, 16)
    sub = _pick(tm, FFN_SUB_ROWS, 16)

    def body(h_ref, w_ref, gu_ref, a_ref):
        for r in range(tm // sub):
            rows = pl.ds(r * sub, sub)
            acc = jnp.dot(h_ref[rows, :], w_ref[...], preferred_element_type=F32)
            gu_ref[rows, :] = acc.astype(BF16)
            g = acc[:, :PW]
            a_ref[rows, :] = (g * jax.nn.sigmoid(g) * acc[:, PW:]).astype(BF16)

    return pl.pallas_call(
        body, name=name, grid=(FFN_PAIRS, S // tm),
        in_specs=[pl.BlockSpec((tm, D), lambda p, i: (i, 0)),
                  pl.BlockSpec((None, D, 2 * PW), lambda p, i: (layer, 0, p))],
        out_specs=[pl.BlockSpec((tm, 2 * PW), lambda p, i: (i, p)), pl.BlockSpec((tm, PW), lambda p, i: (i, p))],
        out_shape=[jax.ShapeDtypeStruct((S, F2), BF16), jax.ShapeDtypeStruct((S, F2 // 2), BF16)],
        compiler_params=_params("parallel", "parallel"),
    )(h, w)


def _ffn_out_dx_act(dy, w, layer, gu, name):
    S, D = dy.shape
    F2 = gu.shape[1]
    PW = F2 // (2 * FFN_PAIRS)
    tm = _pick(S, 512, 16)
    sub = _pick(tm, FFN_SUB_ROWS, 16)

    def body(dy_ref, w_ref, gu_ref, o_ref):
        for r in range(tm // sub):
            rows = pl.ds(r * sub, sub)
            da = lax.dot_general(dy_ref[rows, :], w_ref[...], _DIMS["nt"], preferred_element_type=F32)
            g = gu_ref[rows, 0:PW].astype(F32)
            u = gu_ref[rows, PW:2 * PW].astype(F32)
            sg = jax.nn.sigmoid(g)
            o_ref[rows, 0:PW] = (da * u * (sg * (1.0 + g * (1.0 - sg)))).astype(BF16)
            o_ref[rows, PW:2 * PW] = (da * (g * sg)).astype(BF16)

    return pl.pallas_call(
        body, name=name, grid=(FFN_PAIRS, S // tm),
        in_specs=[pl.BlockSpec((tm, D), lambda p, i: (i, 0)),
                  pl.BlockSpec((None, PW, D), lambda p, i: (layer, p, 0)),
                  pl.BlockSpec((tm, 2 * PW), lambda p, i: (i, p))],
        out_specs=pl.BlockSpec((tm, 2 * PW), lambda p, i: (i, p)),
        out_shape=jax.ShapeDtypeStruct((S, F2), BF16),
        compiler_params=_params("parallel", "parallel"),
    )(dy, w, gu)


HALO = 16


def _conv_terms(bcx_ref, prev_ref, i, tm, D):
    b = bcx_ref[:, 0:D].astype(F32)
    cg = bcx_ref[:, D:2 * D].astype(F32)
    xin = bcx_ref[:, 2 * D:3 * D].astype(F32)
    z = cg * xin
    zp = prev_ref[:, D:2 * D].astype(F32) * prev_ref[:, 2 * D:3 * D].astype(F32)
    zp = jnp.where(i > 0, zp, 0.0)
    z_ext = jnp.concatenate([zp, z], axis=0)
    z1 = pltpu.roll(z_ext, 1, 0)[HALO:, :]
    z2 = pltpu.roll(z_ext, 2, 0)[HALO:, :]
    return b, cg, xin, z, z1, z2


def _conv_gate(bcx, ck, name):
    S, D3 = bcx.shape
    D = D3 // 3
    tm = _pick(S, 256, 16)
    hb = tm // HALO

    def body(bcx_ref, prev_ref, ck_ref, o_ref):
        i = pl.program_id(0)
        b, _, _, z, z1, z2 = _conv_terms(bcx_ref, prev_ref, i, tm, D)
        conv = ck_ref[0:1, :] * z2 + ck_ref[1:2, :] * z1 + ck_ref[2:3, :] * z
        o_ref[...] = (b * conv).astype(BF16)

    return pl.pallas_call(
        body, name=name, grid=(S // tm,),
        in_specs=[_row_spec(tm, D3),
                  pl.BlockSpec((HALO, D3), lambda i: (jnp.maximum(i * hb - 1, 0), 0)),
                  _vec_spec(8, D)],
        out_specs=_row_spec(tm, D),
        out_shape=jax.ShapeDtypeStruct((S, D), BF16),
        compiler_params=_params("parallel"),
    )(bcx, bcx, ck)


def _conv_gate_bwd(du, bcx, ck, name):
    S, D3 = bcx.shape
    D = D3 // 3
    tm = _pick(S, 256, 16)
    hb = tm // HALO
    nt = S // tm

    def body(du_ref, dun_ref, bcx_ref, prev_ref, next_ref, ck_ref, o_ref, dk_ref):
        i = pl.program_id(0)
        b, cg, xin, z, z1, z2 = _conv_terms(bcx_ref, prev_ref, i, tm, D)
        k0, k1, k2 = ck_ref[0:1, :], ck_ref[1:2, :], ck_ref[2:3, :]
        conv = k0 * z2 + k1 * z1 + k2 * z
        d = du_ref[...].astype(F32)
        dconv = d * b
        dcn = jnp.where(i < nt - 1, dun_ref[...].astype(F32) * next_ref[:, 0:D].astype(F32), 0.0)
        d_ext = jnp.concatenate([dconv, dcn], axis=0)
        d1 = pltpu.roll(d_ext, tm + HALO - 1, 0)[:tm, :]
        d2 = pltpu.roll(d_ext, tm + HALO - 2, 0)[:tm, :]
        dz = k2 * dconv + k1 * d1 + k0 * d2
        o_ref[:, 0:D] = (d * conv).astype(BF16)
        o_ref[:, D:2 * D] = (dz * xin).astype(BF16)
        o_ref[:, 2 * D:3 * D] = (dz * cg).astype(BF16)

        @pl.when(i == 0)
        def _():
            dk_ref[...] = jnp.zeros_like(dk_ref)

        dk_ref[0] += _colsum8(dconv * z2)
        dk_ref[1] += _colsum8(dconv * z1)
        dk_ref[2] += _colsum8(dconv * z)

    last = S // HALO - 1
    return pl.pallas_call(
        body, name=name, grid=(nt,),
        in_specs=[_row_spec(tm, D),
                  pl.BlockSpec((HALO, D), lambda i: (jnp.minimum((i + 1) * hb, last), 0)),
                  _row_spec(tm, D3),
                  pl.BlockSpec((HALO, D3), lambda i: (jnp.maximum(i * hb - 1, 0), 0)),
                  pl.BlockSpec((HALO, D3), lambda i: (jnp.minimum((i + 1) * hb, last), 0)),
                  _vec_spec(8, D)],
        out_specs=[_row_spec(tm, D3), pl.BlockSpec((3, 8, D), lambda i: (0, 0, 0))],
        out_shape=[jax.ShapeDtypeStruct((S, D3), BF16), jax.ShapeDtypeStruct((3, 8, D), F32)],
        compiler_params=_params("arbitrary"),
    )(du, du, bcx, bcx, bcx, ck)


def _rel_onehot():
    a = np.arange(CHUNK)[:, None]
    b = np.arange(CHUNK)[None, :]
    idx = np.stack([np.clip((N_LEFT_CHUNKS - dl) * CHUNK + a - b, -MAX_REL, MAX_REL) + MAX_REL
                    for dl in (6, 7, 8)]).reshape(-1)
    return (jnp.asarray(idx)[:, None] == jnp.arange(N_REL)[None, :]).astype(F32)


def _bias_table(rel_bias, name):
    H = rel_bias.shape[0]
    near = jnp.dot(rel_bias, _rel_onehot().T, precision=lax.Precision.HIGHEST).reshape(H, 3, CHUNK, CHUNK)
    far = jnp.broadcast_to(rel_bias[:, N_REL - 1][:, None, None], (H, CHUNK, CHUNK))

    def body(near_ref, far_ref, o_ref):
        neg = jnp.full((CHUNK, CHUNK), NEG, F32)
        for v in range(N_WIN):
            for ic in range(Q_CHUNKS):
                for jc in range(N_WIN * Q_CHUNKS):
                    dl = jc - ic
                    if dl < 0 or dl > N_LEFT_CHUNKS or jc < (N_WIN - 1 - v) * Q_CHUNKS:
                        blk = neg
                    else:
                        blk = far_ref[...] if dl <= 5 else near_ref[dl - 6]
                    o_ref[v, ic * CHUNK:(ic + 1) * CHUNK, jc * CHUNK:(jc + 1) * CHUNK] = blk

    return pl.pallas_call(
        body, name=name, grid=(H,),
        in_specs=[pl.BlockSpec((None, 3, CHUNK, CHUNK), lambda h: (h, 0, 0, 0)),
                  pl.BlockSpec((None, CHUNK, CHUNK), lambda h: (h, 0, 0))],
        out_specs=pl.BlockSpec((N_WIN, None, BQ, N_WIN * BQ), lambda h: (0, h, 0, 0)),
        out_shape=jax.ShapeDtypeStruct((N_WIN, H, BQ, N_WIN * BQ), F32),
        compiler_params=_params("parallel"),
    )(near, far)


def _bias_table_grad(dtab):
    H = dtab.shape[0]
    blk = lambda ic, jc: dtab[:, ic * CHUNK:(ic + 1) * CHUNK, jc * CHUNK:(jc + 1) * CHUNK]
    by_dl = [sum(blk(ic, ic + dl) for ic in range(Q_CHUNKS)) for dl in range(N_LEFT_CHUNKS + 1)]
    far = sum(jnp.sum(by_dl[dl], axis=(1, 2)) for dl in range(6))
    near = jnp.stack(by_dl[6:9], axis=1).reshape(H, 3 * CHUNK * CHUNK)
    g = jnp.dot(near, _rel_onehot(), precision=lax.Precision.HIGHEST)
    return g.at[:, N_REL - 1].add(far)


def _attn_specs(nblk, W):
    last = nblk - 1
    q_spec = pl.BlockSpec((BQ, W), lambda g, i: (jnp.minimum(i, last), g))
    kv_specs = [pl.BlockSpec((BQ, 2 * W), functools.partial(
        lambda g, i, w: (jnp.maximum(jnp.minimum(i, last) - (N_WIN - 1) + w, 0), g), w=w)) for w in range(N_WIN)]
    tab_spec = pl.BlockSpec((None, HEADS_PER_STEP, BQ, N_WIN * BQ),
                            lambda g, i: (jnp.minimum(i, N_WIN - 1), g, 0, 0))
    dtab_spec = pl.BlockSpec((HEADS_PER_STEP, BQ, N_WIN * BQ), lambda g, i: (g, 0, 0))
    return q_spec, kv_specs, tab_spec, dtab_spec


def _attn_exp(q_ref, kT, tab_ref, h, dh):
    s = jnp.dot(q_ref[:, h * dh:(h + 1) * dh], kT[h * dh:(h + 1) * dh, :], preferred_element_type=F32) + tab_ref[h]
    e = jnp.exp(s - jnp.max(s, axis=-1, keepdims=True))
    return e, jnp.sum(e, axis=-1, keepdims=True)


def _attn_fwd(q, kv, tab, name):
    S, D = q.shape
    dh = D // N_HEADS
    W = HEADS_PER_STEP * dh
    assert 2 * W == D, "the kv layout puts one head group's k beside its v: two head groups"
    q_spec, kv_specs, tab_spec, _ = _attn_specs(S // BQ, W)

    def body(q_ref, *rest):
        tab_ref, o_ref = rest[N_WIN], rest[N_WIN + 1]
        kvw = jnp.concatenate([r[...] for r in rest[:N_WIN]], axis=0)
        kT = kvw[:, :W].T
        vw = kvw[:, W:]
        outs = []
        for h in range(HEADS_PER_STEP):
            e, l = _attn_exp(q_ref, kT, tab_ref, h, dh)
            outs.append(jnp.dot(e.astype(BF16), vw[:, h * dh:(h + 1) * dh], preferred_element_type=F32) / l)
        o_ref[...] = jnp.concatenate(outs, axis=1).astype(BF16)

    return pl.pallas_call(
        body, name=name, grid=(N_HEADS // HEADS_PER_STEP, S // BQ),
        in_specs=[q_spec] + kv_specs + [tab_spec],
        out_specs=q_spec,
        out_shape=jax.ShapeDtypeStruct((S, D), BF16),
        compiler_params=_params("parallel", "parallel"),
    )(q, *([kv] * N_WIN), tab)


def _attn_bwd(q, kv, tab, do, name):
    S, D = q.shape
    dh = D // N_HEADS
    W = HEADS_PER_STEP * dh
    nblk = S // BQ
    q_spec, kv_specs, tab_spec, dtab_spec = _attn_specs(nblk, W)

    def body(q_ref, *rest):
        tab_ref, do_ref, dq_ref, dkv_ref, dtab_ref, ring = rest[N_WIN:]
        i = pl.program_id(1)

        @pl.when(i == 0)
        def _():
            dtab_ref[...] = jnp.zeros_like(dtab_ref)
            ring[...] = jnp.zeros_like(ring)

        @pl.when(i < nblk)
        def _():
            kvw = jnp.concatenate([r[...] for r in rest[:N_WIN]], axis=0)
            kT = kvw[:, :W].T
            vT = kvw[:, W:].T
            qT = q_ref[...].T
            doT = do_ref[...].T
            dqs, dks, dvs = [], [], []
            for h in range(HEADS_PER_STEP):
                hd = slice(h * dh, (h + 1) * dh)
                e, l = _attn_exp(q_ref, kT, tab_ref, h, dh)
                p = e * (1.0 / l)
                dp = jnp.dot(do_ref[:, hd], vT[hd, :], preferred_element_type=F32)
                ds = p * (dp - jnp.sum(p * dp, axis=-1, keepdims=True))
                dtab_ref[h] += ds
                dsb = ds.astype(BF16)
                dqs.append(lax.dot_general(kT[hd, :], dsb, _DIMS["nt"], preferred_element_type=F32) * (dh ** -0.5))
                dks.append(jnp.dot(qT[hd, :], dsb, preferred_element_type=F32))
                dvs.append(jnp.dot(doT[hd, :], p.astype(BF16), preferred_element_type=F32))
            dq_ref[...] = jnp.concatenate(dqs, axis=0).T.astype(BF16)
            dkv = jnp.concatenate(dks + dvs, axis=0).T
            for w in range(N_WIN):
                slot = lax.rem(i + 1 + w, N_WIN)
                part = dkv[w * BQ:(w + 1) * BQ, :]
                if w == N_WIN - 1:
                    ring[slot] = part
                else:
                    ring[slot] += part

        dkv_ref[...] = ring[lax.rem(i + 1, N_WIN)].astype(BF16)

    done_spec = pl.BlockSpec((BQ, 2 * W), lambda g, i: (jnp.maximum(i - (N_WIN - 1), 0), g))
    return pl.pallas_call(
        body, name=name, grid=(N_HEADS // HEADS_PER_STEP, nblk + N_WIN - 1),
        in_specs=[q_spec] + kv_specs + [tab_spec, q_spec],
        out_specs=[q_spec, done_spec, dtab_spec],
        out_shape=[jax.ShapeDtypeStruct((S, D), BF16), jax.ShapeDtypeStruct((S, 2 * D), BF16),
                   jax.ShapeDtypeStruct(tab.shape[1:], F32)],
        scratch_shapes=[pltpu.VMEM((N_WIN, BQ, 2 * W), F32)],
        compiler_params=_params("parallel", "arbitrary"),
    )(q, *([kv] * N_WIN), tab, do)


def _adamw(w, g, m, v, name):
    shape = w.shape
    C = shape[-1]
    R = int(np.prod(shape[:-1])) if len(shape) > 1 else 1
    w2, g2, m2, v2 = (t.reshape(R, C) for t in (w, g, m, v))
    tr = _pick(R, max(8, (512 * 1024) // C // 8 * 8), 8)

    def body(w_ref, g_ref, m_ref, v_ref, d_ref, nm_ref, nv_ref):
        gv = g_ref[...]
        nm = ADAM_B1 * m_ref[...] + (1.0 - ADAM_B1) * gv
        nv = ADAM_B2 * v_ref[...] + (1.0 - ADAM_B2) * jnp.square(gv)
        m_hat = nm / (1.0 - ADAM_B1 ** ADAM_STEP)
        v_hat = nv / (1.0 - ADAM_B2 ** ADAM_STEP)
        d_ref[...] = -ADAM_LR * (m_hat / (jnp.sqrt(v_hat) + ADAM_EPS) + ADAM_WD * w_ref[...])
        nm_ref[...] = nm
        nv_ref[...] = nv

    spec = pl.BlockSpec((tr, C), lambda i: (i, 0))
    outs = pl.pallas_call(
        body, name=name, grid=(R // tr,),
        in_specs=[spec] * 4, out_specs=[spec] * 3,
        out_shape=[jax.ShapeDtypeStruct((R, C), F32)] * 3,
        compiler_params=_params("parallel"),
    )(w2, g2, m2, v2)
    return tuple(o.reshape(shape) for o in outs)


def _sum_rows(a, name):
    n, L = a.shape

    def body(a_ref, o_ref):
        acc = a_ref[0:1, :]
        for r in range(1, n):
            acc = acc + a_ref[r:r + 1, :]
        o_ref[...] = acc

    return pl.pallas_call(
        body, name=name, grid=(1,),
        in_specs=[pl.BlockSpec((n, L), lambda i: (0, 0))],
        out_specs=pl.BlockSpec((1, L), lambda i: (0, 0)),
        out_shape=jax.ShapeDtypeStruct((1, L), F32),
        compiler_params=_params("arbitrary"),
    )(a)


def _scalar_call(body, name, scalar, grid, in_specs, out_spec, out_shape, args):
    return pl.pallas_call(
        body, name=name,
        grid_spec=pltpu.PrefetchScalarGridSpec(num_scalar_prefetch=1, grid=grid, in_specs=in_specs,
                                               out_specs=out_spec),
        out_shape=out_shape, compiler_params=_params("parallel"),
    )(jnp.reshape(scalar, (-1,)).astype(jnp.int32), *args)


def _pair_sum(view, got, c, name):
    nb, _, rh, cols = view.shape
    tr = _pick(rh, max(16, (1 << 20) // cols // 16 * 16), 16)
    bpr = rh // tr

    def body(s_ref, a_ref, b_ref, o_ref):
        o_ref[...] = (a_ref[...].astype(F32) + b_ref[...].astype(F32)).astype(BF16)

    spec = pl.BlockSpec((tr, cols), lambda i, s: (i, 0))
    mine = pl.BlockSpec((tr, cols), lambda i, s: ((2 * (i // bpr) + s[0]) * bpr + i % bpr, 0))
    return _scalar_call(body, name, c, (nb * bpr,), [mine, spec], spec,
                        jax.ShapeDtypeStruct((nb * rh, cols), BF16),
                        (view.reshape(nb * 2 * rh, cols), got.reshape(nb * rh, cols)))


def _owner_sum(pair, recv, me, c, it, name):
    _, rh, bc = recv.shape
    tr = _pick(rh, max(16, (1 << 19) // bc // 16 * 16), 16)
    bpr = rh // tr

    def body(s_ref, a_ref, r0, r1, r2, o_ref):
        o_ref[...] = ((a_ref[...].astype(F32) + r0[...].astype(F32)) + r1[...].astype(F32)) + r2[...].astype(F32)

    if it.kind == "col":
        own = pl.BlockSpec((tr, bc), lambda i, s: (i, s[0]))
    else:
        own = pl.BlockSpec((tr, bc), lambda i, s: (s[0] * bpr + i, 0))
    slots = [pl.BlockSpec((None, tr, bc), functools.partial(lambda i, s, k: (k, i, 0), k=k)) for k in range(3)]
    return _scalar_call(body, name, jnp.stack([it.pos(me), c]), (bpr,), [own] + slots,
                        pl.BlockSpec((tr, bc), lambda i, s: (s[1] * bpr + i, 0)),
                        jax.ShapeDtypeStruct((2 * rh, bc), F32), (pair, recv, recv, recv))


def _place():
    x, y, c = lax.axis_index("x"), lax.axis_index("y"), lax.axis_index("c")
    chips = [(1 - x, y), (x, 1 - y), (1 - x, 1 - y)]
    return x, y, c, chips


def _chip_index(px, py):
    return 2 * px + py


def _all_gather_small(x_shard, name):
    m_per, n = x_shard.shape

    def body(x_ref, out_ref, send_sems, recv_sems, local_sem):
        x, y, c, chips = _place()
        me, sibling = (x, y, c), (x, y, 1 - c)

        def rows(px, py, pc):
            return out_ref.at[pl.ds((4 * px + 2 * py + pc) * m_per, m_per), :]

        def copy(k, block, to, src=None):
            return pltpu.make_async_remote_copy(
                src_ref=rows(*block) if src is None else src, dst_ref=rows(*block),
                send_sem=send_sems.at[k], recv_sem=recv_sems.at[k], device_id=to, device_id_type=MESH)

        mine = pltpu.make_async_copy(x_ref, rows(*me), local_sem)
        mine.start()
        first = [copy(0, me, sibling, src=x_ref)]
        first += [copy(1 + j, me, (*chip, c), src=x_ref) for j, chip in enumerate(chips)]
        for cp in first:
            cp.start()
        passed = [copy(4 + j, (*chip, c), sibling) for j, chip in enumerate(chips)]
        for j, chip in enumerate(chips):
            copy(1 + j, (*chip, c), me).wait_recv()
            passed[j].start()
        copy(0, sibling, me).wait_recv()
        for j, chip in enumerate(chips):
            copy(4 + j, (*chip, 1 - c), me).wait_recv()
        for cp in first + passed:
            cp.wait_send()
        mine.wait()

    return pl.pallas_call(
        body, name=name,
        out_shape=jax.ShapeDtypeStruct((N_DEV * m_per, n), x_shard.dtype),
        in_specs=[pl.BlockSpec(memory_space=pltpu.VMEM)],
        out_specs=pl.BlockSpec(memory_space=pltpu.VMEM),
        scratch_shapes=[pltpu.SemaphoreType.DMA((7,)), pltpu.SemaphoreType.DMA((7,)), pltpu.SemaphoreType.DMA],
    )(x_shard)


def _gather_flat(vec, name):
    L = vec.shape[0]
    Lp = -(-L // 1024) * 1024
    g = _all_gather_small(jnp.pad(vec, (0, Lp - L)).reshape(8, Lp // 8), name)
    return g.reshape(N_DEV, Lp)[:, :L]


class _Item:
    def __init__(self, kind, rows, cols, arg, layer, swap=False):
        self.kind, self.rows, self.cols, self.arg, self.layer, self.swap = kind, rows, cols, arg, layer, swap

    def ref(self, refs):
        return refs[self.arg].at[self.layer]

    def pos(self, j):
        return 2 * (j % 2) + j // 2 if self.swap else j


def _block(ref, it, j, half):
    if it.kind == "col":
        ns = it.cols // N_CHIP
        return ref.at[pl.ds(half * (it.rows // 2), it.rows // 2), pl.ds(it.pos(j) * ns, ns)]
    rs = it.rows // N_CHIP
    return ref.at[pl.ds(j * rs + half * (rs // 2), rs // 2), :]


def _cast_place(w, layer, kind, pos, name):
    _, r, n = w.shape
    tr = _pick(r, max(16, (1 << 20) // n // 16 * 16), 16)
    bpr = r // tr

    def body(s_ref, w_ref, o_ref):
        o_ref[...] = w_ref[...].astype(BF16)

    if kind == "col":
        full, out_idx = (1, r, N_CHIP * n), (lambda i, s: (0, i, s[0]))
    else:
        full, out_idx = (1, N_CHIP * r, n), (lambda i, s: (0, s[0] * bpr + i, 0))
    return pl.pallas_call(
        body, name=name,
        grid_spec=pltpu.PrefetchScalarGridSpec(
            num_scalar_prefetch=1, grid=(bpr,),
            in_specs=[pl.BlockSpec((None, tr, n), lambda i, s: (layer, i, 0))],
            out_specs=pl.BlockSpec((None, tr, n), out_idx)),
        out_shape=jax.ShapeDtypeStruct(full, BF16),
        compiler_params=_params("parallel"),
    )(jnp.reshape(pos, (1,)).astype(jnp.int32), w)


HBM_SPEC = pl.BlockSpec(memory_space=pltpu.HBM)
SEM_SPEC = pl.BlockSpec(memory_space=pltpu.SEMAPHORE)
ANY_SPEC = pl.BlockSpec(memory_space=pl.ANY)
SPLIT_PARAMS = dict(has_side_effects=pltpu.SideEffectType.DATAFLOW_SIDE_EFFECTING)


def _in_hbm(a):
    return pltpu.with_memory_space_constraint(a, pltpu.HBM)


def _split_start(copies_of, bufs, n_sem, after, name):
    n = len(bufs)

    def body(*refs):
        ins, send, recv, token = refs[:n], refs[n + 1], refs[n + 2], refs[2 * n + 3]
        for cp in copies_of(ins, send, recv, False)[0]:
            cp.start()
        token[...] = jnp.zeros_like(token)

    outs = pl.pallas_call(
        body, name=name,
        out_shape=(pltpu.SemaphoreType.DMA(n_sem), pltpu.SemaphoreType.DMA(n_sem),
                   *[pltpu.HBM(b.shape, b.dtype) for b in bufs], jax.ShapeDtypeStruct((8, 128), F32)),
        in_specs=[HBM_SPEC] * n + [ANY_SPEC],
        out_specs=(SEM_SPEC, SEM_SPEC, *[HBM_SPEC] * n, pl.BlockSpec(memory_space=pltpu.VMEM)),
        input_output_aliases={t: 2 + t for t in range(n)},
        compiler_params=pltpu.CompilerParams(**SPLIT_PARAMS),
    )(*[_in_hbm(b) for b in bufs], after)
    return outs[0], outs[1], list(outs[2:2 + n]), outs[2 + n]


def _split_wait(copies_of, send, recv, bufs, after, name):
    n = len(bufs)

    def body(*refs):
        ins, send_ref, recv_ref = refs[:n], refs[n], refs[n + 1]
        sends, arrivals = copies_of(ins, send_ref, recv_ref, True)
        for cp in sends:
            cp.wait_send()
        for cp in arrivals:
            cp.wait_recv()

    return pl.pallas_call(
        body, name=name,
        out_shape=[pltpu.HBM(b.shape, b.dtype) for b in bufs],
        in_specs=[HBM_SPEC] * n + [SEM_SPEC, SEM_SPEC, ANY_SPEC],
        out_specs=[HBM_SPEC] * n,
        input_output_aliases={t: t for t in range(n)},
        compiler_params=pltpu.CompilerParams(**SPLIT_PARAMS),
    )(*bufs, send, recv, after)


def _gather_copies(items):
    def copies_of(refs, send, recv, with_arrivals):
        x, y, c, chips = _place()
        me = _chip_index(x, y)
        sends, arrivals = [], []
        for t, it in enumerate(items):
            for k, chip in enumerate(chips):
                for core in range(2):
                    mine = _block(it.ref(refs), it, me, c)
                    sends.append(pltpu.make_async_remote_copy(
                        src_ref=mine, dst_ref=mine, send_sem=send.at[6 * t + 2 * k + core],
                        recv_sem=recv.at[6 * t + 2 * k + c], device_id=(*chip, core), device_id_type=MESH))
                    if with_arrivals:
                        landed = _block(it.ref(refs), it, _chip_index(*chip), core)
                        arrivals.append(pltpu.make_async_remote_copy(
                            src_ref=landed, dst_ref=landed, send_sem=send.at[6 * t + 2 * k + core],
                            recv_sem=recv.at[6 * t + 2 * k + core], device_id=(*chip, core), device_id_type=MESH))
        return sends, arrivals

    return copies_of


def _owner_copies(items):
    n = len(items)

    def blk(ref, it, j):
        if it.kind == "col":
            ns = it.cols // N_CHIP
            return ref.at[:, pl.ds(it.pos(j) * ns, ns)]
        return ref.at[j]

    def copies_of(refs, send, recv, with_arrivals):
        x, y, c, chips = _place()
        sends, arrivals = [], []
        for t, it in enumerate(items):
            for k, chip in enumerate(chips):
                slot = refs[n + t].at[k]
                sends.append(pltpu.make_async_remote_copy(
                    src_ref=blk(refs[t], it, _chip_index(*chip)), dst_ref=slot, send_sem=send.at[3 * t + k],
                    recv_sem=recv.at[3 * t + k], device_id=(*chip, c), device_id_type=MESH))
                if with_arrivals:
                    arrivals.append(pltpu.make_async_remote_copy(
                        src_ref=slot, dst_ref=slot, send_sem=send.at[3 * t + k], recv_sem=recv.at[3 * t + k],
                        device_id=(*chip, c), device_id_type=MESH))
        return sends, arrivals

    return copies_of


def _owner_slot_shape(it):
    if it.kind == "col":
        return (3, it.rows // 2, it.cols // N_CHIP)
    return (3, it.rows // (2 * N_CHIP), it.cols)


def _pair_view(g, it):
    if it.kind == "col":
        return g.reshape(1, 2, it.rows // 2, it.cols)
    return g.reshape(N_CHIP, 2, it.rows // (2 * N_CHIP), it.cols)


def _pair_copies(n):
    def copies_of(refs, send, recv, with_arrivals):
        x, y, c, _ = _place()
        sends, arrivals = [], []
        for t in range(n):
            land = refs[n + t]
            sends.append(pltpu.make_async_remote_copy(
                src_ref=refs[t].at[:, pl.ds(1 - c, 1)], dst_ref=land, send_sem=send.at[t], recv_sem=recv.at[t],
                device_id=(x, y, 1 - c), device_id_type=MESH))
            if with_arrivals:
                arrivals.append(pltpu.make_async_remote_copy(
                    src_ref=land, dst_ref=land, send_sem=send.at[t], recv_sem=recv.at[t],
                    device_id=(x, y, 1 - c), device_id_type=MESH))
        return sends, arrivals

    return copies_of


def _half_copies(n):
    def copies_of(refs, send, recv, with_arrivals):
        x, y, c, _ = _place()
        sends, arrivals = [], []
        for t in range(n):
            r2 = refs[t].shape[0] // 2
            mine = refs[t].at[pl.ds(c * r2, r2), :]
            sends.append(pltpu.make_async_remote_copy(
                src_ref=mine, dst_ref=mine, send_sem=send.at[t], recv_sem=recv.at[t],
                device_id=(x, y, 1 - c), device_id_type=MESH))
            if with_arrivals:
                theirs = refs[t].at[pl.ds((1 - c) * r2, r2), :]
                arrivals.append(pltpu.make_async_remote_copy(
                    src_ref=theirs, dst_ref=theirs, send_sem=send.at[t], recv_sem=recv.at[t],
                    device_id=(x, y, 1 - c), device_id_type=MESH))
        return sends, arrivals

    return copies_of


class _Reduction:
    pass


def _pair_start(grads, items, after, tag):
    n = len(items)
    views = [_pair_view(g, it) for g, it in zip(grads, items)]
    lands = [lax.empty((v.shape[0], 1) + v.shape[2:], v.dtype) for v in views]
    r = _Reduction()
    r.items, r.tag = items, tag
    r.send, r.recv, r.bufs, r.token = _split_start(_pair_copies(n), views + lands, (n,), after, f"rs_pair_start_{tag}")
    return r


def _owner_start(r, after):
    x, y, c, _ = _place()
    n = len(r.items)
    bufs = _split_wait(_pair_copies(n), r.send, r.recv, r.bufs, after, f"rs_pair_wait_{r.tag}")
    pairs = [_pair_sum(bufs[t], bufs[n + t], c, f"rs_pair_sum_{r.tag}_{t}") for t in range(n)]
    shaped = [p if it.kind == "col" else p.reshape(N_CHIP, p.shape[0] // N_CHIP, p.shape[1])
              for p, it in zip(pairs, r.items)]
    lands = [lax.empty(_owner_slot_shape(it), BF16) for it in r.items]
    r.send, r.recv, r.bufs, r.token = _split_start(
        _owner_copies(r.items), shaped + lands, (3 * n,), r.token, f"rs_owner_start_{r.tag}")
    return r


def _reduce_finish(groups, after):
    x, y, c, _ = _place()
    me = _chip_index(x, y)
    halves = []
    for r in groups:
        n = len(r.items)
        bufs = _split_wait(_owner_copies(r.items), r.send, r.recv, r.bufs, after, f"rs_owner_wait_{r.tag}")
        for t, it in enumerate(r.items):
            pair = bufs[t].reshape(-1, bufs[t].shape[-1])
            halves.append(_owner_sum(pair, bufs[n + t], me, c, it, f"rs_owner_sum_{r.tag}_{t}"))
    n = len(halves)
    return _split_start(_half_copies(n), halves, (n,), after, "rs_half_start")


def _silu(v):
    return v * jax.nn.sigmoid(v)


def _sum8(p):
    return jnp.sum(p, axis=-2)


def kernel(x, c, mod_w, mod_b, norm_g, ffn_w_in, ffn_w_out, conv_w_in, conv_k, conv_w_out, kv_mod_w, kv_mod_b, kv_norm_g, w_kv, attn_w_q, attn_w_o, rel_bias, loss_target, m_mod_w, m_mod_b, m_norm_g, m_ffn_w_in, m_ffn_w_out, m_conv_w_in, m_conv_k, m_conv_w_out, m_kv_mod_w, m_kv_mod_b, m_kv_norm_g, m_w_kv, m_attn_w_q, m_attn_w_o, m_rel_bias, v_mod_w, v_mod_b, v_norm_g, v_ffn_w_in, v_ffn_w_out, v_conv_w_in, v_conv_k, v_conv_w_out, v_kv_mod_w, v_kv_mod_b, v_kv_norm_g, v_w_kv, v_attn_w_q, v_attn_w_o, v_rel_bias):
    xi, yi, ci = lax.axis_index("x"), lax.axis_index("y"), lax.axis_index("c")
    chip = 2 * xi + yi
    dev = 2 * chip + ci
    _, S, D = x.shape
    F = ffn_w_out.shape[1] * N_CHIP
    x0 = x.reshape(S, D)
    target = loss_target.reshape(S, D)
    n_mod = mod_w.shape[2]
    n_kvm = kv_mod_w.shape[1]
    dsh = D // N_CHIP
    TF = F // 2

    c_all = _all_gather_small(c.reshape(8, D // 8), "ag_c").reshape(N_DEV, D)
    sc16 = jnp.pad(_silu(c_all), ((0, 8), (0, 0)))
    part = [_mm(sc16, mod_w, "nn", F32, f"mod_fwd_{l}", b_layer=l)[:8] for l in range(2)]
    part.append(_mm(sc16, kv_mod_w, "nn", F32, "mod_fwd_kv")[:8])
    fwd_vec = jnp.concatenate([p.reshape(-1) for p in part] + [norm_g.reshape(-1), conv_k.reshape(-1)])
    fwd_all = _gather_flat(fwd_vec, "ag_fwd_small")[0::2]
    o = 0
    mods = []
    for n in (n_mod, n_mod, n_kvm):
        blk = fwd_all[:, o:o + 8 * n].reshape(N_CHIP, 8, n)
        mods.append(lax.dynamic_index_in_dim(blk, dev, axis=1, keepdims=False).reshape(N_CHIP * n))
        o += 8 * n
    ng = fwd_all[:, o:o + 8 * dsh].reshape(N_CHIP, 2, 4, dsh).transpose(1, 2, 0, 3).reshape(2, 4, D)
    o += 8 * dsh
    ck = fwd_all[:, o:o + 3 * dsh].reshape(N_CHIP, 3, dsh).transpose(1, 0, 2).reshape(3, D)
    ck8 = jnp.pad(ck, ((0, 5), (0, 0)))
    mod = [mods[l] + mod_b[l] for l in range(2)]
    sh1, sc1, g1, sh2, sc2, g2 = zip(*[jnp.split(m, 6) for m in mod])
    kv_sh, kv_sc = jnp.split(mods[2] + kv_mod_b, 2)
    row = lambda v: v.reshape(1, D)

    it_conv = [_Item("col", D, 3 * D, 0, 0), _Item("row", D, D, 1, 0)]
    it_ffn = [_Item("col", D, 2 * F, 0, 0, swap=True), _Item("row", F, D, 1, 0)]
    it_attn = [_Item("col", D, 2 * D, 0, 0, swap=True), _Item("row", D, D, 1, 0), _Item("row", D, D, 2, 0)]

    def placed(w, layer, it, nm, after=None):
        pos = it.pos(chip)
        if after is not None:
            pos = pos + after[0, 0].astype(jnp.int32)
        return _cast_place(w, layer, it.kind, pos, f"place_{nm}")

    flying = {}

    def start(tag, its, bufs, after):
        send, recv, bufs, tok = _split_start(_gather_copies(its), bufs, (6 * len(its),), after, f"ag_start_{tag}")
        flying[tag] = (its, send, recv, bufs)
        return tok

    def arrived(tag, after):
        its, send, recv, bufs = flying[tag]
        return _split_wait(_gather_copies(its), send, recv, bufs, after, f"ag_wait_{tag}")

    one = lambda it: [_Item(it.kind, it.rows, it.cols, 0, 0, it.swap)]
    tok = start("conv_in", one(it_conv[0]), [placed(conv_w_in, 0, it_conv[0], "conv_w_in")], fwd_all)
    tok = start("conv_out", one(it_conv[1]), [placed(conv_w_out, 0, it_conv[1], "conv_w_out", tok)], tok)
    tok = start("ffn0_in", one(it_ffn[0]), [placed(ffn_w_in, 0, it_ffn[0], "ffn_w_in0", tok)], tok)
    tok = start("ffn0_out", one(it_ffn[1]), [placed(ffn_w_out, 0, it_ffn[1], "ffn_w_out0", tok)], tok)
    tok = start("attn", it_attn, [placed(w_kv[None], 0, it_attn[0], "w_kv", tok),
                                  placed(attn_w_q, 0, it_attn[1], "attn_w_q", tok),
                                  placed(attn_w_o, 0, it_attn[2], "attn_w_o", tok)], tok)
    token = start("ffn1", it_ffn, [placed(ffn_w_in, 1, it_ffn[0], "ffn_w_in1", tok),
                                   placed(ffn_w_out, 1, it_ffn[1], "ffn_w_out1", tok)], tok)

    a1 = row(ng[0, 0] * (1.0 + sc1[0])) + token[0, 0]
    (h1,) = _norm_mod(x0, a1, row(sh1[0]), "l0_norm1")
    tab = _bias_table(rel_bias[0], "l1_bias_table")
    h1, tab = lax.optimization_barrier((h1, tab))
    (W_cin,) = arrived("conv_in", h1)
    bcx = _mm(h1, W_cin, "nn", BF16, "l0_conv_in", b_layer=0)
    ug = _conv_gate(bcx, ck8, "l0_conv_gate")
    gt1 = row(g1[0] * ng[0, 1])
    a2 = row(ng[0, 2] * (1.0 + sc2[0]))
    (W_cout,) = arrived("conv_out", ug)
    y1, x1, h2 = _mm_post(ug, W_cout, x0, gt1, "l0_conv_out", scales=a2, shifts=row(sh2[0]))
    (W_fin0,) = arrived("ffn0_in", h2)
    gu0, act0 = _ffn_in_act(h2, W_fin0, 0, "l0_ffn_in")
    (W_fout0,) = arrived("ffn0_out", act0)
    gt2 = row(g2[0] * ng[0, 3])
    a3 = ng[1, 0] * (1.0 + sc1[1])
    akv = kv_norm_g * (1.0 + kv_sc)
    y2, x2, h3, hkv = _mm_post(act0, W_fout0, x1, gt2, "l0_ffn_out",
                               scales=jnp.stack([a3, akv]), shifts=jnp.stack([sh1[1], kv_sh]))
    W_kv, W_q, W_o = arrived("attn", hkv)
    kvp = _mm(hkv, W_kv, "nn", BF16, "l1_kv", b_layer=0)
    att_scale = (D // N_HEADS) ** -0.5
    assert math.log2(att_scale) % 1 == 0, "scaling q before its bf16 cast is exact only for a power of two"
    qp = _mm(h3, W_q, "nn", BF16, "l1_q", b_layer=0, scale=att_scale)
    oh = _attn_fwd(qp, kvp, tab, "l1_attn")
    gt3 = row(g1[1] * ng[1, 1])
    a4 = row(ng[1, 2] * (1.0 + sc2[1]))
    y3, x3, h4 = _mm_post(oh, W_o, x2, gt3, "l1_attn_out", scales=a4, shifts=row(sh2[1]))
    W_fin1, W_fout1 = arrived("ffn1", h4)
    gu1, act1 = _ffn_in_act(h4, W_fin1, 0, "l1_ffn_in")
    gt4 = row(g2[1] * ng[1, 3])
    dx4, sq, dy4, dgt4 = _mm_post(act1, W_fout1, x3, gt4, "l1_ffn_out", target=target)
    loss_part = 0.5 * jnp.sum(sq) / D

    def ffn_grads(dy, h, gu, act, w_out, tag):
        dgu = _ffn_out_dx_act(dy, w_out, 0, gu, f"{tag}_ffn_out_dx")
        g_fout = _mm(act, dy, "tn", BF16, f"{tag}_ffn_out_dw", tm=TF)
        g_fin = _mm(h, dgu, "tn", BF16, f"{tag}_ffn_in_dw", tn=TF)
        return dgu, g_fin, g_fout

    dgu1, G_fin1, G_fout1 = ffn_grads(dy4, h4, gu1, act1, W_fout1, "l1")
    red = [_pair_start([G_fin1, G_fout1], it_ffn, token, "ffn1")]
    dx3, ds4, db4, dy3, dgt3 = _mm_pre_bwd([(dgu1, W_fin1)], x3, dx4, a4 + red[0].token[0, 0], "l1_ffn_in_dx",
                                           post=(y3, gt3))
    doh = _mm(dy3, W_o, "nt", BF16, "l1_attn_out_dx", b_layer=0)
    G_o = _mm(oh, dy3, "tn", BF16, "l1_attn_out_dw")
    _owner_start(red[0], G_o)
    dq, dkv, dtab = _attn_bwd(qp, kvp, tab, doh, "l1_attn_bwd")
    d_rel = _bias_table_grad(dtab)
    G_q = _mm(h3, dq, "tn", BF16, "l1_q_dw")
    G_kv = _mm(hkv, dkv, "tn", BF16, "l1_kv_dw")
    red.append(_pair_start([G_kv, G_q, G_o], it_attn, red[-1].token, "attn"))
    dx2, ds3, db3, dy2, dgt2 = _mm_pre_bwd([(dq, W_q), (dkv, W_kv)], x2, dx3,
                                           jnp.stack([a3, akv]) + red[1].token[0, 0], "l1_qkv_dx", post=(y2, gt2))
    _owner_start(red[1], dx2)

    dgu0, G_fin0, G_fout0 = ffn_grads(dy2, h2, gu0, act0, W_fout0, "l0")
    red.append(_pair_start([G_fin0, G_fout0], it_ffn, red[-1].token, "ffn0"))
    dx1, ds2, db2, dy1, dgt1 = _mm_pre_bwd([(dgu0, W_fin0)], x1, dx2, a2 + red[2].token[0, 0], "l0_ffn_in_dx",
                                           post=(y1, gt1))
    _owner_start(red[2], dy1)
    dug = _mm(dy1, W_cout, "nt", BF16, "l0_conv_out_dx", b_layer=0)
    G_cout = _mm(ug, dy1, "tn", BF16, "l0_conv_out_dw")
    dbcx, dck = _conv_gate_bwd(dug, bcx, ck8, "l0_conv_gate_bwd")
    G_cin = _mm(h1, dbcx, "tn", BF16, "l0_conv_in_dw")
    red.append(_pair_start([G_cin, G_cout], it_conv, red[-1].token, "conv"))
    dx0, ds1, db1 = _mm_pre_bwd([(dbcx, W_cin)], x0, dx1, a1 + red[3].token[0, 0], "l0_conv_in_dx")
    ds1, db1 = _sum8(ds1)[0], _sum8(db1)[0]
    da2, db2 = _sum8(ds2)[0], _sum8(db2)[0]
    ds3, db3 = _sum8(ds3), _sum8(db3)
    da4, db4 = _sum8(ds4)[0], _sum8(db4)[0]
    dgt1, dgt2, dgt3, dgt4 = _sum8(dgt1), _sum8(dgt2), _sum8(dgt3), _sum8(dgt4)

    def dmod_of(l, ds_a, db_a, dgt_a, ds_b, db_b, dgt_b):
        return jnp.concatenate([db_a, ds_a * ng[l, 0], dgt_a * ng[l, 1], db_b, ds_b * ng[l, 2], dgt_b * ng[l, 3]])

    dmod0 = dmod_of(0, ds1, db1, dgt1, da2, db2, dgt2)
    dmod1 = dmod_of(1, ds3[0], db3[0], dgt3, da4, db4, dgt4)
    dkvmod = jnp.concatenate([db3[1], ds3[1] * kv_norm_g])
    dng = jnp.stack([
        jnp.stack([ds1 * (1.0 + sc1[0]), dgt1 * g1[0], da2 * (1.0 + sc2[0]), dgt2 * g2[0]]),
        jnp.stack([ds3[0] * (1.0 + sc1[1]), dgt3 * g1[1], da4 * (1.0 + sc2[1]), dgt4 * g2[1]])])
    dkvng = ds3[1] * (1.0 + kv_sc)
    small = [dmod0, dmod1, dkvmod, dng.reshape(-1), dkvng, _sum8(dck).reshape(-1), d_rel.reshape(-1),
             loss_part.reshape(1)]
    sizes = [int(s.shape[0]) for s in small]
    offs = np.concatenate([[0], np.cumsum(sizes)])
    bwd_all = _gather_flat(jnp.concatenate(small), "ag_bwd_small")
    _owner_start(red[3], bwd_all)
    Lb = bwd_all.shape[1]
    Lp = -(-Lb // 128) * 128
    tot = _sum_rows(jnp.pad(bwd_all, ((0, 0), (0, Lp - Lb))), "sum_small")[0]
    seg = lambda i: tot[offs[i]:offs[i + 1]]
    g_mod_b = jnp.stack([seg(0), seg(1)])
    g_kv_mod_b = seg(2)
    g_norm_g = lax.dynamic_slice_in_dim(seg(3).reshape(2, 4, D), chip * dsh, dsh, axis=2)
    g_kv_norm_g = seg(4)
    g_conv_k = lax.dynamic_slice_in_dim(seg(5).reshape(1, 3, D), chip * dsh, dsh, axis=2)
    g_rel_bias = seg(6).reshape(rel_bias.shape)
    loss = seg(7)[0]

    def dmod_w(i, n, name):
        rows_ = lax.dynamic_slice_in_dim(bwd_all[:, offs[i]:offs[i + 1]], chip * n, n, axis=1)
        return _mm(sc16, jnp.pad(rows_, ((0, 8), (0, 0))), "tn", F32, name)

    g_mod_w = jnp.stack([dmod_w(0, n_mod, "mod_bwd_0"), dmod_w(1, n_mod, "mod_bwd_1")])
    g_kv_mod_w = dmod_w(2, n_kvm, "mod_bwd_kv")

    grads = {
        "mod_w": g_mod_w, "mod_b": g_mod_b, "norm_g": g_norm_g, "conv_k": g_conv_k,
        "kv_mod_w": g_kv_mod_w, "kv_mod_b": g_kv_mod_b, "kv_norm_g": g_kv_norm_g, "rel_bias": g_rel_bias,
    }
    weights = dict(mod_w=mod_w, mod_b=mod_b, norm_g=norm_g, ffn_w_in=ffn_w_in, ffn_w_out=ffn_w_out,
                   conv_w_in=conv_w_in, conv_k=conv_k, conv_w_out=conv_w_out, kv_mod_w=kv_mod_w,
                   kv_mod_b=kv_mod_b, kv_norm_g=kv_norm_g, w_kv=w_kv, attn_w_q=attn_w_q, attn_w_o=attn_w_o,
                   rel_bias=rel_bias)
    m_in = dict(mod_w=m_mod_w, mod_b=m_mod_b, norm_g=m_norm_g, ffn_w_in=m_ffn_w_in, ffn_w_out=m_ffn_w_out,
                conv_w_in=m_conv_w_in, conv_k=m_conv_k, conv_w_out=m_conv_w_out, kv_mod_w=m_kv_mod_w,
                kv_mod_b=m_kv_mod_b, kv_norm_g=m_kv_norm_g, w_kv=m_w_kv, attn_w_q=m_attn_w_q,
                attn_w_o=m_attn_w_o, rel_bias=m_rel_bias)
    v_in = dict(mod_w=v_mod_w, mod_b=v_mod_b, norm_g=v_norm_g, ffn_w_in=v_ffn_w_in, ffn_w_out=v_ffn_w_out,
                conv_w_in=v_conv_w_in, conv_k=v_conv_k, conv_w_out=v_conv_w_out, kv_mod_w=v_kv_mod_w,
                kv_mod_b=v_kv_mod_b, kv_norm_g=v_kv_norm_g, w_kv=v_w_kv, attn_w_q=v_attn_w_q,
                attn_w_o=v_attn_w_o, rel_bias=v_rel_bias)
    names = list(weights)
    step = {}

    def update(n):
        g = grads[n].reshape(weights[n].shape)
        step[n] = (g, *_adamw(weights[n], g, m_in[n], v_in[n], f"adamw_{n}"))

    update("mod_w")
    half_send, half_recv, half_bufs, _ = _reduce_finish(red, step["mod_w"][1])
    for n in list(grads):
        if n not in step:
            update(n)
    r_fin1, r_fout1, r_kv, r_q, r_o, r_fin0, r_fout0, r_cin, r_cout = _split_wait(
        _half_copies(len(half_bufs)), half_send, half_recv, half_bufs, step["kv_mod_w"][1], "rs_half_wait")
    grads.update({
        "ffn_w_in": jnp.stack([r_fin0, r_fin1]), "ffn_w_out": jnp.stack([r_fout0, r_fout1]),
        "conv_w_in": r_cin[None], "conv_w_out": r_cout[None], "w_kv": r_kv,
        "attn_w_q": r_q[None], "attn_w_o": r_o[None],
    })
    for n in names:
        if n not in step:
            update(n)
    return (loss, dx0.reshape(x.shape), *[step[n][k] for k in range(4) for n in names])
```

```python
import functools
import math

import numpy as np
import jax
import jax.numpy as jnp
from jax import lax
from jax.experimental import pallas as pl
from jax.experimental.pallas import tpu as pltpu

CHUNK = 64
N_LEFT_CHUNKS = 8
N_HEADS = 16
MAX_REL = 2 * CHUNK
N_REL = 2 * MAX_REL + 1
EPS = 1e-6
ADAM_LR = 0.001
ADAM_B1 = 0.9
ADAM_B2 = 0.999
ADAM_EPS = 1e-08
ADAM_WD = 0.01
ADAM_STEP = 10

Q_CHUNKS = 4
BQ = Q_CHUNKS * CHUNK
N_WIN = 1 + N_LEFT_CHUNKS // Q_CHUNKS
HEADS_PER_STEP = 8
NEG = -1e30
N_DEV = 8
N_CHIP = 4
SMALL_TENSOR_ELEMS = 1 << 16

BF16 = jnp.bfloat16
F32 = jnp.float32
V7X_VMEM_LIMIT_BYTES = 56 * 1024 * 1024
MESH = pl.DeviceIdType.MESH


def _pick(n, pref, align):
    t = min(pref, n)
    t -= t % align
    while t >= align:
        if n % t == 0:
            return t
        t -= align
    return n


def _params(*sem):
    return pltpu.CompilerParams(dimension_semantics=sem, vmem_limit_bytes=V7X_VMEM_LIMIT_BYTES)


def _colsum8(v):
    r, d = v.shape
    return v.reshape(r // 8, 8, d).sum(axis=0)


_DIMS = {"nn": (((1,), (0,)), ((), ())), "nt": (((1,), (1,)), ((), ())), "tn": (((0,), (0,)), ((), ()))}


def _mm(a, b, mode, out_dtype, name, *, b_layer=None, tm=1024, tn=1024, tk=None, scale=None):
    if tk is None:
        tk = 2048 if mode == "tn" else 3072
    bs = b.shape[1:] if b_layer is not None else b.shape
    if mode == "nn":
        (M, K), (K2, N) = a.shape, bs
    elif mode == "nt":
        (M, K), (N, K2) = a.shape, bs
    else:
        (K, M), (K2, N) = a.shape, bs
    assert K == K2, (name, a.shape, b.shape)
    tm = _pick(M, tm, 128 if mode == "tn" else 16)
    tn = _pick(N, tn, 128)
    tk = _pick(K, tk, 128 if mode != "tn" else 16)
    nk = K // tk
    assert scale is None or nk == 1, name
    dims = _DIMS[mode]

    def body(a_ref, b_ref, o_ref, *acc):
        p = lax.dot_general(a_ref[...].astype(BF16), b_ref[...].astype(BF16), dims,
                            preferred_element_type=F32)
        if nk == 1:
            o_ref[...] = (p if scale is None else p * scale).astype(o_ref.dtype)
        else:
            k = pl.program_id(2)

            @pl.when(k == 0)
            def _():
                acc[0][...] = p

            @pl.when(k > 0)
            def _():
                acc[0][...] += p

            @pl.when(k == nk - 1)
            def _():
                o_ref[...] = acc[0][...].astype(o_ref.dtype)

    a_spec = (pl.BlockSpec((tk, tm), lambda i, j, k: (k, i)) if mode == "tn"
              else pl.BlockSpec((tm, tk), lambda i, j, k: (i, k)))
    if mode == "nt":
        b_blk, b_idx = (tn, tk), (lambda i, j, k: (j, k))
    else:
        b_blk, b_idx = (tk, tn), (lambda i, j, k: (k, j))
    if b_layer is not None:
        b_spec = pl.BlockSpec((None,) + b_blk, lambda i, j, k: (b_layer,) + b_idx(i, j, k))
    else:
        b_spec = pl.BlockSpec(b_blk, b_idx)
    return pl.pallas_call(
        body, name=name,
        grid=(M // tm, N // tn, nk),
        in_specs=[a_spec, b_spec],
        out_specs=pl.BlockSpec((tm, tn), lambda i, j, k: (i, j)),
        out_shape=jax.ShapeDtypeStruct((M, N), out_dtype),
        scratch_shapes=[pltpu.VMEM((tm, tn), F32)] if nk > 1 else [],
        compiler_params=_params("parallel", "parallel", "arbitrary"),
    )(a, b)


def _row_spec(tm, d):
    return pl.BlockSpec((tm, d), lambda i: (i, 0))


def _vec_spec(r, d):
    return pl.BlockSpec((r, d), lambda i: (0, 0))


def _norm_mod(x, scales, shifts, name):
    S, D = x.shape
    nb = scales.shape[0]
    tm = _pick(S, 512, 16)

    def body(x_ref, a_ref, b_ref, *o_refs):
        xv = x_ref[...]
        xh = xv * lax.rsqrt(jnp.mean(xv * xv, axis=-1, keepdims=True) + EPS)
        for n in range(nb):
            o_refs[n][...] = (xh * a_ref[n:n + 1, :] + b_ref[n:n + 1, :]).astype(BF16)

    return pl.pallas_call(
        body, name=name, grid=(S // tm,),
        in_specs=[_row_spec(tm, D), _vec_spec(nb, D), _vec_spec(nb, D)],
        out_specs=[_row_spec(tm, D)] * nb,
        out_shape=[jax.ShapeDtypeStruct((S, D), BF16)] * nb,
        compiler_params=_params("parallel"),
    )(x, scales, shifts)


def _mm_post(a, w, x, gate, name, *, scales=None, shifts=None, target=None):
    M, K = a.shape
    D = w.shape[2]
    tm = _pick(M, 512, 16)
    sub = _pick(tm, 256, 16)
    nb = 0 if scales is None else scales.shape[0]

    def body(a_ref, w_ref, x_ref, g_ref, *rest):
        if target is None:
            sc_ref, sh_ref, y_ref, xn_ref = rest[:4]
            h_refs = rest[4:]
        else:
            t_ref, dx_ref, sq_ref, dy_ref, dg_ref = rest

            @pl.when(pl.program_id(0) == 0)
            def _():
                sq_ref[...] = jnp.zeros_like(sq_ref)
                dg_ref[...] = jnp.zeros_like(dg_ref)

        for r in range(tm // sub):
            rows = pl.ds(r * sub, sub)
            yb = jnp.dot(a_ref[rows, :], w_ref[...], preferred_element_type=F32).astype(BF16)
            yv = yb.astype(F32)
            yh = yv * lax.rsqrt(jnp.mean(yv * yv, axis=-1, keepdims=True) + EPS)
            xn = x_ref[rows, :] + yh * g_ref[...]
            if target is None:
                y_ref[rows, :] = yb
                xn_ref[rows, :] = xn
                xh = xn * lax.rsqrt(jnp.mean(xn * xn, axis=-1, keepdims=True) + EPS)
                for n in range(nb):
                    h_refs[n][rows, :] = (xh * sc_ref[n:n + 1, :] + sh_ref[n:n + 1, :]).astype(BF16)
            else:
                e = xn - t_ref[rows, :]
                dx = e / D
                dx_ref[rows, :] = dx
                sq_ref[...] += _colsum8(e * e)
                dy, dxy = _post_norm_grad(dx, yb, g_ref[...])
                dy_ref[rows, :] = dy.astype(BF16)
                dg_ref[...] += _colsum8(dxy)

    ins = [a, w, x, gate]
    in_specs = [_row_spec(tm, K), pl.BlockSpec((None, K, D), lambda i: (0, 0, 0)), _row_spec(tm, D), _vec_spec(1, D)]
    if target is None:
        ins += [scales, shifts]
        in_specs += [_vec_spec(nb, D), _vec_spec(nb, D)]
        out_specs = [_row_spec(tm, D)] * (2 + nb)
        out_shape = [jax.ShapeDtypeStruct((M, D), BF16), jax.ShapeDtypeStruct((M, D), F32)] \
            + [jax.ShapeDtypeStruct((M, D), BF16)] * nb
    else:
        ins += [target]
        in_specs += [_row_spec(tm, D)]
        out_specs = [_row_spec(tm, D), _vec_spec(8, D), _row_spec(tm, D), _vec_spec(8, D)]
        out_shape = [jax.ShapeDtypeStruct((M, D), F32), jax.ShapeDtypeStruct((8, D), F32),
                     jax.ShapeDtypeStruct((M, D), BF16), jax.ShapeDtypeStruct((8, D), F32)]
    return pl.pallas_call(
        body, name=name, grid=(M // tm,), in_specs=in_specs, out_specs=out_specs, out_shape=out_shape,
        compiler_params=_params("arbitrary" if target is not None else "parallel"),
    )(*ins)


def _post_norm_grad(dxn, yb, gate):
    yv = yb.astype(F32)
    r = lax.rsqrt(jnp.mean(yv * yv, axis=-1, keepdims=True) + EPS)
    yh = yv * r
    dyh = dxn * gate
    return r * (dyh - yh * jnp.mean(dyh * yh, axis=-1, keepdims=True)), dxn * yh


def _mm_pre_bwd(pairs, x, dxn, scales, name, post=None):
    S, D = x.shape
    nb = len(pairs)
    tm = _pick(S, 512, 16)
    sub = _pick(tm, 256, 16)

    def body(*refs):
        a_refs, w_refs = refs[0:2 * nb:2], refs[1:2 * nb:2]
        x_ref, d_ref, sc_ref = refs[2 * nb:2 * nb + 3]
        rest = refs[2 * nb + 3:]
        if post is not None:
            y_ref, g_ref, dx_ref, ds_ref, db_ref, dy_ref, dg_ref = rest
        else:
            dx_ref, ds_ref, db_ref = rest

        @pl.when(pl.program_id(0) == 0)
        def _():
            ds_ref[...] = jnp.zeros_like(ds_ref)
            db_ref[...] = jnp.zeros_like(db_ref)
            if post is not None:
                dg_ref[...] = jnp.zeros_like(dg_ref)

        for r in range(tm // sub):
            rows = pl.ds(r * sub, sub)
            xv = x_ref[rows, :]
            rr = lax.rsqrt(jnp.mean(xv * xv, axis=-1, keepdims=True) + EPS)
            xh = xv * rr
            dxh = jnp.zeros_like(xv)
            for n in range(nb):
                dh = lax.dot_general(a_refs[n][rows, :], w_refs[n][...], _DIMS["nt"], preferred_element_type=F32)
                dxh = dxh + dh * sc_ref[n:n + 1, :]
                ds_ref[n] += _colsum8(dh * xh)
                db_ref[n] += _colsum8(dh)
            dx = d_ref[rows, :] + rr * (dxh - xh * jnp.mean(dxh * xh, axis=-1, keepdims=True))
            dx_ref[rows, :] = dx
            if post is not None:
                dy, dxy = _post_norm_grad(dx, y_ref[rows, :], g_ref[...])
                dy_ref[rows, :] = dy.astype(BF16)
                dg_ref[...] += _colsum8(dxy)

    ins, in_specs = [], []
    for a, w in pairs:
        ins += [a, w]
        in_specs += [_row_spec(tm, a.shape[1]),
                     pl.BlockSpec((None, D, a.shape[1]), lambda i: (0, 0, 0), pipeline_mode=pl.Buffered(1))]
    ins += [x, dxn, scales]
    in_specs += [_row_spec(tm, D), _row_spec(tm, D), _vec_spec(nb, D)]
    acc_spec = pl.BlockSpec((nb, 8, D), lambda i: (0, 0, 0))
    out_specs = [_row_spec(tm, D), acc_spec, acc_spec]
    out_shape = [jax.ShapeDtypeStruct((S, D), F32), jax.ShapeDtypeStruct((nb, 8, D), F32),
                 jax.ShapeDtypeStruct((nb, 8, D), F32)]
    if post is not None:
        ins += list(post)
        in_specs += [_row_spec(tm, D), _vec_spec(1, D)]
        out_specs += [_row_spec(tm, D), _vec_spec(8, D)]
        out_shape += [jax.ShapeDtypeStruct((S, D), BF16), jax.ShapeDtypeStruct((8, D), F32)]
    return pl.pallas_call(
        body, name=name, grid=(S // tm,), in_specs=in_specs, out_specs=out_specs, out_shape=out_shape,
        compiler_params=_params("arbitrary"),
    )(*ins)


FFN_PAIRS = 2
FFN_SUB_ROWS = 256


def _ffn_in_act(h, w, layer, name):
    S, D = h.shape
    F2 = w.shape[2]
    PW = F2 // (2 * FFN_PAIRS)
    tm = _pick(S, 512, 16)
    sub = _pick(tm, FFN_SUB_ROWS, 16)

    def body(h_ref, w_ref, gu_ref, a_ref):
        for r in range(tm // sub):
            rows = pl.ds(r * sub, sub)
            acc = jnp.dot(h_ref[rows, :], w_ref[...], preferred_element_type=F32)
            gu_ref[rows, :] = acc.astype(BF16)
            g = acc[:, :PW]
            a_ref[rows, :] = (g * jax.nn.sigmoid(g) * acc[:, PW:]).astype(BF16)

    return pl.pallas_call(
        body, name=name, grid=(FFN_PAIRS, S // tm),
        in_specs=[pl.BlockSpec((tm, D), lambda p, i: (i, 0)),
                  pl.BlockSpec((None, D, 2 * PW), lambda p, i: (layer, 0, p))],
        out_specs=[pl.BlockSpec((tm, 2 * PW), lambda p, i: (i, p)), pl.BlockSpec((tm, PW), lambda p, i: (i, p))],
        out_shape=[jax.ShapeDtypeStruct((S, F2), BF16), jax.ShapeDtypeStruct((S, F2 // 2), BF16)],
        compiler_params=_params("parallel", "parallel"),
    )(h, w)


def _ffn_out_dx_act(dy, w, layer, gu, name):
    S, D = dy.shape
    F2 = gu.shape[1]
    PW = F2 // (2 * FFN_PAIRS)
    tm = _pick(S, 512, 16)
    sub = _pick(tm, FFN_SUB_ROWS, 16)

    def body(dy_ref, w_ref, gu_ref, o_ref):
        for r in range(tm // sub):
            rows = pl.ds(r * sub, sub)
            da = lax.dot_general(dy_ref[rows, :], w_ref[...], _DIMS["nt"], preferred_element_type=F32)
            g = gu_ref[rows, 0:PW].astype(F32)
            u = gu_ref[rows, PW:2 * PW].astype(F32)
            sg = jax.nn.sigmoid(g)
            o_ref[rows, 0:PW] = (da * u * (sg * (1.0 + g * (1.0 - sg)))).astype(BF16)
            o_ref[rows, PW:2 * PW] = (da * (g * sg)).astype(BF16)

    return pl.pallas_call(
        body, name=name, grid=(FFN_PAIRS, S // tm),
        in_specs=[pl.BlockSpec((tm, D), lambda p, i: (i, 0)),
                  pl.BlockSpec((None, PW, D), lambda p, i: (layer, p, 0)),
                  pl.BlockSpec((tm, 2 * PW), lambda p, i: (i, p))],
        out_specs=pl.BlockSpec((tm, 2 * PW), lambda p, i: (i, p)),
        out_shape=jax.ShapeDtypeStruct((S, F2), BF16),
        compiler_params=_params("parallel", "parallel"),
    )(dy, w, gu)


HALO = 16


def _conv_terms(bcx_ref, prev_ref, i, tm, D):
    b = bcx_ref[:, 0:D].astype(F32)
    cg = bcx_ref[:, D:2 * D].astype(F32)
    xin = bcx_ref[:, 2 * D:3 * D].astype(F32)
    z = cg * xin
    zp = prev_ref[:, D:2 * D].astype(F32) * prev_ref[:, 2 * D:3 * D].astype(F32)
    zp = jnp.where(i > 0, zp, 0.0)
    z_ext = jnp.concatenate([zp, z], axis=0)
    z1 = pltpu.roll(z_ext, 1, 0)[HALO:, :]
    z2 = pltpu.roll(z_ext, 2, 0)[HALO:, :]
    return b, cg, xin, z, z1, z2


def _conv_gate(bcx, ck, name):
    S, D3 = bcx.shape
    D = D3 // 3
    tm = _pick(S, 256, 16)
    hb = tm // HALO

    def body(bcx_ref, prev_ref, ck_ref, o_ref):
        i = pl.program_id(0)
        b, _, _, z, z1, z2 = _conv_terms(bcx_ref, prev_ref, i, tm, D)
        conv = ck_ref[0:1, :] * z2 + ck_ref[1:2, :] * z1 + ck_ref[2:3, :] * z
        o_ref[...] = (b * conv).astype(BF16)

    return pl.pallas_call(
        body, name=name, grid=(S // tm,),
        in_specs=[_row_spec(tm, D3),
                  pl.BlockSpec((HALO, D3), lambda i: (jnp.maximum(i * hb - 1, 0), 0)),
                  _vec_spec(8, D)],
        out_specs=_row_spec(tm, D),
        out_shape=jax.ShapeDtypeStruct((S, D), BF16),
        compiler_params=_params("parallel"),
    )(bcx, bcx, ck)


def _conv_gate_bwd(du, bcx, ck, name):
    S, D3 = bcx.shape
    D = D3 // 3
    tm = _pick(S, 256, 16)
    hb = tm // HALO
    nt = S // tm

    def body(du_ref, dun_ref, bcx_ref, prev_ref, next_ref, ck_ref, o_ref, dk_ref):
        i = pl.program_id(0)
        b, cg, xin, z, z1, z2 = _conv_terms(bcx_ref, prev_ref, i, tm, D)
        k0, k1, k2 = ck_ref[0:1, :], ck_ref[1:2, :], ck_ref[2:3, :]
        conv = k0 * z2 + k1 * z1 + k2 * z
        d = du_ref[...].astype(F32)
        dconv = d * b
        dcn = jnp.where(i < nt - 1, dun_ref[...].astype(F32) * next_ref[:, 0:D].astype(F32), 0.0)
        d_ext = jnp.concatenate([dconv, dcn], axis=0)
        d1 = pltpu.roll(d_ext, tm + HALO - 1, 0)[:tm, :]
        d2 = pltpu.roll(d_ext, tm + HALO - 2, 0)[:tm, :]
        dz = k2 * dconv + k1 * d1 + k0 * d2
        o_ref[:, 0:D] = (d * conv).astype(BF16)
        o_ref[:, D:2 * D] = (dz * xin).astype(BF16)
        o_ref[:, 2 * D:3 * D] = (dz * cg).astype(BF16)

        @pl.when(i == 0)
        def _():
            dk_ref[...] = jnp.zeros_like(dk_ref)

        dk_ref[0] += _colsum8(dconv * z2)
        dk_ref[1] += _colsum8(dconv * z1)
        dk_ref[2] += _colsum8(dconv * z)

    last = S // HALO - 1
    return pl.pallas_call(
        body, name=name, grid=(nt,),
        in_specs=[_row_spec(tm, D),
                  pl.BlockSpec((HALO, D), lambda i: (jnp.minimum((i + 1) * hb, last), 0)),
                  _row_spec(tm, D3),
                  pl.BlockSpec((HALO, D3), lambda i: (jnp.maximum(i * hb - 1, 0), 0)),
                  pl.BlockSpec((HALO, D3), lambda i: (jnp.minimum((i + 1) * hb, last), 0)),
                  _vec_spec(8, D)],
        out_specs=[_row_spec(tm, D3), pl.BlockSpec((3, 8, D), lambda i: (0, 0, 0))],
        out_shape=[jax.ShapeDtypeStruct((S, D3), BF16), jax.ShapeDtypeStruct((3, 8, D), F32)],
        compiler_params=_params("arbitrary"),
    )(du, du, bcx, bcx, bcx, ck)


def _rel_onehot():
    a = np.arange(CHUNK)[:, None]
    b = np.arange(CHUNK)[None, :]
    idx = np.stack([np.clip((N_LEFT_CHUNKS - dl) * CHUNK + a - b, -MAX_REL, MAX_REL) + MAX_REL
                    for dl in (6, 7, 8)]).reshape(-1)
    return (jnp.asarray(idx)[:, None] == jnp.arange(N_REL)[None, :]).astype(F32)


def _bias_table(rel_bias, name):
    H = rel_bias.shape[0]
    near = jnp.dot(rel_bias, _rel_onehot().T, precision=lax.Precision.HIGHEST).reshape(H, 3, CHUNK, CHUNK)
    far = jnp.broadcast_to(rel_bias[:, N_REL - 1][:, None, None], (H, CHUNK, CHUNK))

    def body(near_ref, far_ref, o_ref):
        neg = jnp.full((CHUNK, CHUNK), NEG, F32)
        for v in range(N_WIN):
            for ic in range(Q_CHUNKS):
                for jc in range(N_WIN * Q_CHUNKS):
                    dl = jc - ic
                    if dl < 0 or dl > N_LEFT_CHUNKS or jc < (N_WIN - 1 - v) * Q_CHUNKS:
                        blk = neg
                    else:
                        blk = far_ref[...] if dl <= 5 else near_ref[dl - 6]
                    o_ref[v, ic * CHUNK:(ic + 1) * CHUNK, jc * CHUNK:(jc + 1) * CHUNK] = blk

    return pl.pallas_call(
        body, name=name, grid=(H,),
        in_specs=[pl.BlockSpec((None, 3, CHUNK, CHUNK), lambda h: (h, 0, 0, 0)),
                  pl.BlockSpec((None, CHUNK, CHUNK), lambda h: (h, 0, 0))],
        out_specs=pl.BlockSpec((N_WIN, None, BQ, N_WIN * BQ), lambda h: (0, h, 0, 0)),
        out_shape=jax.ShapeDtypeStruct((N_WIN, H, BQ, N_WIN * BQ), F32),
        compiler_params=_params("parallel"),
    )(near, far)


def _bias_table_grad(dtab):
    H = dtab.shape[0]
    blk = lambda ic, jc: dtab[:, ic * CHUNK:(ic + 1) * CHUNK, jc * CHUNK:(jc + 1) * CHUNK]
    by_dl = [sum(blk(ic, ic + dl) for ic in range(Q_CHUNKS)) for dl in range(N_LEFT_CHUNKS + 1)]
    far = sum(jnp.sum(by_dl[dl], axis=(1, 2)) for dl in range(6))
    near = jnp.stack(by_dl[6:9], axis=1).reshape(H, 3 * CHUNK * CHUNK)
    g = jnp.dot(near, _rel_onehot(), precision=lax.Precision.HIGHEST)
    return g.at[:, N_REL - 1].add(far)


def _attn_specs(nblk, W):
    last = nblk - 1
    q_spec = pl.BlockSpec((BQ, W), lambda g, i: (jnp.minimum(i, last), g))
    kv_specs = [pl.BlockSpec((BQ, 2 * W), functools.partial(
        lambda g, i, w: (jnp.maximum(jnp.minimum(i, last) - (N_WIN - 1) + w, 0), g), w=w)) for w in range(N_WIN)]
    tab_spec = pl.BlockSpec((None, HEADS_PER_STEP, BQ, N_WIN * BQ),
                            lambda g, i: (jnp.minimum(i, N_WIN - 1), g, 0, 0))
    dtab_spec = pl.BlockSpec((HEADS_PER_STEP, BQ, N_WIN * BQ), lambda g, i: (g, 0, 0))
    return q_spec, kv_specs, tab_spec, dtab_spec


def _attn_exp(q_ref, kT, tab_ref, h, dh):
    s = jnp.dot(q_ref[:, h * dh:(h + 1) * dh], kT[h * dh:(h + 1) * dh, :], preferred_element_type=F32) + tab_ref[h]
    e = jnp.exp(s - jnp.max(s, axis=-1, keepdims=True))
    return e, jnp.sum(e, axis=-1, keepdims=True)


def _attn_fwd(q, kv, tab, name):
    S, D = q.shape
    dh = D // N_HEADS
    W = HEADS_PER_STEP * dh
    assert 2 * W == D, "the kv layout puts one head group's k beside its v: two head groups"
    q_spec, kv_specs, tab_spec, _ = _attn_specs(S // BQ, W)

    def body(q_ref, *rest):
        tab_ref, o_ref = rest[N_WIN], rest[N_WIN + 1]
        kvw = jnp.concatenate([r[...] for r in rest[:N_WIN]], axis=0)
        kT = kvw[:, :W].T
        vw = kvw[:, W:]
        outs = []
        for h in range(HEADS_PER_STEP):
            e, l = _attn_exp(q_ref, kT, tab_ref, h, dh)
            outs.append(jnp.dot(e.astype(BF16), vw[:, h * dh:(h + 1) * dh], preferred_element_type=F32) / l)
        o_ref[...] = jnp.concatenate(outs, axis=1).astype(BF16)

    return pl.pallas_call(
        body, name=name, grid=(N_HEADS // HEADS_PER_STEP, S // BQ),
        in_specs=[q_spec] + kv_specs + [tab_spec],
        out_specs=q_spec,
        out_shape=jax.ShapeDtypeStruct((S, D), BF16),
        compiler_params=_params("parallel", "parallel"),
    )(q, *([kv] * N_WIN), tab)


def _attn_bwd(q, kv, tab, do, name):
    S, D = q.shape
    dh = D // N_HEADS
    W = HEADS_PER_STEP * dh
    nblk = S // BQ
    q_spec, kv_specs, tab_spec, dtab_spec = _attn_specs(nblk, W)

    def body(q_ref, *rest):
        tab_ref, do_ref, dq_ref, dkv_ref, dtab_ref, ring = rest[N_WIN:]
        i = pl.program_id(1)

        @pl.when(i == 0)
        def _():
            dtab_ref[...] = jnp.zeros_like(dtab_ref)
            ring[...] = jnp.zeros_like(ring)

        @pl.when(i < nblk)
        def _():
            kvw = jnp.concatenate([r[...] for r in rest[:N_WIN]], axis=0)
            kT = kvw[:, :W].T
            vT = kvw[:, W:].T
            qT = q_ref[...].T
            doT = do_ref[...].T
            dqs, dks, dvs = [], [], []
            for h in range(HEADS_PER_STEP):
                hd = slice(h * dh, (h + 1) * dh)
                e, l = _attn_exp(q_ref, kT, tab_ref, h, dh)
                p = e * (1.0 / l)
                dp = jnp.dot(do_ref[:, hd], vT[hd, :], preferred_element_type=F32)
                ds = p * (dp - jnp.sum(p * dp, axis=-1, keepdims=True))
                dtab_ref[h] += ds
                dsb = ds.astype(BF16)
                dqs.append(lax.dot_general(kT[hd, :], dsb, _DIMS["nt"], preferred_element_type=F32) * (dh ** -0.5))
                dks.append(jnp.dot(qT[hd, :], dsb, preferred_element_type=F32))
                dvs.append(jnp.dot(doT[hd, :], p.astype(BF16), preferred_element_type=F32))
            dq_ref[...] = jnp.concatenate(dqs, axis=0).T.astype(BF16)
            dkv = jnp.concatenate(dks + dvs, axis=0).T
            for w in range(N_WIN):
                slot = lax.rem(i + 1 + w, N_WIN)
                part = dkv[w * BQ:(w + 1) * BQ, :]
                if w == N_WIN - 1:
                    ring[slot] = part
                else:
                    ring[slot] += part

        dkv_ref[...] = ring[lax.rem(i + 1, N_WIN)].astype(BF16)

    done_spec = pl.BlockSpec((BQ, 2 * W), lambda g, i: (jnp.maximum(i - (N_WIN - 1), 0), g))
    return pl.pallas_call(
        body, name=name, grid=(N_HEADS // HEADS_PER_STEP, nblk + N_WIN - 1),
        in_specs=[q_spec] + kv_specs + [tab_spec, q_spec],
        out_specs=[q_spec, done_spec, dtab_spec],
        out_shape=[jax.ShapeDtypeStruct((S, D), BF16), jax.ShapeDtypeStruct((S, 2 * D), BF16),
                   jax.ShapeDtypeStruct(tab.shape[1:], F32)],
        scratch_shapes=[pltpu.VMEM((N_WIN, BQ, 2 * W), F32)],
        compiler_params=_params("parallel", "arbitrary"),
    )(q, *([kv] * N_WIN), tab, do)


def _adamw(w, g, m, v, name):
    shape = w.shape
    C = shape[-1]
    R = int(np.prod(shape[:-1])) if len(shape) > 1 else 1
    whole = len(shape) >= 2 and R * C <= SMALL_TENSOR_ELEMS
    if whole:
        w2, g2, m2, v2 = w, g, m, v
    else:
        w2, g2, m2, v2 = (t.reshape(R, C) for t in (w, g, m, v))
    tr = _pick(R, max(8, (512 * 1024) // C // 8 * 8), 8)

    def body(w_ref, g_ref, m_ref, v_ref, d_ref, nm_ref, nv_ref):
        gv = g_ref[...]
        nm = ADAM_B1 * m_ref[...] + (1.0 - ADAM_B1) * gv
        nv = ADAM_B2 * v_ref[...] + (1.0 - ADAM_B2) * jnp.square(gv)
        m_hat = nm / (1.0 - ADAM_B1 ** ADAM_STEP)
        v_hat = nv / (1.0 - ADAM_B2 ** ADAM_STEP)
        d_ref[...] = -ADAM_LR * (m_hat / (jnp.sqrt(v_hat) + ADAM_EPS) + ADAM_WD * w_ref[...])
        nm_ref[...] = nm
        nv_ref[...] = nv

    if whole:
        spec, grid = pl.BlockSpec(shape, lambda i: (0,) * len(shape)), (1,)
    else:
        spec, grid = pl.BlockSpec((tr, C), lambda i: (i, 0)), (R // tr,)
    outs = pl.pallas_call(
        body, name=name, grid=grid,
        in_specs=[spec] * 4, out_specs=[spec] * 3,
        out_shape=[jax.ShapeDtypeStruct(w2.shape, F32)] * 3,
        compiler_params=_params("parallel"),
    )(w2, g2, m2, v2)
    return tuple(o.reshape(shape) for o in outs)


def _sum_rows(a, name):
    n, L = a.shape

    def body(a_ref, o_ref):
        acc = a_ref[0:1, :]
        for r in range(1, n):
            acc = acc + a_ref[r:r + 1, :]
        o_ref[...] = acc

    return pl.pallas_call(
        body, name=name, grid=(1,),
        in_specs=[pl.BlockSpec((n, L), lambda i: (0, 0))],
        out_specs=pl.BlockSpec((1, L), lambda i: (0, 0)),
        out_shape=jax.ShapeDtypeStruct((1, L), F32),
        compiler_params=_params("arbitrary"),
    )(a)


def _scalar_call(body, name, scalar, grid, in_specs, out_spec, out_shape, args):
    return pl.pallas_call(
        body, name=name,
        grid_spec=pltpu.PrefetchScalarGridSpec(num_scalar_prefetch=1, grid=grid, in_specs=in_specs,
                                               out_specs=out_spec),
        out_shape=out_shape, compiler_params=_params("parallel"),
    )(jnp.reshape(scalar, (-1,)).astype(jnp.int32), *args)


def _pair_sum(view, got, c, name):
    nb, _, rh, cols = view.shape
    tr = _pick(rh, max(16, (1 << 20) // cols // 16 * 16), 16)
    bpr = rh // tr

    def body(s_ref, a_ref, b_ref, o_ref):
        o_ref[...] = (a_ref[...].astype(F32) + b_ref[...].astype(F32)).astype(BF16)

    spec = pl.BlockSpec((tr, cols), lambda i, s: (i, 0))
    mine = pl.BlockSpec((tr, cols), lambda i, s: ((2 * (i // bpr) + s[0]) * bpr + i % bpr, 0))
    return _scalar_call(body, name, c, (nb * bpr,), [mine, spec], spec,
                        jax.ShapeDtypeStruct((nb * rh, cols), BF16),
                        (view.reshape(nb * 2 * rh, cols), got.reshape(nb * rh, cols)))


STACKED_LAYERS = 2


def _owner_sum(pair, recv, me, c, it, name, layer=None, into=None):
    _, rh, bc = recv.shape
    tr = _pick(rh, max(16, (1 << 19) // bc // 16 * 16), 16)
    bpr = rh // tr

    def body(s_ref, a_ref, r0, r1, r2, *rest):
        rest[-1][...] = ((a_ref[...].astype(F32) + r0[...].astype(F32)) + r1[...].astype(F32)) + r2[...].astype(F32)

    if it.kind == "col":
        own = pl.BlockSpec((tr, bc), lambda i, s: (i, s[0]))
    else:
        own = pl.BlockSpec((tr, bc), lambda i, s: (s[0] * bpr + i, 0))
    slots = [pl.BlockSpec((None, tr, bc), functools.partial(lambda i, s, k: (k, i, 0), k=k)) for k in range(3)]
    in_specs, args, aliases = [own] + slots, [pair, recv, recv, recv], {}
    if layer is None:
        out_spec = pl.BlockSpec((tr, bc), lambda i, s: (s[1] * bpr + i, 0))
        out_shape = jax.ShapeDtypeStruct((2 * rh, bc), F32)
    else:
        out_spec = pl.BlockSpec((None, tr, bc), lambda i, s: (layer, s[1] * bpr + i, 0))
        out_shape = jax.ShapeDtypeStruct((STACKED_LAYERS, 2 * rh, bc), F32)
        if into is not None:
            in_specs.append(pl.BlockSpec(memory_space=pl.ANY))
            args.append(into)
            aliases = {len(args): 0}
    return pl.pallas_call(
        body, name=name,
        grid_spec=pltpu.PrefetchScalarGridSpec(num_scalar_prefetch=1, grid=(bpr,), in_specs=in_specs,
                                               out_specs=out_spec),
        out_shape=out_shape, input_output_aliases=aliases, compiler_params=_params("parallel"),
    )(jnp.stack([it.pos(me), c]).astype(jnp.int32), *args)


def _place():
    x, y, c = lax.axis_index("x"), lax.axis_index("y"), lax.axis_index("c")
    chips = [(1 - x, y), (x, 1 - y), (1 - x, 1 - y)]
    return x, y, c, chips


def _chip_index(px, py):
    return 2 * px + py


def _all_gather_small(x_shard, name):
    m_per, n = x_shard.shape

    def body(x_ref, out_ref, send_sems, recv_sems, local_sem):
        x, y, c, chips = _place()
        me, sibling = (x, y, c), (x, y, 1 - c)

        def rows(px, py, pc):
            return out_ref.at[pl.ds((4 * px + 2 * py + pc) * m_per, m_per), :]

        def copy(k, block, to, src=None):
            return pltpu.make_async_remote_copy(
                src_ref=rows(*block) if src is None else src, dst_ref=rows(*block),
                send_sem=send_sems.at[k], recv_sem=recv_sems.at[k], device_id=to, device_id_type=MESH)

        mine = pltpu.make_async_copy(x_ref, rows(*me), local_sem)
        mine.start()
        first = [copy(0, me, sibling, src=x_ref)]
        first += [copy(1 + j, me, (*chip, c), src=x_ref) for j, chip in enumerate(chips)]
        for cp in first:
            cp.start()
        passed = [copy(4 + j, (*chip, c), sibling) for j, chip in enumerate(chips)]
        for j, chip in enumerate(chips):
            copy(1 + j, (*chip, c), me).wait_recv()
            passed[j].start()
        copy(0, sibling, me).wait_recv()
        for j, chip in enumerate(chips):
            copy(4 + j, (*chip, 1 - c), me).wait_recv()
        for cp in first + passed:
            cp.wait_send()
        mine.wait()

    return pl.pallas_call(
        body, name=name,
        out_shape=jax.ShapeDtypeStruct((N_DEV * m_per, n), x_shard.dtype),
        in_specs=[pl.BlockSpec(memory_space=pltpu.VMEM)],
        out_specs=pl.BlockSpec(memory_space=pltpu.VMEM),
        scratch_shapes=[pltpu.SemaphoreType.DMA((7,)), pltpu.SemaphoreType.DMA((7,)), pltpu.SemaphoreType.DMA],
    )(x_shard)


def _gather_flat(vec, name):
    L = vec.shape[0]
    Lp = -(-L // 1024) * 1024
    g = _all_gather_small(jnp.pad(vec, (0, Lp - L)).reshape(8, Lp // 8), name)
    return g.reshape(N_DEV, Lp)[:, :L]


class _Item:
    def __init__(self, kind, rows, cols, arg, layer, swap=False):
        self.kind, self.rows, self.cols, self.arg, self.layer, self.swap = kind, rows, cols, arg, layer, swap

    def ref(self, refs):
        return refs[self.arg].at[self.layer]

    def pos(self, j):
        return 2 * (j % 2) + j // 2 if self.swap else j


def _block(ref, it, j, half):
    if it.kind == "col":
        ns = it.cols // N_CHIP
        return ref.at[pl.ds(half * (it.rows // 2), it.rows // 2), pl.ds(it.pos(j) * ns, ns)]
    rs = it.rows // N_CHIP
    return ref.at[pl.ds(j * rs + half * (rs // 2), rs // 2), :]


def _cast_place(w, layer, kind, pos, after, name):
    _, r, n = w.shape
    tr = _pick(r, max(16, (1 << 20) // n // 16 * 16), 16)
    bpr = r // tr

    def body(s_ref, w_ref, after_ref, o_ref):
        o_ref[...] = w_ref[...].astype(BF16)

    if kind == "col":
        full, out_idx = (1, r, N_CHIP * n), (lambda i, s: (0, i, s[0]))
    else:
        full, out_idx = (1, N_CHIP * r, n), (lambda i, s: (0, s[0] * bpr + i, 0))
    return pl.pallas_call(
        body, name=name,
        grid_spec=pltpu.PrefetchScalarGridSpec(
            num_scalar_prefetch=1, grid=(bpr,),
            in_specs=[pl.BlockSpec((None, tr, n), lambda i, s: (layer, i, 0)), pl.BlockSpec(memory_space=pl.ANY)],
            out_specs=pl.BlockSpec((None, tr, n), out_idx)),
        out_shape=jax.ShapeDtypeStruct(full, BF16),
        compiler_params=_params("parallel"),
    )(jnp.reshape(pos, (1,)).astype(jnp.int32), w, after)


HBM_SPEC = pl.BlockSpec(memory_space=pltpu.HBM)
SEM_SPEC = pl.BlockSpec(memory_space=pltpu.SEMAPHORE)
ANY_SPEC = pl.BlockSpec(memory_space=pl.ANY)
SPLIT_PARAMS = dict(has_side_effects=pltpu.SideEffectType.DATAFLOW_SIDE_EFFECTING)


def _in_hbm(a):
    return pltpu.with_memory_space_constraint(a, pltpu.HBM)


def _split_start(copies_of, bufs, n_sem, after, name):
    n = len(bufs)

    def body(*refs):
        ins, send, recv, token = refs[:n], refs[n + 1], refs[n + 2], refs[2 * n + 3]
        for cp in copies_of(ins, send, recv, False)[0]:
            cp.start()
        token[...] = jnp.zeros_like(token)

    outs = pl.pallas_call(
        body, name=name,
        out_shape=(pltpu.SemaphoreType.DMA(n_sem), pltpu.SemaphoreType.DMA(n_sem),
                   *[pltpu.HBM(b.shape, b.dtype) for b in bufs], jax.ShapeDtypeStruct((8, 128), F32)),
        in_specs=[HBM_SPEC] * n + [ANY_SPEC],
        out_specs=(SEM_SPEC, SEM_SPEC, *[HBM_SPEC] * n, pl.BlockSpec(memory_space=pltpu.VMEM)),
        input_output_aliases={t: 2 + t for t in range(n)},
        compiler_params=pltpu.CompilerParams(**SPLIT_PARAMS),
    )(*[_in_hbm(b) for b in bufs], after)
    return outs[0], outs[1], list(outs[2:2 + n]), outs[2 + n]


def _split_wait(copies_of, send, recv, bufs, after, name):
    n = len(bufs)

    def body(*refs):
        ins, send_ref, recv_ref = refs[:n], refs[n], refs[n + 1]
        sends, arrivals = copies_of(ins, send_ref, recv_ref, True)
        for cp in sends:
            cp.wait_send()
        for cp in arrivals:
            cp.wait_recv()

    return pl.pallas_call(
        body, name=name,
        out_shape=[pltpu.HBM(b.shape, b.dtype) for b in bufs],
        in_specs=[HBM_SPEC] * n + [SEM_SPEC, SEM_SPEC, ANY_SPEC],
        out_specs=[HBM_SPEC] * n,
        input_output_aliases={t: t for t in range(n)},
        compiler_params=pltpu.CompilerParams(**SPLIT_PARAMS),
    )(*bufs, send, recv, after)


def _gather_copies(items):
    def copies_of(refs, send, recv, with_arrivals):
        x, y, c, chips = _place()
        me = _chip_index(x, y)
        sends, arrivals = [], []
        for t, it in enumerate(items):
            for k, chip in enumerate(chips):
                for core in range(2):
                    mine = _block(it.ref(refs), it, me, c)
                    sends.append(pltpu.make_async_remote_copy(
                        src_ref=mine, dst_ref=mine, send_sem=send.at[6 * t + 2 * k + core],
                        recv_sem=recv.at[6 * t + 2 * k + c], device_id=(*chip, core), device_id_type=MESH))
                    if with_arrivals:
                        landed = _block(it.ref(refs), it, _chip_index(*chip), core)
                        arrivals.append(pltpu.make_async_remote_copy(
                            src_ref=landed, dst_ref=landed, send_sem=send.at[6 * t + 2 * k + core],
                            recv_sem=recv.at[6 * t + 2 * k + core], device_id=(*chip, core), device_id_type=MESH))
        return sends, arrivals

    return copies_of


def _owner_copies(items):
    n = len(items)

    def blk(ref, it, j):
        if it.kind == "col":
            ns = it.cols // N_CHIP
            return ref.at[:, pl.ds(it.pos(j) * ns, ns)]
        return ref.at[j]

    def copies_of(refs, send, recv, with_arrivals):
        x, y, c, chips = _place()
        sends, arrivals = [], []
        for t, it in enumerate(items):
            for k, chip in enumerate(chips):
                slot = refs[n + t].at[k]
                sends.append(pltpu.make_async_remote_copy(
                    src_ref=blk(refs[t], it, _chip_index(*chip)), dst_ref=slot, send_sem=send.at[3 * t + k],
                    recv_sem=recv.at[3 * t + k], device_id=(*chip, c), device_id_type=MESH))
                if with_arrivals:
                    arrivals.append(pltpu.make_async_remote_copy(
                        src_ref=slot, dst_ref=slot, send_sem=send.at[3 * t + k], recv_sem=recv.at[3 * t + k],
                        device_id=(*chip, c), device_id_type=MESH))
        return sends, arrivals

    return copies_of


def _owner_slot_shape(it):
    if it.kind == "col":
        return (3, it.rows // 2, it.cols // N_CHIP)
    return (3, it.rows // (2 * N_CHIP), it.cols)


def _pair_view(g, it):
    if it.kind == "col":
        return g.reshape(1, 2, it.rows // 2, it.cols)
    return g.reshape(N_CHIP, 2, it.rows // (2 * N_CHIP), it.cols)


def _pair_copies(n):
    def copies_of(refs, send, recv, with_arrivals):
        x, y, c, _ = _place()
        sends, arrivals = [], []
        for t in range(n):
            land = refs[n + t]
            sends.append(pltpu.make_async_remote_copy(
                src_ref=refs[t].at[:, pl.ds(1 - c, 1)], dst_ref=land, send_sem=send.at[t], recv_sem=recv.at[t],
                device_id=(x, y, 1 - c), device_id_type=MESH))
            if with_arrivals:
                arrivals.append(pltpu.make_async_remote_copy(
                    src_ref=land, dst_ref=land, send_sem=send.at[t], recv_sem=recv.at[t],
                    device_id=(x, y, 1 - c), device_id_type=MESH))
        return sends, arrivals

    return copies_of


def _half_copies(n):
    def half(ref, which):
        r2 = ref.shape[-2] // 2
        rows = pl.ds(which * r2, r2)
        return ref.at[rows, :] if len(ref.shape) == 2 else ref.at[:, rows, :]

    def copies_of(refs, send, recv, with_arrivals):
        x, y, c, _ = _place()
        sends, arrivals = [], []
        for t in range(n):
            mine = half(refs[t], c)
            sends.append(pltpu.make_async_remote_copy(
                src_ref=mine, dst_ref=mine, send_sem=send.at[t], recv_sem=recv.at[t],
                device_id=(x, y, 1 - c), device_id_type=MESH))
            if with_arrivals:
                theirs = half(refs[t], 1 - c)
                arrivals.append(pltpu.make_async_remote_copy(
                    src_ref=theirs, dst_ref=theirs, send_sem=send.at[t], recv_sem=recv.at[t],
                    device_id=(x, y, 1 - c), device_id_type=MESH))
        return sends, arrivals

    return copies_of


class _Reduction:
    pass


def _pair_start(grads, items, after, tag, names, layer=None):
    n = len(items)
    views = [_pair_view(g, it) for g, it in zip(grads, items)]
    lands = [lax.empty((v.shape[0], 1) + v.shape[2:], v.dtype) for v in views]
    r = _Reduction()
    r.items, r.tag, r.names, r.layer = items, tag, names, layer
    r.send, r.recv, r.bufs, r.token = _split_start(_pair_copies(n), views + lands, (n,), after, f"rs_pair_start_{tag}")
    return r


def _owner_start(r, after):
    x, y, c, _ = _place()
    n = len(r.items)
    bufs = _split_wait(_pair_copies(n), r.send, r.recv, r.bufs, after, f"rs_pair_wait_{r.tag}")
    pairs = [_pair_sum(bufs[t], bufs[n + t], c, f"rs_pair_sum_{r.tag}_{t}") for t in range(n)]
    shaped = [p if it.kind == "col" else p.reshape(N_CHIP, p.shape[0] // N_CHIP, p.shape[1])
              for p, it in zip(pairs, r.items)]
    lands = [lax.empty(_owner_slot_shape(it), BF16) for it in r.items]
    r.send, r.recv, r.bufs, r.token = _split_start(
        _owner_copies(r.items), shaped + lands, (3 * n,), r.token, f"rs_owner_start_{r.tag}")
    return r


def _reduce_finish(groups, after):
    x, y, c, _ = _place()
    me = _chip_index(x, y)
    halves = {}
    for r in groups:
        n = len(r.items)
        bufs = _split_wait(_owner_copies(r.items), r.send, r.recv, r.bufs, after, f"rs_owner_wait_{r.tag}")
        for t, (it, nm) in enumerate(zip(r.items, r.names)):
            pair = bufs[t].reshape(-1, bufs[t].shape[-1])
            halves[nm] = _owner_sum(pair, bufs[n + t], me, c, it, f"rs_owner_sum_{r.tag}_{t}",
                                    layer=r.layer, into=halves.get(nm))
    n = len(halves)
    return list(halves), _split_start(_half_copies(n), list(halves.values()), (n,), after, "rs_half_start")


def _silu(v):
    return v * jax.nn.sigmoid(v)


def _sum8(p):
    return jnp.sum(p, axis=-2)


def kernel(x, c, mod_w, mod_b, norm_g, ffn_w_in, ffn_w_out, conv_w_in, conv_k, conv_w_out, kv_mod_w, kv_mod_b, kv_norm_g, w_kv, attn_w_q, attn_w_o, rel_bias, loss_target, m_mod_w, m_mod_b, m_norm_g, m_ffn_w_in, m_ffn_w_out, m_conv_w_in, m_conv_k, m_conv_w_out, m_kv_mod_w, m_kv_mod_b, m_kv_norm_g, m_w_kv, m_attn_w_q, m_attn_w_o, m_rel_bias, v_mod_w, v_mod_b, v_norm_g, v_ffn_w_in, v_ffn_w_out, v_conv_w_in, v_conv_k, v_conv_w_out, v_kv_mod_w, v_kv_mod_b, v_kv_norm_g, v_w_kv, v_attn_w_q, v_attn_w_o, v_rel_bias):
    xi, yi, ci = lax.axis_index("x"), lax.axis_index("y"), lax.axis_index("c")
    chip = 2 * xi + yi
    dev = 2 * chip + ci
    _, S, D = x.shape
    F = ffn_w_out.shape[1] * N_CHIP
    x0 = x.reshape(S, D)
    target = loss_target.reshape(S, D)
    n_mod = mod_w.shape[2]
    n_kvm = kv_mod_w.shape[1]
    dsh = D // N_CHIP
    TF = F // 2

    c_all = _all_gather_small(c.reshape(8, D // 8), "ag_c").reshape(N_DEV, D)
    sc16 = jnp.pad(_silu(c_all), ((0, 8), (0, 0)))
    part = [_mm(sc16, mod_w, "nn", F32, f"mod_fwd_{l}", b_layer=l)[:8] for l in range(2)]
    part.append(_mm(sc16, kv_mod_w, "nn", F32, "mod_fwd_kv")[:8])
    fwd_vec = jnp.concatenate([p.reshape(-1) for p in part] + [norm_g.reshape(-1), conv_k.reshape(-1)])
    fwd_all = _gather_flat(fwd_vec, "ag_fwd_small")[0::2]
    o = 0
    mods = []
    for n in (n_mod, n_mod, n_kvm):
        blk = fwd_all[:, o:o + 8 * n].reshape(N_CHIP, 8, n)
        mods.append(lax.dynamic_index_in_dim(blk, dev, axis=1, keepdims=False).reshape(N_CHIP * n))
        o += 8 * n
    ng = fwd_all[:, o:o + 8 * dsh].reshape(N_CHIP, 2, 4, dsh).transpose(1, 2, 0, 3).reshape(2, 4, D)
    o += 8 * dsh
    ck = fwd_all[:, o:o + 3 * dsh].reshape(N_CHIP, 3, dsh).transpose(1, 0, 2).reshape(3, D)
    ck8 = jnp.pad(ck, ((0, 5), (0, 0)))
    mod = [mods[l] + mod_b[l] for l in range(2)]
    sh1, sc1, g1, sh2, sc2, g2 = zip(*[jnp.split(m, 6) for m in mod])
    kv_sh, kv_sc = jnp.split(mods[2] + kv_mod_b, 2)
    row = lambda v: v.reshape(1, D)

    it_conv = [_Item("col", D, 3 * D, 0, 0), _Item("row", D, D, 1, 0)]
    it_ffn = [_Item("col", D, 2 * F, 0, 0, swap=True), _Item("row", F, D, 1, 0)]
    it_attn = [_Item("col", D, 2 * D, 0, 0, swap=True), _Item("row", D, D, 1, 0), _Item("row", D, D, 2, 0)]

    def placed(w, layer, it, nm, after=fwd_all):
        return _cast_place(w, layer, it.kind, it.pos(chip), after, f"place_{nm}")

    flying = {}

    def start(tag, its, bufs, after):
        send, recv, bufs, tok = _split_start(_gather_copies(its), bufs, (6 * len(its),), after, f"ag_start_{tag}")
        flying[tag] = (its, send, recv, bufs)
        return tok

    def arrived(tag, after):
        its, send, recv, bufs = flying[tag]
        return _split_wait(_gather_copies(its), send, recv, bufs, after, f"ag_wait_{tag}")

    one = lambda it: [_Item(it.kind, it.rows, it.cols, 0, 0, it.swap)]
    tok = start("conv_in", one(it_conv[0]), [placed(conv_w_in, 0, it_conv[0], "conv_w_in")], fwd_all)
    tok = start("conv_out", one(it_conv[1]), [placed(conv_w_out, 0, it_conv[1], "conv_w_out", tok)], tok)
    tok = start("ffn0_in", one(it_ffn[0]), [placed(ffn_w_in, 0, it_ffn[0], "ffn_w_in0", tok)], tok)
    tok = start("ffn0_out", one(it_ffn[1]), [placed(ffn_w_out, 0, it_ffn[1], "ffn_w_out0", tok)], tok)
    tok = start("attn", it_attn, [placed(w_kv[None], 0, it_attn[0], "w_kv", tok),
                                  placed(attn_w_q, 0, it_attn[1], "attn_w_q", tok),
                                  placed(attn_w_o, 0, it_attn[2], "attn_w_o", tok)], tok)
    token = start("ffn1", it_ffn, [placed(ffn_w_in, 1, it_ffn[0], "ffn_w_in1", tok),
                                   placed(ffn_w_out, 1, it_ffn[1], "ffn_w_out1", tok)], tok)

    a1 = row(ng[0, 0] * (1.0 + sc1[0])) + token[0, 0]
    (h1,) = _norm_mod(x0, a1, row(sh1[0]), "l0_norm1")
    tab = _bias_table(rel_bias[0], "l1_bias_table")
    h1, tab = lax.optimization_barrier((h1, tab))
    (W_cin,) = arrived("conv_in", h1)
    bcx = _mm(h1, W_cin, "nn", BF16, "l0_conv_in", b_layer=0)
    ug = _conv_gate(bcx, ck8, "l0_conv_gate")
    gt1 = row(g1[0] * ng[0, 1])
    a2 = row(ng[0, 2] * (1.0 + sc2[0]))
    (W_cout,) = arrived("conv_out", ug)
    y1, x1, h2 = _mm_post(ug, W_cout, x0, gt1, "l0_conv_out", scales=a2, shifts=row(sh2[0]))
    (W_fin0,) = arrived("ffn0_in", h2)
    gu0, act0 = _ffn_in_act(h2, W_fin0, 0, "l0_ffn_in")
    (W_fout0,) = arrived("ffn0_out", act0)
    gt2 = row(g2[0] * ng[0, 3])
    a3 = ng[1, 0] * (1.0 + sc1[1])
    akv = kv_norm_g * (1.0 + kv_sc)
    y2, x2, h3, hkv = _mm_post(act0, W_fout0, x1, gt2, "l0_ffn_out",
                               scales=jnp.stack([a3, akv]), shifts=jnp.stack([sh1[1], kv_sh]))
    W_kv, W_q, W_o = arrived("attn", hkv)
    kvp = _mm(hkv, W_kv, "nn", BF16, "l1_kv", b_layer=0)
    att_scale = (D // N_HEADS) ** -0.5
    assert math.log2(att_scale) % 1 == 0, "scaling q before its bf16 cast is exact only for a power of two"
    qp = _mm(h3, W_q, "nn", BF16, "l1_q", b_layer=0, scale=att_scale)
    oh = _attn_fwd(qp, kvp, tab, "l1_attn")
    gt3 = row(g1[1] * ng[1, 1])
    a4 = row(ng[1, 2] * (1.0 + sc2[1]))
    y3, x3, h4 = _mm_post(oh, W_o, x2, gt3, "l1_attn_out", scales=a4, shifts=row(sh2[1]))
    W_fin1, W_fout1 = arrived("ffn1", h4)
    gu1, act1 = _ffn_in_act(h4, W_fin1, 0, "l1_ffn_in")
    gt4 = row(g2[1] * ng[1, 3])
    dx4, sq, dy4, dgt4 = _mm_post(act1, W_fout1, x3, gt4, "l1_ffn_out", target=target)
    loss_part = 0.5 * jnp.sum(sq) / D

    def ffn_grads(dy, h, gu, act, w_out, tag):
        dgu = _ffn_out_dx_act(dy, w_out, 0, gu, f"{tag}_ffn_out_dx")
        g_fout = _mm(act, dy, "tn", BF16, f"{tag}_ffn_out_dw", tm=TF)
        g_fin = _mm(h, dgu, "tn", BF16, f"{tag}_ffn_in_dw", tn=TF)
        return dgu, g_fin, g_fout

    dgu1, G_fin1, G_fout1 = ffn_grads(dy4, h4, gu1, act1, W_fout1, "l1")
    red = [_pair_start([G_fin1, G_fout1], it_ffn, token, "ffn1", ["ffn_w_in", "ffn_w_out"], layer=1)]
    dx3, ds4, db4, dy3, dgt3 = _mm_pre_bwd([(dgu1, W_fin1)], x3, dx4, a4 + red[0].token[0, 0], "l1_ffn_in_dx",
                                           post=(y3, gt3))
    doh = _mm(dy3, W_o, "nt", BF16, "l1_attn_out_dx", b_layer=0)
    G_o = _mm(oh, dy3, "tn", BF16, "l1_attn_out_dw")
    _owner_start(red[0], G_o)
    dq, dkv, dtab = _attn_bwd(qp, kvp, tab, doh, "l1_attn_bwd")
    d_rel = _bias_table_grad(dtab)
    G_q = _mm(h3, dq, "tn", BF16, "l1_q_dw")
    G_kv = _mm(hkv, dkv, "tn", BF16, "l1_kv_dw")
    red.append(_pair_start([G_kv, G_q, G_o], it_attn, red[-1].token, "attn", ["w_kv", "attn_w_q", "attn_w_o"]))
    dx2, ds3, db3, dy2, dgt2 = _mm_pre_bwd([(dq, W_q), (dkv, W_kv)], x2, dx3,
                                           jnp.stack([a3, akv]) + red[1].token[0, 0], "l1_qkv_dx", post=(y2, gt2))
    _owner_start(red[1], dx2)

    dgu0, G_fin0, G_fout0 = ffn_grads(dy2, h2, gu0, act0, W_fout0, "l0")
    red.append(_pair_start([G_fin0, G_fout0], it_ffn, red[-1].token, "ffn0", ["ffn_w_in", "ffn_w_out"], layer=0))
    dx1, ds2, db2, dy1, dgt1 = _mm_pre_bwd([(dgu0, W_fin0)], x1, dx2, a2 + red[2].token[0, 0], "l0_ffn_in_dx",
                                           post=(y1, gt1))
    _owner_start(red[2], dy1)
    dug = _mm(dy1, W_cout, "nt", BF16, "l0_conv_out_dx", b_layer=0)
    G_cout = _mm(ug, dy1, "tn", BF16, "l0_conv_out_dw")
    dbcx, dck = _conv_gate_bwd(dug, bcx, ck8, "l0_conv_gate_bwd")
    G_cin = _mm(h1, dbcx, "tn", BF16, "l0_conv_in_dw")
    red.append(_pair_start([G_cin, G_cout], it_conv, red[-1].token, "conv", ["conv_w_in", "conv_w_out"]))
    dx0, ds1, db1 = _mm_pre_bwd([(dbcx, W_cin)], x0, dx1, a1 + red[3].token[0, 0], "l0_conv_in_dx")
    ds1, db1 = _sum8(ds1)[0], _sum8(db1)[0]
    da2, db2 = _sum8(ds2)[0], _sum8(db2)[0]
    ds3, db3 = _sum8(ds3), _sum8(db3)
    da4, db4 = _sum8(ds4)[0], _sum8(db4)[0]
    dgt1, dgt2, dgt3, dgt4 = _sum8(dgt1), _sum8(dgt2), _sum8(dgt3), _sum8(dgt4)

    def dmod_of(l, ds_a, db_a, dgt_a, ds_b, db_b, dgt_b):
        return jnp.concatenate([db_a, ds_a * ng[l, 0], dgt_a * ng[l, 1], db_b, ds_b * ng[l, 2], dgt_b * ng[l, 3]])

    dmod0 = dmod_of(0, ds1, db1, dgt1, da2, db2, dgt2)
    dmod1 = dmod_of(1, ds3[0], db3[0], dgt3, da4, db4, dgt4)
    dkvmod = jnp.concatenate([db3[1], ds3[1] * kv_norm_g])
    dng = jnp.stack([
        jnp.stack([ds1 * (1.0 + sc1[0]), dgt1 * g1[0], da2 * (1.0 + sc2[0]), dgt2 * g2[0]]),
        jnp.stack([ds3[0] * (1.0 + sc1[1]), dgt3 * g1[1], da4 * (1.0 + sc2[1]), dgt4 * g2[1]])])
    dkvng = ds3[1] * (1.0 + kv_sc)
    small = [dmod0, dmod1, dkvmod, dng.reshape(-1), dkvng, _sum8(dck).reshape(-1), d_rel.reshape(-1),
             loss_part.reshape(1)]
    sizes = [int(s.shape[0]) for s in small]
    offs = np.concatenate([[0], np.cumsum(sizes)])
    bwd_all = _gather_flat(jnp.concatenate(small), "ag_bwd_small")
    _owner_start(red[3], bwd_all)
    Lb = bwd_all.shape[1]
    Lp = -(-Lb // 128) * 128
    tot = _sum_rows(jnp.pad(bwd_all, ((0, 0), (0, Lp - Lb))), "sum_small")[0]
    seg = lambda i: tot[offs[i]:offs[i + 1]]
    g_mod_b = jnp.stack([seg(0), seg(1)])
    g_kv_mod_b = seg(2)
    g_norm_g = lax.dynamic_slice_in_dim(seg(3).reshape(2, 4, D), chip * dsh, dsh, axis=2)
    g_kv_norm_g = seg(4)
    g_conv_k = lax.dynamic_slice_in_dim(seg(5).reshape(1, 3, D), chip * dsh, dsh, axis=2)
    g_rel_bias = seg(6).reshape(rel_bias.shape)
    loss = seg(7)[0]

    def dmod_w(i, n, name):
        rows_ = lax.dynamic_slice_in_dim(bwd_all[:, offs[i]:offs[i + 1]], chip * n, n, axis=1)
        return _mm(sc16, jnp.pad(rows_, ((0, 8), (0, 0))), "tn", F32, name)

    g_mod_w = jnp.stack([dmod_w(0, n_mod, "mod_bwd_0"), dmod_w(1, n_mod, "mod_bwd_1")])
    g_kv_mod_w = dmod_w(2, n_kvm, "mod_bwd_kv")

    grads = {
        "mod_w": g_mod_w, "mod_b": g_mod_b, "norm_g": g_norm_g, "conv_k": g_conv_k,
        "kv_mod_w": g_kv_mod_w, "kv_mod_b": g_kv_mod_b, "kv_norm_g": g_kv_norm_g, "rel_bias": g_rel_bias,
    }
    weights = dict(mod_w=mod_w, mod_b=mod_b, norm_g=norm_g, ffn_w_in=ffn_w_in, ffn_w_out=ffn_w_out,
                   conv_w_in=conv_w_in, conv_k=conv_k, conv_w_out=conv_w_out, kv_mod_w=kv_mod_w,
                   kv_mod_b=kv_mod_b, kv_norm_g=kv_norm_g, w_kv=w_kv, attn_w_q=attn_w_q, attn_w_o=attn_w_o,
                   rel_bias=rel_bias)
    m_in = dict(mod_w=m_mod_w, mod_b=m_mod_b, norm_g=m_norm_g, ffn_w_in=m_ffn_w_in, ffn_w_out=m_ffn_w_out,
                conv_w_in=m_conv_w_in, conv_k=m_conv_k, conv_w_out=m_conv_w_out, kv_mod_w=m_kv_mod_w,
                kv_mod_b=m_kv_mod_b, kv_norm_g=m_kv_norm_g, w_kv=m_w_kv, attn_w_q=m_attn_w_q,
                attn_w_o=m_attn_w_o, rel_bias=m_rel_bias)
    v_in = dict(mod_w=v_mod_w, mod_b=v_mod_b, norm_g=v_norm_g, ffn_w_in=v_ffn_w_in, ffn_w_out=v_ffn_w_out,
                conv_w_in=v_conv_w_in, conv_k=v_conv_k, conv_w_out=v_conv_w_out, kv_mod_w=v_kv_mod_w,
                kv_mod_b=v_kv_mod_b, kv_norm_g=v_kv_norm_g, w_kv=v_w_kv, attn_w_q=v_attn_w_q,
                attn_w_o=v_attn_w_o, rel_bias=v_rel_bias)
    names = list(weights)
    step = {}

    def update(n):
        g = grads[n].reshape(weights[n].shape)
        step[n] = (g, *_adamw(weights[n], g, m_in[n], v_in[n], f"adamw_{n}"))

    update("mod_w")
    reduced, (half_send, half_recv, half_bufs, _) = _reduce_finish(red, step["mod_w"][1])
    for n in list(grads):
        if n not in step:
            update(n)
    grads.update(zip(reduced, _split_wait(
        _half_copies(len(half_bufs)), half_send, half_recv, half_bufs, step["kv_mod_w"][1], "rs_half_wait")))
    for n in names:
        if n not in step:
            update(n)
    return (loss, dx0.reshape(x.shape), *[step[n][k] for k in range(4) for n in names])
```

```python
import functools
import math

import numpy as np
import jax
import jax.numpy as jnp
from jax import lax
from jax.experimental import pallas as pl
from jax.experimental.pallas import tpu as pltpu

CHUNK = 64
N_LEFT_CHUNKS = 8
N_HEADS = 16
MAX_REL = 2 * CHUNK
N_REL = 2 * MAX_REL + 1
EPS = 1e-6
ADAM_LR = 0.001
ADAM_B1 = 0.9
ADAM_B2 = 0.999
ADAM_EPS = 1e-08
ADAM_WD = 0.01
ADAM_STEP = 10

Q_CHUNKS = 4
BQ = Q_CHUNKS * CHUNK
N_WIN = 1 + N_LEFT_CHUNKS // Q_CHUNKS
HEADS_PER_STEP = 8
NEG = -1e30
N_DEV = 8
N_CHIP = 4
SMALL_TENSOR_ELEMS = 1 << 16

BF16 = jnp.bfloat16
F32 = jnp.float32
V7X_VMEM_LIMIT_BYTES = 56 * 1024 * 1024
MESH = pl.DeviceIdType.MESH


def _pick(n, pref, align):
    t = min(pref, n)
    t -= t % align
    while t >= align:
        if n % t == 0:
            return t
        t -= align
    return n


def _params(*sem):
    return pltpu.CompilerParams(dimension_semantics=sem, vmem_limit_bytes=V7X_VMEM_LIMIT_BYTES)


def _colsum8(v):
    r, d = v.shape
    return v.reshape(r // 8, 8, d).sum(axis=0)


_DIMS = {"nn": (((1,), (0,)), ((), ())), "nt": (((1,), (1,)), ((), ())), "tn": (((0,), (0,)), ((), ()))}


def _mm(a, b, mode, out_dtype, name, *, b_layer=None, tm=1024, tn=1024, tk=None, scale=None, after=None):
    if tk is None:
        tk = 2048 if mode == "tn" else 3072
    bs = b.shape[1:] if b_layer is not None else b.shape
    if mode == "nn":
        (M, K), (K2, N) = a.shape, bs
    elif mode == "nt":
        (M, K), (N, K2) = a.shape, bs
    else:
        (K, M), (K2, N) = a.shape, bs
    assert K == K2, (name, a.shape, b.shape)
    tm = _pick(M, tm, 128 if mode == "tn" else 16)
    tn = _pick(N, tn, 128)
    tk = _pick(K, tk, 128 if mode != "tn" else 16)
    nk = K // tk
    assert scale is None or nk == 1, name
    dims = _DIMS[mode]
    extra = [] if after is None else [after]

    def body(a_ref, b_ref, *rest):
        o_ref, acc = rest[len(extra)], rest[len(extra) + 1:]
        p = lax.dot_general(a_ref[...].astype(BF16), b_ref[...].astype(BF16), dims,
                            preferred_element_type=F32)
        if nk == 1:
            o_ref[...] = (p if scale is None else p * scale).astype(o_ref.dtype)
        else:
            k = pl.program_id(2)

            @pl.when(k == 0)
            def _():
                acc[0][...] = p

            @pl.when(k > 0)
            def _():
                acc[0][...] += p

            @pl.when(k == nk - 1)
            def _():
                o_ref[...] = acc[0][...].astype(o_ref.dtype)

    a_spec = (pl.BlockSpec((tk, tm), lambda i, j, k: (k, i)) if mode == "tn"
              else pl.BlockSpec((tm, tk), lambda i, j, k: (i, k)))
    if mode == "nt":
        b_blk, b_idx = (tn, tk), (lambda i, j, k: (j, k))
    else:
        b_blk, b_idx = (tk, tn), (lambda i, j, k: (k, j))
    if b_layer is not None:
        b_spec = pl.BlockSpec((None,) + b_blk, lambda i, j, k: (b_layer,) + b_idx(i, j, k))
    else:
        b_spec = pl.BlockSpec(b_blk, b_idx)
    return pl.pallas_call(
        body, name=name,
        grid=(M // tm, N // tn, nk),
        in_specs=[a_spec, b_spec] + [pl.BlockSpec(memory_space=pl.ANY)] * len(extra),
        out_specs=pl.BlockSpec((tm, tn), lambda i, j, k: (i, j)),
        out_shape=jax.ShapeDtypeStruct((M, N), out_dtype),
        scratch_shapes=[pltpu.VMEM((tm, tn), F32)] if nk > 1 else [],
        compiler_params=_params("parallel", "parallel", "arbitrary"),
    )(a, b, *extra)


def _row_spec(tm, d):
    return pl.BlockSpec((tm, d), lambda i: (i, 0))


def _vec_spec(r, d):
    return pl.BlockSpec((r, d), lambda i: (0, 0))


def _norm_mod(x, scales, shifts, name):
    S, D = x.shape
    nb = scales.shape[0]
    tm = _pick(S, 512, 16)

    def body(x_ref, a_ref, b_ref, *o_refs):
        xv = x_ref[...]
        xh = xv * lax.rsqrt(jnp.mean(xv * xv, axis=-1, keepdims=True) + EPS)
        for n in range(nb):
            o_refs[n][...] = (xh * a_ref[n:n + 1, :] + b_ref[n:n + 1, :]).astype(BF16)

    return pl.pallas_call(
        body, name=name, grid=(S // tm,),
        in_specs=[_row_spec(tm, D), _vec_spec(nb, D), _vec_spec(nb, D)],
        out_specs=[_row_spec(tm, D)] * nb,
        out_shape=[jax.ShapeDtypeStruct((S, D), BF16)] * nb,
        compiler_params=_params("parallel"),
    )(x, scales, shifts)


def _mm_post(a, w, x, gate, name, *, scales=None, shifts=None, target=None, sub=256):
    M, K = a.shape
    D = w.shape[2]
    tm = _pick(M, 512, 16)
    sub = _pick(tm, sub, 16)
    nb = 0 if scales is None else scales.shape[0]

    def body(a_ref, w_ref, x_ref, g_ref, *rest):
        if target is None:
            sc_ref, sh_ref, y_ref, xn_ref = rest[:4]
            h_refs = rest[4:]
        else:
            t_ref, dx_ref, sq_ref, dy_ref, dg_ref = rest

            @pl.when(pl.program_id(0) == 0)
            def _():
                sq_ref[...] = jnp.zeros_like(sq_ref)
                dg_ref[...] = jnp.zeros_like(dg_ref)

        for r in range(tm // sub):
            rows = pl.ds(r * sub, sub)
            yb = jnp.dot(a_ref[rows, :], w_ref[...], preferred_element_type=F32).astype(BF16)
            yv = yb.astype(F32)
            yh = yv * lax.rsqrt(jnp.mean(yv * yv, axis=-1, keepdims=True) + EPS)
            xn = x_ref[rows, :] + yh * g_ref[...]
            if target is None:
                y_ref[rows, :] = yb
                xn_ref[rows, :] = xn
                xh = xn * lax.rsqrt(jnp.mean(xn * xn, axis=-1, keepdims=True) + EPS)
                for n in range(nb):
                    h_refs[n][rows, :] = (xh * sc_ref[n:n + 1, :] + sh_ref[n:n + 1, :]).astype(BF16)
            else:
                e = xn - t_ref[rows, :]
                dx = e / D
                dx_ref[rows, :] = dx
                sq_ref[...] += _colsum8(e * e)
                dy, dxy = _post_norm_grad(dx, yb, g_ref[...])
                dy_ref[rows, :] = dy.astype(BF16)
                dg_ref[...] += _colsum8(dxy)

    ins = [a, w, x, gate]
    in_specs = [_row_spec(tm, K), pl.BlockSpec((None, K, D), lambda i: (0, 0, 0)), _row_spec(tm, D), _vec_spec(1, D)]
    if target is None:
        ins += [scales, shifts]
        in_specs += [_vec_spec(nb, D), _vec_spec(nb, D)]
        out_specs = [_row_spec(tm, D)] * (2 + nb)
        out_shape = [jax.ShapeDtypeStruct((M, D), BF16), jax.ShapeDtypeStruct((M, D), F32)] \
            + [jax.ShapeDtypeStruct((M, D), BF16)] * nb
    else:
        ins += [target]
        in_specs += [_row_spec(tm, D)]
        out_specs = [_row_spec(tm, D), _vec_spec(8, D), _row_spec(tm, D), _vec_spec(8, D)]
        out_shape = [jax.ShapeDtypeStruct((M, D), F32), jax.ShapeDtypeStruct((8, D), F32),
                     jax.ShapeDtypeStruct((M, D), BF16), jax.ShapeDtypeStruct((8, D), F32)]
    return pl.pallas_call(
        body, name=name, grid=(M // tm,), in_specs=in_specs, out_specs=out_specs, out_shape=out_shape,
        compiler_params=_params("arbitrary" if target is not None else "parallel"),
    )(*ins)


def _post_norm_grad(dxn, yb, gate):
    yv = yb.astype(F32)
    r = lax.rsqrt(jnp.mean(yv * yv, axis=-1, keepdims=True) + EPS)
    yh = yv * r
    dyh = dxn * gate
    return r * (dyh - yh * jnp.mean(dyh * yh, axis=-1, keepdims=True)), dxn * yh


def _mm_pre_bwd(pairs, x, dxn, scales, name, post=None, sub=256):
    S, D = x.shape
    nb = len(pairs)
    tm = _pick(S, 512, 16)
    sub = _pick(tm, sub, 16)

    def body(*refs):
        a_refs, w_refs = refs[0:2 * nb:2], refs[1:2 * nb:2]
        x_ref, d_ref, sc_ref = refs[2 * nb:2 * nb + 3]
        rest = refs[2 * nb + 3:]
        if post is not None:
            y_ref, g_ref, dx_ref, ds_ref, db_ref, dy_ref, dg_ref = rest
        else:
            dx_ref, ds_ref, db_ref = rest

        @pl.when(pl.program_id(0) == 0)
        def _():
            ds_ref[...] = jnp.zeros_like(ds_ref)
            db_ref[...] = jnp.zeros_like(db_ref)
            if post is not None:
                dg_ref[...] = jnp.zeros_like(dg_ref)

        for r in range(tm // sub):
            rows = pl.ds(r * sub, sub)
            xv = x_ref[rows, :]
            rr = lax.rsqrt(jnp.mean(xv * xv, axis=-1, keepdims=True) + EPS)
            xh = xv * rr
            dxh = jnp.zeros_like(xv)
            for n in range(nb):
                dh = lax.dot_general(a_refs[n][rows, :], w_refs[n][...], _DIMS["nt"], preferred_element_type=F32)
                dxh = dxh + dh * sc_ref[n:n + 1, :]
                ds_ref[n] += _colsum8(dh * xh)
                db_ref[n] += _colsum8(dh)
            dx = d_ref[rows, :] + rr * (dxh - xh * jnp.mean(dxh * xh, axis=-1, keepdims=True))
            dx_ref[rows, :] = dx
            if post is not None:
                dy, dxy = _post_norm_grad(dx, y_ref[rows, :], g_ref[...])
                dy_ref[rows, :] = dy.astype(BF16)
                dg_ref[...] += _colsum8(dxy)

    ins, in_specs = [], []
    for a, w in pairs:
        ins += [a, w]
        in_specs += [_row_spec(tm, a.shape[1]),
                     pl.BlockSpec((None, D, a.shape[1]), lambda i: (0, 0, 0), pipeline_mode=pl.Buffered(1))]
    ins += [x, dxn, scales]
    in_specs += [_row_spec(tm, D), _row_spec(tm, D), _vec_spec(nb, D)]
    acc_spec = pl.BlockSpec((nb, 8, D), lambda i: (0, 0, 0))
    out_specs = [_row_spec(tm, D), acc_spec, acc_spec]
    out_shape = [jax.ShapeDtypeStruct((S, D), F32), jax.ShapeDtypeStruct((nb, 8, D), F32),
                 jax.ShapeDtypeStruct((nb, 8, D), F32)]
    if post is not None:
        ins += list(post)
        in_specs += [_row_spec(tm, D), _vec_spec(1, D)]
        out_specs += [_row_spec(tm, D), _vec_spec(8, D)]
        out_shape += [jax.ShapeDtypeStruct((S, D), BF16), jax.ShapeDtypeStruct((8, D), F32)]
    return pl.pallas_call(
        body, name=name, grid=(S // tm,), in_specs=in_specs, out_specs=out_specs, out_shape=out_shape,
        compiler_params=_params("arbitrary"),
    )(*ins)


FFN_PAIRS = 2
FFN_SUB_ROWS = 256


def _ffn_in_act(h, w, layer, name, tm=1024, sub=FFN_SUB_ROWS):
    S, D = h.shape
    F2 = w.shape[2]
    PW = F2 // (2 * FFN_PAIRS)
    tm = _pick(S, tm, 16)
    sub = _pick(tm, sub, 16)

    def body(h_ref, w_ref, gu_ref, a_ref):
        for r in range(tm // sub):
            rows = pl.ds(r * sub, sub)
            acc = jnp.dot(h_ref[rows, :], w_ref[...], preferred_element_type=F32)
            gu_ref[rows, :] = acc.astype(BF16)
            g = acc[:, :PW]
            a_ref[rows, :] = (g * jax.nn.sigmoid(g) * acc[:, PW:]).astype(BF16)

    return pl.pallas_call(
        body, name=name, grid=(FFN_PAIRS, S // tm),
        in_specs=[pl.BlockSpec((tm, D), lambda p, i: (i, 0)),
                  pl.BlockSpec((None, D, 2 * PW), lambda p, i: (layer, 0, p))],
        out_specs=[pl.BlockSpec((tm, 2 * PW), lambda p, i: (i, p)), pl.BlockSpec((tm, PW), lambda p, i: (i, p))],
        out_shape=[jax.ShapeDtypeStruct((S, F2), BF16), jax.ShapeDtypeStruct((S, F2 // 2), BF16)],
        compiler_params=_params("parallel", "parallel"),
    )(h, w)


def _ffn_bwd(dy, w_out, gu, w_in, x, dxn, scale, post, name):
    S, D = dy.shape
    F2 = gu.shape[1]
    PW = F2 // (2 * FFN_PAIRS)
    tm = _pick(S, 256, 16)

    def body(dy_ref, wo_ref, gu_ref, wi_ref, x_ref, d_ref, sc_ref, y_ref, g_ref,
             dgu_ref, dx_ref, ds_ref, db_ref, dyn_ref, dg_ref):
        @pl.when(pl.program_id(0) == 0)
        def _():
            ds_ref[...] = jnp.zeros_like(ds_ref)
            db_ref[...] = jnp.zeros_like(db_ref)
            dg_ref[...] = jnp.zeros_like(dg_ref)

        dh = jnp.zeros((tm, D), F32)
        for p in range(FFN_PAIRS):
            cols = slice(2 * p * PW, 2 * (p + 1) * PW)
            da = lax.dot_general(dy_ref[...], wo_ref[p * PW:(p + 1) * PW, :], _DIMS["nt"],
                                 preferred_element_type=F32)
            g = gu_ref[:, 2 * p * PW:(2 * p + 1) * PW].astype(F32)
            u = gu_ref[:, (2 * p + 1) * PW:2 * (p + 1) * PW].astype(F32)
            sg = jax.nn.sigmoid(g)
            dgu_ref[:, 2 * p * PW:(2 * p + 1) * PW] = (da * u * (sg * (1.0 + g * (1.0 - sg)))).astype(BF16)
            dgu_ref[:, (2 * p + 1) * PW:2 * (p + 1) * PW] = (da * (g * sg)).astype(BF16)
            dh = dh + lax.dot_general(dgu_ref[:, cols], wi_ref[:, cols], _DIMS["nt"], preferred_element_type=F32)
        xv = x_ref[...]
        rr = lax.rsqrt(jnp.mean(xv * xv, axis=-1, keepdims=True) + EPS)
        xh = xv * rr
        dxh = dh * sc_ref[...]
        ds_ref[0] += _colsum8(dh * xh)
        db_ref[0] += _colsum8(dh)
        dx = d_ref[...] + rr * (dxh - xh * jnp.mean(dxh * xh, axis=-1, keepdims=True))
        dx_ref[...] = dx
        dyn, dxy = _post_norm_grad(dx, y_ref[...], g_ref[...])
        dyn_ref[...] = dyn.astype(BF16)
        dg_ref[...] += _colsum8(dxy)

    resident = dict(pipeline_mode=pl.Buffered(1))
    acc_spec = pl.BlockSpec((1, 8, D), lambda i: (0, 0, 0))
    return pl.pallas_call(
        body, name=name, grid=(S // tm,),
        in_specs=[_row_spec(tm, D), pl.BlockSpec((None, F2 // 2, D), lambda i: (0, 0, 0), **resident),
                  _row_spec(tm, F2), pl.BlockSpec((None, D, F2), lambda i: (0, 0, 0), **resident),
                  _row_spec(tm, D), _row_spec(tm, D), _vec_spec(1, D), _row_spec(tm, D), _vec_spec(1, D)],
        out_specs=[_row_spec(tm, F2), _row_spec(tm, D), acc_spec, acc_spec, _row_spec(tm, D), _vec_spec(8, D)],
        out_shape=[jax.ShapeDtypeStruct((S, F2), BF16), jax.ShapeDtypeStruct((S, D), F32),
                   jax.ShapeDtypeStruct((1, 8, D), F32), jax.ShapeDtypeStruct((1, 8, D), F32),
                   jax.ShapeDtypeStruct((S, D), BF16), jax.ShapeDtypeStruct((8, D), F32)],
        compiler_params=_params("arbitrary"),
    )(dy, w_out, gu, w_in, x, dxn, scale, *post)


HALO = 16


def _conv_terms(bcx_ref, prev_ref, i, tm, D):
    b = bcx_ref[:, 0:D].astype(F32)
    cg = bcx_ref[:, D:2 * D].astype(F32)
    xin = bcx_ref[:, 2 * D:3 * D].astype(F32)
    z = cg * xin
    zp = prev_ref[:, D:2 * D].astype(F32) * prev_ref[:, 2 * D:3 * D].astype(F32)
    zp = jnp.where(i > 0, zp, 0.0)
    z_ext = jnp.concatenate([zp, z], axis=0)
    z1 = pltpu.roll(z_ext, 1, 0)[HALO:, :]
    z2 = pltpu.roll(z_ext, 2, 0)[HALO:, :]
    return b, cg, xin, z, z1, z2


def _conv_gate(bcx, ck, name):
    S, D3 = bcx.shape
    D = D3 // 3
    tm = _pick(S, 256, 16)
    hb = tm // HALO

    def body(bcx_ref, prev_ref, ck_ref, o_ref):
        i = pl.program_id(0)
        b, _, _, z, z1, z2 = _conv_terms(bcx_ref, prev_ref, i, tm, D)
        conv = ck_ref[0:1, :] * z2 + ck_ref[1:2, :] * z1 + ck_ref[2:3, :] * z
        o_ref[...] = (b * conv).astype(BF16)

    return pl.pallas_call(
        body, name=name, grid=(S // tm,),
        in_specs=[_row_spec(tm, D3),
                  pl.BlockSpec((HALO, D3), lambda i: (jnp.maximum(i * hb - 1, 0), 0)),
                  _vec_spec(8, D)],
        out_specs=_row_spec(tm, D),
        out_shape=jax.ShapeDtypeStruct((S, D), BF16),
        compiler_params=_params("parallel"),
    )(bcx, bcx, ck)


def _conv_gate_bwd(du, bcx, ck, name):
    S, D3 = bcx.shape
    D = D3 // 3
    tm = _pick(S, 256, 16)
    hb = tm // HALO
    nt = S // tm

    def body(du_ref, dun_ref, bcx_ref, prev_ref, next_ref, ck_ref, o_ref, dk_ref):
        i = pl.program_id(0)
        b, cg, xin, z, z1, z2 = _conv_terms(bcx_ref, prev_ref, i, tm, D)
        k0, k1, k2 = ck_ref[0:1, :], ck_ref[1:2, :], ck_ref[2:3, :]
        conv = k0 * z2 + k1 * z1 + k2 * z
        d = du_ref[...].astype(F32)
        dconv = d * b
        dcn = jnp.where(i < nt - 1, dun_ref[...].astype(F32) * next_ref[:, 0:D].astype(F32), 0.0)
        d_ext = jnp.concatenate([dconv, dcn], axis=0)
        d1 = pltpu.roll(d_ext, tm + HALO - 1, 0)[:tm, :]
        d2 = pltpu.roll(d_ext, tm + HALO - 2, 0)[:tm, :]
        dz = k2 * dconv + k1 * d1 + k0 * d2
        o_ref[:, 0:D] = (d * conv).astype(BF16)
        o_ref[:, D:2 * D] = (dz * xin).astype(BF16)
        o_ref[:, 2 * D:3 * D] = (dz * cg).astype(BF16)

        @pl.when(i == 0)
        def _():
            dk_ref[...] = jnp.zeros_like(dk_ref)

        dk_ref[0] += _colsum8(dconv * z2)
        dk_ref[1] += _colsum8(dconv * z1)
        dk_ref[2] += _colsum8(dconv * z)

    last = S // HALO - 1
    return pl.pallas_call(
        body, name=name, grid=(nt,),
        in_specs=[_row_spec(tm, D),
                  pl.BlockSpec((HALO, D), lambda i: (jnp.minimum((i + 1) * hb, last), 0)),
                  _row_spec(tm, D3),
                  pl.BlockSpec((HALO, D3), lambda i: (jnp.maximum(i * hb - 1, 0), 0)),
                  pl.BlockSpec((HALO, D3), lambda i: (jnp.minimum((i + 1) * hb, last), 0)),
                  _vec_spec(8, D)],
        out_specs=[_row_spec(tm, D3), pl.BlockSpec((3, 8, D), lambda i: (0, 0, 0))],
        out_shape=[jax.ShapeDtypeStruct((S, D3), BF16), jax.ShapeDtypeStruct((3, 8, D), F32)],
        compiler_params=_params("arbitrary"),
    )(du, du, bcx, bcx, bcx, ck)


def _rel_onehot():
    a = np.arange(CHUNK)[:, None]
    b = np.arange(CHUNK)[None, :]
    idx = np.stack([np.clip((N_LEFT_CHUNKS - dl) * CHUNK + a - b, -MAX_REL, MAX_REL) + MAX_REL
                    for dl in (6, 7, 8)]).reshape(-1)
    return (jnp.asarray(idx)[:, None] == jnp.arange(N_REL)[None, :]).astype(F32)


def _bias_table(rel_bias, name):
    H = rel_bias.shape[0]
    near = jnp.dot(rel_bias, _rel_onehot().T, precision=lax.Precision.HIGHEST).reshape(H, 3, CHUNK, CHUNK)
    far = jnp.broadcast_to(rel_bias[:, N_REL - 1][:, None, None], (H, CHUNK, CHUNK))

    def body(near_ref, far_ref, o_ref):
        neg = jnp.full((CHUNK, CHUNK), NEG, F32)
        for v in range(N_WIN):
            for ic in range(Q_CHUNKS):
                for jc in range(N_WIN * Q_CHUNKS):
                    dl = jc - ic
                    if dl < 0 or dl > N_LEFT_CHUNKS or jc < (N_WIN - 1 - v) * Q_CHUNKS:
                        blk = neg
                    else:
                        blk = far_ref[...] if dl <= 5 else near_ref[dl - 6]
                    o_ref[v, ic * CHUNK:(ic + 1) * CHUNK, jc * CHUNK:(jc + 1) * CHUNK] = blk

    return pl.pallas_call(
        body, name=name, grid=(H,),
        in_specs=[pl.BlockSpec((None, 3, CHUNK, CHUNK), lambda h: (h, 0, 0, 0)),
                  pl.BlockSpec((None, CHUNK, CHUNK), lambda h: (h, 0, 0))],
        out_specs=pl.BlockSpec((N_WIN, None, BQ, N_WIN * BQ), lambda h: (0, h, 0, 0)),
        out_shape=jax.ShapeDtypeStruct((N_WIN, H, BQ, N_WIN * BQ), F32),
        compiler_params=_params("parallel"),
    )(near, far)


def _bias_table_grad(dtab):
    H = dtab.shape[0]
    blk = lambda ic, jc: dtab[:, ic * CHUNK:(ic + 1) * CHUNK, jc * CHUNK:(jc + 1) * CHUNK]
    by_dl = [sum(blk(ic, ic + dl) for ic in range(Q_CHUNKS)) for dl in range(N_LEFT_CHUNKS + 1)]
    far = sum(jnp.sum(by_dl[dl], axis=(1, 2)) for dl in range(6))
    near = jnp.stack(by_dl[6:9], axis=1).reshape(H, 3 * CHUNK * CHUNK)
    g = jnp.dot(near, _rel_onehot(), precision=lax.Precision.HIGHEST)
    return g.at[:, N_REL - 1].add(far)


def _attn_specs(nblk, W):
    last = nblk - 1
    q_spec = pl.BlockSpec((BQ, W), lambda g, i: (jnp.minimum(i, last), g))
    kv_specs = [pl.BlockSpec((BQ, 2 * W), functools.partial(
        lambda g, i, w: (jnp.maximum(jnp.minimum(i, last) - (N_WIN - 1) + w, 0), g), w=w)) for w in range(N_WIN)]
    tab_spec = pl.BlockSpec((None, HEADS_PER_STEP, BQ, N_WIN * BQ),
                            lambda g, i: (jnp.minimum(i, N_WIN - 1), g, 0, 0))
    dtab_spec = pl.BlockSpec((HEADS_PER_STEP, BQ, N_WIN * BQ), lambda g, i: (g, 0, 0))
    return q_spec, kv_specs, tab_spec, dtab_spec


def _attn_exp(q_ref, kT, tab_ref, h, dh):
    s = jnp.dot(q_ref[:, h * dh:(h + 1) * dh], kT[h * dh:(h + 1) * dh, :], preferred_element_type=F32) + tab_ref[h]
    e = jnp.exp(s - jnp.max(s, axis=-1, keepdims=True))
    return e, jnp.sum(e, axis=-1, keepdims=True)


def _attn_fwd(q, kv, tab, name):
    S, D = q.shape
    dh = D // N_HEADS
    W = HEADS_PER_STEP * dh
    assert 2 * W == D, "the kv layout puts one head group's k beside its v: two head groups"
    q_spec, kv_specs, tab_spec, _ = _attn_specs(S // BQ, W)

    def body(q_ref, *rest):
        tab_ref, o_ref = rest[N_WIN], rest[N_WIN + 1]
        kvw = jnp.concatenate([r[...] for r in rest[:N_WIN]], axis=0)
        kT = kvw[:, :W].T
        vw = kvw[:, W:]
        outs = []
        for h in range(HEADS_PER_STEP):
            e, l = _attn_exp(q_ref, kT, tab_ref, h, dh)
            outs.append(jnp.dot(e.astype(BF16), vw[:, h * dh:(h + 1) * dh], preferred_element_type=F32) / l)
        o_ref[...] = jnp.concatenate(outs, axis=1).astype(BF16)

    return pl.pallas_call(
        body, name=name, grid=(N_HEADS // HEADS_PER_STEP, S // BQ),
        in_specs=[q_spec] + kv_specs + [tab_spec],
        out_specs=q_spec,
        out_shape=jax.ShapeDtypeStruct((S, D), BF16),
        compiler_params=_params("parallel", "parallel"),
    )(q, *([kv] * N_WIN), tab)


def _attn_bwd(q, kv, tab, do, name):
    S, D = q.shape
    dh = D // N_HEADS
    W = HEADS_PER_STEP * dh
    nblk = S // BQ
    q_spec, kv_specs, tab_spec, dtab_spec = _attn_specs(nblk, W)

    def body(q_ref, *rest):
        tab_ref, do_ref, dq_ref, dkv_ref, dtab_ref, ring = rest[N_WIN:]
        i = pl.program_id(1)

        @pl.when(i == 0)
        def _():
            dtab_ref[...] = jnp.zeros_like(dtab_ref)
            ring[...] = jnp.zeros_like(ring)

        @pl.when(i < nblk)
        def _():
            kvw = jnp.concatenate([r[...] for r in rest[:N_WIN]], axis=0)
            kT = kvw[:, :W].T
            vT = kvw[:, W:].T
            qT = q_ref[...].T
            doT = do_ref[...].T
            dqs, dks, dvs = [], [], []
            for h in range(HEADS_PER_STEP):
                hd = slice(h * dh, (h + 1) * dh)
                e, l = _attn_exp(q_ref, kT, tab_ref, h, dh)
                p = e * (1.0 / l)
                dp = jnp.dot(do_ref[:, hd], vT[hd, :], preferred_element_type=F32)
                ds = p * (dp - jnp.sum(p * dp, axis=-1, keepdims=True))
                dtab_ref[h] += ds
                dsb = ds.astype(BF16)
                dqs.append(lax.dot_general(kT[hd, :], dsb, _DIMS["nt"], preferred_element_type=F32) * (dh ** -0.5))
                dks.append(jnp.dot(qT[hd, :], dsb, preferred_element_type=F32))
                dvs.append(jnp.dot(doT[hd, :], p.astype(BF16), preferred_element_type=F32))
            dq_ref[...] = jnp.concatenate(dqs, axis=0).T.astype(BF16)
            dkv = jnp.concatenate(dks + dvs, axis=0).T
            for w in range(N_WIN):
                slot = lax.rem(i + 1 + w, N_WIN)
                part = dkv[w * BQ:(w + 1) * BQ, :]
                if w == N_WIN - 1:
                    ring[slot] = part
                else:
                    ring[slot] += part

        dkv_ref[...] = ring[lax.rem(i + 1, N_WIN)].astype(BF16)

    done_spec = pl.BlockSpec((BQ, 2 * W), lambda g, i: (jnp.maximum(i - (N_WIN - 1), 0), g))
    return pl.pallas_call(
        body, name=name, grid=(N_HEADS // HEADS_PER_STEP, nblk + N_WIN - 1),
        in_specs=[q_spec] + kv_specs + [tab_spec, q_spec],
        out_specs=[q_spec, done_spec, dtab_spec],
        out_shape=[jax.ShapeDtypeStruct((S, D), BF16), jax.ShapeDtypeStruct((S, 2 * D), BF16),
                   jax.ShapeDtypeStruct(tab.shape[1:], F32)],
        scratch_shapes=[pltpu.VMEM((N_WIN, BQ, 2 * W), F32)],
        compiler_params=_params("parallel", "arbitrary"),
    )(q, *([kv] * N_WIN), tab, do)


def _adamw(w, g, m, v, name):
    shape = w.shape
    C = shape[-1]
    R = int(np.prod(shape[:-1])) if len(shape) > 1 else 1
    whole = len(shape) >= 2 and R * C <= SMALL_TENSOR_ELEMS
    if whole:
        w2, g2, m2, v2 = w, g, m, v
    else:
        w2, g2, m2, v2 = (t.reshape(R, C) for t in (w, g, m, v))
    tr = _pick(R, max(8, (512 * 1024) // C // 8 * 8), 8)

    def body(w_ref, g_ref, m_ref, v_ref, d_ref, nm_ref, nv_ref):
        gv = g_ref[...]
        nm = ADAM_B1 * m_ref[...] + (1.0 - ADAM_B1) * gv
        nv = ADAM_B2 * v_ref[...] + (1.0 - ADAM_B2) * jnp.square(gv)
        m_hat = nm / (1.0 - ADAM_B1 ** ADAM_STEP)
        v_hat = nv / (1.0 - ADAM_B2 ** ADAM_STEP)
        d_ref[...] = -ADAM_LR * (m_hat / (jnp.sqrt(v_hat) + ADAM_EPS) + ADAM_WD * w_ref[...])
        nm_ref[...] = nm
        nv_ref[...] = nv

    if whole:
        spec, grid = pl.BlockSpec(shape, lambda i: (0,) * len(shape)), (1,)
    else:
        spec, grid = pl.BlockSpec((tr, C), lambda i: (i, 0)), (R // tr,)
    outs = pl.pallas_call(
        body, name=name, grid=grid,
        in_specs=[spec] * 4, out_specs=[spec] * 3,
        out_shape=[jax.ShapeDtypeStruct(w2.shape, F32)] * 3,
        compiler_params=_params("parallel"),
    )(w2, g2, m2, v2)
    return tuple(o.reshape(shape) for o in outs)


def _sum_rows(a, name):
    n, L = a.shape

    def body(a_ref, o_ref):
        acc = a_ref[0:1, :]
        for r in range(1, n):
            acc = acc + a_ref[r:r + 1, :]
        o_ref[...] = acc

    return pl.pallas_call(
        body, name=name, grid=(1,),
        in_specs=[pl.BlockSpec((n, L), lambda i: (0, 0))],
        out_specs=pl.BlockSpec((1, L), lambda i: (0, 0)),
        out_shape=jax.ShapeDtypeStruct((1, L), F32),
        compiler_params=_params("arbitrary"),
    )(a)


def _scalar_call(body, name, scalar, grid, in_specs, out_spec, out_shape, args):
    return pl.pallas_call(
        body, name=name,
        grid_spec=pltpu.PrefetchScalarGridSpec(num_scalar_prefetch=1, grid=grid, in_specs=in_specs,
                                               out_specs=out_spec),
        out_shape=out_shape, compiler_params=_params("parallel"),
    )(jnp.reshape(scalar, (-1,)).astype(jnp.int32), *args)


def _pair_sum(view, got, c, name):
    nb, _, rh, cols = view.shape
    tr = _pick(rh, max(16, (1 << 20) // cols // 16 * 16), 16)
    bpr = rh // tr

    def body(s_ref, a_ref, b_ref, o_ref):
        o_ref[...] = (a_ref[...].astype(F32) + b_ref[...].astype(F32)).astype(BF16)

    spec = pl.BlockSpec((tr, cols), lambda i, s: (i, 0))
    mine = pl.BlockSpec((tr, cols), lambda i, s: ((2 * (i // bpr) + s[0]) * bpr + i % bpr, 0))
    return _scalar_call(body, name, c, (nb * bpr,), [mine, spec], spec,
                        jax.ShapeDtypeStruct((nb * rh, cols), BF16),
                        (view.reshape(nb * 2 * rh, cols), got.reshape(nb * rh, cols)))


STACKED_LAYERS = 2


def _owner_sum(pair, recv, me, c, it, name, layer=None, into=None):
    _, rh, bc = recv.shape
    tr = _pick(rh, max(16, (1 << 19) // bc // 16 * 16), 16)
    bpr = rh // tr

    def body(s_ref, a_ref, r0, r1, r2, *rest):
        rest[-1][...] = ((a_ref[...].astype(F32) + r0[...].astype(F32)) + r1[...].astype(F32)) + r2[...].astype(F32)

    if it.kind == "col":
        own = pl.BlockSpec((tr, bc), lambda i, s: (i, s[0]))
    else:
        own = pl.BlockSpec((tr, bc), lambda i, s: (s[0] * bpr + i, 0))
    slots = [pl.BlockSpec((None, tr, bc), functools.partial(lambda i, s, k: (k, i, 0), k=k)) for k in range(3)]
    in_specs, args, aliases = [own] + slots, [pair, recv, recv, recv], {}
    if layer is None:
        out_spec = pl.BlockSpec((tr, bc), lambda i, s: (s[1] * bpr + i, 0))
        out_shape = jax.ShapeDtypeStruct((2 * rh, bc), F32)
    else:
        out_spec = pl.BlockSpec((None, tr, bc), lambda i, s: (layer, s[1] * bpr + i, 0))
        out_shape = jax.ShapeDtypeStruct((STACKED_LAYERS, 2 * rh, bc), F32)
        if into is not None:
            in_specs.append(pl.BlockSpec(memory_space=pl.ANY))
            args.append(into)
            aliases = {len(args): 0}
    return pl.pallas_call(
        body, name=name,
        grid_spec=pltpu.PrefetchScalarGridSpec(num_scalar_prefetch=1, grid=(bpr,), in_specs=in_specs,
                                               out_specs=out_spec),
        out_shape=out_shape, input_output_aliases=aliases, compiler_params=_params("parallel"),
    )(jnp.stack([it.pos(me), c]).astype(jnp.int32), *args)


def _place():
    x, y, c = lax.axis_index("x"), lax.axis_index("y"), lax.axis_index("c")
    chips = [(1 - x, y), (x, 1 - y), (1 - x, 1 - y)]
    return x, y, c, chips


def _chip_index(px, py):
    return 2 * px + py


def _all_gather_small(x_shard, name):
    m_per, n = x_shard.shape

    def body(x_ref, out_ref, send_sems, recv_sems, local_sem):
        x, y, c, chips = _place()
        me, sibling = (x, y, c), (x, y, 1 - c)

        def rows(px, py, pc):
            return out_ref.at[pl.ds((4 * px + 2 * py + pc) * m_per, m_per), :]

        def copy(k, block, to, src=None):
            return pltpu.make_async_remote_copy(
                src_ref=rows(*block) if src is None else src, dst_ref=rows(*block),
                send_sem=send_sems.at[k], recv_sem=recv_sems.at[k], device_id=to, device_id_type=MESH)

        mine = pltpu.make_async_copy(x_ref, rows(*me), local_sem)
        mine.start()
        first = [copy(0, me, sibling, src=x_ref)]
        first += [copy(1 + j, me, (*chip, c), src=x_ref) for j, chip in enumerate(chips)]
        for cp in first:
            cp.start()
        passed = [copy(4 + j, (*chip, c), sibling) for j, chip in enumerate(chips)]
        for j, chip in enumerate(chips):
            copy(1 + j, (*chip, c), me).wait_recv()
            passed[j].start()
        copy(0, sibling, me).wait_recv()
        for j, chip in enumerate(chips):
            copy(4 + j, (*chip, 1 - c), me).wait_recv()
        for cp in first + passed:
            cp.wait_send()
        mine.wait()

    return pl.pallas_call(
        body, name=name,
        out_shape=jax.ShapeDtypeStruct((N_DEV * m_per, n), x_shard.dtype),
        in_specs=[pl.BlockSpec(memory_space=pltpu.VMEM)],
        out_specs=pl.BlockSpec(memory_space=pltpu.VMEM),
        scratch_shapes=[pltpu.SemaphoreType.DMA((7,)), pltpu.SemaphoreType.DMA((7,)), pltpu.SemaphoreType.DMA],
    )(x_shard)


def _gather_flat(vec, name):
    L = vec.shape[0]
    Lp = -(-L // 1024) * 1024
    g = _all_gather_small(jnp.pad(vec, (0, Lp - L)).reshape(8, Lp // 8), name)
    return g.reshape(N_DEV, Lp)[:, :L]


class _Item:
    def __init__(self, kind, rows, cols, arg, layer, swap=False):
        self.kind, self.rows, self.cols, self.arg, self.layer, self.swap = kind, rows, cols, arg, layer, swap

    def ref(self, refs):
        return refs[self.arg].at[self.layer]

    def pos(self, j):
        return 2 * (j % 2) + j // 2 if self.swap else j


def _block(ref, it, j, half):
    if it.kind == "col":
        ns = it.cols // N_CHIP
        return ref.at[pl.ds(half * (it.rows // 2), it.rows // 2), pl.ds(it.pos(j) * ns, ns)]
    rs = it.rows // N_CHIP
    return ref.at[pl.ds(j * rs + half * (rs // 2), rs // 2), :]


def _cast_place(w, layer, kind, pos, after, name):
    _, r, n = w.shape
    tr = _pick(r, max(16, (1 << 20) // n // 16 * 16), 16)
    bpr = r // tr

    def body(s_ref, w_ref, after_ref, o_ref):
        o_ref[...] = w_ref[...].astype(BF16)

    if kind == "col":
        full, out_idx = (1, r, N_CHIP * n), (lambda i, s: (0, i, s[0]))
    else:
        full, out_idx = (1, N_CHIP * r, n), (lambda i, s: (0, s[0] * bpr + i, 0))
    return pl.pallas_call(
        body, name=name,
        grid_spec=pltpu.PrefetchScalarGridSpec(
            num_scalar_prefetch=1, grid=(bpr,),
            in_specs=[pl.BlockSpec((None, tr, n), lambda i, s: (layer, i, 0)), pl.BlockSpec(memory_space=pl.ANY)],
            out_specs=pl.BlockSpec((None, tr, n), out_idx)),
        out_shape=jax.ShapeDtypeStruct(full, BF16),
        compiler_params=_params("parallel"),
    )(jnp.reshape(pos, (1,)).astype(jnp.int32), w, after)


HBM_SPEC = pl.BlockSpec(memory_space=pltpu.HBM)
SEM_SPEC = pl.BlockSpec(memory_space=pltpu.SEMAPHORE)
ANY_SPEC = pl.BlockSpec(memory_space=pl.ANY)
SPLIT_PARAMS = dict(has_side_effects=pltpu.SideEffectType.DATAFLOW_SIDE_EFFECTING)


def _in_hbm(a):
    return pltpu.with_memory_space_constraint(a, pltpu.HBM)


def _split_start(copies_of, bufs, n_sem, after, name):
    n = len(bufs)

    def body(*refs):
        ins, send, recv, token = refs[:n], refs[n + 1], refs[n + 2], refs[2 * n + 3]
        for cp in copies_of(ins, send, recv, False)[0]:
            cp.start()
        token[...] = jnp.zeros_like(token)

    outs = pl.pallas_call(
        body, name=name,
        out_shape=(pltpu.SemaphoreType.DMA(n_sem), pltpu.SemaphoreType.DMA(n_sem),
                   *[pltpu.HBM(b.shape, b.dtype) for b in bufs], jax.ShapeDtypeStruct((8, 128), F32)),
        in_specs=[HBM_SPEC] * n + [ANY_SPEC],
        out_specs=(SEM_SPEC, SEM_SPEC, *[HBM_SPEC] * n, pl.BlockSpec(memory_space=pltpu.VMEM)),
        input_output_aliases={t: 2 + t for t in range(n)},
        compiler_params=pltpu.CompilerParams(**SPLIT_PARAMS),
    )(*[_in_hbm(b) for b in bufs], after)
    return outs[0], outs[1], list(outs[2:2 + n]), outs[2 + n]


def _split_wait(copies_of, send, recv, bufs, after, name):
    n = len(bufs)

    def body(*refs):
        ins, send_ref, recv_ref = refs[:n], refs[n], refs[n + 1]
        sends, arrivals = copies_of(ins, send_ref, recv_ref, True)
        for cp in sends:
            cp.wait_send()
        for cp in arrivals:
            cp.wait_recv()

    return pl.pallas_call(
        body, name=name,
        out_shape=[pltpu.HBM(b.shape, b.dtype) for b in bufs],
        in_specs=[HBM_SPEC] * n + [SEM_SPEC, SEM_SPEC, ANY_SPEC],
        out_specs=[HBM_SPEC] * n,
        input_output_aliases={t: t for t in range(n)},
        compiler_params=pltpu.CompilerParams(**SPLIT_PARAMS),
    )(*bufs, send, recv, after)


def _gather_copies(items):
    def copies_of(refs, send, recv, with_arrivals):
        x, y, c, chips = _place()
        me = _chip_index(x, y)
        sends, arrivals = [], []
        for t, it in enumerate(items):
            for k, chip in enumerate(chips):
                for core in range(2):
                    mine = _block(it.ref(refs), it, me, c)
                    sends.append(pltpu.make_async_remote_copy(
                        src_ref=mine, dst_ref=mine, send_sem=send.at[6 * t + 2 * k + core],
                        recv_sem=recv.at[6 * t + 2 * k + c], device_id=(*chip, core), device_id_type=MESH))
                    if with_arrivals:
                        landed = _block(it.ref(refs), it, _chip_index(*chip), core)
                        arrivals.append(pltpu.make_async_remote_copy(
                            src_ref=landed, dst_ref=landed, send_sem=send.at[6 * t + 2 * k + core],
                            recv_sem=recv.at[6 * t + 2 * k + core], device_id=(*chip, core), device_id_type=MESH))
        return sends, arrivals

    return copies_of


def _owner_copies(items):
    n = len(items)

    def blk(ref, it, j):
        if it.kind == "col":
            ns = it.cols // N_CHIP
            return ref.at[:, pl.ds(it.pos(j) * ns, ns)]
        return ref.at[j]

    def copies_of(refs, send, recv, with_arrivals):
        x, y, c, chips = _place()
        sends, arrivals = [], []
        for t, it in enumerate(items):
            for k, chip in enumerate(chips):
                slot = refs[n + t].at[k]
                sends.append(pltpu.make_async_remote_copy(
                    src_ref=blk(refs[t], it, _chip_index(*chip)), dst_ref=slot, send_sem=send.at[3 * t + k],
                    recv_sem=recv.at[3 * t + k], device_id=(*chip, c), device_id_type=MESH))
                if with_arrivals:
                    arrivals.append(pltpu.make_async_remote_copy(
                        src_ref=slot, dst_ref=slot, send_sem=send.at[3 * t + k], recv_sem=recv.at[3 * t + k],
                        device_id=(*chip, c), device_id_type=MESH))
        return sends, arrivals

    return copies_of


def _owner_slot_shape(it):
    if it.kind == "col":
        return (3, it.rows // 2, it.cols // N_CHIP)
    return (3, it.rows // (2 * N_CHIP), it.cols)


def _pair_view(g, it):
    if it.kind == "col":
        return g.reshape(1, 2, it.rows // 2, it.cols)
    return g.reshape(N_CHIP, 2, it.rows // (2 * N_CHIP), it.cols)


def _pair_copies(n):
    def copies_of(refs, send, recv, with_arrivals):
        x, y, c, _ = _place()
        sends, arrivals = [], []
        for t in range(n):
            land = refs[n + t]
            sends.append(pltpu.make_async_remote_copy(
                src_ref=refs[t].at[:, pl.ds(1 - c, 1)], dst_ref=land, send_sem=send.at[t], recv_sem=recv.at[t],
                device_id=(x, y, 1 - c), device_id_type=MESH))
            if with_arrivals:
                arrivals.append(pltpu.make_async_remote_copy(
                    src_ref=land, dst_ref=land, send_sem=send.at[t], recv_sem=recv.at[t],
                    device_id=(x, y, 1 - c), device_id_type=MESH))
        return sends, arrivals

    return copies_of


def _half_copies(n):
    def half(ref, which):
        r2 = ref.shape[-2] // 2
        rows = pl.ds(which * r2, r2)
        return ref.at[rows, :] if len(ref.shape) == 2 else ref.at[:, rows, :]

    def copies_of(refs, send, recv, with_arrivals):
        x, y, c, _ = _place()
        sends, arrivals = [], []
        for t in range(n):
            mine = half(refs[t], c)
            sends.append(pltpu.make_async_remote_copy(
                src_ref=mine, dst_ref=mine, send_sem=send.at[t], recv_sem=recv.at[t],
                device_id=(x, y, 1 - c), device_id_type=MESH))
            if with_arrivals:
                theirs = half(refs[t], 1 - c)
                arrivals.append(pltpu.make_async_remote_copy(
                    src_ref=theirs, dst_ref=theirs, send_sem=send.at[t], recv_sem=recv.at[t],
                    device_id=(x, y, 1 - c), device_id_type=MESH))
        return sends, arrivals

    return copies_of


class _Reduction:
    pass


def _pair_start(grads, items, after, tag, names, layer=None):
    n = len(items)
    views = [_pair_view(g, it) for g, it in zip(grads, items)]
    lands = [lax.empty((v.shape[0], 1) + v.shape[2:], v.dtype) for v in views]
    r = _Reduction()
    r.items, r.tag, r.names, r.layer = items, tag, names, layer
    r.send, r.recv, r.bufs, r.token = _split_start(_pair_copies(n), views + lands, (n,), after, f"rs_pair_start_{tag}")
    return r


def _owner_start(r, after):
    x, y, c, _ = _place()
    n = len(r.items)
    bufs = _split_wait(_pair_copies(n), r.send, r.recv, r.bufs, after, f"rs_pair_wait_{r.tag}")
    pairs = [_pair_sum(bufs[t], bufs[n + t], c, f"rs_pair_sum_{r.tag}_{t}") for t in range(n)]
    shaped = [p if it.kind == "col" else p.reshape(N_CHIP, p.shape[0] // N_CHIP, p.shape[1])
              for p, it in zip(pairs, r.items)]
    lands = [lax.empty(_owner_slot_shape(it), BF16) for it in r.items]
    r.send, r.recv, r.bufs, r.token = _split_start(
        _owner_copies(r.items), shaped + lands, (3 * n,), r.token, f"rs_owner_start_{r.tag}")
    return r


def _reduce_finish(groups, after):
    x, y, c, _ = _place()
    me = _chip_index(x, y)
    halves = {}
    for r in groups:
        n = len(r.items)
        bufs = _split_wait(_owner_copies(r.items), r.send, r.recv, r.bufs, after, f"rs_owner_wait_{r.tag}")
        for t, (it, nm) in enumerate(zip(r.items, r.names)):
            pair = bufs[t].reshape(-1, bufs[t].shape[-1])
            halves[nm] = _owner_sum(pair, bufs[n + t], me, c, it, f"rs_owner_sum_{r.tag}_{t}",
                                    layer=r.layer, into=halves.get(nm))
    n = len(halves)
    return list(halves), _split_start(_half_copies(n), list(halves.values()), (n,), after, "rs_half_start")


def _silu(v):
    return v * jax.nn.sigmoid(v)


def _sum8(p):
    return jnp.sum(p, axis=-2)


def kernel(x, c, mod_w, mod_b, norm_g, ffn_w_in, ffn_w_out, conv_w_in, conv_k, conv_w_out, kv_mod_w, kv_mod_b, kv_norm_g, w_kv, attn_w_q, attn_w_o, rel_bias, loss_target, m_mod_w, m_mod_b, m_norm_g, m_ffn_w_in, m_ffn_w_out, m_conv_w_in, m_conv_k, m_conv_w_out, m_kv_mod_w, m_kv_mod_b, m_kv_norm_g, m_w_kv, m_attn_w_q, m_attn_w_o, m_rel_bias, v_mod_w, v_mod_b, v_norm_g, v_ffn_w_in, v_ffn_w_out, v_conv_w_in, v_conv_k, v_conv_w_out, v_kv_mod_w, v_kv_mod_b, v_kv_norm_g, v_w_kv, v_attn_w_q, v_attn_w_o, v_rel_bias):
    xi, yi, ci = lax.axis_index("x"), lax.axis_index("y"), lax.axis_index("c")
    chip = 2 * xi + yi
    dev = 2 * chip + ci
    _, S, D = x.shape
    F = ffn_w_out.shape[1] * N_CHIP
    x0 = x.reshape(S, D)
    target = loss_target.reshape(S, D)
    n_mod = mod_w.shape[2]
    n_kvm = kv_mod_w.shape[1]
    dsh = D // N_CHIP
    TF = F // 2

    c_all = _all_gather_small(c.reshape(8, D // 8), "ag_c").reshape(N_DEV, D)
    sc16 = jnp.pad(_silu(c_all), ((0, 8), (0, 0)))
    part = [_mm(sc16, mod_w, "nn", F32, f"mod_fwd_{l}", b_layer=l)[:8] for l in range(2)]
    part.append(_mm(sc16, kv_mod_w, "nn", F32, "mod_fwd_kv")[:8])
    fwd_vec = jnp.concatenate([p.reshape(-1) for p in part] + [norm_g.reshape(-1), conv_k.reshape(-1)])
    fwd_all = _gather_flat(fwd_vec, "ag_fwd_small")[0::2]
    o = 0
    mods = []
    for n in (n_mod, n_mod, n_kvm):
        blk = fwd_all[:, o:o + 8 * n].reshape(N_CHIP, 8, n)
        mods.append(lax.dynamic_index_in_dim(blk, dev, axis=1, keepdims=False).reshape(N_CHIP * n))
        o += 8 * n
    ng = fwd_all[:, o:o + 8 * dsh].reshape(N_CHIP, 2, 4, dsh).transpose(1, 2, 0, 3).reshape(2, 4, D)
    o += 8 * dsh
    ck = fwd_all[:, o:o + 3 * dsh].reshape(N_CHIP, 3, dsh).transpose(1, 0, 2).reshape(3, D)
    ck8 = jnp.pad(ck, ((0, 5), (0, 0)))
    mod = [mods[l] + mod_b[l] for l in range(2)]
    sh1, sc1, g1, sh2, sc2, g2 = zip(*[jnp.split(m, 6) for m in mod])
    kv_sh, kv_sc = jnp.split(mods[2] + kv_mod_b, 2)
    row = lambda v: v.reshape(1, D)

    it_conv = [_Item("col", D, 3 * D, 0, 0), _Item("row", D, D, 1, 0)]
    it_ffn = [_Item("col", D, 2 * F, 0, 0, swap=True), _Item("row", F, D, 1, 0)]
    it_attn = [_Item("col", D, 2 * D, 0, 0, swap=True), _Item("row", D, D, 1, 0), _Item("row", D, D, 2, 0)]

    def placed(w, layer, it, nm, after=fwd_all):
        return _cast_place(w, layer, it.kind, it.pos(chip), after, f"place_{nm}")

    flying = {}

    def start(tag, its, bufs, after):
        send, recv, bufs, tok = _split_start(_gather_copies(its), bufs, (6 * len(its),), after, f"ag_start_{tag}")
        flying[tag] = (its, send, recv, bufs)
        return tok

    def arrived(tag, after):
        its, send, recv, bufs = flying[tag]
        return _split_wait(_gather_copies(its), send, recv, bufs, after, f"ag_wait_{tag}")

    one = lambda it: [_Item(it.kind, it.rows, it.cols, 0, 0, it.swap)]
    tok = start("conv_in", one(it_conv[0]), [placed(conv_w_in, 0, it_conv[0], "conv_w_in")], fwd_all)
    tok = start("conv_out", one(it_conv[1]), [placed(conv_w_out, 0, it_conv[1], "conv_w_out", tok)], tok)
    tok = start("ffn0_in", one(it_ffn[0]), [placed(ffn_w_in, 0, it_ffn[0], "ffn_w_in0", tok)], tok)
    tok = start("ffn0_out", one(it_ffn[1]), [placed(ffn_w_out, 0, it_ffn[1], "ffn_w_out0", tok)], tok)
    tok = start("attn", it_attn, [placed(w_kv[None], 0, it_attn[0], "w_kv", tok),
                                  placed(attn_w_q, 0, it_attn[1], "attn_w_q", tok),
                                  placed(attn_w_o, 0, it_attn[2], "attn_w_o", tok)], tok)
    token = start("ffn1", it_ffn, [placed(ffn_w_in, 1, it_ffn[0], "ffn_w_in1", tok),
                                   placed(ffn_w_out, 1, it_ffn[1], "ffn_w_out1", tok)], tok)

    a1 = row(ng[0, 0] * (1.0 + sc1[0])) + token[0, 0]
    (h1,) = _norm_mod(x0, a1, row(sh1[0]), "l0_norm1")
    tab = _bias_table(rel_bias[0], "l1_bias_table")
    h1, tab = lax.optimization_barrier((h1, tab))
    (W_cin,) = arrived("conv_in", h1)
    bcx = _mm(h1, W_cin, "nn", BF16, "l0_conv_in", b_layer=0)
    ug = _conv_gate(bcx, ck8, "l0_conv_gate")
    gt1 = row(g1[0] * ng[0, 1])
    a2 = row(ng[0, 2] * (1.0 + sc2[0]))
    (W_cout,) = arrived("conv_out", ug)
    y1, x1, h2 = _mm_post(ug, W_cout, x0, gt1, "l0_conv_out", scales=a2, shifts=row(sh2[0]))
    (W_fin0,) = arrived("ffn0_in", h2)
    gu0, act0 = _ffn_in_act(h2, W_fin0, 0, "l0_ffn_in")
    (W_fout0,) = arrived("ffn0_out", act0)
    gt2 = row(g2[0] * ng[0, 3])
    a3 = ng[1, 0] * (1.0 + sc1[1])
    akv = kv_norm_g * (1.0 + kv_sc)
    y2, x2, h3, hkv = _mm_post(act0, W_fout0, x1, gt2, "l0_ffn_out",
                               scales=jnp.stack([a3, akv]), shifts=jnp.stack([sh1[1], kv_sh]))
    W_kv, W_q, W_o = arrived("attn", hkv)
    kvp = _mm(hkv, W_kv, "nn", BF16, "l1_kv", b_layer=0)
    att_scale = (D // N_HEADS) ** -0.5
    assert math.log2(att_scale) % 1 == 0, "scaling q before its bf16 cast is exact only for a power of two"
    qp = _mm(h3, W_q, "nn", BF16, "l1_q", b_layer=0, scale=att_scale)
    oh = _attn_fwd(qp, kvp, tab, "l1_attn")
    gt3 = row(g1[1] * ng[1, 1])
    a4 = row(ng[1, 2] * (1.0 + sc2[1]))
    y3, x3, h4 = _mm_post(oh, W_o, x2, gt3, "l1_attn_out", scales=a4, shifts=row(sh2[1]))
    W_fin1, W_fout1 = arrived("ffn1", h4)
    gu1, act1 = _ffn_in_act(h4, W_fin1, 0, "l1_ffn_in")
    gt4 = row(g2[1] * ng[1, 3])
    dx4, sq, dy4, dgt4 = _mm_post(act1, W_fout1, x3, gt4, "l1_ffn_out", target=target)
    loss_part = 0.5 * jnp.sum(sq) / D

    def ffn_bwd(dy, dxn, xin_, h, gu, act, a, w_in, w_out, post, tag):
        dgu, dx, ds, db, dyn, dgt = _ffn_bwd(dy, w_out, gu, w_in, xin_, dxn, a, post, f"{tag}_ffn_bwd")
        g_fout = _mm(act, dy, "tn", BF16, f"{tag}_ffn_out_dw", tm=TF)
        g_fin = _mm(h, dgu, "tn", BF16, f"{tag}_ffn_in_dw", tn=TF)
        return dx, ds, db, dyn, dgt, g_fin, g_fout

    dx3, ds4, db4, dy3, dgt3, G_fin1, G_fout1 = ffn_bwd(dy4, dx4, x3, h4, gu1, act1, a4, W_fin1, W_fout1,
                                                        (y3, gt3), "l1")
    red = [_pair_start([G_fin1, G_fout1], it_ffn, token, "ffn1", ["ffn_w_in", "ffn_w_out"], layer=1)]
    doh = _mm(dy3, W_o, "nt", BF16, "l1_attn_out_dx", b_layer=0, after=red[0].token)
    G_o = _mm(oh, dy3, "tn", BF16, "l1_attn_out_dw")
    _owner_start(red[0], G_o)
    dq, dkv, dtab = _attn_bwd(qp, kvp, tab, doh, "l1_attn_bwd")
    d_rel = _bias_table_grad(dtab)
    G_q = _mm(h3, dq, "tn", BF16, "l1_q_dw")
    G_kv = _mm(hkv, dkv, "tn", BF16, "l1_kv_dw")
    red.append(_pair_start([G_kv, G_q, G_o], it_attn, red[-1].token, "attn", ["w_kv", "attn_w_q", "attn_w_o"]))
    dx2, ds3, db3, dy2, dgt2 = _mm_pre_bwd([(dq, W_q), (dkv, W_kv)], x2, dx3,
                                           jnp.stack([a3, akv]) + red[1].token[0, 0], "l1_qkv_dx", post=(y2, gt2))
    _owner_start(red[1], dx2)

    dx1, ds2, db2, dy1, dgt1, G_fin0, G_fout0 = ffn_bwd(dy2, dx2, x1, h2, gu0, act0, a2, W_fin0, W_fout0,
                                                        (y1, gt1), "l0")
    red.append(_pair_start([G_fin0, G_fout0], it_ffn, red[-1].token, "ffn0", ["ffn_w_in", "ffn_w_out"], layer=0))
    dug = _mm(dy1, W_cout, "nt", BF16, "l0_conv_out_dx", b_layer=0, after=red[2].token)
    G_cout = _mm(ug, dy1, "tn", BF16, "l0_conv_out_dw")
    _owner_start(red[2], G_cout)
    dbcx, dck = _conv_gate_bwd(dug, bcx, ck8, "l0_conv_gate_bwd")
    G_cin = _mm(h1, dbcx, "tn", BF16, "l0_conv_in_dw")
    red.append(_pair_start([G_cin, G_cout], it_conv, red[-1].token, "conv", ["conv_w_in", "conv_w_out"]))
    dx0, ds1, db1 = _mm_pre_bwd([(dbcx, W_cin)], x0, dx1, a1 + red[3].token[0, 0], "l0_conv_in_dx")
    ds1, db1 = _sum8(ds1)[0], _sum8(db1)[0]
    da2, db2 = _sum8(ds2)[0], _sum8(db2)[0]
    ds3, db3 = _sum8(ds3), _sum8(db3)
    da4, db4 = _sum8(ds4)[0], _sum8(db4)[0]
    dgt1, dgt2, dgt3, dgt4 = _sum8(dgt1), _sum8(dgt2), _sum8(dgt3), _sum8(dgt4)

    def dmod_of(l, ds_a, db_a, dgt_a, ds_b, db_b, dgt_b):
        return jnp.concatenate([db_a, ds_a * ng[l, 0], dgt_a * ng[l, 1], db_b, ds_b * ng[l, 2], dgt_b * ng[l, 3]])

    dmod0 = dmod_of(0, ds1, db1, dgt1, da2, db2, dgt2)
    dmod1 = dmod_of(1, ds3[0], db3[0], dgt3, da4, db4, dgt4)
    dkvmod = jnp.concatenate([db3[1], ds3[1] * kv_norm_g])
    dng = jnp.stack([
        jnp.stack([ds1 * (1.0 + sc1[0]), dgt1 * g1[0], da2 * (1.0 + sc2[0]), dgt2 * g2[0]]),
        jnp.stack([ds3[0] * (1.0 + sc1[1]), dgt3 * g1[1], da4 * (1.0 + sc2[1]), dgt4 * g2[1]])])
    dkvng = ds3[1] * (1.0 + kv_sc)
    small = [dmod0, dmod1, dkvmod, dng.reshape(-1), dkvng, _sum8(dck).reshape(-1), d_rel.reshape(-1),
             loss_part.reshape(1)]
    sizes = [int(s.shape[0]) for s in small]
    offs = np.concatenate([[0], np.cumsum(sizes)])
    bwd_all = _gather_flat(jnp.concatenate(small), "ag_bwd_small")
    _owner_start(red[3], bwd_all)
    Lb = bwd_all.shape[1]
    Lp = -(-Lb // 128) * 128
    tot = _sum_rows(jnp.pad(bwd_all, ((0, 0), (0, Lp - Lb))), "sum_small")[0]
    seg = lambda i: tot[offs[i]:offs[i + 1]]
    g_mod_b = jnp.stack([seg(0), seg(1)])
    g_kv_mod_b = seg(2)
    g_norm_g = lax.dynamic_slice_in_dim(seg(3).reshape(2, 4, D), chip * dsh, dsh, axis=2)
    g_kv_norm_g = seg(4)
    g_conv_k = lax.dynamic_slice_in_dim(seg(5).reshape(1, 3, D), chip * dsh, dsh, axis=2)
    g_rel_bias = seg(6).reshape(rel_bias.shape)
    loss = seg(7)[0]

    def dmod_w(i, n, name):
        rows_ = lax.dynamic_slice_in_dim(bwd_all[:, offs[i]:offs[i + 1]], chip * n, n, axis=1)
        return _mm(sc16, jnp.pad(rows_, ((0, 8), (0, 0))), "tn", F32, name)

    g_mod_w = jnp.stack([dmod_w(0, n_mod, "mod_bwd_0"), dmod_w(1, n_mod, "mod_bwd_1")])
    g_kv_mod_w = dmod_w(2, n_kvm, "mod_bwd_kv")

    grads = {
        "mod_w": g_mod_w, "mod_b": g_mod_b, "norm_g": g_norm_g, "conv_k": g_conv_k,
        "kv_mod_w": g_kv_mod_w, "kv_mod_b": g_kv_mod_b, "kv_norm_g": g_kv_norm_g, "rel_bias": g_rel_bias,
    }
    weights = dict(mod_w=mod_w, mod_b=mod_b, norm_g=norm_g, ffn_w_in=ffn_w_in, ffn_w_out=ffn_w_out,
                   conv_w_in=conv_w_in, conv_k=conv_k, conv_w_out=conv_w_out, kv_mod_w=kv_mod_w,
                   kv_mod_b=kv_mod_b, kv_norm_g=kv_norm_g, w_kv=w_kv, attn_w_q=attn_w_q, attn_w_o=attn_w_o,
                   rel_bias=rel_bias)
    m_in = dict(mod_w=m_mod_w, mod_b=m_mod_b, norm_g=m_norm_g, ffn_w_in=m_ffn_w_in, ffn_w_out=m_ffn_w_out,
                conv_w_in=m_conv_w_in, conv_k=m_conv_k, conv_w_out=m_conv_w_out, kv_mod_w=m_kv_mod_w,
                kv_mod_b=m_kv_mod_b, kv_norm_g=m_kv_norm_g, w_kv=m_w_kv, attn_w_q=m_attn_w_q,
                attn_w_o=m_attn_w_o, rel_bias=m_rel_bias)
    v_in = dict(mod_w=v_mod_w, mod_b=v_mod_b, norm_g=v_norm_g, ffn_w_in=v_ffn_w_in, ffn_w_out=v_ffn_w_out,
                conv_w_in=v_conv_w_in, conv_k=v_conv_k, conv_w_out=v_conv_w_out, kv_mod_w=v_kv_mod_w,
                kv_mod_b=v_kv_mod_b, kv_norm_g=v_kv_norm_g, w_kv=v_w_kv, attn_w_q=v_attn_w_q,
                attn_w_o=v_attn_w_o, rel_bias=v_rel_bias)
    names = list(weights)
    step = {}

    def update(n):
        g = grads[n].reshape(weights[n].shape)
        step[n] = (g, *_adamw(weights[n], g, m_in[n], v_in[n], f"adamw_{n}"))

    update("mod_w")
    reduced, (half_send, half_recv, half_bufs, _) = _reduce_finish(red, step["mod_w"][1])
    for n in list(grads):
        if n not in step:
            update(n)
    grads.update(zip(reduced, _split_wait(
        _half_copies(len(half_bufs)), half_send, half_recv, half_bufs, step["kv_mod_w"][1], "rs_half_wait")))
    for n in names:
        if n not in step:
            update(n)
    return (loss, dx0.reshape(x.shape), *[step[n][k] for k in range(4) for n in names])
```

```python
import functools
import math

import numpy as np
import jax
import jax.numpy as jnp
from jax import lax
from jax.experimental import pallas as pl
from jax.experimental.pallas import tpu as pltpu

CHUNK = 64
N_LEFT_CHUNKS = 8
N_HEADS = 16
MAX_REL = 2 * CHUNK
N_REL = 2 * MAX_REL + 1
EPS = 1e-6
ADAM_LR = 0.001
ADAM_B1 = 0.9
ADAM_B2 = 0.999
ADAM_EPS = 1e-08
ADAM_WD = 0.01
ADAM_STEP = 10

Q_CHUNKS = 4
BQ = Q_CHUNKS * CHUNK
N_WIN = 1 + N_LEFT_CHUNKS // Q_CHUNKS
HEADS_PER_STEP = 8
NEG = -1e30
N_DEV = 8
N_CHIP = 4
SMALL_TENSOR_ELEMS = 1 << 16

BF16 = jnp.bfloat16
F32 = jnp.float32
V7X_VMEM_LIMIT_BYTES = 56 * 1024 * 1024
MESH = pl.DeviceIdType.MESH


def _pick(n, pref, align):
    t = min(pref, n)
    t -= t % align
    while t >= align:
        if n % t == 0:
            return t
        t -= align
    return n


def _params(*sem):
    return pltpu.CompilerParams(dimension_semantics=sem, vmem_limit_bytes=V7X_VMEM_LIMIT_BYTES)


def _colsum8(v):
    r, d = v.shape
    return v.reshape(r // 8, 8, d).sum(axis=0)


_DIMS = {"nn": (((1,), (0,)), ((), ())), "nt": (((1,), (1,)), ((), ())), "tn": (((0,), (0,)), ((), ()))}


def _mm(a, b, mode, out_dtype, name, *, b_layer=None, tm=1024, tn=1024, tk=None, scale=None, after=None):
    if tk is None:
        tk = 2048 if mode == "tn" else 3072
    bs = b.shape[1:] if b_layer is not None else b.shape
    if mode == "nn":
        (M, K), (K2, N) = a.shape, bs
    elif mode == "nt":
        (M, K), (N, K2) = a.shape, bs
    else:
        (K, M), (K2, N) = a.shape, bs
    assert K == K2, (name, a.shape, b.shape)
    tm = _pick(M, tm, 128 if mode == "tn" else 16)
    tn = _pick(N, tn, 128)
    tk = _pick(K, tk, 128 if mode != "tn" else 16)
    nk = K // tk
    assert scale is None or nk == 1, name
    dims = _DIMS[mode]
    extra = [] if after is None else [after]

    def body(a_ref, b_ref, *rest):
        o_ref, acc = rest[len(extra)], rest[len(extra) + 1:]
        p = lax.dot_general(a_ref[...].astype(BF16), b_ref[...].astype(BF16), dims,
                            preferred_element_type=F32)
        if nk == 1:
            o_ref[...] = (p if scale is None else p * scale).astype(o_ref.dtype)
        else:
            k = pl.program_id(2)

            @pl.when(k == 0)
            def _():
                acc[0][...] = p

            @pl.when(k > 0)
            def _():
                acc[0][...] += p

            @pl.when(k == nk - 1)
            def _():
                o_ref[...] = acc[0][...].astype(o_ref.dtype)

    a_spec = (pl.BlockSpec((tk, tm), lambda i, j, k: (k, i)) if mode == "tn"
              else pl.BlockSpec((tm, tk), lambda i, j, k: (i, k)))
    if mode == "nt":
        b_blk, b_idx = (tn, tk), (lambda i, j, k: (j, k))
    else:
        b_blk, b_idx = (tk, tn), (lambda i, j, k: (k, j))
    if b_layer is not None:
        b_spec = pl.BlockSpec((None,) + b_blk, lambda i, j, k: (b_layer,) + b_idx(i, j, k))
    else:
        b_spec = pl.BlockSpec(b_blk, b_idx)
    return pl.pallas_call(
        body, name=name,
        grid=(M // tm, N // tn, nk),
        in_specs=[a_spec, b_spec] + [pl.BlockSpec(memory_space=pl.ANY)] * len(extra),
        out_specs=pl.BlockSpec((tm, tn), lambda i, j, k: (i, j)),
        out_shape=jax.ShapeDtypeStruct((M, N), out_dtype),
        scratch_shapes=[pltpu.VMEM((tm, tn), F32)] if nk > 1 else [],
        compiler_params=_params("parallel", "parallel", "arbitrary"),
    )(a, b, *extra)


def _row_spec(tm, d):
    return pl.BlockSpec((tm, d), lambda i: (i, 0))


def _vec_spec(r, d):
    return pl.BlockSpec((r, d), lambda i: (0, 0))


def _norm_mod(x, scales, shifts, name):
    S, D = x.shape
    nb = scales.shape[0]
    tm = _pick(S, 512, 16)

    def body(x_ref, a_ref, b_ref, *o_refs):
        xv = x_ref[...]
        xh = xv * lax.rsqrt(jnp.mean(xv * xv, axis=-1, keepdims=True) + EPS)
        for n in range(nb):
            o_refs[n][...] = (xh * a_ref[n:n + 1, :] + b_ref[n:n + 1, :]).astype(BF16)

    return pl.pallas_call(
        body, name=name, grid=(S // tm,),
        in_specs=[_row_spec(tm, D), _vec_spec(nb, D), _vec_spec(nb, D)],
        out_specs=[_row_spec(tm, D)] * nb,
        out_shape=[jax.ShapeDtypeStruct((S, D), BF16)] * nb,
        compiler_params=_params("parallel"),
    )(x, scales, shifts)


def _mm_post(a, w, x, gate, name, *, scales=None, shifts=None, target=None, ffn_w_in=None):
    M, K = a.shape
    D = w.shape[2]
    ffn = ffn_w_in is not None
    tm = _pick(M, 256 if ffn else 512, 16)
    sub = _pick(tm, 256, 16)
    nb = 0 if scales is None else scales.shape[0]
    n_in = 4 + ffn + (2 if target is None else 1)
    if ffn:
        F2 = ffn_w_in.shape[2]
        PW = F2 // (2 * FFN_PAIRS)

    def body(*refs):
        a_ref, w_ref, x_ref, g_ref = refs[:4]
        outs = refs[n_in:]
        if ffn:
            wi_ref, gu_ref, act_ref = refs[4], outs[-2], outs[-1]
        if target is None:
            sc_ref, sh_ref = refs[n_in - 2:n_in]
            y_ref, xn_ref, h_refs = outs[0], outs[1], outs[2:2 + nb]
        else:
            t_ref = refs[n_in - 1]
            dx_ref, sq_ref, dy_ref, dg_ref = outs[:4]

            @pl.when(pl.program_id(0) == 0)
            def _():
                sq_ref[...] = jnp.zeros_like(sq_ref)
                dg_ref[...] = jnp.zeros_like(dg_ref)

        for r in range(tm // sub):
            rows = pl.ds(r * sub, sub)
            if ffn:
                y = jnp.zeros((sub, D), F32)
                for p in range(FFN_PAIRS):
                    gu = jnp.dot(a_ref[rows, :], wi_ref[:, 2 * p * PW:2 * (p + 1) * PW], preferred_element_type=F32)
                    gu_ref[rows, 2 * p * PW:2 * (p + 1) * PW] = gu.astype(BF16)
                    g = gu[:, :PW]
                    act = (g * jax.nn.sigmoid(g) * gu[:, PW:]).astype(BF16)
                    act_ref[rows, p * PW:(p + 1) * PW] = act
                    y = y + jnp.dot(act, w_ref[p * PW:(p + 1) * PW, :], preferred_element_type=F32)
                yb = y.astype(BF16)
            else:
                yb = jnp.dot(a_ref[rows, :], w_ref[...], preferred_element_type=F32).astype(BF16)
            yv = yb.astype(F32)
            yh = yv * lax.rsqrt(jnp.mean(yv * yv, axis=-1, keepdims=True) + EPS)
            xn = x_ref[rows, :] + yh * g_ref[...]
            if target is None:
                y_ref[rows, :] = yb
                xn_ref[rows, :] = xn
                xh = xn * lax.rsqrt(jnp.mean(xn * xn, axis=-1, keepdims=True) + EPS)
                for n in range(nb):
                    h_refs[n][rows, :] = (xh * sc_ref[n:n + 1, :] + sh_ref[n:n + 1, :]).astype(BF16)
            else:
                e = xn - t_ref[rows, :]
                dx = e / D
                dx_ref[rows, :] = dx
                sq_ref[...] += _colsum8(e * e)
                dy, dxy = _post_norm_grad(dx, yb, g_ref[...])
                dy_ref[rows, :] = dy.astype(BF16)
                dg_ref[...] += _colsum8(dxy)

    resident = dict(pipeline_mode=pl.Buffered(1)) if ffn else {}
    ins = [a, w, x, gate]
    in_specs = [_row_spec(tm, K), pl.BlockSpec((None,) + w.shape[1:], lambda i: (0, 0, 0), **resident),
                _row_spec(tm, D), _vec_spec(1, D)]
    if ffn:
        ins.append(ffn_w_in)
        in_specs.append(pl.BlockSpec((None,) + ffn_w_in.shape[1:], lambda i: (0, 0, 0), **resident))
    if target is None:
        ins += [scales, shifts]
        in_specs += [_vec_spec(nb, D), _vec_spec(nb, D)]
        out_specs = [_row_spec(tm, D)] * (2 + nb)
        out_shape = [jax.ShapeDtypeStruct((M, D), BF16), jax.ShapeDtypeStruct((M, D), F32)] \
            + [jax.ShapeDtypeStruct((M, D), BF16)] * nb
    else:
        ins += [target]
        in_specs += [_row_spec(tm, D)]
        out_specs = [_row_spec(tm, D), _vec_spec(8, D), _row_spec(tm, D), _vec_spec(8, D)]
        out_shape = [jax.ShapeDtypeStruct((M, D), F32), jax.ShapeDtypeStruct((8, D), F32),
                     jax.ShapeDtypeStruct((M, D), BF16), jax.ShapeDtypeStruct((8, D), F32)]
    if ffn:
        out_specs += [_row_spec(tm, F2), _row_spec(tm, F2 // 2)]
        out_shape += [jax.ShapeDtypeStruct((M, F2), BF16), jax.ShapeDtypeStruct((M, F2 // 2), BF16)]
    return pl.pallas_call(
        body, name=name, grid=(M // tm,), in_specs=in_specs, out_specs=out_specs, out_shape=out_shape,
        compiler_params=_params("arbitrary" if target is not None else "parallel"),
    )(*ins)


def _post_norm_grad(dxn, yb, gate):
    yv = yb.astype(F32)
    r = lax.rsqrt(jnp.mean(yv * yv, axis=-1, keepdims=True) + EPS)
    yh = yv * r
    dyh = dxn * gate
    return r * (dyh - yh * jnp.mean(dyh * yh, axis=-1, keepdims=True)), dxn * yh


def _mm_pre_bwd(pairs, x, dxn, scales, name, post=None, sub=256):
    S, D = x.shape
    nb = len(pairs)
    tm = _pick(S, 512, 16)
    sub = _pick(tm, sub, 16)

    def body(*refs):
        a_refs, w_refs = refs[0:2 * nb:2], refs[1:2 * nb:2]
        x_ref, d_ref, sc_ref = refs[2 * nb:2 * nb + 3]
        rest = refs[2 * nb + 3:]
        if post is not None:
            y_ref, g_ref, dx_ref, ds_ref, db_ref, dy_ref, dg_ref = rest
        else:
            dx_ref, ds_ref, db_ref = rest

        @pl.when(pl.program_id(0) == 0)
        def _():
            ds_ref[...] = jnp.zeros_like(ds_ref)
            db_ref[...] = jnp.zeros_like(db_ref)
            if post is not None:
                dg_ref[...] = jnp.zeros_like(dg_ref)

        for r in range(tm // sub):
            rows = pl.ds(r * sub, sub)
            xv = x_ref[rows, :]
            rr = lax.rsqrt(jnp.mean(xv * xv, axis=-1, keepdims=True) + EPS)
            xh = xv * rr
            dxh = jnp.zeros_like(xv)
            for n in range(nb):
                dh = lax.dot_general(a_refs[n][rows, :], w_refs[n][...], _DIMS["nt"], preferred_element_type=F32)
                dxh = dxh + dh * sc_ref[n:n + 1, :]
                ds_ref[n] += _colsum8(dh * xh)
                db_ref[n] += _colsum8(dh)
            dx = d_ref[rows, :] + rr * (dxh - xh * jnp.mean(dxh * xh, axis=-1, keepdims=True))
            dx_ref[rows, :] = dx
            if post is not None:
                dy, dxy = _post_norm_grad(dx, y_ref[rows, :], g_ref[...])
                dy_ref[rows, :] = dy.astype(BF16)
                dg_ref[...] += _colsum8(dxy)

    ins, in_specs = [], []
    for a, w in pairs:
        ins += [a, w]
        in_specs += [_row_spec(tm, a.shape[1]),
                     pl.BlockSpec((None, D, a.shape[1]), lambda i: (0, 0, 0), pipeline_mode=pl.Buffered(1))]
    ins += [x, dxn, scales]
    in_specs += [_row_spec(tm, D), _row_spec(tm, D), _vec_spec(nb, D)]
    acc_spec = pl.BlockSpec((nb, 8, D), lambda i: (0, 0, 0))
    out_specs = [_row_spec(tm, D), acc_spec, acc_spec]
    out_shape = [jax.ShapeDtypeStruct((S, D), F32), jax.ShapeDtypeStruct((nb, 8, D), F32),
                 jax.ShapeDtypeStruct((nb, 8, D), F32)]
    if post is not None:
        ins += list(post)
        in_specs += [_row_spec(tm, D), _vec_spec(1, D)]
        out_specs += [_row_spec(tm, D), _vec_spec(8, D)]
        out_shape += [jax.ShapeDtypeStruct((S, D), BF16), jax.ShapeDtypeStruct((8, D), F32)]
    return pl.pallas_call(
        body, name=name, grid=(S // tm,), in_specs=in_specs, out_specs=out_specs, out_shape=out_shape,
        compiler_params=_params("arbitrary"),
    )(*ins)


FFN_PAIRS = 2
FFN_SUB_ROWS = 256


def _ffn_bwd(dy, w_out, gu, w_in, x, dxn, scale, post, name):
    S, D = dy.shape
    F2 = gu.shape[1]
    PW = F2 // (2 * FFN_PAIRS)
    tm = _pick(S, 256, 16)

    def body(dy_ref, wo_ref, gu_ref, wi_ref, x_ref, d_ref, sc_ref, y_ref, g_ref,
             dgu_ref, dx_ref, ds_ref, db_ref, dyn_ref, dg_ref):
        @pl.when(pl.program_id(0) == 0)
        def _():
            ds_ref[...] = jnp.zeros_like(ds_ref)
            db_ref[...] = jnp.zeros_like(db_ref)
            dg_ref[...] = jnp.zeros_like(dg_ref)

        dh = jnp.zeros((tm, D), F32)
        for p in range(FFN_PAIRS):
            cols = slice(2 * p * PW, 2 * (p + 1) * PW)
            da = lax.dot_general(dy_ref[...], wo_ref[p * PW:(p + 1) * PW, :], _DIMS["nt"],
                                 preferred_element_type=F32)
            g = gu_ref[:, 2 * p * PW:(2 * p + 1) * PW].astype(F32)
            u = gu_ref[:, (2 * p + 1) * PW:2 * (p + 1) * PW].astype(F32)
            sg = jax.nn.sigmoid(g)
            dgu_ref[:, 2 * p * PW:(2 * p + 1) * PW] = (da * u * (sg * (1.0 + g * (1.0 - sg)))).astype(BF16)
            dgu_ref[:, (2 * p + 1) * PW:2 * (p + 1) * PW] = (da * (g * sg)).astype(BF16)
            dh = dh + lax.dot_general(dgu_ref[:, cols], wi_ref[:, cols], _DIMS["nt"], preferred_element_type=F32)
        xv = x_ref[...]
        rr = lax.rsqrt(jnp.mean(xv * xv, axis=-1, keepdims=True) + EPS)
        xh = xv * rr
        dxh = dh * sc_ref[...]
        ds_ref[0] += _colsum8(dh * xh)
        db_ref[0] += _colsum8(dh)
        dx = d_ref[...] + rr * (dxh - xh * jnp.mean(dxh * xh, axis=-1, keepdims=True))
        dx_ref[...] = dx
        dyn, dxy = _post_norm_grad(dx, y_ref[...], g_ref[...])
        dyn_ref[...] = dyn.astype(BF16)
        dg_ref[...] += _colsum8(dxy)

    resident = dict(pipeline_mode=pl.Buffered(1))
    acc_spec = pl.BlockSpec((1, 8, D), lambda i: (0, 0, 0))
    return pl.pallas_call(
        body, name=name, grid=(S // tm,),
        in_specs=[_row_spec(tm, D), pl.BlockSpec((None, F2 // 2, D), lambda i: (0, 0, 0), **resident),
                  _row_spec(tm, F2), pl.BlockSpec((None, D, F2), lambda i: (0, 0, 0), **resident),
                  _row_spec(tm, D), _row_spec(tm, D), _vec_spec(1, D), _row_spec(tm, D), _vec_spec(1, D)],
        out_specs=[_row_spec(tm, F2), _row_spec(tm, D), acc_spec, acc_spec, _row_spec(tm, D), _vec_spec(8, D)],
        out_shape=[jax.ShapeDtypeStruct((S, F2), BF16), jax.ShapeDtypeStruct((S, D), F32),
                   jax.ShapeDtypeStruct((1, 8, D), F32), jax.ShapeDtypeStruct((1, 8, D), F32),
                   jax.ShapeDtypeStruct((S, D), BF16), jax.ShapeDtypeStruct((8, D), F32)],
        compiler_params=_params("arbitrary"),
    )(dy, w_out, gu, w_in, x, dxn, scale, *post)


HALO = 16


def _conv_terms(bcx_ref, prev_ref, i, tm, D):
    b = bcx_ref[:, 0:D].astype(F32)
    cg = bcx_ref[:, D:2 * D].astype(F32)
    xin = bcx_ref[:, 2 * D:3 * D].astype(F32)
    z = cg * xin
    zp = prev_ref[:, D:2 * D].astype(F32) * prev_ref[:, 2 * D:3 * D].astype(F32)
    zp = jnp.where(i > 0, zp, 0.0)
    z_ext = jnp.concatenate([zp, z], axis=0)
    z1 = pltpu.roll(z_ext, 1, 0)[HALO:, :]
    z2 = pltpu.roll(z_ext, 2, 0)[HALO:, :]
    return b, cg, xin, z, z1, z2


def _conv_gate(bcx, ck, name):
    S, D3 = bcx.shape
    D = D3 // 3
    tm = _pick(S, 256, 16)
    hb = tm // HALO

    def body(bcx_ref, prev_ref, ck_ref, o_ref):
        i = pl.program_id(0)
        b, _, _, z, z1, z2 = _conv_terms(bcx_ref, prev_ref, i, tm, D)
        conv = ck_ref[0:1, :] * z2 + ck_ref[1:2, :] * z1 + ck_ref[2:3, :] * z
        o_ref[...] = (b * conv).astype(BF16)

    return pl.pallas_call(
        body, name=name, grid=(S // tm,),
        in_specs=[_row_spec(tm, D3),
                  pl.BlockSpec((HALO, D3), lambda i: (jnp.maximum(i * hb - 1, 0), 0)),
                  _vec_spec(8, D)],
        out_specs=_row_spec(tm, D),
        out_shape=jax.ShapeDtypeStruct((S, D), BF16),
        compiler_params=_params("parallel"),
    )(bcx, bcx, ck)


def _conv_gate_bwd(du, bcx, ck, name):
    S, D3 = bcx.shape
    D = D3 // 3
    tm = _pick(S, 256, 16)
    hb = tm // HALO
    nt = S // tm

    def body(du_ref, dun_ref, bcx_ref, prev_ref, next_ref, ck_ref, o_ref, dk_ref):
        i = pl.program_id(0)
        b, cg, xin, z, z1, z2 = _conv_terms(bcx_ref, prev_ref, i, tm, D)
        k0, k1, k2 = ck_ref[0:1, :], ck_ref[1:2, :], ck_ref[2:3, :]
        conv = k0 * z2 + k1 * z1 + k2 * z
        d = du_ref[...].astype(F32)
        dconv = d * b
        dcn = jnp.where(i < nt - 1, dun_ref[...].astype(F32) * next_ref[:, 0:D].astype(F32), 0.0)
        d_ext = jnp.concatenate([dconv, dcn], axis=0)
        d1 = pltpu.roll(d_ext, tm + HALO - 1, 0)[:tm, :]
        d2 = pltpu.roll(d_ext, tm + HALO - 2, 0)[:tm, :]
        dz = k2 * dconv + k1 * d1 + k0 * d2
        o_ref[:, 0:D] = (d * conv).astype(BF16)
        o_ref[:, D:2 * D] = (dz * xin).astype(BF16)
        o_ref[:, 2 * D:3 * D] = (dz * cg).astype(BF16)

        @pl.when(i == 0)
        def _():
            dk_ref[...] = jnp.zeros_like(dk_ref)

        dk_ref[0] += _colsum8(dconv * z2)
        dk_ref[1] += _colsum8(dconv * z1)
        dk_ref[2] += _colsum8(dconv * z)

    last = S // HALO - 1
    return pl.pallas_call(
        body, name=name, grid=(nt,),
        in_specs=[_row_spec(tm, D),
                  pl.BlockSpec((HALO, D), lambda i: (jnp.minimum((i + 1) * hb, last), 0)),
                  _row_spec(tm, D3),
                  pl.BlockSpec((HALO, D3), lambda i: (jnp.maximum(i * hb - 1, 0), 0)),
                  pl.BlockSpec((HALO, D3), lambda i: (jnp.minimum((i + 1) * hb, last), 0)),
                  _vec_spec(8, D)],
        out_specs=[_row_spec(tm, D3), pl.BlockSpec((3, 8, D), lambda i: (0, 0, 0))],
        out_shape=[jax.ShapeDtypeStruct((S, D3), BF16), jax.ShapeDtypeStruct((3, 8, D), F32)],
        compiler_params=_params("arbitrary"),
    )(du, du, bcx, bcx, bcx, ck)


def _rel_onehot():
    a = np.arange(CHUNK)[:, None]
    b = np.arange(CHUNK)[None, :]
    idx = np.stack([np.clip((N_LEFT_CHUNKS - dl) * CHUNK + a - b, -MAX_REL, MAX_REL) + MAX_REL
                    for dl in (6, 7, 8)]).reshape(-1)
    return (jnp.asarray(idx)[:, None] == jnp.arange(N_REL)[None, :]).astype(F32)


def _bias_table(rel_bias, name):
    H = rel_bias.shape[0]
    near = jnp.dot(rel_bias, _rel_onehot().T, precision=lax.Precision.HIGHEST).reshape(H, 3, CHUNK, CHUNK)
    far = jnp.broadcast_to(rel_bias[:, N_REL - 1][:, None, None], (H, CHUNK, CHUNK))

    def body(near_ref, far_ref, o_ref):
        neg = jnp.full((CHUNK, CHUNK), NEG, F32)
        for v in range(N_WIN):
            for ic in range(Q_CHUNKS):
                for jc in range(N_WIN * Q_CHUNKS):
                    dl = jc - ic
                    if dl < 0 or dl > N_LEFT_CHUNKS or jc < (N_WIN - 1 - v) * Q_CHUNKS:
                        blk = neg
                    else:
                        blk = far_ref[...] if dl <= 5 else near_ref[dl - 6]
                    o_ref[v, ic * CHUNK:(ic + 1) * CHUNK, jc * CHUNK:(jc + 1) * CHUNK] = blk

    return pl.pallas_call(
        body, name=name, grid=(H,),
        in_specs=[pl.BlockSpec((None, 3, CHUNK, CHUNK), lambda h: (h, 0, 0, 0)),
                  pl.BlockSpec((None, CHUNK, CHUNK), lambda h: (h, 0, 0))],
        out_specs=pl.BlockSpec((N_WIN, None, BQ, N_WIN * BQ), lambda h: (0, h, 0, 0)),
        out_shape=jax.ShapeDtypeStruct((N_WIN, H, BQ, N_WIN * BQ), F32),
        compiler_params=_params("parallel"),
    )(near, far)


def _bias_table_grad(dtab):
    H = dtab.shape[0]
    blk = lambda ic, jc: dtab[:, ic * CHUNK:(ic + 1) * CHUNK, jc * CHUNK:(jc + 1) * CHUNK]
    by_dl = [sum(blk(ic, ic + dl) for ic in range(Q_CHUNKS)) for dl in range(N_LEFT_CHUNKS + 1)]
    far = sum(jnp.sum(by_dl[dl], axis=(1, 2)) for dl in range(6))
    near = jnp.stack(by_dl[6:9], axis=1).reshape(H, 3 * CHUNK * CHUNK)
    g = jnp.dot(near, _rel_onehot(), precision=lax.Precision.HIGHEST)
    return g.at[:, N_REL - 1].add(far)


def _attn_specs(nblk, W):
    last = nblk - 1
    q_spec = pl.BlockSpec((BQ, W), lambda g, i: (jnp.minimum(i, last), g))
    kv_specs = [pl.BlockSpec((BQ, 2 * W), functools.partial(
        lambda g, i, w: (jnp.maximum(jnp.minimum(i, last) - (N_WIN - 1) + w, 0), g), w=w)) for w in range(N_WIN)]
    tab_spec = pl.BlockSpec((None, HEADS_PER_STEP, BQ, N_WIN * BQ),
                            lambda g, i: (jnp.minimum(i, N_WIN - 1), g, 0, 0))
    dtab_spec = pl.BlockSpec((HEADS_PER_STEP, BQ, N_WIN * BQ), lambda g, i: (g, 0, 0))
    return q_spec, kv_specs, tab_spec, dtab_spec


def _attn_exp(q_ref, kT, tab_ref, h, dh):
    s = jnp.dot(q_ref[:, h * dh:(h + 1) * dh], kT[h * dh:(h + 1) * dh, :], preferred_element_type=F32) + tab_ref[h]
    e = jnp.exp(s - jnp.max(s, axis=-1, keepdims=True))
    return e, jnp.sum(e, axis=-1, keepdims=True)


def _attn_fwd(q, kv, tab, name):
    S, D = q.shape
    dh = D // N_HEADS
    W = HEADS_PER_STEP * dh
    assert 2 * W == D, "the kv layout puts one head group's k beside its v: two head groups"
    q_spec, kv_specs, tab_spec, _ = _attn_specs(S // BQ, W)

    def body(q_ref, *rest):
        tab_ref, o_ref = rest[N_WIN], rest[N_WIN + 1]
        kvw = jnp.concatenate([r[...] for r in rest[:N_WIN]], axis=0)
        kT = kvw[:, :W].T
        vw = kvw[:, W:]
        outs = []
        for h in range(HEADS_PER_STEP):
            e, l = _attn_exp(q_ref, kT, tab_ref, h, dh)
            outs.append(jnp.dot(e.astype(BF16), vw[:, h * dh:(h + 1) * dh], preferred_element_type=F32) / l)
        o_ref[...] = jnp.concatenate(outs, axis=1).astype(BF16)

    return pl.pallas_call(
        body, name=name, grid=(N_HEADS // HEADS_PER_STEP, S // BQ),
        in_specs=[q_spec] + kv_specs + [tab_spec],
        out_specs=q_spec,
        out_shape=jax.ShapeDtypeStruct((S, D), BF16),
        compiler_params=_params("parallel", "parallel"),
    )(q, *([kv] * N_WIN), tab)


def _attn_bwd(q, kv, tab, do, name):
    S, D = q.shape
    dh = D // N_HEADS
    W = HEADS_PER_STEP * dh
    nblk = S // BQ
    q_spec, kv_specs, tab_spec, dtab_spec = _attn_specs(nblk, W)

    def body(q_ref, *rest):
        tab_ref, do_ref, dq_ref, dkv_ref, dtab_ref, ring = rest[N_WIN:]
        i = pl.program_id(1)

        @pl.when(i == 0)
        def _():
            dtab_ref[...] = jnp.zeros_like(dtab_ref)
            ring[...] = jnp.zeros_like(ring)

        @pl.when(i < nblk)
        def _():
            kvw = jnp.concatenate([r[...] for r in rest[:N_WIN]], axis=0)
            kT = kvw[:, :W].T
            vT = kvw[:, W:].T
            qT = q_ref[...].T
            doT = do_ref[...].T
            dqs, dks, dvs = [], [], []
            for h in range(HEADS_PER_STEP):
                hd = slice(h * dh, (h + 1) * dh)
                e, l = _attn_exp(q_ref, kT, tab_ref, h, dh)
                p = e * (1.0 / l)
                dp = jnp.dot(do_ref[:, hd], vT[hd, :], preferred_element_type=F32)
                ds = p * (dp - jnp.sum(p * dp, axis=-1, keepdims=True))
                dtab_ref[h] += ds
                dsb = ds.astype(BF16)
                dqs.append(lax.dot_general(kT[hd, :], dsb, _DIMS["nt"], preferred_element_type=F32) * (dh ** -0.5))
                dks.append(jnp.dot(qT[hd, :], dsb, preferred_element_type=F32))
                dvs.append(jnp.dot(doT[hd, :], p.astype(BF16), preferred_element_type=F32))
            dq_ref[...] = jnp.concatenate(dqs, axis=0).T.astype(BF16)
            dkv = jnp.concatenate(dks + dvs, axis=0).T
            for w in range(N_WIN):
                slot = lax.rem(i + 1 + w, N_WIN)
                part = dkv[w * BQ:(w + 1) * BQ, :]
                if w == N_WIN - 1:
                    ring[slot] = part
                else:
                    ring[slot] += part

        dkv_ref[...] = ring[lax.rem(i + 1, N_WIN)].astype(BF16)

    done_spec = pl.BlockSpec((BQ, 2 * W), lambda g, i: (jnp.maximum(i - (N_WIN - 1), 0), g))
    return pl.pallas_call(
        body, name=name, grid=(N_HEADS // HEADS_PER_STEP, nblk + N_WIN - 1),
        in_specs=[q_spec] + kv_specs + [tab_spec, q_spec],
        out_specs=[q_spec, done_spec, dtab_spec],
        out_shape=[jax.ShapeDtypeStruct((S, D), BF16), jax.ShapeDtypeStruct((S, 2 * D), BF16),
                   jax.ShapeDtypeStruct(tab.shape[1:], F32)],
        scratch_shapes=[pltpu.VMEM((N_WIN, BQ, 2 * W), F32)],
        compiler_params=_params("parallel", "arbitrary"),
    )(q, *([kv] * N_WIN), tab, do)


def _adamw(w, g, m, v, name):
    shape = w.shape
    C = shape[-1]
    R = int(np.prod(shape[:-1])) if len(shape) > 1 else 1
    whole = len(shape) >= 2 and R * C <= SMALL_TENSOR_ELEMS
    if whole:
        w2, g2, m2, v2 = w, g, m, v
    else:
        w2, g2, m2, v2 = (t.reshape(R, C) for t in (w, g, m, v))
    tr = _pick(R, max(8, (512 * 1024) // C // 8 * 8), 8)

    def body(w_ref, g_ref, m_ref, v_ref, d_ref, nm_ref, nv_ref):
        gv = g_ref[...]
        nm = ADAM_B1 * m_ref[...] + (1.0 - ADAM_B1) * gv
        nv = ADAM_B2 * v_ref[...] + (1.0 - ADAM_B2) * jnp.square(gv)
        m_hat = nm / (1.0 - ADAM_B1 ** ADAM_STEP)
        v_hat = nv / (1.0 - ADAM_B2 ** ADAM_STEP)
        d_ref[...] = -ADAM_LR * (m_hat / (jnp.sqrt(v_hat) + ADAM_EPS) + ADAM_WD * w_ref[...])
        nm_ref[...] = nm
        nv_ref[...] = nv

    if whole:
        spec, grid = pl.BlockSpec(shape, lambda i: (0,) * len(shape)), (1,)
    else:
        spec, grid = pl.BlockSpec((tr, C), lambda i: (i, 0)), (R // tr,)
    outs = pl.pallas_call(
        body, name=name, grid=grid,
        in_specs=[spec] * 4, out_specs=[spec] * 3,
        out_shape=[jax.ShapeDtypeStruct(w2.shape, F32)] * 3,
        compiler_params=_params("parallel"),
    )(w2, g2, m2, v2)
    return tuple(o.reshape(shape) for o in outs)


def _sum_rows(a, name):
    n, L = a.shape

    def body(a_ref, o_ref):
        acc = a_ref[0:1, :]
        for r in range(1, n):
            acc = acc + a_ref[r:r + 1, :]
        o_ref[...] = acc

    return pl.pallas_call(
        body, name=name, grid=(1,),
        in_specs=[pl.BlockSpec((n, L), lambda i: (0, 0))],
        out_specs=pl.BlockSpec((1, L), lambda i: (0, 0)),
        out_shape=jax.ShapeDtypeStruct((1, L), F32),
        compiler_params=_params("arbitrary"),
    )(a)


def _scalar_call(body, name, scalar, grid, in_specs, out_spec, out_shape, args):
    return pl.pallas_call(
        body, name=name,
        grid_spec=pltpu.PrefetchScalarGridSpec(num_scalar_prefetch=1, grid=grid, in_specs=in_specs,
                                               out_specs=out_spec),
        out_shape=out_shape, compiler_params=_params("parallel"),
    )(jnp.reshape(scalar, (-1,)).astype(jnp.int32), *args)


def _pair_sum(view, got, c, name):
    nb, _, rh, cols = view.shape
    tr = _pick(rh, max(16, (1 << 20) // cols // 16 * 16), 16)
    bpr = rh // tr

    def body(s_ref, a_ref, b_ref, o_ref):
        o_ref[...] = (a_ref[...].astype(F32) + b_ref[...].astype(F32)).astype(BF16)

    spec = pl.BlockSpec((tr, cols), lambda i, s: (i, 0))
    mine = pl.BlockSpec((tr, cols), lambda i, s: ((2 * (i // bpr) + s[0]) * bpr + i % bpr, 0))
    return _scalar_call(body, name, c, (nb * bpr,), [mine, spec], spec,
                        jax.ShapeDtypeStruct((nb * rh, cols), BF16),
                        (view.reshape(nb * 2 * rh, cols), got.reshape(nb * rh, cols)))


STACKED_LAYERS = 2


def _owner_sum(pair, recv, me, c, it, name, layer=None, into=None):
    _, rh, bc = recv.shape
    tr = _pick(rh, max(16, (1 << 19) // bc // 16 * 16), 16)
    bpr = rh // tr

    def body(s_ref, a_ref, r0, r1, r2, *rest):
        rest[-1][...] = ((a_ref[...].astype(F32) + r0[...].astype(F32)) + r1[...].astype(F32)) + r2[...].astype(F32)

    if it.kind == "col":
        own = pl.BlockSpec((tr, bc), lambda i, s: (i, s[0]))
    else:
        own = pl.BlockSpec((tr, bc), lambda i, s: (s[0] * bpr + i, 0))
    slots = [pl.BlockSpec((None, tr, bc), functools.partial(lambda i, s, k: (k, i, 0), k=k)) for k in range(3)]
    in_specs, args, aliases = [own] + slots, [pair, recv, recv, recv], {}
    if layer is None:
        out_spec = pl.BlockSpec((tr, bc), lambda i, s: (s[1] * bpr + i, 0))
        out_shape = jax.ShapeDtypeStruct((2 * rh, bc), F32)
    else:
        out_spec = pl.BlockSpec((None, tr, bc), lambda i, s: (layer, s[1] * bpr + i, 0))
        out_shape = jax.ShapeDtypeStruct((STACKED_LAYERS, 2 * rh, bc), F32)
        if into is not None:
            in_specs.append(pl.BlockSpec(memory_space=pl.ANY))
            args.append(into)
            aliases = {len(args): 0}
    return pl.pallas_call(
        body, name=name,
        grid_spec=pltpu.PrefetchScalarGridSpec(num_scalar_prefetch=1, grid=(bpr,), in_specs=in_specs,
                                               out_specs=out_spec),
        out_shape=out_shape, input_output_aliases=aliases, compiler_params=_params("parallel"),
    )(jnp.stack([it.pos(me), c]).astype(jnp.int32), *args)


def _place():
    x, y, c = lax.axis_index("x"), lax.axis_index("y"), lax.axis_index("c")
    chips = [(1 - x, y), (x, 1 - y), (1 - x, 1 - y)]
    return x, y, c, chips


def _chip_index(px, py):
    return 2 * px + py


def _all_gather_small(x_shard, name):
    m_per, n = x_shard.shape

    def body(x_ref, out_ref, send_sems, recv_sems, local_sem):
        x, y, c, chips = _place()
        me, sibling = (x, y, c), (x, y, 1 - c)

        def rows(px, py, pc):
            return out_ref.at[pl.ds((4 * px + 2 * py + pc) * m_per, m_per), :]

        def copy(k, block, to, src=None):
            return pltpu.make_async_remote_copy(
                src_ref=rows(*block) if src is None else src, dst_ref=rows(*block),
                send_sem=send_sems.at[k], recv_sem=recv_sems.at[k], device_id=to, device_id_type=MESH)

        mine = pltpu.make_async_copy(x_ref, rows(*me), local_sem)
        mine.start()
        first = [copy(0, me, sibling, src=x_ref)]
        first += [copy(1 + j, me, (*chip, c), src=x_ref) for j, chip in enumerate(chips)]
        for cp in first:
            cp.start()
        passed = [copy(4 + j, (*chip, c), sibling) for j, chip in enumerate(chips)]
        for j, chip in enumerate(chips):
            copy(1 + j, (*chip, c), me).wait_recv()
            passed[j].start()
        copy(0, sibling, me).wait_recv()
        for j, chip in enumerate(chips):
            copy(4 + j, (*chip, 1 - c), me).wait_recv()
        for cp in first + passed:
            cp.wait_send()
        mine.wait()

    return pl.pallas_call(
        body, name=name,
        out_shape=jax.ShapeDtypeStruct((N_DEV * m_per, n), x_shard.dtype),
        in_specs=[pl.BlockSpec(memory_space=pltpu.VMEM)],
        out_specs=pl.BlockSpec(memory_space=pltpu.VMEM),
        scratch_shapes=[pltpu.SemaphoreType.DMA((7,)), pltpu.SemaphoreType.DMA((7,)), pltpu.SemaphoreType.DMA],
    )(x_shard)


def _gather_flat(vec, name):
    L = vec.shape[0]
    Lp = -(-L // 1024) * 1024
    g = _all_gather_small(jnp.pad(vec, (0, Lp - L)).reshape(8, Lp // 8), name)
    return g.reshape(N_DEV, Lp)[:, :L]


class _Item:
    def __init__(self, kind, rows, cols, arg, layer, swap=False):
        self.kind, self.rows, self.cols, self.arg, self.layer, self.swap = kind, rows, cols, arg, layer, swap

    def ref(self, refs):
        return refs[self.arg].at[self.layer]

    def pos(self, j):
        return 2 * (j % 2) + j // 2 if self.swap else j


def _block(ref, it, j, half):
    if it.kind == "col":
        ns = it.cols // N_CHIP
        return ref.at[pl.ds(half * (it.rows // 2), it.rows // 2), pl.ds(it.pos(j) * ns, ns)]
    rs = it.rows // N_CHIP
    return ref.at[pl.ds(j * rs + half * (rs // 2), rs // 2), :]


def _cast_place(w, layer, kind, pos, after, name):
    _, r, n = w.shape
    tr = _pick(r, max(16, (1 << 20) // n // 16 * 16), 16)
    bpr = r // tr

    def body(s_ref, w_ref, after_ref, o_ref):
        o_ref[...] = w_ref[...].astype(BF16)

    if kind == "col":
        full, out_idx = (1, r, N_CHIP * n), (lambda i, s: (0, i, s[0]))
    else:
        full, out_idx = (1, N_CHIP * r, n), (lambda i, s: (0, s[0] * bpr + i, 0))
    return pl.pallas_call(
        body, name=name,
        grid_spec=pltpu.PrefetchScalarGridSpec(
            num_scalar_prefetch=1, grid=(bpr,),
            in_specs=[pl.BlockSpec((None, tr, n), lambda i, s: (layer, i, 0)), pl.BlockSpec(memory_space=pl.ANY)],
            out_specs=pl.BlockSpec((None, tr, n), out_idx)),
        out_shape=jax.ShapeDtypeStruct(full, BF16),
        compiler_params=_params("parallel"),
    )(jnp.reshape(pos, (1,)).astype(jnp.int32), w, after)


HBM_SPEC = pl.BlockSpec(memory_space=pltpu.HBM)
SEM_SPEC = pl.BlockSpec(memory_space=pltpu.SEMAPHORE)
ANY_SPEC = pl.BlockSpec(memory_space=pl.ANY)
SPLIT_PARAMS = dict(has_side_effects=pltpu.SideEffectType.DATAFLOW_SIDE_EFFECTING)


def _in_hbm(a):
    return pltpu.with_memory_space_constraint(a, pltpu.HBM)


def _split_start(copies_of, bufs, n_sem, after, name):
    n = len(bufs)

    def body(*refs):
        ins, send, recv, token = refs[:n], refs[n + 1], refs[n + 2], refs[2 * n + 3]
        for cp in copies_of(ins, send, recv, False)[0]:
            cp.start()
        token[...] = jnp.zeros_like(token)

    outs = pl.pallas_call(
        body, name=name,
        out_shape=(pltpu.SemaphoreType.DMA(n_sem), pltpu.SemaphoreType.DMA(n_sem),
                   *[pltpu.HBM(b.shape, b.dtype) for b in bufs], jax.ShapeDtypeStruct((8, 128), F32)),
        in_specs=[HBM_SPEC] * n + [ANY_SPEC],
        out_specs=(SEM_SPEC, SEM_SPEC, *[HBM_SPEC] * n, pl.BlockSpec(memory_space=pltpu.VMEM)),
        input_output_aliases={t: 2 + t for t in range(n)},
        compiler_params=pltpu.CompilerParams(**SPLIT_PARAMS),
    )(*[_in_hbm(b) for b in bufs], after)
    return outs[0], outs[1], list(outs[2:2 + n]), outs[2 + n]


def _split_wait(copies_of, send, recv, bufs, after, name):
    n = len(bufs)

    def body(*refs):
        ins, send_ref, recv_ref = refs[:n], refs[n], refs[n + 1]
        sends, arrivals = copies_of(ins, send_ref, recv_ref, True)
        for cp in sends:
            cp.wait_send()
        for cp in arrivals:
            cp.wait_recv()

    return pl.pallas_call(
        body, name=name,
        out_shape=[pltpu.HBM(b.shape, b.dtype) for b in bufs],
        in_specs=[HBM_SPEC] * n + [SEM_SPEC, SEM_SPEC, ANY_SPEC],
        out_specs=[HBM_SPEC] * n,
        input_output_aliases={t: t for t in range(n)},
        compiler_params=pltpu.CompilerParams(**SPLIT_PARAMS),
    )(*bufs, send, recv, after)


def _gather_copies(items):
    def copies_of(refs, send, recv, with_arrivals):
        x, y, c, chips = _place()
        me = _chip_index(x, y)
        sends, arrivals = [], []
        for t, it in enumerate(items):
            for k, chip in enumerate(chips):
                for core in range(2):
                    mine = _block(it.ref(refs), it, me, c)
                    sends.append(pltpu.make_async_remote_copy(
                        src_ref=mine, dst_ref=mine, send_sem=send.at[6 * t + 2 * k + core],
                        recv_sem=recv.at[6 * t + 2 * k + c], device_id=(*chip, core), device_id_type=MESH))
                    if with_arrivals:
                        landed = _block(it.ref(refs), it, _chip_index(*chip), core)
                        arrivals.append(pltpu.make_async_remote_copy(
                            src_ref=landed, dst_ref=landed, send_sem=send.at[6 * t + 2 * k + core],
                            recv_sem=recv.at[6 * t + 2 * k + core], device_id=(*chip, core), device_id_type=MESH))
        return sends, arrivals

    return copies_of


def _owner_copies(items):
    n = len(items)

    def blk(ref, it, j):
        if it.kind == "col":
            ns = it.cols // N_CHIP
            return ref.at[:, pl.ds(it.pos(j) * ns, ns)]
        return ref.at[j]

    def copies_of(refs, send, recv, with_arrivals):
        x, y, c, chips = _place()
        sends, arrivals = [], []
        for t, it in enumerate(items):
            for k, chip in enumerate(chips):
                slot = refs[n + t].at[k]
                sends.append(pltpu.make_async_remote_copy(
                    src_ref=blk(refs[t], it, _chip_index(*chip)), dst_ref=slot, send_sem=send.at[3 * t + k],
                    recv_sem=recv.at[3 * t + k], device_id=(*chip, c), device_id_type=MESH))
                if with_arrivals:
                    arrivals.append(pltpu.make_async_remote_copy(
                        src_ref=slot, dst_ref=slot, send_sem=send.at[3 * t + k], recv_sem=recv.at[3 * t + k],
                        device_id=(*chip, c), device_id_type=MESH))
        return sends, arrivals

    return copies_of


def _owner_slot_shape(it):
    if it.kind == "col":
        return (3, it.rows // 2, it.cols // N_CHIP)
    return (3, it.rows // (2 * N_CHIP), it.cols)


def _pair_view(g, it):
    if it.kind == "col":
        return g.reshape(1, 2, it.rows // 2, it.cols)
    return g.reshape(N_CHIP, 2, it.rows // (2 * N_CHIP), it.cols)


def _pair_copies(n):
    def copies_of(refs, send, recv, with_arrivals):
        x, y, c, _ = _place()
        sends, arrivals = [], []
        for t in range(n):
            land = refs[n + t]
            sends.append(pltpu.make_async_remote_copy(
                src_ref=refs[t].at[:, pl.ds(1 - c, 1)], dst_ref=land, send_sem=send.at[t], recv_sem=recv.at[t],
                device_id=(x, y, 1 - c), device_id_type=MESH))
            if with_arrivals:
                arrivals.append(pltpu.make_async_remote_copy(
                    src_ref=land, dst_ref=land, send_sem=send.at[t], recv_sem=recv.at[t],
                    device_id=(x, y, 1 - c), device_id_type=MESH))
        return sends, arrivals

    return copies_of


def _half_copies(n):
    def half(ref, which):
        r2 = ref.shape[-2] // 2
        rows = pl.ds(which * r2, r2)
        return ref.at[rows, :] if len(ref.shape) == 2 else ref.at[:, rows, :]

    def copies_of(refs, send, recv, with_arrivals):
        x, y, c, _ = _place()
        sends, arrivals = [], []
        for t in range(n):
            mine = half(refs[t], c)
            sends.append(pltpu.make_async_remote_copy(
                src_ref=mine, dst_ref=mine, send_sem=send.at[t], recv_sem=recv.at[t],
                device_id=(x, y, 1 - c), device_id_type=MESH))
            if with_arrivals:
                theirs = half(refs[t], 1 - c)
                arrivals.append(pltpu.make_async_remote_copy(
                    src_ref=theirs, dst_ref=theirs, send_sem=send.at[t], recv_sem=recv.at[t],
                    device_id=(x, y, 1 - c), device_id_type=MESH))
        return sends, arrivals

    return copies_of


class _Reduction:
    pass


def _pair_start(grads, items, after, tag, names, layer=None):
    n = len(items)
    views = [_pair_view(g, it) for g, it in zip(grads, items)]
    lands = [lax.empty((v.shape[0], 1) + v.shape[2:], v.dtype) for v in views]
    r = _Reduction()
    r.items, r.tag, r.names, r.layer = items, tag, names, layer
    r.send, r.recv, r.bufs, r.token = _split_start(_pair_copies(n), views + lands, (n,), after, f"rs_pair_start_{tag}")
    return r


def _owner_start(r, after):
    x, y, c, _ = _place()
    n = len(r.items)
    bufs = _split_wait(_pair_copies(n), r.send, r.recv, r.bufs, after, f"rs_pair_wait_{r.tag}")
    pairs = [_pair_sum(bufs[t], bufs[n + t], c, f"rs_pair_sum_{r.tag}_{t}") for t in range(n)]
    shaped = [p if it.kind == "col" else p.reshape(N_CHIP, p.shape[0] // N_CHIP, p.shape[1])
              for p, it in zip(pairs, r.items)]
    lands = [lax.empty(_owner_slot_shape(it), BF16) for it in r.items]
    r.send, r.recv, r.bufs, r.token = _split_start(
        _owner_copies(r.items), shaped + lands, (3 * n,), r.token, f"rs_owner_start_{r.tag}")
    return r


def _reduce_finish(groups, after):
    x, y, c, _ = _place()
    me = _chip_index(x, y)
    halves = {}
    for r in groups:
        n = len(r.items)
        bufs = _split_wait(_owner_copies(r.items), r.send, r.recv, r.bufs, after, f"rs_owner_wait_{r.tag}")
        for t, (it, nm) in enumerate(zip(r.items, r.names)):
            pair = bufs[t].reshape(-1, bufs[t].shape[-1])
            halves[nm] = _owner_sum(pair, bufs[n + t], me, c, it, f"rs_owner_sum_{r.tag}_{t}",
                                    layer=r.layer, into=halves.get(nm))
    n = len(halves)
    return list(halves), _split_start(_half_copies(n), list(halves.values()), (n,), after, "rs_half_start")


def _silu(v):
    return v * jax.nn.sigmoid(v)


def _sum8(p):
    return jnp.sum(p, axis=-2)


def kernel(x, c, mod_w, mod_b, norm_g, ffn_w_in, ffn_w_out, conv_w_in, conv_k, conv_w_out, kv_mod_w, kv_mod_b, kv_norm_g, w_kv, attn_w_q, attn_w_o, rel_bias, loss_target, m_mod_w, m_mod_b, m_norm_g, m_ffn_w_in, m_ffn_w_out, m_conv_w_in, m_conv_k, m_conv_w_out, m_kv_mod_w, m_kv_mod_b, m_kv_norm_g, m_w_kv, m_attn_w_q, m_attn_w_o, m_rel_bias, v_mod_w, v_mod_b, v_norm_g, v_ffn_w_in, v_ffn_w_out, v_conv_w_in, v_conv_k, v_conv_w_out, v_kv_mod_w, v_kv_mod_b, v_kv_norm_g, v_w_kv, v_attn_w_q, v_attn_w_o, v_rel_bias):
    xi, yi, ci = lax.axis_index("x"), lax.axis_index("y"), lax.axis_index("c")
    chip = 2 * xi + yi
    dev = 2 * chip + ci
    _, S, D = x.shape
    F = ffn_w_out.shape[1] * N_CHIP
    x0 = x.reshape(S, D)
    target = loss_target.reshape(S, D)
    n_mod = mod_w.shape[2]
    n_kvm = kv_mod_w.shape[1]
    dsh = D // N_CHIP
    TF = F // 2

    c_all = _all_gather_small(c.reshape(8, D // 8), "ag_c").reshape(N_DEV, D)
    sc16 = jnp.pad(_silu(c_all), ((0, 8), (0, 0)))
    part = [_mm(sc16, mod_w, "nn", F32, f"mod_fwd_{l}", b_layer=l)[:8] for l in range(2)]
    part.append(_mm(sc16, kv_mod_w, "nn", F32, "mod_fwd_kv")[:8])
    fwd_vec = jnp.concatenate([p.reshape(-1) for p in part] + [norm_g.reshape(-1), conv_k.reshape(-1)])
    fwd_all = _gather_flat(fwd_vec, "ag_fwd_small")[0::2]
    o = 0
    mods = []
    for n in (n_mod, n_mod, n_kvm):
        blk = fwd_all[:, o:o + 8 * n].reshape(N_CHIP, 8, n)
        mods.append(lax.dynamic_index_in_dim(blk, dev, axis=1, keepdims=False).reshape(N_CHIP * n))
        o += 8 * n
    ng = fwd_all[:, o:o + 8 * dsh].reshape(N_CHIP, 2, 4, dsh).transpose(1, 2, 0, 3).reshape(2, 4, D)
    o += 8 * dsh
    ck = fwd_all[:, o:o + 3 * dsh].reshape(N_CHIP, 3, dsh).transpose(1, 0, 2).reshape(3, D)
    ck8 = jnp.pad(ck, ((0, 5), (0, 0)))
    mod = [mods[l] + mod_b[l] for l in range(2)]
    sh1, sc1, g1, sh2, sc2, g2 = zip(*[jnp.split(m, 6) for m in mod])
    kv_sh, kv_sc = jnp.split(mods[2] + kv_mod_b, 2)
    row = lambda v: v.reshape(1, D)

    it_conv = [_Item("col", D, 3 * D, 0, 0), _Item("row", D, D, 1, 0)]
    it_ffn = [_Item("col", D, 2 * F, 0, 0, swap=True), _Item("row", F, D, 1, 0)]
    it_attn = [_Item("col", D, 2 * D, 0, 0, swap=True), _Item("row", D, D, 1, 0), _Item("row", D, D, 2, 0)]

    def placed(w, layer, it, nm, after=fwd_all):
        return _cast_place(w, layer, it.kind, it.pos(chip), after, f"place_{nm}")

    flying = {}

    def start(tag, its, bufs, after):
        send, recv, bufs, tok = _split_start(_gather_copies(its), bufs, (6 * len(its),), after, f"ag_start_{tag}")
        flying[tag] = (its, send, recv, bufs)
        return tok

    def arrived(tag, after):
        its, send, recv, bufs = flying[tag]
        return _split_wait(_gather_copies(its), send, recv, bufs, after, f"ag_wait_{tag}")

    one = lambda it: [_Item(it.kind, it.rows, it.cols, 0, 0, it.swap)]
    tok = start("conv_in", one(it_conv[0]), [placed(conv_w_in, 0, it_conv[0], "conv_w_in")], fwd_all)
    tok = start("conv_out", one(it_conv[1]), [placed(conv_w_out, 0, it_conv[1], "conv_w_out", tok)], tok)
    tok = start("ffn0_in", one(it_ffn[0]), [placed(ffn_w_in, 0, it_ffn[0], "ffn_w_in0", tok)], tok)
    tok = start("ffn0_out", one(it_ffn[1]), [placed(ffn_w_out, 0, it_ffn[1], "ffn_w_out0", tok)], tok)
    tok = start("attn", it_attn, [placed(w_kv[None], 0, it_attn[0], "w_kv", tok),
                                  placed(attn_w_q, 0, it_attn[1], "attn_w_q", tok),
                                  placed(attn_w_o, 0, it_attn[2], "attn_w_o", tok)], tok)
    token = start("ffn1", it_ffn, [placed(ffn_w_in, 1, it_ffn[0], "ffn_w_in1", tok),
                                   placed(ffn_w_out, 1, it_ffn[1], "ffn_w_out1", tok)], tok)

    a1 = row(ng[0, 0] * (1.0 + sc1[0])) + token[0, 0]
    (h1,) = _norm_mod(x0, a1, row(sh1[0]), "l0_norm1")
    tab = _bias_table(rel_bias[0], "l1_bias_table")
    h1, tab = lax.optimization_barrier((h1, tab))
    (W_cin,) = arrived("conv_in", h1)
    bcx = _mm(h1, W_cin, "nn", BF16, "l0_conv_in", b_layer=0)
    ug = _conv_gate(bcx, ck8, "l0_conv_gate")
    gt1 = row(g1[0] * ng[0, 1])
    a2 = row(ng[0, 2] * (1.0 + sc2[0]))
    (W_cout,) = arrived("conv_out", ug)
    y1, x1, h2 = _mm_post(ug, W_cout, x0, gt1, "l0_conv_out", scales=a2, shifts=row(sh2[0]))
    (W_fin0,) = arrived("ffn0_in", h2)
    (W_fout0,) = arrived("ffn0_out", h2)
    gt2 = row(g2[0] * ng[0, 3])
    a3 = ng[1, 0] * (1.0 + sc1[1])
    akv = kv_norm_g * (1.0 + kv_sc)
    y2, x2, h3, hkv, gu0, act0 = _mm_post(h2, W_fout0, x1, gt2, "l0_ffn", scales=jnp.stack([a3, akv]),
                                          shifts=jnp.stack([sh1[1], kv_sh]), ffn_w_in=W_fin0)
    W_kv, W_q, W_o = arrived("attn", hkv)
    kvp = _mm(hkv, W_kv, "nn", BF16, "l1_kv", b_layer=0)
    att_scale = (D // N_HEADS) ** -0.5
    assert math.log2(att_scale) % 1 == 0, "scaling q before its bf16 cast is exact only for a power of two"
    qp = _mm(h3, W_q, "nn", BF16, "l1_q", b_layer=0, scale=att_scale)
    oh = _attn_fwd(qp, kvp, tab, "l1_attn")
    gt3 = row(g1[1] * ng[1, 1])
    a4 = row(ng[1, 2] * (1.0 + sc2[1]))
    y3, x3, h4 = _mm_post(oh, W_o, x2, gt3, "l1_attn_out", scales=a4, shifts=row(sh2[1]))
    W_fin1, W_fout1 = arrived("ffn1", h4)
    gt4 = row(g2[1] * ng[1, 3])
    dx4, sq, dy4, dgt4, gu1, act1 = _mm_post(h4, W_fout1, x3, gt4, "l1_ffn", target=target, ffn_w_in=W_fin1)
    loss_part = 0.5 * jnp.sum(sq) / D

    def ffn_bwd(dy, dxn, xin_, h, gu, act, a, w_in, w_out, post, tag):
        dgu, dx, ds, db, dyn, dgt = _ffn_bwd(dy, w_out, gu, w_in, xin_, dxn, a, post, f"{tag}_ffn_bwd")
        g_fout = _mm(act, dy, "tn", BF16, f"{tag}_ffn_out_dw", tm=TF)
        g_fin = _mm(h, dgu, "tn", BF16, f"{tag}_ffn_in_dw", tn=TF)
        return dx, ds, db, dyn, dgt, g_fin, g_fout

    dx3, ds4, db4, dy3, dgt3, G_fin1, G_fout1 = ffn_bwd(dy4, dx4, x3, h4, gu1, act1, a4, W_fin1, W_fout1,
                                                        (y3, gt3), "l1")
    red = [_pair_start([G_fin1, G_fout1], it_ffn, token, "ffn1", ["ffn_w_in", "ffn_w_out"], layer=1)]
    doh = _mm(dy3, W_o, "nt", BF16, "l1_attn_out_dx", b_layer=0, after=red[0].token)
    G_o = _mm(oh, dy3, "tn", BF16, "l1_attn_out_dw")
    _owner_start(red[0], G_o)
    dq, dkv, dtab = _attn_bwd(qp, kvp, tab, doh, "l1_attn_bwd")
    d_rel = _bias_table_grad(dtab)
    G_q = _mm(h3, dq, "tn", BF16, "l1_q_dw")
    G_kv = _mm(hkv, dkv, "tn", BF16, "l1_kv_dw")
    red.append(_pair_start([G_kv, G_q, G_o], it_attn, red[-1].token, "attn", ["w_kv", "attn_w_q", "attn_w_o"]))
    dx2, ds3, db3, dy2, dgt2 = _mm_pre_bwd([(dq, W_q), (dkv, W_kv)], x2, dx3,
                                           jnp.stack([a3, akv]) + red[1].token[0, 0], "l1_qkv_dx", post=(y2, gt2))
    _owner_start(red[1], dx2)

    dx1, ds2, db2, dy1, dgt1, G_fin0, G_fout0 = ffn_bwd(dy2, dx2, x1, h2, gu0, act0, a2, W_fin0, W_fout0,
                                                        (y1, gt1), "l0")
    red.append(_pair_start([G_fin0, G_fout0], it_ffn, red[-1].token, "ffn0", ["ffn_w_in", "ffn_w_out"], layer=0))
    dug = _mm(dy1, W_cout, "nt", BF16, "l0_conv_out_dx", b_layer=0, after=red[2].token)
    G_cout = _mm(ug, dy1, "tn", BF16, "l0_conv_out_dw")
    _owner_start(red[2], G_cout)
    dbcx, dck = _conv_gate_bwd(dug, bcx, ck8, "l0_conv_gate_bwd")
    G_cin = _mm(h1, dbcx, "tn", BF16, "l0_conv_in_dw")
    red.append(_pair_start([G_cin, G_cout], it_conv, red[-1].token, "conv", ["conv_w_in", "conv_w_out"]))
    dx0, ds1, db1 = _mm_pre_bwd([(dbcx, W_cin)], x0, dx1, a1 + red[3].token[0, 0], "l0_conv_in_dx")
    ds1, db1 = _sum8(ds1)[0], _sum8(db1)[0]
    da2, db2 = _sum8(ds2)[0], _sum8(db2)[0]
    ds3, db3 = _sum8(ds3), _sum8(db3)
    da4, db4 = _sum8(ds4)[0], _sum8(db4)[0]
    dgt1, dgt2, dgt3, dgt4 = _sum8(dgt1), _sum8(dgt2), _sum8(dgt3), _sum8(dgt4)

    def dmod_of(l, ds_a, db_a, dgt_a, ds_b, db_b, dgt_b):
        return jnp.concatenate([db_a, ds_a * ng[l, 0], dgt_a * ng[l, 1], db_b, ds_b * ng[l, 2], dgt_b * ng[l, 3]])

    dmod0 = dmod_of(0, ds1, db1, dgt1, da2, db2, dgt2)
    dmod1 = dmod_of(1, ds3[0], db3[0], dgt3, da4, db4, dgt4)
    dkvmod = jnp.concatenate([db3[1], ds3[1] * kv_norm_g])
    dng = jnp.stack([
        jnp.stack([ds1 * (1.0 + sc1[0]), dgt1 * g1[0], da2 * (1.0 + sc2[0]), dgt2 * g2[0]]),
        jnp.stack([ds3[0] * (1.0 + sc1[1]), dgt3 * g1[1], da4 * (1.0 + sc2[1]), dgt4 * g2[1]])])
    dkvng = ds3[1] * (1.0 + kv_sc)
    small = [dmod0, dmod1, dkvmod, dng.reshape(-1), dkvng, _sum8(dck).reshape(-1), d_rel.reshape(-1),
             loss_part.reshape(1)]
    sizes = [int(s.shape[0]) for s in small]
    offs = np.concatenate([[0], np.cumsum(sizes)])
    bwd_all = _gather_flat(jnp.concatenate(small), "ag_bwd_small")
    _owner_start(red[3], bwd_all)
    Lb = bwd_all.shape[1]
    Lp = -(-Lb // 128) * 128
    tot = _sum_rows(jnp.pad(bwd_all, ((0, 0), (0, Lp - Lb))), "sum_small")[0]
    seg = lambda i: tot[offs[i]:offs[i + 1]]
    g_mod_b = jnp.stack([seg(0), seg(1)])
    g_kv_mod_b = seg(2)
    g_norm_g = lax.dynamic_slice_in_dim(seg(3).reshape(2, 4, D), chip * dsh, dsh, axis=2)
    g_kv_norm_g = seg(4)
    g_conv_k = lax.dynamic_slice_in_dim(seg(5).reshape(1, 3, D), chip * dsh, dsh, axis=2)
    g_rel_bias = seg(6).reshape(rel_bias.shape)
    loss = seg(7)[0]

    def dmod_w(i, n, name):
        rows_ = lax.dynamic_slice_in_dim(bwd_all[:, offs[i]:offs[i + 1]], chip * n, n, axis=1)
        return _mm(sc16, jnp.pad(rows_, ((0, 8), (0, 0))), "tn", F32, name)

    g_mod_w = jnp.stack([dmod_w(0, n_mod, "mod_bwd_0"), dmod_w(1, n_mod, "mod_bwd_1")])
    g_kv_mod_w = dmod_w(2, n_kvm, "mod_bwd_kv")

    grads = {
        "mod_w": g_mod_w, "mod_b": g_mod_b, "norm_g": g_norm_g, "conv_k": g_conv_k,
        "kv_mod_w": g_kv_mod_w, "kv_mod_b": g_kv_mod_b, "kv_norm_g": g_kv_norm_g, "rel_bias": g_rel_bias,
    }
    weights = dict(mod_w=mod_w, mod_b=mod_b, norm_g=norm_g, ffn_w_in=ffn_w_in, ffn_w_out=ffn_w_out,
                   conv_w_in=conv_w_in, conv_k=conv_k, conv_w_out=conv_w_out, kv_mod_w=kv_mod_w,
                   kv_mod_b=kv_mod_b, kv_norm_g=kv_norm_g, w_kv=w_kv, attn_w_q=attn_w_q, attn_w_o=attn_w_o,
                   rel_bias=rel_bias)
    m_in = dict(mod_w=m_mod_w, mod_b=m_mod_b, norm_g=m_norm_g, ffn_w_in=m_ffn_w_in, ffn_w_out=m_ffn_w_out,
                conv_w_in=m_conv_w_in, conv_k=m_conv_k, conv_w_out=m_conv_w_out, kv_mod_w=m_kv_mod_w,
                kv_mod_b=m_kv_mod_b, kv_norm_g=m_kv_norm_g, w_kv=m_w_kv, attn_w_q=m_attn_w_q,
                attn_w_o=m_attn_w_o, rel_bias=m_rel_bias)
    v_in = dict(mod_w=v_mod_w, mod_b=v_mod_b, norm_g=v_norm_g, ffn_w_in=v_ffn_w_in, ffn_w_out=v_ffn_w_out,
                conv_w_in=v_conv_w_in, conv_k=v_conv_k, conv_w_out=v_conv_w_out, kv_mod_w=v_kv_mod_w,
                kv_mod_b=v_kv_mod_b, kv_norm_g=v_kv_norm_g, w_kv=v_w_kv, attn_w_q=v_attn_w_q,
                attn_w_o=v_attn_w_o, rel_bias=v_rel_bias)
    names = list(weights)
    step = {}

    def update(n):
        g = grads[n].reshape(weights[n].shape)
        step[n] = (g, *_adamw(weights[n], g, m_in[n], v_in[n], f"adamw_{n}"))

    update("mod_w")
    reduced, (half_send, half_recv, half_bufs, _) = _reduce_finish(red, step["mod_w"][1])
    for n in list(grads):
        if n not in step:
            update(n)
    grads.update(zip(reduced, _split_wait(
        _half_copies(len(half_bufs)), half_send, half_recv, half_bufs, step["kv_mod_w"][1], "rs_half_wait")))
    for n in names:
        if n not in step:
            update(n)
    return (loss, dx0.reshape(x.shape), *[step[n][k] for k in range(4) for n in names])
```

```python
import functools
import math

import numpy as np
import jax
import jax.numpy as jnp
from jax import lax
from jax.experimental import pallas as pl
from jax.experimental.pallas import tpu as pltpu

CHUNK = 64
N_LEFT_CHUNKS = 8
N_HEADS = 16
MAX_REL = 2 * CHUNK
N_REL = 2 * MAX_REL + 1
EPS = 1e-6
ADAM_LR = 0.001
ADAM_B1 = 0.9
ADAM_B2 = 0.999
ADAM_EPS = 1e-08
ADAM_WD = 0.01
ADAM_STEP = 10

Q_CHUNKS = 4
BQ = Q_CHUNKS * CHUNK
N_WIN = 1 + N_LEFT_CHUNKS // Q_CHUNKS
HEADS_PER_STEP = 8
NEG = -1e30
N_DEV = 8
N_CHIP = 4
SMALL_TENSOR_ELEMS = 1 << 16

BF16 = jnp.bfloat16
F32 = jnp.float32
V7X_VMEM_LIMIT_BYTES = 56 * 1024 * 1024
MESH = pl.DeviceIdType.MESH


def _pick(n, pref, align):
    t = min(pref, n)
    t -= t % align
    while t >= align:
        if n % t == 0:
            return t
        t -= align
    return n


def _params(*sem):
    return pltpu.CompilerParams(dimension_semantics=sem, vmem_limit_bytes=V7X_VMEM_LIMIT_BYTES)


def _colsum8(v):
    r, d = v.shape
    return v.reshape(r // 8, 8, d).sum(axis=0)


_DIMS = {"nn": (((1,), (0,)), ((), ())), "nt": (((1,), (1,)), ((), ())), "tn": (((0,), (0,)), ((), ()))}


def _mm(a, b, mode, out_dtype, name, *, b_layer=None, tm=1024, tn=1024, tk=None, scale=None, after=None):
    if tk is None:
        tk = 2048 if mode == "tn" else 3072
    bs = b.shape[1:] if b_layer is not None else b.shape
    if mode == "nn":
        (M, K), (K2, N) = a.shape, bs
    elif mode == "nt":
        (M, K), (N, K2) = a.shape, bs
    else:
        (K, M), (K2, N) = a.shape, bs
    assert K == K2, (name, a.shape, b.shape)
    tm = _pick(M, tm, 128 if mode == "tn" else 16)
    tn = _pick(N, tn, 128)
    tk = _pick(K, tk, 128 if mode != "tn" else 16)
    nk = K // tk
    assert scale is None or nk == 1, name
    dims = _DIMS[mode]
    extra = [] if after is None else [after]

    def body(a_ref, b_ref, *rest):
        o_ref, acc = rest[len(extra)], rest[len(extra) + 1:]
        p = lax.dot_general(a_ref[...].astype(BF16), b_ref[...].astype(BF16), dims,
                            preferred_element_type=F32)
        if nk == 1:
            o_ref[...] = (p if scale is None else p * scale).astype(o_ref.dtype)
        else:
            k = pl.program_id(2)

            @pl.when(k == 0)
            def _():
                acc[0][...] = p

            @pl.when(k > 0)
            def _():
                acc[0][...] += p

            @pl.when(k == nk - 1)
            def _():
                o_ref[...] = acc[0][...].astype(o_ref.dtype)

    a_spec = (pl.BlockSpec((tk, tm), lambda i, j, k: (k, i)) if mode == "tn"
              else pl.BlockSpec((tm, tk), lambda i, j, k: (i, k)))
    if mode == "nt":
        b_blk, b_idx = (tn, tk), (lambda i, j, k: (j, k))
    else:
        b_blk, b_idx = (tk, tn), (lambda i, j, k: (k, j))
    if b_layer is not None:
        b_spec = pl.BlockSpec((None,) + b_blk, lambda i, j, k: (b_layer,) + b_idx(i, j, k))
    else:
        b_spec = pl.BlockSpec(b_blk, b_idx)
    return pl.pallas_call(
        body, name=name,
        grid=(M // tm, N // tn, nk),
        in_specs=[a_spec, b_spec] + [pl.BlockSpec(memory_space=pl.ANY)] * len(extra),
        out_specs=pl.BlockSpec((tm, tn), lambda i, j, k: (i, j)),
        out_shape=jax.ShapeDtypeStruct((M, N), out_dtype),
        scratch_shapes=[pltpu.VMEM((tm, tn), F32)] if nk > 1 else [],
        compiler_params=_params("parallel", "parallel", "arbitrary"),
    )(a, b, *extra)


def _row_spec(tm, d):
    return pl.BlockSpec((tm, d), lambda i: (i, 0))


def _vec_spec(r, d):
    return pl.BlockSpec((r, d), lambda i: (0, 0))


def _norm_mod(x, scales, shifts, name):
    S, D = x.shape
    nb = scales.shape[0]
    tm = _pick(S, 1024, 16)

    def body(x_ref, a_ref, b_ref, *o_refs):
        xv = x_ref[...]
        xh = xv * lax.rsqrt(jnp.mean(xv * xv, axis=-1, keepdims=True) + EPS)
        for n in range(nb):
            o_refs[n][...] = (xh * a_ref[n:n + 1, :] + b_ref[n:n + 1, :]).astype(BF16)

    return pl.pallas_call(
        body, name=name, grid=(S // tm,),
        in_specs=[_row_spec(tm, D), _vec_spec(nb, D), _vec_spec(nb, D)],
        out_specs=[_row_spec(tm, D)] * nb,
        out_shape=[jax.ShapeDtypeStruct((S, D), BF16)] * nb,
        compiler_params=_params("parallel"),
    )(x, scales, shifts)


def _mm_post(a, w, x, gate, name, *, scales=None, shifts=None, target=None, sub=256):
    M, K = a.shape
    D = w.shape[2]
    tm = _pick(M, 1024 if K <= D else 512, 16)
    sub = _pick(tm, sub, 16)
    nb = 0 if scales is None else scales.shape[0]

    def body(a_ref, w_ref, x_ref, g_ref, *rest):
        if target is None:
            sc_ref, sh_ref, y_ref, xn_ref = rest[:4]
            h_refs = rest[4:]
        else:
            t_ref, dx_ref, sq_ref, dy_ref, dg_ref = rest

            @pl.when(pl.program_id(0) == 0)
            def _():
                sq_ref[...] = jnp.zeros_like(sq_ref)
                dg_ref[...] = jnp.zeros_like(dg_ref)

        for r in range(tm // sub):
            rows = pl.ds(r * sub, sub)
            yb = jnp.dot(a_ref[rows, :], w_ref[...], preferred_element_type=F32).astype(BF16)
            yv = yb.astype(F32)
            yh = yv * lax.rsqrt(jnp.mean(yv * yv, axis=-1, keepdims=True) + EPS)
            xn = x_ref[rows, :] + yh * g_ref[...]
            if target is None:
                y_ref[rows, :] = yb
                xn_ref[rows, :] = xn
                xh = xn * lax.rsqrt(jnp.mean(xn * xn, axis=-1, keepdims=True) + EPS)
                for n in range(nb):
                    h_refs[n][rows, :] = (xh * sc_ref[n:n + 1, :] + sh_ref[n:n + 1, :]).astype(BF16)
            else:
                e = xn - t_ref[rows, :]
                dx = e / D
                dx_ref[rows, :] = dx
                sq_ref[...] += _colsum8(e * e)
                dy, dxy = _post_norm_grad(dx, yb, g_ref[...])
                dy_ref[rows, :] = dy.astype(BF16)
                dg_ref[...] += _colsum8(dxy)

    ins = [a, w, x, gate]
    in_specs = [_row_spec(tm, K), pl.BlockSpec((None, K, D), lambda i: (0, 0, 0)), _row_spec(tm, D), _vec_spec(1, D)]
    if target is None:
        ins += [scales, shifts]
        in_specs += [_vec_spec(nb, D), _vec_spec(nb, D)]
        out_specs = [_row_spec(tm, D)] * (2 + nb)
        out_shape = [jax.ShapeDtypeStruct((M, D), BF16), jax.ShapeDtypeStruct((M, D), F32)] \
            + [jax.ShapeDtypeStruct((M, D), BF16)] * nb
    else:
        ins += [target]
        in_specs += [_row_spec(tm, D)]
        out_specs = [_row_spec(tm, D), _vec_spec(8, D), _row_spec(tm, D), _vec_spec(8, D)]
        out_shape = [jax.ShapeDtypeStruct((M, D), F32), jax.ShapeDtypeStruct((8, D), F32),
                     jax.ShapeDtypeStruct((M, D), BF16), jax.ShapeDtypeStruct((8, D), F32)]
    return pl.pallas_call(
        body, name=name, grid=(M // tm,), in_specs=in_specs, out_specs=out_specs, out_shape=out_shape,
        compiler_params=_params("arbitrary" if target is not None else "parallel"),
    )(*ins)


def _post_norm_grad(dxn, yb, gate):
    yv = yb.astype(F32)
    r = lax.rsqrt(jnp.mean(yv * yv, axis=-1, keepdims=True) + EPS)
    yh = yv * r
    dyh = dxn * gate
    return r * (dyh - yh * jnp.mean(dyh * yh, axis=-1, keepdims=True)), dxn * yh


def _mm_pre_bwd(pairs, x, dxn, scales, name, post=None, sub=256):
    S, D = x.shape
    nb = len(pairs)
    tm = _pick(S, 512, 16)
    sub = _pick(tm, sub, 16)

    def body(*refs):
        a_refs, w_refs = refs[0:2 * nb:2], refs[1:2 * nb:2]
        x_ref, d_ref, sc_ref = refs[2 * nb:2 * nb + 3]
        rest = refs[2 * nb + 3:]
        if post is not None:
            y_ref, g_ref, dx_ref, ds_ref, db_ref, dy_ref, dg_ref = rest
        else:
            dx_ref, ds_ref, db_ref = rest

        @pl.when(pl.program_id(0) == 0)
        def _():
            ds_ref[...] = jnp.zeros_like(ds_ref)
            db_ref[...] = jnp.zeros_like(db_ref)
            if post is not None:
                dg_ref[...] = jnp.zeros_like(dg_ref)

        for r in range(tm // sub):
            rows = pl.ds(r * sub, sub)
            xv = x_ref[rows, :]
            rr = lax.rsqrt(jnp.mean(xv * xv, axis=-1, keepdims=True) + EPS)
            xh = xv * rr
            dxh = jnp.zeros_like(xv)
            for n in range(nb):
                dh = lax.dot_general(a_refs[n][rows, :], w_refs[n][...], _DIMS["nt"], preferred_element_type=F32)
                dxh = dxh + dh * sc_ref[n:n + 1, :]
                ds_ref[n] += _colsum8(dh * xh)
                db_ref[n] += _colsum8(dh)
            dx = d_ref[rows, :] + rr * (dxh - xh * jnp.mean(dxh * xh, axis=-1, keepdims=True))
            dx_ref[rows, :] = dx
            if post is not None:
                dy, dxy = _post_norm_grad(dx, y_ref[rows, :], g_ref[...])
                dy_ref[rows, :] = dy.astype(BF16)
                dg_ref[...] += _colsum8(dxy)

    ins, in_specs = [], []
    for a, w in pairs:
        ins += [a, w]
        in_specs += [_row_spec(tm, a.shape[1]),
                     pl.BlockSpec((None, D, a.shape[1]), lambda i: (0, 0, 0), pipeline_mode=pl.Buffered(1))]
    ins += [x, dxn, scales]
    in_specs += [_row_spec(tm, D), _row_spec(tm, D), _vec_spec(nb, D)]
    acc_spec = pl.BlockSpec((nb, 8, D), lambda i: (0, 0, 0))
    out_specs = [_row_spec(tm, D), acc_spec, acc_spec]
    out_shape = [jax.ShapeDtypeStruct((S, D), F32), jax.ShapeDtypeStruct((nb, 8, D), F32),
                 jax.ShapeDtypeStruct((nb, 8, D), F32)]
    if post is not None:
        ins += list(post)
        in_specs += [_row_spec(tm, D), _vec_spec(1, D)]
        out_specs += [_row_spec(tm, D), _vec_spec(8, D)]
        out_shape += [jax.ShapeDtypeStruct((S, D), BF16), jax.ShapeDtypeStruct((8, D), F32)]
    return pl.pallas_call(
        body, name=name, grid=(S // tm,), in_specs=in_specs, out_specs=out_specs, out_shape=out_shape,
        compiler_params=_params("arbitrary"),
    )(*ins)


FFN_PAIRS = 2
FFN_SUB_ROWS = 256


def _ffn_in_act(h, w, layer, name, tm=1024, sub=FFN_SUB_ROWS):
    S, D = h.shape
    F2 = w.shape[2]
    PW = F2 // (2 * FFN_PAIRS)
    tm = _pick(S, tm, 16)
    sub = _pick(tm, sub, 16)

    def body(h_ref, w_ref, gu_ref, a_ref):
        for r in range(tm // sub):
            rows = pl.ds(r * sub, sub)
            acc = jnp.dot(h_ref[rows, :], w_ref[...], preferred_element_type=F32)
            gu_ref[rows, :] = acc.astype(BF16)
            g = acc[:, :PW]
            a_ref[rows, :] = (g * jax.nn.sigmoid(g) * acc[:, PW:]).astype(BF16)

    return pl.pallas_call(
        body, name=name, grid=(FFN_PAIRS, S // tm),
        in_specs=[pl.BlockSpec((tm, D), lambda p, i: (i, 0)),
                  pl.BlockSpec((None, D, 2 * PW), lambda p, i: (layer, 0, p))],
        out_specs=[pl.BlockSpec((tm, 2 * PW), lambda p, i: (i, p)), pl.BlockSpec((tm, PW), lambda p, i: (i, p))],
        out_shape=[jax.ShapeDtypeStruct((S, F2), BF16), jax.ShapeDtypeStruct((S, F2 // 2), BF16)],
        compiler_params=_params("parallel", "parallel"),
    )(h, w)


def _ffn_bwd(dy, w_out, gu, w_in, x, dxn, scale, post, name):
    S, D = dy.shape
    F2 = gu.shape[1]
    PW = F2 // (2 * FFN_PAIRS)
    tm = _pick(S, 256, 16)

    def body(dy_ref, wo_ref, gu_ref, wi_ref, x_ref, d_ref, sc_ref, y_ref, g_ref,
             dgu_ref, dx_ref, ds_ref, db_ref, dyn_ref, dg_ref):
        @pl.when(pl.program_id(0) == 0)
        def _():
            ds_ref[...] = jnp.zeros_like(ds_ref)
            db_ref[...] = jnp.zeros_like(db_ref)
            dg_ref[...] = jnp.zeros_like(dg_ref)

        dh = jnp.zeros((tm, D), F32)
        for p in range(FFN_PAIRS):
            cols = slice(2 * p * PW, 2 * (p + 1) * PW)
            da = lax.dot_general(dy_ref[...], wo_ref[p * PW:(p + 1) * PW, :], _DIMS["nt"],
                                 preferred_element_type=F32)
            g = gu_ref[:, 2 * p * PW:(2 * p + 1) * PW].astype(F32)
            u = gu_ref[:, (2 * p + 1) * PW:2 * (p + 1) * PW].astype(F32)
            sg = jax.nn.sigmoid(g)
            dgu_ref[:, 2 * p * PW:(2 * p + 1) * PW] = (da * u * (sg * (1.0 + g * (1.0 - sg)))).astype(BF16)
            dgu_ref[:, (2 * p + 1) * PW:2 * (p + 1) * PW] = (da * (g * sg)).astype(BF16)
            dh = dh + lax.dot_general(dgu_ref[:, cols], wi_ref[:, cols], _DIMS["nt"], preferred_element_type=F32)
        xv = x_ref[...]
        rr = lax.rsqrt(jnp.mean(xv * xv, axis=-1, keepdims=True) + EPS)
        xh = xv * rr
        dxh = dh * sc_ref[...]
        ds_ref[0] += _colsum8(dh * xh)
        db_ref[0] += _colsum8(dh)
        dx = d_ref[...] + rr * (dxh - xh * jnp.mean(dxh * xh, axis=-1, keepdims=True))
        dx_ref[...] = dx
        dyn, dxy = _post_norm_grad(dx, y_ref[...], g_ref[...])
        dyn_ref[...] = dyn.astype(BF16)
        dg_ref[...] += _colsum8(dxy)

    resident = dict(pipeline_mode=pl.Buffered(1))
    acc_spec = pl.BlockSpec((1, 8, D), lambda i: (0, 0, 0))
    return pl.pallas_call(
        body, name=name, grid=(S // tm,),
        in_specs=[_row_spec(tm, D), pl.BlockSpec((None, F2 // 2, D), lambda i: (0, 0, 0), **resident),
                  _row_spec(tm, F2), pl.BlockSpec((None, D, F2), lambda i: (0, 0, 0), **resident),
                  _row_spec(tm, D), _row_spec(tm, D), _vec_spec(1, D), _row_spec(tm, D), _vec_spec(1, D)],
        out_specs=[_row_spec(tm, F2), _row_spec(tm, D), acc_spec, acc_spec, _row_spec(tm, D), _vec_spec(8, D)],
        out_shape=[jax.ShapeDtypeStruct((S, F2), BF16), jax.ShapeDtypeStruct((S, D), F32),
                   jax.ShapeDtypeStruct((1, 8, D), F32), jax.ShapeDtypeStruct((1, 8, D), F32),
                   jax.ShapeDtypeStruct((S, D), BF16), jax.ShapeDtypeStruct((8, D), F32)],
        compiler_params=_params("arbitrary"),
    )(dy, w_out, gu, w_in, x, dxn, scale, *post)


HALO = 16


def _conv_terms(bcx_ref, prev_ref, i, tm, D):
    b = bcx_ref[:, 0:D].astype(F32)
    cg = bcx_ref[:, D:2 * D].astype(F32)
    xin = bcx_ref[:, 2 * D:3 * D].astype(F32)
    z = cg * xin
    zp = prev_ref[:, D:2 * D].astype(F32) * prev_ref[:, 2 * D:3 * D].astype(F32)
    zp = jnp.where(i > 0, zp, 0.0)
    z_ext = jnp.concatenate([zp, z], axis=0)
    z1 = pltpu.roll(z_ext, 1, 0)[HALO:, :]
    z2 = pltpu.roll(z_ext, 2, 0)[HALO:, :]
    return b, cg, xin, z, z1, z2


def _conv_gate(bcx, ck, name):
    S, D3 = bcx.shape
    D = D3 // 3
    tm = _pick(S, 512, 16)
    hb = tm // HALO

    def body(bcx_ref, prev_ref, ck_ref, o_ref):
        i = pl.program_id(0)
        b, _, _, z, z1, z2 = _conv_terms(bcx_ref, prev_ref, i, tm, D)
        conv = ck_ref[0:1, :] * z2 + ck_ref[1:2, :] * z1 + ck_ref[2:3, :] * z
        o_ref[...] = (b * conv).astype(BF16)

    return pl.pallas_call(
        body, name=name, grid=(S // tm,),
        in_specs=[_row_spec(tm, D3),
                  pl.BlockSpec((HALO, D3), lambda i: (jnp.maximum(i * hb - 1, 0), 0)),
                  _vec_spec(8, D)],
        out_specs=_row_spec(tm, D),
        out_shape=jax.ShapeDtypeStruct((S, D), BF16),
        compiler_params=_params("parallel"),
    )(bcx, bcx, ck)


def _conv_gate_bwd(du, bcx, ck, name):
    S, D3 = bcx.shape
    D = D3 // 3
    tm = _pick(S, 512, 16)
    hb = tm // HALO
    nt = S // tm

    def body(du_ref, dun_ref, bcx_ref, prev_ref, next_ref, ck_ref, o_ref, dk_ref):
        i = pl.program_id(0)
        b, cg, xin, z, z1, z2 = _conv_terms(bcx_ref, prev_ref, i, tm, D)
        k0, k1, k2 = ck_ref[0:1, :], ck_ref[1:2, :], ck_ref[2:3, :]
        conv = k0 * z2 + k1 * z1 + k2 * z
        d = du_ref[...].astype(F32)
        dconv = d * b
        dcn = jnp.where(i < nt - 1, dun_ref[...].astype(F32) * next_ref[:, 0:D].astype(F32), 0.0)
        d_ext = jnp.concatenate([dconv, dcn], axis=0)
        d1 = pltpu.roll(d_ext, tm + HALO - 1, 0)[:tm, :]
        d2 = pltpu.roll(d_ext, tm + HALO - 2, 0)[:tm, :]
        dz = k2 * dconv + k1 * d1 + k0 * d2
        o_ref[:, 0:D] = (d * conv).astype(BF16)
        o_ref[:, D:2 * D] = (dz * xin).astype(BF16)
        o_ref[:, 2 * D:3 * D] = (dz * cg).astype(BF16)

        @pl.when(i == 0)
        def _():
            dk_ref[...] = jnp.zeros_like(dk_ref)

        dk_ref[0] += _colsum8(dconv * z2)
        dk_ref[1] += _colsum8(dconv * z1)
        dk_ref[2] += _colsum8(dconv * z)

    last = S // HALO - 1
    return pl.pallas_call(
        body, name=name, grid=(nt,),
        in_specs=[_row_spec(tm, D),
                  pl.BlockSpec((HALO, D), lambda i: (jnp.minimum((i + 1) * hb, last), 0)),
                  _row_spec(tm, D3),
                  pl.BlockSpec((HALO, D3), lambda i: (jnp.maximum(i * hb - 1, 0), 0)),
                  pl.BlockSpec((HALO, D3), lambda i: (jnp.minimum((i + 1) * hb, last), 0)),
                  _vec_spec(8, D)],
        out_specs=[_row_spec(tm, D3), pl.BlockSpec((3, 8, D), lambda i: (0, 0, 0))],
        out_shape=[jax.ShapeDtypeStruct((S, D3), BF16), jax.ShapeDtypeStruct((3, 8, D), F32)],
        compiler_params=_params("arbitrary"),
    )(du, du, bcx, bcx, bcx, ck)


def _rel_onehot():
    a = np.arange(CHUNK)[:, None]
    b = np.arange(CHUNK)[None, :]
    idx = np.stack([np.clip((N_LEFT_CHUNKS - dl) * CHUNK + a - b, -MAX_REL, MAX_REL) + MAX_REL
                    for dl in (6, 7, 8)]).reshape(-1)
    return (jnp.asarray(idx)[:, None] == jnp.arange(N_REL)[None, :]).astype(F32)


def _bias_table(rel_bias, name):
    H = rel_bias.shape[0]
    near = jnp.dot(rel_bias, _rel_onehot().T, precision=lax.Precision.HIGHEST).reshape(H, 3, CHUNK, CHUNK)
    far = jnp.broadcast_to(rel_bias[:, N_REL - 1][:, None, None], (H, CHUNK, CHUNK))

    def body(near_ref, far_ref, o_ref):
        neg = jnp.full((CHUNK, CHUNK), NEG, F32)
        for v in range(N_WIN):
            for ic in range(Q_CHUNKS):
                for jc in range(N_WIN * Q_CHUNKS):
                    dl = jc - ic
                    if dl < 0 or dl > N_LEFT_CHUNKS or jc < (N_WIN - 1 - v) * Q_CHUNKS:
                        blk = neg
                    else:
                        blk = far_ref[...] if dl <= 5 else near_ref[dl - 6]
                    o_ref[v, ic * CHUNK:(ic + 1) * CHUNK, jc * CHUNK:(jc + 1) * CHUNK] = blk

    return pl.pallas_call(
        body, name=name, grid=(H,),
        in_specs=[pl.BlockSpec((None, 3, CHUNK, CHUNK), lambda h: (h, 0, 0, 0)),
                  pl.BlockSpec((None, CHUNK, CHUNK), lambda h: (h, 0, 0))],
        out_specs=pl.BlockSpec((N_WIN, None, BQ, N_WIN * BQ), lambda h: (0, h, 0, 0)),
        out_shape=jax.ShapeDtypeStruct((N_WIN, H, BQ, N_WIN * BQ), F32),
        compiler_params=_params("parallel"),
    )(near, far)


def _bias_table_grad(dtab):
    H = dtab.shape[0]
    blk = lambda ic, jc: dtab[:, ic * CHUNK:(ic + 1) * CHUNK, jc * CHUNK:(jc + 1) * CHUNK]
    by_dl = [sum(blk(ic, ic + dl) for ic in range(Q_CHUNKS)) for dl in range(N_LEFT_CHUNKS + 1)]
    far = sum(jnp.sum(by_dl[dl], axis=(1, 2)) for dl in range(6))
    near = jnp.stack(by_dl[6:9], axis=1).reshape(H, 3 * CHUNK * CHUNK)
    g = jnp.dot(near, _rel_onehot(), precision=lax.Precision.HIGHEST)
    return g.at[:, N_REL - 1].add(far)


def _attn_specs(nblk, W):
    last = nblk - 1
    q_spec = pl.BlockSpec((BQ, W), lambda g, i: (jnp.minimum(i, last), g))
    kv_specs = [pl.BlockSpec((BQ, 2 * W), functools.partial(
        lambda g, i, w: (jnp.maximum(jnp.minimum(i, last) - (N_WIN - 1) + w, 0), g), w=w)) for w in range(N_WIN)]
    tab_spec = pl.BlockSpec((None, HEADS_PER_STEP, BQ, N_WIN * BQ),
                            lambda g, i: (jnp.minimum(i, N_WIN - 1), g, 0, 0))
    dtab_spec = pl.BlockSpec((HEADS_PER_STEP, BQ, N_WIN * BQ), lambda g, i: (g, 0, 0))
    return q_spec, kv_specs, tab_spec, dtab_spec


def _attn_exp(q_ref, kT, tab_ref, h, dh):
    s = jnp.dot(q_ref[:, h * dh:(h + 1) * dh], kT[h * dh:(h + 1) * dh, :], preferred_element_type=F32) + tab_ref[h]
    e = jnp.exp(s - jnp.max(s, axis=-1, keepdims=True))
    return e, jnp.sum(e, axis=-1, keepdims=True)


def _attn_fwd(q, kv, tab, name):
    S, D = q.shape
    dh = D // N_HEADS
    W = HEADS_PER_STEP * dh
    assert 2 * W == D, "the kv layout puts one head group's k beside its v: two head groups"
    q_spec, kv_specs, tab_spec, _ = _attn_specs(S // BQ, W)

    def body(q_ref, *rest):
        tab_ref, o_ref = rest[N_WIN], rest[N_WIN + 1]
        kvw = jnp.concatenate([r[...] for r in rest[:N_WIN]], axis=0)
        kT = kvw[:, :W].T
        vw = kvw[:, W:]
        outs = []
        for h in range(HEADS_PER_STEP):
            e, l = _attn_exp(q_ref, kT, tab_ref, h, dh)
            outs.append(jnp.dot(e.astype(BF16), vw[:, h * dh:(h + 1) * dh], preferred_element_type=F32) / l)
        o_ref[...] = jnp.concatenate(outs, axis=1).astype(BF16)

    return pl.pallas_call(
        body, name=name, grid=(N_HEADS // HEADS_PER_STEP, S // BQ),
        in_specs=[q_spec] + kv_specs + [tab_spec],
        out_specs=q_spec,
        out_shape=jax.ShapeDtypeStruct((S, D), BF16),
        compiler_params=_params("parallel", "parallel"),
    )(q, *([kv] * N_WIN), tab)


def _attn_bwd(q, kv, tab, do, name):
    S, D = q.shape
    dh = D // N_HEADS
    W = HEADS_PER_STEP * dh
    nblk = S // BQ
    q_spec, kv_specs, tab_spec, dtab_spec = _attn_specs(nblk, W)

    def body(q_ref, *rest):
        tab_ref, do_ref, dq_ref, dkv_ref, dtab_ref, ring = rest[N_WIN:]
        i = pl.program_id(1)

        @pl.when(i == 0)
        def _():
            dtab_ref[...] = jnp.zeros_like(dtab_ref)
            ring[...] = jnp.zeros_like(ring)

        @pl.when(i < nblk)
        def _():
            kvw = jnp.concatenate([r[...] for r in rest[:N_WIN]], axis=0)
            kT = kvw[:, :W].T
            vT = kvw[:, W:].T
            qT = q_ref[...].T
            doT = do_ref[...].T
            dqs, dks, dvs = [], [], []
            for h in range(HEADS_PER_STEP):
                hd = slice(h * dh, (h + 1) * dh)
                e, l = _attn_exp(q_ref, kT, tab_ref, h, dh)
                p = e * (1.0 / l)
                dp = jnp.dot(do_ref[:, hd], vT[hd, :], preferred_element_type=F32)
                ds = p * (dp - jnp.sum(p * dp, axis=-1, keepdims=True))
                dtab_ref[h] += ds
                dsb = ds.astype(BF16)
                dqs.append(lax.dot_general(kT[hd, :], dsb, _DIMS["nt"], preferred_element_type=F32) * (dh ** -0.5))
                dks.append(jnp.dot(qT[hd, :], dsb, preferred_element_type=F32))
                dvs.append(jnp.dot(doT[hd, :], p.astype(BF16), preferred_element_type=F32))
            dq_ref[...] = jnp.concatenate(dqs, axis=0).T.astype(BF16)
            dkv = jnp.concatenate(dks + dvs, axis=0).T
            for w in range(N_WIN):
                slot = lax.rem(i + 1 + w, N_WIN)
                part = dkv[w * BQ:(w + 1) * BQ, :]
                if w == N_WIN - 1:
                    ring[slot] = part
                else:
                    ring[slot] += part

        dkv_ref[...] = ring[lax.rem(i + 1, N_WIN)].astype(BF16)

    done_spec = pl.BlockSpec((BQ, 2 * W), lambda g, i: (jnp.maximum(i - (N_WIN - 1), 0), g))
    return pl.pallas_call(
        body, name=name, grid=(N_HEADS // HEADS_PER_STEP, nblk + N_WIN - 1),
        in_specs=[q_spec] + kv_specs + [tab_spec, q_spec],
        out_specs=[q_spec, done_spec, dtab_spec],
        out_shape=[jax.ShapeDtypeStruct((S, D), BF16), jax.ShapeDtypeStruct((S, 2 * D), BF16),
                   jax.ShapeDtypeStruct(tab.shape[1:], F32)],
        scratch_shapes=[pltpu.VMEM((N_WIN, BQ, 2 * W), F32)],
        compiler_params=_params("parallel", "arbitrary"),
    )(q, *([kv] * N_WIN), tab, do)


def _adamw(w, g, m, v, name):
    shape = w.shape
    C = shape[-1]
    R = int(np.prod(shape[:-1])) if len(shape) > 1 else 1
    whole = len(shape) >= 2 and R * C <= SMALL_TENSOR_ELEMS
    if whole:
        w2, g2, m2, v2 = w, g, m, v
    else:
        w2, g2, m2, v2 = (t.reshape(R, C) for t in (w, g, m, v))
    tr = _pick(R, max(8, (512 * 1024) // C // 8 * 8), 8)

    def body(w_ref, g_ref, m_ref, v_ref, d_ref, nm_ref, nv_ref):
        gv = g_ref[...]
        nm = ADAM_B1 * m_ref[...] + (1.0 - ADAM_B1) * gv
        nv = ADAM_B2 * v_ref[...] + (1.0 - ADAM_B2) * jnp.square(gv)
        m_hat = nm / (1.0 - ADAM_B1 ** ADAM_STEP)
        v_hat = nv / (1.0 - ADAM_B2 ** ADAM_STEP)
        d_ref[...] = -ADAM_LR * (m_hat / (jnp.sqrt(v_hat) + ADAM_EPS) + ADAM_WD * w_ref[...])
        nm_ref[...] = nm
        nv_ref[...] = nv

    if whole:
        spec, grid = pl.BlockSpec(shape, lambda i: (0,) * len(shape)), (1,)
    else:
        spec, grid = pl.BlockSpec((tr, C), lambda i: (i, 0)), (R // tr,)
    outs = pl.pallas_call(
        body, name=name, grid=grid,
        in_specs=[spec] * 4, out_specs=[spec] * 3,
        out_shape=[jax.ShapeDtypeStruct(w2.shape, F32)] * 3,
        compiler_params=_params("parallel"),
    )(w2, g2, m2, v2)
    return tuple(o.reshape(shape) for o in outs)


def _sum_rows(a, name):
    n, L = a.shape

    def body(a_ref, o_ref):
        acc = a_ref[0:1, :]
        for r in range(1, n):
            acc = acc + a_ref[r:r + 1, :]
        o_ref[...] = acc

    return pl.pallas_call(
        body, name=name, grid=(1,),
        in_specs=[pl.BlockSpec((n, L), lambda i: (0, 0))],
        out_specs=pl.BlockSpec((1, L), lambda i: (0, 0)),
        out_shape=jax.ShapeDtypeStruct((1, L), F32),
        compiler_params=_params("arbitrary"),
    )(a)


def _scalar_call(body, name, scalar, grid, in_specs, out_spec, out_shape, args):
    return pl.pallas_call(
        body, name=name,
        grid_spec=pltpu.PrefetchScalarGridSpec(num_scalar_prefetch=1, grid=grid, in_specs=in_specs,
                                               out_specs=out_spec),
        out_shape=out_shape, compiler_params=_params("parallel"),
    )(jnp.reshape(scalar, (-1,)).astype(jnp.int32), *args)


def _pair_sum(view, got, c, name):
    nb, _, rh, cols = view.shape
    tr = _pick(rh, max(16, (1 << 20) // cols // 16 * 16), 16)
    bpr = rh // tr

    def body(s_ref, a_ref, b_ref, o_ref):
        o_ref[...] = (a_ref[...].astype(F32) + b_ref[...].astype(F32)).astype(BF16)

    spec = pl.BlockSpec((tr, cols), lambda i, s: (i, 0))
    mine = pl.BlockSpec((tr, cols), lambda i, s: ((2 * (i // bpr) + s[0]) * bpr + i % bpr, 0))
    return _scalar_call(body, name, c, (nb * bpr,), [mine, spec], spec,
                        jax.ShapeDtypeStruct((nb * rh, cols), BF16),
                        (view.reshape(nb * 2 * rh, cols), got.reshape(nb * rh, cols)))


STACKED_LAYERS = 2


def _owner_sum(pair, recv, me, c, it, name, layer=None, into=None):
    _, rh, bc = recv.shape
    tr = _pick(rh, max(16, (1 << 19) // bc // 16 * 16), 16)
    bpr = rh // tr

    def body(s_ref, a_ref, r0, r1, r2, *rest):
        rest[-1][...] = ((a_ref[...].astype(F32) + r0[...].astype(F32)) + r1[...].astype(F32)) + r2[...].astype(F32)

    if it.kind == "col":
        own = pl.BlockSpec((tr, bc), lambda i, s: (i, s[0]))
    else:
        own = pl.BlockSpec((tr, bc), lambda i, s: (s[0] * bpr + i, 0))
    slots = [pl.BlockSpec((None, tr, bc), functools.partial(lambda i, s, k: (k, i, 0), k=k)) for k in range(3)]
    in_specs, args, aliases = [own] + slots, [pair, recv, recv, recv], {}
    if layer is None:
        out_spec = pl.BlockSpec((tr, bc), lambda i, s: (s[1] * bpr + i, 0))
        out_shape = jax.ShapeDtypeStruct((2 * rh, bc), F32)
    else:
        out_spec = pl.BlockSpec((None, tr, bc), lambda i, s: (layer, s[1] * bpr + i, 0))
        out_shape = jax.ShapeDtypeStruct((STACKED_LAYERS, 2 * rh, bc), F32)
        if into is not None:
            in_specs.append(pl.BlockSpec(memory_space=pl.ANY))
            args.append(into)
            aliases = {len(args): 0}
    return pl.pallas_call(
        body, name=name,
        grid_spec=pltpu.PrefetchScalarGridSpec(num_scalar_prefetch=1, grid=(bpr,), in_specs=in_specs,
                                               out_specs=out_spec),
        out_shape=out_shape, input_output_aliases=aliases, compiler_params=_params("parallel"),
    )(jnp.stack([it.pos(me), c]).astype(jnp.int32), *args)


def _place():
    x, y, c = lax.axis_index("x"), lax.axis_index("y"), lax.axis_index("c")
    chips = [(1 - x, y), (x, 1 - y), (1 - x, 1 - y)]
    return x, y, c, chips


def _chip_index(px, py):
    return 2 * px + py


def _all_gather_small(x_shard, name):
    m_per, n = x_shard.shape

    def body(x_ref, out_ref, send_sems, recv_sems, local_sem):
        x, y, c, chips = _place()
        me, sibling = (x, y, c), (x, y, 1 - c)

        def rows(px, py, pc):
            return out_ref.at[pl.ds((4 * px + 2 * py + pc) * m_per, m_per), :]

        def copy(k, block, to, src=None):
            return pltpu.make_async_remote_copy(
                src_ref=rows(*block) if src is None else src, dst_ref=rows(*block),
                send_sem=send_sems.at[k], recv_sem=recv_sems.at[k], device_id=to, device_id_type=MESH)

        mine = pltpu.make_async_copy(x_ref, rows(*me), local_sem)
        mine.start()
        first = [copy(0, me, sibling, src=x_ref)]
        first += [copy(1 + j, me, (*chip, c), src=x_ref) for j, chip in enumerate(chips)]
        for cp in first:
            cp.start()
        passed = [copy(4 + j, (*chip, c), sibling) for j, chip in enumerate(chips)]
        for j, chip in enumerate(chips):
            copy(1 + j, (*chip, c), me).wait_recv()
            passed[j].start()
        copy(0, sibling, me).wait_recv()
        for j, chip in enumerate(chips):
            copy(4 + j, (*chip, 1 - c), me).wait_recv()
        for cp in first + passed:
            cp.wait_send()
        mine.wait()

    return pl.pallas_call(
        body, name=name,
        out_shape=jax.ShapeDtypeStruct((N_DEV * m_per, n), x_shard.dtype),
        in_specs=[pl.BlockSpec(memory_space=pltpu.VMEM)],
        out_specs=pl.BlockSpec(memory_space=pltpu.VMEM),
        scratch_shapes=[pltpu.SemaphoreType.DMA((7,)), pltpu.SemaphoreType.DMA((7,)), pltpu.SemaphoreType.DMA],
    )(x_shard)


def _gather_flat(vec, name):
    L = vec.shape[0]
    Lp = -(-L // 1024) * 1024
    g = _all_gather_small(jnp.pad(vec, (0, Lp - L)).reshape(8, Lp // 8), name)
    return g.reshape(N_DEV, Lp)[:, :L]


class _Item:
    def __init__(self, kind, rows, cols, arg, layer, swap=False):
        self.kind, self.rows, self.cols, self.arg, self.layer, self.swap = kind, rows, cols, arg, layer, swap

    def ref(self, refs):
        return refs[self.arg].at[self.layer]

    def pos(self, j):
        return 2 * (j % 2) + j // 2 if self.swap else j


def _block(ref, it, j, half):
    if it.kind == "col":
        ns = it.cols // N_CHIP
        return ref.at[pl.ds(half * (it.rows // 2), it.rows // 2), pl.ds(it.pos(j) * ns, ns)]
    rs = it.rows // N_CHIP
    return ref.at[pl.ds(j * rs + half * (rs // 2), rs // 2), :]


def _cast_place(w, layer, kind, pos, after, name):
    _, r, n = w.shape
    tr = _pick(r, max(16, (1 << 20) // n // 16 * 16), 16)
    bpr = r // tr

    def body(s_ref, w_ref, after_ref, o_ref):
        o_ref[...] = w_ref[...].astype(BF16)

    if kind == "col":
        full, out_idx = (1, r, N_CHIP * n), (lambda i, s: (0, i, s[0]))
    else:
        full, out_idx = (1, N_CHIP * r, n), (lambda i, s: (0, s[0] * bpr + i, 0))
    return pl.pallas_call(
        body, name=name,
        grid_spec=pltpu.PrefetchScalarGridSpec(
            num_scalar_prefetch=1, grid=(bpr,),
            in_specs=[pl.BlockSpec((None, tr, n), lambda i, s: (layer, i, 0)), pl.BlockSpec(memory_space=pl.ANY)],
            out_specs=pl.BlockSpec((None, tr, n), out_idx)),
        out_shape=jax.ShapeDtypeStruct(full, BF16),
        compiler_params=_params("parallel"),
    )(jnp.reshape(pos, (1,)).astype(jnp.int32), w, after)


HBM_SPEC = pl.BlockSpec(memory_space=pltpu.HBM)
SEM_SPEC = pl.BlockSpec(memory_space=pltpu.SEMAPHORE)
ANY_SPEC = pl.BlockSpec(memory_space=pl.ANY)
SPLIT_PARAMS = dict(has_side_effects=pltpu.SideEffectType.DATAFLOW_SIDE_EFFECTING)


def _in_hbm(a):
    return pltpu.with_memory_space_constraint(a, pltpu.HBM)


def _split_start(copies_of, bufs, n_sem, after, name):
    n = len(bufs)

    def body(*refs):
        ins, send, recv, token = refs[:n], refs[n + 1], refs[n + 2], refs[2 * n + 3]
        for cp in copies_of(ins, send, recv, False)[0]:
            cp.start()
        token[...] = jnp.zeros_like(token)

    outs = pl.pallas_call(
        body, name=name,
        out_shape=(pltpu.SemaphoreType.DMA(n_sem), pltpu.SemaphoreType.DMA(n_sem),
                   *[pltpu.HBM(b.shape, b.dtype) for b in bufs], jax.ShapeDtypeStruct((8, 128), F32)),
        in_specs=[HBM_SPEC] * n + [ANY_SPEC],
        out_specs=(SEM_SPEC, SEM_SPEC, *[HBM_SPEC] * n, pl.BlockSpec(memory_space=pltpu.VMEM)),
        input_output_aliases={t: 2 + t for t in range(n)},
        compiler_params=pltpu.CompilerParams(**SPLIT_PARAMS),
    )(*[_in_hbm(b) for b in bufs], after)
    return outs[0], outs[1], list(outs[2:2 + n]), outs[2 + n]


def _split_wait(copies_of, send, recv, bufs, after, name):
    n = len(bufs)

    def body(*refs):
        ins, send_ref, recv_ref = refs[:n], refs[n], refs[n + 1]
        sends, arrivals = copies_of(ins, send_ref, recv_ref, True)
        for cp in sends:
            cp.wait_send()
        for cp in arrivals:
            cp.wait_recv()

    return pl.pallas_call(
        body, name=name,
        out_shape=[pltpu.HBM(b.shape, b.dtype) for b in bufs],
        in_specs=[HBM_SPEC] * n + [SEM_SPEC, SEM_SPEC, ANY_SPEC],
        out_specs=[HBM_SPEC] * n,
        input_output_aliases={t: t for t in range(n)},
        compiler_params=pltpu.CompilerParams(**SPLIT_PARAMS),
    )(*bufs, send, recv, after)


def _gather_copies(items):
    def copies_of(refs, send, recv, with_arrivals):
        x, y, c, chips = _place()
        me = _chip_index(x, y)
        sends, arrivals = [], []
        for t, it in enumerate(items):
            for k, chip in enumerate(chips):
                for core in range(2):
                    mine = _block(it.ref(refs), it, me, c)
                    sends.append(pltpu.make_async_remote_copy(
                        src_ref=mine, dst_ref=mine, send_sem=send.at[6 * t + 2 * k + core],
                        recv_sem=recv.at[6 * t + 2 * k + c], device_id=(*chip, core), device_id_type=MESH))
                    if with_arrivals:
                        landed = _block(it.ref(refs), it, _chip_index(*chip), core)
                        arrivals.append(pltpu.make_async_remote_copy(
                            src_ref=landed, dst_ref=landed, send_sem=send.at[6 * t + 2 * k + core],
                            recv_sem=recv.at[6 * t + 2 * k + core], device_id=(*chip, core), device_id_type=MESH))
        return sends, arrivals

    return copies_of


def _owner_copies(items):
    n = len(items)

    def blk(ref, it, j):
        if it.kind == "col":
            ns = it.cols // N_CHIP
            return ref.at[:, pl.ds(it.pos(j) * ns, ns)]
        return ref.at[j]

    def copies_of(refs, send, recv, with_arrivals):
        x, y, c, chips = _place()
        sends, arrivals = [], []
        for t, it in enumerate(items):
            for k, chip in enumerate(chips):
                slot = refs[n + t].at[k]
                sends.append(pltpu.make_async_remote_copy(
                    src_ref=blk(refs[t], it, _chip_index(*chip)), dst_ref=slot, send_sem=send.at[3 * t + k],
                    recv_sem=recv.at[3 * t + k], device_id=(*chip, c), device_id_type=MESH))
                if with_arrivals:
                    arrivals.append(pltpu.make_async_remote_copy(
                        src_ref=slot, dst_ref=slot, send_sem=send.at[3 * t + k], recv_sem=recv.at[3 * t + k],
                        device_id=(*chip, c), device_id_type=MESH))
        return sends, arrivals

    return copies_of


def _owner_slot_shape(it):
    if it.kind == "col":
        return (3, it.rows // 2, it.cols // N_CHIP)
    return (3, it.rows // (2 * N_CHIP), it.cols)


def _pair_view(g, it):
    if it.kind == "col":
        return g.reshape(1, 2, it.rows // 2, it.cols)
    return g.reshape(N_CHIP, 2, it.rows // (2 * N_CHIP), it.cols)


def _pair_copies(n):
    def copies_of(refs, send, recv, with_arrivals):
        x, y, c, _ = _place()
        sends, arrivals = [], []
        for t in range(n):
            land = refs[n + t]
            sends.append(pltpu.make_async_remote_copy(
                src_ref=refs[t].at[:, pl.ds(1 - c, 1)], dst_ref=land, send_sem=send.at[t], recv_sem=recv.at[t],
                device_id=(x, y, 1 - c), device_id_type=MESH))
            if with_arrivals:
                arrivals.append(pltpu.make_async_remote_copy(
                    src_ref=land, dst_ref=land, send_sem=send.at[t], recv_sem=recv.at[t],
                    device_id=(x, y, 1 - c), device_id_type=MESH))
        return sends, arrivals

    return copies_of


def _half_copies(n):
    def half(ref, which):
        r2 = ref.shape[-2] // 2
        rows = pl.ds(which * r2, r2)
        return ref.at[rows, :] if len(ref.shape) == 2 else ref.at[:, rows, :]

    def copies_of(refs, send, recv, with_arrivals):
        x, y, c, _ = _place()
        sends, arrivals = [], []
        for t in range(n):
            mine = half(refs[t], c)
            sends.append(pltpu.make_async_remote_copy(
                src_ref=mine, dst_ref=mine, send_sem=send.at[t], recv_sem=recv.at[t],
                device_id=(x, y, 1 - c), device_id_type=MESH))
            if with_arrivals:
                theirs = half(refs[t], 1 - c)
                arrivals.append(pltpu.make_async_remote_copy(
                    src_ref=theirs, dst_ref=theirs, send_sem=send.at[t], recv_sem=recv.at[t],
                    device_id=(x, y, 1 - c), device_id_type=MESH))
        return sends, arrivals

    return copies_of


class _Reduction:
    pass


def _pair_start(grads, items, after, tag, names, layer=None):
    n = len(items)
    views = [_pair_view(g, it) for g, it in zip(grads, items)]
    lands = [lax.empty((v.shape[0], 1) + v.shape[2:], v.dtype) for v in views]
    r = _Reduction()
    r.items, r.tag, r.names, r.layer = items, tag, names, layer
    r.send, r.recv, r.bufs, r.token = _split_start(_pair_copies(n), views + lands, (n,), after, f"rs_pair_start_{tag}")
    return r


def _owner_start(r, after):
    x, y, c, _ = _place()
    n = len(r.items)
    bufs = _split_wait(_pair_copies(n), r.send, r.recv, r.bufs, after, f"rs_pair_wait_{r.tag}")
    pairs = [_pair_sum(bufs[t], bufs[n + t], c, f"rs_pair_sum_{r.tag}_{t}") for t in range(n)]
    shaped = [p if it.kind == "col" else p.reshape(N_CHIP, p.shape[0] // N_CHIP, p.shape[1])
              for p, it in zip(pairs, r.items)]
    lands = [lax.empty(_owner_slot_shape(it), BF16) for it in r.items]
    r.send, r.recv, r.bufs, r.token = _split_start(
        _owner_copies(r.items), shaped + lands, (3 * n,), r.token, f"rs_owner_start_{r.tag}")
    return r


def _reduce_finish(groups, after):
    x, y, c, _ = _place()
    me = _chip_index(x, y)
    halves = {}
    for r in groups:
        n = len(r.items)
        bufs = _split_wait(_owner_copies(r.items), r.send, r.recv, r.bufs, after, f"rs_owner_wait_{r.tag}")
        for t, (it, nm) in enumerate(zip(r.items, r.names)):
            pair = bufs[t].reshape(-1, bufs[t].shape[-1])
            halves[nm] = _owner_sum(pair, bufs[n + t], me, c, it, f"rs_owner_sum_{r.tag}_{t}",
                                    layer=r.layer, into=halves.get(nm))
    n = len(halves)
    return list(halves), _split_start(_half_copies(n), list(halves.values()), (n,), after, "rs_half_start")


def _silu(v):
    return v * jax.nn.sigmoid(v)


def _sum8(p):
    return jnp.sum(p, axis=-2)


def kernel(x, c, mod_w, mod_b, norm_g, ffn_w_in, ffn_w_out, conv_w_in, conv_k, conv_w_out, kv_mod_w, kv_mod_b, kv_norm_g, w_kv, attn_w_q, attn_w_o, rel_bias, loss_target, m_mod_w, m_mod_b, m_norm_g, m_ffn_w_in, m_ffn_w_out, m_conv_w_in, m_conv_k, m_conv_w_out, m_kv_mod_w, m_kv_mod_b, m_kv_norm_g, m_w_kv, m_attn_w_q, m_attn_w_o, m_rel_bias, v_mod_w, v_mod_b, v_norm_g, v_ffn_w_in, v_ffn_w_out, v_conv_w_in, v_conv_k, v_conv_w_out, v_kv_mod_w, v_kv_mod_b, v_kv_norm_g, v_w_kv, v_attn_w_q, v_attn_w_o, v_rel_bias):
    xi, yi, ci = lax.axis_index("x"), lax.axis_index("y"), lax.axis_index("c")
    chip = 2 * xi + yi
    dev = 2 * chip + ci
    _, S, D = x.shape
    F = ffn_w_out.shape[1] * N_CHIP
    x0 = x.reshape(S, D)
    target = loss_target.reshape(S, D)
    n_mod = mod_w.shape[2]
    n_kvm = kv_mod_w.shape[1]
    dsh = D // N_CHIP
    TF = F // 2

    c_all = _all_gather_small(c.reshape(8, D // 8), "ag_c").reshape(N_DEV, D)
    sc16 = jnp.pad(_silu(c_all), ((0, 8), (0, 0)))
    part = [_mm(sc16, mod_w, "nn", F32, f"mod_fwd_{l}", b_layer=l)[:8] for l in range(2)]
    part.append(_mm(sc16, kv_mod_w, "nn", F32, "mod_fwd_kv")[:8])
    fwd_vec = jnp.concatenate([p.reshape(-1) for p in part] + [norm_g.reshape(-1), conv_k.reshape(-1)])
    fwd_all = _gather_flat(fwd_vec, "ag_fwd_small")[0::2]
    o = 0
    mods = []
    for n in (n_mod, n_mod, n_kvm):
        blk = fwd_all[:, o:o + 8 * n].reshape(N_CHIP, 8, n)
        mods.append(lax.dynamic_index_in_dim(blk, dev, axis=1, keepdims=False).reshape(N_CHIP * n))
        o += 8 * n
    ng = fwd_all[:, o:o + 8 * dsh].reshape(N_CHIP, 2, 4, dsh).transpose(1, 2, 0, 3).reshape(2, 4, D)
    o += 8 * dsh
    ck = fwd_all[:, o:o + 3 * dsh].reshape(N_CHIP, 3, dsh).transpose(1, 0, 2).reshape(3, D)
    ck8 = jnp.pad(ck, ((0, 5), (0, 0)))
    mod = [mods[l] + mod_b[l] for l in range(2)]
    sh1, sc1, g1, sh2, sc2, g2 = zip(*[jnp.split(m, 6) for m in mod])
    kv_sh, kv_sc = jnp.split(mods[2] + kv_mod_b, 2)
    row = lambda v: v.reshape(1, D)

    it_conv = [_Item("col", D, 3 * D, 0, 0), _Item("row", D, D, 1, 0)]
    it_ffn = [_Item("col", D, 2 * F, 0, 0, swap=True), _Item("row", F, D, 1, 0)]
    it_attn = [_Item("col", D, 2 * D, 0, 0, swap=True), _Item("row", D, D, 1, 0), _Item("row", D, D, 2, 0)]

    def placed(w, layer, it, nm, after=fwd_all):
        return _cast_place(w, layer, it.kind, it.pos(chip), after, f"place_{nm}")

    flying = {}

    def start(tag, its, bufs, after):
        send, recv, bufs, tok = _split_start(_gather_copies(its), bufs, (6 * len(its),), after, f"ag_start_{tag}")
        flying[tag] = (its, send, recv, bufs)
        return tok

    def arrived(tag, after):
        its, send, recv, bufs = flying[tag]
        return _split_wait(_gather_copies(its), send, recv, bufs, after, f"ag_wait_{tag}")

    one = lambda it: [_Item(it.kind, it.rows, it.cols, 0, 0, it.swap)]
    tok = start("conv_in", one(it_conv[0]), [placed(conv_w_in, 0, it_conv[0], "conv_w_in")], fwd_all)
    tok = start("conv_out", one(it_conv[1]), [placed(conv_w_out, 0, it_conv[1], "conv_w_out", tok)], tok)
    tok = start("ffn0_in", one(it_ffn[0]), [placed(ffn_w_in, 0, it_ffn[0], "ffn_w_in0", tok)], tok)
    tok = start("ffn0_out", one(it_ffn[1]), [placed(ffn_w_out, 0, it_ffn[1], "ffn_w_out0", tok)], tok)
    tok = start("attn", it_attn, [placed(w_kv[None], 0, it_attn[0], "w_kv", tok),
                                  placed(attn_w_q, 0, it_attn[1], "attn_w_q", tok),
                                  placed(attn_w_o, 0, it_attn[2], "attn_w_o", tok)], tok)
    token = start("ffn1", it_ffn, [placed(ffn_w_in, 1, it_ffn[0], "ffn_w_in1", tok),
                                   placed(ffn_w_out, 1, it_ffn[1], "ffn_w_out1", tok)], tok)

    a1 = row(ng[0, 0] * (1.0 + sc1[0])) + token[0, 0]
    (h1,) = _norm_mod(x0, a1, row(sh1[0]), "l0_norm1")
    tab = _bias_table(rel_bias[0], "l1_bias_table")
    h1, tab = lax.optimization_barrier((h1, tab))
    (W_cin,) = arrived("conv_in", h1)
    bcx = _mm(h1, W_cin, "nn", BF16, "l0_conv_in", b_layer=0, tm=512, tn=3 * D)
    ug = _conv_gate(bcx, ck8, "l0_conv_gate")
    gt1 = row(g1[0] * ng[0, 1])
    a2 = row(ng[0, 2] * (1.0 + sc2[0]))
    (W_cout,) = arrived("conv_out", ug)
    y1, x1, h2 = _mm_post(ug, W_cout, x0, gt1, "l0_conv_out", scales=a2, shifts=row(sh2[0]))
    (W_fin0,) = arrived("ffn0_in", h2)
    gu0, act0 = _ffn_in_act(h2, W_fin0, 0, "l0_ffn_in")
    (W_fout0,) = arrived("ffn0_out", act0)
    gt2 = row(g2[0] * ng[0, 3])
    a3 = ng[1, 0] * (1.0 + sc1[1])
    akv = kv_norm_g * (1.0 + kv_sc)
    y2, x2, h3, hkv = _mm_post(act0, W_fout0, x1, gt2, "l0_ffn_out",
                               scales=jnp.stack([a3, akv]), shifts=jnp.stack([sh1[1], kv_sh]))
    W_kv, W_q, W_o = arrived("attn", hkv)
    kvp = _mm(hkv, W_kv, "nn", BF16, "l1_kv", b_layer=0, tm=512, tn=2 * D)
    att_scale = (D // N_HEADS) ** -0.5
    assert math.log2(att_scale) % 1 == 0, "scaling q before its bf16 cast is exact only for a power of two"
    qp = _mm(h3, W_q, "nn", BF16, "l1_q", b_layer=0, scale=att_scale)
    oh = _attn_fwd(qp, kvp, tab, "l1_attn")
    gt3 = row(g1[1] * ng[1, 1])
    a4 = row(ng[1, 2] * (1.0 + sc2[1]))
    y3, x3, h4 = _mm_post(oh, W_o, x2, gt3, "l1_attn_out", scales=a4, shifts=row(sh2[1]))
    W_fin1, W_fout1 = arrived("ffn1", h4)
    gu1, act1 = _ffn_in_act(h4, W_fin1, 0, "l1_ffn_in")
    gt4 = row(g2[1] * ng[1, 3])
    dx4, sq, dy4, dgt4 = _mm_post(act1, W_fout1, x3, gt4, "l1_ffn_out", target=target)
    loss_part = 0.5 * jnp.sum(sq) / D

    def ffn_bwd(dy, dxn, xin_, h, gu, act, a, w_in, w_out, post, tag):
        dgu, dx, ds, db, dyn, dgt = _ffn_bwd(dy, w_out, gu, w_in, xin_, dxn, a, post, f"{tag}_ffn_bwd")
        g_fout = _mm(act, dy, "tn", BF16, f"{tag}_ffn_out_dw", tm=TF)
        g_fin = _mm(h, dgu, "tn", BF16, f"{tag}_ffn_in_dw", tn=TF)
        return dx, ds, db, dyn, dgt, g_fin, g_fout

    dx3, ds4, db4, dy3, dgt3, G_fin1, G_fout1 = ffn_bwd(dy4, dx4, x3, h4, gu1, act1, a4, W_fin1, W_fout1,
                                                        (y3, gt3), "l1")
    red = [_pair_start([G_fin1, G_fout1], it_ffn, token, "ffn1", ["ffn_w_in", "ffn_w_out"], layer=1)]
    doh = _mm(dy3, W_o, "nt", BF16, "l1_attn_out_dx", b_layer=0, after=red[0].token)
    G_o = _mm(oh, dy3, "tn", BF16, "l1_attn_out_dw")
    _owner_start(red[0], G_o)
    dq, dkv, dtab = _attn_bwd(qp, kvp, tab, doh, "l1_attn_bwd")
    d_rel = _bias_table_grad(dtab)
    G_q = _mm(h3, dq, "tn", BF16, "l1_q_dw")
    G_kv = _mm(hkv, dkv, "tn", BF16, "l1_kv_dw")
    red.append(_pair_start([G_kv, G_q, G_o], it_attn, red[-1].token, "attn", ["w_kv", "attn_w_q", "attn_w_o"]))
    dx2, ds3, db3, dy2, dgt2 = _mm_pre_bwd([(dq, W_q), (dkv, W_kv)], x2, dx3,
                                           jnp.stack([a3, akv]) + red[1].token[0, 0], "l1_qkv_dx", post=(y2, gt2))
    _owner_start(red[1], dx2)

    dx1, ds2, db2, dy1, dgt1, G_fin0, G_fout0 = ffn_bwd(dy2, dx2, x1, h2, gu0, act0, a2, W_fin0, W_fout0,
                                                        (y1, gt1), "l0")
    red.append(_pair_start([G_fin0, G_fout0], it_ffn, red[-1].token, "ffn0", ["ffn_w_in", "ffn_w_out"], layer=0))
    dug = _mm(dy1, W_cout, "nt", BF16, "l0_conv_out_dx", b_layer=0, after=red[2].token)
    G_cout = _mm(ug, dy1, "tn", BF16, "l0_conv_out_dw")
    _owner_start(red[2], G_cout)
    dbcx, dck = _conv_gate_bwd(dug, bcx, ck8, "l0_conv_gate_bwd")
    G_cin = _mm(h1, dbcx, "tn", BF16, "l0_conv_in_dw")
    red.append(_pair_start([G_cin, G_cout], it_conv, red[-1].token, "conv", ["conv_w_in", "conv_w_out"]))
    dx0, ds1, db1 = _mm_pre_bwd([(dbcx, W_cin)], x0, dx1, a1 + red[3].token[0, 0], "l0_conv_in_dx")
    ds1, db1 = _sum8(ds1)[0], _sum8(db1)[0]
    da2, db2 = _sum8(ds2)[0], _sum8(db2)[0]
    ds3, db3 = _sum8(ds3), _sum8(db3)
    da4, db4 = _sum8(ds4)[0], _sum8(db4)[0]
    dgt1, dgt2, dgt3, dgt4 = _sum8(dgt1), _sum8(dgt2), _sum8(dgt3), _sum8(dgt4)

    def dmod_of(l, ds_a, db_a, dgt_a, ds_b, db_b, dgt_b):
        return jnp.concatenate([db_a, ds_a * ng[l, 0], dgt_a * ng[l, 1], db_b, ds_b * ng[l, 2], dgt_b * ng[l, 3]])

    dmod0 = dmod_of(0, ds1, db1, dgt1, da2, db2, dgt2)
    dmod1 = dmod_of(1, ds3[0], db3[0], dgt3, da4, db4, dgt4)
    dkvmod = jnp.concatenate([db3[1], ds3[1] * kv_norm_g])
    dng = jnp.stack([
        jnp.stack([ds1 * (1.0 + sc1[0]), dgt1 * g1[0], da2 * (1.0 + sc2[0]), dgt2 * g2[0]]),
        jnp.stack([ds3[0] * (1.0 + sc1[1]), dgt3 * g1[1], da4 * (1.0 + sc2[1]), dgt4 * g2[1]])])
    dkvng = ds3[1] * (1.0 + kv_sc)
    small = [dmod0, dmod1, dkvmod, dng.reshape(-1), dkvng, _sum8(dck).reshape(-1), d_rel.reshape(-1),
             loss_part.reshape(1)]
    sizes = [int(s.shape[0]) for s in small]
    offs = np.concatenate([[0], np.cumsum(sizes)])
    bwd_all = _gather_flat(jnp.concatenate(small), "ag_bwd_small")
    _owner_start(red[3], bwd_all)
    Lb = bwd_all.shape[1]
    Lp = -(-Lb // 128) * 128
    tot = _sum_rows(jnp.pad(bwd_all, ((0, 0), (0, Lp - Lb))), "sum_small")[0]
    seg = lambda i: tot[offs[i]:offs[i + 1]]
    g_mod_b = jnp.stack([seg(0), seg(1)])
    g_kv_mod_b = seg(2)
    g_norm_g = lax.dynamic_slice_in_dim(seg(3).reshape(2, 4, D), chip * dsh, dsh, axis=2)
    g_kv_norm_g = seg(4)
    g_conv_k = lax.dynamic_slice_in_dim(seg(5).reshape(1, 3, D), chip * dsh, dsh, axis=2)
    g_rel_bias = seg(6).reshape(rel_bias.shape)
    loss = seg(7)[0]

    def dmod_w(i, n, name):
        rows_ = lax.dynamic_slice_in_dim(bwd_all[:, offs[i]:offs[i + 1]], chip * n, n, axis=1)
        return _mm(sc16, jnp.pad(rows_, ((0, 8), (0, 0))), "tn", F32, name)

    g_mod_w = jnp.stack([dmod_w(0, n_mod, "mod_bwd_0"), dmod_w(1, n_mod, "mod_bwd_1")])
    g_kv_mod_w = dmod_w(2, n_kvm, "mod_bwd_kv")

    grads = {
        "mod_w": g_mod_w, "mod_b": g_mod_b, "norm_g": g_norm_g, "conv_k": g_conv_k,
        "kv_mod_w": g_kv_mod_w, "kv_mod_b": g_kv_mod_b, "kv_norm_g": g_kv_norm_g, "rel_bias": g_rel_bias,
    }
    weights = dict(mod_w=mod_w, mod_b=mod_b, norm_g=norm_g, ffn_w_in=ffn_w_in, ffn_w_out=ffn_w_out,
                   conv_w_in=conv_w_in, conv_k=conv_k, conv_w_out=conv_w_out, kv_mod_w=kv_mod_w,
                   kv_mod_b=kv_mod_b, kv_norm_g=kv_norm_g, w_kv=w_kv, attn_w_q=attn_w_q, attn_w_o=attn_w_o,
                   rel_bias=rel_bias)
    m_in = dict(mod_w=m_mod_w, mod_b=m_mod_b, norm_g=m_norm_g, ffn_w_in=m_ffn_w_in, ffn_w_out=m_ffn_w_out,
                conv_w_in=m_conv_w_in, conv_k=m_conv_k, conv_w_out=m_conv_w_out, kv_mod_w=m_kv_mod_w,
                kv_mod_b=m_kv_mod_b, kv_norm_g=m_kv_norm_g, w_kv=m_w_kv, attn_w_q=m_attn_w_q,
                attn_w_o=m_attn_w_o, rel_bias=m_rel_bias)
    v_in = dict(mod_w=v_mod_w, mod_b=v_mod_b, norm_g=v_norm_g, ffn_w_in=v_ffn_w_in, ffn_w_out=v_ffn_w_out,
                conv_w_in=v_conv_w_in, conv_k=v_conv_k, conv_w_out=v_conv_w_out, kv_mod_w=v_kv_mod_w,
                kv_mod_b=v_kv_mod_b, kv_norm_g=v_kv_norm_g, w_kv=v_w_kv, attn_w_q=v_attn_w_q,
                attn_w_o=v_attn_w_o, rel_bias=v_rel_bias)
    names = list(weights)
    step = {}

    def update(n):
        g = grads[n].reshape(weights[n].shape)
        step[n] = (g, *_adamw(weights[n], g, m_in[n], v_in[n], f"adamw_{n}"))

    update("mod_w")
    reduced, (half_send, half_recv, half_bufs, _) = _reduce_finish(red, step["mod_w"][1])
    for n in list(grads):
        if n not in step:
            update(n)
    grads.update(zip(reduced, _split_wait(
        _half_copies(len(half_bufs)), half_send, half_recv, half_bufs, step["kv_mod_w"][1], "rs_half_wait")))
    for n in names:
        if n not in step:
            update(n)
    return (loss, dx0.reshape(x.shape), *[step[n][k] for k in range(4) for n in names])
```

```python
import functools
import math

import numpy as np
import jax
import jax.numpy as jnp
from jax import lax
from jax.experimental import pallas as pl
from jax.experimental.pallas import tpu as pltpu

CHUNK = 64
N_LEFT_CHUNKS = 8
N_HEADS = 16
MAX_REL = 2 * CHUNK
N_REL = 2 * MAX_REL + 1
EPS = 1e-6
ADAM_LR = 0.001
ADAM_B1 = 0.9
ADAM_B2 = 0.999
ADAM_EPS = 1e-08
ADAM_WD = 0.01
ADAM_STEP = 10

Q_CHUNKS = 4
BQ = Q_CHUNKS * CHUNK
N_WIN = 1 + N_LEFT_CHUNKS // Q_CHUNKS
HEADS_PER_STEP = 8
NEG = -1e30
N_DEV = 8
N_CHIP = 4
SMALL_TENSOR_ELEMS = 1 << 16

BF16 = jnp.bfloat16
F32 = jnp.float32
V7X_VMEM_LIMIT_BYTES = 56 * 1024 * 1024
MESH = pl.DeviceIdType.MESH


def _pick(n, pref, align):
    t = min(pref, n)
    t -= t % align
    while t >= align:
        if n % t == 0:
            return t
        t -= align
    return n


def _params(*sem):
    return pltpu.CompilerParams(dimension_semantics=sem, vmem_limit_bytes=V7X_VMEM_LIMIT_BYTES)


def _colsum8(v):
    r, d = v.shape
    return v.reshape(r // 8, 8, d).sum(axis=0)


_DIMS = {"nn": (((1,), (0,)), ((), ())), "nt": (((1,), (1,)), ((), ())), "tn": (((0,), (0,)), ((), ()))}


def _mm(a, b, mode, out_dtype, name, *, b_layer=None, tm=1024, tn=1024, tk=None, scale=None, after=None):
    if tk is None:
        tk = 2048 if mode == "tn" else 3072
    bs = b.shape[1:] if b_layer is not None else b.shape
    if mode == "nn":
        (M, K), (K2, N) = a.shape, bs
    elif mode == "nt":
        (M, K), (N, K2) = a.shape, bs
    else:
        (K, M), (K2, N) = a.shape, bs
    assert K == K2, (name, a.shape, b.shape)
    tm = _pick(M, tm, 128 if mode == "tn" else 16)
    tn = _pick(N, tn, 128)
    tk = _pick(K, tk, 128 if mode != "tn" else 16)
    nk = K // tk
    assert scale is None or nk == 1, name
    dims = _DIMS[mode]
    extra = [] if after is None else [after]

    def body(a_ref, b_ref, *rest):
        o_ref, acc = rest[len(extra)], rest[len(extra) + 1:]
        p = lax.dot_general(a_ref[...].astype(BF16), b_ref[...].astype(BF16), dims,
                            preferred_element_type=F32)
        if nk == 1:
            o_ref[...] = (p if scale is None else p * scale).astype(o_ref.dtype)
        else:
            k = pl.program_id(2)

            @pl.when(k == 0)
            def _():
                acc[0][...] = p

            @pl.when(k > 0)
            def _():
                acc[0][...] += p

            @pl.when(k == nk - 1)
            def _():
                o_ref[...] = acc[0][...].astype(o_ref.dtype)

    a_spec = (pl.BlockSpec((tk, tm), lambda i, j, k: (k, i)) if mode == "tn"
              else pl.BlockSpec((tm, tk), lambda i, j, k: (i, k)))
    if mode == "nt":
        b_blk, b_idx = (tn, tk), (lambda i, j, k: (j, k))
    else:
        b_blk, b_idx = (tk, tn), (lambda i, j, k: (k, j))
    if b_layer is not None:
        b_spec = pl.BlockSpec((None,) + b_blk, lambda i, j, k: (b_layer,) + b_idx(i, j, k))
    else:
        b_spec = pl.BlockSpec(b_blk, b_idx)
    return pl.pallas_call(
        body, name=name,
        grid=(M // tm, N // tn, nk),
        in_specs=[a_spec, b_spec] + [pl.BlockSpec(memory_space=pl.ANY)] * len(extra),
        out_specs=pl.BlockSpec((tm, tn), lambda i, j, k: (i, j)),
        out_shape=jax.ShapeDtypeStruct((M, N), out_dtype),
        scratch_shapes=[pltpu.VMEM((tm, tn), F32)] if nk > 1 else [],
        compiler_params=_params("parallel", "parallel", "arbitrary"),
    )(a, b, *extra)


def _row_spec(tm, d):
    return pl.BlockSpec((tm, d), lambda i: (i, 0))


def _vec_spec(r, d):
    return pl.BlockSpec((r, d), lambda i: (0, 0))


def _norm_mod(x, scales, shifts, name):
    S, D = x.shape
    nb = scales.shape[0]
    tm = _pick(S, 1024, 16)

    def body(x_ref, a_ref, b_ref, *o_refs):
        xv = x_ref[...]
        xh = xv * lax.rsqrt(jnp.mean(xv * xv, axis=-1, keepdims=True) + EPS)
        for n in range(nb):
            o_refs[n][...] = (xh * a_ref[n:n + 1, :] + b_ref[n:n + 1, :]).astype(BF16)

    return pl.pallas_call(
        body, name=name, grid=(S // tm,),
        in_specs=[_row_spec(tm, D), _vec_spec(nb, D), _vec_spec(nb, D)],
        out_specs=[_row_spec(tm, D)] * nb,
        out_shape=[jax.ShapeDtypeStruct((S, D), BF16)] * nb,
        compiler_params=_params("parallel"),
    )(x, scales, shifts)


def _mm_post(a, w, x, gate, name, *, scales=None, shifts=None, target=None, sub=256):
    M, K = a.shape
    D = w.shape[2]
    tm = _pick(M, 1024 if K <= D else 512, 16)
    sub = _pick(tm, sub, 16)
    nb = 0 if scales is None else scales.shape[0]

    def body(a_ref, w_ref, x_ref, g_ref, *rest):
        if target is None:
            sc_ref, sh_ref, y_ref, xn_ref = rest[:4]
            h_refs = rest[4:]
        else:
            t_ref, dx_ref, sq_ref, dy_ref, dg_ref = rest

            @pl.when(pl.program_id(0) == 0)
            def _():
                sq_ref[...] = jnp.zeros_like(sq_ref)
                dg_ref[...] = jnp.zeros_like(dg_ref)

        for r in range(tm // sub):
            rows = pl.ds(r * sub, sub)
            yb = jnp.dot(a_ref[rows, :], w_ref[...], preferred_element_type=F32).astype(BF16)
            yv = yb.astype(F32)
            yh = yv * lax.rsqrt(jnp.mean(yv * yv, axis=-1, keepdims=True) + EPS)
            xn = x_ref[rows, :] + yh * g_ref[...]
            if target is None:
                y_ref[rows, :] = yb
                xn_ref[rows, :] = xn
                xh = xn * lax.rsqrt(jnp.mean(xn * xn, axis=-1, keepdims=True) + EPS)
                for n in range(nb):
                    h_refs[n][rows, :] = (xh * sc_ref[n:n + 1, :] + sh_ref[n:n + 1, :]).astype(BF16)
            else:
                e = xn - t_ref[rows, :]
                dx = e / D
                dx_ref[rows, :] = dx
                sq_ref[...] += _colsum8(e * e)
                dy, dxy = _post_norm_grad(dx, yb, g_ref[...])
                dy_ref[rows, :] = dy.astype(BF16)
                dg_ref[...] += _colsum8(dxy)

    ins = [a, w, x, gate]
    in_specs = [_row_spec(tm, K), pl.BlockSpec((None, K, D), lambda i: (0, 0, 0)), _row_spec(tm, D), _vec_spec(1, D)]
    if target is None:
        ins += [scales, shifts]
        in_specs += [_vec_spec(nb, D), _vec_spec(nb, D)]
        out_specs = [_row_spec(tm, D)] * (2 + nb)
        out_shape = [jax.ShapeDtypeStruct((M, D), BF16), jax.ShapeDtypeStruct((M, D), F32)] \
            + [jax.ShapeDtypeStruct((M, D), BF16)] * nb
    else:
        ins += [target]
        in_specs += [_row_spec(tm, D)]
        out_specs = [_row_spec(tm, D), _vec_spec(8, D), _row_spec(tm, D), _vec_spec(8, D)]
        out_shape = [jax.ShapeDtypeStruct((M, D), F32), jax.ShapeDtypeStruct((8, D), F32),
                     jax.ShapeDtypeStruct((M, D), BF16), jax.ShapeDtypeStruct((8, D), F32)]
    return pl.pallas_call(
        body, name=name, grid=(M // tm,), in_specs=in_specs, out_specs=out_specs, out_shape=out_shape,
        compiler_params=_params("arbitrary" if target is not None else "parallel"),
    )(*ins)


def _post_norm_grad(dxn, yb, gate):
    yv = yb.astype(F32)
    r = lax.rsqrt(jnp.mean(yv * yv, axis=-1, keepdims=True) + EPS)
    yh = yv * r
    dyh = dxn * gate
    return r * (dyh - yh * jnp.mean(dyh * yh, axis=-1, keepdims=True)), dxn * yh


def _mm_pre_bwd(pairs, x, dxn, scales, name, post=None, sub=256):
    S, D = x.shape
    nb = len(pairs)
    tm = _pick(S, 512, 16)
    sub = _pick(tm, sub, 16)

    def body(*refs):
        a_refs, w_refs = refs[0:2 * nb:2], refs[1:2 * nb:2]
        x_ref, d_ref, sc_ref = refs[2 * nb:2 * nb + 3]
        rest = refs[2 * nb + 3:]
        if post is not None:
            y_ref, g_ref, dx_ref, ds_ref, db_ref, dy_ref, dg_ref = rest
        else:
            dx_ref, ds_ref, db_ref = rest

        @pl.when(pl.program_id(0) == 0)
        def _():
            ds_ref[...] = jnp.zeros_like(ds_ref)
            db_ref[...] = jnp.zeros_like(db_ref)
            if post is not None:
                dg_ref[...] = jnp.zeros_like(dg_ref)

        for r in range(tm // sub):
            rows = pl.ds(r * sub, sub)
            xv = x_ref[rows, :]
            rr = lax.rsqrt(jnp.mean(xv * xv, axis=-1, keepdims=True) + EPS)
            xh = xv * rr
            dxh = jnp.zeros_like(xv)
            for n in range(nb):
                dh = lax.dot_general(a_refs[n][rows, :], w_refs[n][...], _DIMS["nt"], preferred_element_type=F32)
                dxh = dxh + dh * sc_ref[n:n + 1, :]
                ds_ref[n] += _colsum8(dh * xh)
                db_ref[n] += _colsum8(dh)
            dx = d_ref[rows, :] + rr * (dxh - xh * jnp.mean(dxh * xh, axis=-1, keepdims=True))
            dx_ref[rows, :] = dx
            if post is not None:
                dy, dxy = _post_norm_grad(dx, y_ref[rows, :], g_ref[...])
                dy_ref[rows, :] = dy.astype(BF16)
                dg_ref[...] += _colsum8(dxy)

    ins, in_specs = [], []
    for a, w in pairs:
        ins += [a, w]
        in_specs += [_row_spec(tm, a.shape[1]),
                     pl.BlockSpec((None, D, a.shape[1]), lambda i: (0, 0, 0), pipeline_mode=pl.Buffered(1))]
    ins += [x, dxn, scales]
    in_specs += [_row_spec(tm, D), _row_spec(tm, D), _vec_spec(nb, D)]
    acc_spec = pl.BlockSpec((nb, 8, D), lambda i: (0, 0, 0))
    out_specs = [_row_spec(tm, D), acc_spec, acc_spec]
    out_shape = [jax.ShapeDtypeStruct((S, D), F32), jax.ShapeDtypeStruct((nb, 8, D), F32),
                 jax.ShapeDtypeStruct((nb, 8, D), F32)]
    if post is not None:
        ins += list(post)
        in_specs += [_row_spec(tm, D), _vec_spec(1, D)]
        out_specs += [_row_spec(tm, D), _vec_spec(8, D)]
        out_shape += [jax.ShapeDtypeStruct((S, D), BF16), jax.ShapeDtypeStruct((8, D), F32)]
    return pl.pallas_call(
        body, name=name, grid=(S // tm,), in_specs=in_specs, out_specs=out_specs, out_shape=out_shape,
        compiler_params=_params("arbitrary"),
    )(*ins)


FFN_PAIRS = 2
FFN_SUB_ROWS = 256


def _ffn_in_act(h, w, layer, name, tm=1024, sub=FFN_SUB_ROWS):
    S, D = h.shape
    F2 = w.shape[2]
    PW = F2 // (2 * FFN_PAIRS)
    tm = _pick(S, tm, 16)
    sub = _pick(tm, sub, 16)

    def body(h_ref, w_ref, gu_ref, a_ref):
        for r in range(tm // sub):
            rows = pl.ds(r * sub, sub)
            acc = jnp.dot(h_ref[rows, :], w_ref[...], preferred_element_type=F32)
            gu_ref[rows, :] = acc.astype(BF16)
            g = acc[:, :PW]
            a_ref[rows, :] = (g * jax.nn.sigmoid(g) * acc[:, PW:]).astype(BF16)

    return pl.pallas_call(
        body, name=name, grid=(FFN_PAIRS, S // tm),
        in_specs=[pl.BlockSpec((tm, D), lambda p, i: (i, 0)),
                  pl.BlockSpec((None, D, 2 * PW), lambda p, i: (layer, 0, p))],
        out_specs=[pl.BlockSpec((tm, 2 * PW), lambda p, i: (i, p)), pl.BlockSpec((tm, PW), lambda p, i: (i, p))],
        out_shape=[jax.ShapeDtypeStruct((S, F2), BF16), jax.ShapeDtypeStruct((S, F2 // 2), BF16)],
        compiler_params=_params("parallel", "parallel"),
    )(h, w)


def _ffn_bwd(dy, w_out, gu, w_in, x, dxn, scale, post, name):
    S, D = dy.shape
    F2 = gu.shape[1]
    PW = F2 // (2 * FFN_PAIRS)
    tm = _pick(S, 256, 16)

    def body(dy_ref, wo_ref, gu_ref, wi_ref, x_ref, d_ref, sc_ref, y_ref, g_ref,
             dgu_ref, dx_ref, ds_ref, db_ref, dyn_ref, dg_ref):
        @pl.when(pl.program_id(0) == 0)
        def _():
            ds_ref[...] = jnp.zeros_like(ds_ref)
            db_ref[...] = jnp.zeros_like(db_ref)
            dg_ref[...] = jnp.zeros_like(dg_ref)

        dh = jnp.zeros((tm, D), F32)
        for p in range(FFN_PAIRS):
            cols = slice(2 * p * PW, 2 * (p + 1) * PW)
            da = lax.dot_general(dy_ref[...], wo_ref[p * PW:(p + 1) * PW, :], _DIMS["nt"],
                                 preferred_element_type=F32)
            g = gu_ref[:, 2 * p * PW:(2 * p + 1) * PW].astype(F32)
            u = gu_ref[:, (2 * p + 1) * PW:2 * (p + 1) * PW].astype(F32)
            sg = jax.nn.sigmoid(g)
            dgu_ref[:, 2 * p * PW:(2 * p + 1) * PW] = (da * u * (sg * (1.0 + g * (1.0 - sg)))).astype(BF16)
            dgu_ref[:, (2 * p + 1) * PW:2 * (p + 1) * PW] = (da * (g * sg)).astype(BF16)
            dh = dh + lax.dot_general(dgu_ref[:, cols], wi_ref[:, cols], _DIMS["nt"], preferred_element_type=F32)
        xv = x_ref[...]
        rr = lax.rsqrt(jnp.mean(xv * xv, axis=-1, keepdims=True) + EPS)
        xh = xv * rr
        dxh = dh * sc_ref[...]
        ds_ref[0] += _colsum8(dh * xh)
        db_ref[0] += _colsum8(dh)
        dx = d_ref[...] + rr * (dxh - xh * jnp.mean(dxh * xh, axis=-1, keepdims=True))
        dx_ref[...] = dx
        dyn, dxy = _post_norm_grad(dx, y_ref[...], g_ref[...])
        dyn_ref[...] = dyn.astype(BF16)
        dg_ref[...] += _colsum8(dxy)

    resident = dict(pipeline_mode=pl.Buffered(1))
    acc_spec = pl.BlockSpec((1, 8, D), lambda i: (0, 0, 0))
    return pl.pallas_call(
        body, name=name, grid=(S // tm,),
        in_specs=[_row_spec(tm, D), pl.BlockSpec((None, F2 // 2, D), lambda i: (0, 0, 0), **resident),
                  _row_spec(tm, F2), pl.BlockSpec((None, D, F2), lambda i: (0, 0, 0), **resident),
                  _row_spec(tm, D), _row_spec(tm, D), _vec_spec(1, D), _row_spec(tm, D), _vec_spec(1, D)],
        out_specs=[_row_spec(tm, F2), _row_spec(tm, D), acc_spec, acc_spec, _row_spec(tm, D), _vec_spec(8, D)],
        out_shape=[jax.ShapeDtypeStruct((S, F2), BF16), jax.ShapeDtypeStruct((S, D), F32),
                   jax.ShapeDtypeStruct((1, 8, D), F32), jax.ShapeDtypeStruct((1, 8, D), F32),
                   jax.ShapeDtypeStruct((S, D), BF16), jax.ShapeDtypeStruct((8, D), F32)],
        compiler_params=_params("arbitrary"),
    )(dy, w_out, gu, w_in, x, dxn, scale, *post)


HALO = 16


def _conv_terms(bcx_ref, prev_ref, i, tm, D):
    b = bcx_ref[:, 0:D].astype(F32)
    cg = bcx_ref[:, D:2 * D].astype(F32)
    xin = bcx_ref[:, 2 * D:3 * D].astype(F32)
    z = cg * xin
    zp = prev_ref[:, D:2 * D].astype(F32) * prev_ref[:, 2 * D:3 * D].astype(F32)
    zp = jnp.where(i > 0, zp, 0.0)
    z_ext = jnp.concatenate([zp, z], axis=0)
    z1 = pltpu.roll(z_ext, 1, 0)[HALO:, :]
    z2 = pltpu.roll(z_ext, 2, 0)[HALO:, :]
    return b, cg, xin, z, z1, z2


def _conv_gate(bcx, ck, name):
    S, D3 = bcx.shape
    D = D3 // 3
    tm = _pick(S, 512, 16)
    hb = tm // HALO

    def body(bcx_ref, prev_ref, ck_ref, o_ref):
        i = pl.program_id(0)
        b, _, _, z, z1, z2 = _conv_terms(bcx_ref, prev_ref, i, tm, D)
        conv = ck_ref[0:1, :] * z2 + ck_ref[1:2, :] * z1 + ck_ref[2:3, :] * z
        o_ref[...] = (b * conv).astype(BF16)

    return pl.pallas_call(
        body, name=name, grid=(S // tm,),
        in_specs=[_row_spec(tm, D3),
                  pl.BlockSpec((HALO, D3), lambda i: (jnp.maximum(i * hb - 1, 0), 0)),
                  _vec_spec(8, D)],
        out_specs=_row_spec(tm, D),
        out_shape=jax.ShapeDtypeStruct((S, D), BF16),
        compiler_params=_params("parallel"),
    )(bcx, bcx, ck)


def _conv_gate_bwd(du, bcx, ck, name):
    S, D3 = bcx.shape
    D = D3 // 3
    tm = _pick(S, 512, 16)
    hb = tm // HALO
    nt = S // tm

    def body(du_ref, dun_ref, bcx_ref, prev_ref, next_ref, ck_ref, o_ref, dk_ref):
        i = pl.program_id(0)
        b, cg, xin, z, z1, z2 = _conv_terms(bcx_ref, prev_ref, i, tm, D)
        k0, k1, k2 = ck_ref[0:1, :], ck_ref[1:2, :], ck_ref[2:3, :]
        conv = k0 * z2 + k1 * z1 + k2 * z
        d = du_ref[...].astype(F32)
        dconv = d * b
        dcn = jnp.where(i < nt - 1, dun_ref[...].astype(F32) * next_ref[:, 0:D].astype(F32), 0.0)
        d_ext = jnp.concatenate([dconv, dcn], axis=0)
        d1 = pltpu.roll(d_ext, tm + HALO - 1, 0)[:tm, :]
        d2 = pltpu.roll(d_ext, tm + HALO - 2, 0)[:tm, :]
        dz = k2 * dconv + k1 * d1 + k0 * d2
        o_ref[:, 0:D] = (d * conv).astype(BF16)
        o_ref[:, D:2 * D] = (dz * xin).astype(BF16)
        o_ref[:, 2 * D:3 * D] = (dz * cg).astype(BF16)

        @pl.when(i == 0)
        def _():
            dk_ref[...] = jnp.zeros_like(dk_ref)

        dk_ref[0] += _colsum8(dconv * z2)
        dk_ref[1] += _colsum8(dconv * z1)
        dk_ref[2] += _colsum8(dconv * z)

    last = S // HALO - 1
    return pl.pallas_call(
        body, name=name, grid=(nt,),
        in_specs=[_row_spec(tm, D),
                  pl.BlockSpec((HALO, D), lambda i: (jnp.minimum((i + 1) * hb, last), 0)),
                  _row_spec(tm, D3),
                  pl.BlockSpec((HALO, D3), lambda i: (jnp.maximum(i * hb - 1, 0), 0)),
                  pl.BlockSpec((HALO, D3), lambda i: (jnp.minimum((i + 1) * hb, last), 0)),
                  _vec_spec(8, D)],
        out_specs=[_row_spec(tm, D3), pl.BlockSpec((3, 8, D), lambda i: (0, 0, 0))],
        out_shape=[jax.ShapeDtypeStruct((S, D3), BF16), jax.ShapeDtypeStruct((3, 8, D), F32)],
        compiler_params=_params("arbitrary"),
    )(du, du, bcx, bcx, bcx, ck)


def _rel_onehot():
    a = np.arange(CHUNK)[:, None]
    b = np.arange(CHUNK)[None, :]
    idx = np.stack([np.clip((N_LEFT_CHUNKS - dl) * CHUNK + a - b, -MAX_REL, MAX_REL) + MAX_REL
                    for dl in (6, 7, 8)]).reshape(-1)
    return (jnp.asarray(idx)[:, None] == jnp.arange(N_REL)[None, :]).astype(F32)


def _bias_table(rel_bias, name):
    H = rel_bias.shape[0]
    near = jnp.dot(rel_bias, _rel_onehot().T, precision=lax.Precision.HIGHEST).reshape(H, 3, CHUNK, CHUNK)
    far = jnp.broadcast_to(rel_bias[:, N_REL - 1][:, None, None], (H, CHUNK, CHUNK))

    def body(near_ref, far_ref, o_ref):
        neg = jnp.full((CHUNK, CHUNK), NEG, F32)
        for v in range(N_WIN):
            for ic in range(Q_CHUNKS):
                for jc in range(N_WIN * Q_CHUNKS):
                    dl = jc - ic
                    if dl < 0 or dl > N_LEFT_CHUNKS or jc < (N_WIN - 1 - v) * Q_CHUNKS:
                        blk = neg
                    else:
                        blk = far_ref[...] if dl <= 5 else near_ref[dl - 6]
                    o_ref[v, ic * CHUNK:(ic + 1) * CHUNK, jc * CHUNK:(jc + 1) * CHUNK] = blk

    return pl.pallas_call(
        body, name=name, grid=(H,),
        in_specs=[pl.BlockSpec((None, 3, CHUNK, CHUNK), lambda h: (h, 0, 0, 0)),
                  pl.BlockSpec((None, CHUNK, CHUNK), lambda h: (h, 0, 0))],
        out_specs=pl.BlockSpec((N_WIN, None, BQ, N_WIN * BQ), lambda h: (0, h, 0, 0)),
        out_shape=jax.ShapeDtypeStruct((N_WIN, H, BQ, N_WIN * BQ), F32),
        compiler_params=_params("parallel"),
    )(near, far)


NEAR_FIRST = 6
SLAB_ROWS = 2 * CHUNK
SLAB_COLS = 4 * CHUNK


def _slab(pair):
    c0 = (NEAR_FIRST + 2 * pair) * CHUNK
    return slice(pair * SLAB_ROWS, (pair + 1) * SLAB_ROWS), slice(c0, c0 + SLAB_COLS)


def _bias_table_grad(dslab):
    H = dslab.shape[0]

    def blk(ic, dl):
        pair, r, col = ic // 2, ic % 2, ic + dl - NEAR_FIRST - 2 * (ic // 2)
        return dslab[:, pair, r * CHUNK:(r + 1) * CHUNK, col * CHUNK:(col + 1) * CHUNK]

    by_dl = [sum(blk(ic, dl) for ic in range(Q_CHUNKS)) for dl in (6, 7, 8)]
    near = jnp.stack(by_dl, axis=1).reshape(H, 3 * CHUNK * CHUNK)
    g = jnp.dot(near, _rel_onehot(), precision=lax.Precision.HIGHEST)
    return g.at[:, N_REL - 1].add(-jnp.sum(near, axis=1))


def _attn_specs(nblk, W):
    last = nblk - 1
    q_spec = pl.BlockSpec((BQ, W), lambda g, i: (jnp.minimum(i, last), g))
    kv_specs = [pl.BlockSpec((BQ, 2 * W), functools.partial(
        lambda g, i, w: (jnp.maximum(jnp.minimum(i, last) - (N_WIN - 1) + w, 0), g), w=w)) for w in range(N_WIN)]
    tab_spec = pl.BlockSpec((None, HEADS_PER_STEP, BQ, N_WIN * BQ),
                            lambda g, i: (jnp.minimum(i, N_WIN - 1), g, 0, 0))
    dtab_spec = pl.BlockSpec((HEADS_PER_STEP, Q_CHUNKS // 2, SLAB_ROWS, SLAB_COLS), lambda g, i: (g, 0, 0, 0))
    return q_spec, kv_specs, tab_spec, dtab_spec


def _attn_exp(q_ref, kT, tab_ref, h, dh):
    s = jnp.dot(q_ref[:, h * dh:(h + 1) * dh], kT[h * dh:(h + 1) * dh, :], preferred_element_type=F32) + tab_ref[h]
    e = jnp.exp(s - jnp.max(s, axis=-1, keepdims=True))
    return e, jnp.sum(e, axis=-1, keepdims=True)


def _attn_fwd(q, kv, tab, name):
    S, D = q.shape
    dh = D // N_HEADS
    W = HEADS_PER_STEP * dh
    assert 2 * W == D, "the kv layout puts one head group's k beside its v: two head groups"
    q_spec, kv_specs, tab_spec, _ = _attn_specs(S // BQ, W)

    def body(q_ref, *rest):
        tab_ref, o_ref = rest[N_WIN], rest[N_WIN + 1]
        kvw = jnp.concatenate([r[...] for r in rest[:N_WIN]], axis=0)
        kT = kvw[:, :W].T
        vw = kvw[:, W:]
        outs = []
        for h in range(HEADS_PER_STEP):
            e, l = _attn_exp(q_ref, kT, tab_ref, h, dh)
            outs.append(jnp.dot(e.astype(BF16), vw[:, h * dh:(h + 1) * dh], preferred_element_type=F32) / l)
        o_ref[...] = jnp.concatenate(outs, axis=1).astype(BF16)

    return pl.pallas_call(
        body, name=name, grid=(N_HEADS // HEADS_PER_STEP, S // BQ),
        in_specs=[q_spec] + kv_specs + [tab_spec],
        out_specs=q_spec,
        out_shape=jax.ShapeDtypeStruct((S, D), BF16),
        compiler_params=_params("parallel", "parallel"),
    )(q, *([kv] * N_WIN), tab)


def _attn_bwd(q, kv, tab, do, name):
    S, D = q.shape
    dh = D // N_HEADS
    W = HEADS_PER_STEP * dh
    nblk = S // BQ
    q_spec, kv_specs, tab_spec, dtab_spec = _attn_specs(nblk, W)

    def body(q_ref, *rest):
        tab_ref, do_ref, dq_ref, dkv_ref, dtab_ref, ring = rest[N_WIN:]
        i = pl.program_id(1)

        @pl.when(i == 0)
        def _():
            dtab_ref[...] = jnp.zeros_like(dtab_ref)
            ring[...] = jnp.zeros_like(ring)

        @pl.when(i < nblk)
        def _():
            kvw = jnp.concatenate([r[...] for r in rest[:N_WIN]], axis=0)
            kT = kvw[:, :W].T
            vT = kvw[:, W:].T
            qT = q_ref[...].T
            dqs, dks, dvs = [], [], []
            for h in range(HEADS_PER_STEP):
                hd = slice(h * dh, (h + 1) * dh)
                e, l = _attn_exp(q_ref, kT, tab_ref, h, dh)
                inv_l = 1.0 / l
                do_h = do_ref[:, hd]
                dp = jnp.dot(do_h, vT[hd, :], preferred_element_type=F32)
                delta = jnp.sum(e * dp, axis=-1, keepdims=True) * inv_l
                ds = e * ((dp - delta) * inv_l)
                for pair in range(Q_CHUNKS // 2):
                    rows, cols = _slab(pair)
                    dtab_ref[h, pair] += ds[rows, cols]
                dsb = ds.astype(BF16)
                dqs.append(lax.dot_general(kT[hd, :], dsb, _DIMS["nt"], preferred_element_type=F32) * (dh ** -0.5))
                dks.append(jnp.dot(qT[hd, :], dsb, preferred_element_type=F32))
                do_s = (do_h.astype(F32) * inv_l).astype(BF16)
                dvs.append(jnp.dot(do_s.T, e.astype(BF16), preferred_element_type=F32))
            dq_ref[...] = jnp.concatenate(dqs, axis=0).T.astype(BF16)
            dkv = jnp.concatenate(dks + dvs, axis=0).T
            for w in range(N_WIN):
                slot = lax.rem(i + 1 + w, N_WIN)
                part = dkv[w * BQ:(w + 1) * BQ, :]
                if w == N_WIN - 1:
                    ring[slot] = part
                else:
                    ring[slot] += part

        dkv_ref[...] = ring[lax.rem(i + 1, N_WIN)].astype(BF16)

    done_spec = pl.BlockSpec((BQ, 2 * W), lambda g, i: (jnp.maximum(i - (N_WIN - 1), 0), g))
    return pl.pallas_call(
        body, name=name, grid=(N_HEADS // HEADS_PER_STEP, nblk + N_WIN - 1),
        in_specs=[q_spec] + kv_specs + [tab_spec, q_spec],
        out_specs=[q_spec, done_spec, dtab_spec],
        out_shape=[jax.ShapeDtypeStruct((S, D), BF16), jax.ShapeDtypeStruct((S, 2 * D), BF16),
                   jax.ShapeDtypeStruct((N_HEADS, Q_CHUNKS // 2, SLAB_ROWS, SLAB_COLS), F32)],
        scratch_shapes=[pltpu.VMEM((N_WIN, BQ, 2 * W), F32)],
        compiler_params=_params("parallel", "arbitrary"),
    )(q, *([kv] * N_WIN), tab, do)


def _adamw(w, g, m, v, name):
    shape = w.shape
    C = shape[-1]
    R = int(np.prod(shape[:-1])) if len(shape) > 1 else 1
    whole = len(shape) >= 2 and R * C <= SMALL_TENSOR_ELEMS
    if whole:
        w2, g2, m2, v2 = w, g, m, v
    else:
        w2, g2, m2, v2 = (t.reshape(R, C) for t in (w, g, m, v))
    tr = _pick(R, max(8, (512 * 1024) // C // 8 * 8), 8)

    def body(w_ref, g_ref, m_ref, v_ref, d_ref, nm_ref, nv_ref):
        gv = g_ref[...]
        nm = ADAM_B1 * m_ref[...] + (1.0 - ADAM_B1) * gv
        nv = ADAM_B2 * v_ref[...] + (1.0 - ADAM_B2) * jnp.square(gv)
        m_hat = nm / (1.0 - ADAM_B1 ** ADAM_STEP)
        v_hat = nv / (1.0 - ADAM_B2 ** ADAM_STEP)
        d_ref[...] = -ADAM_LR * (m_hat / (jnp.sqrt(v_hat) + ADAM_EPS) + ADAM_WD * w_ref[...])
        nm_ref[...] = nm
        nv_ref[...] = nv

    if whole:
        spec, grid = pl.BlockSpec(shape, lambda i: (0,) * len(shape)), (1,)
    else:
        spec, grid = pl.BlockSpec((tr, C), lambda i: (i, 0)), (R // tr,)
    outs = pl.pallas_call(
        body, name=name, grid=grid,
        in_specs=[spec] * 4, out_specs=[spec] * 3,
        out_shape=[jax.ShapeDtypeStruct(w2.shape, F32)] * 3,
        compiler_params=_params("parallel"),
    )(w2, g2, m2, v2)
    return tuple(o.reshape(shape) for o in outs)


def _sum_rows(a, name):
    n, L = a.shape

    def body(a_ref, o_ref):
        acc = a_ref[0:1, :]
        for r in range(1, n):
            acc = acc + a_ref[r:r + 1, :]
        o_ref[...] = acc

    return pl.pallas_call(
        body, name=name, grid=(1,),
        in_specs=[pl.BlockSpec((n, L), lambda i: (0, 0))],
        out_specs=pl.BlockSpec((1, L), lambda i: (0, 0)),
        out_shape=jax.ShapeDtypeStruct((1, L), F32),
        compiler_params=_params("arbitrary"),
    )(a)


def _scalar_call(body, name, scalar, grid, in_specs, out_spec, out_shape, args):
    return pl.pallas_call(
        body, name=name,
        grid_spec=pltpu.PrefetchScalarGridSpec(num_scalar_prefetch=1, grid=grid, in_specs=in_specs,
                                               out_specs=out_spec),
        out_shape=out_shape, compiler_params=_params("parallel"),
    )(jnp.reshape(scalar, (-1,)).astype(jnp.int32), *args)


def _pair_sum(view, got, c, name):
    nb, _, rh, cols = view.shape
    tr = _pick(rh, max(16, (1 << 20) // cols // 16 * 16), 16)
    bpr = rh // tr

    def body(s_ref, a_ref, b_ref, o_ref):
        o_ref[...] = (a_ref[...].astype(F32) + b_ref[...].astype(F32)).astype(BF16)

    spec = pl.BlockSpec((tr, cols), lambda i, s: (i, 0))
    mine = pl.BlockSpec((tr, cols), lambda i, s: ((2 * (i // bpr) + s[0]) * bpr + i % bpr, 0))
    return _scalar_call(body, name, c, (nb * bpr,), [mine, spec], spec,
                        jax.ShapeDtypeStruct((nb * rh, cols), BF16),
                        (view.reshape(nb * 2 * rh, cols), got.reshape(nb * rh, cols)))


STACKED_LAYERS = 2


def _owner_sum(pair, recv, me, c, it, name, layer=None, into=None):
    _, rh, bc = recv.shape
    tr = _pick(rh, max(16, (1 << 19) // bc // 16 * 16), 16)
    bpr = rh // tr

    def body(s_ref, a_ref, r0, r1, r2, *rest):
        rest[-1][...] = ((a_ref[...].astype(F32) + r0[...].astype(F32)) + r1[...].astype(F32)) + r2[...].astype(F32)

    if it.kind == "col":
        own = pl.BlockSpec((tr, bc), lambda i, s: (i, s[0]))
    else:
        own = pl.BlockSpec((tr, bc), lambda i, s: (s[0] * bpr + i, 0))
    slots = [pl.BlockSpec((None, tr, bc), functools.partial(lambda i, s, k: (k, i, 0), k=k)) for k in range(3)]
    in_specs, args, aliases = [own] + slots, [pair, recv, recv, recv], {}
    if layer is None:
        out_spec = pl.BlockSpec((tr, bc), lambda i, s: (s[1] * bpr + i, 0))
        out_shape = jax.ShapeDtypeStruct((2 * rh, bc), F32)
    else:
        out_spec = pl.BlockSpec((None, tr, bc), lambda i, s: (layer, s[1] * bpr + i, 0))
        out_shape = jax.ShapeDtypeStruct((STACKED_LAYERS, 2 * rh, bc), F32)
        if into is not None:
            in_specs.append(pl.BlockSpec(memory_space=pl.ANY))
            args.append(into)
            aliases = {len(args): 0}
    return pl.pallas_call(
        body, name=name,
        grid_spec=pltpu.PrefetchScalarGridSpec(num_scalar_prefetch=1, grid=(bpr,), in_specs=in_specs,
                                               out_specs=out_spec),
        out_shape=out_shape, input_output_aliases=aliases, compiler_params=_params("parallel"),
    )(jnp.stack([it.pos(me), c]).astype(jnp.int32), *args)


def _place():
    x, y, c = lax.axis_index("x"), lax.axis_index("y"), lax.axis_index("c")
    chips = [(1 - x, y), (x, 1 - y), (1 - x, 1 - y)]
    return x, y, c, chips


def _chip_index(px, py):
    return 2 * px + py


def _all_gather_small(x_shard, name):
    m_per, n = x_shard.shape

    def body(x_ref, out_ref, send_sems, recv_sems, local_sem):
        x, y, c, chips = _place()
        me, sibling = (x, y, c), (x, y, 1 - c)

        def rows(px, py, pc):
            return out_ref.at[pl.ds((4 * px + 2 * py + pc) * m_per, m_per), :]

        def copy(k, block, to, src=None):
            return pltpu.make_async_remote_copy(
                src_ref=rows(*block) if src is None else src, dst_ref=rows(*block),
                send_sem=send_sems.at[k], recv_sem=recv_sems.at[k], device_id=to, device_id_type=MESH)

        mine = pltpu.make_async_copy(x_ref, rows(*me), local_sem)
        mine.start()
        first = [copy(0, me, sibling, src=x_ref)]
        first += [copy(1 + j, me, (*chip, c), src=x_ref) for j, chip in enumerate(chips)]
        for cp in first:
            cp.start()
        passed = [copy(4 + j, (*chip, c), sibling) for j, chip in enumerate(chips)]
        for j, chip in enumerate(chips):
            copy(1 + j, (*chip, c), me).wait_recv()
            passed[j].start()
        copy(0, sibling, me).wait_recv()
        for j, chip in enumerate(chips):
            copy(4 + j, (*chip, 1 - c), me).wait_recv()
        for cp in first + passed:
            cp.wait_send()
        mine.wait()

    return pl.pallas_call(
        body, name=name,
        out_shape=jax.ShapeDtypeStruct((N_DEV * m_per, n), x_shard.dtype),
        in_specs=[pl.BlockSpec(memory_space=pltpu.VMEM)],
        out_specs=pl.BlockSpec(memory_space=pltpu.VMEM),
        scratch_shapes=[pltpu.SemaphoreType.DMA((7,)), pltpu.SemaphoreType.DMA((7,)), pltpu.SemaphoreType.DMA],
    )(x_shard)


def _gather_flat(vec, name):
    L = vec.shape[0]
    Lp = -(-L // 1024) * 1024
    g = _all_gather_small(jnp.pad(vec, (0, Lp - L)).reshape(8, Lp // 8), name)
    return g.reshape(N_DEV, Lp)[:, :L]


class _Item:
    def __init__(self, kind, rows, cols, arg, layer, swap=False):
        self.kind, self.rows, self.cols, self.arg, self.layer, self.swap = kind, rows, cols, arg, layer, swap

    def ref(self, refs):
        return refs[self.arg].at[self.layer]

    def pos(self, j):
        return 2 * (j % 2) + j // 2 if self.swap else j


def _block(ref, it, j, half):
    if it.kind == "col":
        ns = it.cols // N_CHIP
        return ref.at[pl.ds(half * (it.rows // 2), it.rows // 2), pl.ds(it.pos(j) * ns, ns)]
    rs = it.rows // N_CHIP
    return ref.at[pl.ds(j * rs + half * (rs // 2), rs // 2), :]


def _cast_place(w, layer, kind, pos, after, name):
    _, r, n = w.shape
    tr = _pick(r, max(16, (1 << 20) // n // 16 * 16), 16)
    bpr = r // tr

    def body(s_ref, w_ref, after_ref, o_ref):
        o_ref[...] = w_ref[...].astype(BF16)

    if kind == "col":
        full, out_idx = (1, r, N_CHIP * n), (lambda i, s: (0, i, s[0]))
    else:
        full, out_idx = (1, N_CHIP * r, n), (lambda i, s: (0, s[0] * bpr + i, 0))
    return pl.pallas_call(
        body, name=name,
        grid_spec=pltpu.PrefetchScalarGridSpec(
            num_scalar_prefetch=1, grid=(bpr,),
            in_specs=[pl.BlockSpec((None, tr, n), lambda i, s: (layer, i, 0)), pl.BlockSpec(memory_space=pl.ANY)],
            out_specs=pl.BlockSpec((None, tr, n), out_idx)),
        out_shape=jax.ShapeDtypeStruct(full, BF16),
        compiler_params=_params("parallel"),
    )(jnp.reshape(pos, (1,)).astype(jnp.int32), w, after)


HBM_SPEC = pl.BlockSpec(memory_space=pltpu.HBM)
SEM_SPEC = pl.BlockSpec(memory_space=pltpu.SEMAPHORE)
ANY_SPEC = pl.BlockSpec(memory_space=pl.ANY)
SPLIT_PARAMS = dict(has_side_effects=pltpu.SideEffectType.DATAFLOW_SIDE_EFFECTING)


def _in_hbm(a):
    return pltpu.with_memory_space_constraint(a, pltpu.HBM)


def _split_start(copies_of, bufs, n_sem, after, name):
    n = len(bufs)

    def body(*refs):
        ins, send, recv, token = refs[:n], refs[n + 1], refs[n + 2], refs[2 * n + 3]
        for cp in copies_of(ins, send, recv, False)[0]:
            cp.start()
        token[...] = jnp.zeros_like(token)

    outs = pl.pallas_call(
        body, name=name,
        out_shape=(pltpu.SemaphoreType.DMA(n_sem), pltpu.SemaphoreType.DMA(n_sem),
                   *[pltpu.HBM(b.shape, b.dtype) for b in bufs], jax.ShapeDtypeStruct((8, 128), F32)),
        in_specs=[HBM_SPEC] * n + [ANY_SPEC],
        out_specs=(SEM_SPEC, SEM_SPEC, *[HBM_SPEC] * n, pl.BlockSpec(memory_space=pltpu.VMEM)),
        input_output_aliases={t: 2 + t for t in range(n)},
        compiler_params=pltpu.CompilerParams(**SPLIT_PARAMS),
    )(*[_in_hbm(b) for b in bufs], after)
    return outs[0], outs[1], list(outs[2:2 + n]), outs[2 + n]


def _split_wait(copies_of, send, recv, bufs, after, name):
    n = len(bufs)

    def body(*refs):
        ins, send_ref, recv_ref = refs[:n], refs[n], refs[n + 1]
        sends, arrivals = copies_of(ins, send_ref, recv_ref, True)
        for cp in sends:
            cp.wait_send()
        for cp in arrivals:
            cp.wait_recv()

    return pl.pallas_call(
        body, name=name,
        out_shape=[pltpu.HBM(b.shape, b.dtype) for b in bufs],
        in_specs=[HBM_SPEC] * n + [SEM_SPEC, SEM_SPEC, ANY_SPEC],
        out_specs=[HBM_SPEC] * n,
        input_output_aliases={t: t for t in range(n)},
        compiler_params=pltpu.CompilerParams(**SPLIT_PARAMS),
    )(*bufs, send, recv, after)


def _gather_copies(items):
    def copies_of(refs, send, recv, with_arrivals):
        x, y, c, chips = _place()
        me = _chip_index(x, y)
        sends, arrivals = [], []
        for t, it in enumerate(items):
            for k, chip in enumerate(chips):
                for core in range(2):
                    mine = _block(it.ref(refs), it, me, c)
                    sends.append(pltpu.make_async_remote_copy(
                        src_ref=mine, dst_ref=mine, send_sem=send.at[6 * t + 2 * k + core],
                        recv_sem=recv.at[6 * t + 2 * k + c], device_id=(*chip, core), device_id_type=MESH))
                    if with_arrivals:
                        landed = _block(it.ref(refs), it, _chip_index(*chip), core)
                        arrivals.append(pltpu.make_async_remote_copy(
                            src_ref=landed, dst_ref=landed, send_sem=send.at[6 * t + 2 * k + core],
                            recv_sem=recv.at[6 * t + 2 * k + core], device_id=(*chip, core), device_id_type=MESH))
        return sends, arrivals

    return copies_of


def _owner_copies(items):
    n = len(items)

    def blk(ref, it, j):
        if it.kind == "col":
            ns = it.cols // N_CHIP
            return ref.at[:, pl.ds(it.pos(j) * ns, ns)]
        return ref.at[j]

    def copies_of(refs, send, recv, with_arrivals):
        x, y, c, chips = _place()
        sends, arrivals = [], []
        for t, it in enumerate(items):
            for k, chip in enumerate(chips):
                slot = refs[n + t].at[k]
                sends.append(pltpu.make_async_remote_copy(
                    src_ref=blk(refs[t], it, _chip_index(*chip)), dst_ref=slot, send_sem=send.at[3 * t + k],
                    recv_sem=recv.at[3 * t + k], device_id=(*chip, c), device_id_type=MESH))
                if with_arrivals:
                    arrivals.append(pltpu.make_async_remote_copy(
                        src_ref=slot, dst_ref=slot, send_sem=send.at[3 * t + k], recv_sem=recv.at[3 * t + k],
                        device_id=(*chip, c), device_id_type=MESH))
        return sends, arrivals

    return copies_of


def _owner_slot_shape(it):
    if it.kind == "col":
        return (3, it.rows // 2, it.cols // N_CHIP)
    return (3, it.rows // (2 * N_CHIP), it.cols)


def _pair_view(g, it):
    if it.kind == "col":
        return g.reshape(1, 2, it.rows // 2, it.cols)
    return g.reshape(N_CHIP, 2, it.rows // (2 * N_CHIP), it.cols)


def _pair_copies(n):
    def copies_of(refs, send, recv, with_arrivals):
        x, y, c, _ = _place()
        sends, arrivals = [], []
        for t in range(n):
            land = refs[n + t]
            sends.append(pltpu.make_async_remote_copy(
                src_ref=refs[t].at[:, pl.ds(1 - c, 1)], dst_ref=land, send_sem=send.at[t], recv_sem=recv.at[t],
                device_id=(x, y, 1 - c), device_id_type=MESH))
            if with_arrivals:
                arrivals.append(pltpu.make_async_remote_copy(
                    src_ref=land, dst_ref=land, send_sem=send.at[t], recv_sem=recv.at[t],
                    device_id=(x, y, 1 - c), device_id_type=MESH))
        return sends, arrivals

    return copies_of


def _half_copies(n):
    def half(ref, which):
        r2 = ref.shape[-2] // 2
        rows = pl.ds(which * r2, r2)
        return ref.at[rows, :] if len(ref.shape) == 2 else ref.at[:, rows, :]

    def copies_of(refs, send, recv, with_arrivals):
        x, y, c, _ = _place()
        sends, arrivals = [], []
        for t in range(n):
            mine = half(refs[t], c)
            sends.append(pltpu.make_async_remote_copy(
                src_ref=mine, dst_ref=mine, send_sem=send.at[t], recv_sem=recv.at[t],
                device_id=(x, y, 1 - c), device_id_type=MESH))
            if with_arrivals:
                theirs = half(refs[t], 1 - c)
                arrivals.append(pltpu.make_async_remote_copy(
                    src_ref=theirs, dst_ref=theirs, send_sem=send.at[t], recv_sem=recv.at[t],
                    device_id=(x, y, 1 - c), device_id_type=MESH))
        return sends, arrivals

    return copies_of


class _Reduction:
    pass


def _pair_start(grads, items, after, tag, names, layer=None):
    n = len(items)
    views = [_pair_view(g, it) for g, it in zip(grads, items)]
    lands = [lax.empty((v.shape[0], 1) + v.shape[2:], v.dtype) for v in views]
    r = _Reduction()
    r.items, r.tag, r.names, r.layer = items, tag, names, layer
    r.send, r.recv, r.bufs, r.token = _split_start(_pair_copies(n), views + lands, (n,), after, f"rs_pair_start_{tag}")
    return r


def _owner_start(r, after):
    x, y, c, _ = _place()
    n = len(r.items)
    bufs = _split_wait(_pair_copies(n), r.send, r.recv, r.bufs, after, f"rs_pair_wait_{r.tag}")
    pairs = [_pair_sum(bufs[t], bufs[n + t], c, f"rs_pair_sum_{r.tag}_{t}") for t in range(n)]
    shaped = [p if it.kind == "col" else p.reshape(N_CHIP, p.shape[0] // N_CHIP, p.shape[1])
              for p, it in zip(pairs, r.items)]
    lands = [lax.empty(_owner_slot_shape(it), BF16) for it in r.items]
    r.send, r.recv, r.bufs, r.token = _split_start(
        _owner_copies(r.items), shaped + lands, (3 * n,), r.token, f"rs_owner_start_{r.tag}")
    return r


def _reduce_finish(groups, after):
    x, y, c, _ = _place()
    me = _chip_index(x, y)
    halves = {}
    for r in groups:
        n = len(r.items)
        bufs = _split_wait(_owner_copies(r.items), r.send, r.recv, r.bufs, after, f"rs_owner_wait_{r.tag}")
        for t, (it, nm) in enumerate(zip(r.items, r.names)):
            pair = bufs[t].reshape(-1, bufs[t].shape[-1])
            halves[nm] = _owner_sum(pair, bufs[n + t], me, c, it, f"rs_owner_sum_{r.tag}_{t}",
                                    layer=r.layer, into=halves.get(nm))
    n = len(halves)
    return list(halves), _split_start(_half_copies(n), list(halves.values()), (n,), after, "rs_half_start")


def _silu(v):
    return v * jax.nn.sigmoid(v)


def _sum8(p):
    return jnp.sum(p, axis=-2)


def kernel(x, c, mod_w, mod_b, norm_g, ffn_w_in, ffn_w_out, conv_w_in, conv_k, conv_w_out, kv_mod_w, kv_mod_b, kv_norm_g, w_kv, attn_w_q, attn_w_o, rel_bias, loss_target, m_mod_w, m_mod_b, m_norm_g, m_ffn_w_in, m_ffn_w_out, m_conv_w_in, m_conv_k, m_conv_w_out, m_kv_mod_w, m_kv_mod_b, m_kv_norm_g, m_w_kv, m_attn_w_q, m_attn_w_o, m_rel_bias, v_mod_w, v_mod_b, v_norm_g, v_ffn_w_in, v_ffn_w_out, v_conv_w_in, v_conv_k, v_conv_w_out, v_kv_mod_w, v_kv_mod_b, v_kv_norm_g, v_w_kv, v_attn_w_q, v_attn_w_o, v_rel_bias):
    xi, yi, ci = lax.axis_index("x"), lax.axis_index("y"), lax.axis_index("c")
    chip = 2 * xi + yi
    dev = 2 * chip + ci
    _, S, D = x.shape
    F = ffn_w_out.shape[1] * N_CHIP
    x0 = x.reshape(S, D)
    target = loss_target.reshape(S, D)
    n_mod = mod_w.shape[2]
    n_kvm = kv_mod_w.shape[1]
    dsh = D // N_CHIP
    TF = F // 2

    c_all = _all_gather_small(c.reshape(8, D // 8), "ag_c").reshape(N_DEV, D)
    sc16 = jnp.pad(_silu(c_all), ((0, 8), (0, 0)))
    part = [_mm(sc16, mod_w, "nn", F32, f"mod_fwd_{l}", b_layer=l)[:8] for l in range(2)]
    part.append(_mm(sc16, kv_mod_w, "nn", F32, "mod_fwd_kv")[:8])
    fwd_vec = jnp.concatenate([p.reshape(-1) for p in part] + [norm_g.reshape(-1), conv_k.reshape(-1)])
    fwd_all = _gather_flat(fwd_vec, "ag_fwd_small")[0::2]
    o = 0
    mods = []
    for n in (n_mod, n_mod, n_kvm):
        blk = fwd_all[:, o:o + 8 * n].reshape(N_CHIP, 8, n)
        mods.append(lax.dynamic_index_in_dim(blk, dev, axis=1, keepdims=False).reshape(N_CHIP * n))
        o += 8 * n
    ng = fwd_all[:, o:o + 8 * dsh].reshape(N_CHIP, 2, 4, dsh).transpose(1, 2, 0, 3).reshape(2, 4, D)
    o += 8 * dsh
    ck = fwd_all[:, o:o + 3 * dsh].reshape(N_CHIP, 3, dsh).transpose(1, 0, 2).reshape(3, D)
    ck8 = jnp.pad(ck, ((0, 5), (0, 0)))
    mod = [mods[l] + mod_b[l] for l in range(2)]
    sh1, sc1, g1, sh2, sc2, g2 = zip(*[jnp.split(m, 6) for m in mod])
    kv_sh, kv_sc = jnp.split(mods[2] + kv_mod_b, 2)
    row = lambda v: v.reshape(1, D)

    it_conv = [_Item("col", D, 3 * D, 0, 0), _Item("row", D, D, 1, 0)]
    it_ffn = [_Item("col", D, 2 * F, 0, 0, swap=True), _Item("row", F, D, 1, 0)]
    it_attn = [_Item("col", D, 2 * D, 0, 0, swap=True), _Item("row", D, D, 1, 0), _Item("row", D, D, 2, 0)]

    def placed(w, layer, it, nm, after=fwd_all):
        return _cast_place(w, layer, it.kind, it.pos(chip), after, f"place_{nm}")

    flying = {}

    def start(tag, its, bufs, after):
        send, recv, bufs, tok = _split_start(_gather_copies(its), bufs, (6 * len(its),), after, f"ag_start_{tag}")
        flying[tag] = (its, send, recv, bufs)
        return tok

    def arrived(tag, after):
        its, send, recv, bufs = flying[tag]
        return _split_wait(_gather_copies(its), send, recv, bufs, after, f"ag_wait_{tag}")

    one = lambda it: [_Item(it.kind, it.rows, it.cols, 0, 0, it.swap)]
    tok = start("conv_in", one(it_conv[0]), [placed(conv_w_in, 0, it_conv[0], "conv_w_in")], fwd_all)
    tok = start("conv_out", one(it_conv[1]), [placed(conv_w_out, 0, it_conv[1], "conv_w_out", tok)], tok)
    tok = start("ffn0_in", one(it_ffn[0]), [placed(ffn_w_in, 0, it_ffn[0], "ffn_w_in0", tok)], tok)
    tok = start("ffn0_out", one(it_ffn[1]), [placed(ffn_w_out, 0, it_ffn[1], "ffn_w_out0", tok)], tok)
    tok = start("attn", it_attn, [placed(w_kv[None], 0, it_attn[0], "w_kv", tok),
                                  placed(attn_w_q, 0, it_attn[1], "attn_w_q", tok),
                                  placed(attn_w_o, 0, it_attn[2], "attn_w_o", tok)], tok)
    token = start("ffn1", it_ffn, [placed(ffn_w_in, 1, it_ffn[0], "ffn_w_in1", tok),
                                   placed(ffn_w_out, 1, it_ffn[1], "ffn_w_out1", tok)], tok)

    a1 = row(ng[0, 0] * (1.0 + sc1[0])) + token[0, 0]
    (h1,) = _norm_mod(x0, a1, row(sh1[0]), "l0_norm1")
    tab = _bias_table(rel_bias[0], "l1_bias_table")
    h1, tab = lax.optimization_barrier((h1, tab))
    (W_cin,) = arrived("conv_in", h1)
    bcx = _mm(h1, W_cin, "nn", BF16, "l0_conv_in", b_layer=0, tm=512, tn=3 * D)
    ug = _conv_gate(bcx, ck8, "l0_conv_gate")
    gt1 = row(g1[0] * ng[0, 1])
    a2 = row(ng[0, 2] * (1.0 + sc2[0]))
    (W_cout,) = arrived("conv_out", ug)
    y1, x1, h2 = _mm_post(ug, W_cout, x0, gt1, "l0_conv_out", scales=a2, shifts=row(sh2[0]))
    (W_fin0,) = arrived("ffn0_in", h2)
    gu0, act0 = _ffn_in_act(h2, W_fin0, 0, "l0_ffn_in")
    (W_fout0,) = arrived("ffn0_out", act0)
    gt2 = row(g2[0] * ng[0, 3])
    a3 = ng[1, 0] * (1.0 + sc1[1])
    akv = kv_norm_g * (1.0 + kv_sc)
    y2, x2, h3, hkv = _mm_post(act0, W_fout0, x1, gt2, "l0_ffn_out",
                               scales=jnp.stack([a3, akv]), shifts=jnp.stack([sh1[1], kv_sh]))
    W_kv, W_q, W_o = arrived("attn", hkv)
    kvp = _mm(hkv, W_kv, "nn", BF16, "l1_kv", b_layer=0, tm=512, tn=2 * D)
    att_scale = (D // N_HEADS) ** -0.5
    assert math.log2(att_scale) % 1 == 0, "scaling q before its bf16 cast is exact only for a power of two"
    qp = _mm(h3, W_q, "nn", BF16, "l1_q", b_layer=0, scale=att_scale)
    oh = _attn_fwd(qp, kvp, tab, "l1_attn")
    gt3 = row(g1[1] * ng[1, 1])
    a4 = row(ng[1, 2] * (1.0 + sc2[1]))
    y3, x3, h4 = _mm_post(oh, W_o, x2, gt3, "l1_attn_out", scales=a4, shifts=row(sh2[1]))
    W_fin1, W_fout1 = arrived("ffn1", h4)
    gu1, act1 = _ffn_in_act(h4, W_fin1, 0, "l1_ffn_in")
    gt4 = row(g2[1] * ng[1, 3])
    dx4, sq, dy4, dgt4 = _mm_post(act1, W_fout1, x3, gt4, "l1_ffn_out", target=target)
    loss_part = 0.5 * jnp.sum(sq) / D

    def ffn_bwd(dy, dxn, xin_, h, gu, act, a, w_in, w_out, post, tag):
        dgu, dx, ds, db, dyn, dgt = _ffn_bwd(dy, w_out, gu, w_in, xin_, dxn, a, post, f"{tag}_ffn_bwd")
        g_fout = _mm(act, dy, "tn", BF16, f"{tag}_ffn_out_dw", tm=TF)
        g_fin = _mm(h, dgu, "tn", BF16, f"{tag}_ffn_in_dw", tn=TF)
        return dx, ds, db, dyn, dgt, g_fin, g_fout

    dx3, ds4, db4, dy3, dgt3, G_fin1, G_fout1 = ffn_bwd(dy4, dx4, x3, h4, gu1, act1, a4, W_fin1, W_fout1,
                                                        (y3, gt3), "l1")
    red = [_pair_start([G_fin1, G_fout1], it_ffn, token, "ffn1", ["ffn_w_in", "ffn_w_out"], layer=1)]
    doh = _mm(dy3, W_o, "nt", BF16, "l1_attn_out_dx", b_layer=0, after=red[0].token)
    G_o = _mm(oh, dy3, "tn", BF16, "l1_attn_out_dw")
    _owner_start(red[0], G_o)
    dq, dkv, dtab = _attn_bwd(qp, kvp, tab, doh, "l1_attn_bwd")
    d_rel = _bias_table_grad(dtab)
    G_q = _mm(h3, dq, "tn", BF16, "l1_q_dw")
    G_kv = _mm(hkv, dkv, "tn", BF16, "l1_kv_dw")
    red.append(_pair_start([G_kv, G_q, G_o], it_attn, red[-1].token, "attn", ["w_kv", "attn_w_q", "attn_w_o"]))
    dx2, ds3, db3, dy2, dgt2 = _mm_pre_bwd([(dq, W_q), (dkv, W_kv)], x2, dx3,
                                           jnp.stack([a3, akv]) + red[1].token[0, 0], "l1_qkv_dx", post=(y2, gt2))
    _owner_start(red[1], dx2)

    dx1, ds2, db2, dy1, dgt1, G_fin0, G_fout0 = ffn_bwd(dy2, dx2, x1, h2, gu0, act0, a2, W_fin0, W_fout0,
                                                        (y1, gt1), "l0")
    red.append(_pair_start([G_fin0, G_fout0], it_ffn, red[-1].token, "ffn0", ["ffn_w_in", "ffn_w_out"], layer=0))
    dug = _mm(dy1, W_cout, "nt", BF16, "l0_conv_out_dx", b_layer=0, after=red[2].token)
    G_cout = _mm(ug, dy1, "tn", BF16, "l0_conv_out_dw")
    _owner_start(red[2], G_cout)
    dbcx, dck = _conv_gate_bwd(dug, bcx, ck8, "l0_conv_gate_bwd")
    G_cin = _mm(h1, dbcx, "tn", BF16, "l0_conv_in_dw")
    red.append(_pair_start([G_cin, G_cout], it_conv, red[-1].token, "conv", ["conv_w_in", "conv_w_out"]))
    dx0, ds1, db1 = _mm_pre_bwd([(dbcx, W_cin)], x0, dx1, a1 + red[3].token[0, 0], "l0_conv_in_dx")
    ds1, db1 = _sum8(ds1)[0], _sum8(db1)[0]
    da2, db2 = _sum8(ds2)[0], _sum8(db2)[0]
    ds3, db3 = _sum8(ds3), _sum8(db3)
    da4, db4 = _sum8(ds4)[0], _sum8(db4)[0]
    dgt1, dgt2, dgt3, dgt4 = _sum8(dgt1), _sum8(dgt2), _sum8(dgt3), _sum8(dgt4)

    def dmod_of(l, ds_a, db_a, dgt_a, ds_b, db_b, dgt_b):
        return jnp.concatenate([db_a, ds_a * ng[l, 0], dgt_a * ng[l, 1], db_b, ds_b * ng[l, 2], dgt_b * ng[l, 3]])

    dmod0 = dmod_of(0, ds1, db1, dgt1, da2, db2, dgt2)
    dmod1 = dmod_of(1, ds3[0], db3[0], dgt3, da4, db4, dgt4)
    dkvmod = jnp.concatenate([db3[1], ds3[1] * kv_norm_g])
    dng = jnp.stack([
        jnp.stack([ds1 * (1.0 + sc1[0]), dgt1 * g1[0], da2 * (1.0 + sc2[0]), dgt2 * g2[0]]),
        jnp.stack([ds3[0] * (1.0 + sc1[1]), dgt3 * g1[1], da4 * (1.0 + sc2[1]), dgt4 * g2[1]])])
    dkvng = ds3[1] * (1.0 + kv_sc)
    small = [dmod0, dmod1, dkvmod, dng.reshape(-1), dkvng, _sum8(dck).reshape(-1), d_rel.reshape(-1),
             loss_part.reshape(1)]
    sizes = [int(s.shape[0]) for s in small]
    offs = np.concatenate([[0], np.cumsum(sizes)])
    bwd_all = _gather_flat(jnp.concatenate(small), "ag_bwd_small")
    _owner_start(red[3], bwd_all)
    Lb = bwd_all.shape[1]
    Lp = -(-Lb // 128) * 128
    tot = _sum_rows(jnp.pad(bwd_all, ((0, 0), (0, Lp - Lb))), "sum_small")[0]
    seg = lambda i: tot[offs[i]:offs[i + 1]]
    g_mod_b = jnp.stack([seg(0), seg(1)])
    g_kv_mod_b = seg(2)
    g_norm_g = lax.dynamic_slice_in_dim(seg(3).reshape(2, 4, D), chip * dsh, dsh, axis=2)
    g_kv_norm_g = seg(4)
    g_conv_k = lax.dynamic_slice_in_dim(seg(5).reshape(1, 3, D), chip * dsh, dsh, axis=2)
    g_rel_bias = seg(6).reshape(rel_bias.shape)
    loss = seg(7)[0]

    def dmod_w(i, n, name):
        rows_ = lax.dynamic_slice_in_dim(bwd_all[:, offs[i]:offs[i + 1]], chip * n, n, axis=1)
        return _mm(sc16, jnp.pad(rows_, ((0, 8), (0, 0))), "tn", F32, name)

    g_mod_w = jnp.stack([dmod_w(0, n_mod, "mod_bwd_0"), dmod_w(1, n_mod, "mod_bwd_1")])
    g_kv_mod_w = dmod_w(2, n_kvm, "mod_bwd_kv")

    grads = {
        "mod_w": g_mod_w, "mod_b": g_mod_b, "norm_g": g_norm_g, "conv_k": g_conv_k,
        "kv_mod_w": g_kv_mod_w, "kv_mod_b": g_kv_mod_b, "kv_norm_g": g_kv_norm_g, "rel_bias": g_rel_bias,
    }
    weights = dict(mod_w=mod_w, mod_b=mod_b, norm_g=norm_g, ffn_w_in=ffn_w_in, ffn_w_out=ffn_w_out,
                   conv_w_in=conv_w_in, conv_k=conv_k, conv_w_out=conv_w_out, kv_mod_w=kv_mod_w,
                   kv_mod_b=kv_mod_b, kv_norm_g=kv_norm_g, w_kv=w_kv, attn_w_q=attn_w_q, attn_w_o=attn_w_o,
                   rel_bias=rel_bias)
    m_in = dict(mod_w=m_mod_w, mod_b=m_mod_b, norm_g=m_norm_g, ffn_w_in=m_ffn_w_in, ffn_w_out=m_ffn_w_out,
                conv_w_in=m_conv_w_in, conv_k=m_conv_k, conv_w_out=m_conv_w_out, kv_mod_w=m_kv_mod_w,
                kv_mod_b=m_kv_mod_b, kv_norm_g=m_kv_norm_g, w_kv=m_w_kv, attn_w_q=m_attn_w_q,
                attn_w_o=m_attn_w_o, rel_bias=m_rel_bias)
    v_in = dict(mod_w=v_mod_w, mod_b=v_mod_b, norm_g=v_norm_g, ffn_w_in=v_ffn_w_in, ffn_w_out=v_ffn_w_out,
                conv_w_in=v_conv_w_in, conv_k=v_conv_k, conv_w_out=v_conv_w_out, kv_mod_w=v_kv_mod_w,
                kv_mod_b=v_kv_mod_b, kv_norm_g=v_kv_norm_g, w_kv=v_w_kv, attn_w_q=v_attn_w_q,
                attn_w_o=v_attn_w_o, rel_bias=v_rel_bias)
    names = list(weights)
    step = {}

    def update(n):
        g = grads[n].reshape(weights[n].shape)
        step[n] = (g, *_adamw(weights[n], g, m_in[n], v_in[n], f"adamw_{n}"))

    update("mod_w")
    reduced, (half_send, half_recv, half_bufs, _) = _reduce_finish(red, step["mod_w"][1])
    for n in list(grads):
        if n not in step:
            update(n)
    grads.update(zip(reduced, _split_wait(
        _half_copies(len(half_bufs)), half_send, half_recv, half_bufs, step["kv_mod_w"][1], "rs_half_wait")))
    for n in names:
        if n not in step:
            update(n)
    return (loss, dx0.reshape(x.shape), *[step[n][k] for k in range(4) for n in names])
```

```python
import functools
import math

import numpy as np
import jax
import jax.numpy as jnp
from jax import lax
from jax.experimental import pallas as pl
from jax.experimental.pallas import tpu as pltpu

CHUNK = 64
N_LEFT_CHUNKS = 8
N_HEADS = 16
MAX_REL = 2 * CHUNK
N_REL = 2 * MAX_REL + 1
EPS = 1e-6
ADAM_LR = 0.001
ADAM_B1 = 0.9
ADAM_B2 = 0.999
ADAM_EPS = 1e-08
ADAM_WD = 0.01
ADAM_STEP = 10

Q_CHUNKS = 4
BQ = Q_CHUNKS * CHUNK
N_WIN = 1 + N_LEFT_CHUNKS // Q_CHUNKS
HEADS_PER_STEP = 8
NEG = -1e30
N_DEV = 8
N_CHIP = 4
SMALL_TENSOR_ELEMS = 1 << 16

BF16 = jnp.bfloat16
F32 = jnp.float32
V7X_VMEM_LIMIT_BYTES = 56 * 1024 * 1024
MESH = pl.DeviceIdType.MESH


def _pick(n, pref, align):
    t = min(pref, n)
    t -= t % align
    while t >= align:
        if n % t == 0:
            return t
        t -= align
    return n


def _params(*sem):
    return pltpu.CompilerParams(dimension_semantics=sem, vmem_limit_bytes=V7X_VMEM_LIMIT_BYTES)


def _colsum8(v):
    r, d = v.shape
    return v.reshape(r // 8, 8, d).sum(axis=0)


_DIMS = {"nn": (((1,), (0,)), ((), ())), "nt": (((1,), (1,)), ((), ())), "tn": (((0,), (0,)), ((), ()))}


def _mm(a, b, mode, out_dtype, name, *, b_layer=None, tm=1024, tn=1024, tk=None, scale=None, after=None):
    if tk is None:
        tk = 2048 if mode == "tn" else 3072
    bs = b.shape[1:] if b_layer is not None else b.shape
    if mode == "nn":
        (M, K), (K2, N) = a.shape, bs
    elif mode == "nt":
        (M, K), (N, K2) = a.shape, bs
    else:
        (K, M), (K2, N) = a.shape, bs
    assert K == K2, (name, a.shape, b.shape)
    tm = _pick(M, tm, 128 if mode == "tn" else 16)
    tn = _pick(N, tn, 128)
    tk = _pick(K, tk, 128 if mode != "tn" else 16)
    nk = K // tk
    assert scale is None or nk == 1, name
    dims = _DIMS[mode]
    extra = [] if after is None else [after]

    def body(a_ref, b_ref, *rest):
        o_ref, acc = rest[len(extra)], rest[len(extra) + 1:]
        p = lax.dot_general(a_ref[...].astype(BF16), b_ref[...].astype(BF16), dims,
                            preferred_element_type=F32)
        if nk == 1:
            o_ref[...] = (p if scale is None else p * scale).astype(o_ref.dtype)
        else:
            k = pl.program_id(2)

            @pl.when(k == 0)
            def _():
                acc[0][...] = p

            @pl.when(k > 0)
            def _():
                acc[0][...] += p

            @pl.when(k == nk - 1)
            def _():
                o_ref[...] = acc[0][...].astype(o_ref.dtype)

    a_spec = (pl.BlockSpec((tk, tm), lambda i, j, k: (k, i)) if mode == "tn"
              else pl.BlockSpec((tm, tk), lambda i, j, k: (i, k)))
    if mode == "nt":
        b_blk, b_idx = (tn, tk), (lambda i, j, k: (j, k))
    else:
        b_blk, b_idx = (tk, tn), (lambda i, j, k: (k, j))
    if b_layer is not None:
        b_spec = pl.BlockSpec((None,) + b_blk, lambda i, j, k: (b_layer,) + b_idx(i, j, k))
    else:
        b_spec = pl.BlockSpec(b_blk, b_idx)
    return pl.pallas_call(
        body, name=name,
        grid=(M // tm, N // tn, nk),
        in_specs=[a_spec, b_spec] + [pl.BlockSpec(memory_space=pl.ANY)] * len(extra),
        out_specs=pl.BlockSpec((tm, tn), lambda i, j, k: (i, j)),
        out_shape=jax.ShapeDtypeStruct((M, N), out_dtype),
        scratch_shapes=[pltpu.VMEM((tm, tn), F32)] if nk > 1 else [],
        compiler_params=_params("parallel", "parallel", "arbitrary"),
    )(a, b, *extra)


def _row_spec(tm, d):
    return pl.BlockSpec((tm, d), lambda i: (i, 0))


def _vec_spec(r, d):
    return pl.BlockSpec((r, d), lambda i: (0, 0))


def _norm_mod(x, scales, shifts, name):
    S, D = x.shape
    nb = scales.shape[0]
    tm = _pick(S, 1024, 16)

    def body(x_ref, a_ref, b_ref, *o_refs):
        xv = x_ref[...]
        xh = xv * lax.rsqrt(jnp.mean(xv * xv, axis=-1, keepdims=True) + EPS)
        for n in range(nb):
            o_refs[n][...] = (xh * a_ref[n:n + 1, :] + b_ref[n:n + 1, :]).astype(BF16)

    return pl.pallas_call(
        body, name=name, grid=(S // tm,),
        in_specs=[_row_spec(tm, D), _vec_spec(nb, D), _vec_spec(nb, D)],
        out_specs=[_row_spec(tm, D)] * nb,
        out_shape=[jax.ShapeDtypeStruct((S, D), BF16)] * nb,
        compiler_params=_params("parallel"),
    )(x, scales, shifts)


def _mm_post(a, w, x, gate, name, *, scales=None, shifts=None, target=None, sub=256):
    M, K = a.shape
    D = w.shape[2]
    tm = _pick(M, 1024 if K <= D else 512, 16)
    sub = _pick(tm, sub, 16)
    nb = 0 if scales is None else scales.shape[0]

    def body(a_ref, w_ref, x_ref, g_ref, *rest):
        if target is None:
            sc_ref, sh_ref, y_ref, xn_ref = rest[:4]
            h_refs = rest[4:]
        else:
            t_ref, dx_ref, sq_ref, dy_ref, dg_ref = rest

            @pl.when(pl.program_id(0) == 0)
            def _():
                sq_ref[...] = jnp.zeros_like(sq_ref)
                dg_ref[...] = jnp.zeros_like(dg_ref)

        for r in range(tm // sub):
            rows = pl.ds(r * sub, sub)
            yb = jnp.dot(a_ref[rows, :], w_ref[...], preferred_element_type=F32).astype(BF16)
            yv = yb.astype(F32)
            yh = yv * lax.rsqrt(jnp.mean(yv * yv, axis=-1, keepdims=True) + EPS)
            xn = x_ref[rows, :] + yh * g_ref[...]
            if target is None:
                y_ref[rows, :] = yb
                xn_ref[rows, :] = xn
                xh = xn * lax.rsqrt(jnp.mean(xn * xn, axis=-1, keepdims=True) + EPS)
                for n in range(nb):
                    h_refs[n][rows, :] = (xh * sc_ref[n:n + 1, :] + sh_ref[n:n + 1, :]).astype(BF16)
            else:
                e = xn - t_ref[rows, :]
                dx = e / D
                dx_ref[rows, :] = dx
                sq_ref[...] += _colsum8(e * e)
                dy, dxy = _post_norm_grad(dx, yb, g_ref[...])
                dy_ref[rows, :] = dy.astype(BF16)
                dg_ref[...] += _colsum8(dxy)

    ins = [a, w, x, gate]
    in_specs = [_row_spec(tm, K), pl.BlockSpec((None, K, D), lambda i: (0, 0, 0)), _row_spec(tm, D), _vec_spec(1, D)]
    if target is None:
        ins += [scales, shifts]
        in_specs += [_vec_spec(nb, D), _vec_spec(nb, D)]
        out_specs = [_row_spec(tm, D)] * (2 + nb)
        out_shape = [jax.ShapeDtypeStruct((M, D), BF16), jax.ShapeDtypeStruct((M, D), F32)] \
            + [jax.ShapeDtypeStruct((M, D), BF16)] * nb
    else:
        ins += [target]
        in_specs += [_row_spec(tm, D)]
        out_specs = [_row_spec(tm, D), _vec_spec(8, D), _row_spec(tm, D), _vec_spec(8, D)]
        out_shape = [jax.ShapeDtypeStruct((M, D), F32), jax.ShapeDtypeStruct((8, D), F32),
                     jax.ShapeDtypeStruct((M, D), BF16), jax.ShapeDtypeStruct((8, D), F32)]
    return pl.pallas_call(
        body, name=name, grid=(M // tm,), in_specs=in_specs, out_specs=out_specs, out_shape=out_shape,
        compiler_params=_params("arbitrary" if target is not None else "parallel"),
    )(*ins)


def _post_norm_grad(dxn, yb, gate):
    yv = yb.astype(F32)
    r = lax.rsqrt(jnp.mean(yv * yv, axis=-1, keepdims=True) + EPS)
    yh = yv * r
    dyh = dxn * gate
    return r * (dyh - yh * jnp.mean(dyh * yh, axis=-1, keepdims=True)), dxn * yh


def _mm_pre_bwd(pairs, x, dxn, scales, name, post=None, sub=256):
    S, D = x.shape
    nb = len(pairs)
    tm = _pick(S, 512, 16)
    sub = _pick(tm, sub, 16)

    def body(*refs):
        a_refs, w_refs = refs[0:2 * nb:2], refs[1:2 * nb:2]
        x_ref, d_ref, sc_ref = refs[2 * nb:2 * nb + 3]
        rest = refs[2 * nb + 3:]
        if post is not None:
            y_ref, g_ref, dx_ref, ds_ref, db_ref, dy_ref, dg_ref = rest
        else:
            dx_ref, ds_ref, db_ref = rest

        @pl.when(pl.program_id(0) == 0)
        def _():
            ds_ref[...] = jnp.zeros_like(ds_ref)
            db_ref[...] = jnp.zeros_like(db_ref)
            if post is not None:
                dg_ref[...] = jnp.zeros_like(dg_ref)

        for r in range(tm // sub):
            rows = pl.ds(r * sub, sub)
            xv = x_ref[rows, :]
            rr = lax.rsqrt(jnp.mean(xv * xv, axis=-1, keepdims=True) + EPS)
            xh = xv * rr
            dxh = jnp.zeros_like(xv)
            for n in range(nb):
                dh = lax.dot_general(a_refs[n][rows, :], w_refs[n][...], _DIMS["nt"], preferred_element_type=F32)
                dxh = dxh + dh * sc_ref[n:n + 1, :]
                ds_ref[n] += _colsum8(dh * xh)
                db_ref[n] += _colsum8(dh)
            dx = d_ref[rows, :] + rr * (dxh - xh * jnp.mean(dxh * xh, axis=-1, keepdims=True))
            dx_ref[rows, :] = dx
            if post is not None:
                dy, dxy = _post_norm_grad(dx, y_ref[rows, :], g_ref[...])
                dy_ref[rows, :] = dy.astype(BF16)
                dg_ref[...] += _colsum8(dxy)

    ins, in_specs = [], []
    for a, w in pairs:
        ins += [a, w]
        in_specs += [_row_spec(tm, a.shape[1]),
                     pl.BlockSpec((None, D, a.shape[1]), lambda i: (0, 0, 0), pipeline_mode=pl.Buffered(1))]
    ins += [x, dxn, scales]
    in_specs += [_row_spec(tm, D), _row_spec(tm, D), _vec_spec(nb, D)]
    acc_spec = pl.BlockSpec((nb, 8, D), lambda i: (0, 0, 0))
    out_specs = [_row_spec(tm, D), acc_spec, acc_spec]
    out_shape = [jax.ShapeDtypeStruct((S, D), F32), jax.ShapeDtypeStruct((nb, 8, D), F32),
                 jax.ShapeDtypeStruct((nb, 8, D), F32)]
    if post is not None:
        ins += list(post)
        in_specs += [_row_spec(tm, D), _vec_spec(1, D)]
        out_specs += [_row_spec(tm, D), _vec_spec(8, D)]
        out_shape += [jax.ShapeDtypeStruct((S, D), BF16), jax.ShapeDtypeStruct((8, D), F32)]
    return pl.pallas_call(
        body, name=name, grid=(S // tm,), in_specs=in_specs, out_specs=out_specs, out_shape=out_shape,
        compiler_params=_params("arbitrary"),
    )(*ins)


FFN_PAIRS = 2
FFN_SUB_ROWS = 256


def _ffn_in_act(h, w, layer, name, tm=1024, sub=FFN_SUB_ROWS):
    S, D = h.shape
    F2 = w.shape[2]
    PW = F2 // (2 * FFN_PAIRS)
    tm = _pick(S, tm, 16)
    sub = _pick(tm, sub, 16)

    def body(h_ref, w_ref, gu_ref, a_ref):
        for r in range(tm // sub):
            rows = pl.ds(r * sub, sub)
            acc = jnp.dot(h_ref[rows, :], w_ref[...], preferred_element_type=F32)
            gu_ref[rows, :] = acc.astype(BF16)
            g = acc[:, :PW]
            a_ref[rows, :] = (g * jax.nn.sigmoid(g) * acc[:, PW:]).astype(BF16)

    return pl.pallas_call(
        body, name=name, grid=(FFN_PAIRS, S // tm),
        in_specs=[pl.BlockSpec((tm, D), lambda p, i: (i, 0)),
                  pl.BlockSpec((None, D, 2 * PW), lambda p, i: (layer, 0, p))],
        out_specs=[pl.BlockSpec((tm, 2 * PW), lambda p, i: (i, p)), pl.BlockSpec((tm, PW), lambda p, i: (i, p))],
        out_shape=[jax.ShapeDtypeStruct((S, F2), BF16), jax.ShapeDtypeStruct((S, F2 // 2), BF16)],
        compiler_params=_params("parallel", "parallel"),
    )(h, w)


def _ffn_bwd(dy, w_out, gu, w_in, x, dxn, scale, post, name):
    S, D = dy.shape
    F2 = gu.shape[1]
    PW = F2 // (2 * FFN_PAIRS)
    tm = _pick(S, 256, 16)

    def body(dy_ref, wo_ref, gu_ref, wi_ref, x_ref, d_ref, sc_ref, y_ref, g_ref,
             dgu_ref, dx_ref, ds_ref, db_ref, dyn_ref, dg_ref):
        @pl.when(pl.program_id(0) == 0)
        def _():
            ds_ref[...] = jnp.zeros_like(ds_ref)
            db_ref[...] = jnp.zeros_like(db_ref)
            dg_ref[...] = jnp.zeros_like(dg_ref)

        dh = jnp.zeros((tm, D), F32)
        for p in range(FFN_PAIRS):
            cols = slice(2 * p * PW, 2 * (p + 1) * PW)
            da = lax.dot_general(dy_ref[...], wo_ref[p * PW:(p + 1) * PW, :], _DIMS["nt"],
                                 preferred_element_type=F32)
            g = gu_ref[:, 2 * p * PW:(2 * p + 1) * PW].astype(F32)
            u = gu_ref[:, (2 * p + 1) * PW:2 * (p + 1) * PW].astype(F32)
            sg = jax.nn.sigmoid(g)
            dgu_ref[:, 2 * p * PW:(2 * p + 1) * PW] = (da * u * (sg * (1.0 + g * (1.0 - sg)))).astype(BF16)
            dgu_ref[:, (2 * p + 1) * PW:2 * (p + 1) * PW] = (da * (g * sg)).astype(BF16)
            dh = dh + lax.dot_general(dgu_ref[:, cols], wi_ref[:, cols], _DIMS["nt"], preferred_element_type=F32)
        xv = x_ref[...]
        rr = lax.rsqrt(jnp.mean(xv * xv, axis=-1, keepdims=True) + EPS)
        xh = xv * rr
        dxh = dh * sc_ref[...]
        ds_ref[0] += _colsum8(dh * xh)
        db_ref[0] += _colsum8(dh)
        dx = d_ref[...] + rr * (dxh - xh * jnp.mean(dxh * xh, axis=-1, keepdims=True))
        dx_ref[...] = dx
        dyn, dxy = _post_norm_grad(dx, y_ref[...], g_ref[...])
        dyn_ref[...] = dyn.astype(BF16)
        dg_ref[...] += _colsum8(dxy)

    resident = dict(pipeline_mode=pl.Buffered(1))
    acc_spec = pl.BlockSpec((1, 8, D), lambda i: (0, 0, 0))
    return pl.pallas_call(
        body, name=name, grid=(S // tm,),
        in_specs=[_row_spec(tm, D), pl.BlockSpec((None, F2 // 2, D), lambda i: (0, 0, 0), **resident),
                  _row_spec(tm, F2), pl.BlockSpec((None, D, F2), lambda i: (0, 0, 0), **resident),
                  _row_spec(tm, D), _row_spec(tm, D), _vec_spec(1, D), _row_spec(tm, D), _vec_spec(1, D)],
        out_specs=[_row_spec(tm, F2), _row_spec(tm, D), acc_spec, acc_spec, _row_spec(tm, D), _vec_spec(8, D)],
        out_shape=[jax.ShapeDtypeStruct((S, F2), BF16), jax.ShapeDtypeStruct((S, D), F32),
                   jax.ShapeDtypeStruct((1, 8, D), F32), jax.ShapeDtypeStruct((1, 8, D), F32),
                   jax.ShapeDtypeStruct((S, D), BF16), jax.ShapeDtypeStruct((8, D), F32)],
        compiler_params=_params("arbitrary"),
    )(dy, w_out, gu, w_in, x, dxn, scale, *post)


HALO = 16


def _conv_terms(bcx_ref, prev_ref, i, tm, D):
    b = bcx_ref[:, 0:D].astype(F32)
    cg = bcx_ref[:, D:2 * D].astype(F32)
    xin = bcx_ref[:, 2 * D:3 * D].astype(F32)
    z = cg * xin
    zp = prev_ref[:, D:2 * D].astype(F32) * prev_ref[:, 2 * D:3 * D].astype(F32)
    zp = jnp.where(i > 0, zp, 0.0)
    z_ext = jnp.concatenate([zp, z], axis=0)
    z1 = pltpu.roll(z_ext, 1, 0)[HALO:, :]
    z2 = pltpu.roll(z_ext, 2, 0)[HALO:, :]
    return b, cg, xin, z, z1, z2


def _conv_gate(bcx, ck, name):
    S, D3 = bcx.shape
    D = D3 // 3
    tm = _pick(S, 512, 16)
    hb = tm // HALO

    def body(bcx_ref, prev_ref, ck_ref, o_ref):
        i = pl.program_id(0)
        b, _, _, z, z1, z2 = _conv_terms(bcx_ref, prev_ref, i, tm, D)
        conv = ck_ref[0:1, :] * z2 + ck_ref[1:2, :] * z1 + ck_ref[2:3, :] * z
        o_ref[...] = (b * conv).astype(BF16)

    return pl.pallas_call(
        body, name=name, grid=(S // tm,),
        in_specs=[_row_spec(tm, D3),
                  pl.BlockSpec((HALO, D3), lambda i: (jnp.maximum(i * hb - 1, 0), 0)),
                  _vec_spec(8, D)],
        out_specs=_row_spec(tm, D),
        out_shape=jax.ShapeDtypeStruct((S, D), BF16),
        compiler_params=_params("parallel"),
    )(bcx, bcx, ck)


def _conv_gate_bwd(du, bcx, ck, name):
    S, D3 = bcx.shape
    D = D3 // 3
    tm = _pick(S, 512, 16)
    hb = tm // HALO
    nt = S // tm

    def body(du_ref, dun_ref, bcx_ref, prev_ref, next_ref, ck_ref, o_ref, dk_ref):
        i = pl.program_id(0)
        b, cg, xin, z, z1, z2 = _conv_terms(bcx_ref, prev_ref, i, tm, D)
        k0, k1, k2 = ck_ref[0:1, :], ck_ref[1:2, :], ck_ref[2:3, :]
        conv = k0 * z2 + k1 * z1 + k2 * z
        d = du_ref[...].astype(F32)
        dconv = d * b
        dcn = jnp.where(i < nt - 1, dun_ref[...].astype(F32) * next_ref[:, 0:D].astype(F32), 0.0)
        d_ext = jnp.concatenate([dconv, dcn], axis=0)
        d1 = pltpu.roll(d_ext, tm + HALO - 1, 0)[:tm, :]
        d2 = pltpu.roll(d_ext, tm + HALO - 2, 0)[:tm, :]
        dz = k2 * dconv + k1 * d1 + k0 * d2
        o_ref[:, 0:D] = (d * conv).astype(BF16)
        o_ref[:, D:2 * D] = (dz * xin).astype(BF16)
        o_ref[:, 2 * D:3 * D] = (dz * cg).astype(BF16)

        @pl.when(i == 0)
        def _():
            dk_ref[...] = jnp.zeros_like(dk_ref)

        dk_ref[0] += _colsum8(dconv * z2)
        dk_ref[1] += _colsum8(dconv * z1)
        dk_ref[2] += _colsum8(dconv * z)

    last = S // HALO - 1
    return pl.pallas_call(
        body, name=name, grid=(nt,),
        in_specs=[_row_spec(tm, D),
                  pl.BlockSpec((HALO, D), lambda i: (jnp.minimum((i + 1) * hb, last), 0)),
                  _row_spec(tm, D3),
                  pl.BlockSpec((HALO, D3), lambda i: (jnp.maximum(i * hb - 1, 0), 0)),
                  pl.BlockSpec((HALO, D3), lambda i: (jnp.minimum((i + 1) * hb, last), 0)),
                  _vec_spec(8, D)],
        out_specs=[_row_spec(tm, D3), pl.BlockSpec((3, 8, D), lambda i: (0, 0, 0))],
        out_shape=[jax.ShapeDtypeStruct((S, D3), BF16), jax.ShapeDtypeStruct((3, 8, D), F32)],
        compiler_params=_params("arbitrary"),
    )(du, du, bcx, bcx, bcx, ck)


def _rel_onehot():
    a = np.arange(CHUNK)[:, None]
    b = np.arange(CHUNK)[None, :]
    idx = np.stack([np.clip((N_LEFT_CHUNKS - dl) * CHUNK + a - b, -MAX_REL, MAX_REL) + MAX_REL
                    for dl in (6, 7, 8)]).reshape(-1)
    return (jnp.asarray(idx)[:, None] == jnp.arange(N_REL)[None, :]).astype(F32)


def _bias_table(rel_bias, name):
    H = rel_bias.shape[0]
    near = jnp.dot(rel_bias, _rel_onehot().T, precision=lax.Precision.HIGHEST).reshape(H, 3, CHUNK, CHUNK)
    far = jnp.broadcast_to(rel_bias[:, N_REL - 1][:, None, None], (H, CHUNK, CHUNK))

    def body(near_ref, far_ref, o_ref):
        neg = jnp.full((CHUNK, CHUNK), NEG, F32)
        for v in range(N_WIN):
            for ic in range(Q_CHUNKS):
                for jc in range(N_WIN * Q_CHUNKS):
                    dl = jc - ic
                    if dl < 0 or dl > N_LEFT_CHUNKS or jc < (N_WIN - 1 - v) * Q_CHUNKS:
                        blk = neg
                    else:
                        blk = far_ref[...] if dl <= 5 else near_ref[dl - 6]
                    o_ref[v, ic * CHUNK:(ic + 1) * CHUNK, jc * CHUNK:(jc + 1) * CHUNK] = blk

    return pl.pallas_call(
        body, name=name, grid=(H,),
        in_specs=[pl.BlockSpec((None, 3, CHUNK, CHUNK), lambda h: (h, 0, 0, 0)),
                  pl.BlockSpec((None, CHUNK, CHUNK), lambda h: (h, 0, 0))],
        out_specs=pl.BlockSpec((N_WIN, None, BQ, N_WIN * BQ), lambda h: (0, h, 0, 0)),
        out_shape=jax.ShapeDtypeStruct((N_WIN, H, BQ, N_WIN * BQ), F32),
        compiler_params=_params("parallel"),
    )(near, far)


NEAR_FIRST = 6
SLAB_ROWS = 2 * CHUNK
SLAB_COLS = 4 * CHUNK


def _slab(pair):
    c0 = (NEAR_FIRST + 2 * pair) * CHUNK
    return slice(pair * SLAB_ROWS, (pair + 1) * SLAB_ROWS), slice(c0, c0 + SLAB_COLS)


def _bias_table_grad(dslab):
    H = dslab.shape[0]

    def blk(ic, dl):
        pair, r, col = ic // 2, ic % 2, ic + dl - NEAR_FIRST - 2 * (ic // 2)
        return dslab[:, pair, r * CHUNK:(r + 1) * CHUNK, col * CHUNK:(col + 1) * CHUNK]

    by_dl = [sum(blk(ic, dl) for ic in range(Q_CHUNKS)) for dl in (6, 7, 8)]
    near = jnp.stack(by_dl, axis=1).reshape(H, 3 * CHUNK * CHUNK)
    g = jnp.dot(near, _rel_onehot(), precision=lax.Precision.HIGHEST)
    return g.at[:, N_REL - 1].add(-jnp.sum(near, axis=1))


def _attn_specs(nblk, W):
    last = nblk - 1
    q_spec = pl.BlockSpec((BQ, W), lambda g, i: (jnp.minimum(i, last), g))
    kv_specs = [pl.BlockSpec((BQ, 2 * W), functools.partial(
        lambda g, i, w: (jnp.maximum(jnp.minimum(i, last) - (N_WIN - 1) + w, 0), g), w=w)) for w in range(N_WIN)]
    tab_spec = pl.BlockSpec((None, HEADS_PER_STEP, BQ, N_WIN * BQ),
                            lambda g, i: (jnp.minimum(i, N_WIN - 1), g, 0, 0))
    dtab_spec = pl.BlockSpec((HEADS_PER_STEP, Q_CHUNKS // 2, SLAB_ROWS, SLAB_COLS), lambda g, i: (g, 0, 0, 0))
    return q_spec, kv_specs, tab_spec, dtab_spec


def _attn_scores(q_ref, kT, tab_ref, h, dh):
    return jnp.dot(q_ref[:, h * dh:(h + 1) * dh], kT[h * dh:(h + 1) * dh, :], preferred_element_type=F32) + tab_ref[h]


def _attn_fwd(q, kv, tab, name):
    S, D = q.shape
    dh = D // N_HEADS
    W = HEADS_PER_STEP * dh
    assert 2 * W == D, "the kv layout puts one head group's k beside its v: two head groups"
    q_spec, kv_specs, tab_spec, _ = _attn_specs(S // BQ, W)

    def body(q_ref, *rest):
        tab_ref, o_ref = rest[N_WIN], rest[N_WIN + 1]
        kvw = jnp.concatenate([r[...] for r in rest[:N_WIN]], axis=0)
        kT = kvw[:, :W].T
        vw = kvw[:, W:]
        outs = []
        s = _attn_scores(q_ref, kT, tab_ref, 0, dh)
        for h in range(HEADS_PER_STEP):
            s_next = _attn_scores(q_ref, kT, tab_ref, h + 1, dh) if h + 1 < HEADS_PER_STEP else None
            e = jnp.exp(s - jnp.max(s, axis=-1, keepdims=True))
            l = jnp.sum(e, axis=-1, keepdims=True)
            outs.append(jnp.dot(e.astype(BF16), vw[:, h * dh:(h + 1) * dh], preferred_element_type=F32) / l)
            s = s_next
        o_ref[...] = jnp.concatenate(outs, axis=1).astype(BF16)

    return pl.pallas_call(
        body, name=name, grid=(N_HEADS // HEADS_PER_STEP, S // BQ),
        in_specs=[q_spec] + kv_specs + [tab_spec],
        out_specs=q_spec,
        out_shape=jax.ShapeDtypeStruct((S, D), BF16),
        compiler_params=_params("parallel", "parallel"),
    )(q, *([kv] * N_WIN), tab)


def _attn_bwd(q, kv, tab, do, name):
    S, D = q.shape
    dh = D // N_HEADS
    W = HEADS_PER_STEP * dh
    nblk = S // BQ
    q_spec, kv_specs, tab_spec, dtab_spec = _attn_specs(nblk, W)

    def body(q_ref, *rest):
        tab_ref, do_ref, dq_ref, dkv_ref, dtab_ref, ring = rest[N_WIN:]
        i = pl.program_id(1)

        @pl.when(i == 0)
        def _():
            dtab_ref[...] = jnp.zeros_like(dtab_ref)
            ring[...] = jnp.zeros_like(ring)

        @pl.when(i < nblk)
        def _():
            kvw = jnp.concatenate([r[...] for r in rest[:N_WIN]], axis=0)
            kT = kvw[:, :W].T
            vT = kvw[:, W:].T
            qT = q_ref[...].T
            dqs, dks, dvs = [], [], []

            s = _attn_scores(q_ref, kT, tab_ref, 0, dh)
            for h in range(HEADS_PER_STEP):
                hd = slice(h * dh, (h + 1) * dh)
                do_h = do_ref[:, hd]
                dp = jnp.dot(do_h, vT[hd, :], preferred_element_type=F32)
                e = jnp.exp(s - jnp.max(s, axis=-1, keepdims=True))
                inv_l = 1.0 / jnp.sum(e, axis=-1, keepdims=True)
                if h + 1 < HEADS_PER_STEP:
                    s = _attn_scores(q_ref, kT, tab_ref, h + 1, dh)
                delta = jnp.sum(e * dp, axis=-1, keepdims=True) * inv_l
                ds = e * ((dp - delta) * inv_l)
                for pair in range(Q_CHUNKS // 2):
                    rows, cols = _slab(pair)
                    dtab_ref[h, pair] += ds[rows, cols]
                dsb = ds.astype(BF16)
                dqs.append(lax.dot_general(kT[hd, :], dsb, _DIMS["nt"], preferred_element_type=F32) * (dh ** -0.5))
                dks.append(jnp.dot(qT[hd, :], dsb, preferred_element_type=F32))
                do_s = (do_h.astype(F32) * inv_l).astype(BF16)
                dvs.append(jnp.dot(do_s.T, e.astype(BF16), preferred_element_type=F32))
            dq_ref[...] = jnp.concatenate(dqs, axis=0).T.astype(BF16)
            dkv = jnp.concatenate(dks + dvs, axis=0).T
            for w in range(N_WIN):
                slot = lax.rem(i + 1 + w, N_WIN)
                part = dkv[w * BQ:(w + 1) * BQ, :]
                if w == N_WIN - 1:
                    ring[slot] = part
                else:
                    ring[slot] += part

        dkv_ref[...] = ring[lax.rem(i + 1, N_WIN)].astype(BF16)

    done_spec = pl.BlockSpec((BQ, 2 * W), lambda g, i: (jnp.maximum(i - (N_WIN - 1), 0), g))
    return pl.pallas_call(
        body, name=name, grid=(N_HEADS // HEADS_PER_STEP, nblk + N_WIN - 1),
        in_specs=[q_spec] + kv_specs + [tab_spec, q_spec],
        out_specs=[q_spec, done_spec, dtab_spec],
        out_shape=[jax.ShapeDtypeStruct((S, D), BF16), jax.ShapeDtypeStruct((S, 2 * D), BF16),
                   jax.ShapeDtypeStruct((N_HEADS, Q_CHUNKS // 2, SLAB_ROWS, SLAB_COLS), F32)],
        scratch_shapes=[pltpu.VMEM((N_WIN, BQ, 2 * W), F32)],
        compiler_params=_params("parallel", "arbitrary"),
    )(q, *([kv] * N_WIN), tab, do)


def _adamw(w, g, m, v, name):
    shape = w.shape
    C = shape[-1]
    R = int(np.prod(shape[:-1])) if len(shape) > 1 else 1
    whole = len(shape) >= 2 and R * C <= SMALL_TENSOR_ELEMS
    if whole:
        w2, g2, m2, v2 = w, g, m, v
    else:
        w2, g2, m2, v2 = (t.reshape(R, C) for t in (w, g, m, v))
    tr = _pick(R, max(8, (512 * 1024) // C // 8 * 8), 8)

    def body(w_ref, g_ref, m_ref, v_ref, d_ref, nm_ref, nv_ref):
        gv = g_ref[...]
        nm = ADAM_B1 * m_ref[...] + (1.0 - ADAM_B1) * gv
        nv = ADAM_B2 * v_ref[...] + (1.0 - ADAM_B2) * jnp.square(gv)
        m_hat = nm / (1.0 - ADAM_B1 ** ADAM_STEP)
        v_hat = nv / (1.0 - ADAM_B2 ** ADAM_STEP)
        d_ref[...] = -ADAM_LR * (m_hat / (jnp.sqrt(v_hat) + ADAM_EPS) + ADAM_WD * w_ref[...])
        nm_ref[...] = nm
        nv_ref[...] = nv

    if whole:
        spec, grid = pl.BlockSpec(shape, lambda i: (0,) * len(shape)), (1,)
    else:
        spec, grid = pl.BlockSpec((tr, C), lambda i: (i, 0)), (R // tr,)
    outs = pl.pallas_call(
        body, name=name, grid=grid,
        in_specs=[spec] * 4, out_specs=[spec] * 3,
        out_shape=[jax.ShapeDtypeStruct(w2.shape, F32)] * 3,
        compiler_params=_params("parallel"),
    )(w2, g2, m2, v2)
    return tuple(o.reshape(shape) for o in outs)


def _sum_rows(a, name):
    n, L = a.shape

    def body(a_ref, o_ref):
        acc = a_ref[0:1, :]
        for r in range(1, n):
            acc = acc + a_ref[r:r + 1, :]
        o_ref[...] = acc

    return pl.pallas_call(
        body, name=name, grid=(1,),
        in_specs=[pl.BlockSpec((n, L), lambda i: (0, 0))],
        out_specs=pl.BlockSpec((1, L), lambda i: (0, 0)),
        out_shape=jax.ShapeDtypeStruct((1, L), F32),
        compiler_params=_params("arbitrary"),
    )(a)


def _scalar_call(body, name, scalar, grid, in_specs, out_spec, out_shape, args):
    return pl.pallas_call(
        body, name=name,
        grid_spec=pltpu.PrefetchScalarGridSpec(num_scalar_prefetch=1, grid=grid, in_specs=in_specs,
                                               out_specs=out_spec),
        out_shape=out_shape, compiler_params=_params("parallel"),
    )(jnp.reshape(scalar, (-1,)).astype(jnp.int32), *args)


def _pair_sum(view, got, c, name):
    nb, _, rh, cols = view.shape
    tr = _pick(rh, max(16, (1 << 20) // cols // 16 * 16), 16)
    bpr = rh // tr

    def body(s_ref, a_ref, b_ref, o_ref):
        o_ref[...] = (a_ref[...].astype(F32) + b_ref[...].astype(F32)).astype(BF16)

    spec = pl.BlockSpec((tr, cols), lambda i, s: (i, 0))
    mine = pl.BlockSpec((tr, cols), lambda i, s: ((2 * (i // bpr) + s[0]) * bpr + i % bpr, 0))
    return _scalar_call(body, name, c, (nb * bpr,), [mine, spec], spec,
                        jax.ShapeDtypeStruct((nb * rh, cols), BF16),
                        (view.reshape(nb * 2 * rh, cols), got.reshape(nb * rh, cols)))


STACKED_LAYERS = 2


def _owner_sum(pair, recv, me, c, it, name, layer=None, into=None):
    _, rh, bc = recv.shape
    tr = _pick(rh, max(16, (1 << 19) // bc // 16 * 16), 16)
    bpr = rh // tr

    def body(s_ref, a_ref, r0, r1, r2, *rest):
        rest[-1][...] = ((a_ref[...].astype(F32) + r0[...].astype(F32)) + r1[...].astype(F32)) + r2[...].astype(F32)

    if it.kind == "col":
        own = pl.BlockSpec((tr, bc), lambda i, s: (i, s[0]))
    else:
        own = pl.BlockSpec((tr, bc), lambda i, s: (s[0] * bpr + i, 0))
    slots = [pl.BlockSpec((None, tr, bc), functools.partial(lambda i, s, k: (k, i, 0), k=k)) for k in range(3)]
    in_specs, args, aliases = [own] + slots, [pair, recv, recv, recv], {}
    if layer is None:
        out_spec = pl.BlockSpec((tr, bc), lambda i, s: (s[1] * bpr + i, 0))
        out_shape = jax.ShapeDtypeStruct((2 * rh, bc), F32)
    else:
        out_spec = pl.BlockSpec((None, tr, bc), lambda i, s: (layer, s[1] * bpr + i, 0))
        out_shape = jax.ShapeDtypeStruct((STACKED_LAYERS, 2 * rh, bc), F32)
        if into is not None:
            in_specs.append(pl.BlockSpec(memory_space=pl.ANY))
            args.append(into)
            aliases = {len(args): 0}
    return pl.pallas_call(
        body, name=name,
        grid_spec=pltpu.PrefetchScalarGridSpec(num_scalar_prefetch=1, grid=(bpr,), in_specs=in_specs,
                                               out_specs=out_spec),
        out_shape=out_shape, input_output_aliases=aliases, compiler_params=_params("parallel"),
    )(jnp.stack([it.pos(me), c]).astype(jnp.int32), *args)


def _place():
    x, y, c = lax.axis_index("x"), lax.axis_index("y"), lax.axis_index("c")
    chips = [(1 - x, y), (x, 1 - y), (1 - x, 1 - y)]
    return x, y, c, chips


def _chip_index(px, py):
    return 2 * px + py


def _all_gather_small(x_shard, name):
    m_per, n = x_shard.shape

    def body(x_ref, out_ref, send_sems, recv_sems, local_sem):
        x, y, c, chips = _place()
        me, sibling = (x, y, c), (x, y, 1 - c)

        def rows(px, py, pc):
            return out_ref.at[pl.ds((4 * px + 2 * py + pc) * m_per, m_per), :]

        def copy(k, block, to, src=None):
            return pltpu.make_async_remote_copy(
                src_ref=rows(*block) if src is None else src, dst_ref=rows(*block),
                send_sem=send_sems.at[k], recv_sem=recv_sems.at[k], device_id=to, device_id_type=MESH)

        mine = pltpu.make_async_copy(x_ref, rows(*me), local_sem)
        mine.start()
        first = [copy(0, me, sibling, src=x_ref)]
        first += [copy(1 + j, me, (*chip, c), src=x_ref) for j, chip in enumerate(chips)]
        for cp in first:
            cp.start()
        passed = [copy(4 + j, (*chip, c), sibling) for j, chip in enumerate(chips)]
        for j, chip in enumerate(chips):
            copy(1 + j, (*chip, c), me).wait_recv()
            passed[j].start()
        copy(0, sibling, me).wait_recv()
        for j, chip in enumerate(chips):
            copy(4 + j, (*chip, 1 - c), me).wait_recv()
        for cp in first + passed:
            cp.wait_send()
        mine.wait()

    return pl.pallas_call(
        body, name=name,
        out_shape=jax.ShapeDtypeStruct((N_DEV * m_per, n), x_shard.dtype),
        in_specs=[pl.BlockSpec(memory_space=pltpu.VMEM)],
        out_specs=pl.BlockSpec(memory_space=pltpu.VMEM),
        scratch_shapes=[pltpu.SemaphoreType.DMA((7,)), pltpu.SemaphoreType.DMA((7,)), pltpu.SemaphoreType.DMA],
    )(x_shard)


def _gather_flat(vec, name):
    L = vec.shape[0]
    Lp = -(-L // 1024) * 1024
    g = _all_gather_small(jnp.pad(vec, (0, Lp - L)).reshape(8, Lp // 8), name)
    return g.reshape(N_DEV, Lp)[:, :L]


class _Item:
    def __init__(self, kind, rows, cols, arg, layer, swap=False):
        self.kind, self.rows, self.cols, self.arg, self.layer, self.swap = kind, rows, cols, arg, layer, swap

    def ref(self, refs):
        return refs[self.arg].at[self.layer]

    def pos(self, j):
        return 2 * (j % 2) + j // 2 if self.swap else j


def _block(ref, it, j, half):
    if it.kind == "col":
        ns = it.cols // N_CHIP
        return ref.at[pl.ds(half * (it.rows // 2), it.rows // 2), pl.ds(it.pos(j) * ns, ns)]
    rs = it.rows // N_CHIP
    return ref.at[pl.ds(j * rs + half * (rs // 2), rs // 2), :]


def _cast_place(w, layer, kind, pos, after, name):
    _, r, n = w.shape
    tr = _pick(r, max(16, (1 << 20) // n // 16 * 16), 16)
    bpr = r // tr

    def body(s_ref, w_ref, after_ref, o_ref):
        o_ref[...] = w_ref[...].astype(BF16)

    if kind == "col":
        full, out_idx = (1, r, N_CHIP * n), (lambda i, s: (0, i, s[0]))
    else:
        full, out_idx = (1, N_CHIP * r, n), (lambda i, s: (0, s[0] * bpr + i, 0))
    return pl.pallas_call(
        body, name=name,
        grid_spec=pltpu.PrefetchScalarGridSpec(
            num_scalar_prefetch=1, grid=(bpr,),
            in_specs=[pl.BlockSpec((None, tr, n), lambda i, s: (layer, i, 0)), pl.BlockSpec(memory_space=pl.ANY)],
            out_specs=pl.BlockSpec((None, tr, n), out_idx)),
        out_shape=jax.ShapeDtypeStruct(full, BF16),
        compiler_params=_params("parallel"),
    )(jnp.reshape(pos, (1,)).astype(jnp.int32), w, after)


HBM_SPEC = pl.BlockSpec(memory_space=pltpu.HBM)
SEM_SPEC = pl.BlockSpec(memory_space=pltpu.SEMAPHORE)
ANY_SPEC = pl.BlockSpec(memory_space=pl.ANY)
SPLIT_PARAMS = dict(has_side_effects=pltpu.SideEffectType.DATAFLOW_SIDE_EFFECTING)


def _in_hbm(a):
    return pltpu.with_memory_space_constraint(a, pltpu.HBM)


def _split_start(copies_of, bufs, n_sem, after, name):
    n = len(bufs)

    def body(*refs):
        ins, send, recv, token = refs[:n], refs[n + 1], refs[n + 2], refs[2 * n + 3]
        for cp in copies_of(ins, send, recv, False)[0]:
            cp.start()
        token[...] = jnp.zeros_like(token)

    outs = pl.pallas_call(
        body, name=name,
        out_shape=(pltpu.SemaphoreType.DMA(n_sem), pltpu.SemaphoreType.DMA(n_sem),
                   *[pltpu.HBM(b.shape, b.dtype) for b in bufs], jax.ShapeDtypeStruct((8, 128), F32)),
        in_specs=[HBM_SPEC] * n + [ANY_SPEC],
        out_specs=(SEM_SPEC, SEM_SPEC, *[HBM_SPEC] * n, pl.BlockSpec(memory_space=pltpu.VMEM)),
        input_output_aliases={t: 2 + t for t in range(n)},
        compiler_params=pltpu.CompilerParams(**SPLIT_PARAMS),
    )(*[_in_hbm(b) for b in bufs], after)
    return outs[0], outs[1], list(outs[2:2 + n]), outs[2 + n]


def _split_wait(copies_of, send, recv, bufs, after, name):
    n = len(bufs)

    def body(*refs):
        ins, send_ref, recv_ref = refs[:n], refs[n], refs[n + 1]
        sends, arrivals = copies_of(ins, send_ref, recv_ref, True)
        for cp in sends:
            cp.wait_send()
        for cp in arrivals:
            cp.wait_recv()

    return pl.pallas_call(
        body, name=name,
        out_shape=[pltpu.HBM(b.shape, b.dtype) for b in bufs],
        in_specs=[HBM_SPEC] * n + [SEM_SPEC, SEM_SPEC, ANY_SPEC],
        out_specs=[HBM_SPEC] * n,
        input_output_aliases={t: t for t in range(n)},
        compiler_params=pltpu.CompilerParams(**SPLIT_PARAMS),
    )(*bufs, send, recv, after)


def _gather_copies(items):
    def copies_of(refs, send, recv, with_arrivals):
        x, y, c, chips = _place()
        me = _chip_index(x, y)
        sends, arrivals = [], []
        for t, it in enumerate(items):
            for k, chip in enumerate(chips):
                for core in range(2):
                    mine = _block(it.ref(refs), it, me, c)
                    sends.append(pltpu.make_async_remote_copy(
                        src_ref=mine, dst_ref=mine, send_sem=send.at[6 * t + 2 * k + core],
                        recv_sem=recv.at[6 * t + 2 * k + c], device_id=(*chip, core), device_id_type=MESH))
                    if with_arrivals:
                        landed = _block(it.ref(refs), it, _chip_index(*chip), core)
                        arrivals.append(pltpu.make_async_remote_copy(
                            src_ref=landed, dst_ref=landed, send_sem=send.at[6 * t + 2 * k + core],
                            recv_sem=recv.at[6 * t + 2 * k + core], device_id=(*chip, core), device_id_type=MESH))
        return sends, arrivals

    return copies_of


def _owner_copies(items):
    n = len(items)

    def blk(ref, it, j):
        if it.kind == "col":
            ns = it.cols // N_CHIP
            return ref.at[:, pl.ds(it.pos(j) * ns, ns)]
        return ref.at[j]

    def copies_of(refs, send, recv, with_arrivals):
        x, y, c, chips = _place()
        sends, arrivals = [], []
        for t, it in enumerate(items):
            for k, chip in enumerate(chips):
                slot = refs[n + t].at[k]
                sends.append(pltpu.make_async_remote_copy(
                    src_ref=blk(refs[t], it, _chip_index(*chip)), dst_ref=slot, send_sem=send.at[3 * t + k],
                    recv_sem=recv.at[3 * t + k], device_id=(*chip, c), device_id_type=MESH))
                if with_arrivals:
                    arrivals.append(pltpu.make_async_remote_copy(
                        src_ref=slot, dst_ref=slot, send_sem=send.at[3 * t + k], recv_sem=recv.at[3 * t + k],
                        device_id=(*chip, c), device_id_type=MESH))
        return sends, arrivals

    return copies_of


def _owner_slot_shape(it):
    if it.kind == "col":
        return (3, it.rows // 2, it.cols // N_CHIP)
    return (3, it.rows // (2 * N_CHIP), it.cols)


def _pair_view(g, it):
    if it.kind == "col":
        return g.reshape(1, 2, it.rows // 2, it.cols)
    return g.reshape(N_CHIP, 2, it.rows // (2 * N_CHIP), it.cols)


def _pair_copies(n):
    def copies_of(refs, send, recv, with_arrivals):
        x, y, c, _ = _place()
        sends, arrivals = [], []
        for t in range(n):
            land = refs[n + t]
            sends.append(pltpu.make_async_remote_copy(
                src_ref=refs[t].at[:, pl.ds(1 - c, 1)], dst_ref=land, send_sem=send.at[t], recv_sem=recv.at[t],
                device_id=(x, y, 1 - c), device_id_type=MESH))
            if with_arrivals:
                arrivals.append(pltpu.make_async_remote_copy(
                    src_ref=land, dst_ref=land, send_sem=send.at[t], recv_sem=recv.at[t],
                    device_id=(x, y, 1 - c), device_id_type=MESH))
        return sends, arrivals

    return copies_of


def _half_copies(n):
    def half(ref, which):
        r2 = ref.shape[-2] // 2
        rows = pl.ds(which * r2, r2)
        return ref.at[rows, :] if len(ref.shape) == 2 else ref.at[:, rows, :]

    def copies_of(refs, send, recv, with_arrivals):
        x, y, c, _ = _place()
        sends, arrivals = [], []
        for t in range(n):
            mine = half(refs[t], c)
            sends.append(pltpu.make_async_remote_copy(
                src_ref=mine, dst_ref=mine, send_sem=send.at[t], recv_sem=recv.at[t],
                device_id=(x, y, 1 - c), device_id_type=MESH))
            if with_arrivals:
                theirs = half(refs[t], 1 - c)
                arrivals.append(pltpu.make_async_remote_copy(
                    src_ref=theirs, dst_ref=theirs, send_sem=send.at[t], recv_sem=recv.at[t],
                    device_id=(x, y, 1 - c), device_id_type=MESH))
        return sends, arrivals

    return copies_of


class _Reduction:
    pass


def _pair_start(grads, items, after, tag, names, layer=None):
    n = len(items)
    views = [_pair_view(g, it) for g, it in zip(grads, items)]
    lands = [lax.empty((v.shape[0], 1) + v.shape[2:], v.dtype) for v in views]
    r = _Reduction()
    r.items, r.tag, r.names, r.layer = items, tag, names, layer
    r.send, r.recv, r.bufs, r.token = _split_start(_pair_copies(n), views + lands, (n,), after, f"rs_pair_start_{tag}")
    return r


def _owner_start(r, after):
    x, y, c, _ = _place()
    n = len(r.items)
    bufs = _split_wait(_pair_copies(n), r.send, r.recv, r.bufs, after, f"rs_pair_wait_{r.tag}")
    pairs = [_pair_sum(bufs[t], bufs[n + t], c, f"rs_pair_sum_{r.tag}_{t}") for t in range(n)]
    shaped = [p if it.kind == "col" else p.reshape(N_CHIP, p.shape[0] // N_CHIP, p.shape[1])
              for p, it in zip(pairs, r.items)]
    lands = [lax.empty(_owner_slot_shape(it), BF16) for it in r.items]
    r.send, r.recv, r.bufs, r.token = _split_start(
        _owner_copies(r.items), shaped + lands, (3 * n,), r.token, f"rs_owner_start_{r.tag}")
    return r


def _reduce_finish(groups, after):
    x, y, c, _ = _place()
    me = _chip_index(x, y)
    halves = {}
    for r in groups:
        n = len(r.items)
        bufs = _split_wait(_owner_copies(r.items), r.send, r.recv, r.bufs, after, f"rs_owner_wait_{r.tag}")
        for t, (it, nm) in enumerate(zip(r.items, r.names)):
            pair = bufs[t].reshape(-1, bufs[t].shape[-1])
            halves[nm] = _owner_sum(pair, bufs[n + t], me, c, it, f"rs_owner_sum_{r.tag}_{t}",
                                    layer=r.layer, into=halves.get(nm))
    n = len(halves)
    return list(halves), _split_start(_half_copies(n), list(halves.values()), (n,), after, "rs_half_start")


def _silu(v):
    return v * jax.nn.sigmoid(v)


def _sum8(p):
    return jnp.sum(p, axis=-2)


def kernel(x, c, mod_w, mod_b, norm_g, ffn_w_in, ffn_w_out, conv_w_in, conv_k, conv_w_out, kv_mod_w, kv_mod_b, kv_norm_g, w_kv, attn_w_q, attn_w_o, rel_bias, loss_target, m_mod_w, m_mod_b, m_norm_g, m_ffn_w_in, m_ffn_w_out, m_conv_w_in, m_conv_k, m_conv_w_out, m_kv_mod_w, m_kv_mod_b, m_kv_norm_g, m_w_kv, m_attn_w_q, m_attn_w_o, m_rel_bias, v_mod_w, v_mod_b, v_norm_g, v_ffn_w_in, v_ffn_w_out, v_conv_w_in, v_conv_k, v_conv_w_out, v_kv_mod_w, v_kv_mod_b, v_kv_norm_g, v_w_kv, v_attn_w_q, v_attn_w_o, v_rel_bias):
    xi, yi, ci = lax.axis_index("x"), lax.axis_index("y"), lax.axis_index("c")
    chip = 2 * xi + yi
    dev = 2 * chip + ci
    _, S, D = x.shape
    F = ffn_w_out.shape[1] * N_CHIP
    x0 = x.reshape(S, D)
    target = loss_target.reshape(S, D)
    n_mod = mod_w.shape[2]
    n_kvm = kv_mod_w.shape[1]
    dsh = D // N_CHIP
    TF = F // 2

    c_all = _all_gather_small(c.reshape(8, D // 8), "ag_c").reshape(N_DEV, D)
    sc16 = jnp.pad(_silu(c_all), ((0, 8), (0, 0)))
    part = [_mm(sc16, mod_w, "nn", F32, f"mod_fwd_{l}", b_layer=l)[:8] for l in range(2)]
    part.append(_mm(sc16, kv_mod_w, "nn", F32, "mod_fwd_kv")[:8])
    fwd_vec = jnp.concatenate([p.reshape(-1) for p in part] + [norm_g.reshape(-1), conv_k.reshape(-1)])
    fwd_all = _gather_flat(fwd_vec, "ag_fwd_small")[0::2]
    o = 0
    mods = []
    for n in (n_mod, n_mod, n_kvm):
        blk = fwd_all[:, o:o + 8 * n].reshape(N_CHIP, 8, n)
        mods.append(lax.dynamic_index_in_dim(blk, dev, axis=1, keepdims=False).reshape(N_CHIP * n))
        o += 8 * n
    ng = fwd_all[:, o:o + 8 * dsh].reshape(N_CHIP, 2, 4, dsh).transpose(1, 2, 0, 3).reshape(2, 4, D)
    o += 8 * dsh
    ck = fwd_all[:, o:o + 3 * dsh].reshape(N_CHIP, 3, dsh).transpose(1, 0, 2).reshape(3, D)
    ck8 = jnp.pad(ck, ((0, 5), (0, 0)))
    mod = [mods[l] + mod_b[l] for l in range(2)]
    sh1, sc1, g1, sh2, sc2, g2 = zip(*[jnp.split(m, 6) for m in mod])
    kv_sh, kv_sc = jnp.split(mods[2] + kv_mod_b, 2)
    row = lambda v: v.reshape(1, D)

    it_conv = [_Item("col", D, 3 * D, 0, 0), _Item("row", D, D, 1, 0)]
    it_ffn = [_Item("col", D, 2 * F, 0, 0, swap=True), _Item("row", F, D, 1, 0)]
    it_attn = [_Item("col", D, 2 * D, 0, 0, swap=True), _Item("row", D, D, 1, 0), _Item("row", D, D, 2, 0)]

    def placed(w, layer, it, nm, after=fwd_all):
        return _cast_place(w, layer, it.kind, it.pos(chip), after, f"place_{nm}")

    flying = {}

    def start(tag, its, bufs, after):
        send, recv, bufs, tok = _split_start(_gather_copies(its), bufs, (6 * len(its),), after, f"ag_start_{tag}")
        flying[tag] = (its, send, recv, bufs)
        return tok

    def arrived(tag, after):
        its, send, recv, bufs = flying[tag]
        return _split_wait(_gather_copies(its), send, recv, bufs, after, f"ag_wait_{tag}")

    one = lambda it: [_Item(it.kind, it.rows, it.cols, 0, 0, it.swap)]
    tok = start("conv_in", one(it_conv[0]), [placed(conv_w_in, 0, it_conv[0], "conv_w_in")], fwd_all)
    tok = start("conv_out", one(it_conv[1]), [placed(conv_w_out, 0, it_conv[1], "conv_w_out", tok)], tok)
    tok = start("ffn0_in", one(it_ffn[0]), [placed(ffn_w_in, 0, it_ffn[0], "ffn_w_in0", tok)], tok)
    tok = start("ffn0_out", one(it_ffn[1]), [placed(ffn_w_out, 0, it_ffn[1], "ffn_w_out0", tok)], tok)
    tok = start("attn", it_attn, [placed(w_kv[None], 0, it_attn[0], "w_kv", tok),
                                  placed(attn_w_q, 0, it_attn[1], "attn_w_q", tok),
                                  placed(attn_w_o, 0, it_attn[2], "attn_w_o", tok)], tok)
    token = start("ffn1", it_ffn, [placed(ffn_w_in, 1, it_ffn[0], "ffn_w_in1", tok),
                                   placed(ffn_w_out, 1, it_ffn[1], "ffn_w_out1", tok)], tok)

    a1 = row(ng[0, 0] * (1.0 + sc1[0])) + token[0, 0]
    (h1,) = _norm_mod(x0, a1, row(sh1[0]), "l0_norm1")
    tab = _bias_table(rel_bias[0], "l1_bias_table")
    h1, tab = lax.optimization_barrier((h1, tab))
    (W_cin,) = arrived("conv_in", h1)
    bcx = _mm(h1, W_cin, "nn", BF16, "l0_conv_in", b_layer=0, tm=512, tn=3 * D)
    ug = _conv_gate(bcx, ck8, "l0_conv_gate")
    gt1 = row(g1[0] * ng[0, 1])
    a2 = row(ng[0, 2] * (1.0 + sc2[0]))
    (W_cout,) = arrived("conv_out", ug)
    y1, x1, h2 = _mm_post(ug, W_cout, x0, gt1, "l0_conv_out", scales=a2, shifts=row(sh2[0]))
    (W_fin0,) = arrived("ffn0_in", h2)
    gu0, act0 = _ffn_in_act(h2, W_fin0, 0, "l0_ffn_in")
    (W_fout0,) = arrived("ffn0_out", act0)
    gt2 = row(g2[0] * ng[0, 3])
    a3 = ng[1, 0] * (1.0 + sc1[1])
    akv = kv_norm_g * (1.0 + kv_sc)
    y2, x2, h3, hkv = _mm_post(act0, W_fout0, x1, gt2, "l0_ffn_out",
                               scales=jnp.stack([a3, akv]), shifts=jnp.stack([sh1[1], kv_sh]))
    W_kv, W_q, W_o = arrived("attn", hkv)
    kvp = _mm(hkv, W_kv, "nn", BF16, "l1_kv", b_layer=0, tm=512, tn=2 * D)
    att_scale = (D // N_HEADS) ** -0.5
    assert math.log2(att_scale) % 1 == 0, "scaling q before its bf16 cast is exact only for a power of two"
    qp = _mm(h3, W_q, "nn", BF16, "l1_q", b_layer=0, scale=att_scale)
    oh = _attn_fwd(qp, kvp, tab, "l1_attn")
    gt3 = row(g1[1] * ng[1, 1])
    a4 = row(ng[1, 2] * (1.0 + sc2[1]))
    y3, x3, h4 = _mm_post(oh, W_o, x2, gt3, "l1_attn_out", scales=a4, shifts=row(sh2[1]))
    W_fin1, W_fout1 = arrived("ffn1", h4)
    gu1, act1 = _ffn_in_act(h4, W_fin1, 0, "l1_ffn_in")
    gt4 = row(g2[1] * ng[1, 3])
    dx4, sq, dy4, dgt4 = _mm_post(act1, W_fout1, x3, gt4, "l1_ffn_out", target=target)
    loss_part = 0.5 * jnp.sum(sq) / D

    def ffn_bwd(dy, dxn, xin_, h, gu, act, a, w_in, w_out, post, tag):
        dgu, dx, ds, db, dyn, dgt = _ffn_bwd(dy, w_out, gu, w_in, xin_, dxn, a, post, f"{tag}_ffn_bwd")
        g_fout = _mm(act, dy, "tn", BF16, f"{tag}_ffn_out_dw", tm=TF)
        g_fin = _mm(h, dgu, "tn", BF16, f"{tag}_ffn_in_dw", tn=TF)
        return dx, ds, db, dyn, dgt, g_fin, g_fout

    dx3, ds4, db4, dy3, dgt3, G_fin1, G_fout1 = ffn_bwd(dy4, dx4, x3, h4, gu1, act1, a4, W_fin1, W_fout1,
                                                        (y3, gt3), "l1")
    red = [_pair_start([G_fin1, G_fout1], it_ffn, token, "ffn1", ["ffn_w_in", "ffn_w_out"], layer=1)]
    doh = _mm(dy3, W_o, "nt", BF16, "l1_attn_out_dx", b_layer=0, after=red[0].token)
    G_o = _mm(oh, dy3, "tn", BF16, "l1_attn_out_dw")
    _owner_start(red[0], G_o)
    dq, dkv, dtab = _attn_bwd(qp, kvp, tab, doh, "l1_attn_bwd")
    d_rel = _bias_table_grad(dtab)
    G_q = _mm(h3, dq, "tn", BF16, "l1_q_dw")
    G_kv = _mm(hkv, dkv, "tn", BF16, "l1_kv_dw")
    red.append(_pair_start([G_kv, G_q, G_o], it_attn, red[-1].token, "attn", ["w_kv", "attn_w_q", "attn_w_o"]))
    dx2, ds3, db3, dy2, dgt2 = _mm_pre_bwd([(dq, W_q), (dkv, W_kv)], x2, dx3,
                                           jnp.stack([a3, akv]) + red[1].token[0, 0], "l1_qkv_dx", post=(y2, gt2))
    _owner_start(red[1], dx2)

    dx1, ds2, db2, dy1, dgt1, G_fin0, G_fout0 = ffn_bwd(dy2, dx2, x1, h2, gu0, act0, a2, W_fin0, W_fout0,
                                                        (y1, gt1), "l0")
    red.append(_pair_start([G_fin0, G_fout0], it_ffn, red[-1].token, "ffn0", ["ffn_w_in", "ffn_w_out"], layer=0))
    dug = _mm(dy1, W_cout, "nt", BF16, "l0_conv_out_dx", b_layer=0, after=red[2].token)
    G_cout = _mm(ug, dy1, "tn", BF16, "l0_conv_out_dw")
    _owner_start(red[2], G_cout)
    dbcx, dck = _conv_gate_bwd(dug, bcx, ck8, "l0_conv_gate_bwd")
    G_cin = _mm(h1, dbcx, "tn", BF16, "l0_conv_in_dw")
    red.append(_pair_start([G_cin, G_cout], it_conv, red[-1].token, "conv", ["conv_w_in", "conv_w_out"]))
    dx0, ds1, db1 = _mm_pre_bwd([(dbcx, W_cin)], x0, dx1, a1 + red[3].token[0, 0], "l0_conv_in_dx")
    ds1, db1 = _sum8(ds1)[0], _sum8(db1)[0]
    da2, db2 = _sum8(ds2)[0], _sum8(db2)[0]
    ds3, db3 = _sum8(ds3), _sum8(db3)
    da4, db4 = _sum8(ds4)[0], _sum8(db4)[0]
    dgt1, dgt2, dgt3, dgt4 = _sum8(dgt1), _sum8(dgt2), _sum8(dgt3), _sum8(dgt4)

    def dmod_of(l, ds_a, db_a, dgt_a, ds_b, db_b, dgt_b):
        return jnp.concatenate([db_a, ds_a * ng[l, 0], dgt_a * ng[l, 1], db_b, ds_b * ng[l, 2], dgt_b * ng[l, 3]])

    dmod0 = dmod_of(0, ds1, db1, dgt1, da2, db2, dgt2)
    dmod1 = dmod_of(1, ds3[0], db3[0], dgt3, da4, db4, dgt4)
    dkvmod = jnp.concatenate([db3[1], ds3[1] * kv_norm_g])
    dng = jnp.stack([
        jnp.stack([ds1 * (1.0 + sc1[0]), dgt1 * g1[0], da2 * (1.0 + sc2[0]), dgt2 * g2[0]]),
        jnp.stack([ds3[0] * (1.0 + sc1[1]), dgt3 * g1[1], da4 * (1.0 + sc2[1]), dgt4 * g2[1]])])
    dkvng = ds3[1] * (1.0 + kv_sc)
    small = [dmod0, dmod1, dkvmod, dng.reshape(-1), dkvng, _sum8(dck).reshape(-1), d_rel.reshape(-1),
             loss_part.reshape(1)]
    sizes = [int(s.shape[0]) for s in small]
    offs = np.concatenate([[0], np.cumsum(sizes)])
    bwd_all = _gather_flat(jnp.concatenate(small), "ag_bwd_small")
    _owner_start(red[3], bwd_all)
    Lb = bwd_all.shape[1]
    Lp = -(-Lb // 128) * 128
    tot = _sum_rows(jnp.pad(bwd_all, ((0, 0), (0, Lp - Lb))), "sum_small")[0]
    seg = lambda i: tot[offs[i]:offs[i + 1]]
    g_mod_b = jnp.stack([seg(0), seg(1)])
    g_kv_mod_b = seg(2)
    g_norm_g = lax.dynamic_slice_in_dim(seg(3).reshape(2, 4, D), chip * dsh, dsh, axis=2)
    g_kv_norm_g = seg(4)
    g_conv_k = lax.dynamic_slice_in_dim(seg(5).reshape(1, 3, D), chip * dsh, dsh, axis=2)
    g_rel_bias = seg(6).reshape(rel_bias.shape)
    loss = seg(7)[0]

    def dmod_w(i, n, name):
        rows_ = lax.dynamic_slice_in_dim(bwd_all[:, offs[i]:offs[i + 1]], chip * n, n, axis=1)
        return _mm(sc16, jnp.pad(rows_, ((0, 8), (0, 0))), "tn", F32, name)

    g_mod_w = jnp.stack([dmod_w(0, n_mod, "mod_bwd_0"), dmod_w(1, n_mod, "mod_bwd_1")])
    g_kv_mod_w = dmod_w(2, n_kvm, "mod_bwd_kv")

    grads = {
        "mod_w": g_mod_w, "mod_b": g_mod_b, "norm_g": g_norm_g, "conv_k": g_conv_k,
        "kv_mod_w": g_kv_mod_w, "kv_mod_b": g_kv_mod_b, "kv_norm_g": g_kv_norm_g, "rel_bias": g_rel_bias,
    }
    weights = dict(mod_w=mod_w, mod_b=mod_b, norm_g=norm_g, ffn_w_in=ffn_w_in, ffn_w_out=ffn_w_out,
                   conv_w_in=conv_w_in, conv_k=conv_k, conv_w_out=conv_w_out, kv_mod_w=kv_mod_w,
                   kv_mod_b=kv_mod_b, kv_norm_g=kv_norm_g, w_kv=w_kv, attn_w_q=attn_w_q, attn_w_o=attn_w_o,
                   rel_bias=rel_bias)
    m_in = dict(mod_w=m_mod_w, mod_b=m_mod_b, norm_g=m_norm_g, ffn_w_in=m_ffn_w_in, ffn_w_out=m_ffn_w_out,
                conv_w_in=m_conv_w_in, conv_k=m_conv_k, conv_w_out=m_conv_w_out, kv_mod_w=m_kv_mod_w,
                kv_mod_b=m_kv_mod_b, kv_norm_g=m_kv_norm_g, w_kv=m_w_kv, attn_w_q=m_attn_w_q,
                attn_w_o=m_attn_w_o, rel_bias=m_rel_bias)
    v_in = dict(mod_w=v_mod_w, mod_b=v_mod_b, norm_g=v_norm_g, ffn_w_in=v_ffn_w_in, ffn_w_out=v_ffn_w_out,
                conv_w_in=v_conv_w_in, conv_k=v_conv_k, conv_w_out=v_conv_w_out, kv_mod_w=v_kv_mod_w,
                kv_mod_b=v_kv_mod_b, kv_norm_g=v_kv_norm_g, w_kv=v_w_kv, attn_w_q=v_attn_w_q,
                attn_w_o=v_attn_w_o, rel_bias=v_rel_bias)
    names = list(weights)
    step = {}

    def update(n):
        g = grads[n].reshape(weights[n].shape)
        step[n] = (g, *_adamw(weights[n], g, m_in[n], v_in[n], f"adamw_{n}"))

    update("mod_w")
    reduced, (half_send, half_recv, half_bufs, _) = _reduce_finish(red, step["mod_w"][1])
    for n in list(grads):
        if n not in step:
            update(n)
    grads.update(zip(reduced, _split_wait(
        _half_copies(len(half_bufs)), half_send, half_recv, half_bufs, step["kv_mod_w"][1], "rs_half_wait")))
    for n in names:
        if n not in step:
            update(n)
    return (loss, dx0.reshape(x.shape), *[step[n][k] for k in range(4) for n in names])
```

```python
import functools
import math

import numpy as np
import jax
import jax.numpy as jnp
from jax import lax
from jax.experimental import pallas as pl
from jax.experimental.pallas import tpu as pltpu

CHUNK = 64
N_LEFT_CHUNKS = 8
N_HEADS = 16
MAX_REL = 2 * CHUNK
N_REL = 2 * MAX_REL + 1
EPS = 1e-6
ADAM_LR = 0.001
ADAM_B1 = 0.9
ADAM_B2 = 0.999
ADAM_EPS = 1e-08
ADAM_WD = 0.01
ADAM_STEP = 10

Q_CHUNKS = 4
BQ = Q_CHUNKS * CHUNK
N_WIN = 1 + N_LEFT_CHUNKS // Q_CHUNKS
HEADS_PER_STEP = 8
NEG = -1e30
N_DEV = 8
N_CHIP = 4
SMALL_TENSOR_ELEMS = 1 << 16

BF16 = jnp.bfloat16
F32 = jnp.float32
V7X_VMEM_LIMIT_BYTES = 56 * 1024 * 1024
MESH = pl.DeviceIdType.MESH


def _pick(n, pref, align):
    t = min(pref, n)
    t -= t % align
    while t >= align:
        if n % t == 0:
            return t
        t -= align
    return n


def _params(*sem):
    return pltpu.CompilerParams(dimension_semantics=sem, vmem_limit_bytes=V7X_VMEM_LIMIT_BYTES)


def _colsum8(v):
    r, d = v.shape
    return v.reshape(r // 8, 8, d).sum(axis=0)


_DIMS = {"nn": (((1,), (0,)), ((), ())), "nt": (((1,), (1,)), ((), ())), "tn": (((0,), (0,)), ((), ()))}


def _mm(a, b, mode, out_dtype, name, *, b_layer=None, tm=1024, tn=1024, tk=None, scale=None, after=None):
    if tk is None:
        tk = 2048 if mode == "tn" else 3072
    bs = b.shape[1:] if b_layer is not None else b.shape
    if mode == "nn":
        (M, K), (K2, N) = a.shape, bs
    elif mode == "nt":
        (M, K), (N, K2) = a.shape, bs
    else:
        (K, M), (K2, N) = a.shape, bs
    assert K == K2, (name, a.shape, b.shape)
    tm = _pick(M, tm, 128 if mode == "tn" else 16)
    tn = _pick(N, tn, 128)
    tk = _pick(K, tk, 128 if mode != "tn" else 16)
    nk = K // tk
    assert scale is None or nk == 1, name
    dims = _DIMS[mode]
    extra = [] if after is None else [after]

    def body(a_ref, b_ref, *rest):
        o_ref, acc = rest[len(extra)], rest[len(extra) + 1:]
        p = lax.dot_general(a_ref[...].astype(BF16), b_ref[...].astype(BF16), dims,
                            preferred_element_type=F32)
        if nk == 1:
            o_ref[...] = (p if scale is None else p * scale).astype(o_ref.dtype)
        else:
            k = pl.program_id(2)

            @pl.when(k == 0)
            def _():
                acc[0][...] = p

            @pl.when(k > 0)
            def _():
                acc[0][...] += p

            @pl.when(k == nk - 1)
            def _():
                o_ref[...] = acc[0][...].astype(o_ref.dtype)

    a_spec = (pl.BlockSpec((tk, tm), lambda i, j, k: (k, i)) if mode == "tn"
              else pl.BlockSpec((tm, tk), lambda i, j, k: (i, k)))
    if mode == "nt":
        b_blk, b_idx = (tn, tk), (lambda i, j, k: (j, k))
    else:
        b_blk, b_idx = (tk, tn), (lambda i, j, k: (k, j))
    if b_layer is not None:
        b_spec = pl.BlockSpec((None,) + b_blk, lambda i, j, k: (b_layer,) + b_idx(i, j, k))
    else:
        b_spec = pl.BlockSpec(b_blk, b_idx)
    return pl.pallas_call(
        body, name=name,
        grid=(M // tm, N // tn, nk),
        in_specs=[a_spec, b_spec] + [pl.BlockSpec(memory_space=pl.ANY)] * len(extra),
        out_specs=pl.BlockSpec((tm, tn), lambda i, j, k: (i, j)),
        out_shape=jax.ShapeDtypeStruct((M, N), out_dtype),
        scratch_shapes=[pltpu.VMEM((tm, tn), F32)] if nk > 1 else [],
        compiler_params=_params("parallel", "parallel", "arbitrary"),
    )(a, b, *extra)


def _row_spec(tm, d):
    return pl.BlockSpec((tm, d), lambda i: (i, 0))


def _vec_spec(r, d):
    return pl.BlockSpec((r, d), lambda i: (0, 0))


def _norm_mod(x, scales, shifts, name):
    S, D = x.shape
    nb = scales.shape[0]
    tm = _pick(S, 1024, 16)

    def body(x_ref, a_ref, b_ref, *o_refs):
        xv = x_ref[...]
        xh = xv * lax.rsqrt(jnp.mean(xv * xv, axis=-1, keepdims=True) + EPS)
        for n in range(nb):
            o_refs[n][...] = (xh * a_ref[n:n + 1, :] + b_ref[n:n + 1, :]).astype(BF16)

    return pl.pallas_call(
        body, name=name, grid=(S // tm,),
        in_specs=[_row_spec(tm, D), _vec_spec(nb, D), _vec_spec(nb, D)],
        out_specs=[_row_spec(tm, D)] * nb,
        out_shape=[jax.ShapeDtypeStruct((S, D), BF16)] * nb,
        compiler_params=_params("parallel"),
    )(x, scales, shifts)


def _mm_post(a, w, x, gate, name, *, scales=None, shifts=None, target=None, sub=256):
    M, K = a.shape
    D = w.shape[2]
    tm = _pick(M, 1024 if K <= D else 512, 16)
    sub = _pick(tm, sub, 16)
    nb = 0 if scales is None else scales.shape[0]

    def body(a_ref, w_ref, x_ref, g_ref, *rest):
        if target is None:
            sc_ref, sh_ref, y_ref, xn_ref = rest[:4]
            h_refs = rest[4:]
        else:
            t_ref, dx_ref, sq_ref, dy_ref, dg_ref = rest

            @pl.when(pl.program_id(0) == 0)
            def _():
                sq_ref[...] = jnp.zeros_like(sq_ref)
                dg_ref[...] = jnp.zeros_like(dg_ref)

        def product(r):
            return jnp.dot(a_ref[pl.ds(r * sub, sub), :], w_ref[...], preferred_element_type=F32)

        y = product(0)
        for r in range(tm // sub):
            rows = pl.ds(r * sub, sub)
            yb = y.astype(BF16)
            if r + 1 < tm // sub:
                y = product(r + 1)
            yv = yb.astype(F32)
            yh = yv * lax.rsqrt(jnp.mean(yv * yv, axis=-1, keepdims=True) + EPS)
            xn = x_ref[rows, :] + yh * g_ref[...]
            if target is None:
                y_ref[rows, :] = yb
                xn_ref[rows, :] = xn
                xh = xn * lax.rsqrt(jnp.mean(xn * xn, axis=-1, keepdims=True) + EPS)
                for n in range(nb):
                    h_refs[n][rows, :] = (xh * sc_ref[n:n + 1, :] + sh_ref[n:n + 1, :]).astype(BF16)
            else:
                e = xn - t_ref[rows, :]
                dx = e / D
                dx_ref[rows, :] = dx
                sq_ref[...] += _colsum8(e * e)
                dy, dxy = _post_norm_grad(dx, yb, g_ref[...])
                dy_ref[rows, :] = dy.astype(BF16)
                dg_ref[...] += _colsum8(dxy)

    ins = [a, w, x, gate]
    in_specs = [_row_spec(tm, K), pl.BlockSpec((None, K, D), lambda i: (0, 0, 0)), _row_spec(tm, D), _vec_spec(1, D)]
    if target is None:
        ins += [scales, shifts]
        in_specs += [_vec_spec(nb, D), _vec_spec(nb, D)]
        out_specs = [_row_spec(tm, D)] * (2 + nb)
        out_shape = [jax.ShapeDtypeStruct((M, D), BF16), jax.ShapeDtypeStruct((M, D), F32)] \
            + [jax.ShapeDtypeStruct((M, D), BF16)] * nb
    else:
        ins += [target]
        in_specs += [_row_spec(tm, D)]
        out_specs = [_row_spec(tm, D), _vec_spec(8, D), _row_spec(tm, D), _vec_spec(8, D)]
        out_shape = [jax.ShapeDtypeStruct((M, D), F32), jax.ShapeDtypeStruct((8, D), F32),
                     jax.ShapeDtypeStruct((M, D), BF16), jax.ShapeDtypeStruct((8, D), F32)]
    return pl.pallas_call(
        body, name=name, grid=(M // tm,), in_specs=in_specs, out_specs=out_specs, out_shape=out_shape,
        compiler_params=_params("arbitrary" if target is not None else "parallel"),
    )(*ins)


def _post_norm_grad(dxn, yb, gate):
    yv = yb.astype(F32)
    r = lax.rsqrt(jnp.mean(yv * yv, axis=-1, keepdims=True) + EPS)
    yh = yv * r
    dyh = dxn * gate
    return r * (dyh - yh * jnp.mean(dyh * yh, axis=-1, keepdims=True)), dxn * yh


def _mm_pre_bwd(pairs, x, dxn, scales, name, post=None, sub=256):
    S, D = x.shape
    nb = len(pairs)
    tm = _pick(S, 512, 16)
    sub = _pick(tm, sub, 16)

    def body(*refs):
        a_refs, w_refs = refs[0:2 * nb:2], refs[1:2 * nb:2]
        x_ref, d_ref, sc_ref = refs[2 * nb:2 * nb + 3]
        rest = refs[2 * nb + 3:]
        if post is not None:
            y_ref, g_ref, dx_ref, ds_ref, db_ref, dy_ref, dg_ref = rest
        else:
            dx_ref, ds_ref, db_ref = rest

        @pl.when(pl.program_id(0) == 0)
        def _():
            ds_ref[...] = jnp.zeros_like(ds_ref)
            db_ref[...] = jnp.zeros_like(db_ref)
            if post is not None:
                dg_ref[...] = jnp.zeros_like(dg_ref)

        def products(r):
            return [lax.dot_general(a_refs[n][pl.ds(r * sub, sub), :], w_refs[n][...], _DIMS["nt"],
                                    preferred_element_type=F32) for n in range(nb)]

        nxt = products(0)
        for r in range(tm // sub):
            rows = pl.ds(r * sub, sub)
            dhs = nxt
            if r + 1 < tm // sub:
                nxt = products(r + 1)
            xv = x_ref[rows, :]
            rr = lax.rsqrt(jnp.mean(xv * xv, axis=-1, keepdims=True) + EPS)
            xh = xv * rr
            dxh = jnp.zeros_like(xv)
            for n in range(nb):
                dh = dhs[n]
                dxh = dxh + dh * sc_ref[n:n + 1, :]
                ds_ref[n] += _colsum8(dh * xh)
                db_ref[n] += _colsum8(dh)
            dx = d_ref[rows, :] + rr * (dxh - xh * jnp.mean(dxh * xh, axis=-1, keepdims=True))
            dx_ref[rows, :] = dx
            if post is not None:
                dy, dxy = _post_norm_grad(dx, y_ref[rows, :], g_ref[...])
                dy_ref[rows, :] = dy.astype(BF16)
                dg_ref[...] += _colsum8(dxy)

    ins, in_specs = [], []
    for a, w in pairs:
        ins += [a, w]
        in_specs += [_row_spec(tm, a.shape[1]),
                     pl.BlockSpec((None, D, a.shape[1]), lambda i: (0, 0, 0), pipeline_mode=pl.Buffered(1))]
    ins += [x, dxn, scales]
    in_specs += [_row_spec(tm, D), _row_spec(tm, D), _vec_spec(nb, D)]
    acc_spec = pl.BlockSpec((nb, 8, D), lambda i: (0, 0, 0))
    out_specs = [_row_spec(tm, D), acc_spec, acc_spec]
    out_shape = [jax.ShapeDtypeStruct((S, D), F32), jax.ShapeDtypeStruct((nb, 8, D), F32),
                 jax.ShapeDtypeStruct((nb, 8, D), F32)]
    if post is not None:
        ins += list(post)
        in_specs += [_row_spec(tm, D), _vec_spec(1, D)]
        out_specs += [_row_spec(tm, D), _vec_spec(8, D)]
        out_shape += [jax.ShapeDtypeStruct((S, D), BF16), jax.ShapeDtypeStruct((8, D), F32)]
    return pl.pallas_call(
        body, name=name, grid=(S // tm,), in_specs=in_specs, out_specs=out_specs, out_shape=out_shape,
        compiler_params=_params("arbitrary"),
    )(*ins)


FFN_PAIRS = 2
FFN_SUB_ROWS = 256


def _ffn_in_act(h, w, layer, name, tm=1024, sub=FFN_SUB_ROWS):
    S, D = h.shape
    F2 = w.shape[2]
    PW = F2 // (2 * FFN_PAIRS)
    tm = _pick(S, tm, 16)
    sub = _pick(tm, sub, 16)

    def body(h_ref, w_ref, gu_ref, a_ref):
        def product(r):
            return jnp.dot(h_ref[pl.ds(r * sub, sub), :], w_ref[...], preferred_element_type=F32)

        nxt = product(0)
        for r in range(tm // sub):
            rows = pl.ds(r * sub, sub)
            acc = nxt
            if r + 1 < tm // sub:
                nxt = product(r + 1)
            gu_ref[rows, :] = acc.astype(BF16)
            g = acc[:, :PW]
            a_ref[rows, :] = (g * jax.nn.sigmoid(g) * acc[:, PW:]).astype(BF16)

    return pl.pallas_call(
        body, name=name, grid=(FFN_PAIRS, S // tm),
        in_specs=[pl.BlockSpec((tm, D), lambda p, i: (i, 0)),
                  pl.BlockSpec((None, D, 2 * PW), lambda p, i: (layer, 0, p))],
        out_specs=[pl.BlockSpec((tm, 2 * PW), lambda p, i: (i, p)), pl.BlockSpec((tm, PW), lambda p, i: (i, p))],
        out_shape=[jax.ShapeDtypeStruct((S, F2), BF16), jax.ShapeDtypeStruct((S, F2 // 2), BF16)],
        compiler_params=_params("parallel", "parallel"),
    )(h, w)


def _ffn_bwd(dy, w_out, gu, w_in, x, dxn, scale, post, name):
    S, D = dy.shape
    F2 = gu.shape[1]
    PW = F2 // (2 * FFN_PAIRS)
    tm = _pick(S, 256, 16)

    def body(dy_ref, wo_ref, gu_ref, wi_ref, x_ref, d_ref, sc_ref, y_ref, g_ref,
             dgu_ref, dx_ref, ds_ref, db_ref, dyn_ref, dg_ref):
        @pl.when(pl.program_id(0) == 0)
        def _():
            ds_ref[...] = jnp.zeros_like(ds_ref)
            db_ref[...] = jnp.zeros_like(db_ref)
            dg_ref[...] = jnp.zeros_like(dg_ref)

        def first_product(p):
            return lax.dot_general(dy_ref[...], wo_ref[p * PW:(p + 1) * PW, :], _DIMS["nt"],
                                   preferred_element_type=F32)

        dh = jnp.zeros((tm, D), F32)
        nxt = first_product(0)
        for p in range(FFN_PAIRS):
            cols = slice(2 * p * PW, 2 * (p + 1) * PW)
            da = nxt
            if p + 1 < FFN_PAIRS:
                nxt = first_product(p + 1)
            g = gu_ref[:, 2 * p * PW:(2 * p + 1) * PW].astype(F32)
            u = gu_ref[:, (2 * p + 1) * PW:2 * (p + 1) * PW].astype(F32)
            sg = jax.nn.sigmoid(g)
            dgu_ref[:, 2 * p * PW:(2 * p + 1) * PW] = (da * u * (sg * (1.0 + g * (1.0 - sg)))).astype(BF16)
            dgu_ref[:, (2 * p + 1) * PW:2 * (p + 1) * PW] = (da * (g * sg)).astype(BF16)
            dh = dh + lax.dot_general(dgu_ref[:, cols], wi_ref[:, cols], _DIMS["nt"], preferred_element_type=F32)
        xv = x_ref[...]
        rr = lax.rsqrt(jnp.mean(xv * xv, axis=-1, keepdims=True) + EPS)
        xh = xv * rr
        dxh = dh * sc_ref[...]
        ds_ref[0] += _colsum8(dh * xh)
        db_ref[0] += _colsum8(dh)
        dx = d_ref[...] + rr * (dxh - xh * jnp.mean(dxh * xh, axis=-1, keepdims=True))
        dx_ref[...] = dx
        dyn, dxy = _post_norm_grad(dx, y_ref[...], g_ref[...])
        dyn_ref[...] = dyn.astype(BF16)
        dg_ref[...] += _colsum8(dxy)

    resident = dict(pipeline_mode=pl.Buffered(1))
    acc_spec = pl.BlockSpec((1, 8, D), lambda i: (0, 0, 0))
    return pl.pallas_call(
        body, name=name, grid=(S // tm,),
        in_specs=[_row_spec(tm, D), pl.BlockSpec((None, F2 // 2, D), lambda i: (0, 0, 0), **resident),
                  _row_spec(tm, F2), pl.BlockSpec((None, D, F2), lambda i: (0, 0, 0), **resident),
                  _row_spec(tm, D), _row_spec(tm, D), _vec_spec(1, D), _row_spec(tm, D), _vec_spec(1, D)],
        out_specs=[_row_spec(tm, F2), _row_spec(tm, D), acc_spec, acc_spec, _row_spec(tm, D), _vec_spec(8, D)],
        out_shape=[jax.ShapeDtypeStruct((S, F2), BF16), jax.ShapeDtypeStruct((S, D), F32),
                   jax.ShapeDtypeStruct((1, 8, D), F32), jax.ShapeDtypeStruct((1, 8, D), F32),
                   jax.ShapeDtypeStruct((S, D), BF16), jax.ShapeDtypeStruct((8, D), F32)],
        compiler_params=_params("arbitrary"),
    )(dy, w_out, gu, w_in, x, dxn, scale, *post)


HALO = 16


def _conv_terms(bcx_ref, prev_ref, i, tm, D):
    b = bcx_ref[:, 0:D].astype(F32)
    cg = bcx_ref[:, D:2 * D].astype(F32)
    xin = bcx_ref[:, 2 * D:3 * D].astype(F32)
    z = cg * xin
    zp = prev_ref[:, D:2 * D].astype(F32) * prev_ref[:, 2 * D:3 * D].astype(F32)
    zp = jnp.where(i > 0, zp, 0.0)
    z_ext = jnp.concatenate([zp, z], axis=0)
    z1 = pltpu.roll(z_ext, 1, 0)[HALO:, :]
    z2 = pltpu.roll(z_ext, 2, 0)[HALO:, :]
    return b, cg, xin, z, z1, z2


def _conv_gate(bcx, ck, name):
    S, D3 = bcx.shape
    D = D3 // 3
    tm = _pick(S, 512, 16)
    hb = tm // HALO

    def body(bcx_ref, prev_ref, ck_ref, o_ref):
        i = pl.program_id(0)
        b, _, _, z, z1, z2 = _conv_terms(bcx_ref, prev_ref, i, tm, D)
        conv = ck_ref[0:1, :] * z2 + ck_ref[1:2, :] * z1 + ck_ref[2:3, :] * z
        o_ref[...] = (b * conv).astype(BF16)

    return pl.pallas_call(
        body, name=name, grid=(S // tm,),
        in_specs=[_row_spec(tm, D3),
                  pl.BlockSpec((HALO, D3), lambda i: (jnp.maximum(i * hb - 1, 0), 0)),
                  _vec_spec(8, D)],
        out_specs=_row_spec(tm, D),
        out_shape=jax.ShapeDtypeStruct((S, D), BF16),
        compiler_params=_params("parallel"),
    )(bcx, bcx, ck)


def _conv_gate_bwd(du, bcx, ck, name):
    S, D3 = bcx.shape
    D = D3 // 3
    tm = _pick(S, 512, 16)
    hb = tm // HALO
    nt = S // tm

    def body(du_ref, dun_ref, bcx_ref, prev_ref, next_ref, ck_ref, o_ref, dk_ref):
        i = pl.program_id(0)
        b, cg, xin, z, z1, z2 = _conv_terms(bcx_ref, prev_ref, i, tm, D)
        k0, k1, k2 = ck_ref[0:1, :], ck_ref[1:2, :], ck_ref[2:3, :]
        conv = k0 * z2 + k1 * z1 + k2 * z
        d = du_ref[...].astype(F32)
        dconv = d * b
        dcn = jnp.where(i < nt - 1, dun_ref[...].astype(F32) * next_ref[:, 0:D].astype(F32), 0.0)
        d_ext = jnp.concatenate([dconv, dcn], axis=0)
        d1 = pltpu.roll(d_ext, tm + HALO - 1, 0)[:tm, :]
        d2 = pltpu.roll(d_ext, tm + HALO - 2, 0)[:tm, :]
        dz = k2 * dconv + k1 * d1 + k0 * d2
        o_ref[:, 0:D] = (d * conv).astype(BF16)
        o_ref[:, D:2 * D] = (dz * xin).astype(BF16)
        o_ref[:, 2 * D:3 * D] = (dz * cg).astype(BF16)

        @pl.when(i == 0)
        def _():
            dk_ref[...] = jnp.zeros_like(dk_ref)

        dk_ref[0] += _colsum8(dconv * z2)
        dk_ref[1] += _colsum8(dconv * z1)
        dk_ref[2] += _colsum8(dconv * z)

    last = S // HALO - 1
    return pl.pallas_call(
        body, name=name, grid=(nt,),
        in_specs=[_row_spec(tm, D),
                  pl.BlockSpec((HALO, D), lambda i: (jnp.minimum((i + 1) * hb, last), 0)),
                  _row_spec(tm, D3),
                  pl.BlockSpec((HALO, D3), lambda i: (jnp.maximum(i * hb - 1, 0), 0)),
                  pl.BlockSpec((HALO, D3), lambda i: (jnp.minimum((i + 1) * hb, last), 0)),
                  _vec_spec(8, D)],
        out_specs=[_row_spec(tm, D3), pl.BlockSpec((3, 8, D), lambda i: (0, 0, 0))],
        out_shape=[jax.ShapeDtypeStruct((S, D3), BF16), jax.ShapeDtypeStruct((3, 8, D), F32)],
        compiler_params=_params("arbitrary"),
    )(du, du, bcx, bcx, bcx, ck)


def _rel_onehot():
    a = np.arange(CHUNK)[:, None]
    b = np.arange(CHUNK)[None, :]
    idx = np.stack([np.clip((N_LEFT_CHUNKS - dl) * CHUNK + a - b, -MAX_REL, MAX_REL) + MAX_REL
                    for dl in (6, 7, 8)]).reshape(-1)
    return (jnp.asarray(idx)[:, None] == jnp.arange(N_REL)[None, :]).astype(F32)


def _bias_table(rel_bias, name):
    H = rel_bias.shape[0]
    near = jnp.dot(rel_bias, _rel_onehot().T, precision=lax.Precision.HIGHEST).reshape(H, 3, CHUNK, CHUNK)
    far = jnp.broadcast_to(rel_bias[:, N_REL - 1][:, None, None], (H, CHUNK, CHUNK))

    def body(near_ref, far_ref, o_ref):
        neg = jnp.full((CHUNK, CHUNK), NEG, F32)
        for v in range(N_WIN):
            for ic in range(Q_CHUNKS):
                for jc in range(N_WIN * Q_CHUNKS):
                    dl = jc - ic
                    if dl < 0 or dl > N_LEFT_CHUNKS or jc < (N_WIN - 1 - v) * Q_CHUNKS:
                        blk = neg
                    else:
                        blk = far_ref[...] if dl <= 5 else near_ref[dl - 6]
                    o_ref[v, ic * CHUNK:(ic + 1) * CHUNK, jc * CHUNK:(jc + 1) * CHUNK] = blk

    return pl.pallas_call(
        body, name=name, grid=(H,),
        in_specs=[pl.BlockSpec((None, 3, CHUNK, CHUNK), lambda h: (h, 0, 0, 0)),
                  pl.BlockSpec((None, CHUNK, CHUNK), lambda h: (h, 0, 0))],
        out_specs=pl.BlockSpec((N_WIN, None, BQ, N_WIN * BQ), lambda h: (0, h, 0, 0)),
        out_shape=jax.ShapeDtypeStruct((N_WIN, H, BQ, N_WIN * BQ), F32),
        compiler_params=_params("parallel"),
    )(near, far)


NEAR_FIRST = 6
SLAB_ROWS = 2 * CHUNK
SLAB_COLS = 4 * CHUNK


def _slab(pair):
    c0 = (NEAR_FIRST + 2 * pair) * CHUNK
    return slice(pair * SLAB_ROWS, (pair + 1) * SLAB_ROWS), slice(c0, c0 + SLAB_COLS)


def _bias_table_grad(dslab):
    H = dslab.shape[0]

    def blk(ic, dl):
        pair, r, col = ic // 2, ic % 2, ic + dl - NEAR_FIRST - 2 * (ic // 2)
        return dslab[:, pair, r * CHUNK:(r + 1) * CHUNK, col * CHUNK:(col + 1) * CHUNK]

    by_dl = [sum(blk(ic, dl) for ic in range(Q_CHUNKS)) for dl in (6, 7, 8)]
    near = jnp.stack(by_dl, axis=1).reshape(H, 3 * CHUNK * CHUNK)
    g = jnp.dot(near, _rel_onehot(), precision=lax.Precision.HIGHEST)
    return g.at[:, N_REL - 1].add(-jnp.sum(near, axis=1))


def _attn_specs(nblk, W):
    last = nblk - 1
    q_spec = pl.BlockSpec((BQ, W), lambda g, i: (jnp.minimum(i, last), g))
    kv_specs = [pl.BlockSpec((BQ, 2 * W), functools.partial(
        lambda g, i, w: (jnp.maximum(jnp.minimum(i, last) - (N_WIN - 1) + w, 0), g), w=w)) for w in range(N_WIN)]
    tab_spec = pl.BlockSpec((None, HEADS_PER_STEP, BQ, N_WIN * BQ),
                            lambda g, i: (jnp.minimum(i, N_WIN - 1), g, 0, 0))
    dtab_spec = pl.BlockSpec((HEADS_PER_STEP, Q_CHUNKS // 2, SLAB_ROWS, SLAB_COLS), lambda g, i: (g, 0, 0, 0))
    return q_spec, kv_specs, tab_spec, dtab_spec


def _attn_scores(q_ref, kT, tab_ref, h, dh):
    return jnp.dot(q_ref[:, h * dh:(h + 1) * dh], kT[h * dh:(h + 1) * dh, :], preferred_element_type=F32) + tab_ref[h]


def _attn_fwd(q, kv, tab, name):
    S, D = q.shape
    dh = D // N_HEADS
    W = HEADS_PER_STEP * dh
    assert 2 * W == D, "the kv layout puts one head group's k beside its v: two head groups"
    q_spec, kv_specs, tab_spec, _ = _attn_specs(S // BQ, W)

    def body(q_ref, *rest):
        tab_ref, o_ref = rest[N_WIN], rest[N_WIN + 1]
        kvw = jnp.concatenate([r[...] for r in rest[:N_WIN]], axis=0)
        kT = kvw[:, :W].T
        vw = kvw[:, W:]
        outs = []
        s = _attn_scores(q_ref, kT, tab_ref, 0, dh)
        for h in range(HEADS_PER_STEP):
            s_next = _attn_scores(q_ref, kT, tab_ref, h + 1, dh) if h + 1 < HEADS_PER_STEP else None
            e = jnp.exp(s - jnp.max(s, axis=-1, keepdims=True))
            l = jnp.sum(e, axis=-1, keepdims=True)
            outs.append(jnp.dot(e.astype(BF16), vw[:, h * dh:(h + 1) * dh], preferred_element_type=F32) / l)
            s = s_next
        o_ref[...] = jnp.concatenate(outs, axis=1).astype(BF16)

    return pl.pallas_call(
        body, name=name, grid=(N_HEADS // HEADS_PER_STEP, S // BQ),
        in_specs=[q_spec] + kv_specs + [tab_spec],
        out_specs=q_spec,
        out_shape=jax.ShapeDtypeStruct((S, D), BF16),
        compiler_params=_params("parallel", "parallel"),
    )(q, *([kv] * N_WIN), tab)


def _attn_bwd(q, kv, tab, do, name):
    S, D = q.shape
    dh = D // N_HEADS
    W = HEADS_PER_STEP * dh
    nblk = S // BQ
    q_spec, kv_specs, tab_spec, dtab_spec = _attn_specs(nblk, W)

    def body(q_ref, *rest):
        tab_ref, do_ref, dq_ref, dkv_ref, dtab_ref, ring = rest[N_WIN:]
        i = pl.program_id(1)

        @pl.when(i == 0)
        def _():
            dtab_ref[...] = jnp.zeros_like(dtab_ref)
            ring[...] = jnp.zeros_like(ring)

        @pl.when(i < nblk)
        def _():
            kvw = jnp.concatenate([r[...] for r in rest[:N_WIN]], axis=0)
            kT = kvw[:, :W].T
            vT = kvw[:, W:].T
            qT = q_ref[...].T
            dqs, dks, dvs = [], [], []

            s = _attn_scores(q_ref, kT, tab_ref, 0, dh)
            for h in range(HEADS_PER_STEP):
                hd = slice(h * dh, (h + 1) * dh)
                do_h = do_ref[:, hd]
                dp = jnp.dot(do_h, vT[hd, :], preferred_element_type=F32)
                e = jnp.exp(s - jnp.max(s, axis=-1, keepdims=True))
                inv_l = 1.0 / jnp.sum(e, axis=-1, keepdims=True)
                if h + 1 < HEADS_PER_STEP:
                    s = _attn_scores(q_ref, kT, tab_ref, h + 1, dh)
                delta = jnp.sum(e * dp, axis=-1, keepdims=True) * inv_l
                ds = e * ((dp - delta) * inv_l)
                for pair in range(Q_CHUNKS // 2):
                    rows, cols = _slab(pair)
                    dtab_ref[h, pair] += ds[rows, cols]
                dsb = ds.astype(BF16)
                dqs.append(lax.dot_general(kT[hd, :], dsb, _DIMS["nt"], preferred_element_type=F32) * (dh ** -0.5))
                dks.append(jnp.dot(qT[hd, :], dsb, preferred_element_type=F32))
                do_s = (do_h.astype(F32) * inv_l).astype(BF16)
                dvs.append(jnp.dot(do_s.T, e.astype(BF16), preferred_element_type=F32))
            dq_ref[...] = jnp.concatenate(dqs, axis=0).T.astype(BF16)
            dkv = jnp.concatenate(dks + dvs, axis=0).T
            for w in range(N_WIN):
                slot = lax.rem(i + 1 + w, N_WIN)
                part = dkv[w * BQ:(w + 1) * BQ, :]
                if w == N_WIN - 1:
                    ring[slot] = part
                else:
                    ring[slot] += part

        dkv_ref[...] = ring[lax.rem(i + 1, N_WIN)].astype(BF16)

    done_spec = pl.BlockSpec((BQ, 2 * W), lambda g, i: (jnp.maximum(i - (N_WIN - 1), 0), g))
    return pl.pallas_call(
        body, name=name, grid=(N_HEADS // HEADS_PER_STEP, nblk + N_WIN - 1),
        in_specs=[q_spec] + kv_specs + [tab_spec, q_spec],
        out_specs=[q_spec, done_spec, dtab_spec],
        out_shape=[jax.ShapeDtypeStruct((S, D), BF16), jax.ShapeDtypeStruct((S, 2 * D), BF16),
                   jax.ShapeDtypeStruct((N_HEADS, Q_CHUNKS // 2, SLAB_ROWS, SLAB_COLS), F32)],
        scratch_shapes=[pltpu.VMEM((N_WIN, BQ, 2 * W), F32)],
        compiler_params=_params("parallel", "arbitrary"),
    )(q, *([kv] * N_WIN), tab, do)


def _adamw(w, g, m, v, name):
    shape = w.shape
    C = shape[-1]
    R = int(np.prod(shape[:-1])) if len(shape) > 1 else 1
    whole = len(shape) >= 2 and R * C <= SMALL_TENSOR_ELEMS
    if whole:
        w2, g2, m2, v2 = w, g, m, v
    else:
        w2, g2, m2, v2 = (t.reshape(R, C) for t in (w, g, m, v))
    tr = _pick(R, max(8, (512 * 1024) // C // 8 * 8), 8)

    def body(w_ref, g_ref, m_ref, v_ref, d_ref, nm_ref, nv_ref):
        gv = g_ref[...]
        nm = ADAM_B1 * m_ref[...] + (1.0 - ADAM_B1) * gv
        nv = ADAM_B2 * v_ref[...] + (1.0 - ADAM_B2) * jnp.square(gv)
        m_hat = nm / (1.0 - ADAM_B1 ** ADAM_STEP)
        v_hat = nv / (1.0 - ADAM_B2 ** ADAM_STEP)
        d_ref[...] = -ADAM_LR * (m_hat / (jnp.sqrt(v_hat) + ADAM_EPS) + ADAM_WD * w_ref[...])
        nm_ref[...] = nm
        nv_ref[...] = nv

    if whole:
        spec, grid = pl.BlockSpec(shape, lambda i: (0,) * len(shape)), (1,)
    else:
        spec, grid = pl.BlockSpec((tr, C), lambda i: (i, 0)), (R // tr,)
    outs = pl.pallas_call(
        body, name=name, grid=grid,
        in_specs=[spec] * 4, out_specs=[spec] * 3,
        out_shape=[jax.ShapeDtypeStruct(w2.shape, F32)] * 3,
        compiler_params=_params("parallel"),
    )(w2, g2, m2, v2)
    return tuple(o.reshape(shape) for o in outs)


def _sum_rows(a, name):
    n, L = a.shape

    def body(a_ref, o_ref):
        acc = a_ref[0:1, :]
        for r in range(1, n):
            acc = acc + a_ref[r:r + 1, :]
        o_ref[...] = acc

    return pl.pallas_call(
        body, name=name, grid=(1,),
        in_specs=[pl.BlockSpec((n, L), lambda i: (0, 0))],
        out_specs=pl.BlockSpec((1, L), lambda i: (0, 0)),
        out_shape=jax.ShapeDtypeStruct((1, L), F32),
        compiler_params=_params("arbitrary"),
    )(a)


def _scalar_call(body, name, scalar, grid, in_specs, out_spec, out_shape, args):
    return pl.pallas_call(
        body, name=name,
        grid_spec=pltpu.PrefetchScalarGridSpec(num_scalar_prefetch=1, grid=grid, in_specs=in_specs,
                                               out_specs=out_spec),
        out_shape=out_shape, compiler_params=_params("parallel"),
    )(jnp.reshape(scalar, (-1,)).astype(jnp.int32), *args)


def _pair_sum(view, got, c, name):
    nb, _, rh, cols = view.shape
    tr = _pick(rh, max(16, (1 << 20) // cols // 16 * 16), 16)
    bpr = rh // tr

    def body(s_ref, a_ref, b_ref, o_ref):
        o_ref[...] = (a_ref[...].astype(F32) + b_ref[...].astype(F32)).astype(BF16)

    spec = pl.BlockSpec((tr, cols), lambda i, s: (i, 0))
    mine = pl.BlockSpec((tr, cols), lambda i, s: ((2 * (i // bpr) + s[0]) * bpr + i % bpr, 0))
    return _scalar_call(body, name, c, (nb * bpr,), [mine, spec], spec,
                        jax.ShapeDtypeStruct((nb * rh, cols), BF16),
                        (view.reshape(nb * 2 * rh, cols), got.reshape(nb * rh, cols)))


STACKED_LAYERS = 2


def _owner_sum(pair, recv, me, c, it, name, layer=None, into=None):
    _, rh, bc = recv.shape
    tr = _pick(rh, max(16, (1 << 19) // bc // 16 * 16), 16)
    bpr = rh // tr

    def body(s_ref, a_ref, r0, r1, r2, *rest):
        rest[-1][...] = ((a_ref[...].astype(F32) + r0[...].astype(F32)) + r1[...].astype(F32)) + r2[...].astype(F32)

    if it.kind == "col":
        own = pl.BlockSpec((tr, bc), lambda i, s: (i, s[0]))
    else:
        own = pl.BlockSpec((tr, bc), lambda i, s: (s[0] * bpr + i, 0))
    slots = [pl.BlockSpec((None, tr, bc), functools.partial(lambda i, s, k: (k, i, 0), k=k)) for k in range(3)]
    in_specs, args, aliases = [own] + slots, [pair, recv, recv, recv], {}
    if layer is None:
        out_spec = pl.BlockSpec((tr, bc), lambda i, s: (s[1] * bpr + i, 0))
        out_shape = jax.ShapeDtypeStruct((2 * rh, bc), F32)
    else:
        out_spec = pl.BlockSpec((None, tr, bc), lambda i, s: (layer, s[1] * bpr + i, 0))
        out_shape = jax.ShapeDtypeStruct((STACKED_LAYERS, 2 * rh, bc), F32)
        if into is not None:
            in_specs.append(pl.BlockSpec(memory_space=pl.ANY))
            args.append(into)
            aliases = {len(args): 0}
    return pl.pallas_call(
        body, name=name,
        grid_spec=pltpu.PrefetchScalarGridSpec(num_scalar_prefetch=1, grid=(bpr,), in_specs=in_specs,
                                               out_specs=out_spec),
        out_shape=out_shape, input_output_aliases=aliases, compiler_params=_params("parallel"),
    )(jnp.stack([it.pos(me), c]).astype(jnp.int32), *args)


def _place():
    x, y, c = lax.axis_index("x"), lax.axis_index("y"), lax.axis_index("c")
    chips = [(1 - x, y), (x, 1 - y), (1 - x, 1 - y)]
    return x, y, c, chips


def _chip_index(px, py):
    return 2 * px + py


def _all_gather_small(x_shard, name):
    m_per, n = x_shard.shape

    def body(x_ref, out_ref, send_sems, recv_sems, local_sem):
        x, y, c, chips = _place()
        me, sibling = (x, y, c), (x, y, 1 - c)

        def rows(px, py, pc):
            return out_ref.at[pl.ds((4 * px + 2 * py + pc) * m_per, m_per), :]

        def copy(k, block, to, src=None):
            return pltpu.make_async_remote_copy(
                src_ref=rows(*block) if src is None else src, dst_ref=rows(*block),
                send_sem=send_sems.at[k], recv_sem=recv_sems.at[k], device_id=to, device_id_type=MESH)

        mine = pltpu.make_async_copy(x_ref, rows(*me), local_sem)
        mine.start()
        first = [copy(0, me, sibling, src=x_ref)]
        first += [copy(1 + j, me, (*chip, c), src=x_ref) for j, chip in enumerate(chips)]
        for cp in first:
            cp.start()
        passed = [copy(4 + j, (*chip, c), sibling) for j, chip in enumerate(chips)]
        for j, chip in enumerate(chips):
            copy(1 + j, (*chip, c), me).wait_recv()
            passed[j].start()
        copy(0, sibling, me).wait_recv()
        for j, chip in enumerate(chips):
            copy(4 + j, (*chip, 1 - c), me).wait_recv()
        for cp in first + passed:
            cp.wait_send()
        mine.wait()

    return pl.pallas_call(
        body, name=name,
        out_shape=jax.ShapeDtypeStruct((N_DEV * m_per, n), x_shard.dtype),
        in_specs=[pl.BlockSpec(memory_space=pltpu.VMEM)],
        out_specs=pl.BlockSpec(memory_space=pltpu.VMEM),
        scratch_shapes=[pltpu.SemaphoreType.DMA((7,)), pltpu.SemaphoreType.DMA((7,)), pltpu.SemaphoreType.DMA],
    )(x_shard)


def _gather_flat(vec, name):
    L = vec.shape[0]
    Lp = -(-L // 1024) * 1024
    g = _all_gather_small(jnp.pad(vec, (0, Lp - L)).reshape(8, Lp // 8), name)
    return g.reshape(N_DEV, Lp)[:, :L]


class _Item:
    def __init__(self, kind, rows, cols, arg, layer, swap=False):
        self.kind, self.rows, self.cols, self.arg, self.layer, self.swap = kind, rows, cols, arg, layer, swap

    def ref(self, refs):
        return refs[self.arg].at[self.layer]

    def pos(self, j):
        return 2 * (j % 2) + j // 2 if self.swap else j


def _block(ref, it, j, half):
    if it.kind == "col":
        ns = it.cols // N_CHIP
        return ref.at[pl.ds(half * (it.rows // 2), it.rows // 2), pl.ds(it.pos(j) * ns, ns)]
    rs = it.rows // N_CHIP
    return ref.at[pl.ds(j * rs + half * (rs // 2), rs // 2), :]


def _cast_place(w, layer, kind, pos, after, name):
    _, r, n = w.shape
    tr = _pick(r, max(16, (1 << 20) // n // 16 * 16), 16)
    bpr = r // tr

    def body(s_ref, w_ref, after_ref, o_ref):
        o_ref[...] = w_ref[...].astype(BF16)

    if kind == "col":
        full, out_idx = (1, r, N_CHIP * n), (lambda i, s: (0, i, s[0]))
    else:
        full, out_idx = (1, N_CHIP * r, n), (lambda i, s: (0, s[0] * bpr + i, 0))
    return pl.pallas_call(
        body, name=name,
        grid_spec=pltpu.PrefetchScalarGridSpec(
            num_scalar_prefetch=1, grid=(bpr,),
            in_specs=[pl.BlockSpec((None, tr, n), lambda i, s: (layer, i, 0)), pl.BlockSpec(memory_space=pl.ANY)],
            out_specs=pl.BlockSpec((None, tr, n), out_idx)),
        out_shape=jax.ShapeDtypeStruct(full, BF16),
        compiler_params=_params("parallel"),
    )(jnp.reshape(pos, (1,)).astype(jnp.int32), w, after)


HBM_SPEC = pl.BlockSpec(memory_space=pltpu.HBM)
SEM_SPEC = pl.BlockSpec(memory_space=pltpu.SEMAPHORE)
ANY_SPEC = pl.BlockSpec(memory_space=pl.ANY)
SPLIT_PARAMS = dict(has_side_effects=pltpu.SideEffectType.DATAFLOW_SIDE_EFFECTING)


def _in_hbm(a):
    return pltpu.with_memory_space_constraint(a, pltpu.HBM)


def _split_start(copies_of, bufs, n_sem, after, name):
    n = len(bufs)

    def body(*refs):
        ins, send, recv, token = refs[:n], refs[n + 1], refs[n + 2], refs[2 * n + 3]
        for cp in copies_of(ins, send, recv, False)[0]:
            cp.start()
        token[...] = jnp.zeros_like(token)

    outs = pl.pallas_call(
        body, name=name,
        out_shape=(pltpu.SemaphoreType.DMA(n_sem), pltpu.SemaphoreType.DMA(n_sem),
                   *[pltpu.HBM(b.shape, b.dtype) for b in bufs], jax.ShapeDtypeStruct((8, 128), F32)),
        in_specs=[HBM_SPEC] * n + [ANY_SPEC],
        out_specs=(SEM_SPEC, SEM_SPEC, *[HBM_SPEC] * n, pl.BlockSpec(memory_space=pltpu.VMEM)),
        input_output_aliases={t: 2 + t for t in range(n)},
        compiler_params=pltpu.CompilerParams(**SPLIT_PARAMS),
    )(*[_in_hbm(b) for b in bufs], after)
    return outs[0], outs[1], list(outs[2:2 + n]), outs[2 + n]


def _split_wait(copies_of, send, recv, bufs, after, name):
    n = len(bufs)

    def body(*refs):
        ins, send_ref, recv_ref = refs[:n], refs[n], refs[n + 1]
        sends, arrivals = copies_of(ins, send_ref, recv_ref, True)
        for cp in sends:
            cp.wait_send()
        for cp in arrivals:
            cp.wait_recv()

    return pl.pallas_call(
        body, name=name,
        out_shape=[pltpu.HBM(b.shape, b.dtype) for b in bufs],
        in_specs=[HBM_SPEC] * n + [SEM_SPEC, SEM_SPEC, ANY_SPEC],
        out_specs=[HBM_SPEC] * n,
        input_output_aliases={t: t for t in range(n)},
        compiler_params=pltpu.CompilerParams(**SPLIT_PARAMS),
    )(*bufs, send, recv, after)


def _gather_copies(items):
    def copies_of(refs, send, recv, with_arrivals):
        x, y, c, chips = _place()
        me = _chip_index(x, y)
        sends, arrivals = [], []
        for t, it in enumerate(items):
            for k, chip in enumerate(chips):
                for core in range(2):
                    mine = _block(it.ref(refs), it, me, c)
                    sends.append(pltpu.make_async_remote_copy(
                        src_ref=mine, dst_ref=mine, send_sem=send.at[6 * t + 2 * k + core],
                        recv_sem=recv.at[6 * t + 2 * k + c], device_id=(*chip, core), device_id_type=MESH))
                    if with_arrivals:
                        landed = _block(it.ref(refs), it, _chip_index(*chip), core)
                        arrivals.append(pltpu.make_async_remote_copy(
                            src_ref=landed, dst_ref=landed, send_sem=send.at[6 * t + 2 * k + core],
                            recv_sem=recv.at[6 * t + 2 * k + core], device_id=(*chip, core), device_id_type=MESH))
        return sends, arrivals

    return copies_of


def _owner_copies(items):
    n = len(items)

    def blk(ref, it, j):
        if it.kind == "col":
            ns = it.cols // N_CHIP
            return ref.at[:, pl.ds(it.pos(j) * ns, ns)]
        return ref.at[j]

    def copies_of(refs, send, recv, with_arrivals):
        x, y, c, chips = _place()
        sends, arrivals = [], []
        for t, it in enumerate(items):
            for k, chip in enumerate(chips):
                slot = refs[n + t].at[k]
                sends.append(pltpu.make_async_remote_copy(
                    src_ref=blk(refs[t], it, _chip_index(*chip)), dst_ref=slot, send_sem=send.at[3 * t + k],
                    recv_sem=recv.at[3 * t + k], device_id=(*chip, c), device_id_type=MESH))
                if with_arrivals:
                    arrivals.append(pltpu.make_async_remote_copy(
                        src_ref=slot, dst_ref=slot, send_sem=send.at[3 * t + k], recv_sem=recv.at[3 * t + k],
                        device_id=(*chip, c), device_id_type=MESH))
        return sends, arrivals

    return copies_of


def _owner_slot_shape(it):
    if it.kind == "col":
        return (3, it.rows // 2, it.cols // N_CHIP)
    return (3, it.rows // (2 * N_CHIP), it.cols)


def _pair_view(g, it):
    if it.kind == "col":
        return g.reshape(1, 2, it.rows // 2, it.cols)
    return g.reshape(N_CHIP, 2, it.rows // (2 * N_CHIP), it.cols)


def _pair_copies(n):
    def copies_of(refs, send, recv, with_arrivals):
        x, y, c, _ = _place()
        sends, arrivals = [], []
        for t in range(n):
            land = refs[n + t]
            sends.append(pltpu.make_async_remote_copy(
                src_ref=refs[t].at[:, pl.ds(1 - c, 1)], dst_ref=land, send_sem=send.at[t], recv_sem=recv.at[t],
                device_id=(x, y, 1 - c), device_id_type=MESH))
            if with_arrivals:
                arrivals.append(pltpu.make_async_remote_copy(
                    src_ref=land, dst_ref=land, send_sem=send.at[t], recv_sem=recv.at[t],
                    device_id=(x, y, 1 - c), device_id_type=MESH))
        return sends, arrivals

    return copies_of


def _half_copies(n):
    def half(ref, which):
        r2 = ref.shape[-2] // 2
        rows = pl.ds(which * r2, r2)
        return ref.at[rows, :] if len(ref.shape) == 2 else ref.at[:, rows, :]

    def copies_of(refs, send, recv, with_arrivals):
        x, y, c, _ = _place()
        sends, arrivals = [], []
        for t in range(n):
            mine = half(refs[t], c)
            sends.append(pltpu.make_async_remote_copy(
                src_ref=mine, dst_ref=mine, send_sem=send.at[t], recv_sem=recv.at[t],
                device_id=(x, y, 1 - c), device_id_type=MESH))
            if with_arrivals:
                theirs = half(refs[t], 1 - c)
                arrivals.append(pltpu.make_async_remote_copy(
                    src_ref=theirs, dst_ref=theirs, send_sem=send.at[t], recv_sem=recv.at[t],
                    device_id=(x, y, 1 - c), device_id_type=MESH))
        return sends, arrivals

    return copies_of


class _Reduction:
    pass


def _pair_start(grads, items, after, tag, names, layer=None):
    n = len(items)
    views = [_pair_view(g, it) for g, it in zip(grads, items)]
    lands = [lax.empty((v.shape[0], 1) + v.shape[2:], v.dtype) for v in views]
    r = _Reduction()
    r.items, r.tag, r.names, r.layer = items, tag, names, layer
    r.send, r.recv, r.bufs, r.token = _split_start(_pair_copies(n), views + lands, (n,), after, f"rs_pair_start_{tag}")
    return r


def _owner_start(r, after):
    x, y, c, _ = _place()
    n = len(r.items)
    bufs = _split_wait(_pair_copies(n), r.send, r.recv, r.bufs, after, f"rs_pair_wait_{r.tag}")
    pairs = [_pair_sum(bufs[t], bufs[n + t], c, f"rs_pair_sum_{r.tag}_{t}") for t in range(n)]
    shaped = [p if it.kind == "col" else p.reshape(N_CHIP, p.shape[0] // N_CHIP, p.shape[1])
              for p, it in zip(pairs, r.items)]
    lands = [lax.empty(_owner_slot_shape(it), BF16) for it in r.items]
    r.send, r.recv, r.bufs, r.token = _split_start(
        _owner_copies(r.items), shaped + lands, (3 * n,), r.token, f"rs_owner_start_{r.tag}")
    return r


def _reduce_finish(groups, after):
    x, y, c, _ = _place()
    me = _chip_index(x, y)
    halves = {}
    for r in groups:
        n = len(r.items)
        bufs = _split_wait(_owner_copies(r.items), r.send, r.recv, r.bufs, after, f"rs_owner_wait_{r.tag}")
        for t, (it, nm) in enumerate(zip(r.items, r.names)):
            pair = bufs[t].reshape(-1, bufs[t].shape[-1])
            halves[nm] = _owner_sum(pair, bufs[n + t], me, c, it, f"rs_owner_sum_{r.tag}_{t}",
                                    layer=r.layer, into=halves.get(nm))
    n = len(halves)
    return list(halves), _split_start(_half_copies(n), list(halves.values()), (n,), after, "rs_half_start")


def _silu(v):
    return v * jax.nn.sigmoid(v)


def _sum8(p):
    return jnp.sum(p, axis=-2)


def kernel(x, c, mod_w, mod_b, norm_g, ffn_w_in, ffn_w_out, conv_w_in, conv_k, conv_w_out, kv_mod_w, kv_mod_b, kv_norm_g, w_kv, attn_w_q, attn_w_o, rel_bias, loss_target, m_mod_w, m_mod_b, m_norm_g, m_ffn_w_in, m_ffn_w_out, m_conv_w_in, m_conv_k, m_conv_w_out, m_kv_mod_w, m_kv_mod_b, m_kv_norm_g, m_w_kv, m_attn_w_q, m_attn_w_o, m_rel_bias, v_mod_w, v_mod_b, v_norm_g, v_ffn_w_in, v_ffn_w_out, v_conv_w_in, v_conv_k, v_conv_w_out, v_kv_mod_w, v_kv_mod_b, v_kv_norm_g, v_w_kv, v_attn_w_q, v_attn_w_o, v_rel_bias):
    xi, yi, ci = lax.axis_index("x"), lax.axis_index("y"), lax.axis_index("c")
    chip = 2 * xi + yi
    dev = 2 * chip + ci
    _, S, D = x.shape
    F = ffn_w_out.shape[1] * N_CHIP
    x0 = x.reshape(S, D)
    target = loss_target.reshape(S, D)
    n_mod = mod_w.shape[2]
    n_kvm = kv_mod_w.shape[1]
    dsh = D // N_CHIP
    TF = F // 2

    c_all = _all_gather_small(c.reshape(8, D // 8), "ag_c").reshape(N_DEV, D)
    sc16 = jnp.pad(_silu(c_all), ((0, 8), (0, 0)))
    part = [_mm(sc16, mod_w, "nn", F32, f"mod_fwd_{l}", b_layer=l)[:8] for l in range(2)]
    part.append(_mm(sc16, kv_mod_w, "nn", F32, "mod_fwd_kv")[:8])
    fwd_vec = jnp.concatenate([p.reshape(-1) for p in part] + [norm_g.reshape(-1), conv_k.reshape(-1)])
    fwd_all = _gather_flat(fwd_vec, "ag_fwd_small")[0::2]
    o = 0
    mods = []
    for n in (n_mod, n_mod, n_kvm):
        blk = fwd_all[:, o:o + 8 * n].reshape(N_CHIP, 8, n)
        mods.append(lax.dynamic_index_in_dim(blk, dev, axis=1, keepdims=False).reshape(N_CHIP * n))
        o += 8 * n
    ng = fwd_all[:, o:o + 8 * dsh].reshape(N_CHIP, 2, 4, dsh).transpose(1, 2, 0, 3).reshape(2, 4, D)
    o += 8 * dsh
    ck = fwd_all[:, o:o + 3 * dsh].reshape(N_CHIP, 3, dsh).transpose(1, 0, 2).reshape(3, D)
    ck8 = jnp.pad(ck, ((0, 5), (0, 0)))
    mod = [mods[l] + mod_b[l] for l in range(2)]
    sh1, sc1, g1, sh2, sc2, g2 = zip(*[jnp.split(m, 6) for m in mod])
    kv_sh, kv_sc = jnp.split(mods[2] + kv_mod_b, 2)
    row = lambda v: v.reshape(1, D)

    it_conv = [_Item("col", D, 3 * D, 0, 0), _Item("row", D, D, 1, 0)]
    it_ffn = [_Item("col", D, 2 * F, 0, 0, swap=True), _Item("row", F, D, 1, 0)]
    it_attn = [_Item("col", D, 2 * D, 0, 0, swap=True), _Item("row", D, D, 1, 0), _Item("row", D, D, 2, 0)]

    def placed(w, layer, it, nm, after=fwd_all):
        return _cast_place(w, layer, it.kind, it.pos(chip), after, f"place_{nm}")

    flying = {}

    def start(tag, its, bufs, after):
        send, recv, bufs, tok = _split_start(_gather_copies(its), bufs, (6 * len(its),), after, f"ag_start_{tag}")
        flying[tag] = (its, send, recv, bufs)
        return tok

    def arrived(tag, after):
        its, send, recv, bufs = flying[tag]
        return _split_wait(_gather_copies(its), send, recv, bufs, after, f"ag_wait_{tag}")

    one = lambda it: [_Item(it.kind, it.rows, it.cols, 0, 0, it.swap)]
    tok = start("conv_in", one(it_conv[0]), [placed(conv_w_in, 0, it_conv[0], "conv_w_in")], fwd_all)
    tok = start("conv_out", one(it_conv[1]), [placed(conv_w_out, 0, it_conv[1], "conv_w_out", tok)], tok)
    tok = start("ffn0_in", one(it_ffn[0]), [placed(ffn_w_in, 0, it_ffn[0], "ffn_w_in0", tok)], tok)
    tok = start("ffn0_out", one(it_ffn[1]), [placed(ffn_w_out, 0, it_ffn[1], "ffn_w_out0", tok)], tok)
    tok = start("attn", it_attn, [placed(w_kv[None], 0, it_attn[0], "w_kv", tok),
                                  placed(attn_w_q, 0, it_attn[1], "attn_w_q", tok),
                                  placed(attn_w_o, 0, it_attn[2], "attn_w_o", tok)], tok)
    token = start("ffn1", it_ffn, [placed(ffn_w_in, 1, it_ffn[0], "ffn_w_in1", tok),
                                   placed(ffn_w_out, 1, it_ffn[1], "ffn_w_out1", tok)], tok)

    a1 = row(ng[0, 0] * (1.0 + sc1[0])) + token[0, 0]
    (h1,) = _norm_mod(x0, a1, row(sh1[0]), "l0_norm1")
    tab = _bias_table(rel_bias[0], "l1_bias_table")
    h1, tab = lax.optimization_barrier((h1, tab))
    (W_cin,) = arrived("conv_in", h1)
    bcx = _mm(h1, W_cin, "nn", BF16, "l0_conv_in", b_layer=0, tm=512, tn=3 * D)
    ug = _conv_gate(bcx, ck8, "l0_conv_gate")
    gt1 = row(g1[0] * ng[0, 1])
    a2 = row(ng[0, 2] * (1.0 + sc2[0]))
    (W_cout,) = arrived("conv_out", ug)
    y1, x1, h2 = _mm_post(ug, W_cout, x0, gt1, "l0_conv_out", scales=a2, shifts=row(sh2[0]))
    (W_fin0,) = arrived("ffn0_in", h2)
    gu0, act0 = _ffn_in_act(h2, W_fin0, 0, "l0_ffn_in")
    (W_fout0,) = arrived("ffn0_out", act0)
    gt2 = row(g2[0] * ng[0, 3])
    a3 = ng[1, 0] * (1.0 + sc1[1])
    akv = kv_norm_g * (1.0 + kv_sc)
    y2, x2, h3, hkv = _mm_post(act0, W_fout0, x1, gt2, "l0_ffn_out",
                               scales=jnp.stack([a3, akv]), shifts=jnp.stack([sh1[1], kv_sh]))
    W_kv, W_q, W_o = arrived("attn", hkv)
    kvp = _mm(hkv, W_kv, "nn", BF16, "l1_kv", b_layer=0, tm=512, tn=2 * D)
    att_scale = (D // N_HEADS) ** -0.5
    assert math.log2(att_scale) % 1 == 0, "scaling q before its bf16 cast is exact only for a power of two"
    qp = _mm(h3, W_q, "nn", BF16, "l1_q", b_layer=0, scale=att_scale)
    oh = _attn_fwd(qp, kvp, tab, "l1_attn")
    gt3 = row(g1[1] * ng[1, 1])
    a4 = row(ng[1, 2] * (1.0 + sc2[1]))
    y3, x3, h4 = _mm_post(oh, W_o, x2, gt3, "l1_attn_out", scales=a4, shifts=row(sh2[1]))
    W_fin1, W_fout1 = arrived("ffn1", h4)
    gu1, act1 = _ffn_in_act(h4, W_fin1, 0, "l1_ffn_in")
    gt4 = row(g2[1] * ng[1, 3])
    dx4, sq, dy4, dgt4 = _mm_post(act1, W_fout1, x3, gt4, "l1_ffn_out", target=target)
    loss_part = 0.5 * jnp.sum(sq) / D

    def ffn_bwd(dy, dxn, xin_, h, gu, act, a, w_in, w_out, post, tag):
        dgu, dx, ds, db, dyn, dgt = _ffn_bwd(dy, w_out, gu, w_in, xin_, dxn, a, post, f"{tag}_ffn_bwd")
        g_fout = _mm(act, dy, "tn", BF16, f"{tag}_ffn_out_dw", tm=TF)
        g_fin = _mm(h, dgu, "tn", BF16, f"{tag}_ffn_in_dw", tn=TF)
        return dx, ds, db, dyn, dgt, g_fin, g_fout

    dx3, ds4, db4, dy3, dgt3, G_fin1, G_fout1 = ffn_bwd(dy4, dx4, x3, h4, gu1, act1, a4, W_fin1, W_fout1,
                                                        (y3, gt3), "l1")
    red = [_pair_start([G_fin1, G_fout1], it_ffn, token, "ffn1", ["ffn_w_in", "ffn_w_out"], layer=1)]
    doh = _mm(dy3, W_o, "nt", BF16, "l1_attn_out_dx", b_layer=0, after=red[0].token)
    G_o = _mm(oh, dy3, "tn", BF16, "l1_attn_out_dw")
    _owner_start(red[0], G_o)
    dq, dkv, dtab = _attn_bwd(qp, kvp, tab, doh, "l1_attn_bwd")
    d_rel = _bias_table_grad(dtab)
    G_q = _mm(h3, dq, "tn", BF16, "l1_q_dw")
    G_kv = _mm(hkv, dkv, "tn", BF16, "l1_kv_dw")
    red.append(_pair_start([G_kv, G_q, G_o], it_attn, red[-1].token, "attn", ["w_kv", "attn_w_q", "attn_w_o"]))
    dx2, ds3, db3, dy2, dgt2 = _mm_pre_bwd([(dq, W_q), (dkv, W_kv)], x2, dx3,
                                           jnp.stack([a3, akv]) + red[1].token[0, 0], "l1_qkv_dx", post=(y2, gt2))
    _owner_start(red[1], dx2)

    dx1, ds2, db2, dy1, dgt1, G_fin0, G_fout0 = ffn_bwd(dy2, dx2, x1, h2, gu0, act0, a2, W_fin0, W_fout0,
                                                        (y1, gt1), "l0")
    red.append(_pair_start([G_fin0, G_fout0], it_ffn, red[-1].token, "ffn0", ["ffn_w_in", "ffn_w_out"], layer=0))
    dug = _mm(dy1, W_cout, "nt", BF16, "l0_conv_out_dx", b_layer=0, after=red[2].token)
    G_cout = _mm(ug, dy1, "tn", BF16, "l0_conv_out_dw")
    _owner_start(red[2], G_cout)
    dbcx, dck = _conv_gate_bwd(dug, bcx, ck8, "l0_conv_gate_bwd")
    G_cin = _mm(h1, dbcx, "tn", BF16, "l0_conv_in_dw")
    red.append(_pair_start([G_cin, G_cout], it_conv, red[-1].token, "conv", ["conv_w_in", "conv_w_out"]))
    dx0, ds1, db1 = _mm_pre_bwd([(dbcx, W_cin)], x0, dx1, a1 + red[3].token[0, 0], "l0_conv_in_dx")
    ds1, db1 = _sum8(ds1)[0], _sum8(db1)[0]
    da2, db2 = _sum8(ds2)[0], _sum8(db2)[0]
    ds3, db3 = _sum8(ds3), _sum8(db3)
    da4, db4 = _sum8(ds4)[0], _sum8(db4)[0]
    dgt1, dgt2, dgt3, dgt4 = _sum8(dgt1), _sum8(dgt2), _sum8(dgt3), _sum8(dgt4)

    def dmod_of(l, ds_a, db_a, dgt_a, ds_b, db_b, dgt_b):
        return jnp.concatenate([db_a, ds_a * ng[l, 0], dgt_a * ng[l, 1], db_b, ds_b * ng[l, 2], dgt_b * ng[l, 3]])

    dmod0 = dmod_of(0, ds1, db1, dgt1, da2, db2, dgt2)
    dmod1 = dmod_of(1, ds3[0], db3[0], dgt3, da4, db4, dgt4)
    dkvmod = jnp.concatenate([db3[1], ds3[1] * kv_norm_g])
    dng = jnp.stack([
        jnp.stack([ds1 * (1.0 + sc1[0]), dgt1 * g1[0], da2 * (1.0 + sc2[0]), dgt2 * g2[0]]),
        jnp.stack([ds3[0] * (1.0 + sc1[1]), dgt3 * g1[1], da4 * (1.0 + sc2[1]), dgt4 * g2[1]])])
    dkvng = ds3[1] * (1.0 + kv_sc)
    small = [dmod0, dmod1, dkvmod, dng.reshape(-1), dkvng, _sum8(dck).reshape(-1), d_rel.reshape(-1),
             loss_part.reshape(1)]
    sizes = [int(s.shape[0]) for s in small]
    offs = np.concatenate([[0], np.cumsum(sizes)])
    bwd_all = _gather_flat(jnp.concatenate(small), "ag_bwd_small")
    _owner_start(red[3], bwd_all)
    Lb = bwd_all.shape[1]
    Lp = -(-Lb // 128) * 128
    tot = _sum_rows(jnp.pad(bwd_all, ((0, 0), (0, Lp - Lb))), "sum_small")[0]
    seg = lambda i: tot[offs[i]:offs[i + 1]]
    g_mod_b = jnp.stack([seg(0), seg(1)])
    g_kv_mod_b = seg(2)
    g_norm_g = lax.dynamic_slice_in_dim(seg(3).reshape(2, 4, D), chip * dsh, dsh, axis=2)
    g_kv_norm_g = seg(4)
    g_conv_k = lax.dynamic_slice_in_dim(seg(5).reshape(1, 3, D), chip * dsh, dsh, axis=2)
    g_rel_bias = seg(6).reshape(rel_bias.shape)
    loss = seg(7)[0]

    def dmod_w(i, n, name):
        rows_ = lax.dynamic_slice_in_dim(bwd_all[:, offs[i]:offs[i + 1]], chip * n, n, axis=1)
        return _mm(sc16, jnp.pad(rows_, ((0, 8), (0, 0))), "tn", F32, name)

    g_mod_w = jnp.stack([dmod_w(0, n_mod, "mod_bwd_0"), dmod_w(1, n_mod, "mod_bwd_1")])
    g_kv_mod_w = dmod_w(2, n_kvm, "mod_bwd_kv")

    grads = {
        "mod_w": g_mod_w, "mod_b": g_mod_b, "norm_g": g_norm_g, "conv_k": g_conv_k,
        "kv_mod_w": g_kv_mod_w, "kv_mod_b": g_kv_mod_b, "kv_norm_g": g_kv_norm_g, "rel_bias": g_rel_bias,
    }
    weights = dict(mod_w=mod_w, mod_b=mod_b, norm_g=norm_g, ffn_w_in=ffn_w_in, ffn_w_out=ffn_w_out,
                   conv_w_in=conv_w_in, conv_k=conv_k, conv_w_out=conv_w_out, kv_mod_w=kv_mod_w,
                   kv_mod_b=kv_mod_b, kv_norm_g=kv_norm_g, w_kv=w_kv, attn_w_q=attn_w_q, attn_w_o=attn_w_o,
                   rel_bias=rel_bias)
    m_in = dict(mod_w=m_mod_w, mod_b=m_mod_b, norm_g=m_norm_g, ffn_w_in=m_ffn_w_in, ffn_w_out=m_ffn_w_out,
                conv_w_in=m_conv_w_in, conv_k=m_conv_k, conv_w_out=m_conv_w_out, kv_mod_w=m_kv_mod_w,
                kv_mod_b=m_kv_mod_b, kv_norm_g=m_kv_norm_g, w_kv=m_w_kv, attn_w_q=m_attn_w_q,
                attn_w_o=m_attn_w_o, rel_bias=m_rel_bias)
    v_in = dict(mod_w=v_mod_w, mod_b=v_mod_b, norm_g=v_norm_g, ffn_w_in=v_ffn_w_in, ffn_w_out=v_ffn_w_out,
                conv_w_in=v_conv_w_in, conv_k=v_conv_k, conv_w_out=v_conv_w_out, kv_mod_w=v_kv_mod_w,
                kv_mod_b=v_kv_mod_b, kv_norm_g=v_kv_norm_g, w_kv=v_w_kv, attn_w_q=v_attn_w_q,
                attn_w_o=v_attn_w_o, rel_bias=v_rel_bias)
    names = list(weights)
    step = {}

    def update(n):
        g = grads[n].reshape(weights[n].shape)
        step[n] = (g, *_adamw(weights[n], g, m_in[n], v_in[n], f"adamw_{n}"))

    update("mod_w")
    reduced, (half_send, half_recv, half_bufs, _) = _reduce_finish(red, step["mod_w"][1])
    for n in list(grads):
        if n not in step:
            update(n)
    grads.update(zip(reduced, _split_wait(
        _half_copies(len(half_bufs)), half_send, half_recv, half_bufs, step["kv_mod_w"][1], "rs_half_wait")))
    for n in names:
        if n not in step:
            update(n)
    return (loss, dx0.reshape(x.shape), *[step[n][k] for k in range(4) for n in names])
```

```python
import functools
import math

import numpy as np
import jax
import jax.numpy as jnp
from jax import lax
from jax.experimental import pallas as pl
from jax.experimental.pallas import tpu as pltpu

CHUNK = 64
N_LEFT_CHUNKS = 8
N_HEADS = 16
MAX_REL = 2 * CHUNK
N_REL = 2 * MAX_REL + 1
EPS = 1e-6
ADAM_LR = 0.001
ADAM_B1 = 0.9
ADAM_B2 = 0.999
ADAM_EPS = 1e-08
ADAM_WD = 0.01
ADAM_STEP = 10

Q_CHUNKS = 4
BQ = Q_CHUNKS * CHUNK
N_WIN = 1 + N_LEFT_CHUNKS // Q_CHUNKS
HEADS_PER_STEP = 8
NEG = -1e30
N_DEV = 8
N_CHIP = 4
SMALL_TENSOR_ELEMS = 1 << 16
PIECE_ROWS = 256

BF16 = jnp.bfloat16
F32 = jnp.float32
V7X_VMEM_LIMIT_BYTES = 56 * 1024 * 1024
MESH = pl.DeviceIdType.MESH


def _pick(n, pref, align):
    t = min(pref, n)
    t -= t % align
    while t >= align:
        if n % t == 0:
            return t
        t -= align
    return n


def _params(*sem):
    return pltpu.CompilerParams(dimension_semantics=sem, vmem_limit_bytes=V7X_VMEM_LIMIT_BYTES)


def _colsum8(v):
    r, d = v.shape
    return v.reshape(r // 8, 8, d).sum(axis=0)


_DIMS = {"nn": (((1,), (0,)), ((), ())), "nt": (((1,), (1,)), ((), ())), "tn": (((0,), (0,)), ((), ()))}


def _mm(a, b, mode, out_dtype, name, *, b_layer=None, tm=1024, tn=1024, tk=None, scale=None, after=None):
    if tk is None:
        tk = 2048 if mode == "tn" else 3072
    bs = b.shape[1:] if b_layer is not None else b.shape
    if mode == "nn":
        (M, K), (K2, N) = a.shape, bs
    elif mode == "nt":
        (M, K), (N, K2) = a.shape, bs
    else:
        (K, M), (K2, N) = a.shape, bs
    assert K == K2, (name, a.shape, b.shape)
    tm = _pick(M, tm, 128 if mode == "tn" else 16)
    tn = _pick(N, tn, 128)
    tk = _pick(K, tk, 128 if mode != "tn" else 16)
    nk = K // tk
    assert scale is None or nk == 1, name
    dims = _DIMS[mode]
    extra = [] if after is None else [after]

    def body(a_ref, b_ref, *rest):
        o_ref, acc = rest[len(extra)], rest[len(extra) + 1:]
        p = lax.dot_general(a_ref[...].astype(BF16), b_ref[...].astype(BF16), dims,
                            preferred_element_type=F32)
        if nk == 1:
            o_ref[...] = (p if scale is None else p * scale).astype(o_ref.dtype)
        else:
            k = pl.program_id(2)

            @pl.when(k == 0)
            def _():
                acc[0][...] = p

            @pl.when(k > 0)
            def _():
                acc[0][...] += p

            @pl.when(k == nk - 1)
            def _():
                o_ref[...] = acc[0][...].astype(o_ref.dtype)

    a_spec = (pl.BlockSpec((tk, tm), lambda i, j, k: (k, i)) if mode == "tn"
              else pl.BlockSpec((tm, tk), lambda i, j, k: (i, k)))
    if mode == "nt":
        b_blk, b_idx = (tn, tk), (lambda i, j, k: (j, k))
    else:
        b_blk, b_idx = (tk, tn), (lambda i, j, k: (k, j))
    if b_layer is not None:
        b_spec = pl.BlockSpec((None,) + b_blk, lambda i, j, k: (b_layer,) + b_idx(i, j, k))
    else:
        b_spec = pl.BlockSpec(b_blk, b_idx)
    return pl.pallas_call(
        body, name=name,
        grid=(M // tm, N // tn, nk),
        in_specs=[a_spec, b_spec] + [pl.BlockSpec(memory_space=pl.ANY)] * len(extra),
        out_specs=pl.BlockSpec((tm, tn), lambda i, j, k: (i, j)),
        out_shape=jax.ShapeDtypeStruct((M, N), out_dtype),
        scratch_shapes=[pltpu.VMEM((tm, tn), F32)] if nk > 1 else [],
        compiler_params=_params("parallel", "parallel", "arbitrary"),
    )(a, b, *extra)


def _row_spec(tm, d):
    return pl.BlockSpec((tm, d), lambda i: (i, 0))


def _vec_spec(r, d):
    return pl.BlockSpec((r, d), lambda i: (0, 0))


def _norm_mod(x, scales, shifts, name):
    S, D = x.shape
    nb = scales.shape[0]
    tm = _pick(S, 1024, 16)

    def body(x_ref, a_ref, b_ref, *o_refs):
        xv = x_ref[...]
        xh = xv * lax.rsqrt(jnp.mean(xv * xv, axis=-1, keepdims=True) + EPS)
        for n in range(nb):
            o_refs[n][...] = (xh * a_ref[n:n + 1, :] + b_ref[n:n + 1, :]).astype(BF16)

    return pl.pallas_call(
        body, name=name, grid=(S // tm,),
        in_specs=[_row_spec(tm, D), _vec_spec(nb, D), _vec_spec(nb, D)],
        out_specs=[_row_spec(tm, D)] * nb,
        out_shape=[jax.ShapeDtypeStruct((S, D), BF16)] * nb,
        compiler_params=_params("parallel"),
    )(x, scales, shifts)


def _mm_post(a, w, x, gate, name, *, scales=None, shifts=None, target=None):
    M, K = a.shape
    D = w.shape[2]
    tm = _pick(M, 1024 if K <= D else 512, 16)
    sub = _pick(tm, PIECE_ROWS, 16)
    nb = 0 if scales is None else scales.shape[0]

    def body(a_ref, w_ref, x_ref, g_ref, *rest):
        if target is None:
            sc_ref, sh_ref, y_ref, xn_ref = rest[:4]
            h_refs = rest[4:]
        else:
            t_ref, dx_ref, sq_ref, dy_ref, dg_ref = rest

            @pl.when(pl.program_id(0) == 0)
            def _():
                sq_ref[...] = jnp.zeros_like(sq_ref)
                dg_ref[...] = jnp.zeros_like(dg_ref)

        def product(r):
            return jnp.dot(a_ref[pl.ds(r * sub, sub), :], w_ref[...], preferred_element_type=F32)

        y = product(0)
        for r in range(tm // sub):
            rows = pl.ds(r * sub, sub)
            yb = y.astype(BF16)
            if r + 1 < tm // sub:
                y = product(r + 1)
            yv = yb.astype(F32)
            yh = yv * lax.rsqrt(jnp.mean(yv * yv, axis=-1, keepdims=True) + EPS)
            xn = x_ref[rows, :] + yh * g_ref[...]
            if target is None:
                y_ref[rows, :] = yb
                xn_ref[rows, :] = xn
                xh = xn * lax.rsqrt(jnp.mean(xn * xn, axis=-1, keepdims=True) + EPS)
                for n in range(nb):
                    h_refs[n][rows, :] = (xh * sc_ref[n:n + 1, :] + sh_ref[n:n + 1, :]).astype(BF16)
            else:
                e = xn - t_ref[rows, :]
                dx = e / D
                dx_ref[rows, :] = dx
                sq_ref[...] += _colsum8(e * e)
                dy, dxy = _post_norm_grad(dx, yb, g_ref[...])
                dy_ref[rows, :] = dy.astype(BF16)
                dg_ref[...] += _colsum8(dxy)

    ins = [a, w, x, gate]
    in_specs = [_row_spec(tm, K), pl.BlockSpec((None, K, D), lambda i: (0, 0, 0)), _row_spec(tm, D), _vec_spec(1, D)]
    if target is None:
        ins += [scales, shifts]
        in_specs += [_vec_spec(nb, D), _vec_spec(nb, D)]
        out_specs = [_row_spec(tm, D)] * (2 + nb)
        out_shape = [jax.ShapeDtypeStruct((M, D), BF16), jax.ShapeDtypeStruct((M, D), F32)] \
            + [jax.ShapeDtypeStruct((M, D), BF16)] * nb
    else:
        ins += [target]
        in_specs += [_row_spec(tm, D)]
        out_specs = [_row_spec(tm, D), _vec_spec(8, D), _row_spec(tm, D), _vec_spec(8, D)]
        out_shape = [jax.ShapeDtypeStruct((M, D), F32), jax.ShapeDtypeStruct((8, D), F32),
                     jax.ShapeDtypeStruct((M, D), BF16), jax.ShapeDtypeStruct((8, D), F32)]
    return pl.pallas_call(
        body, name=name, grid=(M // tm,), in_specs=in_specs, out_specs=out_specs, out_shape=out_shape,
        compiler_params=_params("arbitrary" if target is not None else "parallel"),
    )(*ins)


def _post_norm_grad(dxn, yb, gate):
    yv = yb.astype(F32)
    r = lax.rsqrt(jnp.mean(yv * yv, axis=-1, keepdims=True) + EPS)
    yh = yv * r
    dyh = dxn * gate
    return r * (dyh - yh * jnp.mean(dyh * yh, axis=-1, keepdims=True)), dxn * yh


def _mm_pre_bwd(pairs, x, dxn, scales, name, post=None):
    S, D = x.shape
    nb = len(pairs)
    tm = _pick(S, 512, 16)
    sub = _pick(tm, PIECE_ROWS, 16)

    def body(*refs):
        a_refs, w_refs = refs[0:2 * nb:2], refs[1:2 * nb:2]
        x_ref, d_ref, sc_ref = refs[2 * nb:2 * nb + 3]
        rest = refs[2 * nb + 3:]
        if post is not None:
            y_ref, g_ref, dx_ref, ds_ref, db_ref, dy_ref, dg_ref = rest
        else:
            dx_ref, ds_ref, db_ref = rest

        @pl.when(pl.program_id(0) == 0)
        def _():
            ds_ref[...] = jnp.zeros_like(ds_ref)
            db_ref[...] = jnp.zeros_like(db_ref)
            if post is not None:
                dg_ref[...] = jnp.zeros_like(dg_ref)

        def products(r):
            return [lax.dot_general(a_refs[n][pl.ds(r * sub, sub), :], w_refs[n][...], _DIMS["nt"],
                                    preferred_element_type=F32) for n in range(nb)]

        nxt = products(0)
        for r in range(tm // sub):
            rows = pl.ds(r * sub, sub)
            dhs = nxt
            if r + 1 < tm // sub:
                nxt = products(r + 1)
            xv = x_ref[rows, :]
            rr = lax.rsqrt(jnp.mean(xv * xv, axis=-1, keepdims=True) + EPS)
            xh = xv * rr
            dxh = jnp.zeros_like(xv)
            for n in range(nb):
                dh = dhs[n]
                dxh = dxh + dh * sc_ref[n:n + 1, :]
                ds_ref[n] += _colsum8(dh * xh)
                db_ref[n] += _colsum8(dh)
            dx = d_ref[rows, :] + rr * (dxh - xh * jnp.mean(dxh * xh, axis=-1, keepdims=True))
            dx_ref[rows, :] = dx
            if post is not None:
                dy, dxy = _post_norm_grad(dx, y_ref[rows, :], g_ref[...])
                dy_ref[rows, :] = dy.astype(BF16)
                dg_ref[...] += _colsum8(dxy)

    ins, in_specs = [], []
    for a, w in pairs:
        ins += [a, w]
        in_specs += [_row_spec(tm, a.shape[1]),
                     pl.BlockSpec((None, D, a.shape[1]), lambda i: (0, 0, 0), pipeline_mode=pl.Buffered(1))]
    ins += [x, dxn, scales]
    in_specs += [_row_spec(tm, D), _row_spec(tm, D), _vec_spec(nb, D)]
    acc_spec = pl.BlockSpec((nb, 8, D), lambda i: (0, 0, 0))
    out_specs = [_row_spec(tm, D), acc_spec, acc_spec]
    out_shape = [jax.ShapeDtypeStruct((S, D), F32), jax.ShapeDtypeStruct((nb, 8, D), F32),
                 jax.ShapeDtypeStruct((nb, 8, D), F32)]
    if post is not None:
        ins += list(post)
        in_specs += [_row_spec(tm, D), _vec_spec(1, D)]
        out_specs += [_row_spec(tm, D), _vec_spec(8, D)]
        out_shape += [jax.ShapeDtypeStruct((S, D), BF16), jax.ShapeDtypeStruct((8, D), F32)]
    return pl.pallas_call(
        body, name=name, grid=(S // tm,), in_specs=in_specs, out_specs=out_specs, out_shape=out_shape,
        compiler_params=_params("arbitrary"),
    )(*ins)


FFN_PAIRS = 2


def _ffn_in_act(h, w, layer, name):
    S, D = h.shape
    F2 = w.shape[2]
    PW = F2 // (2 * FFN_PAIRS)
    tm = _pick(S, 1024, 16)
    sub = _pick(tm, PIECE_ROWS, 16)

    def body(h_ref, w_ref, gu_ref, a_ref):
        def product(r):
            return jnp.dot(h_ref[pl.ds(r * sub, sub), :], w_ref[...], preferred_element_type=F32)

        nxt = product(0)
        for r in range(tm // sub):
            rows = pl.ds(r * sub, sub)
            acc = nxt
            if r + 1 < tm // sub:
                nxt = product(r + 1)
            gu_ref[rows, :] = acc.astype(BF16)
            g = acc[:, :PW]
            a_ref[rows, :] = (g * jax.nn.sigmoid(g) * acc[:, PW:]).astype(BF16)

    return pl.pallas_call(
        body, name=name, grid=(FFN_PAIRS, S // tm),
        in_specs=[pl.BlockSpec((tm, D), lambda p, i: (i, 0)),
                  pl.BlockSpec((None, D, 2 * PW), lambda p, i: (layer, 0, p))],
        out_specs=[pl.BlockSpec((tm, 2 * PW), lambda p, i: (i, p)), pl.BlockSpec((tm, PW), lambda p, i: (i, p))],
        out_shape=[jax.ShapeDtypeStruct((S, F2), BF16), jax.ShapeDtypeStruct((S, F2 // 2), BF16)],
        compiler_params=_params("parallel", "parallel"),
    )(h, w)


def _ffn_bwd(dy, w_out, gu, w_in, x, dxn, scale, post, name):
    S, D = dy.shape
    F2 = gu.shape[1]
    PW = F2 // (2 * FFN_PAIRS)
    tm = _pick(S, 256, 16)

    def body(dy_ref, wo_ref, gu_ref, wi_ref, x_ref, d_ref, sc_ref, y_ref, g_ref,
             dgu_ref, dx_ref, ds_ref, db_ref, dyn_ref, dg_ref):
        @pl.when(pl.program_id(0) == 0)
        def _():
            ds_ref[...] = jnp.zeros_like(ds_ref)
            db_ref[...] = jnp.zeros_like(db_ref)
            dg_ref[...] = jnp.zeros_like(dg_ref)

        def first_product(p):
            return lax.dot_general(dy_ref[...], wo_ref[p * PW:(p + 1) * PW, :], _DIMS["nt"],
                                   preferred_element_type=F32)

        dh = jnp.zeros((tm, D), F32)
        nxt = first_product(0)
        for p in range(FFN_PAIRS):
            cols = slice(2 * p * PW, 2 * (p + 1) * PW)
            da = nxt
            if p + 1 < FFN_PAIRS:
                nxt = first_product(p + 1)
            g = gu_ref[:, 2 * p * PW:(2 * p + 1) * PW].astype(F32)
            u = gu_ref[:, (2 * p + 1) * PW:2 * (p + 1) * PW].astype(F32)
            sg = jax.nn.sigmoid(g)
            dgu_ref[:, 2 * p * PW:(2 * p + 1) * PW] = (da * u * (sg * (1.0 + g * (1.0 - sg)))).astype(BF16)
            dgu_ref[:, (2 * p + 1) * PW:2 * (p + 1) * PW] = (da * (g * sg)).astype(BF16)
            dh = dh + lax.dot_general(dgu_ref[:, cols], wi_ref[:, cols], _DIMS["nt"], preferred_element_type=F32)
        xv = x_ref[...]
        rr = lax.rsqrt(jnp.mean(xv * xv, axis=-1, keepdims=True) + EPS)
        xh = xv * rr
        dxh = dh * sc_ref[...]
        ds_ref[0] += _colsum8(dh * xh)
        db_ref[0] += _colsum8(dh)
        dx = d_ref[...] + rr * (dxh - xh * jnp.mean(dxh * xh, axis=-1, keepdims=True))
        dx_ref[...] = dx
        dyn, dxy = _post_norm_grad(dx, y_ref[...], g_ref[...])
        dyn_ref[...] = dyn.astype(BF16)
        dg_ref[...] += _colsum8(dxy)

    resident = dict(pipeline_mode=pl.Buffered(1))
    acc_spec = pl.BlockSpec((1, 8, D), lambda i: (0, 0, 0))
    return pl.pallas_call(
        body, name=name, grid=(S // tm,),
        in_specs=[_row_spec(tm, D), pl.BlockSpec((None, F2 // 2, D), lambda i: (0, 0, 0), **resident),
                  _row_spec(tm, F2), pl.BlockSpec((None, D, F2), lambda i: (0, 0, 0), **resident),
                  _row_spec(tm, D), _row_spec(tm, D), _vec_spec(1, D), _row_spec(tm, D), _vec_spec(1, D)],
        out_specs=[_row_spec(tm, F2), _row_spec(tm, D), acc_spec, acc_spec, _row_spec(tm, D), _vec_spec(8, D)],
        out_shape=[jax.ShapeDtypeStruct((S, F2), BF16), jax.ShapeDtypeStruct((S, D), F32),
                   jax.ShapeDtypeStruct((1, 8, D), F32), jax.ShapeDtypeStruct((1, 8, D), F32),
                   jax.ShapeDtypeStruct((S, D), BF16), jax.ShapeDtypeStruct((8, D), F32)],
        compiler_params=_params("arbitrary"),
    )(dy, w_out, gu, w_in, x, dxn, scale, *post)


HALO = 16


def _conv_terms(bcx_ref, prev_ref, i, tm, D):
    b = bcx_ref[:, 0:D].astype(F32)
    cg = bcx_ref[:, D:2 * D].astype(F32)
    xin = bcx_ref[:, 2 * D:3 * D].astype(F32)
    z = cg * xin
    zp = prev_ref[:, D:2 * D].astype(F32) * prev_ref[:, 2 * D:3 * D].astype(F32)
    zp = jnp.where(i > 0, zp, 0.0)
    z_ext = jnp.concatenate([zp, z], axis=0)
    z1 = pltpu.roll(z_ext, 1, 0)[HALO:, :]
    z2 = pltpu.roll(z_ext, 2, 0)[HALO:, :]
    return b, cg, xin, z, z1, z2


def _conv_gate(bcx, ck, name):
    S, D3 = bcx.shape
    D = D3 // 3
    tm = _pick(S, 512, 16)
    hb = tm // HALO

    def body(bcx_ref, prev_ref, ck_ref, o_ref):
        i = pl.program_id(0)
        b, _, _, z, z1, z2 = _conv_terms(bcx_ref, prev_ref, i, tm, D)
        conv = ck_ref[0:1, :] * z2 + ck_ref[1:2, :] * z1 + ck_ref[2:3, :] * z
        o_ref[...] = (b * conv).astype(BF16)

    return pl.pallas_call(
        body, name=name, grid=(S // tm,),
        in_specs=[_row_spec(tm, D3),
                  pl.BlockSpec((HALO, D3), lambda i: (jnp.maximum(i * hb - 1, 0), 0)),
                  _vec_spec(8, D)],
        out_specs=_row_spec(tm, D),
        out_shape=jax.ShapeDtypeStruct((S, D), BF16),
        compiler_params=_params("parallel"),
    )(bcx, bcx, ck)


def _conv_gate_bwd(du, bcx, ck, name):
    S, D3 = bcx.shape
    D = D3 // 3
    tm = _pick(S, 512, 16)
    hb = tm // HALO
    nt = S // tm

    def body(du_ref, dun_ref, bcx_ref, prev_ref, next_ref, ck_ref, o_ref, dk_ref):
        i = pl.program_id(0)
        b, cg, xin, z, z1, z2 = _conv_terms(bcx_ref, prev_ref, i, tm, D)
        k0, k1, k2 = ck_ref[0:1, :], ck_ref[1:2, :], ck_ref[2:3, :]
        conv = k0 * z2 + k1 * z1 + k2 * z
        d = du_ref[...].astype(F32)
        dconv = d * b
        dcn = jnp.where(i < nt - 1, dun_ref[...].astype(F32) * next_ref[:, 0:D].astype(F32), 0.0)
        d_ext = jnp.concatenate([dconv, dcn], axis=0)
        d1 = pltpu.roll(d_ext, tm + HALO - 1, 0)[:tm, :]
        d2 = pltpu.roll(d_ext, tm + HALO - 2, 0)[:tm, :]
        dz = k2 * dconv + k1 * d1 + k0 * d2
        o_ref[:, 0:D] = (d * conv).astype(BF16)
        o_ref[:, D:2 * D] = (dz * xin).astype(BF16)
        o_ref[:, 2 * D:3 * D] = (dz * cg).astype(BF16)

        @pl.when(i == 0)
        def _():
            dk_ref[...] = jnp.zeros_like(dk_ref)

        dk_ref[0] += _colsum8(dconv * z2)
        dk_ref[1] += _colsum8(dconv * z1)
        dk_ref[2] += _colsum8(dconv * z)

    last = S // HALO - 1
    return pl.pallas_call(
        body, name=name, grid=(nt,),
        in_specs=[_row_spec(tm, D),
                  pl.BlockSpec((HALO, D), lambda i: (jnp.minimum((i + 1) * hb, last), 0)),
                  _row_spec(tm, D3),
                  pl.BlockSpec((HALO, D3), lambda i: (jnp.maximum(i * hb - 1, 0), 0)),
                  pl.BlockSpec((HALO, D3), lambda i: (jnp.minimum((i + 1) * hb, last), 0)),
                  _vec_spec(8, D)],
        out_specs=[_row_spec(tm, D3), pl.BlockSpec((3, 8, D), lambda i: (0, 0, 0))],
        out_shape=[jax.ShapeDtypeStruct((S, D3), BF16), jax.ShapeDtypeStruct((3, 8, D), F32)],
        compiler_params=_params("arbitrary"),
    )(du, du, bcx, bcx, bcx, ck)


def _rel_onehot():
    a = np.arange(CHUNK)[:, None]
    b = np.arange(CHUNK)[None, :]
    idx = np.stack([np.clip((N_LEFT_CHUNKS - dl) * CHUNK + a - b, -MAX_REL, MAX_REL) + MAX_REL
                    for dl in (6, 7, 8)]).reshape(-1)
    return (jnp.asarray(idx)[:, None] == jnp.arange(N_REL)[None, :]).astype(F32)


def _bias_table(rel_bias, name):
    H = rel_bias.shape[0]
    near = jnp.dot(rel_bias, _rel_onehot().T, precision=lax.Precision.HIGHEST).reshape(H, 3, CHUNK, CHUNK)
    far = jnp.broadcast_to(rel_bias[:, N_REL - 1][:, None, None], (H, CHUNK, CHUNK))

    def body(near_ref, far_ref, o_ref):
        neg = jnp.full((CHUNK, CHUNK), NEG, F32)
        for v in range(N_WIN):
            for ic in range(Q_CHUNKS):
                for jc in range(N_WIN * Q_CHUNKS):
                    dl = jc - ic
                    if dl < 0 or dl > N_LEFT_CHUNKS or jc < (N_WIN - 1 - v) * Q_CHUNKS:
                        blk = neg
                    else:
                        blk = far_ref[...] if dl <= 5 else near_ref[dl - 6]
                    o_ref[v, ic * CHUNK:(ic + 1) * CHUNK, jc * CHUNK:(jc + 1) * CHUNK] = blk

    return pl.pallas_call(
        body, name=name, grid=(H,),
        in_specs=[pl.BlockSpec((None, 3, CHUNK, CHUNK), lambda h: (h, 0, 0, 0)),
                  pl.BlockSpec((None, CHUNK, CHUNK), lambda h: (h, 0, 0))],
        out_specs=pl.BlockSpec((N_WIN, None, BQ, N_WIN * BQ), lambda h: (0, h, 0, 0)),
        out_shape=jax.ShapeDtypeStruct((N_WIN, H, BQ, N_WIN * BQ), F32),
        compiler_params=_params("parallel"),
    )(near, far)


NEAR_FIRST = 6
SLAB_ROWS = 2 * CHUNK
SLAB_COLS = 4 * CHUNK


def _slab(pair):
    c0 = (NEAR_FIRST + 2 * pair) * CHUNK
    return slice(pair * SLAB_ROWS, (pair + 1) * SLAB_ROWS), slice(c0, c0 + SLAB_COLS)


def _bias_table_grad(dslab):
    H = dslab.shape[0]

    def blk(ic, dl):
        pair, r, col = ic // 2, ic % 2, ic + dl - NEAR_FIRST - 2 * (ic // 2)
        return dslab[:, pair, r * CHUNK:(r + 1) * CHUNK, col * CHUNK:(col + 1) * CHUNK]

    by_dl = [sum(blk(ic, dl) for ic in range(Q_CHUNKS)) for dl in (6, 7, 8)]
    near = jnp.stack(by_dl, axis=1).reshape(H, 3 * CHUNK * CHUNK)
    g = jnp.dot(near, _rel_onehot(), precision=lax.Precision.HIGHEST)
    return g.at[:, N_REL - 1].add(-jnp.sum(near, axis=1))


def _attn_specs(nblk, W):
    last = nblk - 1
    q_spec = pl.BlockSpec((BQ, W), lambda g, i: (jnp.minimum(i, last), g))
    kv_specs = [pl.BlockSpec((BQ, 2 * W), functools.partial(
        lambda g, i, w: (jnp.maximum(jnp.minimum(i, last) - (N_WIN - 1) + w, 0), g), w=w)) for w in range(N_WIN)]
    tab_spec = pl.BlockSpec((None, HEADS_PER_STEP, BQ, N_WIN * BQ),
                            lambda g, i: (jnp.minimum(i, N_WIN - 1), g, 0, 0))
    dtab_spec = pl.BlockSpec((HEADS_PER_STEP, Q_CHUNKS // 2, SLAB_ROWS, SLAB_COLS), lambda g, i: (g, 0, 0, 0))
    return q_spec, kv_specs, tab_spec, dtab_spec


def _attn_scores(q_ref, kT, tab_ref, h, dh):
    return jnp.dot(q_ref[:, h * dh:(h + 1) * dh], kT[h * dh:(h + 1) * dh, :], preferred_element_type=F32) + tab_ref[h]


def _attn_fwd(q, kv, tab, name):
    S, D = q.shape
    dh = D // N_HEADS
    W = HEADS_PER_STEP * dh
    assert 2 * W == D, "the kv layout puts one head group's k beside its v: two head groups"
    q_spec, kv_specs, tab_spec, _ = _attn_specs(S // BQ, W)

    def body(q_ref, *rest):
        tab_ref, o_ref = rest[N_WIN], rest[N_WIN + 1]
        kvw = jnp.concatenate([r[...] for r in rest[:N_WIN]], axis=0)
        kT = kvw[:, :W].T
        vw = kvw[:, W:]
        outs = []
        s = _attn_scores(q_ref, kT, tab_ref, 0, dh)
        for h in range(HEADS_PER_STEP):
            s_next = _attn_scores(q_ref, kT, tab_ref, h + 1, dh) if h + 1 < HEADS_PER_STEP else None
            e = jnp.exp(s - jnp.max(s, axis=-1, keepdims=True))
            l = jnp.sum(e, axis=-1, keepdims=True)
            outs.append(jnp.dot(e.astype(BF16), vw[:, h * dh:(h + 1) * dh], preferred_element_type=F32) / l)
            s = s_next
        o_ref[...] = jnp.concatenate(outs, axis=1).astype(BF16)

    return pl.pallas_call(
        body, name=name, grid=(N_HEADS // HEADS_PER_STEP, S // BQ),
        in_specs=[q_spec] + kv_specs + [tab_spec],
        out_specs=q_spec,
        out_shape=jax.ShapeDtypeStruct((S, D), BF16),
        compiler_params=_params("parallel", "parallel"),
    )(q, *([kv] * N_WIN), tab)


def _attn_bwd(q, kv, tab, do, name):
    S, D = q.shape
    dh = D // N_HEADS
    W = HEADS_PER_STEP * dh
    nblk = S // BQ
    q_spec, kv_specs, tab_spec, dtab_spec = _attn_specs(nblk, W)

    def body(q_ref, *rest):
        tab_ref, do_ref, dq_ref, dkv_ref, dtab_ref, ring = rest[N_WIN:]
        i = pl.program_id(1)

        @pl.when(i == 0)
        def _():
            dtab_ref[...] = jnp.zeros_like(dtab_ref)
            ring[...] = jnp.zeros_like(ring)

        @pl.when(i < nblk)
        def _():
            kvw = jnp.concatenate([r[...] for r in rest[:N_WIN]], axis=0)
            kT = kvw[:, :W].T
            vw = kvw[:, W:]
            qT = q_ref[...].T
            dqs, dks, dvs = [], [], []

            s = _attn_scores(q_ref, kT, tab_ref, 0, dh)
            for h in range(HEADS_PER_STEP):
                hd = slice(h * dh, (h + 1) * dh)
                do_h = do_ref[:, hd]
                dp = lax.dot_general(do_h, vw[:, hd], _DIMS["nt"], preferred_element_type=F32)
                e = jnp.exp(s - jnp.max(s, axis=-1, keepdims=True))
                inv_l = 1.0 / jnp.sum(e, axis=-1, keepdims=True)
                if h + 1 < HEADS_PER_STEP:
                    s = _attn_scores(q_ref, kT, tab_ref, h + 1, dh)
                delta = jnp.sum(e * dp, axis=-1, keepdims=True) * inv_l
                ds = e * ((dp - delta) * inv_l)
                for pair in range(Q_CHUNKS // 2):
                    rows, cols = _slab(pair)
                    dtab_ref[h, pair] += ds[rows, cols]
                dsb = ds.astype(BF16)
                dqs.append(lax.dot_general(kT[hd, :], dsb, _DIMS["nt"], preferred_element_type=F32) * (dh ** -0.5))
                dks.append(jnp.dot(qT[hd, :], dsb, preferred_element_type=F32))
                do_s = (do_h.astype(F32) * inv_l).astype(BF16)
                dvs.append(jnp.dot(do_s.T, e.astype(BF16), preferred_element_type=F32))
            dq_ref[...] = jnp.concatenate(dqs, axis=0).T.astype(BF16)
            dkv = jnp.concatenate(dks + dvs, axis=0).T
            for w in range(N_WIN):
                slot = lax.rem(i + 1 + w, N_WIN)
                part = dkv[w * BQ:(w + 1) * BQ, :]
                if w == N_WIN - 1:
                    ring[slot] = part
                else:
                    ring[slot] += part

        dkv_ref[...] = ring[lax.rem(i + 1, N_WIN)].astype(BF16)

    done_spec = pl.BlockSpec((BQ, 2 * W), lambda g, i: (jnp.maximum(i - (N_WIN - 1), 0), g))
    return pl.pallas_call(
        body, name=name, grid=(N_HEADS // HEADS_PER_STEP, nblk + N_WIN - 1),
        in_specs=[q_spec] + kv_specs + [tab_spec, q_spec],
        out_specs=[q_spec, done_spec, dtab_spec],
        out_shape=[jax.ShapeDtypeStruct((S, D), BF16), jax.ShapeDtypeStruct((S, 2 * D), BF16),
                   jax.ShapeDtypeStruct((N_HEADS, Q_CHUNKS // 2, SLAB_ROWS, SLAB_COLS), F32)],
        scratch_shapes=[pltpu.VMEM((N_WIN, BQ, 2 * W), F32)],
        compiler_params=_params("parallel", "arbitrary"),
    )(q, *([kv] * N_WIN), tab, do)


def _adamw(w, g, m, v, name):
    shape = w.shape
    C = shape[-1]
    R = int(np.prod(shape[:-1])) if len(shape) > 1 else 1
    whole = len(shape) >= 2 and R * C <= SMALL_TENSOR_ELEMS
    if whole:
        w2, g2, m2, v2 = w, g, m, v
    else:
        w2, g2, m2, v2 = (t.reshape(R, C) for t in (w, g, m, v))
    tr = _pick(R, max(8, (512 * 1024) // C // 8 * 8), 8)

    def body(w_ref, g_ref, m_ref, v_ref, d_ref, nm_ref, nv_ref):
        gv = g_ref[...]
        nm = ADAM_B1 * m_ref[...] + (1.0 - ADAM_B1) * gv
        nv = ADAM_B2 * v_ref[...] + (1.0 - ADAM_B2) * jnp.square(gv)
        m_hat = nm / (1.0 - ADAM_B1 ** ADAM_STEP)
        v_hat = nv / (1.0 - ADAM_B2 ** ADAM_STEP)
        d_ref[...] = -ADAM_LR * (m_hat / (jnp.sqrt(v_hat) + ADAM_EPS) + ADAM_WD * w_ref[...])
        nm_ref[...] = nm
        nv_ref[...] = nv

    if whole:
        spec, grid = pl.BlockSpec(shape, lambda i: (0,) * len(shape)), (1,)
    else:
        spec, grid = pl.BlockSpec((tr, C), lambda i: (i, 0)), (R // tr,)
    outs = pl.pallas_call(
        body, name=name, grid=grid,
        in_specs=[spec] * 4, out_specs=[spec] * 3,
        out_shape=[jax.ShapeDtypeStruct(w2.shape, F32)] * 3,
        compiler_params=_params("parallel"),
    )(w2, g2, m2, v2)
    return tuple(o.reshape(shape) for o in outs)


def _sum_rows(a, name):
    n, L = a.shape

    def body(a_ref, o_ref):
        acc = a_ref[0:1, :]
        for r in range(1, n):
            acc = acc + a_ref[r:r + 1, :]
        o_ref[...] = acc

    return pl.pallas_call(
        body, name=name, grid=(1,),
        in_specs=[pl.BlockSpec((n, L), lambda i: (0, 0))],
        out_specs=pl.BlockSpec((1, L), lambda i: (0, 0)),
        out_shape=jax.ShapeDtypeStruct((1, L), F32),
        compiler_params=_params("arbitrary"),
    )(a)


def _scalar_call(body, name, scalar, grid, in_specs, out_spec, out_shape, args):
    return pl.pallas_call(
        body, name=name,
        grid_spec=pltpu.PrefetchScalarGridSpec(num_scalar_prefetch=1, grid=grid, in_specs=in_specs,
                                               out_specs=out_spec),
        out_shape=out_shape, compiler_params=_params("parallel"),
    )(jnp.reshape(scalar, (-1,)).astype(jnp.int32), *args)


def _pair_sum(view, got, c, name):
    nb, _, rh, cols = view.shape
    tr = _pick(rh, max(16, (1 << 20) // cols // 16 * 16), 16)
    bpr = rh // tr

    def body(s_ref, a_ref, b_ref, o_ref):
        o_ref[...] = (a_ref[...].astype(F32) + b_ref[...].astype(F32)).astype(BF16)

    spec = pl.BlockSpec((tr, cols), lambda i, s: (i, 0))
    mine = pl.BlockSpec((tr, cols), lambda i, s: ((2 * (i // bpr) + s[0]) * bpr + i % bpr, 0))
    return _scalar_call(body, name, c, (nb * bpr,), [mine, spec], spec,
                        jax.ShapeDtypeStruct((nb * rh, cols), BF16),
                        (view.reshape(nb * 2 * rh, cols), got.reshape(nb * rh, cols)))


STACKED_LAYERS = 2


def _owner_sum(pair, recv, me, c, it, name, layer=None, into=None):
    _, rh, bc = recv.shape
    tr = _pick(rh, max(16, (1 << 19) // bc // 16 * 16), 16)
    bpr = rh // tr

    def body(s_ref, a_ref, r0, r1, r2, *rest):
        rest[-1][...] = ((a_ref[...].astype(F32) + r0[...].astype(F32)) + r1[...].astype(F32)) + r2[...].astype(F32)

    if it.kind == "col":
        own = pl.BlockSpec((tr, bc), lambda i, s: (i, s[0]))
    else:
        own = pl.BlockSpec((tr, bc), lambda i, s: (s[0] * bpr + i, 0))
    slots = [pl.BlockSpec((None, tr, bc), functools.partial(lambda i, s, k: (k, i, 0), k=k)) for k in range(3)]
    in_specs, args, aliases = [own] + slots, [pair, recv, recv, recv], {}
    if layer is None:
        out_spec = pl.BlockSpec((tr, bc), lambda i, s: (s[1] * bpr + i, 0))
        out_shape = jax.ShapeDtypeStruct((2 * rh, bc), F32)
    else:
        out_spec = pl.BlockSpec((None, tr, bc), lambda i, s: (layer, s[1] * bpr + i, 0))
        out_shape = jax.ShapeDtypeStruct((STACKED_LAYERS, 2 * rh, bc), F32)
        if into is not None:
            in_specs.append(pl.BlockSpec(memory_space=pl.ANY))
            args.append(into)
            aliases = {len(args): 0}
    return pl.pallas_call(
        body, name=name,
        grid_spec=pltpu.PrefetchScalarGridSpec(num_scalar_prefetch=1, grid=(bpr,), in_specs=in_specs,
                                               out_specs=out_spec),
        out_shape=out_shape, input_output_aliases=aliases, compiler_params=_params("parallel"),
    )(jnp.stack([it.pos(me), c]).astype(jnp.int32), *args)


def _place():
    x, y, c = lax.axis_index("x"), lax.axis_index("y"), lax.axis_index("c")
    chips = [(1 - x, y), (x, 1 - y), (1 - x, 1 - y)]
    return x, y, c, chips


def _chip_index(px, py):
    return 2 * px + py


def _all_gather_small(x_shard, name):
    m_per, n = x_shard.shape

    def body(x_ref, out_ref, send_sems, recv_sems, local_sem):
        x, y, c, chips = _place()
        me, sibling = (x, y, c), (x, y, 1 - c)

        def rows(px, py, pc):
            return out_ref.at[pl.ds((4 * px + 2 * py + pc) * m_per, m_per), :]

        def copy(k, block, to, src=None):
            return pltpu.make_async_remote_copy(
                src_ref=rows(*block) if src is None else src, dst_ref=rows(*block),
                send_sem=send_sems.at[k], recv_sem=recv_sems.at[k], device_id=to, device_id_type=MESH)

        mine = pltpu.make_async_copy(x_ref, rows(*me), local_sem)
        mine.start()
        first = [copy(0, me, sibling, src=x_ref)]
        first += [copy(1 + j, me, (*chip, c), src=x_ref) for j, chip in enumerate(chips)]
        for cp in first:
            cp.start()
        passed = [copy(4 + j, (*chip, c), sibling) for j, chip in enumerate(chips)]
        for j, chip in enumerate(chips):
            copy(1 + j, (*chip, c), me).wait_recv()
            passed[j].start()
        copy(0, sibling, me).wait_recv()
        for j, chip in enumerate(chips):
            copy(4 + j, (*chip, 1 - c), me).wait_recv()
        for cp in first + passed:
            cp.wait_send()
        mine.wait()

    return pl.pallas_call(
        body, name=name,
        out_shape=jax.ShapeDtypeStruct((N_DEV * m_per, n), x_shard.dtype),
        in_specs=[pl.BlockSpec(memory_space=pltpu.VMEM)],
        out_specs=pl.BlockSpec(memory_space=pltpu.VMEM),
        scratch_shapes=[pltpu.SemaphoreType.DMA((7,)), pltpu.SemaphoreType.DMA((7,)), pltpu.SemaphoreType.DMA],
    )(x_shard)


def _gather_flat(vec, name):
    L = vec.shape[0]
    Lp = -(-L // 1024) * 1024
    g = _all_gather_small(jnp.pad(vec, (0, Lp - L)).reshape(8, Lp // 8), name)
    return g.reshape(N_DEV, Lp)[:, :L]


class _Item:
    def __init__(self, kind, rows, cols, arg, layer, swap=False):
        self.kind, self.rows, self.cols, self.arg, self.layer, self.swap = kind, rows, cols, arg, layer, swap

    def ref(self, refs):
        return refs[self.arg].at[self.layer]

    def pos(self, j):
        return 2 * (j % 2) + j // 2 if self.swap else j


def _block(ref, it, j, half):
    if it.kind == "col":
        ns = it.cols // N_CHIP
        return ref.at[pl.ds(half * (it.rows // 2), it.rows // 2), pl.ds(it.pos(j) * ns, ns)]
    rs = it.rows // N_CHIP
    return ref.at[pl.ds(j * rs + half * (rs // 2), rs // 2), :]


def _cast_place(w, layer, kind, pos, after, name):
    _, r, n = w.shape
    tr = _pick(r, max(16, (1 << 20) // n // 16 * 16), 16)
    bpr = r // tr

    def body(s_ref, w_ref, after_ref, o_ref):
        o_ref[...] = w_ref[...].astype(BF16)

    if kind == "col":
        full, out_idx = (1, r, N_CHIP * n), (lambda i, s: (0, i, s[0]))
    else:
        full, out_idx = (1, N_CHIP * r, n), (lambda i, s: (0, s[0] * bpr + i, 0))
    return pl.pallas_call(
        body, name=name,
        grid_spec=pltpu.PrefetchScalarGridSpec(
            num_scalar_prefetch=1, grid=(bpr,),
            in_specs=[pl.BlockSpec((None, tr, n), lambda i, s: (layer, i, 0)), pl.BlockSpec(memory_space=pl.ANY)],
            out_specs=pl.BlockSpec((None, tr, n), out_idx)),
        out_shape=jax.ShapeDtypeStruct(full, BF16),
        compiler_params=_params("parallel"),
    )(jnp.reshape(pos, (1,)).astype(jnp.int32), w, after)


HBM_SPEC = pl.BlockSpec(memory_space=pltpu.HBM)
SEM_SPEC = pl.BlockSpec(memory_space=pltpu.SEMAPHORE)
ANY_SPEC = pl.BlockSpec(memory_space=pl.ANY)
SPLIT_PARAMS = dict(has_side_effects=pltpu.SideEffectType.DATAFLOW_SIDE_EFFECTING)


def _in_hbm(a):
    return pltpu.with_memory_space_constraint(a, pltpu.HBM)


def _split_start(copies_of, bufs, n_sem, after, name):
    n = len(bufs)

    def body(*refs):
        ins, send, recv, token = refs[:n], refs[n + 1], refs[n + 2], refs[2 * n + 3]
        for cp in copies_of(ins, send, recv, False)[0]:
            cp.start()
        token[...] = jnp.zeros_like(token)

    outs = pl.pallas_call(
        body, name=name,
        out_shape=(pltpu.SemaphoreType.DMA(n_sem), pltpu.SemaphoreType.DMA(n_sem),
                   *[pltpu.HBM(b.shape, b.dtype) for b in bufs], jax.ShapeDtypeStruct((8, 128), F32)),
        in_specs=[HBM_SPEC] * n + [ANY_SPEC],
        out_specs=(SEM_SPEC, SEM_SPEC, *[HBM_SPEC] * n, pl.BlockSpec(memory_space=pltpu.VMEM)),
        input_output_aliases={t: 2 + t for t in range(n)},
        compiler_params=pltpu.CompilerParams(**SPLIT_PARAMS),
    )(*[_in_hbm(b) for b in bufs], after)
    return outs[0], outs[1], list(outs[2:2 + n]), outs[2 + n]


def _split_wait(copies_of, send, recv, bufs, after, name):
    n = len(bufs)

    def body(*refs):
        ins, send_ref, recv_ref = refs[:n], refs[n], refs[n + 1]
        sends, arrivals = copies_of(ins, send_ref, recv_ref, True)
        for cp in sends:
            cp.wait_send()
        for cp in arrivals:
            cp.wait_recv()

    return pl.pallas_call(
        body, name=name,
        out_shape=[pltpu.HBM(b.shape, b.dtype) for b in bufs],
        in_specs=[HBM_SPEC] * n + [SEM_SPEC, SEM_SPEC, ANY_SPEC],
        out_specs=[HBM_SPEC] * n,
        input_output_aliases={t: t for t in range(n)},
        compiler_params=pltpu.CompilerParams(**SPLIT_PARAMS),
    )(*bufs, send, recv, after)


def _gather_copies(items):
    def copies_of(refs, send, recv, with_arrivals):
        x, y, c, chips = _place()
        me = _chip_index(x, y)
        sends, arrivals = [], []
        for t, it in enumerate(items):
            for k, chip in enumerate(chips):
                for core in range(2):
                    mine = _block(it.ref(refs), it, me, c)
                    sends.append(pltpu.make_async_remote_copy(
                        src_ref=mine, dst_ref=mine, send_sem=send.at[6 * t + 2 * k + core],
                        recv_sem=recv.at[6 * t + 2 * k + c], device_id=(*chip, core), device_id_type=MESH))
                    if with_arrivals:
                        landed = _block(it.ref(refs), it, _chip_index(*chip), core)
                        arrivals.append(pltpu.make_async_remote_copy(
                            src_ref=landed, dst_ref=landed, send_sem=send.at[6 * t + 2 * k + core],
                            recv_sem=recv.at[6 * t + 2 * k + core], device_id=(*chip, core), device_id_type=MESH))
        return sends, arrivals

    return copies_of


def _owner_copies(items):
    n = len(items)

    def blk(ref, it, j):
        if it.kind == "col":
            ns = it.cols // N_CHIP
            return ref.at[:, pl.ds(it.pos(j) * ns, ns)]
        return ref.at[j]

    def copies_of(refs, send, recv, with_arrivals):
        x, y, c, chips = _place()
        sends, arrivals = [], []
        for t, it in enumerate(items):
            for k, chip in enumerate(chips):
                slot = refs[n + t].at[k]
                sends.append(pltpu.make_async_remote_copy(
                    src_ref=blk(refs[t], it, _chip_index(*chip)), dst_ref=slot, send_sem=send.at[3 * t + k],
                    recv_sem=recv.at[3 * t + k], device_id=(*chip, c), device_id_type=MESH))
                if with_arrivals:
                    arrivals.append(pltpu.make_async_remote_copy(
                        src_ref=slot, dst_ref=slot, send_sem=send.at[3 * t + k], recv_sem=recv.at[3 * t + k],
                        device_id=(*chip, c), device_id_type=MESH))
        return sends, arrivals

    return copies_of


def _owner_slot_shape(it):
    if it.kind == "col":
        return (3, it.rows // 2, it.cols // N_CHIP)
    return (3, it.rows // (2 * N_CHIP), it.cols)


def _pair_view(g, it):
    if it.kind == "col":
        return g.reshape(1, 2, it.rows // 2, it.cols)
    return g.reshape(N_CHIP, 2, it.rows // (2 * N_CHIP), it.cols)


def _pair_copies(n):
    def copies_of(refs, send, recv, with_arrivals):
        x, y, c, _ = _place()
        sends, arrivals = [], []
        for t in range(n):
            land = refs[n + t]
            sends.append(pltpu.make_async_remote_copy(
                src_ref=refs[t].at[:, pl.ds(1 - c, 1)], dst_ref=land, send_sem=send.at[t], recv_sem=recv.at[t],
                device_id=(x, y, 1 - c), device_id_type=MESH))
            if with_arrivals:
                arrivals.append(pltpu.make_async_remote_copy(
                    src_ref=land, dst_ref=land, send_sem=send.at[t], recv_sem=recv.at[t],
                    device_id=(x, y, 1 - c), device_id_type=MESH))
        return sends, arrivals

    return copies_of


def _half_copies(n):
    def half(ref, which):
        r2 = ref.shape[-2] // 2
        rows = pl.ds(which * r2, r2)
        return ref.at[rows, :] if len(ref.shape) == 2 else ref.at[:, rows, :]

    def copies_of(refs, send, recv, with_arrivals):
        x, y, c, _ = _place()
        sends, arrivals = [], []
        for t in range(n):
            mine = half(refs[t], c)
            sends.append(pltpu.make_async_remote_copy(
                src_ref=mine, dst_ref=mine, send_sem=send.at[t], recv_sem=recv.at[t],
                device_id=(x, y, 1 - c), device_id_type=MESH))
            if with_arrivals:
                theirs = half(refs[t], 1 - c)
                arrivals.append(pltpu.make_async_remote_copy(
                    src_ref=theirs, dst_ref=theirs, send_sem=send.at[t], recv_sem=recv.at[t],
                    device_id=(x, y, 1 - c), device_id_type=MESH))
        return sends, arrivals

    return copies_of


class _Reduction:
    pass


def _pair_start(grads, items, after, tag, names, layer=None):
    n = len(items)
    views = [_pair_view(g, it) for g, it in zip(grads, items)]
    lands = [lax.empty((v.shape[0], 1) + v.shape[2:], v.dtype) for v in views]
    r = _Reduction()
    r.items, r.tag, r.names, r.layer = items, tag, names, layer
    r.send, r.recv, r.bufs, r.token = _split_start(_pair_copies(n), views + lands, (n,), after, f"rs_pair_start_{tag}")
    return r


def _owner_start(r, after):
    x, y, c, _ = _place()
    n = len(r.items)
    bufs = _split_wait(_pair_copies(n), r.send, r.recv, r.bufs, after, f"rs_pair_wait_{r.tag}")
    pairs = [_pair_sum(bufs[t], bufs[n + t], c, f"rs_pair_sum_{r.tag}_{t}") for t in range(n)]
    shaped = [p if it.kind == "col" else p.reshape(N_CHIP, p.shape[0] // N_CHIP, p.shape[1])
              for p, it in zip(pairs, r.items)]
    lands = [lax.empty(_owner_slot_shape(it), BF16) for it in r.items]
    r.send, r.recv, r.bufs, r.token = _split_start(
        _owner_copies(r.items), shaped + lands, (3 * n,), r.token, f"rs_owner_start_{r.tag}")
    return r


def _reduce_finish(groups, after):
    x, y, c, _ = _place()
    me = _chip_index(x, y)
    halves = {}
    for r in groups:
        n = len(r.items)
        bufs = _split_wait(_owner_copies(r.items), r.send, r.recv, r.bufs, after, f"rs_owner_wait_{r.tag}")
        for t, (it, nm) in enumerate(zip(r.items, r.names)):
            pair = bufs[t].reshape(-1, bufs[t].shape[-1])
            halves[nm] = _owner_sum(pair, bufs[n + t], me, c, it, f"rs_owner_sum_{r.tag}_{t}",
                                    layer=r.layer, into=halves.get(nm))
    n = len(halves)
    return list(halves), _split_start(_half_copies(n), list(halves.values()), (n,), after, "rs_half_start")


def _silu(v):
    return v * jax.nn.sigmoid(v)


def _sum8(p):
    return jnp.sum(p, axis=-2)


def kernel(x, c, mod_w, mod_b, norm_g, ffn_w_in, ffn_w_out, conv_w_in, conv_k, conv_w_out, kv_mod_w, kv_mod_b, kv_norm_g, w_kv, attn_w_q, attn_w_o, rel_bias, loss_target, m_mod_w, m_mod_b, m_norm_g, m_ffn_w_in, m_ffn_w_out, m_conv_w_in, m_conv_k, m_conv_w_out, m_kv_mod_w, m_kv_mod_b, m_kv_norm_g, m_w_kv, m_attn_w_q, m_attn_w_o, m_rel_bias, v_mod_w, v_mod_b, v_norm_g, v_ffn_w_in, v_ffn_w_out, v_conv_w_in, v_conv_k, v_conv_w_out, v_kv_mod_w, v_kv_mod_b, v_kv_norm_g, v_w_kv, v_attn_w_q, v_attn_w_o, v_rel_bias):
    xi, yi, ci = lax.axis_index("x"), lax.axis_index("y"), lax.axis_index("c")
    chip = 2 * xi + yi
    dev = 2 * chip + ci
    _, S, D = x.shape
    F = ffn_w_out.shape[1] * N_CHIP
    x0 = x.reshape(S, D)
    target = loss_target.reshape(S, D)
    n_mod = mod_w.shape[2]
    n_kvm = kv_mod_w.shape[1]
    dsh = D // N_CHIP
    TF = F // 2

    c_all = _all_gather_small(c.reshape(8, D // 8), "ag_c").reshape(N_DEV, D)
    sc16 = jnp.pad(_silu(c_all), ((0, 8), (0, 0)))
    part = [_mm(sc16, mod_w, "nn", F32, f"mod_fwd_{l}", b_layer=l)[:8] for l in range(2)]
    part.append(_mm(sc16, kv_mod_w, "nn", F32, "mod_fwd_kv")[:8])
    fwd_vec = jnp.concatenate([p.reshape(-1) for p in part] + [norm_g.reshape(-1), conv_k.reshape(-1)])
    fwd_all = _gather_flat(fwd_vec, "ag_fwd_small")[0::2]
    o = 0
    mods = []
    for n in (n_mod, n_mod, n_kvm):
        blk = fwd_all[:, o:o + 8 * n].reshape(N_CHIP, 8, n)
        mods.append(lax.dynamic_index_in_dim(blk, dev, axis=1, keepdims=False).reshape(N_CHIP * n))
        o += 8 * n
    ng = fwd_all[:, o:o + 8 * dsh].reshape(N_CHIP, 2, 4, dsh).transpose(1, 2, 0, 3).reshape(2, 4, D)
    o += 8 * dsh
    ck = fwd_all[:, o:o + 3 * dsh].reshape(N_CHIP, 3, dsh).transpose(1, 0, 2).reshape(3, D)
    ck8 = jnp.pad(ck, ((0, 5), (0, 0)))
    mod = [mods[l] + mod_b[l] for l in range(2)]
    sh1, sc1, g1, sh2, sc2, g2 = zip(*[jnp.split(m, 6) for m in mod])
    kv_sh, kv_sc = jnp.split(mods[2] + kv_mod_b, 2)
    row = lambda v: v.reshape(1, D)

    it_conv = [_Item("col", D, 3 * D, 0, 0), _Item("row", D, D, 1, 0)]
    it_ffn = [_Item("col", D, 2 * F, 0, 0, swap=True), _Item("row", F, D, 1, 0)]
    it_attn = [_Item("col", D, 2 * D, 0, 0, swap=True), _Item("row", D, D, 1, 0), _Item("row", D, D, 2, 0)]

    def placed(w, layer, it, nm, after=fwd_all):
        return _cast_place(w, layer, it.kind, it.pos(chip), after, f"place_{nm}")

    flying = {}

    def start(tag, its, bufs, after):
        send, recv, bufs, tok = _split_start(_gather_copies(its), bufs, (6 * len(its),), after, f"ag_start_{tag}")
        flying[tag] = (its, send, recv, bufs)
        return tok

    def arrived(tag, after):
        its, send, recv, bufs = flying[tag]
        return _split_wait(_gather_copies(its), send, recv, bufs, after, f"ag_wait_{tag}")

    one = lambda it: [_Item(it.kind, it.rows, it.cols, 0, 0, it.swap)]
    tok = start("conv_in", one(it_conv[0]), [placed(conv_w_in, 0, it_conv[0], "conv_w_in")], fwd_all)
    tok = start("conv_out", one(it_conv[1]), [placed(conv_w_out, 0, it_conv[1], "conv_w_out", tok)], tok)
    tok = start("ffn0_in", one(it_ffn[0]), [placed(ffn_w_in, 0, it_ffn[0], "ffn_w_in0", tok)], tok)
    tok = start("ffn0_out", one(it_ffn[1]), [placed(ffn_w_out, 0, it_ffn[1], "ffn_w_out0", tok)], tok)
    tok = start("attn", it_attn, [placed(w_kv[None], 0, it_attn[0], "w_kv", tok),
                                  placed(attn_w_q, 0, it_attn[1], "attn_w_q", tok),
                                  placed(attn_w_o, 0, it_attn[2], "attn_w_o", tok)], tok)
    token = start("ffn1", it_ffn, [placed(ffn_w_in, 1, it_ffn[0], "ffn_w_in1", tok),
                                   placed(ffn_w_out, 1, it_ffn[1], "ffn_w_out1", tok)], tok)

    a1 = row(ng[0, 0] * (1.0 + sc1[0])) + token[0, 0]
    (h1,) = _norm_mod(x0, a1, row(sh1[0]), "l0_norm1")
    tab = _bias_table(rel_bias[0], "l1_bias_table")
    h1, tab = lax.optimization_barrier((h1, tab))
    (W_cin,) = arrived("conv_in", h1)
    bcx = _mm(h1, W_cin, "nn", BF16, "l0_conv_in", b_layer=0, tm=512, tn=3 * D)
    ug = _conv_gate(bcx, ck8, "l0_conv_gate")
    gt1 = row(g1[0] * ng[0, 1])
    a2 = row(ng[0, 2] * (1.0 + sc2[0]))
    (W_cout,) = arrived("conv_out", ug)
    y1, x1, h2 = _mm_post(ug, W_cout, x0, gt1, "l0_conv_out", scales=a2, shifts=row(sh2[0]))
    (W_fin0,) = arrived("ffn0_in", h2)
    gu0, act0 = _ffn_in_act(h2, W_fin0, 0, "l0_ffn_in")
    (W_fout0,) = arrived("ffn0_out", act0)
    gt2 = row(g2[0] * ng[0, 3])
    a3 = ng[1, 0] * (1.0 + sc1[1])
    akv = kv_norm_g * (1.0 + kv_sc)
    y2, x2, h3, hkv = _mm_post(act0, W_fout0, x1, gt2, "l0_ffn_out",
                               scales=jnp.stack([a3, akv]), shifts=jnp.stack([sh1[1], kv_sh]))
    W_kv, W_q, W_o = arrived("attn", hkv)
    kvp = _mm(hkv, W_kv, "nn", BF16, "l1_kv", b_layer=0, tm=512, tn=2 * D)
    att_scale = (D // N_HEADS) ** -0.5
    assert math.log2(att_scale) % 1 == 0, "scaling q before its bf16 cast is exact only for a power of two"
    qp = _mm(h3, W_q, "nn", BF16, "l1_q", b_layer=0, scale=att_scale)
    oh = _attn_fwd(qp, kvp, tab, "l1_attn")
    gt3 = row(g1[1] * ng[1, 1])
    a4 = row(ng[1, 2] * (1.0 + sc2[1]))
    y3, x3, h4 = _mm_post(oh, W_o, x2, gt3, "l1_attn_out", scales=a4, shifts=row(sh2[1]))
    W_fin1, W_fout1 = arrived("ffn1", h4)
    gu1, act1 = _ffn_in_act(h4, W_fin1, 0, "l1_ffn_in")
    gt4 = row(g2[1] * ng[1, 3])
    dx4, sq, dy4, dgt4 = _mm_post(act1, W_fout1, x3, gt4, "l1_ffn_out", target=target)
    loss_part = 0.5 * jnp.sum(sq) / D

    def ffn_bwd(dy, dxn, xin_, h, gu, act, a, w_in, w_out, post, tag):
        dgu, dx, ds, db, dyn, dgt = _ffn_bwd(dy, w_out, gu, w_in, xin_, dxn, a, post, f"{tag}_ffn_bwd")
        g_fout = _mm(act, dy, "tn", BF16, f"{tag}_ffn_out_dw", tm=TF)
        g_fin = _mm(h, dgu, "tn", BF16, f"{tag}_ffn_in_dw", tn=TF)
        return dx, ds, db, dyn, dgt, g_fin, g_fout

    dx3, ds4, db4, dy3, dgt3, G_fin1, G_fout1 = ffn_bwd(dy4, dx4, x3, h4, gu1, act1, a4, W_fin1, W_fout1,
                                                        (y3, gt3), "l1")
    red = [_pair_start([G_fin1, G_fout1], it_ffn, token, "ffn1", ["ffn_w_in", "ffn_w_out"], layer=1)]
    doh = _mm(dy3, W_o, "nt", BF16, "l1_attn_out_dx", b_layer=0, after=red[0].token)
    G_o = _mm(oh, dy3, "tn", BF16, "l1_attn_out_dw")
    _owner_start(red[0], G_o)
    dq, dkv, dtab = _attn_bwd(qp, kvp, tab, doh, "l1_attn_bwd")
    d_rel = _bias_table_grad(dtab)
    G_q = _mm(h3, dq, "tn", BF16, "l1_q_dw")
    G_kv = _mm(hkv, dkv, "tn", BF16, "l1_kv_dw")
    red.append(_pair_start([G_kv, G_q, G_o], it_attn, red[-1].token, "attn", ["w_kv", "attn_w_q", "attn_w_o"]))
    dx2, ds3, db3, dy2, dgt2 = _mm_pre_bwd([(dq, W_q), (dkv, W_kv)], x2, dx3,
                                           jnp.stack([a3, akv]) + red[1].token[0, 0], "l1_qkv_dx", post=(y2, gt2))
    _owner_start(red[1], dx2)

    dx1, ds2, db2, dy1, dgt1, G_fin0, G_fout0 = ffn_bwd(dy2, dx2, x1, h2, gu0, act0, a2, W_fin0, W_fout0,
                                                        (y1, gt1), "l0")
    red.append(_pair_start([G_fin0, G_fout0], it_ffn, red[-1].token, "ffn0", ["ffn_w_in", "ffn_w_out"], layer=0))
    dug = _mm(dy1, W_cout, "nt", BF16, "l0_conv_out_dx", b_layer=0, after=red[2].token)
    G_cout = _mm(ug, dy1, "tn", BF16, "l0_conv_out_dw")
    _owner_start(red[2], G_cout)
    dbcx, dck = _conv_gate_bwd(dug, bcx, ck8, "l0_conv_gate_bwd")
    G_cin = _mm(h1, dbcx, "tn", BF16, "l0_conv_in_dw")
    red.append(_pair_start([G_cin, G_cout], it_conv, red[-1].token, "conv", ["conv_w_in", "conv_w_out"]))
    dx0, ds1, db1 = _mm_pre_bwd([(dbcx, W_cin)], x0, dx1, a1 + red[3].token[0, 0], "l0_conv_in_dx")
    ds1, db1 = _sum8(ds1)[0], _sum8(db1)[0]
    da2, db2 = _sum8(ds2)[0], _sum8(db2)[0]
    ds3, db3 = _sum8(ds3), _sum8(db3)
    da4, db4 = _sum8(ds4)[0], _sum8(db4)[0]
    dgt1, dgt2, dgt3, dgt4 = _sum8(dgt1), _sum8(dgt2), _sum8(dgt3), _sum8(dgt4)

    def dmod_of(l, ds_a, db_a, dgt_a, ds_b, db_b, dgt_b):
        return jnp.concatenate([db_a, ds_a * ng[l, 0], dgt_a * ng[l, 1], db_b, ds_b * ng[l, 2], dgt_b * ng[l, 3]])

    dmod0 = dmod_of(0, ds1, db1, dgt1, da2, db2, dgt2)
    dmod1 = dmod_of(1, ds3[0], db3[0], dgt3, da4, db4, dgt4)
    dkvmod = jnp.concatenate([db3[1], ds3[1] * kv_norm_g])
    dng = jnp.stack([
        jnp.stack([ds1 * (1.0 + sc1[0]), dgt1 * g1[0], da2 * (1.0 + sc2[0]), dgt2 * g2[0]]),
        jnp.stack([ds3[0] * (1.0 + sc1[1]), dgt3 * g1[1], da4 * (1.0 + sc2[1]), dgt4 * g2[1]])])
    dkvng = ds3[1] * (1.0 + kv_sc)
    small = [dmod0, dmod1, dkvmod, dng.reshape(-1), dkvng, _sum8(dck).reshape(-1), d_rel.reshape(-1),
             loss_part.reshape(1)]
    sizes = [int(s.shape[0]) for s in small]
    offs = np.concatenate([[0], np.cumsum(sizes)])
    bwd_all = _gather_flat(jnp.concatenate(small), "ag_bwd_small")
    _owner_start(red[3], bwd_all)
    Lb = bwd_all.shape[1]
    Lp = -(-Lb // 128) * 128
    tot = _sum_rows(jnp.pad(bwd_all, ((0, 0), (0, Lp - Lb))), "sum_small")[0]
    seg = lambda i: tot[offs[i]:offs[i + 1]]
    g_mod_b = jnp.stack([seg(0), seg(1)])
    g_kv_mod_b = seg(2)
    g_norm_g = lax.dynamic_slice_in_dim(seg(3).reshape(2, 4, D), chip * dsh, dsh, axis=2)
    g_kv_norm_g = seg(4)
    g_conv_k = lax.dynamic_slice_in_dim(seg(5).reshape(1, 3, D), chip * dsh, dsh, axis=2)
    g_rel_bias = seg(6).reshape(rel_bias.shape)
    loss = seg(7)[0]

    def dmod_w(i, n, name):
        rows_ = lax.dynamic_slice_in_dim(bwd_all[:, offs[i]:offs[i + 1]], chip * n, n, axis=1)
        return _mm(sc16, jnp.pad(rows_, ((0, 8), (0, 0))), "tn", F32, name)

    g_mod_w = jnp.stack([dmod_w(0, n_mod, "mod_bwd_0"), dmod_w(1, n_mod, "mod_bwd_1")])
    g_kv_mod_w = dmod_w(2, n_kvm, "mod_bwd_kv")

    grads = {
        "mod_w": g_mod_w, "mod_b": g_mod_b, "norm_g": g_norm_g, "conv_k": g_conv_k,
        "kv_mod_w": g_kv_mod_w, "kv_mod_b": g_kv_mod_b, "kv_norm_g": g_kv_norm_g, "rel_bias": g_rel_bias,
    }
    weights = dict(mod_w=mod_w, mod_b=mod_b, norm_g=norm_g, ffn_w_in=ffn_w_in, ffn_w_out=ffn_w_out,
                   conv_w_in=conv_w_in, conv_k=conv_k, conv_w_out=conv_w_out, kv_mod_w=kv_mod_w,
                   kv_mod_b=kv_mod_b, kv_norm_g=kv_norm_g, w_kv=w_kv, attn_w_q=attn_w_q, attn_w_o=attn_w_o,
                   rel_bias=rel_bias)
    m_in = dict(mod_w=m_mod_w, mod_b=m_mod_b, norm_g=m_norm_g, ffn_w_in=m_ffn_w_in, ffn_w_out=m_ffn_w_out,
                conv_w_in=m_conv_w_in, conv_k=m_conv_k, conv_w_out=m_conv_w_out, kv_mod_w=m_kv_mod_w,
                kv_mod_b=m_kv_mod_b, kv_norm_g=m_kv_norm_g, w_kv=m_w_kv, attn_w_q=m_attn_w_q,
                attn_w_o=m_attn_w_o, rel_bias=m_rel_bias)
    v_in = dict(mod_w=v_mod_w, mod_b=v_mod_b, norm_g=v_norm_g, ffn_w_in=v_ffn_w_in, ffn_w_out=v_ffn_w_out,
                conv_w_in=v_conv_w_in, conv_k=v_conv_k, conv_w_out=v_conv_w_out, kv_mod_w=v_kv_mod_w,
                kv_mod_b=v_kv_mod_b, kv_norm_g=v_kv_norm_g, w_kv=v_w_kv, attn_w_q=v_attn_w_q,
                attn_w_o=v_attn_w_o, rel_bias=v_rel_bias)
    names = list(weights)
    step = {}

    def update(n):
        g = grads[n].reshape(weights[n].shape)
        step[n] = (g, *_adamw(weights[n], g, m_in[n], v_in[n], f"adamw_{n}"))

    update("mod_w")
    reduced, (half_send, half_recv, half_bufs, _) = _reduce_finish(red, step["mod_w"][1])
    for n in list(grads):
        if n not in step:
            update(n)
    grads.update(zip(reduced, _split_wait(
        _half_copies(len(half_bufs)), half_send, half_recv, half_bufs, step["kv_mod_w"][1], "rs_half_wait")))
    for n in names:
        if n not in step:
            update(n)
    return (loss, dx0.reshape(x.shape), *[step[n][k] for k in range(4) for n in names])
```

```python
import functools
import math

import numpy as np
import jax
import jax.numpy as jnp
from jax import lax
from jax.experimental import pallas as pl
from jax.experimental.pallas import tpu as pltpu

CHUNK = 64
N_LEFT_CHUNKS = 8
N_HEADS = 16
MAX_REL = 2 * CHUNK
N_REL = 2 * MAX_REL + 1
EPS = 1e-6
ADAM_LR = 0.001
ADAM_B1 = 0.9
ADAM_B2 = 0.999
ADAM_EPS = 1e-08
ADAM_WD = 0.01
ADAM_STEP = 10

Q_CHUNKS = 4
BQ = Q_CHUNKS * CHUNK
N_WIN = 1 + N_LEFT_CHUNKS // Q_CHUNKS
HEADS_PER_STEP = 8
NEG = -1e30
N_DEV = 8
N_CHIP = 4
SMALL_TENSOR_ELEMS = 1 << 16
PIECE_ROWS = 256

BF16 = jnp.bfloat16
F32 = jnp.float32
V7X_VMEM_LIMIT_BYTES = 56 * 1024 * 1024
MESH = pl.DeviceIdType.MESH


def _pick(n, pref, align):
    t = min(pref, n)
    t -= t % align
    while t >= align:
        if n % t == 0:
            return t
        t -= align
    return n


def _params(*sem):
    return pltpu.CompilerParams(dimension_semantics=sem, vmem_limit_bytes=V7X_VMEM_LIMIT_BYTES)


def _colsum8(v):
    r, d = v.shape
    return v.reshape(r // 8, 8, d).sum(axis=0)


_DIMS = {"nn": (((1,), (0,)), ((), ())), "nt": (((1,), (1,)), ((), ())), "tn": (((0,), (0,)), ((), ()))}


def _mm(a, b, mode, out_dtype, name, *, b_layer=None, tm=1024, tn=1024, tk=None, scale=None, after=None,
        out_layers=1):
    if tk is None:
        tk = 2048 if mode == "tn" else 3072
    bs = b.shape[1:] if b_layer is not None else b.shape
    if mode == "nn":
        (M, K), (K2, N) = a.shape, bs
    elif mode == "nt":
        (M, K), (N, K2) = a.shape, bs
    else:
        (K, M), (K2, N) = a.shape, bs
    assert K == K2, (name, a.shape, b.shape)
    tm = _pick(M, tm, 128 if mode == "tn" else 16)
    tn = _pick(N // out_layers, tn, 128)
    tk = _pick(K, tk, 128 if mode != "tn" else 16)
    nk = K // tk
    assert scale is None or nk == 1, name
    dims = _DIMS[mode]
    extra = [] if after is None else [after]

    def body(a_ref, b_ref, *rest):
        o_ref, acc = rest[len(extra)], rest[len(extra) + 1:]
        p = lax.dot_general(a_ref[...].astype(BF16), b_ref[...].astype(BF16), dims,
                            preferred_element_type=F32)
        if nk == 1:
            o_ref[...] = (p if scale is None else p * scale).astype(o_ref.dtype)
        else:
            k = pl.program_id(2)

            @pl.when(k == 0)
            def _():
                acc[0][...] = p

            @pl.when(k > 0)
            def _():
                acc[0][...] += p

            @pl.when(k == nk - 1)
            def _():
                o_ref[...] = acc[0][...].astype(o_ref.dtype)

    a_spec = (pl.BlockSpec((tk, tm), lambda i, j, k: (k, i)) if mode == "tn"
              else pl.BlockSpec((tm, tk), lambda i, j, k: (i, k)))
    if mode == "nt":
        b_blk, b_idx = (tn, tk), (lambda i, j, k: (j, k))
    else:
        b_blk, b_idx = (tk, tn), (lambda i, j, k: (k, j))
    if b_layer is not None:
        b_spec = pl.BlockSpec((None,) + b_blk, lambda i, j, k: (b_layer,) + b_idx(i, j, k))
    else:
        b_spec = pl.BlockSpec(b_blk, b_idx)
    if out_layers > 1:
        per_layer = N // out_layers // tn
        o_spec = pl.BlockSpec((None, tm, tn), lambda i, j, k: (j // per_layer, i, j % per_layer))
        o_shape = (out_layers, M, N // out_layers)
    else:
        o_spec, o_shape = pl.BlockSpec((tm, tn), lambda i, j, k: (i, j)), (M, N)
    return pl.pallas_call(
        body, name=name,
        grid=(M // tm, N // tn, nk),
        in_specs=[a_spec, b_spec] + [pl.BlockSpec(memory_space=pl.ANY)] * len(extra),
        out_specs=o_spec,
        out_shape=jax.ShapeDtypeStruct(o_shape, out_dtype),
        scratch_shapes=[pltpu.VMEM((tm, tn), F32)] if nk > 1 else [],
        compiler_params=_params("parallel", "parallel", "arbitrary"),
    )(a, b, *extra)


def _row_spec(tm, d):
    return pl.BlockSpec((tm, d), lambda i: (i, 0))


def _vec_spec(r, d):
    return pl.BlockSpec((r, d), lambda i: (0, 0))


def _norm_mod(x, scales, shifts, name):
    S, D = x.shape
    nb = scales.shape[0]
    tm = _pick(S, 1024, 16)

    def body(x_ref, a_ref, b_ref, *o_refs):
        xv = x_ref[...]
        xh = xv * lax.rsqrt(jnp.mean(xv * xv, axis=-1, keepdims=True) + EPS)
        for n in range(nb):
            o_refs[n][...] = (xh * a_ref[n:n + 1, :] + b_ref[n:n + 1, :]).astype(BF16)

    return pl.pallas_call(
        body, name=name, grid=(S // tm,),
        in_specs=[_row_spec(tm, D), _vec_spec(nb, D), _vec_spec(nb, D)],
        out_specs=[_row_spec(tm, D)] * nb,
        out_shape=[jax.ShapeDtypeStruct((S, D), BF16)] * nb,
        compiler_params=_params("parallel"),
    )(x, scales, shifts)


def _mm_post(a, w, x, gate, name, *, scales=None, shifts=None, target=None):
    M, K = a.shape
    D = w.shape[2]
    tm = _pick(M, 1024 if K <= D else 512, 16)
    sub = _pick(tm, PIECE_ROWS, 16)
    nb = 0 if scales is None else scales.shape[0]

    def body(a_ref, w_ref, x_ref, g_ref, *rest):
        if target is None:
            sc_ref, sh_ref, y_ref, xn_ref = rest[:4]
            h_refs = rest[4:]
        else:
            t_ref, dx_ref, sq_ref, dy_ref, dg_ref = rest

            @pl.when(pl.program_id(0) == 0)
            def _():
                sq_ref[...] = jnp.zeros_like(sq_ref)
                dg_ref[...] = jnp.zeros_like(dg_ref)

        def product(r):
            return jnp.dot(a_ref[pl.ds(r * sub, sub), :], w_ref[...], preferred_element_type=F32)

        y = product(0)
        for r in range(tm // sub):
            rows = pl.ds(r * sub, sub)
            yb = y.astype(BF16)
            if r + 1 < tm // sub:
                y = product(r + 1)
            yv = yb.astype(F32)
            yh = yv * lax.rsqrt(jnp.mean(yv * yv, axis=-1, keepdims=True) + EPS)
            xn = x_ref[rows, :] + yh * g_ref[...]
            if target is None:
                y_ref[rows, :] = yb
                xn_ref[rows, :] = xn
                xh = xn * lax.rsqrt(jnp.mean(xn * xn, axis=-1, keepdims=True) + EPS)
                for n in range(nb):
                    h_refs[n][rows, :] = (xh * sc_ref[n:n + 1, :] + sh_ref[n:n + 1, :]).astype(BF16)
            else:
                e = xn - t_ref[rows, :]
                dx = e / D
                dx_ref[rows, :] = dx
                sq_ref[...] += _colsum8(e * e)
                dy, dxy = _post_norm_grad(dx, yb, g_ref[...])
                dy_ref[rows, :] = dy.astype(BF16)
                dg_ref[...] += _colsum8(dxy)

    ins = [a, w, x, gate]
    in_specs = [_row_spec(tm, K), pl.BlockSpec((None, K, D), lambda i: (0, 0, 0)), _row_spec(tm, D), _vec_spec(1, D)]
    if target is None:
        ins += [scales, shifts]
        in_specs += [_vec_spec(nb, D), _vec_spec(nb, D)]
        out_specs = [_row_spec(tm, D)] * (2 + nb)
        out_shape = [jax.ShapeDtypeStruct((M, D), BF16), jax.ShapeDtypeStruct((M, D), F32)] \
            + [jax.ShapeDtypeStruct((M, D), BF16)] * nb
    else:
        ins += [target]
        in_specs += [_row_spec(tm, D)]
        out_specs = [_row_spec(tm, D), _vec_spec(8, D), _row_spec(tm, D), _vec_spec(8, D)]
        out_shape = [jax.ShapeDtypeStruct((M, D), F32), jax.ShapeDtypeStruct((8, D), F32),
                     jax.ShapeDtypeStruct((M, D), BF16), jax.ShapeDtypeStruct((8, D), F32)]
    return pl.pallas_call(
        body, name=name, grid=(M // tm,), in_specs=in_specs, out_specs=out_specs, out_shape=out_shape,
        compiler_params=_params("arbitrary" if target is not None else "parallel"),
    )(*ins)


def _post_norm_grad(dxn, yb, gate):
    yv = yb.astype(F32)
    r = lax.rsqrt(jnp.mean(yv * yv, axis=-1, keepdims=True) + EPS)
    yh = yv * r
    dyh = dxn * gate
    return r * (dyh - yh * jnp.mean(dyh * yh, axis=-1, keepdims=True)), dxn * yh


def _mm_pre_bwd(pairs, x, dxn, scales, name, post=None):
    S, D = x.shape
    nb = len(pairs)
    tm = _pick(S, 512, 16)
    sub = _pick(tm, PIECE_ROWS, 16)

    def body(*refs):
        a_refs, w_refs = refs[0:2 * nb:2], refs[1:2 * nb:2]
        x_ref, d_ref, sc_ref = refs[2 * nb:2 * nb + 3]
        rest = refs[2 * nb + 3:]
        if post is not None:
            y_ref, g_ref, dx_ref, ds_ref, db_ref, dy_ref, dg_ref = rest
        else:
            dx_ref, ds_ref, db_ref = rest

        @pl.when(pl.program_id(0) == 0)
        def _():
            ds_ref[...] = jnp.zeros_like(ds_ref)
            db_ref[...] = jnp.zeros_like(db_ref)
            if post is not None:
                dg_ref[...] = jnp.zeros_like(dg_ref)

        def products(r):
            return [lax.dot_general(a_refs[n][pl.ds(r * sub, sub), :], w_refs[n][...], _DIMS["nt"],
                                    preferred_element_type=F32) for n in range(nb)]

        nxt = products(0)
        for r in range(tm // sub):
            rows = pl.ds(r * sub, sub)
            dhs = nxt
            if r + 1 < tm // sub:
                nxt = products(r + 1)
            xv = x_ref[rows, :]
            rr = lax.rsqrt(jnp.mean(xv * xv, axis=-1, keepdims=True) + EPS)
            xh = xv * rr
            dxh = jnp.zeros_like(xv)
            for n in range(nb):
                dh = dhs[n]
                dxh = dxh + dh * sc_ref[n:n + 1, :]
                ds_ref[n] += _colsum8(dh * xh)
                db_ref[n] += _colsum8(dh)
            dx = d_ref[rows, :] + rr * (dxh - xh * jnp.mean(dxh * xh, axis=-1, keepdims=True))
            dx_ref[rows, :] = dx
            if post is not None:
                dy, dxy = _post_norm_grad(dx, y_ref[rows, :], g_ref[...])
                dy_ref[rows, :] = dy.astype(BF16)
                dg_ref[...] += _colsum8(dxy)

    ins, in_specs = [], []
    for a, w in pairs:
        ins += [a, w]
        in_specs += [_row_spec(tm, a.shape[1]),
                     pl.BlockSpec((None, D, a.shape[1]), lambda i: (0, 0, 0), pipeline_mode=pl.Buffered(1))]
    ins += [x, dxn, scales]
    in_specs += [_row_spec(tm, D), _row_spec(tm, D), _vec_spec(nb, D)]
    acc_spec = pl.BlockSpec((nb, 8, D), lambda i: (0, 0, 0))
    out_specs = [_row_spec(tm, D), acc_spec, acc_spec]
    out_shape = [jax.ShapeDtypeStruct((S, D), F32), jax.ShapeDtypeStruct((nb, 8, D), F32),
                 jax.ShapeDtypeStruct((nb, 8, D), F32)]
    if post is not None:
        ins += list(post)
        in_specs += [_row_spec(tm, D), _vec_spec(1, D)]
        out_specs += [_row_spec(tm, D), _vec_spec(8, D)]
        out_shape += [jax.ShapeDtypeStruct((S, D), BF16), jax.ShapeDtypeStruct((8, D), F32)]
    return pl.pallas_call(
        body, name=name, grid=(S // tm,), in_specs=in_specs, out_specs=out_specs, out_shape=out_shape,
        compiler_params=_params("arbitrary"),
    )(*ins)


FFN_PAIRS = 2


def _ffn_in_act(h, w, layer, name):
    S, D = h.shape
    F2 = w.shape[2]
    PW = F2 // (2 * FFN_PAIRS)
    tm = _pick(S, 1024, 16)
    sub = _pick(tm, PIECE_ROWS, 16)

    def body(h_ref, w_ref, gu_ref, a_ref):
        def product(r):
            return jnp.dot(h_ref[pl.ds(r * sub, sub), :], w_ref[...], preferred_element_type=F32)

        nxt = product(0)
        for r in range(tm // sub):
            rows = pl.ds(r * sub, sub)
            acc = nxt
            if r + 1 < tm // sub:
                nxt = product(r + 1)
            gu_ref[rows, :] = acc.astype(BF16)
            g = acc[:, :PW]
            a_ref[rows, :] = (g * jax.nn.sigmoid(g) * acc[:, PW:]).astype(BF16)

    return pl.pallas_call(
        body, name=name, grid=(FFN_PAIRS, S // tm),
        in_specs=[pl.BlockSpec((tm, D), lambda p, i: (i, 0)),
                  pl.BlockSpec((None, D, 2 * PW), lambda p, i: (layer, 0, p))],
        out_specs=[pl.BlockSpec((tm, 2 * PW), lambda p, i: (i, p)), pl.BlockSpec((tm, PW), lambda p, i: (i, p))],
        out_shape=[jax.ShapeDtypeStruct((S, F2), BF16), jax.ShapeDtypeStruct((S, F2 // 2), BF16)],
        compiler_params=_params("parallel", "parallel"),
    )(h, w)


def _ffn_bwd(dy, w_out, gu, w_in, x, dxn, scale, post, name):
    S, D = dy.shape
    F2 = gu.shape[1]
    PW = F2 // (2 * FFN_PAIRS)
    tm = _pick(S, 256, 16)

    def body(dy_ref, wo_ref, gu_ref, wi_ref, x_ref, d_ref, sc_ref, y_ref, g_ref,
             dgu_ref, dx_ref, ds_ref, db_ref, dyn_ref, dg_ref):
        @pl.when(pl.program_id(0) == 0)
        def _():
            ds_ref[...] = jnp.zeros_like(ds_ref)
            db_ref[...] = jnp.zeros_like(db_ref)
            dg_ref[...] = jnp.zeros_like(dg_ref)

        def first_product(p):
            return lax.dot_general(dy_ref[...], wo_ref[p * PW:(p + 1) * PW, :], _DIMS["nt"],
                                   preferred_element_type=F32)

        dh = jnp.zeros((tm, D), F32)
        nxt = first_product(0)
        for p in range(FFN_PAIRS):
            cols = slice(2 * p * PW, 2 * (p + 1) * PW)
            da = nxt
            if p + 1 < FFN_PAIRS:
                nxt = first_product(p + 1)
            g = gu_ref[:, 2 * p * PW:(2 * p + 1) * PW].astype(F32)
            u = gu_ref[:, (2 * p + 1) * PW:2 * (p + 1) * PW].astype(F32)
            sg = jax.nn.sigmoid(g)
            dgu_ref[:, 2 * p * PW:(2 * p + 1) * PW] = (da * u * (sg * (1.0 + g * (1.0 - sg)))).astype(BF16)
            dgu_ref[:, (2 * p + 1) * PW:2 * (p + 1) * PW] = (da * (g * sg)).astype(BF16)
            dh = dh + lax.dot_general(dgu_ref[:, cols], wi_ref[:, cols], _DIMS["nt"], preferred_element_type=F32)
        xv = x_ref[...]
        rr = lax.rsqrt(jnp.mean(xv * xv, axis=-1, keepdims=True) + EPS)
        xh = xv * rr
        dxh = dh * sc_ref[...]
        ds_ref[0] += _colsum8(dh * xh)
        db_ref[0] += _colsum8(dh)
        dx = d_ref[...] + rr * (dxh - xh * jnp.mean(dxh * xh, axis=-1, keepdims=True))
        dx_ref[...] = dx
        dyn, dxy = _post_norm_grad(dx, y_ref[...], g_ref[...])
        dyn_ref[...] = dyn.astype(BF16)
        dg_ref[...] += _colsum8(dxy)

    resident = dict(pipeline_mode=pl.Buffered(1))
    acc_spec = pl.BlockSpec((1, 8, D), lambda i: (0, 0, 0))
    return pl.pallas_call(
        body, name=name, grid=(S // tm,),
        in_specs=[_row_spec(tm, D), pl.BlockSpec((None, F2 // 2, D), lambda i: (0, 0, 0), **resident),
                  _row_spec(tm, F2), pl.BlockSpec((None, D, F2), lambda i: (0, 0, 0), **resident),
                  _row_spec(tm, D), _row_spec(tm, D), _vec_spec(1, D), _row_spec(tm, D), _vec_spec(1, D)],
        out_specs=[_row_spec(tm, F2), _row_spec(tm, D), acc_spec, acc_spec, _row_spec(tm, D), _vec_spec(8, D)],
        out_shape=[jax.ShapeDtypeStruct((S, F2), BF16), jax.ShapeDtypeStruct((S, D), F32),
                   jax.ShapeDtypeStruct((1, 8, D), F32), jax.ShapeDtypeStruct((1, 8, D), F32),
                   jax.ShapeDtypeStruct((S, D), BF16), jax.ShapeDtypeStruct((8, D), F32)],
        compiler_params=_params("arbitrary"),
    )(dy, w_out, gu, w_in, x, dxn, scale, *post)


HALO = 16


def _conv_terms(bcx_ref, prev_ref, i, tm, D):
    b = bcx_ref[:, 0:D].astype(F32)
    cg = bcx_ref[:, D:2 * D].astype(F32)
    xin = bcx_ref[:, 2 * D:3 * D].astype(F32)
    z = cg * xin
    zp = prev_ref[:, D:2 * D].astype(F32) * prev_ref[:, 2 * D:3 * D].astype(F32)
    zp = jnp.where(i > 0, zp, 0.0)
    z_ext = jnp.concatenate([zp, z], axis=0)
    z1 = pltpu.roll(z_ext, 1, 0)[HALO:, :]
    z2 = pltpu.roll(z_ext, 2, 0)[HALO:, :]
    return b, cg, xin, z, z1, z2


def _conv_gate(bcx, ck, name):
    S, D3 = bcx.shape
    D = D3 // 3
    tm = _pick(S, 512, 16)
    hb = tm // HALO

    def body(bcx_ref, prev_ref, ck_ref, o_ref):
        i = pl.program_id(0)
        b, _, _, z, z1, z2 = _conv_terms(bcx_ref, prev_ref, i, tm, D)
        conv = ck_ref[0:1, :] * z2 + ck_ref[1:2, :] * z1 + ck_ref[2:3, :] * z
        o_ref[...] = (b * conv).astype(BF16)

    return pl.pallas_call(
        body, name=name, grid=(S // tm,),
        in_specs=[_row_spec(tm, D3),
                  pl.BlockSpec((HALO, D3), lambda i: (jnp.maximum(i * hb - 1, 0), 0)),
                  _vec_spec(8, D)],
        out_specs=_row_spec(tm, D),
        out_shape=jax.ShapeDtypeStruct((S, D), BF16),
        compiler_params=_params("parallel"),
    )(bcx, bcx, ck)


def _conv_gate_bwd(du, bcx, ck, name):
    S, D3 = bcx.shape
    D = D3 // 3
    tm = _pick(S, 512, 16)
    hb = tm // HALO
    nt = S // tm

    def body(du_ref, dun_ref, bcx_ref, prev_ref, next_ref, ck_ref, o_ref, dk_ref):
        i = pl.program_id(0)
        b, cg, xin, z, z1, z2 = _conv_terms(bcx_ref, prev_ref, i, tm, D)
        k0, k1, k2 = ck_ref[0:1, :], ck_ref[1:2, :], ck_ref[2:3, :]
        conv = k0 * z2 + k1 * z1 + k2 * z
        d = du_ref[...].astype(F32)
        dconv = d * b
        dcn = jnp.where(i < nt - 1, dun_ref[...].astype(F32) * next_ref[:, 0:D].astype(F32), 0.0)
        d_ext = jnp.concatenate([dconv, dcn], axis=0)
        d1 = pltpu.roll(d_ext, tm + HALO - 1, 0)[:tm, :]
        d2 = pltpu.roll(d_ext, tm + HALO - 2, 0)[:tm, :]
        dz = k2 * dconv + k1 * d1 + k0 * d2
        o_ref[:, 0:D] = (d * conv).astype(BF16)
        o_ref[:, D:2 * D] = (dz * xin).astype(BF16)
        o_ref[:, 2 * D:3 * D] = (dz * cg).astype(BF16)

        @pl.when(i == 0)
        def _():
            dk_ref[...] = jnp.zeros_like(dk_ref)

        dk_ref[0] += _colsum8(dconv * z2)
        dk_ref[1] += _colsum8(dconv * z1)
        dk_ref[2] += _colsum8(dconv * z)

    last = S // HALO - 1
    return pl.pallas_call(
        body, name=name, grid=(nt,),
        in_specs=[_row_spec(tm, D),
                  pl.BlockSpec((HALO, D), lambda i: (jnp.minimum((i + 1) * hb, last), 0)),
                  _row_spec(tm, D3),
                  pl.BlockSpec((HALO, D3), lambda i: (jnp.maximum(i * hb - 1, 0), 0)),
                  pl.BlockSpec((HALO, D3), lambda i: (jnp.minimum((i + 1) * hb, last), 0)),
                  _vec_spec(8, D)],
        out_specs=[_row_spec(tm, D3), pl.BlockSpec((3, 8, D), lambda i: (0, 0, 0))],
        out_shape=[jax.ShapeDtypeStruct((S, D3), BF16), jax.ShapeDtypeStruct((3, 8, D), F32)],
        compiler_params=_params("arbitrary"),
    )(du, du, bcx, bcx, bcx, ck)


def _rel_onehot():
    a = np.arange(CHUNK)[:, None]
    b = np.arange(CHUNK)[None, :]
    idx = np.stack([np.clip((N_LEFT_CHUNKS - dl) * CHUNK + a - b, -MAX_REL, MAX_REL) + MAX_REL
                    for dl in (6, 7, 8)]).reshape(-1)
    return (jnp.asarray(idx)[:, None] == jnp.arange(N_REL)[None, :]).astype(F32)


def _bias_table(rel_bias, name):
    H = rel_bias.shape[0]
    near = jnp.dot(rel_bias, _rel_onehot().T, precision=lax.Precision.HIGHEST).reshape(H, 3, CHUNK, CHUNK)
    far = jnp.broadcast_to(rel_bias[:, N_REL - 1][:, None, None], (H, CHUNK, CHUNK))

    def body(near_ref, far_ref, o_ref):
        neg = jnp.full((CHUNK, CHUNK), NEG, F32)
        for v in range(N_WIN):
            for ic in range(Q_CHUNKS):
                for jc in range(N_WIN * Q_CHUNKS):
                    dl = jc - ic
                    if dl < 0 or dl > N_LEFT_CHUNKS or jc < (N_WIN - 1 - v) * Q_CHUNKS:
                        blk = neg
                    else:
                        blk = far_ref[...] if dl <= 5 else near_ref[dl - 6]
                    o_ref[v, ic * CHUNK:(ic + 1) * CHUNK, jc * CHUNK:(jc + 1) * CHUNK] = blk

    return pl.pallas_call(
        body, name=name, grid=(H,),
        in_specs=[pl.BlockSpec((None, 3, CHUNK, CHUNK), lambda h: (h, 0, 0, 0)),
                  pl.BlockSpec((None, CHUNK, CHUNK), lambda h: (h, 0, 0))],
        out_specs=pl.BlockSpec((N_WIN, None, BQ, N_WIN * BQ), lambda h: (0, h, 0, 0)),
        out_shape=jax.ShapeDtypeStruct((N_WIN, H, BQ, N_WIN * BQ), F32),
        compiler_params=_params("parallel"),
    )(near, far)


NEAR_FIRST = 6
SLAB_ROWS = 2 * CHUNK
SLAB_COLS = 4 * CHUNK


def _slab(pair):
    c0 = (NEAR_FIRST + 2 * pair) * CHUNK
    return slice(pair * SLAB_ROWS, (pair + 1) * SLAB_ROWS), slice(c0, c0 + SLAB_COLS)


def _bias_table_grad(dslab):
    H = dslab.shape[0]

    def blk(ic, dl):
        pair, r, col = ic // 2, ic % 2, ic + dl - NEAR_FIRST - 2 * (ic // 2)
        return dslab[:, pair, r * CHUNK:(r + 1) * CHUNK, col * CHUNK:(col + 1) * CHUNK]

    by_dl = [sum(blk(ic, dl) for ic in range(Q_CHUNKS)) for dl in (6, 7, 8)]
    near = jnp.stack(by_dl, axis=1).reshape(H, 3 * CHUNK * CHUNK)
    g = jnp.dot(near, _rel_onehot(), precision=lax.Precision.HIGHEST)
    return g.at[:, N_REL - 1].add(-jnp.sum(near, axis=1))


def _attn_specs(nblk, W):
    last = nblk - 1
    q_spec = pl.BlockSpec((BQ, W), lambda g, i: (jnp.minimum(i, last), g))
    kv_specs = [pl.BlockSpec((BQ, 2 * W), functools.partial(
        lambda g, i, w: (jnp.maximum(jnp.minimum(i, last) - (N_WIN - 1) + w, 0), g), w=w)) for w in range(N_WIN)]
    tab_spec = pl.BlockSpec((None, HEADS_PER_STEP, BQ, N_WIN * BQ),
                            lambda g, i: (jnp.minimum(i, N_WIN - 1), g, 0, 0))
    dtab_spec = pl.BlockSpec((HEADS_PER_STEP, Q_CHUNKS // 2, SLAB_ROWS, SLAB_COLS), lambda g, i: (g, 0, 0, 0))
    return q_spec, kv_specs, tab_spec, dtab_spec


def _attn_scores(q_ref, kT, tab_ref, h, dh):
    return jnp.dot(q_ref[:, h * dh:(h + 1) * dh], kT[h * dh:(h + 1) * dh, :], preferred_element_type=F32) + tab_ref[h]


def _attn_fwd(q, kv, tab, name):
    S, D = q.shape
    dh = D // N_HEADS
    W = HEADS_PER_STEP * dh
    assert 2 * W == D, "the kv layout puts one head group's k beside its v: two head groups"
    q_spec, kv_specs, tab_spec, _ = _attn_specs(S // BQ, W)

    def body(q_ref, *rest):
        tab_ref, o_ref = rest[N_WIN], rest[N_WIN + 1]
        kvw = jnp.concatenate([r[...] for r in rest[:N_WIN]], axis=0)
        kT = kvw[:, :W].T
        vw = kvw[:, W:]
        outs = []
        s = _attn_scores(q_ref, kT, tab_ref, 0, dh)
        for h in range(HEADS_PER_STEP):
            s_next = _attn_scores(q_ref, kT, tab_ref, h + 1, dh) if h + 1 < HEADS_PER_STEP else None
            e = jnp.exp(s - jnp.max(s, axis=-1, keepdims=True))
            l = jnp.sum(e, axis=-1, keepdims=True)
            outs.append(jnp.dot(e.astype(BF16), vw[:, h * dh:(h + 1) * dh], preferred_element_type=F32) / l)
            s = s_next
        o_ref[...] = jnp.concatenate(outs, axis=1).astype(BF16)

    return pl.pallas_call(
        body, name=name, grid=(N_HEADS // HEADS_PER_STEP, S // BQ),
        in_specs=[q_spec] + kv_specs + [tab_spec],
        out_specs=q_spec,
        out_shape=jax.ShapeDtypeStruct((S, D), BF16),
        compiler_params=_params("parallel", "parallel"),
    )(q, *([kv] * N_WIN), tab)


def _attn_bwd(q, kv, tab, do, name):
    S, D = q.shape
    dh = D // N_HEADS
    W = HEADS_PER_STEP * dh
    nblk = S // BQ
    q_spec, kv_specs, tab_spec, dtab_spec = _attn_specs(nblk, W)

    def body(q_ref, *rest):
        tab_ref, do_ref, dq_ref, dkv_ref, dtab_ref, ring = rest[N_WIN:]
        i = pl.program_id(1)

        @pl.when(i == 0)
        def _():
            dtab_ref[...] = jnp.zeros_like(dtab_ref)
            ring[...] = jnp.zeros_like(ring)

        @pl.when(i < nblk)
        def _():
            kvw = jnp.concatenate([r[...] for r in rest[:N_WIN]], axis=0)
            kT = kvw[:, :W].T
            vw = kvw[:, W:]
            qT = q_ref[...].T
            dqs, dks, dvs = [], [], []

            s = _attn_scores(q_ref, kT, tab_ref, 0, dh)
            for h in range(HEADS_PER_STEP):
                hd = slice(h * dh, (h + 1) * dh)
                do_h = do_ref[:, hd]
                dp = lax.dot_general(do_h, vw[:, hd], _DIMS["nt"], preferred_element_type=F32)
                e = jnp.exp(s - jnp.max(s, axis=-1, keepdims=True))
                inv_l = 1.0 / jnp.sum(e, axis=-1, keepdims=True)
                if h + 1 < HEADS_PER_STEP:
                    s = _attn_scores(q_ref, kT, tab_ref, h + 1, dh)
                delta = jnp.sum(e * dp, axis=-1, keepdims=True) * inv_l
                ds = e * ((dp - delta) * inv_l)
                for pair in range(Q_CHUNKS // 2):
                    rows, cols = _slab(pair)
                    dtab_ref[h, pair] += ds[rows, cols]
                dsb = ds.astype(BF16)
                dqs.append(lax.dot_general(kT[hd, :], dsb, _DIMS["nt"], preferred_element_type=F32) * (dh ** -0.5))
                dks.append(jnp.dot(qT[hd, :], dsb, preferred_element_type=F32))
                do_s = (do_h.astype(F32) * inv_l).astype(BF16)
                dvs.append(jnp.dot(do_s.T, e.astype(BF16), preferred_element_type=F32))
            dq_ref[...] = jnp.concatenate(dqs, axis=0).T.astype(BF16)
            dkv = jnp.concatenate(dks + dvs, axis=0).T
            for w in range(N_WIN):
                slot = lax.rem(i + 1 + w, N_WIN)
                part = dkv[w * BQ:(w + 1) * BQ, :]
                if w == N_WIN - 1:
                    ring[slot] = part
                else:
                    ring[slot] += part

        dkv_ref[...] = ring[lax.rem(i + 1, N_WIN)].astype(BF16)

    done_spec = pl.BlockSpec((BQ, 2 * W), lambda g, i: (jnp.maximum(i - (N_WIN - 1), 0), g))
    return pl.pallas_call(
        body, name=name, grid=(N_HEADS // HEADS_PER_STEP, nblk + N_WIN - 1),
        in_specs=[q_spec] + kv_specs + [tab_spec, q_spec],
        out_specs=[q_spec, done_spec, dtab_spec],
        out_shape=[jax.ShapeDtypeStruct((S, D), BF16), jax.ShapeDtypeStruct((S, 2 * D), BF16),
                   jax.ShapeDtypeStruct((N_HEADS, Q_CHUNKS // 2, SLAB_ROWS, SLAB_COLS), F32)],
        scratch_shapes=[pltpu.VMEM((N_WIN, BQ, 2 * W), F32)],
        compiler_params=_params("parallel", "arbitrary"),
    )(q, *([kv] * N_WIN), tab, do)


def _adamw(w, g, m, v, name):
    shape = w.shape
    C = shape[-1]
    R = int(np.prod(shape[:-1])) if len(shape) > 1 else 1
    whole = len(shape) >= 2 and R * C <= SMALL_TENSOR_ELEMS
    if whole:
        w2, g2, m2, v2 = w, g, m, v
    else:
        w2, g2, m2, v2 = (t.reshape(R, C) for t in (w, g, m, v))
    tr = _pick(R, max(8, (512 * 1024) // C // 8 * 8), 8)

    def body(w_ref, g_ref, m_ref, v_ref, d_ref, nm_ref, nv_ref):
        gv = g_ref[...]
        nm = ADAM_B1 * m_ref[...] + (1.0 - ADAM_B1) * gv
        nv = ADAM_B2 * v_ref[...] + (1.0 - ADAM_B2) * jnp.square(gv)
        m_hat = nm / (1.0 - ADAM_B1 ** ADAM_STEP)
        v_hat = nv / (1.0 - ADAM_B2 ** ADAM_STEP)
        d_ref[...] = -ADAM_LR * (m_hat / (jnp.sqrt(v_hat) + ADAM_EPS) + ADAM_WD * w_ref[...])
        nm_ref[...] = nm
        nv_ref[...] = nv

    if whole:
        spec, grid = pl.BlockSpec(shape, lambda i: (0,) * len(shape)), (1,)
    else:
        spec, grid = pl.BlockSpec((tr, C), lambda i: (i, 0)), (R // tr,)
    outs = pl.pallas_call(
        body, name=name, grid=grid,
        in_specs=[spec] * 4, out_specs=[spec] * 3,
        out_shape=[jax.ShapeDtypeStruct(w2.shape, F32)] * 3,
        compiler_params=_params("parallel"),
    )(w2, g2, m2, v2)
    return tuple(o.reshape(shape) for o in outs)


def _sum_rows(a, name):
    n, L = a.shape

    def body(a_ref, o_ref):
        acc = a_ref[0:1, :]
        for r in range(1, n):
            acc = acc + a_ref[r:r + 1, :]
        o_ref[...] = acc

    return pl.pallas_call(
        body, name=name, grid=(1,),
        in_specs=[pl.BlockSpec((n, L), lambda i: (0, 0))],
        out_specs=pl.BlockSpec((1, L), lambda i: (0, 0)),
        out_shape=jax.ShapeDtypeStruct((1, L), F32),
        compiler_params=_params("arbitrary"),
    )(a)


def _scalar_call(body, name, scalar, grid, in_specs, out_spec, out_shape, args):
    return pl.pallas_call(
        body, name=name,
        grid_spec=pltpu.PrefetchScalarGridSpec(num_scalar_prefetch=1, grid=grid, in_specs=in_specs,
                                               out_specs=out_spec),
        out_shape=out_shape, compiler_params=_params("parallel"),
    )(jnp.reshape(scalar, (-1,)).astype(jnp.int32), *args)


def _pair_sum(view, got, c, name):
    nb, _, rh, cols = view.shape
    tr = _pick(rh, max(16, (1 << 20) // cols // 16 * 16), 16)
    bpr = rh // tr

    def body(s_ref, a_ref, b_ref, o_ref):
        o_ref[...] = (a_ref[...].astype(F32) + b_ref[...].astype(F32)).astype(BF16)

    spec = pl.BlockSpec((tr, cols), lambda i, s: (i, 0))
    mine = pl.BlockSpec((tr, cols), lambda i, s: ((2 * (i // bpr) + s[0]) * bpr + i % bpr, 0))
    return _scalar_call(body, name, c, (nb * bpr,), [mine, spec], spec,
                        jax.ShapeDtypeStruct((nb * rh, cols), BF16),
                        (view.reshape(nb * 2 * rh, cols), got.reshape(nb * rh, cols)))


STACKED_LAYERS = 2


def _owner_sum(pair, recv, me, c, it, name, layer=None, into=None):
    _, rh, bc = recv.shape
    tr = _pick(rh, max(16, (1 << 19) // bc // 16 * 16), 16)
    bpr = rh // tr

    def body(s_ref, a_ref, r0, r1, r2, *rest):
        rest[-1][...] = ((a_ref[...].astype(F32) + r0[...].astype(F32)) + r1[...].astype(F32)) + r2[...].astype(F32)

    if it.kind == "col":
        own = pl.BlockSpec((tr, bc), lambda i, s: (i, s[0]))
    else:
        own = pl.BlockSpec((tr, bc), lambda i, s: (s[0] * bpr + i, 0))
    slots = [pl.BlockSpec((None, tr, bc), functools.partial(lambda i, s, k: (k, i, 0), k=k)) for k in range(3)]
    in_specs, args, aliases = [own] + slots, [pair, recv, recv, recv], {}
    if layer is None:
        out_spec = pl.BlockSpec((tr, bc), lambda i, s: (s[1] * bpr + i, 0))
        out_shape = jax.ShapeDtypeStruct((2 * rh, bc), F32)
    else:
        out_spec = pl.BlockSpec((None, tr, bc), lambda i, s: (layer, s[1] * bpr + i, 0))
        out_shape = jax.ShapeDtypeStruct((STACKED_LAYERS, 2 * rh, bc), F32)
        if into is not None:
            in_specs.append(pl.BlockSpec(memory_space=pl.ANY))
            args.append(into)
            aliases = {len(args): 0}
    return pl.pallas_call(
        body, name=name,
        grid_spec=pltpu.PrefetchScalarGridSpec(num_scalar_prefetch=1, grid=(bpr,), in_specs=in_specs,
                                               out_specs=out_spec),
        out_shape=out_shape, input_output_aliases=aliases, compiler_params=_params("parallel"),
    )(jnp.stack([it.pos(me), c]).astype(jnp.int32), *args)


def _place():
    x, y, c = lax.axis_index("x"), lax.axis_index("y"), lax.axis_index("c")
    chips = [(1 - x, y), (x, 1 - y), (1 - x, 1 - y)]
    return x, y, c, chips


def _chip_index(px, py):
    return 2 * px + py


def _all_gather_small(x_shard, name):
    m_per, n = x_shard.shape

    def body(x_ref, out_ref, send_sems, recv_sems, local_sem):
        x, y, c, chips = _place()
        me, sibling = (x, y, c), (x, y, 1 - c)

        def rows(px, py, pc):
            return out_ref.at[pl.ds((4 * px + 2 * py + pc) * m_per, m_per), :]

        def copy(k, block, to, src=None):
            return pltpu.make_async_remote_copy(
                src_ref=rows(*block) if src is None else src, dst_ref=rows(*block),
                send_sem=send_sems.at[k], recv_sem=recv_sems.at[k], device_id=to, device_id_type=MESH)

        mine = pltpu.make_async_copy(x_ref, rows(*me), local_sem)
        mine.start()
        first = [copy(0, me, sibling, src=x_ref)]
        first += [copy(1 + j, me, (*chip, c), src=x_ref) for j, chip in enumerate(chips)]
        for cp in first:
            cp.start()
        passed = [copy(4 + j, (*chip, c), sibling) for j, chip in enumerate(chips)]
        for j, chip in enumerate(chips):
            copy(1 + j, (*chip, c), me).wait_recv()
            passed[j].start()
        copy(0, sibling, me).wait_recv()
        for j, chip in enumerate(chips):
            copy(4 + j, (*chip, 1 - c), me).wait_recv()
        for cp in first + passed:
            cp.wait_send()
        mine.wait()

    return pl.pallas_call(
        body, name=name,
        out_shape=jax.ShapeDtypeStruct((N_DEV * m_per, n), x_shard.dtype),
        in_specs=[pl.BlockSpec(memory_space=pltpu.VMEM)],
        out_specs=pl.BlockSpec(memory_space=pltpu.VMEM),
        scratch_shapes=[pltpu.SemaphoreType.DMA((7,)), pltpu.SemaphoreType.DMA((7,)), pltpu.SemaphoreType.DMA],
    )(x_shard)


def _gather_flat(vec, name):
    L = vec.shape[0]
    Lp = -(-L // 1024) * 1024
    g = _all_gather_small(jnp.pad(vec, (0, Lp - L)).reshape(8, Lp // 8), name)
    return g.reshape(N_DEV, Lp)[:, :L]


class _Item:
    def __init__(self, kind, rows, cols, arg, layer, swap=False):
        self.kind, self.rows, self.cols, self.arg, self.layer, self.swap = kind, rows, cols, arg, layer, swap

    def ref(self, refs):
        return refs[self.arg].at[self.layer]

    def pos(self, j):
        return 2 * (j % 2) + j // 2 if self.swap else j


def _block(ref, it, j, half):
    if it.kind == "col":
        ns = it.cols // N_CHIP
        return ref.at[pl.ds(half * (it.rows // 2), it.rows // 2), pl.ds(it.pos(j) * ns, ns)]
    rs = it.rows // N_CHIP
    return ref.at[pl.ds(j * rs + half * (rs // 2), rs // 2), :]


def _cast_place(w, layer, kind, pos, after, name):
    _, r, n = w.shape
    tr = _pick(r, max(16, (1 << 20) // n // 16 * 16), 16)
    bpr = r // tr

    def body(s_ref, w_ref, after_ref, o_ref):
        o_ref[...] = w_ref[...].astype(BF16)

    if kind == "col":
        full, out_idx = (1, r, N_CHIP * n), (lambda i, s: (0, i, s[0]))
    else:
        full, out_idx = (1, N_CHIP * r, n), (lambda i, s: (0, s[0] * bpr + i, 0))
    return pl.pallas_call(
        body, name=name,
        grid_spec=pltpu.PrefetchScalarGridSpec(
            num_scalar_prefetch=1, grid=(bpr,),
            in_specs=[pl.BlockSpec((None, tr, n), lambda i, s: (layer, i, 0)), pl.BlockSpec(memory_space=pl.ANY)],
            out_specs=pl.BlockSpec((None, tr, n), out_idx)),
        out_shape=jax.ShapeDtypeStruct(full, BF16),
        compiler_params=_params("parallel"),
    )(jnp.reshape(pos, (1,)).astype(jnp.int32), w, after)


HBM_SPEC = pl.BlockSpec(memory_space=pltpu.HBM)
SEM_SPEC = pl.BlockSpec(memory_space=pltpu.SEMAPHORE)
ANY_SPEC = pl.BlockSpec(memory_space=pl.ANY)
SPLIT_PARAMS = dict(has_side_effects=pltpu.SideEffectType.DATAFLOW_SIDE_EFFECTING)


def _in_hbm(a):
    return pltpu.with_memory_space_constraint(a, pltpu.HBM)


def _split_start(copies_of, bufs, n_sem, after, name):
    n = len(bufs)

    def body(*refs):
        ins, send, recv, token = refs[:n], refs[n + 1], refs[n + 2], refs[2 * n + 3]
        for cp in copies_of(ins, send, recv, False)[0]:
            cp.start()
        token[...] = jnp.zeros_like(token)

    outs = pl.pallas_call(
        body, name=name,
        out_shape=(pltpu.SemaphoreType.DMA(n_sem), pltpu.SemaphoreType.DMA(n_sem),
                   *[pltpu.HBM(b.shape, b.dtype) for b in bufs], jax.ShapeDtypeStruct((8, 128), F32)),
        in_specs=[HBM_SPEC] * n + [ANY_SPEC],
        out_specs=(SEM_SPEC, SEM_SPEC, *[HBM_SPEC] * n, pl.BlockSpec(memory_space=pltpu.VMEM)),
        input_output_aliases={t: 2 + t for t in range(n)},
        compiler_params=pltpu.CompilerParams(**SPLIT_PARAMS),
    )(*[_in_hbm(b) for b in bufs], after)
    return outs[0], outs[1], list(outs[2:2 + n]), outs[2 + n]


def _split_wait(copies_of, send, recv, bufs, after, name):
    n = len(bufs)
    after = list(after) if isinstance(after, (list, tuple)) else [after]

    def body(*refs):
        ins, send_ref, recv_ref = refs[:n], refs[n], refs[n + 1]
        sends, arrivals = copies_of(ins, send_ref, recv_ref, True)
        for cp in sends:
            cp.wait_send()
        for cp in arrivals:
            cp.wait_recv()

    return pl.pallas_call(
        body, name=name,
        out_shape=[pltpu.HBM(b.shape, b.dtype) for b in bufs],
        in_specs=[HBM_SPEC] * n + [SEM_SPEC, SEM_SPEC] + [ANY_SPEC] * len(after),
        out_specs=[HBM_SPEC] * n,
        input_output_aliases={t: t for t in range(n)},
        compiler_params=pltpu.CompilerParams(**SPLIT_PARAMS),
    )(*bufs, send, recv, *after)


def _gather_copies(items):
    def copies_of(refs, send, recv, with_arrivals):
        x, y, c, chips = _place()
        me = _chip_index(x, y)
        sends, arrivals = [], []
        for t, it in enumerate(items):
            for k, chip in enumerate(chips):
                for core in range(2):
                    mine = _block(it.ref(refs), it, me, c)
                    sends.append(pltpu.make_async_remote_copy(
                        src_ref=mine, dst_ref=mine, send_sem=send.at[6 * t + 2 * k + core],
                        recv_sem=recv.at[6 * t + 2 * k + c], device_id=(*chip, core), device_id_type=MESH))
                    if with_arrivals:
                        landed = _block(it.ref(refs), it, _chip_index(*chip), core)
                        arrivals.append(pltpu.make_async_remote_copy(
                            src_ref=landed, dst_ref=landed, send_sem=send.at[6 * t + 2 * k + core],
                            recv_sem=recv.at[6 * t + 2 * k + core], device_id=(*chip, core), device_id_type=MESH))
        return sends, arrivals

    return copies_of


def _owner_copies(items):
    n = len(items)

    def blk(ref, it, j):
        if it.kind == "col":
            ns = it.cols // N_CHIP
            return ref.at[:, pl.ds(it.pos(j) * ns, ns)]
        return ref.at[j]

    def copies_of(refs, send, recv, with_arrivals):
        x, y, c, chips = _place()
        sends, arrivals = [], []
        for t, it in enumerate(items):
            for k, chip in enumerate(chips):
                slot = refs[n + t].at[k]
                sends.append(pltpu.make_async_remote_copy(
                    src_ref=blk(refs[t], it, _chip_index(*chip)), dst_ref=slot, send_sem=send.at[3 * t + k],
                    recv_sem=recv.at[3 * t + k], device_id=(*chip, c), device_id_type=MESH))
                if with_arrivals:
                    arrivals.append(pltpu.make_async_remote_copy(
                        src_ref=slot, dst_ref=slot, send_sem=send.at[3 * t + k], recv_sem=recv.at[3 * t + k],
                        device_id=(*chip, c), device_id_type=MESH))
        return sends, arrivals

    return copies_of


def _owner_slot_shape(it):
    if it.kind == "col":
        return (3, it.rows // 2, it.cols // N_CHIP)
    return (3, it.rows // (2 * N_CHIP), it.cols)


def _pair_view(g, it):
    if it.kind == "col":
        return g.reshape(1, 2, it.rows // 2, it.cols)
    return g.reshape(N_CHIP, 2, it.rows // (2 * N_CHIP), it.cols)


def _pair_copies(n):
    def copies_of(refs, send, recv, with_arrivals):
        x, y, c, _ = _place()
        sends, arrivals = [], []
        for t in range(n):
            land = refs[n + t]
            sends.append(pltpu.make_async_remote_copy(
                src_ref=refs[t].at[:, pl.ds(1 - c, 1)], dst_ref=land, send_sem=send.at[t], recv_sem=recv.at[t],
                device_id=(x, y, 1 - c), device_id_type=MESH))
            if with_arrivals:
                arrivals.append(pltpu.make_async_remote_copy(
                    src_ref=land, dst_ref=land, send_sem=send.at[t], recv_sem=recv.at[t],
                    device_id=(x, y, 1 - c), device_id_type=MESH))
        return sends, arrivals

    return copies_of


def _half_copies(n):
    def half(ref, which):
        r2 = ref.shape[-2] // 2
        rows = pl.ds(which * r2, r2)
        return ref.at[rows, :] if len(ref.shape) == 2 else ref.at[:, rows, :]

    def copies_of(refs, send, recv, with_arrivals):
        x, y, c, _ = _place()
        sends, arrivals = [], []
        for t in range(n):
            mine = half(refs[t], c)
            sends.append(pltpu.make_async_remote_copy(
                src_ref=mine, dst_ref=mine, send_sem=send.at[t], recv_sem=recv.at[t],
                device_id=(x, y, 1 - c), device_id_type=MESH))
            if with_arrivals:
                theirs = half(refs[t], 1 - c)
                arrivals.append(pltpu.make_async_remote_copy(
                    src_ref=theirs, dst_ref=theirs, send_sem=send.at[t], recv_sem=recv.at[t],
                    device_id=(x, y, 1 - c), device_id_type=MESH))
        return sends, arrivals

    return copies_of


class _Reduction:
    pass


def _pair_start(grads, items, after, tag, names, layer=None):
    n = len(items)
    views = [_pair_view(g, it) for g, it in zip(grads, items)]
    lands = [lax.empty((v.shape[0], 1) + v.shape[2:], v.dtype) for v in views]
    r = _Reduction()
    r.items, r.tag, r.names, r.layer = items, tag, names, layer
    r.send, r.recv, r.bufs, r.token = _split_start(_pair_copies(n), views + lands, (n,), after, f"rs_pair_start_{tag}")
    return r


def _owner_start(r, after):
    x, y, c, _ = _place()
    n = len(r.items)
    bufs = _split_wait(_pair_copies(n), r.send, r.recv, r.bufs, after, f"rs_pair_wait_{r.tag}")
    pairs = [_pair_sum(bufs[t], bufs[n + t], c, f"rs_pair_sum_{r.tag}_{t}") for t in range(n)]
    shaped = [p if it.kind == "col" else p.reshape(N_CHIP, p.shape[0] // N_CHIP, p.shape[1])
              for p, it in zip(pairs, r.items)]
    lands = [lax.empty(_owner_slot_shape(it), BF16) for it in r.items]
    r.send, r.recv, r.bufs, r.token = _split_start(
        _owner_copies(r.items), shaped + lands, (3 * n,), r.token, f"rs_owner_start_{r.tag}")
    return r


def _reduce_finish(groups, after):
    x, y, c, _ = _place()
    me = _chip_index(x, y)
    halves = {}
    for r in groups:
        n = len(r.items)
        bufs = _split_wait(_owner_copies(r.items), r.send, r.recv, r.bufs, after, f"rs_owner_wait_{r.tag}")
        for t, (it, nm) in enumerate(zip(r.items, r.names)):
            pair = bufs[t].reshape(-1, bufs[t].shape[-1])
            halves[nm] = _owner_sum(pair, bufs[n + t], me, c, it, f"rs_owner_sum_{r.tag}_{t}",
                                    layer=r.layer, into=halves.get(nm))
    n = len(halves)
    return list(halves), _split_start(_half_copies(n), list(halves.values()), (n,), after, "rs_half_start")


def _silu(v):
    return v * jax.nn.sigmoid(v)


def _sum8(p):
    return jnp.sum(p, axis=-2)


def kernel(x, c, mod_w, mod_b, norm_g, ffn_w_in, ffn_w_out, conv_w_in, conv_k, conv_w_out, kv_mod_w, kv_mod_b, kv_norm_g, w_kv, attn_w_q, attn_w_o, rel_bias, loss_target, m_mod_w, m_mod_b, m_norm_g, m_ffn_w_in, m_ffn_w_out, m_conv_w_in, m_conv_k, m_conv_w_out, m_kv_mod_w, m_kv_mod_b, m_kv_norm_g, m_w_kv, m_attn_w_q, m_attn_w_o, m_rel_bias, v_mod_w, v_mod_b, v_norm_g, v_ffn_w_in, v_ffn_w_out, v_conv_w_in, v_conv_k, v_conv_w_out, v_kv_mod_w, v_kv_mod_b, v_kv_norm_g, v_w_kv, v_attn_w_q, v_attn_w_o, v_rel_bias):
    xi, yi, ci = lax.axis_index("x"), lax.axis_index("y"), lax.axis_index("c")
    chip = 2 * xi + yi
    dev = 2 * chip + ci
    _, S, D = x.shape
    F = ffn_w_out.shape[1] * N_CHIP
    x0 = x.reshape(S, D)
    target = loss_target.reshape(S, D)
    n_mod = mod_w.shape[2]
    n_kvm = kv_mod_w.shape[1]
    dsh = D // N_CHIP
    TF = F // 2

    c_all = _all_gather_small(c.reshape(8, D // 8), "ag_c").reshape(N_DEV, D)
    sc16 = jnp.pad(_silu(c_all), ((0, 8), (0, 0)))
    part = [_mm(sc16, mod_w, "nn", F32, f"mod_fwd_{l}", b_layer=l)[:8] for l in range(2)]
    part.append(_mm(sc16, kv_mod_w, "nn", F32, "mod_fwd_kv")[:8])
    fwd_vec = jnp.concatenate([p.reshape(-1) for p in part] + [norm_g.reshape(-1), conv_k.reshape(-1)])
    fwd_all = _gather_flat(fwd_vec, "ag_fwd_small")[0::2]
    o = 0
    mods = []
    for n in (n_mod, n_mod, n_kvm):
        blk = fwd_all[:, o:o + 8 * n].reshape(N_CHIP, 8, n)
        mods.append(lax.dynamic_index_in_dim(blk, dev, axis=1, keepdims=False).reshape(N_CHIP * n))
        o += 8 * n
    ng = fwd_all[:, o:o + 8 * dsh].reshape(N_CHIP, 2, 4, dsh).transpose(1, 2, 0, 3).reshape(2, 4, D)
    o += 8 * dsh
    ck = fwd_all[:, o:o + 3 * dsh].reshape(N_CHIP, 3, dsh).transpose(1, 0, 2).reshape(3, D)
    ck8 = jnp.pad(ck, ((0, 5), (0, 0)))
    mod = [mods[l] + mod_b[l] for l in range(2)]
    sh1, sc1, g1, sh2, sc2, g2 = zip(*[jnp.split(m, 6) for m in mod])
    kv_sh, kv_sc = jnp.split(mods[2] + kv_mod_b, 2)
    row = lambda v: v.reshape(1, D)

    it_conv = [_Item("col", D, 3 * D, 0, 0), _Item("row", D, D, 1, 0)]
    it_ffn = [_Item("col", D, 2 * F, 0, 0, swap=True), _Item("row", F, D, 1, 0)]
    it_attn = [_Item("col", D, 2 * D, 0, 0, swap=True), _Item("row", D, D, 1, 0), _Item("row", D, D, 2, 0)]

    def placed(w, layer, it, nm, after=fwd_all):
        return _cast_place(w, layer, it.kind, it.pos(chip), after, f"place_{nm}")

    flying = {}

    def start(tag, its, bufs, after):
        send, recv, bufs, tok = _split_start(_gather_copies(its), bufs, (6 * len(its),), after, f"ag_start_{tag}")
        flying[tag] = (its, send, recv, bufs)
        return tok

    def arrived(tag, after):
        its, send, recv, bufs = flying[tag]
        return _split_wait(_gather_copies(its), send, recv, bufs, after, f"ag_wait_{tag}")

    one = lambda it: [_Item(it.kind, it.rows, it.cols, 0, 0, it.swap)]
    tok = start("conv_in", one(it_conv[0]), [placed(conv_w_in, 0, it_conv[0], "conv_w_in")], fwd_all)
    tok = start("conv_out", one(it_conv[1]), [placed(conv_w_out, 0, it_conv[1], "conv_w_out", tok)], tok)
    tok = start("ffn0_in", one(it_ffn[0]), [placed(ffn_w_in, 0, it_ffn[0], "ffn_w_in0", tok)], tok)
    tok = start("ffn0_out", one(it_ffn[1]), [placed(ffn_w_out, 0, it_ffn[1], "ffn_w_out0", tok)], tok)
    tok = start("attn", it_attn, [placed(w_kv[None], 0, it_attn[0], "w_kv", tok),
                                  placed(attn_w_q, 0, it_attn[1], "attn_w_q", tok),
                                  placed(attn_w_o, 0, it_attn[2], "attn_w_o", tok)], tok)
    token = start("ffn1", it_ffn, [placed(ffn_w_in, 1, it_ffn[0], "ffn_w_in1", tok),
                                   placed(ffn_w_out, 1, it_ffn[1], "ffn_w_out1", tok)], tok)

    a1 = row(ng[0, 0] * (1.0 + sc1[0])) + token[0, 0]
    (h1,) = _norm_mod(x0, a1, row(sh1[0]), "l0_norm1")
    tab = _bias_table(rel_bias[0], "l1_bias_table")
    h1, tab = lax.optimization_barrier((h1, tab))
    (W_cin,) = arrived("conv_in", h1)
    bcx = _mm(h1, W_cin, "nn", BF16, "l0_conv_in", b_layer=0, tm=512, tn=3 * D)
    ug = _conv_gate(bcx, ck8, "l0_conv_gate")
    gt1 = row(g1[0] * ng[0, 1])
    a2 = row(ng[0, 2] * (1.0 + sc2[0]))
    (W_cout,) = arrived("conv_out", ug)
    y1, x1, h2 = _mm_post(ug, W_cout, x0, gt1, "l0_conv_out", scales=a2, shifts=row(sh2[0]))
    (W_fin0,) = arrived("ffn0_in", h2)
    gu0, act0 = _ffn_in_act(h2, W_fin0, 0, "l0_ffn_in")
    (W_fout0,) = arrived("ffn0_out", act0)
    gt2 = row(g2[0] * ng[0, 3])
    a3 = ng[1, 0] * (1.0 + sc1[1])
    akv = kv_norm_g * (1.0 + kv_sc)
    y2, x2, h3, hkv = _mm_post(act0, W_fout0, x1, gt2, "l0_ffn_out",
                               scales=jnp.stack([a3, akv]), shifts=jnp.stack([sh1[1], kv_sh]))
    W_kv, W_q, W_o = arrived("attn", hkv)
    kvp = _mm(hkv, W_kv, "nn", BF16, "l1_kv", b_layer=0, tm=512, tn=2 * D)
    att_scale = (D // N_HEADS) ** -0.5
    assert math.log2(att_scale) % 1 == 0, "scaling q before its bf16 cast is exact only for a power of two"
    qp = _mm(h3, W_q, "nn", BF16, "l1_q", b_layer=0, scale=att_scale)
    oh = _attn_fwd(qp, kvp, tab, "l1_attn")
    gt3 = row(g1[1] * ng[1, 1])
    a4 = row(ng[1, 2] * (1.0 + sc2[1]))
    y3, x3, h4 = _mm_post(oh, W_o, x2, gt3, "l1_attn_out", scales=a4, shifts=row(sh2[1]))
    W_fin1, W_fout1 = arrived("ffn1", h4)
    gu1, act1 = _ffn_in_act(h4, W_fin1, 0, "l1_ffn_in")
    gt4 = row(g2[1] * ng[1, 3])
    dx4, sq, dy4, dgt4 = _mm_post(act1, W_fout1, x3, gt4, "l1_ffn_out", target=target)
    loss_part = 0.5 * jnp.sum(sq) / D

    def ffn_bwd(dy, dxn, xin_, h, gu, act, a, w_in, w_out, post, tag):
        dgu, dx, ds, db, dyn, dgt = _ffn_bwd(dy, w_out, gu, w_in, xin_, dxn, a, post, f"{tag}_ffn_bwd")
        g_fout = _mm(act, dy, "tn", BF16, f"{tag}_ffn_out_dw", tm=TF)
        g_fin = _mm(h, dgu, "tn", BF16, f"{tag}_ffn_in_dw", tn=TF)
        return dx, ds, db, dyn, dgt, g_fin, g_fout

    dx3, ds4, db4, dy3, dgt3, G_fin1, G_fout1 = ffn_bwd(dy4, dx4, x3, h4, gu1, act1, a4, W_fin1, W_fout1,
                                                        (y3, gt3), "l1")
    red = [_pair_start([G_fin1, G_fout1], it_ffn, token, "ffn1", ["ffn_w_in", "ffn_w_out"], layer=1)]
    doh = _mm(dy3, W_o, "nt", BF16, "l1_attn_out_dx", b_layer=0, after=red[0].token)
    G_o = _mm(oh, dy3, "tn", BF16, "l1_attn_out_dw")
    _owner_start(red[0], G_o)
    dq, dkv, dtab = _attn_bwd(qp, kvp, tab, doh, "l1_attn_bwd")
    d_rel = _bias_table_grad(dtab)
    G_q = _mm(h3, dq, "tn", BF16, "l1_q_dw")
    G_kv = _mm(hkv, dkv, "tn", BF16, "l1_kv_dw")
    red.append(_pair_start([G_kv, G_q, G_o], it_attn, red[-1].token, "attn", ["w_kv", "attn_w_q", "attn_w_o"]))
    dx2, ds3, db3, dy2, dgt2 = _mm_pre_bwd([(dq, W_q), (dkv, W_kv)], x2, dx3,
                                           jnp.stack([a3, akv]) + red[1].token[0, 0], "l1_qkv_dx", post=(y2, gt2))
    _owner_start(red[1], dx2)

    dx1, ds2, db2, dy1, dgt1, G_fin0, G_fout0 = ffn_bwd(dy2, dx2, x1, h2, gu0, act0, a2, W_fin0, W_fout0,
                                                        (y1, gt1), "l0")
    red.append(_pair_start([G_fin0, G_fout0], it_ffn, red[-1].token, "ffn0", ["ffn_w_in", "ffn_w_out"], layer=0))
    dug = _mm(dy1, W_cout, "nt", BF16, "l0_conv_out_dx", b_layer=0, after=red[2].token)
    G_cout = _mm(ug, dy1, "tn", BF16, "l0_conv_out_dw")
    dbcx, dck = _conv_gate_bwd(dug, bcx, ck8, "l0_conv_gate_bwd")
    _owner_start(red[2], dbcx)
    G_cin = _mm(h1, dbcx, "tn", BF16, "l0_conv_in_dw")
    red.append(_pair_start([G_cin, G_cout], it_conv, red[-1].token, "conv", ["conv_w_in", "conv_w_out"]))
    dx0, ds1, db1 = _mm_pre_bwd([(dbcx, W_cin)], x0, dx1, a1 + red[3].token[0, 0], "l0_conv_in_dx")
    ds1, db1 = _sum8(ds1)[0], _sum8(db1)[0]
    da2, db2 = _sum8(ds2)[0], _sum8(db2)[0]
    ds3, db3 = _sum8(ds3), _sum8(db3)
    da4, db4 = _sum8(ds4)[0], _sum8(db4)[0]
    dgt1, dgt2, dgt3, dgt4 = _sum8(dgt1), _sum8(dgt2), _sum8(dgt3), _sum8(dgt4)

    def dmod_of(l, ds_a, db_a, dgt_a, ds_b, db_b, dgt_b):
        return jnp.concatenate([db_a, ds_a * ng[l, 0], dgt_a * ng[l, 1], db_b, ds_b * ng[l, 2], dgt_b * ng[l, 3]])

    dmod0 = dmod_of(0, ds1, db1, dgt1, da2, db2, dgt2)
    dmod1 = dmod_of(1, ds3[0], db3[0], dgt3, da4, db4, dgt4)
    dkvmod = jnp.concatenate([db3[1], ds3[1] * kv_norm_g])
    dng = jnp.stack([
        jnp.stack([ds1 * (1.0 + sc1[0]), dgt1 * g1[0], da2 * (1.0 + sc2[0]), dgt2 * g2[0]]),
        jnp.stack([ds3[0] * (1.0 + sc1[1]), dgt3 * g1[1], da4 * (1.0 + sc2[1]), dgt4 * g2[1]])])
    dkvng = ds3[1] * (1.0 + kv_sc)
    small = [dmod0, dmod1, dkvmod, dng.reshape(-1), dkvng, _sum8(dck).reshape(-1), d_rel.reshape(-1),
             loss_part.reshape(1)]
    sizes = [int(s.shape[0]) for s in small]
    offs = np.concatenate([[0], np.cumsum(sizes)])
    bwd_all = _gather_flat(jnp.concatenate(small), "ag_bwd_small")
    _owner_start(red[3], bwd_all)
    Lb = bwd_all.shape[1]
    Lp = -(-Lb // 128) * 128
    tot = _sum_rows(jnp.pad(bwd_all, ((0, 0), (0, Lp - Lb))), "sum_small")[0]
    seg = lambda i: tot[offs[i]:offs[i + 1]]
    g_mod_b = jnp.stack([seg(0), seg(1)])
    g_kv_mod_b = seg(2)
    g_norm_g = lax.dynamic_slice_in_dim(seg(3).reshape(2, 4, D), chip * dsh, dsh, axis=2)
    g_kv_norm_g = seg(4)
    g_conv_k = lax.dynamic_slice_in_dim(seg(5).reshape(1, 3, D), chip * dsh, dsh, axis=2)
    g_rel_bias = seg(6).reshape(rel_bias.shape)
    loss = seg(7)[0]

    def dmod_rows(i, n):
        rows_ = lax.dynamic_slice_in_dim(bwd_all[:, offs[i]:offs[i + 1]], chip * n, n, axis=1)
        return jnp.pad(rows_, ((0, 8), (0, 0)))

    g_mod_w = _mm(sc16, jnp.concatenate([dmod_rows(0, n_mod), dmod_rows(1, n_mod)], axis=1), "tn", F32,
                  "mod_bwd", out_layers=STACKED_LAYERS)
    g_kv_mod_w = _mm(sc16, dmod_rows(2, n_kvm), "tn", F32, "mod_bwd_kv")

    grads = {
        "mod_w": g_mod_w, "mod_b": g_mod_b, "norm_g": g_norm_g, "conv_k": g_conv_k,
        "kv_mod_w": g_kv_mod_w, "kv_mod_b": g_kv_mod_b, "kv_norm_g": g_kv_norm_g, "rel_bias": g_rel_bias,
    }
    weights = dict(mod_w=mod_w, mod_b=mod_b, norm_g=norm_g, ffn_w_in=ffn_w_in, ffn_w_out=ffn_w_out,
                   conv_w_in=conv_w_in, conv_k=conv_k, conv_w_out=conv_w_out, kv_mod_w=kv_mod_w,
                   kv_mod_b=kv_mod_b, kv_norm_g=kv_norm_g, w_kv=w_kv, attn_w_q=attn_w_q, attn_w_o=attn_w_o,
                   rel_bias=rel_bias)
    m_in = dict(mod_w=m_mod_w, mod_b=m_mod_b, norm_g=m_norm_g, ffn_w_in=m_ffn_w_in, ffn_w_out=m_ffn_w_out,
                conv_w_in=m_conv_w_in, conv_k=m_conv_k, conv_w_out=m_conv_w_out, kv_mod_w=m_kv_mod_w,
                kv_mod_b=m_kv_mod_b, kv_norm_g=m_kv_norm_g, w_kv=m_w_kv, attn_w_q=m_attn_w_q,
                attn_w_o=m_attn_w_o, rel_bias=m_rel_bias)
    v_in = dict(mod_w=v_mod_w, mod_b=v_mod_b, norm_g=v_norm_g, ffn_w_in=v_ffn_w_in, ffn_w_out=v_ffn_w_out,
                conv_w_in=v_conv_w_in, conv_k=v_conv_k, conv_w_out=v_conv_w_out, kv_mod_w=v_kv_mod_w,
                kv_mod_b=v_kv_mod_b, kv_norm_g=v_kv_norm_g, w_kv=v_w_kv, attn_w_q=v_attn_w_q,
                attn_w_o=v_attn_w_o, rel_bias=v_rel_bias)
    names = list(weights)
    step = {}

    def update(n):
        g = grads[n].reshape(weights[n].shape)
        step[n] = (g, *_adamw(weights[n], g, m_in[n], v_in[n], f"adamw_{n}"))

    update("mod_w")
    reduced, (half_send, half_recv, half_bufs, _) = _reduce_finish(red, step["mod_w"][1])
    local = [n for n in grads if n != "mod_w"]
    for n in local:
        update(n)
    grads.update(zip(reduced, _split_wait(
        _half_copies(len(half_bufs)), half_send, half_recv, half_bufs, [step[n][1] for n in local], "rs_half_wait")))
    for n in names:
        if n not in step:
            update(n)
    return (loss, dx0.reshape(x.shape), *[step[n][k] for k in range(4) for n in names])
```

```python
import functools
import math

import numpy as np
import jax
import jax.numpy as jnp
from jax import lax
from jax.experimental import pallas as pl
from jax.experimental.pallas import tpu as pltpu

CHUNK = 64
N_LEFT_CHUNKS = 8
N_HEADS = 16
MAX_REL = 2 * CHUNK
N_REL = 2 * MAX_REL + 1
EPS = 1e-6
ADAM_LR = 0.001
ADAM_B1 = 0.9
ADAM_B2 = 0.999
ADAM_EPS = 1e-08
ADAM_WD = 0.01
ADAM_STEP = 10

Q_CHUNKS = 4
BQ = Q_CHUNKS * CHUNK
N_WIN = 1 + N_LEFT_CHUNKS // Q_CHUNKS
HEADS_PER_STEP = 8
NEG = -1e30
N_DEV = 8
N_CHIP = 4
SMALL_TENSOR_ELEMS = 1 << 16
PIECE_ROWS = 256

BF16 = jnp.bfloat16
F32 = jnp.float32
V7X_VMEM_LIMIT_BYTES = 56 * 1024 * 1024
MESH = pl.DeviceIdType.MESH


def _pick(n, pref, align):
    t = min(pref, n)
    t -= t % align
    while t >= align:
        if n % t == 0:
            return t
        t -= align
    return n


def _params(*sem):
    return pltpu.CompilerParams(dimension_semantics=sem, vmem_limit_bytes=V7X_VMEM_LIMIT_BYTES)


def _colsum8(v):
    r, d = v.shape
    return v.reshape(r // 8, 8, d).sum(axis=0)


_DIMS = {"nn": (((1,), (0,)), ((), ())), "nt": (((1,), (1,)), ((), ())), "tn": (((0,), (0,)), ((), ()))}


def _mm(a, b, mode, out_dtype, name, *, b_layer=None, tm=1024, tn=1024, tk=None, scale=None, after=None,
        out_layers=1):
    if tk is None:
        tk = 2048 if mode == "tn" else 3072
    bs = b.shape[1:] if b_layer is not None else b.shape
    if mode == "nn":
        (M, K), (K2, N) = a.shape, bs
    elif mode == "nt":
        (M, K), (N, K2) = a.shape, bs
    else:
        (K, M), (K2, N) = a.shape, bs
    assert K == K2, (name, a.shape, b.shape)
    tm = _pick(M, tm, 128 if mode == "tn" else 16)
    tn = _pick(N // out_layers, tn, 128)
    tk = _pick(K, tk, 128 if mode != "tn" else 16)
    nk = K // tk
    assert scale is None or nk == 1, name
    dims = _DIMS[mode]
    extra = [] if after is None else [after]

    def body(a_ref, b_ref, *rest):
        o_ref, acc = rest[len(extra)], rest[len(extra) + 1:]
        p = lax.dot_general(a_ref[...].astype(BF16), b_ref[...].astype(BF16), dims,
                            preferred_element_type=F32)
        if nk == 1:
            o_ref[...] = (p if scale is None else p * scale).astype(o_ref.dtype)
        else:
            k = pl.program_id(2)

            @pl.when(k == 0)
            def _():
                acc[0][...] = p

            @pl.when(k > 0)
            def _():
                acc[0][...] += p

            @pl.when(k == nk - 1)
            def _():
                o_ref[...] = acc[0][...].astype(o_ref.dtype)

    a_spec = (pl.BlockSpec((tk, tm), lambda i, j, k: (k, i)) if mode == "tn"
              else pl.BlockSpec((tm, tk), lambda i, j, k: (i, k)))
    if mode == "nt":
        b_blk, b_idx = (tn, tk), (lambda i, j, k: (j, k))
    else:
        b_blk, b_idx = (tk, tn), (lambda i, j, k: (k, j))
    if b_layer is not None:
        b_spec = pl.BlockSpec((None,) + b_blk, lambda i, j, k: (b_layer,) + b_idx(i, j, k))
    else:
        b_spec = pl.BlockSpec(b_blk, b_idx)
    if out_layers > 1:
        per_layer = N // out_layers // tn
        o_spec = pl.BlockSpec((None, tm, tn), lambda i, j, k: (j // per_layer, i, j % per_layer))
        o_shape = (out_layers, M, N // out_layers)
    else:
        o_spec, o_shape = pl.BlockSpec((tm, tn), lambda i, j, k: (i, j)), (M, N)
    return pl.pallas_call(
        body, name=name,
        grid=(M // tm, N // tn, nk),
        in_specs=[a_spec, b_spec] + [pl.BlockSpec(memory_space=pl.ANY)] * len(extra),
        out_specs=o_spec,
        out_shape=jax.ShapeDtypeStruct(o_shape, out_dtype),
        scratch_shapes=[pltpu.VMEM((tm, tn), F32)] if nk > 1 else [],
        compiler_params=_params("parallel", "parallel", "arbitrary"),
    )(a, b, *extra)


def _row_spec(tm, d):
    return pl.BlockSpec((tm, d), lambda i: (i, 0))


def _vec_spec(r, d):
    return pl.BlockSpec((r, d), lambda i: (0, 0))


def _norm_mod(x, scales, shifts, name):
    S, D = x.shape
    nb = scales.shape[0]
    tm = _pick(S, 1024, 16)

    def body(x_ref, a_ref, b_ref, *o_refs):
        xv = x_ref[...]
        xh = xv * lax.rsqrt(jnp.mean(xv * xv, axis=-1, keepdims=True) + EPS)
        for n in range(nb):
            o_refs[n][...] = (xh * a_ref[n:n + 1, :] + b_ref[n:n + 1, :]).astype(BF16)

    return pl.pallas_call(
        body, name=name, grid=(S // tm,),
        in_specs=[_row_spec(tm, D), _vec_spec(nb, D), _vec_spec(nb, D)],
        out_specs=[_row_spec(tm, D)] * nb,
        out_shape=[jax.ShapeDtypeStruct((S, D), BF16)] * nb,
        compiler_params=_params("parallel"),
    )(x, scales, shifts)


def _mm_post(a, w, x, gate, name, *, scales=None, shifts=None, target=None):
    M, K = a.shape
    D = w.shape[2]
    tm = _pick(M, 1024 if K <= D else 512, 16)
    sub = _pick(tm, PIECE_ROWS, 16)
    nb = 0 if scales is None else scales.shape[0]

    def body(a_ref, w_ref, x_ref, g_ref, *rest):
        if target is None:
            sc_ref, sh_ref, y_ref, xn_ref = rest[:4]
            h_refs = rest[4:]
        else:
            t_ref, dx_ref, sq_ref, dy_ref, dg_ref = rest

            @pl.when(pl.program_id(0) == 0)
            def _():
                sq_ref[...] = jnp.zeros_like(sq_ref)
                dg_ref[...] = jnp.zeros_like(dg_ref)

        def product(r):
            return jnp.dot(a_ref[pl.ds(r * sub, sub), :], w_ref[...], preferred_element_type=F32)

        y = product(0)
        for r in range(tm // sub):
            rows = pl.ds(r * sub, sub)
            yb = y.astype(BF16)
            if r + 1 < tm // sub:
                y = product(r + 1)
            yv = yb.astype(F32)
            yh = yv * lax.rsqrt(jnp.mean(yv * yv, axis=-1, keepdims=True) + EPS)
            xn = x_ref[rows, :] + yh * g_ref[...]
            if target is None:
                y_ref[rows, :] = yb
                xn_ref[rows, :] = xn
                xh = xn * lax.rsqrt(jnp.mean(xn * xn, axis=-1, keepdims=True) + EPS)
                for n in range(nb):
                    h_refs[n][rows, :] = (xh * sc_ref[n:n + 1, :] + sh_ref[n:n + 1, :]).astype(BF16)
            else:
                e = xn - t_ref[rows, :]
                dx = e / D
                dx_ref[rows, :] = dx
                sq_ref[...] += _colsum8(e * e)
                dy, dxy = _post_norm_grad(dx, yb, g_ref[...])
                dy_ref[rows, :] = dy.astype(BF16)
                dg_ref[...] += _colsum8(dxy)

    ins = [a, w, x, gate]
    in_specs = [_row_spec(tm, K), pl.BlockSpec((None, K, D), lambda i: (0, 0, 0)), _row_spec(tm, D), _vec_spec(1, D)]
    if target is None:
        ins += [scales, shifts]
        in_specs += [_vec_spec(nb, D), _vec_spec(nb, D)]
        out_specs = [_row_spec(tm, D)] * (2 + nb)
        out_shape = [jax.ShapeDtypeStruct((M, D), BF16), jax.ShapeDtypeStruct((M, D), F32)] \
            + [jax.ShapeDtypeStruct((M, D), BF16)] * nb
    else:
        ins += [target]
        in_specs += [_row_spec(tm, D)]
        out_specs = [_row_spec(tm, D), _vec_spec(8, D), _row_spec(tm, D), _vec_spec(8, D)]
        out_shape = [jax.ShapeDtypeStruct((M, D), F32), jax.ShapeDtypeStruct((8, D), F32),
                     jax.ShapeDtypeStruct((M, D), BF16), jax.ShapeDtypeStruct((8, D), F32)]
    return pl.pallas_call(
        body, name=name, grid=(M // tm,), in_specs=in_specs, out_specs=out_specs, out_shape=out_shape,
        compiler_params=_params("arbitrary" if target is not None else "parallel"),
    )(*ins)


def _post_norm_grad(dxn, yb, gate):
    yv = yb.astype(F32)
    r = lax.rsqrt(jnp.mean(yv * yv, axis=-1, keepdims=True) + EPS)
    yh = yv * r
    dyh = dxn * gate
    return r * (dyh - yh * jnp.mean(dyh * yh, axis=-1, keepdims=True)), dxn * yh


def _mm_pre_bwd(pairs, x, dxn, scales, name, post=None):
    S, D = x.shape
    nb = len(pairs)
    tm = _pick(S, 512, 16)
    sub = _pick(tm, PIECE_ROWS, 16)

    def body(*refs):
        a_refs, w_refs = refs[0:2 * nb:2], refs[1:2 * nb:2]
        x_ref, d_ref, sc_ref = refs[2 * nb:2 * nb + 3]
        rest = refs[2 * nb + 3:]
        if post is not None:
            y_ref, g_ref, dx_ref, ds_ref, db_ref, dy_ref, dg_ref = rest
        else:
            dx_ref, ds_ref, db_ref = rest

        @pl.when(pl.program_id(0) == 0)
        def _():
            ds_ref[...] = jnp.zeros_like(ds_ref)
            db_ref[...] = jnp.zeros_like(db_ref)
            if post is not None:
                dg_ref[...] = jnp.zeros_like(dg_ref)

        def products(r):
            return [lax.dot_general(a_refs[n][pl.ds(r * sub, sub), :], w_refs[n][...], _DIMS["nt"],
                                    preferred_element_type=F32) for n in range(nb)]

        nxt = products(0)
        for r in range(tm // sub):
            rows = pl.ds(r * sub, sub)
            dhs = nxt
            if r + 1 < tm // sub:
                nxt = products(r + 1)
            xv = x_ref[rows, :]
            rr = lax.rsqrt(jnp.mean(xv * xv, axis=-1, keepdims=True) + EPS)
            xh = xv * rr
            dxh = jnp.zeros_like(xv)
            for n in range(nb):
                dh = dhs[n]
                dxh = dxh + dh * sc_ref[n:n + 1, :]
                ds_ref[n] += _colsum8(dh * xh)
                db_ref[n] += _colsum8(dh)
            dx = d_ref[rows, :] + rr * (dxh - xh * jnp.mean(dxh * xh, axis=-1, keepdims=True))
            dx_ref[rows, :] = dx
            if post is not None:
                dy, dxy = _post_norm_grad(dx, y_ref[rows, :], g_ref[...])
                dy_ref[rows, :] = dy.astype(BF16)
                dg_ref[...] += _colsum8(dxy)

    ins, in_specs = [], []
    for a, w in pairs:
        ins += [a, w]
        in_specs += [_row_spec(tm, a.shape[1]),
                     pl.BlockSpec((None, D, a.shape[1]), lambda i: (0, 0, 0), pipeline_mode=pl.Buffered(1))]
    ins += [x, dxn, scales]
    in_specs += [_row_spec(tm, D), _row_spec(tm, D), _vec_spec(nb, D)]
    acc_spec = pl.BlockSpec((nb, 8, D), lambda i: (0, 0, 0))
    out_specs = [_row_spec(tm, D), acc_spec, acc_spec]
    out_shape = [jax.ShapeDtypeStruct((S, D), F32), jax.ShapeDtypeStruct((nb, 8, D), F32),
                 jax.ShapeDtypeStruct((nb, 8, D), F32)]
    if post is not None:
        ins += list(post)
        in_specs += [_row_spec(tm, D), _vec_spec(1, D)]
        out_specs += [_row_spec(tm, D), _vec_spec(8, D)]
        out_shape += [jax.ShapeDtypeStruct((S, D), BF16), jax.ShapeDtypeStruct((8, D), F32)]
    return pl.pallas_call(
        body, name=name, grid=(S // tm,), in_specs=in_specs, out_specs=out_specs, out_shape=out_shape,
        compiler_params=_params("arbitrary"),
    )(*ins)


FFN_PAIRS = 2


def _ffn_in_act(h, w, layer, name):
    S, D = h.shape
    F2 = w.shape[2]
    PW = F2 // (2 * FFN_PAIRS)
    tm = _pick(S, 1024, 16)
    sub = _pick(tm, PIECE_ROWS, 16)

    def body(h_ref, w_ref, gu_ref, a_ref):
        def product(r):
            return jnp.dot(h_ref[pl.ds(r * sub, sub), :], w_ref[...], preferred_element_type=F32)

        nxt = product(0)
        for r in range(tm // sub):
            rows = pl.ds(r * sub, sub)
            acc = nxt
            if r + 1 < tm // sub:
                nxt = product(r + 1)
            gu_ref[rows, :] = acc.astype(BF16)
            g = acc[:, :PW]
            a_ref[rows, :] = (g * jax.nn.sigmoid(g) * acc[:, PW:]).astype(BF16)

    return pl.pallas_call(
        body, name=name, grid=(FFN_PAIRS, S // tm),
        in_specs=[pl.BlockSpec((tm, D), lambda p, i: (i, 0)),
                  pl.BlockSpec((None, D, 2 * PW), lambda p, i: (layer, 0, p))],
        out_specs=[pl.BlockSpec((tm, 2 * PW), lambda p, i: (i, p)), pl.BlockSpec((tm, PW), lambda p, i: (i, p))],
        out_shape=[jax.ShapeDtypeStruct((S, F2), BF16), jax.ShapeDtypeStruct((S, F2 // 2), BF16)],
        compiler_params=_params("parallel", "parallel"),
    )(h, w)


def _ffn_bwd(dy, w_out, gu, w_in, x, dxn, scale, post, name):
    S, D = dy.shape
    F2 = gu.shape[1]
    PW = F2 // (2 * FFN_PAIRS)
    tm = _pick(S, 256, 16)

    def body(dy_ref, wo_ref, gu_ref, wi_ref, x_ref, d_ref, sc_ref, y_ref, g_ref,
             dgu_ref, dx_ref, ds_ref, db_ref, dyn_ref, dg_ref):
        @pl.when(pl.program_id(0) == 0)
        def _():
            ds_ref[...] = jnp.zeros_like(ds_ref)
            db_ref[...] = jnp.zeros_like(db_ref)
            dg_ref[...] = jnp.zeros_like(dg_ref)

        def first_product(p):
            return lax.dot_general(dy_ref[...], wo_ref[p * PW:(p + 1) * PW, :], _DIMS["nt"],
                                   preferred_element_type=F32)

        dh = jnp.zeros((tm, D), F32)
        nxt = first_product(0)
        for p in range(FFN_PAIRS):
            cols = slice(2 * p * PW, 2 * (p + 1) * PW)
            da = nxt
            if p + 1 < FFN_PAIRS:
                nxt = first_product(p + 1)
            g = gu_ref[:, 2 * p * PW:(2 * p + 1) * PW].astype(F32)
            u = gu_ref[:, (2 * p + 1) * PW:2 * (p + 1) * PW].astype(F32)
            sg = jax.nn.sigmoid(g)
            dgu_ref[:, 2 * p * PW:(2 * p + 1) * PW] = (da * u * (sg * (1.0 + g * (1.0 - sg)))).astype(BF16)
            dgu_ref[:, (2 * p + 1) * PW:2 * (p + 1) * PW] = (da * (g * sg)).astype(BF16)
            dh = dh + lax.dot_general(dgu_ref[:, cols], wi_ref[:, cols], _DIMS["nt"], preferred_element_type=F32)
        xv = x_ref[...]
        rr = lax.rsqrt(jnp.mean(xv * xv, axis=-1, keepdims=True) + EPS)
        xh = xv * rr
        dxh = dh * sc_ref[...]
        ds_ref[0] += _colsum8(dh * xh)
        db_ref[0] += _colsum8(dh)
        dx = d_ref[...] + rr * (dxh - xh * jnp.mean(dxh * xh, axis=-1, keepdims=True))
        dx_ref[...] = dx
        dyn, dxy = _post_norm_grad(dx, y_ref[...], g_ref[...])
        dyn_ref[...] = dyn.astype(BF16)
        dg_ref[...] += _colsum8(dxy)

    resident = dict(pipeline_mode=pl.Buffered(1))
    acc_spec = pl.BlockSpec((1, 8, D), lambda i: (0, 0, 0))
    return pl.pallas_call(
        body, name=name, grid=(S // tm,),
        in_specs=[_row_spec(tm, D), pl.BlockSpec((None, F2 // 2, D), lambda i: (0, 0, 0), **resident),
                  _row_spec(tm, F2), pl.BlockSpec((None, D, F2), lambda i: (0, 0, 0), **resident),
                  _row_spec(tm, D), _row_spec(tm, D), _vec_spec(1, D), _row_spec(tm, D), _vec_spec(1, D)],
        out_specs=[_row_spec(tm, F2), _row_spec(tm, D), acc_spec, acc_spec, _row_spec(tm, D), _vec_spec(8, D)],
        out_shape=[jax.ShapeDtypeStruct((S, F2), BF16), jax.ShapeDtypeStruct((S, D), F32),
                   jax.ShapeDtypeStruct((1, 8, D), F32), jax.ShapeDtypeStruct((1, 8, D), F32),
                   jax.ShapeDtypeStruct((S, D), BF16), jax.ShapeDtypeStruct((8, D), F32)],
        compiler_params=_params("arbitrary"),
    )(dy, w_out, gu, w_in, x, dxn, scale, *post)


HALO = 16


def _conv_terms(bcx_ref, prev_ref, i, tm, D):
    b = bcx_ref[:, 0:D].astype(F32)
    cg = bcx_ref[:, D:2 * D].astype(F32)
    xin = bcx_ref[:, 2 * D:3 * D].astype(F32)
    z = cg * xin
    zp = prev_ref[:, D:2 * D].astype(F32) * prev_ref[:, 2 * D:3 * D].astype(F32)
    zp = jnp.where(i > 0, zp, 0.0)
    z_ext = jnp.concatenate([zp, z], axis=0)
    z1 = pltpu.roll(z_ext, 1, 0)[HALO:, :]
    z2 = pltpu.roll(z_ext, 2, 0)[HALO:, :]
    return b, cg, xin, z, z1, z2


def _conv_gate(bcx, ck, name):
    S, D3 = bcx.shape
    D = D3 // 3
    tm = _pick(S, 512, 16)
    hb = tm // HALO

    def body(bcx_ref, prev_ref, ck_ref, o_ref):
        i = pl.program_id(0)
        b, _, _, z, z1, z2 = _conv_terms(bcx_ref, prev_ref, i, tm, D)
        conv = ck_ref[0:1, :] * z2 + ck_ref[1:2, :] * z1 + ck_ref[2:3, :] * z
        o_ref[...] = (b * conv).astype(BF16)

    return pl.pallas_call(
        body, name=name, grid=(S // tm,),
        in_specs=[_row_spec(tm, D3),
                  pl.BlockSpec((HALO, D3), lambda i: (jnp.maximum(i * hb - 1, 0), 0)),
                  _vec_spec(8, D)],
        out_specs=_row_spec(tm, D),
        out_shape=jax.ShapeDtypeStruct((S, D), BF16),
        compiler_params=_params("parallel"),
    )(bcx, bcx, ck)


def _conv_gate_bwd(du, bcx, ck, name):
    S, D3 = bcx.shape
    D = D3 // 3
    tm = _pick(S, 512, 16)
    hb = tm // HALO
    nt = S // tm

    def body(du_ref, dun_ref, bcx_ref, prev_ref, next_ref, ck_ref, o_ref, dk_ref):
        i = pl.program_id(0)
        b, cg, xin, z, z1, z2 = _conv_terms(bcx_ref, prev_ref, i, tm, D)
        k0, k1, k2 = ck_ref[0:1, :], ck_ref[1:2, :], ck_ref[2:3, :]
        conv = k0 * z2 + k1 * z1 + k2 * z
        d = du_ref[...].astype(F32)
        dconv = d * b
        dcn = jnp.where(i < nt - 1, dun_ref[...].astype(F32) * next_ref[:, 0:D].astype(F32), 0.0)
        d_ext = jnp.concatenate([dconv, dcn], axis=0)
        d1 = pltpu.roll(d_ext, tm + HALO - 1, 0)[:tm, :]
        d2 = pltpu.roll(d_ext, tm + HALO - 2, 0)[:tm, :]
        dz = k2 * dconv + k1 * d1 + k0 * d2
        o_ref[:, 0:D] = (d * conv).astype(BF16)
        o_ref[:, D:2 * D] = (dz * xin).astype(BF16)
        o_ref[:, 2 * D:3 * D] = (dz * cg).astype(BF16)

        @pl.when(i == 0)
        def _():
            dk_ref[...] = jnp.zeros_like(dk_ref)

        dk_ref[0] += _colsum8(dconv * z2)
        dk_ref[1] += _colsum8(dconv * z1)
        dk_ref[2] += _colsum8(dconv * z)

    last = S // HALO - 1
    return pl.pallas_call(
        body, name=name, grid=(nt,),
        in_specs=[_row_spec(tm, D),
                  pl.BlockSpec((HALO, D), lambda i: (jnp.minimum((i + 1) * hb, last), 0)),
                  _row_spec(tm, D3),
                  pl.BlockSpec((HALO, D3), lambda i: (jnp.maximum(i * hb - 1, 0), 0)),
                  pl.BlockSpec((HALO, D3), lambda i: (jnp.minimum((i + 1) * hb, last), 0)),
                  _vec_spec(8, D)],
        out_specs=[_row_spec(tm, D3), pl.BlockSpec((3, 8, D), lambda i: (0, 0, 0))],
        out_shape=[jax.ShapeDtypeStruct((S, D3), BF16), jax.ShapeDtypeStruct((3, 8, D), F32)],
        compiler_params=_params("arbitrary"),
    )(du, du, bcx, bcx, bcx, ck)


def _rel_onehot():
    a = np.arange(CHUNK)[:, None]
    b = np.arange(CHUNK)[None, :]
    idx = np.stack([np.clip((N_LEFT_CHUNKS - dl) * CHUNK + a - b, -MAX_REL, MAX_REL) + MAX_REL
                    for dl in (6, 7, 8)]).reshape(-1)
    return (jnp.asarray(idx)[:, None] == jnp.arange(N_REL)[None, :]).astype(F32)


def _bias_table(rel_bias, name):
    H = rel_bias.shape[0]
    near = jnp.dot(rel_bias, _rel_onehot().T, precision=lax.Precision.HIGHEST).reshape(H, 3, CHUNK, CHUNK)
    far = jnp.broadcast_to(rel_bias[:, N_REL - 1][:, None, None], (H, CHUNK, CHUNK))

    def body(near_ref, far_ref, o_ref):
        neg = jnp.full((CHUNK, CHUNK), NEG, F32)
        for v in range(N_WIN):
            for ic in range(Q_CHUNKS):
                for jc in range(N_WIN * Q_CHUNKS):
                    dl = jc - ic
                    if dl < 0 or dl > N_LEFT_CHUNKS or jc < (N_WIN - 1 - v) * Q_CHUNKS:
                        blk = neg
                    else:
                        blk = far_ref[...] if dl <= 5 else near_ref[dl - 6]
                    o_ref[v, ic * CHUNK:(ic + 1) * CHUNK, jc * CHUNK:(jc + 1) * CHUNK] = blk

    return pl.pallas_call(
        body, name=name, grid=(H,),
        in_specs=[pl.BlockSpec((None, 3, CHUNK, CHUNK), lambda h: (h, 0, 0, 0)),
                  pl.BlockSpec((None, CHUNK, CHUNK), lambda h: (h, 0, 0))],
        out_specs=pl.BlockSpec((N_WIN, None, BQ, N_WIN * BQ), lambda h: (0, h, 0, 0)),
        out_shape=jax.ShapeDtypeStruct((N_WIN, H, BQ, N_WIN * BQ), F32),
        compiler_params=_params("parallel"),
    )(near, far)


NEAR_FIRST = 6
SLAB_ROWS = 2 * CHUNK
SLAB_COLS = 4 * CHUNK


def _slab(pair):
    c0 = (NEAR_FIRST + 2 * pair) * CHUNK
    return slice(pair * SLAB_ROWS, (pair + 1) * SLAB_ROWS), slice(c0, c0 + SLAB_COLS)


def _bias_table_grad(dslab):
    H = dslab.shape[0]

    def blk(ic, dl):
        pair, r, col = ic // 2, ic % 2, ic + dl - NEAR_FIRST - 2 * (ic // 2)
        return dslab[:, pair, r * CHUNK:(r + 1) * CHUNK, col * CHUNK:(col + 1) * CHUNK]

    by_dl = [sum(blk(ic, dl) for ic in range(Q_CHUNKS)) for dl in (6, 7, 8)]
    near = jnp.stack(by_dl, axis=1).reshape(H, 3 * CHUNK * CHUNK)
    g = jnp.dot(near, _rel_onehot(), precision=lax.Precision.HIGHEST)
    return g.at[:, N_REL - 1].add(-jnp.sum(near, axis=1))


def _attn_specs(nblk, W):
    last = nblk - 1
    q_spec = pl.BlockSpec((BQ, W), lambda g, i: (jnp.minimum(i, last), g))
    kv_specs = [pl.BlockSpec((BQ, 2 * W), functools.partial(
        lambda g, i, w: (jnp.maximum(jnp.minimum(i, last) - (N_WIN - 1) + w, 0), g), w=w)) for w in range(N_WIN)]
    tab_spec = pl.BlockSpec((None, HEADS_PER_STEP, BQ, N_WIN * BQ),
                            lambda g, i: (jnp.minimum(i, N_WIN - 1), g, 0, 0))
    dtab_spec = pl.BlockSpec((HEADS_PER_STEP, Q_CHUNKS // 2, SLAB_ROWS, SLAB_COLS), lambda g, i: (g, 0, 0, 0))
    return q_spec, kv_specs, tab_spec, dtab_spec


def _attn_scores(q_ref, kT, tab_ref, h, dh):
    return jnp.dot(q_ref[:, h * dh:(h + 1) * dh], kT[h * dh:(h + 1) * dh, :], preferred_element_type=F32) + tab_ref[h]


def _attn_fwd(q, kv, tab, name):
    S, D = q.shape
    dh = D // N_HEADS
    W = HEADS_PER_STEP * dh
    assert 2 * W == D, "the kv layout puts one head group's k beside its v: two head groups"
    q_spec, kv_specs, tab_spec, _ = _attn_specs(S // BQ, W)

    def body(q_ref, *rest):
        tab_ref, o_ref = rest[N_WIN], rest[N_WIN + 1]
        kvw = jnp.concatenate([r[...] for r in rest[:N_WIN]], axis=0)
        kT = kvw[:, :W].T
        vw = kvw[:, W:]
        outs = []
        s = _attn_scores(q_ref, kT, tab_ref, 0, dh)
        for h in range(HEADS_PER_STEP):
            s_next = _attn_scores(q_ref, kT, tab_ref, h + 1, dh) if h + 1 < HEADS_PER_STEP else None
            e = jnp.exp(s - jnp.max(s, axis=-1, keepdims=True))
            l = jnp.sum(e, axis=-1, keepdims=True)
            outs.append(jnp.dot(e.astype(BF16), vw[:, h * dh:(h + 1) * dh], preferred_element_type=F32) / l)
            s = s_next
        o_ref[...] = jnp.concatenate(outs, axis=1).astype(BF16)

    return pl.pallas_call(
        body, name=name, grid=(N_HEADS // HEADS_PER_STEP, S // BQ),
        in_specs=[q_spec] + kv_specs + [tab_spec],
        out_specs=q_spec,
        out_shape=jax.ShapeDtypeStruct((S, D), BF16),
        compiler_params=_params("parallel", "parallel"),
    )(q, *([kv] * N_WIN), tab)


def _attn_bwd(q, kv, tab, do, name):
    S, D = q.shape
    dh = D // N_HEADS
    W = HEADS_PER_STEP * dh
    nblk = S // BQ
    q_spec, kv_specs, tab_spec, dtab_spec = _attn_specs(nblk, W)

    def body(q_ref, *rest):
        tab_ref, do_ref, dq_ref, dkv_ref, dtab_ref, ring = rest[N_WIN:]
        i = pl.program_id(1)

        @pl.when(i == 0)
        def _():
            dtab_ref[...] = jnp.zeros_like(dtab_ref)
            ring[...] = jnp.zeros_like(ring)

        @pl.when(i < nblk)
        def _():
            kvw = jnp.concatenate([r[...] for r in rest[:N_WIN]], axis=0)
            kT = kvw[:, :W].T
            vw = kvw[:, W:]
            qT = q_ref[...].T
            dqs, dks, dvs = [], [], []

            s = _attn_scores(q_ref, kT, tab_ref, 0, dh)
            for h in range(HEADS_PER_STEP):
                hd = slice(h * dh, (h + 1) * dh)
                do_h = do_ref[:, hd]
                dp = lax.dot_general(do_h, vw[:, hd], _DIMS["nt"], preferred_element_type=F32)
                e = jnp.exp(s - jnp.max(s, axis=-1, keepdims=True))
                inv_l = 1.0 / jnp.sum(e, axis=-1, keepdims=True)
                if h + 1 < HEADS_PER_STEP:
                    s = _attn_scores(q_ref, kT, tab_ref, h + 1, dh)
                delta = jnp.sum(e * dp, axis=-1, keepdims=True) * inv_l
                ds = e * ((dp - delta) * inv_l)
                for pair in range(Q_CHUNKS // 2):
                    rows, cols = _slab(pair)
                    dtab_ref[h, pair] += ds[rows, cols]
                dsb = ds.astype(BF16)
                dqs.append(lax.dot_general(kT[hd, :], dsb, _DIMS["nt"], preferred_element_type=F32) * (dh ** -0.5))
                dks.append(jnp.dot(qT[hd, :], dsb, preferred_element_type=F32))
                do_s = (do_h.astype(F32) * inv_l).astype(BF16)
                dvs.append(jnp.dot(do_s.T, e.astype(BF16), preferred_element_type=F32))
            dq_ref[...] = jnp.concatenate(dqs, axis=0).T.astype(BF16)
            dkv = jnp.concatenate(dks + dvs, axis=0).T
            for w in range(N_WIN):
                slot = lax.rem(i + 1 + w, N_WIN)
                part = dkv[w * BQ:(w + 1) * BQ, :]
                if w == N_WIN - 1:
                    ring[slot] = part
                else:
                    ring[slot] += part

        dkv_ref[...] = ring[lax.rem(i + 1, N_WIN)].astype(BF16)

    done_spec = pl.BlockSpec((BQ, 2 * W), lambda g, i: (jnp.maximum(i - (N_WIN - 1), 0), g))
    return pl.pallas_call(
        body, name=name, grid=(N_HEADS // HEADS_PER_STEP, nblk + N_WIN - 1),
        in_specs=[q_spec] + kv_specs + [tab_spec, q_spec],
        out_specs=[q_spec, done_spec, dtab_spec],
        out_shape=[jax.ShapeDtypeStruct((S, D), BF16), jax.ShapeDtypeStruct((S, 2 * D), BF16),
                   jax.ShapeDtypeStruct((N_HEADS, Q_CHUNKS // 2, SLAB_ROWS, SLAB_COLS), F32)],
        scratch_shapes=[pltpu.VMEM((N_WIN, BQ, 2 * W), F32)],
        compiler_params=_params("parallel", "arbitrary"),
    )(q, *([kv] * N_WIN), tab, do)


def _adamw(w, g, m, v, name):
    shape = w.shape
    C = shape[-1]
    R = int(np.prod(shape[:-1])) if len(shape) > 1 else 1
    whole = len(shape) >= 2 and R * C <= SMALL_TENSOR_ELEMS
    if whole:
        w2, g2, m2, v2 = w, g, m, v
    else:
        w2, g2, m2, v2 = (t.reshape(R, C) for t in (w, g, m, v))
    tr = _pick(R, max(8, (512 * 1024) // C // 8 * 8), 8)

    def body(w_ref, g_ref, m_ref, v_ref, d_ref, nm_ref, nv_ref):
        gv = g_ref[...]
        nm = ADAM_B1 * m_ref[...] + (1.0 - ADAM_B1) * gv
        nv = ADAM_B2 * v_ref[...] + (1.0 - ADAM_B2) * jnp.square(gv)
        m_hat = nm / (1.0 - ADAM_B1 ** ADAM_STEP)
        v_hat = nv / (1.0 - ADAM_B2 ** ADAM_STEP)
        d_ref[...] = -ADAM_LR * (m_hat / (jnp.sqrt(v_hat) + ADAM_EPS) + ADAM_WD * w_ref[...])
        nm_ref[...] = nm
        nv_ref[...] = nv

    if whole:
        spec, grid = pl.BlockSpec(shape, lambda i: (0,) * len(shape)), (1,)
    else:
        spec, grid = pl.BlockSpec((tr, C), lambda i: (i, 0)), (R // tr,)
    outs = pl.pallas_call(
        body, name=name, grid=grid,
        in_specs=[spec] * 4, out_specs=[spec] * 3,
        out_shape=[jax.ShapeDtypeStruct(w2.shape, F32)] * 3,
        compiler_params=_params("parallel"),
    )(w2, g2, m2, v2)
    return tuple(o.reshape(shape) for o in outs)


def _sum_rows(a, name):
    n, L = a.shape

    def body(a_ref, o_ref):
        acc = a_ref[0:1, :]
        for r in range(1, n):
            acc = acc + a_ref[r:r + 1, :]
        o_ref[...] = acc

    return pl.pallas_call(
        body, name=name, grid=(1,),
        in_specs=[pl.BlockSpec((n, L), lambda i: (0, 0))],
        out_specs=pl.BlockSpec((1, L), lambda i: (0, 0)),
        out_shape=jax.ShapeDtypeStruct((1, L), F32),
        compiler_params=_params("arbitrary"),
    )(a)


def _scalar_call(body, name, scalar, grid, in_specs, out_spec, out_shape, args):
    return pl.pallas_call(
        body, name=name,
        grid_spec=pltpu.PrefetchScalarGridSpec(num_scalar_prefetch=1, grid=grid, in_specs=in_specs,
                                               out_specs=out_spec),
        out_shape=out_shape, compiler_params=_params("parallel"),
    )(jnp.reshape(scalar, (-1,)).astype(jnp.int32), *args)


def _pair_sum(view, got, c, name):
    nb, _, rh, cols = view.shape
    tr = _pick(rh, max(16, (1 << 20) // cols // 16 * 16), 16)
    bpr = rh // tr

    def body(s_ref, a_ref, b_ref, o_ref):
        o_ref[...] = (a_ref[...].astype(F32) + b_ref[...].astype(F32)).astype(BF16)

    spec = pl.BlockSpec((tr, cols), lambda i, s: (i, 0))
    mine = pl.BlockSpec((tr, cols), lambda i, s: ((2 * (i // bpr) + s[0]) * bpr + i % bpr, 0))
    return _scalar_call(body, name, c, (nb * bpr,), [mine, spec], spec,
                        jax.ShapeDtypeStruct((nb * rh, cols), BF16),
                        (view.reshape(nb * 2 * rh, cols), got.reshape(nb * rh, cols)))


STACKED_LAYERS = 2


def _owner_sum(pair, recv, me, c, it, name, layer=None, into=None):
    _, rh, bc = recv.shape
    tr = _pick(rh, max(16, (1 << 19) // bc // 16 * 16), 16)
    bpr = rh // tr

    def body(s_ref, a_ref, r0, r1, r2, *rest):
        rest[-1][...] = ((a_ref[...].astype(F32) + r0[...].astype(F32)) + r1[...].astype(F32)) + r2[...].astype(F32)

    if it.kind == "col":
        own = pl.BlockSpec((tr, bc), lambda i, s: (i, s[0]))
    else:
        own = pl.BlockSpec((tr, bc), lambda i, s: (s[0] * bpr + i, 0))
    slots = [pl.BlockSpec((None, tr, bc), functools.partial(lambda i, s, k: (k, i, 0), k=k)) for k in range(3)]
    in_specs, args, aliases = [own] + slots, [pair, recv, recv, recv], {}
    if layer is None:
        out_spec = pl.BlockSpec((tr, bc), lambda i, s: (s[1] * bpr + i, 0))
        out_shape = jax.ShapeDtypeStruct((2 * rh, bc), F32)
    else:
        out_spec = pl.BlockSpec((None, tr, bc), lambda i, s: (layer, s[1] * bpr + i, 0))
        out_shape = jax.ShapeDtypeStruct((STACKED_LAYERS, 2 * rh, bc), F32)
        if into is not None:
            in_specs.append(pl.BlockSpec(memory_space=pl.ANY))
            args.append(into)
            aliases = {len(args): 0}
    return pl.pallas_call(
        body, name=name,
        grid_spec=pltpu.PrefetchScalarGridSpec(num_scalar_prefetch=1, grid=(bpr,), in_specs=in_specs,
                                               out_specs=out_spec),
        out_shape=out_shape, input_output_aliases=aliases, compiler_params=_params("parallel"),
    )(jnp.stack([it.pos(me), c]).astype(jnp.int32), *args)


def _place():
    x, y, c = lax.axis_index("x"), lax.axis_index("y"), lax.axis_index("c")
    chips = [(1 - x, y), (x, 1 - y), (1 - x, 1 - y)]
    return x, y, c, chips


def _chip_index(px, py):
    return 2 * px + py


def _all_gather_small(x_shard, name):
    m_per, n = x_shard.shape

    def body(x_ref, out_ref, send_sems, recv_sems, local_sem):
        x, y, c, chips = _place()
        me, sibling = (x, y, c), (x, y, 1 - c)

        def rows(px, py, pc):
            return out_ref.at[pl.ds((4 * px + 2 * py + pc) * m_per, m_per), :]

        def copy(k, block, to, src=None):
            return pltpu.make_async_remote_copy(
                src_ref=rows(*block) if src is None else src, dst_ref=rows(*block),
                send_sem=send_sems.at[k], recv_sem=recv_sems.at[k], device_id=to, device_id_type=MESH)

        mine = pltpu.make_async_copy(x_ref, rows(*me), local_sem)
        mine.start()
        first = [copy(0, me, sibling, src=x_ref)]
        first += [copy(1 + j, me, (*chip, c), src=x_ref) for j, chip in enumerate(chips)]
        for cp in first:
            cp.start()
        passed = [copy(4 + j, (*chip, c), sibling) for j, chip in enumerate(chips)]
        for j, chip in enumerate(chips):
            copy(1 + j, (*chip, c), me).wait_recv()
            passed[j].start()
        copy(0, sibling, me).wait_recv()
        for j, chip in enumerate(chips):
            copy(4 + j, (*chip, 1 - c), me).wait_recv()
        for cp in first + passed:
            cp.wait_send()
        mine.wait()

    return pl.pallas_call(
        body, name=name,
        out_shape=jax.ShapeDtypeStruct((N_DEV * m_per, n), x_shard.dtype),
        in_specs=[pl.BlockSpec(memory_space=pltpu.VMEM)],
        out_specs=pl.BlockSpec(memory_space=pltpu.VMEM),
        scratch_shapes=[pltpu.SemaphoreType.DMA((7,)), pltpu.SemaphoreType.DMA((7,)), pltpu.SemaphoreType.DMA],
    )(x_shard)


def _gather_flat(vec, name):
    L = vec.shape[0]
    Lp = -(-L // 1024) * 1024
    g = _all_gather_small(jnp.pad(vec, (0, Lp - L)).reshape(8, Lp // 8), name)
    return g.reshape(N_DEV, Lp)[:, :L]


class _Item:
    def __init__(self, kind, rows, cols, arg, layer, swap=False):
        self.kind, self.rows, self.cols, self.arg, self.layer, self.swap = kind, rows, cols, arg, layer, swap

    def ref(self, refs):
        return refs[self.arg].at[self.layer]

    def pos(self, j):
        return 2 * (j % 2) + j // 2 if self.swap else j


def _block(ref, it, j, half):
    if it.kind == "col":
        ns = it.cols // N_CHIP
        return ref.at[pl.ds(half * (it.rows // 2), it.rows // 2), pl.ds(it.pos(j) * ns, ns)]
    rs = it.rows // N_CHIP
    return ref.at[pl.ds(j * rs + half * (rs // 2), rs // 2), :]


def _cast_place(w, layer, kind, pos, after, name):
    _, r, n = w.shape
    tr = _pick(r, max(16, (1 << 20) // n // 16 * 16), 16)
    bpr = r // tr

    def body(s_ref, w_ref, after_ref, o_ref):
        o_ref[...] = w_ref[...].astype(BF16)

    if kind == "col":
        full, out_idx = (1, r, N_CHIP * n), (lambda i, s: (0, i, s[0]))
    else:
        full, out_idx = (1, N_CHIP * r, n), (lambda i, s: (0, s[0] * bpr + i, 0))
    return pl.pallas_call(
        body, name=name,
        grid_spec=pltpu.PrefetchScalarGridSpec(
            num_scalar_prefetch=1, grid=(bpr,),
            in_specs=[pl.BlockSpec((None, tr, n), lambda i, s: (layer, i, 0)), pl.BlockSpec(memory_space=pl.ANY)],
            out_specs=pl.BlockSpec((None, tr, n), out_idx)),
        out_shape=jax.ShapeDtypeStruct(full, BF16),
        compiler_params=_params("parallel"),
    )(jnp.reshape(pos, (1,)).astype(jnp.int32), w, after)


HBM_SPEC = pl.BlockSpec(memory_space=pltpu.HBM)
SEM_SPEC = pl.BlockSpec(memory_space=pltpu.SEMAPHORE)
ANY_SPEC = pl.BlockSpec(memory_space=pl.ANY)
SPLIT_PARAMS = dict(has_side_effects=pltpu.SideEffectType.DATAFLOW_SIDE_EFFECTING)


def _in_hbm(a):
    return pltpu.with_memory_space_constraint(a, pltpu.HBM)


def _split_start(copies_of, bufs, n_sem, after, name):
    n = len(bufs)

    def body(*refs):
        ins, send, recv, token = refs[:n], refs[n + 1], refs[n + 2], refs[2 * n + 3]
        for cp in copies_of(ins, send, recv, False)[0]:
            cp.start()
        token[...] = jnp.zeros_like(token)

    outs = pl.pallas_call(
        body, name=name,
        out_shape=(pltpu.SemaphoreType.DMA(n_sem), pltpu.SemaphoreType.DMA(n_sem),
                   *[pltpu.HBM(b.shape, b.dtype) for b in bufs], jax.ShapeDtypeStruct((8, 128), F32)),
        in_specs=[HBM_SPEC] * n + [ANY_SPEC],
        out_specs=(SEM_SPEC, SEM_SPEC, *[HBM_SPEC] * n, pl.BlockSpec(memory_space=pltpu.VMEM)),
        input_output_aliases={t: 2 + t for t in range(n)},
        compiler_params=pltpu.CompilerParams(**SPLIT_PARAMS),
    )(*[_in_hbm(b) for b in bufs], after)
    return outs[0], outs[1], list(outs[2:2 + n]), outs[2 + n]


def _split_wait(copies_of, send, recv, bufs, after, name):
    n = len(bufs)
    after = list(after) if isinstance(after, (list, tuple)) else [after]

    def body(*refs):
        ins, send_ref, recv_ref = refs[:n], refs[n], refs[n + 1]
        sends, arrivals = copies_of(ins, send_ref, recv_ref, True)
        for cp in sends:
            cp.wait_send()
        for cp in arrivals:
            cp.wait_recv()

    return pl.pallas_call(
        body, name=name,
        out_shape=[pltpu.HBM(b.shape, b.dtype) for b in bufs],
        in_specs=[HBM_SPEC] * n + [SEM_SPEC, SEM_SPEC] + [ANY_SPEC] * len(after),
        out_specs=[HBM_SPEC] * n,
        input_output_aliases={t: t for t in range(n)},
        compiler_params=pltpu.CompilerParams(**SPLIT_PARAMS),
    )(*bufs, send, recv, *after)


def _gather_copies(items):
    def copies_of(refs, send, recv, with_arrivals):
        x, y, c, chips = _place()
        me = _chip_index(x, y)
        sends, arrivals = [], []
        for t, it in enumerate(items):
            for k, chip in enumerate(chips):
                for core in range(2):
                    mine = _block(it.ref(refs), it, me, c)
                    sends.append(pltpu.make_async_remote_copy(
                        src_ref=mine, dst_ref=mine, send_sem=send.at[6 * t + 2 * k + core],
                        recv_sem=recv.at[6 * t + 2 * k + c], device_id=(*chip, core), device_id_type=MESH))
                    if with_arrivals:
                        landed = _block(it.ref(refs), it, _chip_index(*chip), core)
                        arrivals.append(pltpu.make_async_remote_copy(
                            src_ref=landed, dst_ref=landed, send_sem=send.at[6 * t + 2 * k + core],
                            recv_sem=recv.at[6 * t + 2 * k + core], device_id=(*chip, core), device_id_type=MESH))
        return sends, arrivals

    return copies_of


def _owner_copies(items):
    n = len(items)

    def blk(ref, it, j):
        if it.kind == "col":
            ns = it.cols // N_CHIP
            return ref.at[:, pl.ds(it.pos(j) * ns, ns)]
        return ref.at[j]

    def copies_of(refs, send, recv, with_arrivals):
        x, y, c, chips = _place()
        sends, arrivals = [], []
        for t, it in enumerate(items):
            for k, chip in enumerate(chips):
                slot = refs[n + t].at[k]
                sends.append(pltpu.make_async_remote_copy(
                    src_ref=blk(refs[t], it, _chip_index(*chip)), dst_ref=slot, send_sem=send.at[3 * t + k],
                    recv_sem=recv.at[3 * t + k], device_id=(*chip, c), device_id_type=MESH))
                if with_arrivals:
                    arrivals.append(pltpu.make_async_remote_copy(
                        src_ref=slot, dst_ref=slot, send_sem=send.at[3 * t + k], recv_sem=recv.at[3 * t + k],
                        device_id=(*chip, c), device_id_type=MESH))
        return sends, arrivals

    return copies_of


def _owner_slot_shape(it):
    if it.kind == "col":
        return (3, it.rows // 2, it.cols // N_CHIP)
    return (3, it.rows // (2 * N_CHIP), it.cols)


def _pair_view(g, it):
    if it.kind == "col":
        return g.reshape(1, 2, it.rows // 2, it.cols)
    return g.reshape(N_CHIP, 2, it.rows // (2 * N_CHIP), it.cols)


def _pair_copies(n):
    def copies_of(refs, send, recv, with_arrivals):
        x, y, c, _ = _place()
        sends, arrivals = [], []
        for t in range(n):
            land = refs[n + t]
            sends.append(pltpu.make_async_remote_copy(
                src_ref=refs[t].at[:, pl.ds(1 - c, 1)], dst_ref=land, send_sem=send.at[t], recv_sem=recv.at[t],
                device_id=(x, y, 1 - c), device_id_type=MESH))
            if with_arrivals:
                arrivals.append(pltpu.make_async_remote_copy(
                    src_ref=land, dst_ref=land, send_sem=send.at[t], recv_sem=recv.at[t],
                    device_id=(x, y, 1 - c), device_id_type=MESH))
        return sends, arrivals

    return copies_of


def _half_copies(n):
    def half(ref, which):
        r2 = ref.shape[-2] // 2
        rows = pl.ds(which * r2, r2)
        return ref.at[rows, :] if len(ref.shape) == 2 else ref.at[:, rows, :]

    def copies_of(refs, send, recv, with_arrivals):
        x, y, c, _ = _place()
        sends, arrivals = [], []
        for t in range(n):
            mine = half(refs[t], c)
            sends.append(pltpu.make_async_remote_copy(
                src_ref=mine, dst_ref=mine, send_sem=send.at[t], recv_sem=recv.at[t],
                device_id=(x, y, 1 - c), device_id_type=MESH))
            if with_arrivals:
                theirs = half(refs[t], 1 - c)
                arrivals.append(pltpu.make_async_remote_copy(
                    src_ref=theirs, dst_ref=theirs, send_sem=send.at[t], recv_sem=recv.at[t],
                    device_id=(x, y, 1 - c), device_id_type=MESH))
        return sends, arrivals

    return copies_of


class _Reduction:
    pass


def _pair_start(grads, items, after, tag, names, layer=None):
    n = len(items)
    views = [_pair_view(g, it) for g, it in zip(grads, items)]
    lands = [lax.empty((v.shape[0], 1) + v.shape[2:], v.dtype) for v in views]
    r = _Reduction()
    r.items, r.tag, r.names, r.layer = items, tag, names, layer
    r.send, r.recv, r.bufs, r.token = _split_start(_pair_copies(n), views + lands, (n,), after, f"rs_pair_start_{tag}")
    return r


def _owner_start(r, after):
    x, y, c, _ = _place()
    n = len(r.items)
    bufs = _split_wait(_pair_copies(n), r.send, r.recv, r.bufs, after, f"rs_pair_wait_{r.tag}")
    pairs = [_pair_sum(bufs[t], bufs[n + t], c, f"rs_pair_sum_{r.tag}_{t}") for t in range(n)]
    shaped = [p if it.kind == "col" else p.reshape(N_CHIP, p.shape[0] // N_CHIP, p.shape[1])
              for p, it in zip(pairs, r.items)]
    lands = [lax.empty(_owner_slot_shape(it), BF16) for it in r.items]
    r.send, r.recv, r.bufs, r.token = _split_start(
        _owner_copies(r.items), shaped + lands, (3 * n,), r.token, f"rs_owner_start_{r.tag}")
    return r


def _reduce_finish(groups, after):
    x, y, c, _ = _place()
    me = _chip_index(x, y)
    halves = {}
    behind = [after]
    for r in groups:
        n = len(r.items)
        bufs = _split_wait(_owner_copies(r.items), r.send, r.recv, r.bufs, behind, f"rs_owner_wait_{r.tag}")
        for t, (it, nm) in enumerate(zip(r.items, r.names)):
            pair = bufs[t].reshape(-1, bufs[t].shape[-1])
            halves[nm] = _owner_sum(pair, bufs[n + t], me, c, it, f"rs_owner_sum_{r.tag}_{t}",
                                    layer=r.layer, into=halves.get(nm))
        behind = [after] + [halves[nm] for nm in r.names]
    n = len(halves)
    return list(halves), _split_start(_half_copies(n), list(halves.values()), (n,), after, "rs_half_start")


def _silu(v):
    return v * jax.nn.sigmoid(v)


def _sum8(p):
    return jnp.sum(p, axis=-2)


def kernel(x, c, mod_w, mod_b, norm_g, ffn_w_in, ffn_w_out, conv_w_in, conv_k, conv_w_out, kv_mod_w, kv_mod_b, kv_norm_g, w_kv, attn_w_q, attn_w_o, rel_bias, loss_target, m_mod_w, m_mod_b, m_norm_g, m_ffn_w_in, m_ffn_w_out, m_conv_w_in, m_conv_k, m_conv_w_out, m_kv_mod_w, m_kv_mod_b, m_kv_norm_g, m_w_kv, m_attn_w_q, m_attn_w_o, m_rel_bias, v_mod_w, v_mod_b, v_norm_g, v_ffn_w_in, v_ffn_w_out, v_conv_w_in, v_conv_k, v_conv_w_out, v_kv_mod_w, v_kv_mod_b, v_kv_norm_g, v_w_kv, v_attn_w_q, v_attn_w_o, v_rel_bias):
    xi, yi, ci = lax.axis_index("x"), lax.axis_index("y"), lax.axis_index("c")
    chip = 2 * xi + yi
    dev = 2 * chip + ci
    _, S, D = x.shape
    F = ffn_w_out.shape[1] * N_CHIP
    x0 = x.reshape(S, D)
    target = loss_target.reshape(S, D)
    n_mod = mod_w.shape[2]
    n_kvm = kv_mod_w.shape[1]
    dsh = D // N_CHIP
    TF = F // 2

    c_all = _all_gather_small(c.reshape(8, D // 8), "ag_c").reshape(N_DEV, D)
    sc16 = jnp.pad(_silu(c_all), ((0, 8), (0, 0)))
    part = [_mm(sc16, mod_w, "nn", F32, f"mod_fwd_{l}", b_layer=l)[:8] for l in range(2)]
    part.append(_mm(sc16, kv_mod_w, "nn", F32, "mod_fwd_kv")[:8])
    fwd_vec = jnp.concatenate([p.reshape(-1) for p in part] + [norm_g.reshape(-1), conv_k.reshape(-1)])
    fwd_all = _gather_flat(fwd_vec, "ag_fwd_small")[0::2]
    o = 0
    mods = []
    for n in (n_mod, n_mod, n_kvm):
        blk = fwd_all[:, o:o + 8 * n].reshape(N_CHIP, 8, n)
        mods.append(lax.dynamic_index_in_dim(blk, dev, axis=1, keepdims=False).reshape(N_CHIP * n))
        o += 8 * n
    ng = fwd_all[:, o:o + 8 * dsh].reshape(N_CHIP, 2, 4, dsh).transpose(1, 2, 0, 3).reshape(2, 4, D)
    o += 8 * dsh
    ck = fwd_all[:, o:o + 3 * dsh].reshape(N_CHIP, 3, dsh).transpose(1, 0, 2).reshape(3, D)
    ck8 = jnp.pad(ck, ((0, 5), (0, 0)))
    mod = [mods[l] + mod_b[l] for l in range(2)]
    sh1, sc1, g1, sh2, sc2, g2 = zip(*[jnp.split(m, 6) for m in mod])
    kv_sh, kv_sc = jnp.split(mods[2] + kv_mod_b, 2)
    row = lambda v: v.reshape(1, D)

    it_conv = [_Item("col", D, 3 * D, 0, 0), _Item("row", D, D, 1, 0)]
    it_ffn = [_Item("col", D, 2 * F, 0, 0, swap=True), _Item("row", F, D, 1, 0)]
    it_attn = [_Item("col", D, 2 * D, 0, 0, swap=True), _Item("row", D, D, 1, 0), _Item("row", D, D, 2, 0)]

    def placed(w, layer, it, nm, after=fwd_all):
        return _cast_place(w, layer, it.kind, it.pos(chip), after, f"place_{nm}")

    flying = {}

    def start(tag, its, bufs, after):
        send, recv, bufs, tok = _split_start(_gather_copies(its), bufs, (6 * len(its),), after, f"ag_start_{tag}")
        flying[tag] = (its, send, recv, bufs)
        return tok

    def arrived(tag, after):
        its, send, recv, bufs = flying[tag]
        return _split_wait(_gather_copies(its), send, recv, bufs, after, f"ag_wait_{tag}")

    one = lambda it: [_Item(it.kind, it.rows, it.cols, 0, 0, it.swap)]
    tok = start("conv_in", one(it_conv[0]), [placed(conv_w_in, 0, it_conv[0], "conv_w_in")], fwd_all)
    tok = start("conv_out", one(it_conv[1]), [placed(conv_w_out, 0, it_conv[1], "conv_w_out", tok)], tok)
    tok = start("ffn0_in", one(it_ffn[0]), [placed(ffn_w_in, 0, it_ffn[0], "ffn_w_in0", tok)], tok)
    tok = start("ffn0_out", one(it_ffn[1]), [placed(ffn_w_out, 0, it_ffn[1], "ffn_w_out0", tok)], tok)
    tok = start("attn", it_attn, [placed(w_kv[None], 0, it_attn[0], "w_kv", tok),
                                  placed(attn_w_q, 0, it_attn[1], "attn_w_q", tok),
                                  placed(attn_w_o, 0, it_attn[2], "attn_w_o", tok)], tok)
    token = start("ffn1", it_ffn, [placed(ffn_w_in, 1, it_ffn[0], "ffn_w_in1", tok),
                                   placed(ffn_w_out, 1, it_ffn[1], "ffn_w_out1", tok)], tok)

    a1 = row(ng[0, 0] * (1.0 + sc1[0])) + token[0, 0]
    (h1,) = _norm_mod(x0, a1, row(sh1[0]), "l0_norm1")
    tab = _bias_table(rel_bias[0], "l1_bias_table")
    h1, tab = lax.optimization_barrier((h1, tab))
    (W_cin,) = arrived("conv_in", h1)
    bcx = _mm(h1, W_cin, "nn", BF16, "l0_conv_in", b_layer=0, tm=512, tn=3 * D)
    ug = _conv_gate(bcx, ck8, "l0_conv_gate")
    gt1 = row(g1[0] * ng[0, 1])
    a2 = row(ng[0, 2] * (1.0 + sc2[0]))
    (W_cout,) = arrived("conv_out", ug)
    y1, x1, h2 = _mm_post(ug, W_cout, x0, gt1, "l0_conv_out", scales=a2, shifts=row(sh2[0]))
    (W_fin0,) = arrived("ffn0_in", h2)
    gu0, act0 = _ffn_in_act(h2, W_fin0, 0, "l0_ffn_in")
    (W_fout0,) = arrived("ffn0_out", act0)
    gt2 = row(g2[0] * ng[0, 3])
    a3 = ng[1, 0] * (1.0 + sc1[1])
    akv = kv_norm_g * (1.0 + kv_sc)
    y2, x2, h3, hkv = _mm_post(act0, W_fout0, x1, gt2, "l0_ffn_out",
                               scales=jnp.stack([a3, akv]), shifts=jnp.stack([sh1[1], kv_sh]))
    W_kv, W_q, W_o = arrived("attn", hkv)
    kvp = _mm(hkv, W_kv, "nn", BF16, "l1_kv", b_layer=0, tm=512, tn=2 * D)
    att_scale = (D // N_HEADS) ** -0.5
    assert math.log2(att_scale) % 1 == 0, "scaling q before its bf16 cast is exact only for a power of two"
    qp = _mm(h3, W_q, "nn", BF16, "l1_q", b_layer=0, scale=att_scale)
    oh = _attn_fwd(qp, kvp, tab, "l1_attn")
    gt3 = row(g1[1] * ng[1, 1])
    a4 = row(ng[1, 2] * (1.0 + sc2[1]))
    y3, x3, h4 = _mm_post(oh, W_o, x2, gt3, "l1_attn_out", scales=a4, shifts=row(sh2[1]))
    W_fin1, W_fout1 = arrived("ffn1", h4)
    gu1, act1 = _ffn_in_act(h4, W_fin1, 0, "l1_ffn_in")
    gt4 = row(g2[1] * ng[1, 3])
    dx4, sq, dy4, dgt4 = _mm_post(act1, W_fout1, x3, gt4, "l1_ffn_out", target=target)
    loss_part = 0.5 * jnp.sum(sq) / D

    def ffn_bwd(dy, dxn, xin_, h, gu, act, a, w_in, w_out, post, tag):
        dgu, dx, ds, db, dyn, dgt = _ffn_bwd(dy, w_out, gu, w_in, xin_, dxn, a, post, f"{tag}_ffn_bwd")
        g_fout = _mm(act, dy, "tn", BF16, f"{tag}_ffn_out_dw", tm=TF)
        g_fin = _mm(h, dgu, "tn", BF16, f"{tag}_ffn_in_dw", tn=TF)
        return dx, ds, db, dyn, dgt, g_fin, g_fout

    dx3, ds4, db4, dy3, dgt3, G_fin1, G_fout1 = ffn_bwd(dy4, dx4, x3, h4, gu1, act1, a4, W_fin1, W_fout1,
                                                        (y3, gt3), "l1")
    red = [_pair_start([G_fin1, G_fout1], it_ffn, token, "ffn1", ["ffn_w_in", "ffn_w_out"], layer=1)]
    doh = _mm(dy3, W_o, "nt", BF16, "l1_attn_out_dx", b_layer=0, after=red[0].token)
    G_o = _mm(oh, dy3, "tn", BF16, "l1_attn_out_dw")
    _owner_start(red[0], G_o)
    dq, dkv, dtab = _attn_bwd(qp, kvp, tab, doh, "l1_attn_bwd")
    d_rel = _bias_table_grad(dtab)
    G_q = _mm(h3, dq, "tn", BF16, "l1_q_dw")
    G_kv = _mm(hkv, dkv, "tn", BF16, "l1_kv_dw")
    red.append(_pair_start([G_kv, G_q, G_o], it_attn, red[-1].token, "attn", ["w_kv", "attn_w_q", "attn_w_o"]))
    dx2, ds3, db3, dy2, dgt2 = _mm_pre_bwd([(dq, W_q), (dkv, W_kv)], x2, dx3,
                                           jnp.stack([a3, akv]) + red[1].token[0, 0], "l1_qkv_dx", post=(y2, gt2))
    _owner_start(red[1], dx2)

    dx1, ds2, db2, dy1, dgt1, G_fin0, G_fout0 = ffn_bwd(dy2, dx2, x1, h2, gu0, act0, a2, W_fin0, W_fout0,
                                                        (y1, gt1), "l0")
    red.append(_pair_start([G_fin0, G_fout0], it_ffn, red[-1].token, "ffn0", ["ffn_w_in", "ffn_w_out"], layer=0))
    dug = _mm(dy1, W_cout, "nt", BF16, "l0_conv_out_dx", b_layer=0, after=red[2].token)
    G_cout = _mm(ug, dy1, "tn", BF16, "l0_conv_out_dw")
    dbcx, dck = _conv_gate_bwd(dug, bcx, ck8, "l0_conv_gate_bwd")
    _owner_start(red[2], dbcx)
    G_cin = _mm(h1, dbcx, "tn", BF16, "l0_conv_in_dw")
    red.append(_pair_start([G_cin, G_cout], it_conv, red[-1].token, "conv", ["conv_w_in", "conv_w_out"]))
    dx0, ds1, db1 = _mm_pre_bwd([(dbcx, W_cin)], x0, dx1, a1 + red[3].token[0, 0], "l0_conv_in_dx")
    ds1, db1 = _sum8(ds1)[0], _sum8(db1)[0]
    da2, db2 = _sum8(ds2)[0], _sum8(db2)[0]
    ds3, db3 = _sum8(ds3), _sum8(db3)
    da4, db4 = _sum8(ds4)[0], _sum8(db4)[0]
    dgt1, dgt2, dgt3, dgt4 = _sum8(dgt1), _sum8(dgt2), _sum8(dgt3), _sum8(dgt4)

    def dmod_of(l, ds_a, db_a, dgt_a, ds_b, db_b, dgt_b):
        return jnp.concatenate([db_a, ds_a * ng[l, 0], dgt_a * ng[l, 1], db_b, ds_b * ng[l, 2], dgt_b * ng[l, 3]])

    dmod0 = dmod_of(0, ds1, db1, dgt1, da2, db2, dgt2)
    dmod1 = dmod_of(1, ds3[0], db3[0], dgt3, da4, db4, dgt4)
    dkvmod = jnp.concatenate([db3[1], ds3[1] * kv_norm_g])
    dng = jnp.stack([
        jnp.stack([ds1 * (1.0 + sc1[0]), dgt1 * g1[0], da2 * (1.0 + sc2[0]), dgt2 * g2[0]]),
        jnp.stack([ds3[0] * (1.0 + sc1[1]), dgt3 * g1[1], da4 * (1.0 + sc2[1]), dgt4 * g2[1]])])
    dkvng = ds3[1] * (1.0 + kv_sc)
    small = [dmod0, dmod1, dkvmod, dng.reshape(-1), dkvng, _sum8(dck).reshape(-1), d_rel.reshape(-1),
             loss_part.reshape(1)]
    sizes = [int(s.shape[0]) for s in small]
    offs = np.concatenate([[0], np.cumsum(sizes)])
    bwd_all = _gather_flat(jnp.concatenate(small), "ag_bwd_small")
    _owner_start(red[3], bwd_all)
    Lb = bwd_all.shape[1]
    Lp = -(-Lb // 128) * 128
    tot = _sum_rows(jnp.pad(bwd_all, ((0, 0), (0, Lp - Lb))), "sum_small")[0]
    seg = lambda i: tot[offs[i]:offs[i + 1]]
    g_mod_b = jnp.stack([seg(0), seg(1)])
    g_kv_mod_b = seg(2)
    g_norm_g = lax.dynamic_slice_in_dim(seg(3).reshape(2, 4, D), chip * dsh, dsh, axis=2)
    g_kv_norm_g = seg(4)
    g_conv_k = lax.dynamic_slice_in_dim(seg(5).reshape(1, 3, D), chip * dsh, dsh, axis=2)
    g_rel_bias = seg(6).reshape(rel_bias.shape)
    loss = seg(7)[0]

    def dmod_rows(i, n):
        rows_ = lax.dynamic_slice_in_dim(bwd_all[:, offs[i]:offs[i + 1]], chip * n, n, axis=1)
        return jnp.pad(rows_, ((0, 8), (0, 0)))

    g_mod_w = _mm(sc16, jnp.concatenate([dmod_rows(0, n_mod), dmod_rows(1, n_mod)], axis=1), "tn", F32,
                  "mod_bwd", out_layers=STACKED_LAYERS)
    g_kv_mod_w = _mm(sc16, dmod_rows(2, n_kvm), "tn", F32, "mod_bwd_kv")

    grads = {
        "mod_w": g_mod_w, "mod_b": g_mod_b, "norm_g": g_norm_g, "conv_k": g_conv_k,
        "kv_mod_w": g_kv_mod_w, "kv_mod_b": g_kv_mod_b, "kv_norm_g": g_kv_norm_g, "rel_bias": g_rel_bias,
    }
    weights = dict(mod_w=mod_w, mod_b=mod_b, norm_g=norm_g, ffn_w_in=ffn_w_in, ffn_w_out=ffn_w_out,
                   conv_w_in=conv_w_in, conv_k=conv_k, conv_w_out=conv_w_out, kv_mod_w=kv_mod_w,
                   kv_mod_b=kv_mod_b, kv_norm_g=kv_norm_g, w_kv=w_kv, attn_w_q=attn_w_q, attn_w_o=attn_w_o,
                   rel_bias=rel_bias)
    m_in = dict(mod_w=m_mod_w, mod_b=m_mod_b, norm_g=m_norm_g, ffn_w_in=m_ffn_w_in, ffn_w_out=m_ffn_w_out,
                conv_w_in=m_conv_w_in, conv_k=m_conv_k, conv_w_out=m_conv_w_out, kv_mod_w=m_kv_mod_w,
                kv_mod_b=m_kv_mod_b, kv_norm_g=m_kv_norm_g, w_kv=m_w_kv, attn_w_q=m_attn_w_q,
                attn_w_o=m_attn_w_o, rel_bias=m_rel_bias)
    v_in = dict(mod_w=v_mod_w, mod_b=v_mod_b, norm_g=v_norm_g, ffn_w_in=v_ffn_w_in, ffn_w_out=v_ffn_w_out,
                conv_w_in=v_conv_w_in, conv_k=v_conv_k, conv_w_out=v_conv_w_out, kv_mod_w=v_kv_mod_w,
                kv_mod_b=v_kv_mod_b, kv_norm_g=v_kv_norm_g, w_kv=v_w_kv, attn_w_q=v_attn_w_q,
                attn_w_o=v_attn_w_o, rel_bias=v_rel_bias)
    names = list(weights)
    step = {}

    def update(n):
        g = grads[n].reshape(weights[n].shape)
        step[n] = (g, *_adamw(weights[n], g, m_in[n], v_in[n], f"adamw_{n}"))

    update("mod_w")
    reduced, (half_send, half_recv, half_bufs, _) = _reduce_finish(red, step["mod_w"][1])
    local = [n for n in grads if n != "mod_w"]
    for n in local:
        update(n)
    grads.update(zip(reduced, _split_wait(
        _half_copies(len(half_bufs)), half_send, half_recv, half_bufs, [step[n][1] for n in local], "rs_half_wait")))
    for n in names:
        if n not in step:
            update(n)
    return (loss, dx0.reshape(x.shape), *[step[n][k] for k in range(4) for n in names])
```

```python
import functools
import math

import numpy as np
import jax
import jax.numpy as jnp
from jax import lax
from jax.experimental import pallas as pl
from jax.experimental.pallas import tpu as pltpu

CHUNK = 64
N_LEFT_CHUNKS = 8
N_HEADS = 16
MAX_REL = 2 * CHUNK
N_REL = 2 * MAX_REL + 1
EPS = 1e-6
ADAM_LR = 0.001
ADAM_B1 = 0.9
ADAM_B2 = 0.999
ADAM_EPS = 1e-08
ADAM_WD = 0.01
ADAM_STEP = 10

Q_CHUNKS = 4
BQ = Q_CHUNKS * CHUNK
N_WIN = 1 + N_LEFT_CHUNKS // Q_CHUNKS
HEADS_PER_STEP = 8
NEG = -1e30
N_DEV = 8
N_CHIP = 4
SMALL_TENSOR_ELEMS = 1 << 16
PIECE_ROWS = 256

BF16 = jnp.bfloat16
F32 = jnp.float32
V7X_VMEM_LIMIT_BYTES = 56 * 1024 * 1024
MESH = pl.DeviceIdType.MESH


def _pick(n, pref, align):
    t = min(pref, n)
    t -= t % align
    while t >= align:
        if n % t == 0:
            return t
        t -= align
    return n


def _params(*sem):
    return pltpu.CompilerParams(dimension_semantics=sem, vmem_limit_bytes=V7X_VMEM_LIMIT_BYTES)


def _colsum8(v):
    r, d = v.shape
    return v.reshape(r // 8, 8, d).sum(axis=0)


_DIMS = {"nn": (((1,), (0,)), ((), ())), "nt": (((1,), (1,)), ((), ())), "tn": (((0,), (0,)), ((), ()))}


def _mm(a, b, mode, out_dtype, name, *, b_layer=None, tm=1024, tn=1024, tk=None, scale=None, after=None,
        out_layers=1):
    if tk is None:
        tk = 2048 if mode == "tn" else 3072
    bs = b.shape[1:] if b_layer is not None else b.shape
    if mode == "nn":
        (M, K), (K2, N) = a.shape, bs
    elif mode == "nt":
        (M, K), (N, K2) = a.shape, bs
    else:
        (K, M), (K2, N) = a.shape, bs
    assert K == K2, (name, a.shape, b.shape)
    tm = _pick(M, tm, 128 if mode == "tn" else 16)
    tn = _pick(N // out_layers, tn, 128)
    tk = _pick(K, tk, 128 if mode != "tn" else 16)
    nk = K // tk
    assert scale is None or nk == 1, name
    dims = _DIMS[mode]
    extra = [] if after is None else [after]

    def body(a_ref, b_ref, *rest):
        o_ref, acc = rest[len(extra)], rest[len(extra) + 1:]
        p = lax.dot_general(a_ref[...].astype(BF16), b_ref[...].astype(BF16), dims,
                            preferred_element_type=F32)
        if nk == 1:
            o_ref[...] = (p if scale is None else p * scale).astype(o_ref.dtype)
        else:
            k = pl.program_id(2)

            @pl.when(k == 0)
            def _():
                acc[0][...] = p

            @pl.when(k > 0)
            def _():
                acc[0][...] += p

            @pl.when(k == nk - 1)
            def _():
                o_ref[...] = acc[0][...].astype(o_ref.dtype)

    a_spec = (pl.BlockSpec((tk, tm), lambda i, j, k: (k, i)) if mode == "tn"
              else pl.BlockSpec((tm, tk), lambda i, j, k: (i, k)))
    if mode == "nt":
        b_blk, b_idx = (tn, tk), (lambda i, j, k: (j, k))
    else:
        b_blk, b_idx = (tk, tn), (lambda i, j, k: (k, j))
    if b_layer is not None:
        b_spec = pl.BlockSpec((None,) + b_blk, lambda i, j, k: (b_layer,) + b_idx(i, j, k))
    else:
        b_spec = pl.BlockSpec(b_blk, b_idx)
    if out_layers > 1:
        per_layer = N // out_layers // tn
        o_spec = pl.BlockSpec((None, tm, tn), lambda i, j, k: (j // per_layer, i, j % per_layer))
        o_shape = (out_layers, M, N // out_layers)
    else:
        o_spec, o_shape = pl.BlockSpec((tm, tn), lambda i, j, k: (i, j)), (M, N)
    return pl.pallas_call(
        body, name=name,
        grid=(M // tm, N // tn, nk),
        in_specs=[a_spec, b_spec] + [pl.BlockSpec(memory_space=pl.ANY)] * len(extra),
        out_specs=o_spec,
        out_shape=jax.ShapeDtypeStruct(o_shape, out_dtype),
        scratch_shapes=[pltpu.VMEM((tm, tn), F32)] if nk > 1 else [],
        compiler_params=_params("parallel", "parallel", "arbitrary"),
    )(a, b, *extra)


def _row_spec(tm, d):
    return pl.BlockSpec((tm, d), lambda i: (i, 0))


def _vec_spec(r, d):
    return pl.BlockSpec((r, d), lambda i: (0, 0))


def _norm_mod(x, scales, shifts, name):
    S, D = x.shape
    nb = scales.shape[0]
    tm = _pick(S, 1024, 16)

    def body(x_ref, a_ref, b_ref, *o_refs):
        xv = x_ref[...]
        xh = xv * lax.rsqrt(jnp.mean(xv * xv, axis=-1, keepdims=True) + EPS)
        for n in range(nb):
            o_refs[n][...] = (xh * a_ref[n:n + 1, :] + b_ref[n:n + 1, :]).astype(BF16)

    return pl.pallas_call(
        body, name=name, grid=(S // tm,),
        in_specs=[_row_spec(tm, D), _vec_spec(nb, D), _vec_spec(nb, D)],
        out_specs=[_row_spec(tm, D)] * nb,
        out_shape=[jax.ShapeDtypeStruct((S, D), BF16)] * nb,
        compiler_params=_params("parallel"),
    )(x, scales, shifts)


def _mm_post(a, w, x, gate, name, *, scales=None, shifts=None, target=None):
    M, K = a.shape
    D = w.shape[2]
    tm = _pick(M, 1024 if K <= D else 512, 16)
    sub = _pick(tm, PIECE_ROWS, 16)
    nb = 0 if scales is None else scales.shape[0]

    def body(a_ref, w_ref, x_ref, g_ref, *rest):
        if target is None:
            sc_ref, sh_ref, y_ref, xn_ref = rest[:4]
            h_refs = rest[4:]
        else:
            t_ref, dx_ref, sq_ref, dy_ref, dg_ref = rest

            @pl.when(pl.program_id(0) == 0)
            def _():
                sq_ref[...] = jnp.zeros_like(sq_ref)
                dg_ref[...] = jnp.zeros_like(dg_ref)

        def product(r):
            return jnp.dot(a_ref[pl.ds(r * sub, sub), :], w_ref[...], preferred_element_type=F32)

        y = product(0)
        for r in range(tm // sub):
            rows = pl.ds(r * sub, sub)
            yb = y.astype(BF16)
            if r + 1 < tm // sub:
                y = product(r + 1)
            yv = yb.astype(F32)
            yh = yv * lax.rsqrt(jnp.mean(yv * yv, axis=-1, keepdims=True) + EPS)
            xn = x_ref[rows, :] + yh * g_ref[...]
            if target is None:
                y_ref[rows, :] = yb
                xn_ref[rows, :] = xn
                xh = xn * lax.rsqrt(jnp.mean(xn * xn, axis=-1, keepdims=True) + EPS)
                for n in range(nb):
                    h_refs[n][rows, :] = (xh * sc_ref[n:n + 1, :] + sh_ref[n:n + 1, :]).astype(BF16)
            else:
                e = xn - t_ref[rows, :]
                dx = e / D
                dx_ref[rows, :] = dx
                sq_ref[...] += _colsum8(e * e)
                dy, dxy = _post_norm_grad(dx, yb, g_ref[...])
                dy_ref[rows, :] = dy.astype(BF16)
                dg_ref[...] += _colsum8(dxy)

    ins = [a, w, x, gate]
    in_specs = [_row_spec(tm, K), pl.BlockSpec((None, K, D), lambda i: (0, 0, 0)), _row_spec(tm, D), _vec_spec(1, D)]
    if target is None:
        ins += [scales, shifts]
        in_specs += [_vec_spec(nb, D), _vec_spec(nb, D)]
        out_specs = [_row_spec(tm, D)] * (2 + nb)
        out_shape = [jax.ShapeDtypeStruct((M, D), BF16), jax.ShapeDtypeStruct((M, D), F32)] \
            + [jax.ShapeDtypeStruct((M, D), BF16)] * nb
    else:
        ins += [target]
        in_specs += [_row_spec(tm, D)]
        out_specs = [_row_spec(tm, D), _vec_spec(8, D), _row_spec(tm, D), _vec_spec(8, D)]
        out_shape = [jax.ShapeDtypeStruct((M, D), F32), jax.ShapeDtypeStruct((8, D), F32),
                     jax.ShapeDtypeStruct((M, D), BF16), jax.ShapeDtypeStruct((8, D), F32)]
    return pl.pallas_call(
        body, name=name, grid=(M // tm,), in_specs=in_specs, out_specs=out_specs, out_shape=out_shape,
        compiler_params=_params("arbitrary" if target is not None else "parallel"),
    )(*ins)


def _post_norm_grad(dxn, yb, gate):
    yv = yb.astype(F32)
    r = lax.rsqrt(jnp.mean(yv * yv, axis=-1, keepdims=True) + EPS)
    yh = yv * r
    dyh = dxn * gate
    return r * (dyh - yh * jnp.mean(dyh * yh, axis=-1, keepdims=True)), dxn * yh


def _mm_pre_bwd(pairs, x, dxn, scales, name, post=None):
    S, D = x.shape
    nb = len(pairs)
    tm = _pick(S, 512, 16)
    sub = _pick(tm, PIECE_ROWS, 16)

    def body(*refs):
        a_refs, w_refs = refs[0:2 * nb:2], refs[1:2 * nb:2]
        x_ref, d_ref, sc_ref = refs[2 * nb:2 * nb + 3]
        rest = refs[2 * nb + 3:]
        if post is not None:
            y_ref, g_ref, dx_ref, ds_ref, db_ref, dy_ref, dg_ref = rest
        else:
            dx_ref, ds_ref, db_ref = rest

        @pl.when(pl.program_id(0) == 0)
        def _():
            ds_ref[...] = jnp.zeros_like(ds_ref)
            db_ref[...] = jnp.zeros_like(db_ref)
            if post is not None:
                dg_ref[...] = jnp.zeros_like(dg_ref)

        def products(r):
            return [lax.dot_general(a_refs[n][pl.ds(r * sub, sub), :], w_refs[n][...], _DIMS["nt"],
                                    preferred_element_type=F32) for n in range(nb)]

        nxt = products(0)
        for r in range(tm // sub):
            rows = pl.ds(r * sub, sub)
            dhs = nxt
            if r + 1 < tm // sub:
                nxt = products(r + 1)
            xv = x_ref[rows, :]
            rr = lax.rsqrt(jnp.mean(xv * xv, axis=-1, keepdims=True) + EPS)
            xh = xv * rr
            dxh = jnp.zeros_like(xv)
            for n in range(nb):
                dh = dhs[n]
                dxh = dxh + dh * sc_ref[n:n + 1, :]
                ds_ref[n] += _colsum8(dh * xh)
                db_ref[n] += _colsum8(dh)
            dx = d_ref[rows, :] + rr * (dxh - xh * jnp.mean(dxh * xh, axis=-1, keepdims=True))
            dx_ref[rows, :] = dx
            if post is not None:
                dy, dxy = _post_norm_grad(dx, y_ref[rows, :], g_ref[...])
                dy_ref[rows, :] = dy.astype(BF16)
                dg_ref[...] += _colsum8(dxy)

    ins, in_specs = [], []
    for a, w in pairs:
        ins += [a, w]
        in_specs += [_row_spec(tm, a.shape[1]),
                     pl.BlockSpec((None, D, a.shape[1]), lambda i: (0, 0, 0), pipeline_mode=pl.Buffered(1))]
    ins += [x, dxn, scales]
    in_specs += [_row_spec(tm, D), _row_spec(tm, D), _vec_spec(nb, D)]
    acc_spec = pl.BlockSpec((nb, 8, D), lambda i: (0, 0, 0))
    out_specs = [_row_spec(tm, D), acc_spec, acc_spec]
    out_shape = [jax.ShapeDtypeStruct((S, D), F32), jax.ShapeDtypeStruct((nb, 8, D), F32),
                 jax.ShapeDtypeStruct((nb, 8, D), F32)]
    if post is not None:
        ins += list(post)
        in_specs += [_row_spec(tm, D), _vec_spec(1, D)]
        out_specs += [_row_spec(tm, D), _vec_spec(8, D)]
        out_shape += [jax.ShapeDtypeStruct((S, D), BF16), jax.ShapeDtypeStruct((8, D), F32)]
    return pl.pallas_call(
        body, name=name, grid=(S // tm,), in_specs=in_specs, out_specs=out_specs, out_shape=out_shape,
        compiler_params=_params("arbitrary"),
    )(*ins)


FFN_PAIRS = 2


def _ffn_in_act(h, w, layer, name):
    S, D = h.shape
    F2 = w.shape[2]
    PW = F2 // (2 * FFN_PAIRS)
    tm = _pick(S, 1024, 16)
    sub = _pick(tm, PIECE_ROWS, 16)

    def body(h_ref, w_ref, gu_ref, a_ref):
        def product(r):
            return jnp.dot(h_ref[pl.ds(r * sub, sub), :], w_ref[...], preferred_element_type=F32)

        nxt = product(0)
        for r in range(tm // sub):
            rows = pl.ds(r * sub, sub)
            acc = nxt
            if r + 1 < tm // sub:
                nxt = product(r + 1)
            gu_ref[rows, :] = acc.astype(BF16)
            g = acc[:, :PW]
            a_ref[rows, :] = (g * jax.nn.sigmoid(g) * acc[:, PW:]).astype(BF16)

    return pl.pallas_call(
        body, name=name, grid=(FFN_PAIRS, S // tm),
        in_specs=[pl.BlockSpec((tm, D), lambda p, i: (i, 0)),
                  pl.BlockSpec((None, D, 2 * PW), lambda p, i: (layer, 0, p))],
        out_specs=[pl.BlockSpec((tm, 2 * PW), lambda p, i: (i, p)), pl.BlockSpec((tm, PW), lambda p, i: (i, p))],
        out_shape=[jax.ShapeDtypeStruct((S, F2), BF16), jax.ShapeDtypeStruct((S, F2 // 2), BF16)],
        compiler_params=_params("parallel", "parallel"),
    )(h, w)


def _ffn_bwd(dy, w_out, gu, w_in, x, dxn, scale, post, name):
    S, D = dy.shape
    F2 = gu.shape[1]
    PW = F2 // (2 * FFN_PAIRS)
    tm = _pick(S, 256, 16)

    def body(dy_ref, wo_ref, gu_ref, wi_ref, x_ref, d_ref, sc_ref, y_ref, g_ref,
             dgu_ref, dx_ref, ds_ref, db_ref, dyn_ref, dg_ref):
        @pl.when(pl.program_id(0) == 0)
        def _():
            ds_ref[...] = jnp.zeros_like(ds_ref)
            db_ref[...] = jnp.zeros_like(db_ref)
            dg_ref[...] = jnp.zeros_like(dg_ref)

        def first_product(p):
            return lax.dot_general(dy_ref[...], wo_ref[p * PW:(p + 1) * PW, :], _DIMS["nt"],
                                   preferred_element_type=F32)

        dh = jnp.zeros((tm, D), F32)
        nxt = first_product(0)
        for p in range(FFN_PAIRS):
            cols = slice(2 * p * PW, 2 * (p + 1) * PW)
            da = nxt
            if p + 1 < FFN_PAIRS:
                nxt = first_product(p + 1)
            g = gu_ref[:, 2 * p * PW:(2 * p + 1) * PW].astype(F32)
            u = gu_ref[:, (2 * p + 1) * PW:2 * (p + 1) * PW].astype(F32)
            sg = jax.nn.sigmoid(g)
            dgu_ref[:, 2 * p * PW:(2 * p + 1) * PW] = (da * u * (sg * (1.0 + g * (1.0 - sg)))).astype(BF16)
            dgu_ref[:, (2 * p + 1) * PW:2 * (p + 1) * PW] = (da * (g * sg)).astype(BF16)
            dh = dh + lax.dot_general(dgu_ref[:, cols], wi_ref[:, cols], _DIMS["nt"], preferred_element_type=F32)
        xv = x_ref[...]
        rr = lax.rsqrt(jnp.mean(xv * xv, axis=-1, keepdims=True) + EPS)
        xh = xv * rr
        dxh = dh * sc_ref[...]
        ds_ref[0] += _colsum8(dh * xh)
        db_ref[0] += _colsum8(dh)
        dx = d_ref[...] + rr * (dxh - xh * jnp.mean(dxh * xh, axis=-1, keepdims=True))
        dx_ref[...] = dx
        dyn, dxy = _post_norm_grad(dx, y_ref[...], g_ref[...])
        dyn_ref[...] = dyn.astype(BF16)
        dg_ref[...] += _colsum8(dxy)

    resident = dict(pipeline_mode=pl.Buffered(1))
    acc_spec = pl.BlockSpec((1, 8, D), lambda i: (0, 0, 0))
    return pl.pallas_call(
        body, name=name, grid=(S // tm,),
        in_specs=[_row_spec(tm, D), pl.BlockSpec((None, F2 // 2, D), lambda i: (0, 0, 0), **resident),
                  _row_spec(tm, F2), pl.BlockSpec((None, D, F2), lambda i: (0, 0, 0), **resident),
                  _row_spec(tm, D), _row_spec(tm, D), _vec_spec(1, D), _row_spec(tm, D), _vec_spec(1, D)],
        out_specs=[_row_spec(tm, F2), _row_spec(tm, D), acc_spec, acc_spec, _row_spec(tm, D), _vec_spec(8, D)],
        out_shape=[jax.ShapeDtypeStruct((S, F2), BF16), jax.ShapeDtypeStruct((S, D), F32),
                   jax.ShapeDtypeStruct((1, 8, D), F32), jax.ShapeDtypeStruct((1, 8, D), F32),
                   jax.ShapeDtypeStruct((S, D), BF16), jax.ShapeDtypeStruct((8, D), F32)],
        compiler_params=_params("arbitrary"),
    )(dy, w_out, gu, w_in, x, dxn, scale, *post)


HALO = 16


def _conv_terms(bcx_ref, prev_ref, i, tm, D):
    b = bcx_ref[:, 0:D].astype(F32)
    cg = bcx_ref[:, D:2 * D].astype(F32)
    xin = bcx_ref[:, 2 * D:3 * D].astype(F32)
    z = cg * xin
    zp = prev_ref[:, D:2 * D].astype(F32) * prev_ref[:, 2 * D:3 * D].astype(F32)
    zp = jnp.where(i > 0, zp, 0.0)
    z_ext = jnp.concatenate([zp, z], axis=0)
    z1 = pltpu.roll(z_ext, 1, 0)[HALO:, :]
    z2 = pltpu.roll(z_ext, 2, 0)[HALO:, :]
    return b, cg, xin, z, z1, z2


def _conv_gate(bcx, ck, name):
    S, D3 = bcx.shape
    D = D3 // 3
    tm = _pick(S, 512, 16)
    hb = tm // HALO

    def body(bcx_ref, prev_ref, ck_ref, o_ref):
        i = pl.program_id(0)
        b, _, _, z, z1, z2 = _conv_terms(bcx_ref, prev_ref, i, tm, D)
        conv = ck_ref[0:1, :] * z2 + ck_ref[1:2, :] * z1 + ck_ref[2:3, :] * z
        o_ref[...] = (b * conv).astype(BF16)

    return pl.pallas_call(
        body, name=name, grid=(S // tm,),
        in_specs=[_row_spec(tm, D3),
                  pl.BlockSpec((HALO, D3), lambda i: (jnp.maximum(i * hb - 1, 0), 0)),
                  _vec_spec(8, D)],
        out_specs=_row_spec(tm, D),
        out_shape=jax.ShapeDtypeStruct((S, D), BF16),
        compiler_params=_params("parallel"),
    )(bcx, bcx, ck)


def _conv_gate_bwd(du, bcx, ck, name):
    S, D3 = bcx.shape
    D = D3 // 3
    tm = _pick(S, 512, 16)
    hb = tm // HALO
    nt = S // tm

    def body(du_ref, dun_ref, bcx_ref, prev_ref, next_ref, ck_ref, o_ref, dk_ref):
        i = pl.program_id(0)
        b, cg, xin, z, z1, z2 = _conv_terms(bcx_ref, prev_ref, i, tm, D)
        k0, k1, k2 = ck_ref[0:1, :], ck_ref[1:2, :], ck_ref[2:3, :]
        conv = k0 * z2 + k1 * z1 + k2 * z
        d = du_ref[...].astype(F32)
        dconv = d * b
        dcn = jnp.where(i < nt - 1, dun_ref[...].astype(F32) * next_ref[:, 0:D].astype(F32), 0.0)
        d_ext = jnp.concatenate([dconv, dcn], axis=0)
        d1 = pltpu.roll(d_ext, tm + HALO - 1, 0)[:tm, :]
        d2 = pltpu.roll(d_ext, tm + HALO - 2, 0)[:tm, :]
        dz = k2 * dconv + k1 * d1 + k0 * d2
        o_ref[:, 0:D] = (d * conv).astype(BF16)
        o_ref[:, D:2 * D] = (dz * xin).astype(BF16)
        o_ref[:, 2 * D:3 * D] = (dz * cg).astype(BF16)

        @pl.when(i == 0)
        def _():
            dk_ref[...] = jnp.zeros_like(dk_ref)

        dk_ref[0] += _colsum8(dconv * z2)
        dk_ref[1] += _colsum8(dconv * z1)
        dk_ref[2] += _colsum8(dconv * z)

    last = S // HALO - 1
    return pl.pallas_call(
        body, name=name, grid=(nt,),
        in_specs=[_row_spec(tm, D),
                  pl.BlockSpec((HALO, D), lambda i: (jnp.minimum((i + 1) * hb, last), 0)),
                  _row_spec(tm, D3),
                  pl.BlockSpec((HALO, D3), lambda i: (jnp.maximum(i * hb - 1, 0), 0)),
                  pl.BlockSpec((HALO, D3), lambda i: (jnp.minimum((i + 1) * hb, last), 0)),
                  _vec_spec(8, D)],
        out_specs=[_row_spec(tm, D3), pl.BlockSpec((3, 8, D), lambda i: (0, 0, 0))],
        out_shape=[jax.ShapeDtypeStruct((S, D3), BF16), jax.ShapeDtypeStruct((3, 8, D), F32)],
        compiler_params=_params("arbitrary"),
    )(du, du, bcx, bcx, bcx, ck)


def _rel_onehot():
    a = np.arange(CHUNK)[:, None]
    b = np.arange(CHUNK)[None, :]
    idx = np.stack([np.clip((N_LEFT_CHUNKS - dl) * CHUNK + a - b, -MAX_REL, MAX_REL) + MAX_REL
                    for dl in (6, 7, 8)]).reshape(-1)
    return (jnp.asarray(idx)[:, None] == jnp.arange(N_REL)[None, :]).astype(F32)


def _bias_table(rel_bias, name):
    H = rel_bias.shape[0]
    near = jnp.dot(rel_bias, _rel_onehot().T, precision=lax.Precision.HIGHEST).reshape(H, 3, CHUNK, CHUNK)
    far = jnp.broadcast_to(rel_bias[:, N_REL - 1][:, None, None], (H, CHUNK, CHUNK))

    def body(near_ref, far_ref, o_ref):
        neg = jnp.full((CHUNK, CHUNK), NEG, F32)
        for v in range(N_WIN):
            for ic in range(Q_CHUNKS):
                for jc in range(N_WIN * Q_CHUNKS):
                    dl = jc - ic
                    if dl < 0 or dl > N_LEFT_CHUNKS or jc < (N_WIN - 1 - v) * Q_CHUNKS:
                        blk = neg
                    else:
                        blk = far_ref[...] if dl <= 5 else near_ref[dl - 6]
                    o_ref[v, ic * CHUNK:(ic + 1) * CHUNK, jc * CHUNK:(jc + 1) * CHUNK] = blk

    return pl.pallas_call(
        body, name=name, grid=(H,),
        in_specs=[pl.BlockSpec((None, 3, CHUNK, CHUNK), lambda h: (h, 0, 0, 0)),
                  pl.BlockSpec((None, CHUNK, CHUNK), lambda h: (h, 0, 0))],
        out_specs=pl.BlockSpec((N_WIN, None, BQ, N_WIN * BQ), lambda h: (0, h, 0, 0)),
        out_shape=jax.ShapeDtypeStruct((N_WIN, H, BQ, N_WIN * BQ), F32),
        compiler_params=_params("parallel"),
    )(near, far)


NEAR_FIRST = 6
SLAB_ROWS = 2 * CHUNK
SLAB_COLS = 4 * CHUNK


def _slab(pair):
    c0 = (NEAR_FIRST + 2 * pair) * CHUNK
    return slice(pair * SLAB_ROWS, (pair + 1) * SLAB_ROWS), slice(c0, c0 + SLAB_COLS)


def _bias_table_grad(dslab):
    H = dslab.shape[0]

    def blk(ic, dl):
        pair, r, col = ic // 2, ic % 2, ic + dl - NEAR_FIRST - 2 * (ic // 2)
        return dslab[:, pair, r * CHUNK:(r + 1) * CHUNK, col * CHUNK:(col + 1) * CHUNK]

    by_dl = [sum(blk(ic, dl) for ic in range(Q_CHUNKS)) for dl in (6, 7, 8)]
    near = jnp.stack(by_dl, axis=1).reshape(H, 3 * CHUNK * CHUNK)
    g = jnp.dot(near, _rel_onehot(), precision=lax.Precision.HIGHEST)
    return g.at[:, N_REL - 1].add(-jnp.sum(near, axis=1))


def _attn_specs(nblk, W):
    last = nblk - 1
    q_spec = pl.BlockSpec((BQ, W), lambda g, i: (jnp.minimum(i, last), g))
    kv_specs = [pl.BlockSpec((BQ, 2 * W), functools.partial(
        lambda g, i, w: (jnp.maximum(jnp.minimum(i, last) - (N_WIN - 1) + w, 0), g), w=w)) for w in range(N_WIN)]
    tab_spec = pl.BlockSpec((None, HEADS_PER_STEP, BQ, N_WIN * BQ),
                            lambda g, i: (jnp.minimum(i, N_WIN - 1), g, 0, 0))
    dtab_spec = pl.BlockSpec((HEADS_PER_STEP, Q_CHUNKS // 2, SLAB_ROWS, SLAB_COLS), lambda g, i: (g, 0, 0, 0))
    return q_spec, kv_specs, tab_spec, dtab_spec


def _attn_scores(q_ref, kT, tab_ref, h, dh):
    return jnp.dot(q_ref[:, h * dh:(h + 1) * dh], kT[h * dh:(h + 1) * dh, :], preferred_element_type=F32) + tab_ref[h]


def _attn_fwd(q, kv, tab, name):
    S, D = q.shape
    dh = D // N_HEADS
    W = HEADS_PER_STEP * dh
    assert 2 * W == D, "the kv layout puts one head group's k beside its v: two head groups"
    q_spec, kv_specs, tab_spec, _ = _attn_specs(S // BQ, W)

    def body(q_ref, *rest):
        tab_ref, o_ref = rest[N_WIN], rest[N_WIN + 1]
        kvw = jnp.concatenate([r[...] for r in rest[:N_WIN]], axis=0)
        kT = kvw[:, :W].T
        vw = kvw[:, W:]
        outs = []
        s = _attn_scores(q_ref, kT, tab_ref, 0, dh)
        for h in range(HEADS_PER_STEP):
            s_next = _attn_scores(q_ref, kT, tab_ref, h + 1, dh) if h + 1 < HEADS_PER_STEP else None
            e = jnp.exp(s - jnp.max(s, axis=-1, keepdims=True))
            l = jnp.sum(e, axis=-1, keepdims=True)
            outs.append(jnp.dot(e.astype(BF16), vw[:, h * dh:(h + 1) * dh], preferred_element_type=F32) / l)
            s = s_next
        o_ref[...] = jnp.concatenate(outs, axis=1).astype(BF16)

    return pl.pallas_call(
        body, name=name, grid=(N_HEADS // HEADS_PER_STEP, S // BQ),
        in_specs=[q_spec] + kv_specs + [tab_spec],
        out_specs=q_spec,
        out_shape=jax.ShapeDtypeStruct((S, D), BF16),
        compiler_params=_params("parallel", "parallel"),
    )(q, *([kv] * N_WIN), tab)


def _attn_bwd(q, kv, tab, do, name):
    S, D = q.shape
    dh = D // N_HEADS
    W = HEADS_PER_STEP * dh
    nblk = S // BQ
    q_spec, kv_specs, tab_spec, dtab_spec = _attn_specs(nblk, W)

    def body(q_ref, *rest):
        tab_ref, do_ref, dq_ref, dkv_ref, dtab_ref, ring = rest[N_WIN:]
        i = pl.program_id(1)

        @pl.when(i == 0)
        def _():
            dtab_ref[...] = jnp.zeros_like(dtab_ref)
            ring[...] = jnp.zeros_like(ring)

        @pl.when(i < nblk)
        def _():
            kvw = jnp.concatenate([r[...] for r in rest[:N_WIN]], axis=0)
            kT = kvw[:, :W].T
            vw = kvw[:, W:]
            qT = q_ref[...].T
            dqs, dks, dvs = [], [], []

            s = _attn_scores(q_ref, kT, tab_ref, 0, dh)
            for h in range(HEADS_PER_STEP):
                hd = slice(h * dh, (h + 1) * dh)
                do_h = do_ref[:, hd]
                dp = lax.dot_general(do_h, vw[:, hd], _DIMS["nt"], preferred_element_type=F32)
                e = jnp.exp(s - jnp.max(s, axis=-1, keepdims=True))
                inv_l = 1.0 / jnp.sum(e, axis=-1, keepdims=True)
                if h + 1 < HEADS_PER_STEP:
                    s = _attn_scores(q_ref, kT, tab_ref, h + 1, dh)
                delta = jnp.sum(e * dp, axis=-1, keepdims=True) * inv_l
                ds = e * ((dp - delta) * inv_l)
                for pair in range(Q_CHUNKS // 2):
                    rows, cols = _slab(pair)
                    dtab_ref[h, pair] += ds[rows, cols]
                dsb = ds.astype(BF16)
                dqs.append(lax.dot_general(kT[hd, :], dsb, _DIMS["nt"], preferred_element_type=F32) * (dh ** -0.5))
                dks.append(jnp.dot(qT[hd, :], dsb, preferred_element_type=F32))
                do_s = (do_h.astype(F32) * inv_l).astype(BF16)
                dvs.append(jnp.dot(do_s.T, e.astype(BF16), preferred_element_type=F32))
            dq_ref[...] = jnp.concatenate(dqs, axis=0).T.astype(BF16)
            dkv = jnp.concatenate(dks + dvs, axis=0).T
            for w in range(N_WIN):
                slot = lax.rem(i + 1 + w, N_WIN)
                part = dkv[w * BQ:(w + 1) * BQ, :]
                if w == N_WIN - 1:
                    ring[slot] = part
                else:
                    ring[slot] += part

        dkv_ref[...] = ring[lax.rem(i + 1, N_WIN)].astype(BF16)

    done_spec = pl.BlockSpec((BQ, 2 * W), lambda g, i: (jnp.maximum(i - (N_WIN - 1), 0), g))
    return pl.pallas_call(
        body, name=name, grid=(N_HEADS // HEADS_PER_STEP, nblk + N_WIN - 1),
        in_specs=[q_spec] + kv_specs + [tab_spec, q_spec],
        out_specs=[q_spec, done_spec, dtab_spec],
        out_shape=[jax.ShapeDtypeStruct((S, D), BF16), jax.ShapeDtypeStruct((S, 2 * D), BF16),
                   jax.ShapeDtypeStruct((N_HEADS, Q_CHUNKS // 2, SLAB_ROWS, SLAB_COLS), F32)],
        scratch_shapes=[pltpu.VMEM((N_WIN, BQ, 2 * W), F32)],
        compiler_params=_params("parallel", "arbitrary"),
    )(q, *([kv] * N_WIN), tab, do)


def _adamw(w, g, m, v, name, echo=False):
    shape = w.shape
    C = shape[-1]
    R = int(np.prod(shape[:-1])) if len(shape) > 1 else 1
    whole = len(shape) >= 2 and R * C <= SMALL_TENSOR_ELEMS
    if whole:
        w2, g2, m2, v2 = w, g, m, v
    else:
        w2, g2, m2, v2 = (t.reshape(R, C) for t in (w, g, m, v))
    tr = _pick(R, max(8, (512 * 1024) // C // 8 * 8), 8)

    def body(w_ref, g_ref, m_ref, v_ref, *out_refs):
        d_ref, nm_ref, nv_ref = out_refs[-3:]
        gv = g_ref[...]
        if echo:
            out_refs[0][...] = gv
        nm = ADAM_B1 * m_ref[...] + (1.0 - ADAM_B1) * gv
        nv = ADAM_B2 * v_ref[...] + (1.0 - ADAM_B2) * jnp.square(gv)
        m_hat = nm / (1.0 - ADAM_B1 ** ADAM_STEP)
        v_hat = nv / (1.0 - ADAM_B2 ** ADAM_STEP)
        d_ref[...] = -ADAM_LR * (m_hat / (jnp.sqrt(v_hat) + ADAM_EPS) + ADAM_WD * w_ref[...])
        nm_ref[...] = nm
        nv_ref[...] = nv

    if whole:
        spec, grid = pl.BlockSpec(shape, lambda i: (0,) * len(shape)), (1,)
    else:
        spec, grid = pl.BlockSpec((tr, C), lambda i: (i, 0)), (R // tr,)
    outs = pl.pallas_call(
        body, name=name, grid=grid,
        in_specs=[spec] * 4, out_specs=[spec] * (3 + echo),
        out_shape=[jax.ShapeDtypeStruct(w2.shape, F32)] * (3 + echo),
        compiler_params=_params("parallel"),
    )(w2, g2, m2, v2)
    return tuple(o.reshape(shape) for o in outs)


def _sum_rows(a, name):
    n, L = a.shape

    def body(a_ref, o_ref):
        acc = a_ref[0:1, :]
        for r in range(1, n):
            acc = acc + a_ref[r:r + 1, :]
        o_ref[...] = acc

    return pl.pallas_call(
        body, name=name, grid=(1,),
        in_specs=[pl.BlockSpec((n, L), lambda i: (0, 0))],
        out_specs=pl.BlockSpec((1, L), lambda i: (0, 0)),
        out_shape=jax.ShapeDtypeStruct((1, L), F32),
        compiler_params=_params("arbitrary"),
    )(a)


def _scalar_call(body, name, scalar, grid, in_specs, out_spec, out_shape, args):
    return pl.pallas_call(
        body, name=name,
        grid_spec=pltpu.PrefetchScalarGridSpec(num_scalar_prefetch=1, grid=grid, in_specs=in_specs,
                                               out_specs=out_spec),
        out_shape=out_shape, compiler_params=_params("parallel"),
    )(jnp.reshape(scalar, (-1,)).astype(jnp.int32), *args)


def _pair_sum(view, got, c, name):
    nb, _, rh, cols = view.shape
    tr = _pick(rh, max(16, (1 << 20) // cols // 16 * 16), 16)
    bpr = rh // tr

    def body(s_ref, a_ref, b_ref, o_ref):
        o_ref[...] = (a_ref[...].astype(F32) + b_ref[...].astype(F32)).astype(BF16)

    spec = pl.BlockSpec((tr, cols), lambda i, s: (i, 0))
    mine = pl.BlockSpec((tr, cols), lambda i, s: ((2 * (i // bpr) + s[0]) * bpr + i % bpr, 0))
    return _scalar_call(body, name, c, (nb * bpr,), [mine, spec], spec,
                        jax.ShapeDtypeStruct((nb * rh, cols), BF16),
                        (view.reshape(nb * 2 * rh, cols), got.reshape(nb * rh, cols)))


STACKED_LAYERS = 2


def _owner_sum(pair, recv, me, c, it, name, layer=None, into=None):
    _, rh, bc = recv.shape
    tr = _pick(rh, max(16, (1 << 19) // bc // 16 * 16), 16)
    bpr = rh // tr

    def body(s_ref, a_ref, r0, r1, r2, *rest):
        rest[-1][...] = ((a_ref[...].astype(F32) + r0[...].astype(F32)) + r1[...].astype(F32)) + r2[...].astype(F32)

    if it.kind == "col":
        own = pl.BlockSpec((tr, bc), lambda i, s: (i, s[0]))
    else:
        own = pl.BlockSpec((tr, bc), lambda i, s: (s[0] * bpr + i, 0))
    slots = [pl.BlockSpec((None, tr, bc), functools.partial(lambda i, s, k: (k, i, 0), k=k)) for k in range(3)]
    in_specs, args, aliases = [own] + slots, [pair, recv, recv, recv], {}
    if layer is None:
        out_spec = pl.BlockSpec((tr, bc), lambda i, s: (s[1] * bpr + i, 0))
        out_shape = jax.ShapeDtypeStruct((2 * rh, bc), F32)
    else:
        out_spec = pl.BlockSpec((None, tr, bc), lambda i, s: (layer, s[1] * bpr + i, 0))
        out_shape = jax.ShapeDtypeStruct((STACKED_LAYERS, 2 * rh, bc), F32)
        if into is not None:
            in_specs.append(pl.BlockSpec(memory_space=pl.ANY))
            args.append(into)
            aliases = {len(args): 0}
    return pl.pallas_call(
        body, name=name,
        grid_spec=pltpu.PrefetchScalarGridSpec(num_scalar_prefetch=1, grid=(bpr,), in_specs=in_specs,
                                               out_specs=out_spec),
        out_shape=out_shape, input_output_aliases=aliases, compiler_params=_params("parallel"),
    )(jnp.stack([it.pos(me), c]).astype(jnp.int32), *args)


def _place():
    x, y, c = lax.axis_index("x"), lax.axis_index("y"), lax.axis_index("c")
    chips = [(1 - x, y), (x, 1 - y), (1 - x, 1 - y)]
    return x, y, c, chips


def _chip_index(px, py):
    return 2 * px + py


def _all_gather_small(x_shard, name):
    m_per, n = x_shard.shape

    def body(x_ref, out_ref, send_sems, recv_sems, local_sem):
        x, y, c, chips = _place()
        me, sibling = (x, y, c), (x, y, 1 - c)

        def rows(px, py, pc):
            return out_ref.at[pl.ds((4 * px + 2 * py + pc) * m_per, m_per), :]

        def copy(k, block, to, src=None):
            return pltpu.make_async_remote_copy(
                src_ref=rows(*block) if src is None else src, dst_ref=rows(*block),
                send_sem=send_sems.at[k], recv_sem=recv_sems.at[k], device_id=to, device_id_type=MESH)

        mine = pltpu.make_async_copy(x_ref, rows(*me), local_sem)
        mine.start()
        first = [copy(0, me, sibling, src=x_ref)]
        first += [copy(1 + j, me, (*chip, c), src=x_ref) for j, chip in enumerate(chips)]
        for cp in first:
            cp.start()
        passed = [copy(4 + j, (*chip, c), sibling) for j, chip in enumerate(chips)]
        for j, chip in enumerate(chips):
            copy(1 + j, (*chip, c), me).wait_recv()
            passed[j].start()
        copy(0, sibling, me).wait_recv()
        for j, chip in enumerate(chips):
            copy(4 + j, (*chip, 1 - c), me).wait_recv()
        for cp in first + passed:
            cp.wait_send()
        mine.wait()

    return pl.pallas_call(
        body, name=name,
        out_shape=jax.ShapeDtypeStruct((N_DEV * m_per, n), x_shard.dtype),
        in_specs=[pl.BlockSpec(memory_space=pltpu.VMEM)],
        out_specs=pl.BlockSpec(memory_space=pltpu.VMEM),
        scratch_shapes=[pltpu.SemaphoreType.DMA((7,)), pltpu.SemaphoreType.DMA((7,)), pltpu.SemaphoreType.DMA],
    )(x_shard)


def _gather_flat(vec, name):
    L = vec.shape[0]
    Lp = -(-L // 1024) * 1024
    g = _all_gather_small(jnp.pad(vec, (0, Lp - L)).reshape(8, Lp // 8), name)
    return g.reshape(N_DEV, Lp)[:, :L]


class _Item:
    def __init__(self, kind, rows, cols, arg, layer, swap=False):
        self.kind, self.rows, self.cols, self.arg, self.layer, self.swap = kind, rows, cols, arg, layer, swap

    def ref(self, refs):
        return refs[self.arg].at[self.layer]

    def pos(self, j):
        return 2 * (j % 2) + j // 2 if self.swap else j


def _block(ref, it, j, half):
    if it.kind == "col":
        ns = it.cols // N_CHIP
        return ref.at[pl.ds(half * (it.rows // 2), it.rows // 2), pl.ds(it.pos(j) * ns, ns)]
    rs = it.rows // N_CHIP
    return ref.at[pl.ds(j * rs + half * (rs // 2), rs // 2), :]


def _cast_place(w, layer, kind, pos, after, name):
    _, r, n = w.shape
    tr = _pick(r, max(16, (1 << 20) // n // 16 * 16), 16)
    bpr = r // tr

    def body(s_ref, w_ref, after_ref, o_ref):
        o_ref[...] = w_ref[...].astype(BF16)

    if kind == "col":
        full, out_idx = (1, r, N_CHIP * n), (lambda i, s: (0, i, s[0]))
    else:
        full, out_idx = (1, N_CHIP * r, n), (lambda i, s: (0, s[0] * bpr + i, 0))
    return pl.pallas_call(
        body, name=name,
        grid_spec=pltpu.PrefetchScalarGridSpec(
            num_scalar_prefetch=1, grid=(bpr,),
            in_specs=[pl.BlockSpec((None, tr, n), lambda i, s: (layer, i, 0)), pl.BlockSpec(memory_space=pl.ANY)],
            out_specs=pl.BlockSpec((None, tr, n), out_idx)),
        out_shape=jax.ShapeDtypeStruct(full, BF16),
        compiler_params=_params("parallel"),
    )(jnp.reshape(pos, (1,)).astype(jnp.int32), w, after)


HBM_SPEC = pl.BlockSpec(memory_space=pltpu.HBM)
SEM_SPEC = pl.BlockSpec(memory_space=pltpu.SEMAPHORE)
ANY_SPEC = pl.BlockSpec(memory_space=pl.ANY)
SPLIT_PARAMS = dict(has_side_effects=pltpu.SideEffectType.DATAFLOW_SIDE_EFFECTING)


def _in_hbm(a):
    return pltpu.with_memory_space_constraint(a, pltpu.HBM)


def _split_start(copies_of, bufs, n_sem, after, name):
    n = len(bufs)

    def body(*refs):
        ins, send, recv, token = refs[:n], refs[n + 1], refs[n + 2], refs[2 * n + 3]
        for cp in copies_of(ins, send, recv, False)[0]:
            cp.start()
        token[...] = jnp.zeros_like(token)

    outs = pl.pallas_call(
        body, name=name,
        out_shape=(pltpu.SemaphoreType.DMA(n_sem), pltpu.SemaphoreType.DMA(n_sem),
                   *[pltpu.HBM(b.shape, b.dtype) for b in bufs], jax.ShapeDtypeStruct((8, 128), F32)),
        in_specs=[HBM_SPEC] * n + [ANY_SPEC],
        out_specs=(SEM_SPEC, SEM_SPEC, *[HBM_SPEC] * n, pl.BlockSpec(memory_space=pltpu.VMEM)),
        input_output_aliases={t: 2 + t for t in range(n)},
        compiler_params=pltpu.CompilerParams(**SPLIT_PARAMS),
    )(*[_in_hbm(b) for b in bufs], after)
    return outs[0], outs[1], list(outs[2:2 + n]), outs[2 + n]


def _split_wait(copies_of, send, recv, bufs, after, name):
    n = len(bufs)
    after = list(after) if isinstance(after, (list, tuple)) else [after]

    def body(*refs):
        ins, send_ref, recv_ref = refs[:n], refs[n], refs[n + 1]
        sends, arrivals = copies_of(ins, send_ref, recv_ref, True)
        for cp in sends:
            cp.wait_send()
        for cp in arrivals:
            cp.wait_recv()

    return pl.pallas_call(
        body, name=name,
        out_shape=[pltpu.HBM(b.shape, b.dtype) for b in bufs],
        in_specs=[HBM_SPEC] * n + [SEM_SPEC, SEM_SPEC] + [ANY_SPEC] * len(after),
        out_specs=[HBM_SPEC] * n,
        input_output_aliases={t: t for t in range(n)},
        compiler_params=pltpu.CompilerParams(**SPLIT_PARAMS),
    )(*bufs, send, recv, *after)


def _gather_copies(items):
    def copies_of(refs, send, recv, with_arrivals):
        x, y, c, chips = _place()
        me = _chip_index(x, y)
        sends, arrivals = [], []
        for t, it in enumerate(items):
            for k, chip in enumerate(chips):
                for core in range(2):
                    mine = _block(it.ref(refs), it, me, c)
                    sends.append(pltpu.make_async_remote_copy(
                        src_ref=mine, dst_ref=mine, send_sem=send.at[6 * t + 2 * k + core],
                        recv_sem=recv.at[6 * t + 2 * k + c], device_id=(*chip, core), device_id_type=MESH))
                    if with_arrivals:
                        landed = _block(it.ref(refs), it, _chip_index(*chip), core)
                        arrivals.append(pltpu.make_async_remote_copy(
                            src_ref=landed, dst_ref=landed, send_sem=send.at[6 * t + 2 * k + core],
                            recv_sem=recv.at[6 * t + 2 * k + core], device_id=(*chip, core), device_id_type=MESH))
        return sends, arrivals

    return copies_of


def _owner_copies(items):
    n = len(items)

    def blk(ref, it, j):
        if it.kind == "col":
            ns = it.cols // N_CHIP
            return ref.at[:, pl.ds(it.pos(j) * ns, ns)]
        return ref.at[j]

    def copies_of(refs, send, recv, with_arrivals):
        x, y, c, chips = _place()
        sends, arrivals = [], []
        for t, it in enumerate(items):
            for k, chip in enumerate(chips):
                slot = refs[n + t].at[k]
                sends.append(pltpu.make_async_remote_copy(
                    src_ref=blk(refs[t], it, _chip_index(*chip)), dst_ref=slot, send_sem=send.at[3 * t + k],
                    recv_sem=recv.at[3 * t + k], device_id=(*chip, c), device_id_type=MESH))
                if with_arrivals:
                    arrivals.append(pltpu.make_async_remote_copy(
                        src_ref=slot, dst_ref=slot, send_sem=send.at[3 * t + k], recv_sem=recv.at[3 * t + k],
                        device_id=(*chip, c), device_id_type=MESH))
        return sends, arrivals

    return copies_of


def _owner_slot_shape(it):
    if it.kind == "col":
        return (3, it.rows // 2, it.cols // N_CHIP)
    return (3, it.rows // (2 * N_CHIP), it.cols)


def _pair_view(g, it):
    if it.kind == "col":
        return g.reshape(1, 2, it.rows // 2, it.cols)
    return g.reshape(N_CHIP, 2, it.rows // (2 * N_CHIP), it.cols)


def _pair_copies(n):
    def copies_of(refs, send, recv, with_arrivals):
        x, y, c, _ = _place()
        sends, arrivals = [], []
        for t in range(n):
            land = refs[n + t]
            sends.append(pltpu.make_async_remote_copy(
                src_ref=refs[t].at[:, pl.ds(1 - c, 1)], dst_ref=land, send_sem=send.at[t], recv_sem=recv.at[t],
                device_id=(x, y, 1 - c), device_id_type=MESH))
            if with_arrivals:
                arrivals.append(pltpu.make_async_remote_copy(
                    src_ref=land, dst_ref=land, send_sem=send.at[t], recv_sem=recv.at[t],
                    device_id=(x, y, 1 - c), device_id_type=MESH))
        return sends, arrivals

    return copies_of


def _half_copies(n):
    def half(ref, which):
        r2 = ref.shape[-2] // 2
        rows = pl.ds(which * r2, r2)
        return ref.at[rows, :] if len(ref.shape) == 2 else ref.at[:, rows, :]

    def copies_of(refs, send, recv, with_arrivals):
        x, y, c, _ = _place()
        sends, arrivals = [], []
        for t in range(n):
            mine = half(refs[t], c)
            sends.append(pltpu.make_async_remote_copy(
                src_ref=mine, dst_ref=mine, send_sem=send.at[t], recv_sem=recv.at[t],
                device_id=(x, y, 1 - c), device_id_type=MESH))
            if with_arrivals:
                theirs = half(refs[t], 1 - c)
                arrivals.append(pltpu.make_async_remote_copy(
                    src_ref=theirs, dst_ref=theirs, send_sem=send.at[t], recv_sem=recv.at[t],
                    device_id=(x, y, 1 - c), device_id_type=MESH))
        return sends, arrivals

    return copies_of


class _Reduction:
    pass


def _pair_start(grads, items, after, tag, names, layer=None):
    n = len(items)
    views = [_pair_view(g, it) for g, it in zip(grads, items)]
    lands = [lax.empty((v.shape[0], 1) + v.shape[2:], v.dtype) for v in views]
    r = _Reduction()
    r.items, r.tag, r.names, r.layer = items, tag, names, layer
    r.send, r.recv, r.bufs, r.token = _split_start(_pair_copies(n), views + lands, (n,), after, f"rs_pair_start_{tag}")
    return r


def _owner_start(r, after):
    x, y, c, _ = _place()
    n = len(r.items)
    bufs = _split_wait(_pair_copies(n), r.send, r.recv, r.bufs, after, f"rs_pair_wait_{r.tag}")
    pairs = [_pair_sum(bufs[t], bufs[n + t], c, f"rs_pair_sum_{r.tag}_{t}") for t in range(n)]
    shaped = [p if it.kind == "col" else p.reshape(N_CHIP, p.shape[0] // N_CHIP, p.shape[1])
              for p, it in zip(pairs, r.items)]
    lands = [lax.empty(_owner_slot_shape(it), BF16) for it in r.items]
    r.send, r.recv, r.bufs, r.token = _split_start(
        _owner_copies(r.items), shaped + lands, (3 * n,), r.token, f"rs_owner_start_{r.tag}")
    return r


def _reduce_finish(groups, after):
    x, y, c, _ = _place()
    me = _chip_index(x, y)
    halves = {}
    behind = [after]
    for r in groups:
        n = len(r.items)
        bufs = _split_wait(_owner_copies(r.items), r.send, r.recv, r.bufs, behind, f"rs_owner_wait_{r.tag}")
        for t, (it, nm) in enumerate(zip(r.items, r.names)):
            pair = bufs[t].reshape(-1, bufs[t].shape[-1])
            halves[nm] = _owner_sum(pair, bufs[n + t], me, c, it, f"rs_owner_sum_{r.tag}_{t}",
                                    layer=r.layer, into=halves.get(nm))
        behind = [after] + [halves[nm] for nm in r.names]
    n = len(halves)
    return list(halves), _split_start(_half_copies(n), list(halves.values()), (n,), after, "rs_half_start")


def _silu(v):
    return v * jax.nn.sigmoid(v)


def _sum8(p):
    return jnp.sum(p, axis=-2)


def kernel(x, c, mod_w, mod_b, norm_g, ffn_w_in, ffn_w_out, conv_w_in, conv_k, conv_w_out, kv_mod_w, kv_mod_b, kv_norm_g, w_kv, attn_w_q, attn_w_o, rel_bias, loss_target, m_mod_w, m_mod_b, m_norm_g, m_ffn_w_in, m_ffn_w_out, m_conv_w_in, m_conv_k, m_conv_w_out, m_kv_mod_w, m_kv_mod_b, m_kv_norm_g, m_w_kv, m_attn_w_q, m_attn_w_o, m_rel_bias, v_mod_w, v_mod_b, v_norm_g, v_ffn_w_in, v_ffn_w_out, v_conv_w_in, v_conv_k, v_conv_w_out, v_kv_mod_w, v_kv_mod_b, v_kv_norm_g, v_w_kv, v_attn_w_q, v_attn_w_o, v_rel_bias):
    xi, yi, ci = lax.axis_index("x"), lax.axis_index("y"), lax.axis_index("c")
    chip = 2 * xi + yi
    dev = 2 * chip + ci
    _, S, D = x.shape
    F = ffn_w_out.shape[1] * N_CHIP
    x0 = x.reshape(S, D)
    target = loss_target.reshape(S, D)
    n_mod = mod_w.shape[2]
    n_kvm = kv_mod_w.shape[1]
    dsh = D // N_CHIP
    TF = F // 2

    c_all = _all_gather_small(c.reshape(8, D // 8), "ag_c").reshape(N_DEV, D)
    sc16 = jnp.pad(_silu(c_all), ((0, 8), (0, 0)))
    part = [_mm(sc16, mod_w, "nn", F32, f"mod_fwd_{l}", b_layer=l)[:8] for l in range(2)]
    part.append(_mm(sc16, kv_mod_w, "nn", F32, "mod_fwd_kv")[:8])
    fwd_vec = jnp.concatenate([p.reshape(-1) for p in part] + [norm_g.reshape(-1), conv_k.reshape(-1)])
    fwd_all = _gather_flat(fwd_vec, "ag_fwd_small")[0::2]
    o = 0
    mods = []
    for n in (n_mod, n_mod, n_kvm):
        blk = fwd_all[:, o:o + 8 * n].reshape(N_CHIP, 8, n)
        mods.append(lax.dynamic_index_in_dim(blk, dev, axis=1, keepdims=False).reshape(N_CHIP * n))
        o += 8 * n
    ng = fwd_all[:, o:o + 8 * dsh].reshape(N_CHIP, 2, 4, dsh).transpose(1, 2, 0, 3).reshape(2, 4, D)
    o += 8 * dsh
    ck = fwd_all[:, o:o + 3 * dsh].reshape(N_CHIP, 3, dsh).transpose(1, 0, 2).reshape(3, D)
    ck8 = jnp.pad(ck, ((0, 5), (0, 0)))
    mod = [mods[l] + mod_b[l] for l in range(2)]
    sh1, sc1, g1, sh2, sc2, g2 = zip(*[jnp.split(m, 6) for m in mod])
    kv_sh, kv_sc = jnp.split(mods[2] + kv_mod_b, 2)
    row = lambda v: v.reshape(1, D)

    it_conv = [_Item("col", D, 3 * D, 0, 0), _Item("row", D, D, 1, 0)]
    it_ffn = [_Item("col", D, 2 * F, 0, 0, swap=True), _Item("row", F, D, 1, 0)]
    it_attn = [_Item("col", D, 2 * D, 0, 0, swap=True), _Item("row", D, D, 1, 0), _Item("row", D, D, 2, 0)]

    def placed(w, layer, it, nm, after=fwd_all):
        return _cast_place(w, layer, it.kind, it.pos(chip), after, f"place_{nm}")

    flying = {}

    def start(tag, its, bufs, after):
        send, recv, bufs, tok = _split_start(_gather_copies(its), bufs, (6 * len(its),), after, f"ag_start_{tag}")
        flying[tag] = (its, send, recv, bufs)
        return tok

    def arrived(tag, after):
        its, send, recv, bufs = flying[tag]
        return _split_wait(_gather_copies(its), send, recv, bufs, after, f"ag_wait_{tag}")

    one = lambda it: [_Item(it.kind, it.rows, it.cols, 0, 0, it.swap)]
    tok = start("conv_in", one(it_conv[0]), [placed(conv_w_in, 0, it_conv[0], "conv_w_in")], fwd_all)
    tok = start("conv_out", one(it_conv[1]), [placed(conv_w_out, 0, it_conv[1], "conv_w_out", tok)], tok)
    tok = start("ffn0_in", one(it_ffn[0]), [placed(ffn_w_in, 0, it_ffn[0], "ffn_w_in0", tok)], tok)
    tok = start("ffn0_out", one(it_ffn[1]), [placed(ffn_w_out, 0, it_ffn[1], "ffn_w_out0", tok)], tok)
    tok = start("attn", it_attn, [placed(w_kv[None], 0, it_attn[0], "w_kv", tok),
                                  placed(attn_w_q, 0, it_attn[1], "attn_w_q", tok),
                                  placed(attn_w_o, 0, it_attn[2], "attn_w_o", tok)], tok)
    token = start("ffn1", it_ffn, [placed(ffn_w_in, 1, it_ffn[0], "ffn_w_in1", tok),
                                   placed(ffn_w_out, 1, it_ffn[1], "ffn_w_out1", tok)], tok)

    a1 = row(ng[0, 0] * (1.0 + sc1[0])) + token[0, 0]
    (h1,) = _norm_mod(x0, a1, row(sh1[0]), "l0_norm1")
    tab = _bias_table(rel_bias[0], "l1_bias_table")
    h1, tab = lax.optimization_barrier((h1, tab))
    (W_cin,) = arrived("conv_in", h1)
    bcx = _mm(h1, W_cin, "nn", BF16, "l0_conv_in", b_layer=0, tm=512, tn=3 * D)
    ug = _conv_gate(bcx, ck8, "l0_conv_gate")
    gt1 = row(g1[0] * ng[0, 1])
    a2 = row(ng[0, 2] * (1.0 + sc2[0]))
    (W_cout,) = arrived("conv_out", ug)
    y1, x1, h2 = _mm_post(ug, W_cout, x0, gt1, "l0_conv_out", scales=a2, shifts=row(sh2[0]))
    (W_fin0,) = arrived("ffn0_in", h2)
    gu0, act0 = _ffn_in_act(h2, W_fin0, 0, "l0_ffn_in")
    (W_fout0,) = arrived("ffn0_out", act0)
    gt2 = row(g2[0] * ng[0, 3])
    a3 = ng[1, 0] * (1.0 + sc1[1])
    akv = kv_norm_g * (1.0 + kv_sc)
    y2, x2, h3, hkv = _mm_post(act0, W_fout0, x1, gt2, "l0_ffn_out",
                               scales=jnp.stack([a3, akv]), shifts=jnp.stack([sh1[1], kv_sh]))
    W_kv, W_q, W_o = arrived("attn", hkv)
    kvp = _mm(hkv, W_kv, "nn", BF16, "l1_kv", b_layer=0, tm=512, tn=2 * D)
    att_scale = (D // N_HEADS) ** -0.5
    assert math.log2(att_scale) % 1 == 0, "scaling q before its bf16 cast is exact only for a power of two"
    qp = _mm(h3, W_q, "nn", BF16, "l1_q", b_layer=0, scale=att_scale)
    oh = _attn_fwd(qp, kvp, tab, "l1_attn")
    gt3 = row(g1[1] * ng[1, 1])
    a4 = row(ng[1, 2] * (1.0 + sc2[1]))
    y3, x3, h4 = _mm_post(oh, W_o, x2, gt3, "l1_attn_out", scales=a4, shifts=row(sh2[1]))
    W_fin1, W_fout1 = arrived("ffn1", h4)
    gu1, act1 = _ffn_in_act(h4, W_fin1, 0, "l1_ffn_in")
    gt4 = row(g2[1] * ng[1, 3])
    dx4, sq, dy4, dgt4 = _mm_post(act1, W_fout1, x3, gt4, "l1_ffn_out", target=target)
    loss_part = 0.5 * jnp.sum(sq) / D

    def ffn_bwd(dy, dxn, xin_, h, gu, act, a, w_in, w_out, post, tag):
        dgu, dx, ds, db, dyn, dgt = _ffn_bwd(dy, w_out, gu, w_in, xin_, dxn, a, post, f"{tag}_ffn_bwd")
        g_fout = _mm(act, dy, "tn", BF16, f"{tag}_ffn_out_dw", tm=TF)
        g_fin = _mm(h, dgu, "tn", BF16, f"{tag}_ffn_in_dw", tn=TF)
        return dx, ds, db, dyn, dgt, g_fin, g_fout

    dx3, ds4, db4, dy3, dgt3, G_fin1, G_fout1 = ffn_bwd(dy4, dx4, x3, h4, gu1, act1, a4, W_fin1, W_fout1,
                                                        (y3, gt3), "l1")
    red = [_pair_start([G_fin1, G_fout1], it_ffn, token, "ffn1", ["ffn_w_in", "ffn_w_out"], layer=1)]
    doh = _mm(dy3, W_o, "nt", BF16, "l1_attn_out_dx", b_layer=0, after=red[0].token)
    G_o = _mm(oh, dy3, "tn", BF16, "l1_attn_out_dw")
    _owner_start(red[0], G_o)
    dq, dkv, dtab = _attn_bwd(qp, kvp, tab, doh, "l1_attn_bwd")
    d_rel = _bias_table_grad(dtab)
    G_q = _mm(h3, dq, "tn", BF16, "l1_q_dw")
    G_kv = _mm(hkv, dkv, "tn", BF16, "l1_kv_dw")
    red.append(_pair_start([G_kv, G_q, G_o], it_attn, red[-1].token, "attn", ["w_kv", "attn_w_q", "attn_w_o"]))
    dx2, ds3, db3, dy2, dgt2 = _mm_pre_bwd([(dq, W_q), (dkv, W_kv)], x2, dx3,
                                           jnp.stack([a3, akv]) + red[1].token[0, 0], "l1_qkv_dx", post=(y2, gt2))
    _owner_start(red[1], dx2)

    dx1, ds2, db2, dy1, dgt1, G_fin0, G_fout0 = ffn_bwd(dy2, dx2, x1, h2, gu0, act0, a2, W_fin0, W_fout0,
                                                        (y1, gt1), "l0")
    red.append(_pair_start([G_fin0, G_fout0], it_ffn, red[-1].token, "ffn0", ["ffn_w_in", "ffn_w_out"], layer=0))
    dug = _mm(dy1, W_cout, "nt", BF16, "l0_conv_out_dx", b_layer=0, after=red[2].token)
    G_cout = _mm(ug, dy1, "tn", BF16, "l0_conv_out_dw")
    dbcx, dck = _conv_gate_bwd(dug, bcx, ck8, "l0_conv_gate_bwd")
    _owner_start(red[2], dbcx)
    G_cin = _mm(h1, dbcx, "tn", BF16, "l0_conv_in_dw")
    red.append(_pair_start([G_cin, G_cout], it_conv, red[-1].token, "conv", ["conv_w_in", "conv_w_out"]))
    dx0, ds1, db1 = _mm_pre_bwd([(dbcx, W_cin)], x0, dx1, a1 + red[3].token[0, 0], "l0_conv_in_dx")
    ds1, db1 = _sum8(ds1)[0], _sum8(db1)[0]
    da2, db2 = _sum8(ds2)[0], _sum8(db2)[0]
    ds3, db3 = _sum8(ds3), _sum8(db3)
    da4, db4 = _sum8(ds4)[0], _sum8(db4)[0]
    dgt1, dgt2, dgt3, dgt4 = _sum8(dgt1), _sum8(dgt2), _sum8(dgt3), _sum8(dgt4)

    def dmod_of(l, ds_a, db_a, dgt_a, ds_b, db_b, dgt_b):
        return jnp.concatenate([db_a, ds_a * ng[l, 0], dgt_a * ng[l, 1], db_b, ds_b * ng[l, 2], dgt_b * ng[l, 3]])

    dmod0 = dmod_of(0, ds1, db1, dgt1, da2, db2, dgt2)
    dmod1 = dmod_of(1, ds3[0], db3[0], dgt3, da4, db4, dgt4)
    dkvmod = jnp.concatenate([db3[1], ds3[1] * kv_norm_g])
    dng = jnp.stack([
        jnp.stack([ds1 * (1.0 + sc1[0]), dgt1 * g1[0], da2 * (1.0 + sc2[0]), dgt2 * g2[0]]),
        jnp.stack([ds3[0] * (1.0 + sc1[1]), dgt3 * g1[1], da4 * (1.0 + sc2[1]), dgt4 * g2[1]])])
    dkvng = ds3[1] * (1.0 + kv_sc)
    small = [dmod0, dmod1, dkvmod, dng.reshape(-1), dkvng, _sum8(dck).reshape(-1), d_rel.reshape(-1),
             loss_part.reshape(1)]
    sizes = [int(s.shape[0]) for s in small]
    offs = np.concatenate([[0], np.cumsum(sizes)])
    bwd_all = _gather_flat(jnp.concatenate(small), "ag_bwd_small")
    _owner_start(red[3], bwd_all)
    Lb = bwd_all.shape[1]
    Lp = -(-Lb // 128) * 128
    tot = _sum_rows(jnp.pad(bwd_all, ((0, 0), (0, Lp - Lb))), "sum_small")[0]
    seg = lambda i: tot[offs[i]:offs[i + 1]]
    g_mod_b = jnp.stack([seg(0), seg(1)])
    g_kv_mod_b = seg(2)
    g_norm_g = lax.dynamic_slice_in_dim(seg(3).reshape(2, 4, D), chip * dsh, dsh, axis=2)
    g_kv_norm_g = seg(4)
    g_conv_k = lax.dynamic_slice_in_dim(seg(5).reshape(1, 3, D), chip * dsh, dsh, axis=2)
    g_rel_bias = seg(6).reshape(rel_bias.shape)
    loss = seg(7)[0]

    def dmod_rows(i, n):
        rows_ = lax.dynamic_slice_in_dim(bwd_all[:, offs[i]:offs[i + 1]], chip * n, n, axis=1)
        return jnp.pad(rows_, ((0, 8), (0, 0)))

    g_mod_w = _mm(sc16, jnp.concatenate([dmod_rows(0, n_mod), dmod_rows(1, n_mod)], axis=1), "tn", F32,
                  "mod_bwd", out_layers=STACKED_LAYERS)
    g_kv_mod_w = _mm(sc16, dmod_rows(2, n_kvm), "tn", F32, "mod_bwd_kv")

    grads = {
        "mod_w": g_mod_w, "mod_b": g_mod_b, "norm_g": g_norm_g, "conv_k": g_conv_k,
        "kv_mod_w": g_kv_mod_w, "kv_mod_b": g_kv_mod_b, "kv_norm_g": g_kv_norm_g, "rel_bias": g_rel_bias,
    }
    weights = dict(mod_w=mod_w, mod_b=mod_b, norm_g=norm_g, ffn_w_in=ffn_w_in, ffn_w_out=ffn_w_out,
                   conv_w_in=conv_w_in, conv_k=conv_k, conv_w_out=conv_w_out, kv_mod_w=kv_mod_w,
                   kv_mod_b=kv_mod_b, kv_norm_g=kv_norm_g, w_kv=w_kv, attn_w_q=attn_w_q, attn_w_o=attn_w_o,
                   rel_bias=rel_bias)
    m_in = dict(mod_w=m_mod_w, mod_b=m_mod_b, norm_g=m_norm_g, ffn_w_in=m_ffn_w_in, ffn_w_out=m_ffn_w_out,
                conv_w_in=m_conv_w_in, conv_k=m_conv_k, conv_w_out=m_conv_w_out, kv_mod_w=m_kv_mod_w,
                kv_mod_b=m_kv_mod_b, kv_norm_g=m_kv_norm_g, w_kv=m_w_kv, attn_w_q=m_attn_w_q,
                attn_w_o=m_attn_w_o, rel_bias=m_rel_bias)
    v_in = dict(mod_w=v_mod_w, mod_b=v_mod_b, norm_g=v_norm_g, ffn_w_in=v_ffn_w_in, ffn_w_out=v_ffn_w_out,
                conv_w_in=v_conv_w_in, conv_k=v_conv_k, conv_w_out=v_conv_w_out, kv_mod_w=v_kv_mod_w,
                kv_mod_b=v_kv_mod_b, kv_norm_g=v_kv_norm_g, w_kv=v_w_kv, attn_w_q=v_attn_w_q,
                attn_w_o=v_attn_w_o, rel_bias=v_rel_bias)
    names = list(weights)
    step = {}

    def update(n, echo=False):
        g = grads[n].reshape(weights[n].shape)
        outs = _adamw(weights[n], g, m_in[n], v_in[n], f"adamw_{n}", echo=echo)
        step[n] = outs if echo else (g, *outs)

    update("mod_w")
    reduced, (half_send, half_recv, half_bufs, _) = _reduce_finish(red, step["mod_w"][1])
    local = [n for n in grads if n != "mod_w"]
    for n in local:
        update(n)
    grads.update(zip(reduced, _split_wait(
        _half_copies(len(half_bufs)), half_send, half_recv, half_bufs, [step[n][1] for n in local], "rs_half_wait")))
    for n in reduced:
        update(n, echo=True)
    return (loss, dx0.reshape(x.shape), *[step[n][k] for k in range(4) for n in names])
```

```python
import functools
import math

import numpy as np
import jax
import jax.numpy as jnp
from jax import lax
from jax.experimental import pallas as pl
from jax.experimental.pallas import tpu as pltpu

CHUNK = 64
N_LEFT_CHUNKS = 8
N_HEADS = 16
MAX_REL = 2 * CHUNK
N_REL = 2 * MAX_REL + 1
EPS = 1e-6
ADAM_LR = 0.001
ADAM_B1 = 0.9
ADAM_B2 = 0.999
ADAM_EPS = 1e-08
ADAM_WD = 0.01
ADAM_STEP = 10

Q_CHUNKS = 4
BQ = Q_CHUNKS * CHUNK
N_WIN = 1 + N_LEFT_CHUNKS // Q_CHUNKS
HEADS_PER_STEP = 8
NEG = -1e30
N_DEV = 8
N_CHIP = 4
SMALL_TENSOR_ELEMS = 1 << 16
PIECE_ROWS = 256

BF16 = jnp.bfloat16
F32 = jnp.float32
V7X_VMEM_LIMIT_BYTES = 56 * 1024 * 1024
MESH = pl.DeviceIdType.MESH


def _pick(n, pref, align):
    t = min(pref, n)
    t -= t % align
    while t >= align:
        if n % t == 0:
            return t
        t -= align
    return n


def _params(*sem):
    return pltpu.CompilerParams(dimension_semantics=sem, vmem_limit_bytes=V7X_VMEM_LIMIT_BYTES)


def _colsum8(v):
    r, d = v.shape
    return v.reshape(r // 8, 8, d).sum(axis=0)


_DIMS = {"nn": (((1,), (0,)), ((), ())), "nt": (((1,), (1,)), ((), ())), "tn": (((0,), (0,)), ((), ()))}


def _mm(a, b, mode, out_dtype, name, *, b_layer=None, tm=1024, tn=1024, tk=None, scale=None, after=None,
        out_layers=1):
    if tk is None:
        tk = 2048 if mode == "tn" else 3072
    bs = b.shape[1:] if b_layer is not None else b.shape
    if mode == "nn":
        (M, K), (K2, N) = a.shape, bs
    elif mode == "nt":
        (M, K), (N, K2) = a.shape, bs
    else:
        (K, M), (K2, N) = a.shape, bs
    assert K == K2, (name, a.shape, b.shape)
    tm = _pick(M, tm, 128 if mode == "tn" else 16)
    tn = _pick(N // out_layers, tn, 128)
    tk = _pick(K, tk, 128 if mode != "tn" else 16)
    nk = K // tk
    assert scale is None or nk == 1, name
    dims = _DIMS[mode]
    extra = [] if after is None else [after]

    def body(a_ref, b_ref, *rest):
        o_ref, acc = rest[len(extra)], rest[len(extra) + 1:]
        p = lax.dot_general(a_ref[...].astype(BF16), b_ref[...].astype(BF16), dims,
                            preferred_element_type=F32)
        if nk == 1:
            o_ref[...] = (p if scale is None else p * scale).astype(o_ref.dtype)
        else:
            k = pl.program_id(2)

            @pl.when(k == 0)
            def _():
                acc[0][...] = p

            @pl.when(k > 0)
            def _():
                acc[0][...] += p

            @pl.when(k == nk - 1)
            def _():
                o_ref[...] = acc[0][...].astype(o_ref.dtype)

    a_spec = (pl.BlockSpec((tk, tm), lambda i, j, k: (k, i)) if mode == "tn"
              else pl.BlockSpec((tm, tk), lambda i, j, k: (i, k)))
    if mode == "nt":
        b_blk, b_idx = (tn, tk), (lambda i, j, k: (j, k))
    else:
        b_blk, b_idx = (tk, tn), (lambda i, j, k: (k, j))
    if b_layer is not None:
        b_spec = pl.BlockSpec((None,) + b_blk, lambda i, j, k: (b_layer,) + b_idx(i, j, k))
    else:
        b_spec = pl.BlockSpec(b_blk, b_idx)
    if out_layers > 1:
        per_layer = N // out_layers // tn
        o_spec = pl.BlockSpec((None, tm, tn), lambda i, j, k: (j // per_layer, i, j % per_layer))
        o_shape = (out_layers, M, N // out_layers)
    else:
        o_spec, o_shape = pl.BlockSpec((tm, tn), lambda i, j, k: (i, j)), (M, N)
    return pl.pallas_call(
        body, name=name,
        grid=(M // tm, N // tn, nk),
        in_specs=[a_spec, b_spec] + [pl.BlockSpec(memory_space=pl.ANY)] * len(extra),
        out_specs=o_spec,
        out_shape=jax.ShapeDtypeStruct(o_shape, out_dtype),
        scratch_shapes=[pltpu.VMEM((tm, tn), F32)] if nk > 1 else [],
        compiler_params=_params("parallel", "parallel", "arbitrary"),
    )(a, b, *extra)


def _row_spec(tm, d):
    return pl.BlockSpec((tm, d), lambda i: (i, 0))


def _vec_spec(r, d):
    return pl.BlockSpec((r, d), lambda i: (0, 0))


def _norm_mod(x, scales, shifts, name):
    S, D = x.shape
    nb = scales.shape[0]
    tm = _pick(S, 1024, 16)

    def body(x_ref, a_ref, b_ref, *o_refs):
        xv = x_ref[...]
        xh = xv * lax.rsqrt(jnp.mean(xv * xv, axis=-1, keepdims=True) + EPS)
        for n in range(nb):
            o_refs[n][...] = (xh * a_ref[n:n + 1, :] + b_ref[n:n + 1, :]).astype(BF16)

    return pl.pallas_call(
        body, name=name, grid=(S // tm,),
        in_specs=[_row_spec(tm, D), _vec_spec(nb, D), _vec_spec(nb, D)],
        out_specs=[_row_spec(tm, D)] * nb,
        out_shape=[jax.ShapeDtypeStruct((S, D), BF16)] * nb,
        compiler_params=_params("parallel"),
    )(x, scales, shifts)


def _mm_post(a, w, x, gate, name, *, scales=None, shifts=None, target=None):
    M, K = a.shape
    D = w.shape[2]
    tm = _pick(M, 1024 if K <= D else 512, 16)
    sub = _pick(tm, PIECE_ROWS, 16)
    nb = 0 if scales is None else scales.shape[0]

    def body(a_ref, w_ref, x_ref, g_ref, *rest):
        if target is None:
            sc_ref, sh_ref, y_ref, xn_ref = rest[:4]
            h_refs = rest[4:]
        else:
            t_ref, dx_ref, sq_ref, dy_ref, dg_ref = rest

            @pl.when(pl.program_id(0) == 0)
            def _():
                sq_ref[...] = jnp.zeros_like(sq_ref)
                dg_ref[...] = jnp.zeros_like(dg_ref)

        def product(r):
            return jnp.dot(a_ref[pl.ds(r * sub, sub), :], w_ref[...], preferred_element_type=F32)

        y = product(0)
        for r in range(tm // sub):
            rows = pl.ds(r * sub, sub)
            yb = y.astype(BF16)
            if r + 1 < tm // sub:
                y = product(r + 1)
            yv = yb.astype(F32)
            yh = yv * lax.rsqrt(jnp.mean(yv * yv, axis=-1, keepdims=True) + EPS)
            xn = x_ref[rows, :] + yh * g_ref[...]
            if target is None:
                y_ref[rows, :] = yb
                xn_ref[rows, :] = xn
                xh = xn * lax.rsqrt(jnp.mean(xn * xn, axis=-1, keepdims=True) + EPS)
                for n in range(nb):
                    h_refs[n][rows, :] = (xh * sc_ref[n:n + 1, :] + sh_ref[n:n + 1, :]).astype(BF16)
            else:
                e = xn - t_ref[rows, :]
                dx = e / D
                dx_ref[rows, :] = dx
                sq_ref[...] += _colsum8(e * e)
                dy, dxy = _post_norm_grad(dx, yb, g_ref[...])
                dy_ref[rows, :] = dy.astype(BF16)
                dg_ref[...] += _colsum8(dxy)

    ins = [a, w, x, gate]
    in_specs = [_row_spec(tm, K), pl.BlockSpec((None, K, D), lambda i: (0, 0, 0)), _row_spec(tm, D), _vec_spec(1, D)]
    if target is None:
        ins += [scales, shifts]
        in_specs += [_vec_spec(nb, D), _vec_spec(nb, D)]
        out_specs = [_row_spec(tm, D)] * (2 + nb)
        out_shape = [jax.ShapeDtypeStruct((M, D), BF16), jax.ShapeDtypeStruct((M, D), F32)] \
            + [jax.ShapeDtypeStruct((M, D), BF16)] * nb
    else:
        ins += [target]
        in_specs += [_row_spec(tm, D)]
        out_specs = [_row_spec(tm, D), _vec_spec(8, D), _row_spec(tm, D), _vec_spec(8, D)]
        out_shape = [jax.ShapeDtypeStruct((M, D), F32), jax.ShapeDtypeStruct((8, D), F32),
                     jax.ShapeDtypeStruct((M, D), BF16), jax.ShapeDtypeStruct((8, D), F32)]
    return pl.pallas_call(
        body, name=name, grid=(M // tm,), in_specs=in_specs, out_specs=out_specs, out_shape=out_shape,
        compiler_params=_params("arbitrary" if target is not None else "parallel"),
    )(*ins)


def _post_norm_grad(dxn, yb, gate):
    yv = yb.astype(F32)
    r = lax.rsqrt(jnp.mean(yv * yv, axis=-1, keepdims=True) + EPS)
    yh = yv * r
    dyh = dxn * gate
    return r * (dyh - yh * jnp.mean(dyh * yh, axis=-1, keepdims=True)), dxn * yh


def _mm_pre_bwd(pairs, x, dxn, scales, name, post=None):
    S, D = x.shape
    nb = len(pairs)
    tm = _pick(S, 512, 16)
    sub = _pick(tm, PIECE_ROWS, 16)

    def body(*refs):
        a_refs, w_refs = refs[0:2 * nb:2], refs[1:2 * nb:2]
        x_ref, d_ref, sc_ref = refs[2 * nb:2 * nb + 3]
        rest = refs[2 * nb + 3:]
        if post is not None:
            y_ref, g_ref, dx_ref, ds_ref, db_ref, dy_ref, dg_ref = rest
        else:
            dx_ref, ds_ref, db_ref = rest

        @pl.when(pl.program_id(0) == 0)
        def _():
            ds_ref[...] = jnp.zeros_like(ds_ref)
            db_ref[...] = jnp.zeros_like(db_ref)
            if post is not None:
                dg_ref[...] = jnp.zeros_like(dg_ref)

        def products(r):
            return [lax.dot_general(a_refs[n][pl.ds(r * sub, sub), :], w_refs[n][...], _DIMS["nt"],
                                    preferred_element_type=F32) for n in range(nb)]

        nxt = products(0)
        for r in range(tm // sub):
            rows = pl.ds(r * sub, sub)
            dhs = nxt
            if r + 1 < tm // sub:
                nxt = products(r + 1)
            xv = x_ref[rows, :]
            rr = lax.rsqrt(jnp.mean(xv * xv, axis=-1, keepdims=True) + EPS)
            xh = xv * rr
            dxh = jnp.zeros_like(xv)
            for n in range(nb):
                dh = dhs[n]
                dxh = dxh + dh * sc_ref[n:n + 1, :]
                ds_ref[n] += _colsum8(dh * xh)
                db_ref[n] += _colsum8(dh)
            dx = d_ref[rows, :] + rr * (dxh - xh * jnp.mean(dxh * xh, axis=-1, keepdims=True))
            dx_ref[rows, :] = dx
            if post is not None:
                dy, dxy = _post_norm_grad(dx, y_ref[rows, :], g_ref[...])
                dy_ref[rows, :] = dy.astype(BF16)
                dg_ref[...] += _colsum8(dxy)

    ins, in_specs = [], []
    for a, w in pairs:
        ins += [a, w]
        in_specs += [_row_spec(tm, a.shape[1]),
                     pl.BlockSpec((None, D, a.shape[1]), lambda i: (0, 0, 0), pipeline_mode=pl.Buffered(1))]
    ins += [x, dxn, scales]
    in_specs += [_row_spec(tm, D), _row_spec(tm, D), _vec_spec(nb, D)]
    acc_spec = pl.BlockSpec((nb, 8, D), lambda i: (0, 0, 0))
    out_specs = [_row_spec(tm, D), acc_spec, acc_spec]
    out_shape = [jax.ShapeDtypeStruct((S, D), F32), jax.ShapeDtypeStruct((nb, 8, D), F32),
                 jax.ShapeDtypeStruct((nb, 8, D), F32)]
    if post is not None:
        ins += list(post)
        in_specs += [_row_spec(tm, D), _vec_spec(1, D)]
        out_specs += [_row_spec(tm, D), _vec_spec(8, D)]
        out_shape += [jax.ShapeDtypeStruct((S, D), BF16), jax.ShapeDtypeStruct((8, D), F32)]
    return pl.pallas_call(
        body, name=name, grid=(S // tm,), in_specs=in_specs, out_specs=out_specs, out_shape=out_shape,
        compiler_params=_params("arbitrary"),
    )(*ins)


FFN_PAIRS = 2


def _ffn_in_act(h, w, layer, name):
    S, D = h.shape
    F2 = w.shape[2]
    PW = F2 // (2 * FFN_PAIRS)
    tm = _pick(S, 1024, 16)
    sub = _pick(tm, PIECE_ROWS, 16)

    def body(h_ref, w_ref, gu_ref, a_ref):
        def product(r):
            return jnp.dot(h_ref[pl.ds(r * sub, sub), :], w_ref[...], preferred_element_type=F32)

        nxt = product(0)
        for r in range(tm // sub):
            rows = pl.ds(r * sub, sub)
            acc = nxt
            if r + 1 < tm // sub:
                nxt = product(r + 1)
            gu_ref[rows, :] = acc.astype(BF16)
            g = acc[:, :PW]
            a_ref[rows, :] = (g * jax.nn.sigmoid(g) * acc[:, PW:]).astype(BF16)

    return pl.pallas_call(
        body, name=name, grid=(FFN_PAIRS, S // tm),
        in_specs=[pl.BlockSpec((tm, D), lambda p, i: (i, 0)),
                  pl.BlockSpec((None, D, 2 * PW), lambda p, i: (layer, 0, p))],
        out_specs=[pl.BlockSpec((tm, 2 * PW), lambda p, i: (i, p)), pl.BlockSpec((tm, PW), lambda p, i: (i, p))],
        out_shape=[jax.ShapeDtypeStruct((S, F2), BF16), jax.ShapeDtypeStruct((S, F2 // 2), BF16)],
        compiler_params=_params("parallel", "parallel"),
    )(h, w)


def _ffn_bwd(dy, w_out, gu, w_in, x, dxn, scale, post, name):
    S, D = dy.shape
    F2 = gu.shape[1]
    PW = F2 // (2 * FFN_PAIRS)
    tm = _pick(S, 256, 16)

    def body(dy_ref, wo_ref, gu_ref, wi_ref, x_ref, d_ref, sc_ref, y_ref, g_ref,
             dgu_ref, dx_ref, ds_ref, db_ref, dyn_ref, dg_ref):
        @pl.when(pl.program_id(0) == 0)
        def _():
            ds_ref[...] = jnp.zeros_like(ds_ref)
            db_ref[...] = jnp.zeros_like(db_ref)
            dg_ref[...] = jnp.zeros_like(dg_ref)

        def first_product(p):
            return lax.dot_general(dy_ref[...], wo_ref[p * PW:(p + 1) * PW, :], _DIMS["nt"],
                                   preferred_element_type=F32)

        dh = jnp.zeros((tm, D), F32)
        nxt = first_product(0)
        for p in range(FFN_PAIRS):
            cols = slice(2 * p * PW, 2 * (p + 1) * PW)
            da = nxt
            if p + 1 < FFN_PAIRS:
                nxt = first_product(p + 1)
            g = gu_ref[:, 2 * p * PW:(2 * p + 1) * PW].astype(F32)
            u = gu_ref[:, (2 * p + 1) * PW:2 * (p + 1) * PW].astype(F32)
            sg = jax.nn.sigmoid(g)
            dgu_ref[:, 2 * p * PW:(2 * p + 1) * PW] = (da * u * (sg * (1.0 + g * (1.0 - sg)))).astype(BF16)
            dgu_ref[:, (2 * p + 1) * PW:2 * (p + 1) * PW] = (da * (g * sg)).astype(BF16)
            dh = dh + lax.dot_general(dgu_ref[:, cols], wi_ref[:, cols], _DIMS["nt"], preferred_element_type=F32)
        xv = x_ref[...]
        rr = lax.rsqrt(jnp.mean(xv * xv, axis=-1, keepdims=True) + EPS)
        xh = xv * rr
        dxh = dh * sc_ref[...]
        ds_ref[0] += _colsum8(dh * xh)
        db_ref[0] += _colsum8(dh)
        dx = d_ref[...] + rr * (dxh - xh * jnp.mean(dxh * xh, axis=-1, keepdims=True))
        dx_ref[...] = dx
        dyn, dxy = _post_norm_grad(dx, y_ref[...], g_ref[...])
        dyn_ref[...] = dyn.astype(BF16)
        dg_ref[...] += _colsum8(dxy)

    resident = dict(pipeline_mode=pl.Buffered(1))
    acc_spec = pl.BlockSpec((1, 8, D), lambda i: (0, 0, 0))
    return pl.pallas_call(
        body, name=name, grid=(S // tm,),
        in_specs=[_row_spec(tm, D), pl.BlockSpec((None, F2 // 2, D), lambda i: (0, 0, 0), **resident),
                  _row_spec(tm, F2), pl.BlockSpec((None, D, F2), lambda i: (0, 0, 0), **resident),
                  _row_spec(tm, D), _row_spec(tm, D), _vec_spec(1, D), _row_spec(tm, D), _vec_spec(1, D)],
        out_specs=[_row_spec(tm, F2), _row_spec(tm, D), acc_spec, acc_spec, _row_spec(tm, D), _vec_spec(8, D)],
        out_shape=[jax.ShapeDtypeStruct((S, F2), BF16), jax.ShapeDtypeStruct((S, D), F32),
                   jax.ShapeDtypeStruct((1, 8, D), F32), jax.ShapeDtypeStruct((1, 8, D), F32),
                   jax.ShapeDtypeStruct((S, D), BF16), jax.ShapeDtypeStruct((8, D), F32)],
        compiler_params=_params("arbitrary"),
    )(dy, w_out, gu, w_in, x, dxn, scale, *post)


HALO = 16


def _conv_terms(bcx_ref, prev_ref, i, tm, D):
    b = bcx_ref[:, 0:D].astype(F32)
    cg = bcx_ref[:, D:2 * D].astype(F32)
    xin = bcx_ref[:, 2 * D:3 * D].astype(F32)
    z = cg * xin
    zp = prev_ref[:, D:2 * D].astype(F32) * prev_ref[:, 2 * D:3 * D].astype(F32)
    zp = jnp.where(i > 0, zp, 0.0)
    z_ext = jnp.concatenate([zp, z], axis=0)
    z1 = pltpu.roll(z_ext, 1, 0)[HALO:, :]
    z2 = pltpu.roll(z_ext, 2, 0)[HALO:, :]
    return b, cg, xin, z, z1, z2


def _conv_in_gate(h, w, ck, name):
    S, D = h.shape
    D3 = w.shape[-1]
    assert h.dtype == BF16 and w.dtype == BF16, name
    tm = _pick(S, 512, 16)

    def body(h_ref, w_ref, ck_ref, bcx_ref, o_ref, tail):
        @pl.when(pl.program_id(0) == 0)
        def _():
            tail[...] = jnp.zeros_like(tail)

        bcx_ref[...] = lax.dot_general(h_ref[...], w_ref[...], _DIMS["nn"],
                                       preferred_element_type=F32).astype(BF16)
        b = bcx_ref[:, 0:D].astype(F32)
        z = bcx_ref[:, D:2 * D].astype(F32) * bcx_ref[:, 2 * D:3 * D].astype(F32)
        z_ext = jnp.concatenate([tail[...], z], axis=0)
        z1 = pltpu.roll(z_ext, 1, 0)[HALO:, :]
        z2 = pltpu.roll(z_ext, 2, 0)[HALO:, :]
        conv = ck_ref[0:1, :] * z2 + ck_ref[1:2, :] * z1 + ck_ref[2:3, :] * z
        o_ref[...] = (b * conv).astype(BF16)
        tail[...] = z[tm - HALO:, :]

    return pl.pallas_call(
        body, name=name, grid=(S // tm,),
        in_specs=[_row_spec(tm, D),
                  pl.BlockSpec((None, D, D3), lambda i: (0, 0, 0), pipeline_mode=pl.Buffered(1)),
                  _vec_spec(8, D)],
        out_specs=[_row_spec(tm, D3), _row_spec(tm, D)],
        out_shape=[jax.ShapeDtypeStruct((S, D3), BF16), jax.ShapeDtypeStruct((S, D), BF16)],
        scratch_shapes=[pltpu.VMEM((HALO, D), F32)],
        compiler_params=_params("arbitrary"),
    )(h, w, ck)


def _conv_gate_bwd(du, bcx, ck, name):
    S, D3 = bcx.shape
    D = D3 // 3
    tm = _pick(S, 512, 16)
    hb = tm // HALO
    nt = S // tm

    def body(du_ref, dun_ref, bcx_ref, prev_ref, next_ref, ck_ref, o_ref, dk_ref):
        i = pl.program_id(0)
        b, cg, xin, z, z1, z2 = _conv_terms(bcx_ref, prev_ref, i, tm, D)
        k0, k1, k2 = ck_ref[0:1, :], ck_ref[1:2, :], ck_ref[2:3, :]
        conv = k0 * z2 + k1 * z1 + k2 * z
        d = du_ref[...].astype(F32)
        dconv = d * b
        dcn = jnp.where(i < nt - 1, dun_ref[...].astype(F32) * next_ref[:, 0:D].astype(F32), 0.0)
        d_ext = jnp.concatenate([dconv, dcn], axis=0)
        d1 = pltpu.roll(d_ext, tm + HALO - 1, 0)[:tm, :]
        d2 = pltpu.roll(d_ext, tm + HALO - 2, 0)[:tm, :]
        dz = k2 * dconv + k1 * d1 + k0 * d2
        o_ref[:, 0:D] = (d * conv).astype(BF16)
        o_ref[:, D:2 * D] = (dz * xin).astype(BF16)
        o_ref[:, 2 * D:3 * D] = (dz * cg).astype(BF16)

        @pl.when(i == 0)
        def _():
            dk_ref[...] = jnp.zeros_like(dk_ref)

        dk_ref[0] += _colsum8(dconv * z2)
        dk_ref[1] += _colsum8(dconv * z1)
        dk_ref[2] += _colsum8(dconv * z)

    last = S // HALO - 1
    return pl.pallas_call(
        body, name=name, grid=(nt,),
        in_specs=[_row_spec(tm, D),
                  pl.BlockSpec((HALO, D), lambda i: (jnp.minimum((i + 1) * hb, last), 0)),
                  _row_spec(tm, D3),
                  pl.BlockSpec((HALO, D3), lambda i: (jnp.maximum(i * hb - 1, 0), 0)),
                  pl.BlockSpec((HALO, D3), lambda i: (jnp.minimum((i + 1) * hb, last), 0)),
                  _vec_spec(8, D)],
        out_specs=[_row_spec(tm, D3), pl.BlockSpec((3, 8, D), lambda i: (0, 0, 0))],
        out_shape=[jax.ShapeDtypeStruct((S, D3), BF16), jax.ShapeDtypeStruct((3, 8, D), F32)],
        compiler_params=_params("arbitrary"),
    )(du, du, bcx, bcx, bcx, ck)


def _rel_onehot():
    a = np.arange(CHUNK)[:, None]
    b = np.arange(CHUNK)[None, :]
    idx = np.stack([np.clip((N_LEFT_CHUNKS - dl) * CHUNK + a - b, -MAX_REL, MAX_REL) + MAX_REL
                    for dl in (6, 7, 8)]).reshape(-1)
    return (jnp.asarray(idx)[:, None] == jnp.arange(N_REL)[None, :]).astype(F32)


def _bias_table(rel_bias, name):
    H = rel_bias.shape[0]
    near = jnp.dot(rel_bias, _rel_onehot().T, precision=lax.Precision.HIGHEST).reshape(H, 3, CHUNK, CHUNK)
    far = jnp.broadcast_to(rel_bias[:, N_REL - 1][:, None, None], (H, CHUNK, CHUNK))

    def body(near_ref, far_ref, o_ref):
        neg = jnp.full((CHUNK, CHUNK), NEG, F32)
        for v in range(N_WIN):
            for ic in range(Q_CHUNKS):
                for jc in range(N_WIN * Q_CHUNKS):
                    dl = jc - ic
                    if dl < 0 or dl > N_LEFT_CHUNKS or jc < (N_WIN - 1 - v) * Q_CHUNKS:
                        blk = neg
                    else:
                        blk = far_ref[...] if dl <= 5 else near_ref[dl - 6]
                    o_ref[v, ic * CHUNK:(ic + 1) * CHUNK, jc * CHUNK:(jc + 1) * CHUNK] = blk

    return pl.pallas_call(
        body, name=name, grid=(H,),
        in_specs=[pl.BlockSpec((None, 3, CHUNK, CHUNK), lambda h: (h, 0, 0, 0)),
                  pl.BlockSpec((None, CHUNK, CHUNK), lambda h: (h, 0, 0))],
        out_specs=pl.BlockSpec((N_WIN, None, BQ, N_WIN * BQ), lambda h: (0, h, 0, 0)),
        out_shape=jax.ShapeDtypeStruct((N_WIN, H, BQ, N_WIN * BQ), F32),
        compiler_params=_params("parallel"),
    )(near, far)


NEAR_FIRST = 6
SLAB_ROWS = 2 * CHUNK
SLAB_COLS = 4 * CHUNK


def _slab(pair):
    c0 = (NEAR_FIRST + 2 * pair) * CHUNK
    return slice(pair * SLAB_ROWS, (pair + 1) * SLAB_ROWS), slice(c0, c0 + SLAB_COLS)


def _bias_table_grad(dslab):
    H = dslab.shape[0]

    def blk(ic, dl):
        pair, r, col = ic // 2, ic % 2, ic + dl - NEAR_FIRST - 2 * (ic // 2)
        return dslab[:, pair, r * CHUNK:(r + 1) * CHUNK, col * CHUNK:(col + 1) * CHUNK]

    by_dl = [sum(blk(ic, dl) for ic in range(Q_CHUNKS)) for dl in (6, 7, 8)]
    near = jnp.stack(by_dl, axis=1).reshape(H, 3 * CHUNK * CHUNK)
    g = jnp.dot(near, _rel_onehot(), precision=lax.Precision.HIGHEST)
    return g.at[:, N_REL - 1].add(-jnp.sum(near, axis=1))


def _attn_specs(nblk, W):
    last = nblk - 1
    q_spec = pl.BlockSpec((BQ, W), lambda g, i: (jnp.minimum(i, last), g))
    kv_specs = [pl.BlockSpec((BQ, 2 * W), functools.partial(
        lambda g, i, w: (jnp.maximum(jnp.minimum(i, last) - (N_WIN - 1) + w, 0), g), w=w)) for w in range(N_WIN)]
    tab_spec = pl.BlockSpec((None, HEADS_PER_STEP, BQ, N_WIN * BQ),
                            lambda g, i: (jnp.minimum(i, N_WIN - 1), g, 0, 0))
    dtab_spec = pl.BlockSpec((HEADS_PER_STEP, Q_CHUNKS // 2, SLAB_ROWS, SLAB_COLS), lambda g, i: (g, 0, 0, 0))
    return q_spec, kv_specs, tab_spec, dtab_spec


def _attn_scores(q_ref, kT, tab_ref, h, dh):
    return jnp.dot(q_ref[:, h * dh:(h + 1) * dh], kT[h * dh:(h + 1) * dh, :], preferred_element_type=F32) + tab_ref[h]


def _attn_fwd(q, kv, tab, name):
    S, D = q.shape
    dh = D // N_HEADS
    W = HEADS_PER_STEP * dh
    assert 2 * W == D, "the kv layout puts one head group's k beside its v: two head groups"
    q_spec, kv_specs, tab_spec, _ = _attn_specs(S // BQ, W)

    def body(q_ref, *rest):
        tab_ref, o_ref = rest[N_WIN], rest[N_WIN + 1]
        kvw = jnp.concatenate([r[...] for r in rest[:N_WIN]], axis=0)
        kT = kvw[:, :W].T
        vw = kvw[:, W:]
        outs = []
        s = _attn_scores(q_ref, kT, tab_ref, 0, dh)
        for h in range(HEADS_PER_STEP):
            s_next = _attn_scores(q_ref, kT, tab_ref, h + 1, dh) if h + 1 < HEADS_PER_STEP else None
            e = jnp.exp(s - jnp.max(s, axis=-1, keepdims=True))
            l = jnp.sum(e, axis=-1, keepdims=True)
            outs.append(jnp.dot(e.astype(BF16), vw[:, h * dh:(h + 1) * dh], preferred_element_type=F32) / l)
            s = s_next
        o_ref[...] = jnp.concatenate(outs, axis=1).astype(BF16)

    return pl.pallas_call(
        body, name=name, grid=(N_HEADS // HEADS_PER_STEP, S // BQ),
        in_specs=[q_spec] + kv_specs + [tab_spec],
        out_specs=q_spec,
        out_shape=jax.ShapeDtypeStruct((S, D), BF16),
        compiler_params=_params("parallel", "parallel"),
    )(q, *([kv] * N_WIN), tab)


def _attn_bwd(q, kv, tab, do, name):
    S, D = q.shape
    dh = D // N_HEADS
    W = HEADS_PER_STEP * dh
    nblk = S // BQ
    q_spec, kv_specs, tab_spec, dtab_spec = _attn_specs(nblk, W)

    def body(q_ref, *rest):
        tab_ref, do_ref, dq_ref, dkv_ref, dtab_ref, ring = rest[N_WIN:]
        i = pl.program_id(1)

        @pl.when(i == 0)
        def _():
            dtab_ref[...] = jnp.zeros_like(dtab_ref)
            ring[...] = jnp.zeros_like(ring)

        @pl.when(i < nblk)
        def _():
            kvw = jnp.concatenate([r[...] for r in rest[:N_WIN]], axis=0)
            kT = kvw[:, :W].T
            vw = kvw[:, W:]
            qT = q_ref[...].T
            dqs, dks, dvs = [], [], []

            s = _attn_scores(q_ref, kT, tab_ref, 0, dh)
            for h in range(HEADS_PER_STEP):
                hd = slice(h * dh, (h + 1) * dh)
                do_h = do_ref[:, hd]
                dp = lax.dot_general(do_h, vw[:, hd], _DIMS["nt"], preferred_element_type=F32)
                e = jnp.exp(s - jnp.max(s, axis=-1, keepdims=True))
                inv_l = 1.0 / jnp.sum(e, axis=-1, keepdims=True)
                if h + 1 < HEADS_PER_STEP:
                    s = _attn_scores(q_ref, kT, tab_ref, h + 1, dh)
                delta = jnp.sum(e * dp, axis=-1, keepdims=True) * inv_l
                ds = e * ((dp - delta) * inv_l)
                for pair in range(Q_CHUNKS // 2):
                    rows, cols = _slab(pair)
                    dtab_ref[h, pair] += ds[rows, cols]
                dsb = ds.astype(BF16)
                dqs.append(lax.dot_general(kT[hd, :], dsb, _DIMS["nt"], preferred_element_type=F32) * (dh ** -0.5))
                dks.append(jnp.dot(qT[hd, :], dsb, preferred_element_type=F32))
                do_s = (do_h.astype(F32) * inv_l).astype(BF16)
                dvs.append(jnp.dot(do_s.T, e.astype(BF16), preferred_element_type=F32))
            dq_ref[...] = jnp.concatenate(dqs, axis=0).T.astype(BF16)
            dkv = jnp.concatenate(dks + dvs, axis=0).T
            for w in range(N_WIN):
                slot = lax.rem(i + 1 + w, N_WIN)
                part = dkv[w * BQ:(w + 1) * BQ, :]
                if w == N_WIN - 1:
                    ring[slot] = part
                else:
                    ring[slot] += part

        dkv_ref[...] = ring[lax.rem(i + 1, N_WIN)].astype(BF16)

    done_spec = pl.BlockSpec((BQ, 2 * W), lambda g, i: (jnp.maximum(i - (N_WIN - 1), 0), g))
    return pl.pallas_call(
        body, name=name, grid=(N_HEADS // HEADS_PER_STEP, nblk + N_WIN - 1),
        in_specs=[q_spec] + kv_specs + [tab_spec, q_spec],
        out_specs=[q_spec, done_spec, dtab_spec],
        out_shape=[jax.ShapeDtypeStruct((S, D), BF16), jax.ShapeDtypeStruct((S, 2 * D), BF16),
                   jax.ShapeDtypeStruct((N_HEADS, Q_CHUNKS // 2, SLAB_ROWS, SLAB_COLS), F32)],
        scratch_shapes=[pltpu.VMEM((N_WIN, BQ, 2 * W), F32)],
        compiler_params=_params("parallel", "arbitrary"),
    )(q, *([kv] * N_WIN), tab, do)


def _adamw(w, g, m, v, name, echo=False):
    shape = w.shape
    C = shape[-1]
    R = int(np.prod(shape[:-1])) if len(shape) > 1 else 1
    whole = len(shape) >= 2 and R * C <= SMALL_TENSOR_ELEMS
    if whole:
        w2, g2, m2, v2 = w, g, m, v
    else:
        w2, g2, m2, v2 = (t.reshape(R, C) for t in (w, g, m, v))
    tr = _pick(R, max(8, (512 * 1024) // C // 8 * 8), 8)

    def body(w_ref, g_ref, m_ref, v_ref, *out_refs):
        d_ref, nm_ref, nv_ref = out_refs[-3:]
        gv = g_ref[...]
        if echo:
            out_refs[0][...] = gv
        nm = ADAM_B1 * m_ref[...] + (1.0 - ADAM_B1) * gv
        nv = ADAM_B2 * v_ref[...] + (1.0 - ADAM_B2) * jnp.square(gv)
        m_hat = nm / (1.0 - ADAM_B1 ** ADAM_STEP)
        v_hat = nv / (1.0 - ADAM_B2 ** ADAM_STEP)
        d_ref[...] = -ADAM_LR * (m_hat / (jnp.sqrt(v_hat) + ADAM_EPS) + ADAM_WD * w_ref[...])
        nm_ref[...] = nm
        nv_ref[...] = nv

    if whole:
        spec, grid = pl.BlockSpec(shape, lambda i: (0,) * len(shape)), (1,)
    else:
        spec, grid = pl.BlockSpec((tr, C), lambda i: (i, 0)), (R // tr,)
    outs = pl.pallas_call(
        body, name=name, grid=grid,
        in_specs=[spec] * 4, out_specs=[spec] * (3 + echo),
        out_shape=[jax.ShapeDtypeStruct(w2.shape, F32)] * (3 + echo),
        compiler_params=_params("parallel"),
    )(w2, g2, m2, v2)
    return tuple(o.reshape(shape) for o in outs)


def _sum_rows(a, name):
    n, L = a.shape

    def body(a_ref, o_ref):
        acc = a_ref[0:1, :]
        for r in range(1, n):
            acc = acc + a_ref[r:r + 1, :]
        o_ref[...] = acc

    return pl.pallas_call(
        body, name=name, grid=(1,),
        in_specs=[pl.BlockSpec((n, L), lambda i: (0, 0))],
        out_specs=pl.BlockSpec((1, L), lambda i: (0, 0)),
        out_shape=jax.ShapeDtypeStruct((1, L), F32),
        compiler_params=_params("arbitrary"),
    )(a)


def _scalar_call(body, name, scalar, grid, in_specs, out_spec, out_shape, args):
    return pl.pallas_call(
        body, name=name,
        grid_spec=pltpu.PrefetchScalarGridSpec(num_scalar_prefetch=1, grid=grid, in_specs=in_specs,
                                               out_specs=out_spec),
        out_shape=out_shape, compiler_params=_params("parallel"),
    )(jnp.reshape(scalar, (-1,)).astype(jnp.int32), *args)


def _pair_sum(view, got, c, name):
    nb, _, rh, cols = view.shape
    tr = _pick(rh, max(16, (1 << 20) // cols // 16 * 16), 16)
    bpr = rh // tr

    def body(s_ref, a_ref, b_ref, o_ref):
        o_ref[...] = (a_ref[...].astype(F32) + b_ref[...].astype(F32)).astype(BF16)

    spec = pl.BlockSpec((tr, cols), lambda i, s: (i, 0))
    mine = pl.BlockSpec((tr, cols), lambda i, s: ((2 * (i // bpr) + s[0]) * bpr + i % bpr, 0))
    return _scalar_call(body, name, c, (nb * bpr,), [mine, spec], spec,
                        jax.ShapeDtypeStruct((nb * rh, cols), BF16),
                        (view.reshape(nb * 2 * rh, cols), got.reshape(nb * rh, cols)))


STACKED_LAYERS = 2


def _owner_sum(pair, recv, me, c, it, name, layer=None, into=None):
    _, rh, bc = recv.shape
    tr = _pick(rh, max(16, (1 << 19) // bc // 16 * 16), 16)
    bpr = rh // tr

    def body(s_ref, a_ref, r0, r1, r2, *rest):
        rest[-1][...] = ((a_ref[...].astype(F32) + r0[...].astype(F32)) + r1[...].astype(F32)) + r2[...].astype(F32)

    if it.kind == "col":
        own = pl.BlockSpec((tr, bc), lambda i, s: (i, s[0]))
    else:
        own = pl.BlockSpec((tr, bc), lambda i, s: (s[0] * bpr + i, 0))
    slots = [pl.BlockSpec((None, tr, bc), functools.partial(lambda i, s, k: (k, i, 0), k=k)) for k in range(3)]
    in_specs, args, aliases = [own] + slots, [pair, recv, recv, recv], {}
    if layer is None:
        out_spec = pl.BlockSpec((tr, bc), lambda i, s: (s[1] * bpr + i, 0))
        out_shape = jax.ShapeDtypeStruct((2 * rh, bc), F32)
    else:
        out_spec = pl.BlockSpec((None, tr, bc), lambda i, s: (layer, s[1] * bpr + i, 0))
        out_shape = jax.ShapeDtypeStruct((STACKED_LAYERS, 2 * rh, bc), F32)
        if into is not None:
            in_specs.append(pl.BlockSpec(memory_space=pl.ANY))
            args.append(into)
            aliases = {len(args): 0}
    return pl.pallas_call(
        body, name=name,
        grid_spec=pltpu.PrefetchScalarGridSpec(num_scalar_prefetch=1, grid=(bpr,), in_specs=in_specs,
                                               out_specs=out_spec),
        out_shape=out_shape, input_output_aliases=aliases, compiler_params=_params("parallel"),
    )(jnp.stack([it.pos(me), c]).astype(jnp.int32), *args)


def _place():
    x, y, c = lax.axis_index("x"), lax.axis_index("y"), lax.axis_index("c")
    chips = [(1 - x, y), (x, 1 - y), (1 - x, 1 - y)]
    return x, y, c, chips


def _chip_index(px, py):
    return 2 * px + py


def _all_gather_small(x_shard, name):
    m_per, n = x_shard.shape

    def body(x_ref, out_ref, send_sems, recv_sems, local_sem):
        x, y, c, chips = _place()
        me, sibling = (x, y, c), (x, y, 1 - c)

        def rows(px, py, pc):
            return out_ref.at[pl.ds((4 * px + 2 * py + pc) * m_per, m_per), :]

        def copy(k, block, to, src=None):
            return pltpu.make_async_remote_copy(
                src_ref=rows(*block) if src is None else src, dst_ref=rows(*block),
                send_sem=send_sems.at[k], recv_sem=recv_sems.at[k], device_id=to, device_id_type=MESH)

        mine = pltpu.make_async_copy(x_ref, rows(*me), local_sem)
        mine.start()
        first = [copy(0, me, sibling, src=x_ref)]
        first += [copy(1 + j, me, (*chip, c), src=x_ref) for j, chip in enumerate(chips)]
        for cp in first:
            cp.start()
        passed = [copy(4 + j, (*chip, c), sibling) for j, chip in enumerate(chips)]
        for j, chip in enumerate(chips):
            copy(1 + j, (*chip, c), me).wait_recv()
            passed[j].start()
        copy(0, sibling, me).wait_recv()
        for j, chip in enumerate(chips):
            copy(4 + j, (*chip, 1 - c), me).wait_recv()
        for cp in first + passed:
            cp.wait_send()
        mine.wait()

    return pl.pallas_call(
        body, name=name,
        out_shape=jax.ShapeDtypeStruct((N_DEV * m_per, n), x_shard.dtype),
        in_specs=[pl.BlockSpec(memory_space=pltpu.VMEM)],
        out_specs=pl.BlockSpec(memory_space=pltpu.VMEM),
        scratch_shapes=[pltpu.SemaphoreType.DMA((7,)), pltpu.SemaphoreType.DMA((7,)), pltpu.SemaphoreType.DMA],
    )(x_shard)


def _gather_flat(vec, name):
    L = vec.shape[0]
    Lp = -(-L // 1024) * 1024
    g = _all_gather_small(jnp.pad(vec, (0, Lp - L)).reshape(8, Lp // 8), name)
    return g.reshape(N_DEV, Lp)[:, :L]


class _Item:
    def __init__(self, kind, rows, cols, arg, layer, swap=False):
        self.kind, self.rows, self.cols, self.arg, self.layer, self.swap = kind, rows, cols, arg, layer, swap

    def ref(self, refs):
        return refs[self.arg].at[self.layer]

    def pos(self, j):
        return 2 * (j % 2) + j // 2 if self.swap else j


def _block(ref, it, j, half):
    if it.kind == "col":
        ns = it.cols // N_CHIP
        return ref.at[pl.ds(half * (it.rows // 2), it.rows // 2), pl.ds(it.pos(j) * ns, ns)]
    rs = it.rows // N_CHIP
    return ref.at[pl.ds(j * rs + half * (rs // 2), rs // 2), :]


def _cast_place(w, layer, kind, pos, after, name):
    _, r, n = w.shape
    tr = _pick(r, max(16, (1 << 20) // n // 16 * 16), 16)
    bpr = r // tr

    def body(s_ref, w_ref, after_ref, o_ref):
        o_ref[...] = w_ref[...].astype(BF16)

    if kind == "col":
        full, out_idx = (1, r, N_CHIP * n), (lambda i, s: (0, i, s[0]))
    else:
        full, out_idx = (1, N_CHIP * r, n), (lambda i, s: (0, s[0] * bpr + i, 0))
    return pl.pallas_call(
        body, name=name,
        grid_spec=pltpu.PrefetchScalarGridSpec(
            num_scalar_prefetch=1, grid=(bpr,),
            in_specs=[pl.BlockSpec((None, tr, n), lambda i, s: (layer, i, 0)), pl.BlockSpec(memory_space=pl.ANY)],
            out_specs=pl.BlockSpec((None, tr, n), out_idx)),
        out_shape=jax.ShapeDtypeStruct(full, BF16),
        compiler_params=_params("parallel"),
    )(jnp.reshape(pos, (1,)).astype(jnp.int32), w, after)


HBM_SPEC = pl.BlockSpec(memory_space=pltpu.HBM)
SEM_SPEC = pl.BlockSpec(memory_space=pltpu.SEMAPHORE)
ANY_SPEC = pl.BlockSpec(memory_space=pl.ANY)
SPLIT_PARAMS = dict(has_side_effects=pltpu.SideEffectType.DATAFLOW_SIDE_EFFECTING)


def _in_hbm(a):
    return pltpu.with_memory_space_constraint(a, pltpu.HBM)


def _split_start(copies_of, bufs, n_sem, after, name):
    n = len(bufs)

    def body(*refs):
        ins, send, recv, token = refs[:n], refs[n + 1], refs[n + 2], refs[2 * n + 3]
        for cp in copies_of(ins, send, recv, False)[0]:
            cp.start()
        token[...] = jnp.zeros_like(token)

    outs = pl.pallas_call(
        body, name=name,
        out_shape=(pltpu.SemaphoreType.DMA(n_sem), pltpu.SemaphoreType.DMA(n_sem),
                   *[pltpu.HBM(b.shape, b.dtype) for b in bufs], jax.ShapeDtypeStruct((8, 128), F32)),
        in_specs=[HBM_SPEC] * n + [ANY_SPEC],
        out_specs=(SEM_SPEC, SEM_SPEC, *[HBM_SPEC] * n, pl.BlockSpec(memory_space=pltpu.VMEM)),
        input_output_aliases={t: 2 + t for t in range(n)},
        compiler_params=pltpu.CompilerParams(**SPLIT_PARAMS),
    )(*[_in_hbm(b) for b in bufs], after)
    return outs[0], outs[1], list(outs[2:2 + n]), outs[2 + n]


def _split_wait(copies_of, send, recv, bufs, after, name):
    n = len(bufs)
    after = list(after) if isinstance(after, (list, tuple)) else [after]

    def body(*refs):
        ins, send_ref, recv_ref = refs[:n], refs[n], refs[n + 1]
        sends, arrivals = copies_of(ins, send_ref, recv_ref, True)
        for cp in sends:
            cp.wait_send()
        for cp in arrivals:
            cp.wait_recv()

    return pl.pallas_call(
        body, name=name,
        out_shape=[pltpu.HBM(b.shape, b.dtype) for b in bufs],
        in_specs=[HBM_SPEC] * n + [SEM_SPEC, SEM_SPEC] + [ANY_SPEC] * len(after),
        out_specs=[HBM_SPEC] * n,
        input_output_aliases={t: t for t in range(n)},
        compiler_params=pltpu.CompilerParams(**SPLIT_PARAMS),
    )(*bufs, send, recv, *after)


def _gather_copies(items):
    def copies_of(refs, send, recv, with_arrivals):
        x, y, c, chips = _place()
        me = _chip_index(x, y)
        sends, arrivals = [], []
        for t, it in enumerate(items):
            for k, chip in enumerate(chips):
                for core in range(2):
                    mine = _block(it.ref(refs), it, me, c)
                    sends.append(pltpu.make_async_remote_copy(
                        src_ref=mine, dst_ref=mine, send_sem=send.at[6 * t + 2 * k + core],
                        recv_sem=recv.at[6 * t + 2 * k + c], device_id=(*chip, core), device_id_type=MESH))
                    if with_arrivals:
                        landed = _block(it.ref(refs), it, _chip_index(*chip), core)
                        arrivals.append(pltpu.make_async_remote_copy(
                            src_ref=landed, dst_ref=landed, send_sem=send.at[6 * t + 2 * k + core],
                            recv_sem=recv.at[6 * t + 2 * k + core], device_id=(*chip, core), device_id_type=MESH))
        return sends, arrivals

    return copies_of


def _owner_copies(items):
    n = len(items)

    def blk(ref, it, j):
        if it.kind == "col":
            ns = it.cols // N_CHIP
            return ref.at[:, pl.ds(it.pos(j) * ns, ns)]
        return ref.at[j]

    def copies_of(refs, send, recv, with_arrivals):
        x, y, c, chips = _place()
        sends, arrivals = [], []
        for t, it in enumerate(items):
            for k, chip in enumerate(chips):
                slot = refs[n + t].at[k]
                sends.append(pltpu.make_async_remote_copy(
                    src_ref=blk(refs[t], it, _chip_index(*chip)), dst_ref=slot, send_sem=send.at[3 * t + k],
                    recv_sem=recv.at[3 * t + k], device_id=(*chip, c), device_id_type=MESH))
                if with_arrivals:
                    arrivals.append(pltpu.make_async_remote_copy(
                        src_ref=slot, dst_ref=slot, send_sem=send.at[3 * t + k], recv_sem=recv.at[3 * t + k],
                        device_id=(*chip, c), device_id_type=MESH))
        return sends, arrivals

    return copies_of


def _owner_slot_shape(it):
    if it.kind == "col":
        return (3, it.rows // 2, it.cols // N_CHIP)
    return (3, it.rows // (2 * N_CHIP), it.cols)


def _pair_view(g, it):
    if it.kind == "col":
        return g.reshape(1, 2, it.rows // 2, it.cols)
    return g.reshape(N_CHIP, 2, it.rows // (2 * N_CHIP), it.cols)


def _pair_copies(n):
    def copies_of(refs, send, recv, with_arrivals):
        x, y, c, _ = _place()
        sends, arrivals = [], []
        for t in range(n):
            land = refs[n + t]
            sends.append(pltpu.make_async_remote_copy(
                src_ref=refs[t].at[:, pl.ds(1 - c, 1)], dst_ref=land, send_sem=send.at[t], recv_sem=recv.at[t],
                device_id=(x, y, 1 - c), device_id_type=MESH))
            if with_arrivals:
                arrivals.append(pltpu.make_async_remote_copy(
                    src_ref=land, dst_ref=land, send_sem=send.at[t], recv_sem=recv.at[t],
                    device_id=(x, y, 1 - c), device_id_type=MESH))
        return sends, arrivals

    return copies_of


def _half_copies(n):
    def half(ref, which):
        r2 = ref.shape[-2] // 2
        rows = pl.ds(which * r2, r2)
        return ref.at[rows, :] if len(ref.shape) == 2 else ref.at[:, rows, :]

    def copies_of(refs, send, recv, with_arrivals):
        x, y, c, _ = _place()
        sends, arrivals = [], []
        for t in range(n):
            mine = half(refs[t], c)
            sends.append(pltpu.make_async_remote_copy(
                src_ref=mine, dst_ref=mine, send_sem=send.at[t], recv_sem=recv.at[t],
                device_id=(x, y, 1 - c), device_id_type=MESH))
            if with_arrivals:
                theirs = half(refs[t], 1 - c)
                arrivals.append(pltpu.make_async_remote_copy(
                    src_ref=theirs, dst_ref=theirs, send_sem=send.at[t], recv_sem=recv.at[t],
                    device_id=(x, y, 1 - c), device_id_type=MESH))
        return sends, arrivals

    return copies_of


class _Reduction:
    pass


def _pair_start(grads, items, after, tag, names, layer=None):
    n = len(items)
    views = [_pair_view(g, it) for g, it in zip(grads, items)]
    lands = [lax.empty((v.shape[0], 1) + v.shape[2:], v.dtype) for v in views]
    r = _Reduction()
    r.items, r.tag, r.names, r.layer = items, tag, names, layer
    r.send, r.recv, r.bufs, r.token = _split_start(_pair_copies(n), views + lands, (n,), after, f"rs_pair_start_{tag}")
    return r


def _owner_start(r, after):
    x, y, c, _ = _place()
    n = len(r.items)
    bufs = _split_wait(_pair_copies(n), r.send, r.recv, r.bufs, after, f"rs_pair_wait_{r.tag}")
    pairs = [_pair_sum(bufs[t], bufs[n + t], c, f"rs_pair_sum_{r.tag}_{t}") for t in range(n)]
    shaped = [p if it.kind == "col" else p.reshape(N_CHIP, p.shape[0] // N_CHIP, p.shape[1])
              for p, it in zip(pairs, r.items)]
    lands = [lax.empty(_owner_slot_shape(it), BF16) for it in r.items]
    r.send, r.recv, r.bufs, r.token = _split_start(
        _owner_copies(r.items), shaped + lands, (3 * n,), r.token, f"rs_owner_start_{r.tag}")
    return r


def _reduce_finish(groups, after):
    x, y, c, _ = _place()
    me = _chip_index(x, y)
    halves = {}
    behind = [after]
    for r in groups:
        n = len(r.items)
        bufs = _split_wait(_owner_copies(r.items), r.send, r.recv, r.bufs, behind, f"rs_owner_wait_{r.tag}")
        for t, (it, nm) in enumerate(zip(r.items, r.names)):
            pair = bufs[t].reshape(-1, bufs[t].shape[-1])
            halves[nm] = _owner_sum(pair, bufs[n + t], me, c, it, f"rs_owner_sum_{r.tag}_{t}",
                                    layer=r.layer, into=halves.get(nm))
        behind = [after] + [halves[nm] for nm in r.names]
    n = len(halves)
    return list(halves), _split_start(_half_copies(n), list(halves.values()), (n,), after, "rs_half_start")


def _silu(v):
    return v * jax.nn.sigmoid(v)


def _sum8(p):
    return jnp.sum(p, axis=-2)


def kernel(x, c, mod_w, mod_b, norm_g, ffn_w_in, ffn_w_out, conv_w_in, conv_k, conv_w_out, kv_mod_w, kv_mod_b, kv_norm_g, w_kv, attn_w_q, attn_w_o, rel_bias, loss_target, m_mod_w, m_mod_b, m_norm_g, m_ffn_w_in, m_ffn_w_out, m_conv_w_in, m_conv_k, m_conv_w_out, m_kv_mod_w, m_kv_mod_b, m_kv_norm_g, m_w_kv, m_attn_w_q, m_attn_w_o, m_rel_bias, v_mod_w, v_mod_b, v_norm_g, v_ffn_w_in, v_ffn_w_out, v_conv_w_in, v_conv_k, v_conv_w_out, v_kv_mod_w, v_kv_mod_b, v_kv_norm_g, v_w_kv, v_attn_w_q, v_attn_w_o, v_rel_bias):
    xi, yi, ci = lax.axis_index("x"), lax.axis_index("y"), lax.axis_index("c")
    chip = 2 * xi + yi
    dev = 2 * chip + ci
    _, S, D = x.shape
    F = ffn_w_out.shape[1] * N_CHIP
    x0 = x.reshape(S, D)
    target = loss_target.reshape(S, D)
    n_mod = mod_w.shape[2]
    n_kvm = kv_mod_w.shape[1]
    dsh = D // N_CHIP
    TF = F // 2

    c_all = _all_gather_small(c.reshape(8, D // 8), "ag_c").reshape(N_DEV, D)
    sc16 = jnp.pad(_silu(c_all), ((0, 8), (0, 0)))
    part = [_mm(sc16, mod_w, "nn", F32, f"mod_fwd_{l}", b_layer=l)[:8] for l in range(2)]
    part.append(_mm(sc16, kv_mod_w, "nn", F32, "mod_fwd_kv")[:8])
    fwd_vec = jnp.concatenate([p.reshape(-1) for p in part] + [norm_g.reshape(-1), conv_k.reshape(-1)])
    fwd_all = _gather_flat(fwd_vec, "ag_fwd_small")[0::2]
    o = 0
    mods = []
    for n in (n_mod, n_mod, n_kvm):
        blk = fwd_all[:, o:o + 8 * n].reshape(N_CHIP, 8, n)
        mods.append(lax.dynamic_index_in_dim(blk, dev, axis=1, keepdims=False).reshape(N_CHIP * n))
        o += 8 * n
    ng = fwd_all[:, o:o + 8 * dsh].reshape(N_CHIP, 2, 4, dsh).transpose(1, 2, 0, 3).reshape(2, 4, D)
    o += 8 * dsh
    ck = fwd_all[:, o:o + 3 * dsh].reshape(N_CHIP, 3, dsh).transpose(1, 0, 2).reshape(3, D)
    ck8 = jnp.pad(ck, ((0, 5), (0, 0)))
    mod = [mods[l] + mod_b[l] for l in range(2)]
    sh1, sc1, g1, sh2, sc2, g2 = zip(*[jnp.split(m, 6) for m in mod])
    kv_sh, kv_sc = jnp.split(mods[2] + kv_mod_b, 2)
    row = lambda v: v.reshape(1, D)

    it_conv = [_Item("col", D, 3 * D, 0, 0), _Item("row", D, D, 1, 0)]
    it_ffn = [_Item("col", D, 2 * F, 0, 0, swap=True), _Item("row", F, D, 1, 0)]
    it_attn = [_Item("col", D, 2 * D, 0, 0, swap=True), _Item("row", D, D, 1, 0), _Item("row", D, D, 2, 0)]

    def placed(w, layer, it, nm, after=fwd_all):
        return _cast_place(w, layer, it.kind, it.pos(chip), after, f"place_{nm}")

    flying = {}

    def start(tag, its, bufs, after):
        send, recv, bufs, tok = _split_start(_gather_copies(its), bufs, (6 * len(its),), after, f"ag_start_{tag}")
        flying[tag] = (its, send, recv, bufs)
        return tok

    def arrived(tag, after):
        its, send, recv, bufs = flying[tag]
        return _split_wait(_gather_copies(its), send, recv, bufs, after, f"ag_wait_{tag}")

    one = lambda it: [_Item(it.kind, it.rows, it.cols, 0, 0, it.swap)]
    tok = start("conv_in", one(it_conv[0]), [placed(conv_w_in, 0, it_conv[0], "conv_w_in")], fwd_all)
    tok = start("conv_out", one(it_conv[1]), [placed(conv_w_out, 0, it_conv[1], "conv_w_out", tok)], tok)
    tok = start("ffn0_in", one(it_ffn[0]), [placed(ffn_w_in, 0, it_ffn[0], "ffn_w_in0", tok)], tok)
    tok = start("ffn0_out", one(it_ffn[1]), [placed(ffn_w_out, 0, it_ffn[1], "ffn_w_out0", tok)], tok)
    tok = start("attn", it_attn, [placed(w_kv[None], 0, it_attn[0], "w_kv", tok),
                                  placed(attn_w_q, 0, it_attn[1], "attn_w_q", tok),
                                  placed(attn_w_o, 0, it_attn[2], "attn_w_o", tok)], tok)
    token = start("ffn1", it_ffn, [placed(ffn_w_in, 1, it_ffn[0], "ffn_w_in1", tok),
                                   placed(ffn_w_out, 1, it_ffn[1], "ffn_w_out1", tok)], tok)

    a1 = row(ng[0, 0] * (1.0 + sc1[0])) + token[0, 0]
    (h1,) = _norm_mod(x0, a1, row(sh1[0]), "l0_norm1")
    tab = _bias_table(rel_bias[0], "l1_bias_table")
    h1, tab = lax.optimization_barrier((h1, tab))
    (W_cin,) = arrived("conv_in", h1)
    bcx, ug = _conv_in_gate(h1, W_cin, ck8, "l0_conv_in_gate")
    gt1 = row(g1[0] * ng[0, 1])
    a2 = row(ng[0, 2] * (1.0 + sc2[0]))
    (W_cout,) = arrived("conv_out", ug)
    y1, x1, h2 = _mm_post(ug, W_cout, x0, gt1, "l0_conv_out", scales=a2, shifts=row(sh2[0]))
    (W_fin0,) = arrived("ffn0_in", h2)
    gu0, act0 = _ffn_in_act(h2, W_fin0, 0, "l0_ffn_in")
    (W_fout0,) = arrived("ffn0_out", act0)
    gt2 = row(g2[0] * ng[0, 3])
    a3 = ng[1, 0] * (1.0 + sc1[1])
    akv = kv_norm_g * (1.0 + kv_sc)
    y2, x2, h3, hkv = _mm_post(act0, W_fout0, x1, gt2, "l0_ffn_out",
                               scales=jnp.stack([a3, akv]), shifts=jnp.stack([sh1[1], kv_sh]))
    W_kv, W_q, W_o = arrived("attn", hkv)
    kvp = _mm(hkv, W_kv, "nn", BF16, "l1_kv", b_layer=0, tm=512, tn=2 * D)
    att_scale = (D // N_HEADS) ** -0.5
    assert math.log2(att_scale) % 1 == 0, "scaling q before its bf16 cast is exact only for a power of two"
    qp = _mm(h3, W_q, "nn", BF16, "l1_q", b_layer=0, scale=att_scale)
    oh = _attn_fwd(qp, kvp, tab, "l1_attn")
    gt3 = row(g1[1] * ng[1, 1])
    a4 = row(ng[1, 2] * (1.0 + sc2[1]))
    y3, x3, h4 = _mm_post(oh, W_o, x2, gt3, "l1_attn_out", scales=a4, shifts=row(sh2[1]))
    W_fin1, W_fout1 = arrived("ffn1", h4)
    gu1, act1 = _ffn_in_act(h4, W_fin1, 0, "l1_ffn_in")
    gt4 = row(g2[1] * ng[1, 3])
    dx4, sq, dy4, dgt4 = _mm_post(act1, W_fout1, x3, gt4, "l1_ffn_out", target=target)
    loss_part = 0.5 * jnp.sum(sq) / D

    def ffn_bwd(dy, dxn, xin_, h, gu, act, a, w_in, w_out, post, tag):
        dgu, dx, ds, db, dyn, dgt = _ffn_bwd(dy, w_out, gu, w_in, xin_, dxn, a, post, f"{tag}_ffn_bwd")
        g_fout = _mm(act, dy, "tn", BF16, f"{tag}_ffn_out_dw", tm=TF)
        g_fin = _mm(h, dgu, "tn", BF16, f"{tag}_ffn_in_dw", tn=TF)
        return dx, ds, db, dyn, dgt, g_fin, g_fout

    dx3, ds4, db4, dy3, dgt3, G_fin1, G_fout1 = ffn_bwd(dy4, dx4, x3, h4, gu1, act1, a4, W_fin1, W_fout1,
                                                        (y3, gt3), "l1")
    red = [_pair_start([G_fin1, G_fout1], it_ffn, token, "ffn1", ["ffn_w_in", "ffn_w_out"], layer=1)]
    doh = _mm(dy3, W_o, "nt", BF16, "l1_attn_out_dx", b_layer=0, after=red[0].token)
    G_o = _mm(oh, dy3, "tn", BF16, "l1_attn_out_dw")
    _owner_start(red[0], G_o)
    dq, dkv, dtab = _attn_bwd(qp, kvp, tab, doh, "l1_attn_bwd")
    d_rel = _bias_table_grad(dtab)
    G_q = _mm(h3, dq, "tn", BF16, "l1_q_dw")
    G_kv = _mm(hkv, dkv, "tn", BF16, "l1_kv_dw")
    red.append(_pair_start([G_kv, G_q, G_o], it_attn, red[-1].token, "attn", ["w_kv", "attn_w_q", "attn_w_o"]))
    dx2, ds3, db3, dy2, dgt2 = _mm_pre_bwd([(dq, W_q), (dkv, W_kv)], x2, dx3,
                                           jnp.stack([a3, akv]) + red[1].token[0, 0], "l1_qkv_dx", post=(y2, gt2))
    _owner_start(red[1], dx2)

    dx1, ds2, db2, dy1, dgt1, G_fin0, G_fout0 = ffn_bwd(dy2, dx2, x1, h2, gu0, act0, a2, W_fin0, W_fout0,
                                                        (y1, gt1), "l0")
    red.append(_pair_start([G_fin0, G_fout0], it_ffn, red[-1].token, "ffn0", ["ffn_w_in", "ffn_w_out"], layer=0))
    dug = _mm(dy1, W_cout, "nt", BF16, "l0_conv_out_dx", b_layer=0, after=red[2].token)
    G_cout = _mm(ug, dy1, "tn", BF16, "l0_conv_out_dw")
    dbcx, dck = _conv_gate_bwd(dug, bcx, ck8, "l0_conv_gate_bwd")
    _owner_start(red[2], dbcx)
    G_cin = _mm(h1, dbcx, "tn", BF16, "l0_conv_in_dw")
    red.append(_pair_start([G_cin, G_cout], it_conv, red[-1].token, "conv", ["conv_w_in", "conv_w_out"]))
    dx0, ds1, db1 = _mm_pre_bwd([(dbcx, W_cin)], x0, dx1, a1 + red[3].token[0, 0], "l0_conv_in_dx")
    ds1, db1 = _sum8(ds1)[0], _sum8(db1)[0]
    da2, db2 = _sum8(ds2)[0], _sum8(db2)[0]
    ds3, db3 = _sum8(ds3), _sum8(db3)
    da4, db4 = _sum8(ds4)[0], _sum8(db4)[0]
    dgt1, dgt2, dgt3, dgt4 = _sum8(dgt1), _sum8(dgt2), _sum8(dgt3), _sum8(dgt4)

    def dmod_of(l, ds_a, db_a, dgt_a, ds_b, db_b, dgt_b):
        return jnp.concatenate([db_a, ds_a * ng[l, 0], dgt_a * ng[l, 1], db_b, ds_b * ng[l, 2], dgt_b * ng[l, 3]])

    dmod0 = dmod_of(0, ds1, db1, dgt1, da2, db2, dgt2)
    dmod1 = dmod_of(1, ds3[0], db3[0], dgt3, da4, db4, dgt4)
    dkvmod = jnp.concatenate([db3[1], ds3[1] * kv_norm_g])
    dng = jnp.stack([
        jnp.stack([ds1 * (1.0 + sc1[0]), dgt1 * g1[0], da2 * (1.0 + sc2[0]), dgt2 * g2[0]]),
        jnp.stack([ds3[0] * (1.0 + sc1[1]), dgt3 * g1[1], da4 * (1.0 + sc2[1]), dgt4 * g2[1]])])
    dkvng = ds3[1] * (1.0 + kv_sc)
    small = [dmod0, dmod1, dkvmod, dng.reshape(-1), dkvng, _sum8(dck).reshape(-1), d_rel.reshape(-1),
             loss_part.reshape(1)]
    sizes = [int(s.shape[0]) for s in small]
    offs = np.concatenate([[0], np.cumsum(sizes)])
    bwd_all = _gather_flat(jnp.concatenate(small), "ag_bwd_small")
    _owner_start(red[3], bwd_all)
    Lb = bwd_all.shape[1]
    Lp = -(-Lb // 128) * 128
    tot = _sum_rows(jnp.pad(bwd_all, ((0, 0), (0, Lp - Lb))), "sum_small")[0]
    seg = lambda i: tot[offs[i]:offs[i + 1]]
    g_mod_b = jnp.stack([seg(0), seg(1)])
    g_kv_mod_b = seg(2)
    g_norm_g = lax.dynamic_slice_in_dim(seg(3).reshape(2, 4, D), chip * dsh, dsh, axis=2)
    g_kv_norm_g = seg(4)
    g_conv_k = lax.dynamic_slice_in_dim(seg(5).reshape(1, 3, D), chip * dsh, dsh, axis=2)
    g_rel_bias = seg(6).reshape(rel_bias.shape)
    loss = seg(7)[0]

    def dmod_rows(i, n):
        rows_ = lax.dynamic_slice_in_dim(bwd_all[:, offs[i]:offs[i + 1]], chip * n, n, axis=1)
        return jnp.pad(rows_, ((0, 8), (0, 0)))

    g_mod_w = _mm(sc16, jnp.concatenate([dmod_rows(0, n_mod), dmod_rows(1, n_mod)], axis=1), "tn", F32,
                  "mod_bwd", out_layers=STACKED_LAYERS)
    g_kv_mod_w = _mm(sc16, dmod_rows(2, n_kvm), "tn", F32, "mod_bwd_kv")

    grads = {
        "mod_w": g_mod_w, "mod_b": g_mod_b, "norm_g": g_norm_g, "conv_k": g_conv_k,
        "kv_mod_w": g_kv_mod_w, "kv_mod_b": g_kv_mod_b, "kv_norm_g": g_kv_norm_g, "rel_bias": g_rel_bias,
    }
    weights = dict(mod_w=mod_w, mod_b=mod_b, norm_g=norm_g, ffn_w_in=ffn_w_in, ffn_w_out=ffn_w_out,
                   conv_w_in=conv_w_in, conv_k=conv_k, conv_w_out=conv_w_out, kv_mod_w=kv_mod_w,
                   kv_mod_b=kv_mod_b, kv_norm_g=kv_norm_g, w_kv=w_kv, attn_w_q=attn_w_q, attn_w_o=attn_w_o,
                   rel_bias=rel_bias)
    m_in = dict(mod_w=m_mod_w, mod_b=m_mod_b, norm_g=m_norm_g, ffn_w_in=m_ffn_w_in, ffn_w_out=m_ffn_w_out,
                conv_w_in=m_conv_w_in, conv_k=m_conv_k, conv_w_out=m_conv_w_out, kv_mod_w=m_kv_mod_w,
                kv_mod_b=m_kv_mod_b, kv_norm_g=m_kv_norm_g, w_kv=m_w_kv, attn_w_q=m_attn_w_q,
                attn_w_o=m_attn_w_o, rel_bias=m_rel_bias)
    v_in = dict(mod_w=v_mod_w, mod_b=v_mod_b, norm_g=v_norm_g, ffn_w_in=v_ffn_w_in, ffn_w_out=v_ffn_w_out,
                conv_w_in=v_conv_w_in, conv_k=v_conv_k, conv_w_out=v_conv_w_out, kv_mod_w=v_kv_mod_w,
                kv_mod_b=v_kv_mod_b, kv_norm_g=v_kv_norm_g, w_kv=v_w_kv, attn_w_q=v_attn_w_q,
                attn_w_o=v_attn_w_o, rel_bias=v_rel_bias)
    names = list(weights)
    step = {}

    def update(n, echo=False):
        g = grads[n].reshape(weights[n].shape)
        outs = _adamw(weights[n], g, m_in[n], v_in[n], f"adamw_{n}", echo=echo)
        step[n] = outs if echo else (g, *outs)

    update("mod_w")
    reduced, (half_send, half_recv, half_bufs, _) = _reduce_finish(red, step["mod_w"][1])
    local = [n for n in grads if n != "mod_w"]
    for n in local:
        update(n)
    grads.update(zip(reduced, _split_wait(
        _half_copies(len(half_bufs)), half_send, half_recv, half_bufs, [step[n][1] for n in local], "rs_half_wait")))
    for n in reduced:
        update(n, echo=True)
    return (loss, dx0.reshape(x.shape), *[step[n][k] for k in range(4) for n in names])
```

```python
import functools
import math

import numpy as np
import jax
import jax.numpy as jnp
from jax import lax
from jax.experimental import pallas as pl
from jax.experimental.pallas import tpu as pltpu

CHUNK = 64
N_LEFT_CHUNKS = 8
N_HEADS = 16
MAX_REL = 2 * CHUNK
N_REL = 2 * MAX_REL + 1
EPS = 1e-6
ADAM_LR = 0.001
ADAM_B1 = 0.9
ADAM_B2 = 0.999
ADAM_EPS = 1e-08
ADAM_WD = 0.01
ADAM_STEP = 10

Q_CHUNKS = 4
BQ = Q_CHUNKS * CHUNK
N_WIN = 1 + N_LEFT_CHUNKS // Q_CHUNKS
HEADS_PER_STEP = 8
NEG = -1e30
N_DEV = 8
N_CHIP = 4
SMALL_TENSOR_ELEMS = 1 << 16
GU_RING = 3
PIECE_ROWS = 256

BF16 = jnp.bfloat16
F32 = jnp.float32
V7X_VMEM_LIMIT_BYTES = 56 * 1024 * 1024
MESH = pl.DeviceIdType.MESH


def _pick(n, pref, align):
    t = min(pref, n)
    t -= t % align
    while t >= align:
        if n % t == 0:
            return t
        t -= align
    return n


def _params(*sem):
    return pltpu.CompilerParams(dimension_semantics=sem, vmem_limit_bytes=V7X_VMEM_LIMIT_BYTES)


def _colsum8(v):
    r, d = v.shape
    return v.reshape(r // 8, 8, d).sum(axis=0)


_DIMS = {"nn": (((1,), (0,)), ((), ())), "nt": (((1,), (1,)), ((), ())), "tn": (((0,), (0,)), ((), ()))}


def _mm(a, b, mode, out_dtype, name, *, b_layer=None, tm=1024, tn=1024, tk=None, scale=None, after=None,
        out_layers=1):
    if tk is None:
        tk = 2048 if mode == "tn" else 3072
    bs = b.shape[1:] if b_layer is not None else b.shape
    if mode == "nn":
        (M, K), (K2, N) = a.shape, bs
    elif mode == "nt":
        (M, K), (N, K2) = a.shape, bs
    else:
        (K, M), (K2, N) = a.shape, bs
    assert K == K2, (name, a.shape, b.shape)
    tm = _pick(M, tm, 128 if mode == "tn" else 16)
    tn = _pick(N // out_layers, tn, 128)
    tk = _pick(K, tk, 128 if mode != "tn" else 16)
    nk = K // tk
    assert scale is None or nk == 1, name
    dims = _DIMS[mode]
    extra = [] if after is None else [after]

    def body(a_ref, b_ref, *rest):
        o_ref, acc = rest[len(extra)], rest[len(extra) + 1:]
        p = lax.dot_general(a_ref[...].astype(BF16), b_ref[...].astype(BF16), dims,
                            preferred_element_type=F32)
        if nk == 1:
            o_ref[...] = (p if scale is None else p * scale).astype(o_ref.dtype)
        else:
            k = pl.program_id(2)

            @pl.when(k == 0)
            def _():
                acc[0][...] = p

            @pl.when(k > 0)
            def _():
                acc[0][...] += p

            @pl.when(k == nk - 1)
            def _():
                o_ref[...] = acc[0][...].astype(o_ref.dtype)

    a_spec = (pl.BlockSpec((tk, tm), lambda i, j, k: (k, i)) if mode == "tn"
              else pl.BlockSpec((tm, tk), lambda i, j, k: (i, k)))
    if mode == "nt":
        b_blk, b_idx = (tn, tk), (lambda i, j, k: (j, k))
    else:
        b_blk, b_idx = (tk, tn), (lambda i, j, k: (k, j))
    if b_layer is not None:
        b_spec = pl.BlockSpec((None,) + b_blk, lambda i, j, k: (b_layer,) + b_idx(i, j, k))
    else:
        b_spec = pl.BlockSpec(b_blk, b_idx)
    if out_layers > 1:
        per_layer = N // out_layers // tn
        o_spec = pl.BlockSpec((None, tm, tn), lambda i, j, k: (j // per_layer, i, j % per_layer))
        o_shape = (out_layers, M, N // out_layers)
    else:
        o_spec, o_shape = pl.BlockSpec((tm, tn), lambda i, j, k: (i, j)), (M, N)
    return pl.pallas_call(
        body, name=name,
        grid=(M // tm, N // tn, nk),
        in_specs=[a_spec, b_spec] + [pl.BlockSpec(memory_space=pl.ANY)] * len(extra),
        out_specs=o_spec,
        out_shape=jax.ShapeDtypeStruct(o_shape, out_dtype),
        scratch_shapes=[pltpu.VMEM((tm, tn), F32)] if nk > 1 else [],
        compiler_params=_params("parallel", "parallel", "arbitrary"),
    )(a, b, *extra)


def _row_spec(tm, d):
    return pl.BlockSpec((tm, d), lambda i: (i, 0))


def _vec_spec(r, d):
    return pl.BlockSpec((r, d), lambda i: (0, 0))


def _norm_mod(x, scales, shifts, name):
    S, D = x.shape
    nb = scales.shape[0]
    tm = _pick(S, 1024, 16)

    def body(x_ref, a_ref, b_ref, *o_refs):
        xv = x_ref[...]
        xh = xv * lax.rsqrt(jnp.mean(xv * xv, axis=-1, keepdims=True) + EPS)
        for n in range(nb):
            o_refs[n][...] = (xh * a_ref[n:n + 1, :] + b_ref[n:n + 1, :]).astype(BF16)

    return pl.pallas_call(
        body, name=name, grid=(S // tm,),
        in_specs=[_row_spec(tm, D), _vec_spec(nb, D), _vec_spec(nb, D)],
        out_specs=[_row_spec(tm, D)] * nb,
        out_shape=[jax.ShapeDtypeStruct((S, D), BF16)] * nb,
        compiler_params=_params("parallel"),
    )(x, scales, shifts)


def _mm_post(a, w, x, gate, name, *, scales=None, shifts=None, target=None):
    M, K = a.shape
    D = w.shape[2]
    tm = _pick(M, 1024 if K <= D else 512, 16)
    sub = _pick(tm, PIECE_ROWS, 16)
    nb = 0 if scales is None else scales.shape[0]

    def body(a_ref, w_ref, x_ref, g_ref, *rest):
        if target is None:
            sc_ref, sh_ref, y_ref, xn_ref = rest[:4]
            h_refs = rest[4:]
        else:
            t_ref, dx_ref, sq_ref, dy_ref, dg_ref = rest

            @pl.when(pl.program_id(0) == 0)
            def _():
                sq_ref[...] = jnp.zeros_like(sq_ref)
                dg_ref[...] = jnp.zeros_like(dg_ref)

        def product(r):
            return jnp.dot(a_ref[pl.ds(r * sub, sub), :], w_ref[...], preferred_element_type=F32)

        y = product(0)
        for r in range(tm // sub):
            rows = pl.ds(r * sub, sub)
            yb = y.astype(BF16)
            if r + 1 < tm // sub:
                y = product(r + 1)
            yv = yb.astype(F32)
            yh = yv * lax.rsqrt(jnp.mean(yv * yv, axis=-1, keepdims=True) + EPS)
            xn = x_ref[rows, :] + yh * g_ref[...]
            if target is None:
                y_ref[rows, :] = yb
                xn_ref[rows, :] = xn
                xh = xn * lax.rsqrt(jnp.mean(xn * xn, axis=-1, keepdims=True) + EPS)
                for n in range(nb):
                    h_refs[n][rows, :] = (xh * sc_ref[n:n + 1, :] + sh_ref[n:n + 1, :]).astype(BF16)
            else:
                e = xn - t_ref[rows, :]
                dx = e / D
                dx_ref[rows, :] = dx
                sq_ref[...] += _colsum8(e * e)
                dy, dxy = _post_norm_grad(dx, yb, g_ref[...])
                dy_ref[rows, :] = dy.astype(BF16)
                dg_ref[...] += _colsum8(dxy)

    ins = [a, w, x, gate]
    in_specs = [_row_spec(tm, K), pl.BlockSpec((None, K, D), lambda i: (0, 0, 0)), _row_spec(tm, D), _vec_spec(1, D)]
    if target is None:
        ins += [scales, shifts]
        in_specs += [_vec_spec(nb, D), _vec_spec(nb, D)]
        out_specs = [_row_spec(tm, D)] * (2 + nb)
        out_shape = [jax.ShapeDtypeStruct((M, D), BF16), jax.ShapeDtypeStruct((M, D), F32)] \
            + [jax.ShapeDtypeStruct((M, D), BF16)] * nb
    else:
        ins += [target]
        in_specs += [_row_spec(tm, D)]
        out_specs = [_row_spec(tm, D), _vec_spec(8, D), _row_spec(tm, D), _vec_spec(8, D)]
        out_shape = [jax.ShapeDtypeStruct((M, D), F32), jax.ShapeDtypeStruct((8, D), F32),
                     jax.ShapeDtypeStruct((M, D), BF16), jax.ShapeDtypeStruct((8, D), F32)]
    return pl.pallas_call(
        body, name=name, grid=(M // tm,), in_specs=in_specs, out_specs=out_specs, out_shape=out_shape,
        compiler_params=_params("arbitrary" if target is not None else "parallel"),
    )(*ins)


def _post_norm_grad(dxn, yb, gate):
    yv = yb.astype(F32)
    r = lax.rsqrt(jnp.mean(yv * yv, axis=-1, keepdims=True) + EPS)
    yh = yv * r
    dyh = dxn * gate
    return r * (dyh - yh * jnp.mean(dyh * yh, axis=-1, keepdims=True)), dxn * yh


def _mm_pre_bwd(pairs, x, dxn, scales, name, post=None):
    S, D = x.shape
    nb = len(pairs)
    tm = _pick(S, 512, 16)
    sub = _pick(tm, PIECE_ROWS, 16)

    def body(*refs):
        a_refs, w_refs = refs[0:2 * nb:2], refs[1:2 * nb:2]
        x_ref, d_ref, sc_ref = refs[2 * nb:2 * nb + 3]
        rest = refs[2 * nb + 3:]
        if post is not None:
            y_ref, g_ref, dx_ref, ds_ref, db_ref, dy_ref, dg_ref = rest
        else:
            dx_ref, ds_ref, db_ref = rest

        @pl.when(pl.program_id(0) == 0)
        def _():
            ds_ref[...] = jnp.zeros_like(ds_ref)
            db_ref[...] = jnp.zeros_like(db_ref)
            if post is not None:
                dg_ref[...] = jnp.zeros_like(dg_ref)

        def products(r):
            return [lax.dot_general(a_refs[n][pl.ds(r * sub, sub), :], w_refs[n][...], _DIMS["nt"],
                                    preferred_element_type=F32) for n in range(nb)]

        nxt = products(0)
        for r in range(tm // sub):
            rows = pl.ds(r * sub, sub)
            dhs = nxt
            if r + 1 < tm // sub:
                nxt = products(r + 1)
            xv = x_ref[rows, :]
            rr = lax.rsqrt(jnp.mean(xv * xv, axis=-1, keepdims=True) + EPS)
            xh = xv * rr
            dxh = jnp.zeros_like(xv)
            for n in range(nb):
                dh = dhs[n]
                dxh = dxh + dh * sc_ref[n:n + 1, :]
                ds_ref[n] += _colsum8(dh * xh)
                db_ref[n] += _colsum8(dh)
            dx = d_ref[rows, :] + rr * (dxh - xh * jnp.mean(dxh * xh, axis=-1, keepdims=True))
            dx_ref[rows, :] = dx
            if post is not None:
                dy, dxy = _post_norm_grad(dx, y_ref[rows, :], g_ref[...])
                dy_ref[rows, :] = dy.astype(BF16)
                dg_ref[...] += _colsum8(dxy)

    ins, in_specs = [], []
    for a, w in pairs:
        ins += [a, w]
        in_specs += [_row_spec(tm, a.shape[1]),
                     pl.BlockSpec((None, D, a.shape[1]), lambda i: (0, 0, 0), pipeline_mode=pl.Buffered(1))]
    ins += [x, dxn, scales]
    in_specs += [_row_spec(tm, D), _row_spec(tm, D), _vec_spec(nb, D)]
    acc_spec = pl.BlockSpec((nb, 8, D), lambda i: (0, 0, 0))
    out_specs = [_row_spec(tm, D), acc_spec, acc_spec]
    out_shape = [jax.ShapeDtypeStruct((S, D), F32), jax.ShapeDtypeStruct((nb, 8, D), F32),
                 jax.ShapeDtypeStruct((nb, 8, D), F32)]
    if post is not None:
        ins += list(post)
        in_specs += [_row_spec(tm, D), _vec_spec(1, D)]
        out_specs += [_row_spec(tm, D), _vec_spec(8, D)]
        out_shape += [jax.ShapeDtypeStruct((S, D), BF16), jax.ShapeDtypeStruct((8, D), F32)]
    return pl.pallas_call(
        body, name=name, grid=(S // tm,), in_specs=in_specs, out_specs=out_specs, out_shape=out_shape,
        compiler_params=_params("arbitrary"),
    )(*ins)


FFN_PAIRS = 2


def _ffn_in_act(h, w, layer, name):
    S, D = h.shape
    F2 = w.shape[2]
    PW = F2 // (2 * FFN_PAIRS)
    tm = _pick(S, 1024, 16)
    sub = _pick(tm, PIECE_ROWS, 16)

    def body(h_ref, w_ref, gu_ref, a_ref):
        def product(r):
            return jnp.dot(h_ref[pl.ds(r * sub, sub), :], w_ref[...], preferred_element_type=F32)

        nxt = product(0)
        for r in range(tm // sub):
            rows = pl.ds(r * sub, sub)
            acc = nxt
            if r + 1 < tm // sub:
                nxt = product(r + 1)
            gu_ref[rows, :] = acc.astype(BF16)
            g = acc[:, :PW]
            a_ref[rows, :] = (g * jax.nn.sigmoid(g) * acc[:, PW:]).astype(BF16)

    return pl.pallas_call(
        body, name=name, grid=(FFN_PAIRS, S // tm),
        in_specs=[pl.BlockSpec((tm, D), lambda p, i: (i, 0)),
                  pl.BlockSpec((None, D, 2 * PW), lambda p, i: (layer, 0, p))],
        out_specs=[pl.BlockSpec((tm, 2 * PW), lambda p, i: (i, p)), pl.BlockSpec((tm, PW), lambda p, i: (i, p))],
        out_shape=[jax.ShapeDtypeStruct((S, F2), BF16), jax.ShapeDtypeStruct((S, F2 // 2), BF16)],
        compiler_params=_params("parallel", "parallel"),
    )(h, w)


def _ffn_bwd(dy, w_out, gu, w_in, x, dxn, scale, post, name):
    S, D = dy.shape
    F2 = gu.shape[1]
    PW = F2 // (2 * FFN_PAIRS)
    tm = _pick(S, 256, 16)

    n_tiles = S // tm
    assert n_tiles >= GU_RING - 1, name

    def body(dy_ref, wo_ref, gu_hbm, wi_ref, x_ref, d_ref, sc_ref, y_ref, g_ref,
             dgu_ref, dx_ref, ds_ref, db_ref, dyn_ref, dg_ref, ring, ring_sem):
        i = pl.program_id(0)

        def gu_copy(tile):
            slot = tile % GU_RING
            return pltpu.make_async_copy(gu_hbm.at[pl.ds(tile * tm, tm), :], ring.at[slot], ring_sem.at[slot])

        @pl.when(i == 0)
        def _():
            ds_ref[...] = jnp.zeros_like(ds_ref)
            db_ref[...] = jnp.zeros_like(db_ref)
            dg_ref[...] = jnp.zeros_like(dg_ref)
            for t in range(GU_RING - 1):
                gu_copy(t).start()

        @pl.when(i + GU_RING - 1 < n_tiles)
        def _():
            gu_copy(i + GU_RING - 1).start()

        gu_copy(i).wait()
        gu_ref = ring.at[i % GU_RING]

        def first_product(p):
            return lax.dot_general(dy_ref[...], wo_ref[p * PW:(p + 1) * PW, :], _DIMS["nt"],
                                   preferred_element_type=F32)

        dh = jnp.zeros((tm, D), F32)
        nxt = first_product(0)
        for p in range(FFN_PAIRS):
            cols = slice(2 * p * PW, 2 * (p + 1) * PW)
            da = nxt
            if p + 1 < FFN_PAIRS:
                nxt = first_product(p + 1)
            g = gu_ref[:, 2 * p * PW:(2 * p + 1) * PW].astype(F32)
            u = gu_ref[:, (2 * p + 1) * PW:2 * (p + 1) * PW].astype(F32)
            sg = jax.nn.sigmoid(g)
            dgu_ref[:, 2 * p * PW:(2 * p + 1) * PW] = (da * u * (sg * (1.0 + g * (1.0 - sg)))).astype(BF16)
            dgu_ref[:, (2 * p + 1) * PW:2 * (p + 1) * PW] = (da * (g * sg)).astype(BF16)
            dh = dh + lax.dot_general(dgu_ref[:, cols], wi_ref[:, cols], _DIMS["nt"], preferred_element_type=F32)
        xv = x_ref[...]
        rr = lax.rsqrt(jnp.mean(xv * xv, axis=-1, keepdims=True) + EPS)
        xh = xv * rr
        dxh = dh * sc_ref[...]
        ds_ref[0] += _colsum8(dh * xh)
        db_ref[0] += _colsum8(dh)
        dx = d_ref[...] + rr * (dxh - xh * jnp.mean(dxh * xh, axis=-1, keepdims=True))
        dx_ref[...] = dx
        dyn, dxy = _post_norm_grad(dx, y_ref[...], g_ref[...])
        dyn_ref[...] = dyn.astype(BF16)
        dg_ref[...] += _colsum8(dxy)

    resident = dict(pipeline_mode=pl.Buffered(1))
    acc_spec = pl.BlockSpec((1, 8, D), lambda i: (0, 0, 0))
    return pl.pallas_call(
        body, name=name, grid=(S // tm,),
        in_specs=[_row_spec(tm, D), pl.BlockSpec((None, F2 // 2, D), lambda i: (0, 0, 0), **resident),
                  pl.BlockSpec(memory_space=pl.ANY), pl.BlockSpec((None, D, F2), lambda i: (0, 0, 0), **resident),
                  _row_spec(tm, D), _row_spec(tm, D), _vec_spec(1, D), _row_spec(tm, D), _vec_spec(1, D)],
        out_specs=[_row_spec(tm, F2), _row_spec(tm, D), acc_spec, acc_spec, _row_spec(tm, D), _vec_spec(8, D)],
        out_shape=[jax.ShapeDtypeStruct((S, F2), BF16), jax.ShapeDtypeStruct((S, D), F32),
                   jax.ShapeDtypeStruct((1, 8, D), F32), jax.ShapeDtypeStruct((1, 8, D), F32),
                   jax.ShapeDtypeStruct((S, D), BF16), jax.ShapeDtypeStruct((8, D), F32)],
        scratch_shapes=[pltpu.VMEM((GU_RING, tm, F2), BF16), pltpu.SemaphoreType.DMA((GU_RING,))],
        compiler_params=_params("arbitrary"),
    )(dy, w_out, gu, w_in, x, dxn, scale, *post)


HALO = 16


def _conv_terms(bcx_ref, prev_ref, i, tm, D):
    b = bcx_ref[:, 0:D].astype(F32)
    cg = bcx_ref[:, D:2 * D].astype(F32)
    xin = bcx_ref[:, 2 * D:3 * D].astype(F32)
    z = cg * xin
    zp = prev_ref[:, D:2 * D].astype(F32) * prev_ref[:, 2 * D:3 * D].astype(F32)
    zp = jnp.where(i > 0, zp, 0.0)
    z_ext = jnp.concatenate([zp, z], axis=0)
    z1 = pltpu.roll(z_ext, 1, 0)[HALO:, :]
    z2 = pltpu.roll(z_ext, 2, 0)[HALO:, :]
    return b, cg, xin, z, z1, z2


def _conv_in_gate(h, w, ck, name):
    S, D = h.shape
    D3 = w.shape[-1]
    assert h.dtype == BF16 and w.dtype == BF16, name
    tm = _pick(S, 512, 16)

    def body(h_ref, w_ref, ck_ref, bcx_ref, o_ref, tail):
        @pl.when(pl.program_id(0) == 0)
        def _():
            tail[...] = jnp.zeros_like(tail)

        bcx_ref[...] = lax.dot_general(h_ref[...], w_ref[...], _DIMS["nn"],
                                       preferred_element_type=F32).astype(BF16)
        b = bcx_ref[:, 0:D].astype(F32)
        z = bcx_ref[:, D:2 * D].astype(F32) * bcx_ref[:, 2 * D:3 * D].astype(F32)
        z_ext = jnp.concatenate([tail[...], z], axis=0)
        z1 = pltpu.roll(z_ext, 1, 0)[HALO:, :]
        z2 = pltpu.roll(z_ext, 2, 0)[HALO:, :]
        conv = ck_ref[0:1, :] * z2 + ck_ref[1:2, :] * z1 + ck_ref[2:3, :] * z
        o_ref[...] = (b * conv).astype(BF16)
        tail[...] = z[tm - HALO:, :]

    return pl.pallas_call(
        body, name=name, grid=(S // tm,),
        in_specs=[_row_spec(tm, D),
                  pl.BlockSpec((None, D, D3), lambda i: (0, 0, 0), pipeline_mode=pl.Buffered(1)),
                  _vec_spec(8, D)],
        out_specs=[_row_spec(tm, D3), _row_spec(tm, D)],
        out_shape=[jax.ShapeDtypeStruct((S, D3), BF16), jax.ShapeDtypeStruct((S, D), BF16)],
        scratch_shapes=[pltpu.VMEM((HALO, D), F32)],
        compiler_params=_params("arbitrary"),
    )(h, w, ck)


def _conv_gate_bwd(du, bcx, ck, name):
    S, D3 = bcx.shape
    D = D3 // 3
    tm = _pick(S, 512, 16)
    hb = tm // HALO
    nt = S // tm

    def body(du_ref, dun_ref, bcx_ref, prev_ref, next_ref, ck_ref, o_ref, dk_ref):
        i = pl.program_id(0)
        b, cg, xin, z, z1, z2 = _conv_terms(bcx_ref, prev_ref, i, tm, D)
        k0, k1, k2 = ck_ref[0:1, :], ck_ref[1:2, :], ck_ref[2:3, :]
        conv = k0 * z2 + k1 * z1 + k2 * z
        d = du_ref[...].astype(F32)
        dconv = d * b
        dcn = jnp.where(i < nt - 1, dun_ref[...].astype(F32) * next_ref[:, 0:D].astype(F32), 0.0)
        d_ext = jnp.concatenate([dconv, dcn], axis=0)
        d1 = pltpu.roll(d_ext, tm + HALO - 1, 0)[:tm, :]
        d2 = pltpu.roll(d_ext, tm + HALO - 2, 0)[:tm, :]
        dz = k2 * dconv + k1 * d1 + k0 * d2
        o_ref[:, 0:D] = (d * conv).astype(BF16)
        o_ref[:, D:2 * D] = (dz * xin).astype(BF16)
        o_ref[:, 2 * D:3 * D] = (dz * cg).astype(BF16)

        @pl.when(i == 0)
        def _():
            dk_ref[...] = jnp.zeros_like(dk_ref)

        dk_ref[0] += _colsum8(dconv * z2)
        dk_ref[1] += _colsum8(dconv * z1)
        dk_ref[2] += _colsum8(dconv * z)

    last = S // HALO - 1
    return pl.pallas_call(
        body, name=name, grid=(nt,),
        in_specs=[_row_spec(tm, D),
                  pl.BlockSpec((HALO, D), lambda i: (jnp.minimum((i + 1) * hb, last), 0)),
                  _row_spec(tm, D3),
                  pl.BlockSpec((HALO, D3), lambda i: (jnp.maximum(i * hb - 1, 0), 0)),
                  pl.BlockSpec((HALO, D3), lambda i: (jnp.minimum((i + 1) * hb, last), 0)),
                  _vec_spec(8, D)],
        out_specs=[_row_spec(tm, D3), pl.BlockSpec((3, 8, D), lambda i: (0, 0, 0))],
        out_shape=[jax.ShapeDtypeStruct((S, D3), BF16), jax.ShapeDtypeStruct((3, 8, D), F32)],
        compiler_params=_params("arbitrary"),
    )(du, du, bcx, bcx, bcx, ck)


def _rel_onehot():
    a = np.arange(CHUNK)[:, None]
    b = np.arange(CHUNK)[None, :]
    idx = np.stack([np.clip((N_LEFT_CHUNKS - dl) * CHUNK + a - b, -MAX_REL, MAX_REL) + MAX_REL
                    for dl in (6, 7, 8)]).reshape(-1)
    return (jnp.asarray(idx)[:, None] == jnp.arange(N_REL)[None, :]).astype(F32)


def _bias_table(rel_bias, name):
    H = rel_bias.shape[0]
    near = jnp.dot(rel_bias, _rel_onehot().T, precision=lax.Precision.HIGHEST).reshape(H, 3, CHUNK, CHUNK)
    far = jnp.broadcast_to(rel_bias[:, N_REL - 1][:, None, None], (H, CHUNK, CHUNK))

    def body(near_ref, far_ref, o_ref):
        neg = jnp.full((CHUNK, CHUNK), NEG, F32)
        for v in range(N_WIN):
            for ic in range(Q_CHUNKS):
                for jc in range(N_WIN * Q_CHUNKS):
                    dl = jc - ic
                    if dl < 0 or dl > N_LEFT_CHUNKS or jc < (N_WIN - 1 - v) * Q_CHUNKS:
                        blk = neg
                    else:
                        blk = far_ref[...] if dl <= 5 else near_ref[dl - 6]
                    o_ref[v, ic * CHUNK:(ic + 1) * CHUNK, jc * CHUNK:(jc + 1) * CHUNK] = blk

    return pl.pallas_call(
        body, name=name, grid=(H,),
        in_specs=[pl.BlockSpec((None, 3, CHUNK, CHUNK), lambda h: (h, 0, 0, 0)),
                  pl.BlockSpec((None, CHUNK, CHUNK), lambda h: (h, 0, 0))],
        out_specs=pl.BlockSpec((N_WIN, None, BQ, N_WIN * BQ), lambda h: (0, h, 0, 0)),
        out_shape=jax.ShapeDtypeStruct((N_WIN, H, BQ, N_WIN * BQ), F32),
        compiler_params=_params("parallel"),
    )(near, far)


NEAR_FIRST = 6
SLAB_ROWS = 2 * CHUNK
SLAB_COLS = 4 * CHUNK


def _slab(pair):
    c0 = (NEAR_FIRST + 2 * pair) * CHUNK
    return slice(pair * SLAB_ROWS, (pair + 1) * SLAB_ROWS), slice(c0, c0 + SLAB_COLS)


def _bias_table_grad(dslab):
    H = dslab.shape[0]

    def blk(ic, dl):
        pair, r, col = ic // 2, ic % 2, ic + dl - NEAR_FIRST - 2 * (ic // 2)
        return dslab[:, pair, r * CHUNK:(r + 1) * CHUNK, col * CHUNK:(col + 1) * CHUNK]

    by_dl = [sum(blk(ic, dl) for ic in range(Q_CHUNKS)) for dl in (6, 7, 8)]
    near = jnp.stack(by_dl, axis=1).reshape(H, 3 * CHUNK * CHUNK)
    g = jnp.dot(near, _rel_onehot(), precision=lax.Precision.HIGHEST)
    return g.at[:, N_REL - 1].add(-jnp.sum(near, axis=1))


def _attn_specs(nblk, W):
    last = nblk - 1
    q_spec = pl.BlockSpec((BQ, W), lambda g, i: (jnp.minimum(i, last), g))
    kv_specs = [pl.BlockSpec((BQ, 2 * W), functools.partial(
        lambda g, i, w: (jnp.maximum(jnp.minimum(i, last) - (N_WIN - 1) + w, 0), g), w=w)) for w in range(N_WIN)]
    tab_spec = pl.BlockSpec((None, HEADS_PER_STEP, BQ, N_WIN * BQ),
                            lambda g, i: (jnp.minimum(i, N_WIN - 1), g, 0, 0))
    dtab_spec = pl.BlockSpec((HEADS_PER_STEP, Q_CHUNKS // 2, SLAB_ROWS, SLAB_COLS), lambda g, i: (g, 0, 0, 0))
    return q_spec, kv_specs, tab_spec, dtab_spec


def _attn_scores(q_ref, kT, tab_ref, h, dh):
    return jnp.dot(q_ref[:, h * dh:(h + 1) * dh], kT[h * dh:(h + 1) * dh, :], preferred_element_type=F32) + tab_ref[h]


def _attn_fwd(q, kv, tab, name):
    S, D = q.shape
    dh = D // N_HEADS
    W = HEADS_PER_STEP * dh
    assert 2 * W == D, "the kv layout puts one head group's k beside its v: two head groups"
    q_spec, kv_specs, tab_spec, _ = _attn_specs(S // BQ, W)

    def body(q_ref, *rest):
        tab_ref, o_ref = rest[N_WIN], rest[N_WIN + 1]
        kvw = jnp.concatenate([r[...] for r in rest[:N_WIN]], axis=0)
        kT = kvw[:, :W].T
        vw = kvw[:, W:]
        outs = []
        s = _attn_scores(q_ref, kT, tab_ref, 0, dh)
        for h in range(HEADS_PER_STEP):
            s_next = _attn_scores(q_ref, kT, tab_ref, h + 1, dh) if h + 1 < HEADS_PER_STEP else None
            e = jnp.exp(s - jnp.max(s, axis=-1, keepdims=True))
            l = jnp.sum(e, axis=-1, keepdims=True)
            outs.append(jnp.dot(e.astype(BF16), vw[:, h * dh:(h + 1) * dh], preferred_element_type=F32) / l)
            s = s_next
        o_ref[...] = jnp.concatenate(outs, axis=1).astype(BF16)

    return pl.pallas_call(
        body, name=name, grid=(N_HEADS // HEADS_PER_STEP, S // BQ),
        in_specs=[q_spec] + kv_specs + [tab_spec],
        out_specs=q_spec,
        out_shape=jax.ShapeDtypeStruct((S, D), BF16),
        compiler_params=_params("parallel", "parallel"),
    )(q, *([kv] * N_WIN), tab)


def _attn_bwd(q, kv, tab, do, name):
    S, D = q.shape
    dh = D // N_HEADS
    W = HEADS_PER_STEP * dh
    nblk = S // BQ
    q_spec, kv_specs, tab_spec, dtab_spec = _attn_specs(nblk, W)

    def body(q_ref, *rest):
        tab_ref, do_ref, dq_ref, dkv_ref, dtab_ref, ring = rest[N_WIN:]
        i = pl.program_id(1)

        @pl.when(i == 0)
        def _():
            dtab_ref[...] = jnp.zeros_like(dtab_ref)
            ring[...] = jnp.zeros_like(ring)

        @pl.when(i < nblk)
        def _():
            kvw = jnp.concatenate([r[...] for r in rest[:N_WIN]], axis=0)
            kT = kvw[:, :W].T
            vw = kvw[:, W:]
            qT = q_ref[...].T
            dqs, dks, dvs = [], [], []

            s = _attn_scores(q_ref, kT, tab_ref, 0, dh)
            for h in range(HEADS_PER_STEP):
                hd = slice(h * dh, (h + 1) * dh)
                do_h = do_ref[:, hd]
                dp = lax.dot_general(do_h, vw[:, hd], _DIMS["nt"], preferred_element_type=F32)
                e = jnp.exp(s - jnp.max(s, axis=-1, keepdims=True))
                inv_l = 1.0 / jnp.sum(e, axis=-1, keepdims=True)
                if h + 1 < HEADS_PER_STEP:
                    s = _attn_scores(q_ref, kT, tab_ref, h + 1, dh)
                delta = jnp.sum(e * dp, axis=-1, keepdims=True) * inv_l
                ds = e * ((dp - delta) * inv_l)
                for pair in range(Q_CHUNKS // 2):
                    rows, cols = _slab(pair)
                    dtab_ref[h, pair] += ds[rows, cols]
                dsb = ds.astype(BF16)
                dqs.append(lax.dot_general(kT[hd, :], dsb, _DIMS["nt"], preferred_element_type=F32) * (dh ** -0.5))
                dks.append(jnp.dot(qT[hd, :], dsb, preferred_element_type=F32))
                do_s = (do_h.astype(F32) * inv_l).astype(BF16)
                dvs.append(jnp.dot(do_s.T, e.astype(BF16), preferred_element_type=F32))
            dq_ref[...] = jnp.concatenate(dqs, axis=0).T.astype(BF16)
            dkv = jnp.concatenate(dks + dvs, axis=0).T
            for w in range(N_WIN):
                slot = lax.rem(i + 1 + w, N_WIN)
                part = dkv[w * BQ:(w + 1) * BQ, :]
                if w == N_WIN - 1:
                    ring[slot] = part
                else:
                    ring[slot] += part

        dkv_ref[...] = ring[lax.rem(i + 1, N_WIN)].astype(BF16)

    done_spec = pl.BlockSpec((BQ, 2 * W), lambda g, i: (jnp.maximum(i - (N_WIN - 1), 0), g))
    return pl.pallas_call(
        body, name=name, grid=(N_HEADS // HEADS_PER_STEP, nblk + N_WIN - 1),
        in_specs=[q_spec] + kv_specs + [tab_spec, q_spec],
        out_specs=[q_spec, done_spec, dtab_spec],
        out_shape=[jax.ShapeDtypeStruct((S, D), BF16), jax.ShapeDtypeStruct((S, 2 * D), BF16),
                   jax.ShapeDtypeStruct((N_HEADS, Q_CHUNKS // 2, SLAB_ROWS, SLAB_COLS), F32)],
        scratch_shapes=[pltpu.VMEM((N_WIN, BQ, 2 * W), F32)],
        compiler_params=_params("parallel", "arbitrary"),
    )(q, *([kv] * N_WIN), tab, do)


def _adamw(w, g, m, v, name, echo=False):
    shape = w.shape
    C = shape[-1]
    R = int(np.prod(shape[:-1])) if len(shape) > 1 else 1
    whole = len(shape) >= 2 and R * C <= SMALL_TENSOR_ELEMS
    if whole:
        w2, g2, m2, v2 = w, g, m, v
    else:
        w2, g2, m2, v2 = (t.reshape(R, C) for t in (w, g, m, v))
    tr = _pick(R, max(8, (512 * 1024) // C // 8 * 8), 8)

    def body(w_ref, g_ref, m_ref, v_ref, *out_refs):
        d_ref, nm_ref, nv_ref = out_refs[-3:]
        gv = g_ref[...]
        if echo:
            out_refs[0][...] = gv
        nm = ADAM_B1 * m_ref[...] + (1.0 - ADAM_B1) * gv
        nv = ADAM_B2 * v_ref[...] + (1.0 - ADAM_B2) * jnp.square(gv)
        m_hat = nm / (1.0 - ADAM_B1 ** ADAM_STEP)
        v_hat = nv / (1.0 - ADAM_B2 ** ADAM_STEP)
        d_ref[...] = -ADAM_LR * (m_hat / (jnp.sqrt(v_hat) + ADAM_EPS) + ADAM_WD * w_ref[...])
        nm_ref[...] = nm
        nv_ref[...] = nv

    if whole:
        spec, grid = pl.BlockSpec(shape, lambda i: (0,) * len(shape)), (1,)
    else:
        spec, grid = pl.BlockSpec((tr, C), lambda i: (i, 0)), (R // tr,)
    outs = pl.pallas_call(
        body, name=name, grid=grid,
        in_specs=[spec] * 4, out_specs=[spec] * (3 + echo),
        out_shape=[jax.ShapeDtypeStruct(w2.shape, F32)] * (3 + echo),
        compiler_params=_params("parallel"),
    )(w2, g2, m2, v2)
    return tuple(o.reshape(shape) for o in outs)


def _sum_rows(a, name):
    n, L = a.shape

    def body(a_ref, o_ref):
        acc = a_ref[0:1, :]
        for r in range(1, n):
            acc = acc + a_ref[r:r + 1, :]
        o_ref[...] = acc

    return pl.pallas_call(
        body, name=name, grid=(1,),
        in_specs=[pl.BlockSpec((n, L), lambda i: (0, 0))],
        out_specs=pl.BlockSpec((1, L), lambda i: (0, 0)),
        out_shape=jax.ShapeDtypeStruct((1, L), F32),
        compiler_params=_params("arbitrary"),
    )(a)


def _scalar_call(body, name, scalar, grid, in_specs, out_spec, out_shape, args):
    return pl.pallas_call(
        body, name=name,
        grid_spec=pltpu.PrefetchScalarGridSpec(num_scalar_prefetch=1, grid=grid, in_specs=in_specs,
                                               out_specs=out_spec),
        out_shape=out_shape, compiler_params=_params("parallel"),
    )(jnp.reshape(scalar, (-1,)).astype(jnp.int32), *args)


def _pair_sum(view, got, c, name):
    nb, _, rh, cols = view.shape
    tr = _pick(rh, max(16, (1 << 20) // cols // 16 * 16), 16)
    bpr = rh // tr

    def body(s_ref, a_ref, b_ref, o_ref):
        o_ref[...] = (a_ref[...].astype(F32) + b_ref[...].astype(F32)).astype(BF16)

    spec = pl.BlockSpec((tr, cols), lambda i, s: (i, 0))
    mine = pl.BlockSpec((tr, cols), lambda i, s: ((2 * (i // bpr) + s[0]) * bpr + i % bpr, 0))
    return _scalar_call(body, name, c, (nb * bpr,), [mine, spec], spec,
                        jax.ShapeDtypeStruct((nb * rh, cols), BF16),
                        (view.reshape(nb * 2 * rh, cols), got.reshape(nb * rh, cols)))


STACKED_LAYERS = 2


def _owner_sum(pair, recv, me, c, it, name, layer=None, into=None):
    _, rh, bc = recv.shape
    tr = _pick(rh, max(16, (1 << 19) // bc // 16 * 16), 16)
    bpr = rh // tr

    def body(s_ref, a_ref, r0, r1, r2, *rest):
        rest[-1][...] = ((a_ref[...].astype(F32) + r0[...].astype(F32)) + r1[...].astype(F32)) + r2[...].astype(F32)

    if it.kind == "col":
        own = pl.BlockSpec((tr, bc), lambda i, s: (i, s[0]))
    else:
        own = pl.BlockSpec((tr, bc), lambda i, s: (s[0] * bpr + i, 0))
    slots = [pl.BlockSpec((None, tr, bc), functools.partial(lambda i, s, k: (k, i, 0), k=k)) for k in range(3)]
    in_specs, args, aliases = [own] + slots, [pair, recv, recv, recv], {}
    if layer is None:
        out_spec = pl.BlockSpec((tr, bc), lambda i, s: (s[1] * bpr + i, 0))
        out_shape = jax.ShapeDtypeStruct((2 * rh, bc), F32)
    else:
        out_spec = pl.BlockSpec((None, tr, bc), lambda i, s: (layer, s[1] * bpr + i, 0))
        out_shape = jax.ShapeDtypeStruct((STACKED_LAYERS, 2 * rh, bc), F32)
        if into is not None:
            in_specs.append(pl.BlockSpec(memory_space=pl.ANY))
            args.append(into)
            aliases = {len(args): 0}
    return pl.pallas_call(
        body, name=name,
        grid_spec=pltpu.PrefetchScalarGridSpec(num_scalar_prefetch=1, grid=(bpr,), in_specs=in_specs,
                                               out_specs=out_spec),
        out_shape=out_shape, input_output_aliases=aliases, compiler_params=_params("parallel"),
    )(jnp.stack([it.pos(me), c]).astype(jnp.int32), *args)


def _place():
    x, y, c = lax.axis_index("x"), lax.axis_index("y"), lax.axis_index("c")
    chips = [(1 - x, y), (x, 1 - y), (1 - x, 1 - y)]
    return x, y, c, chips


def _chip_index(px, py):
    return 2 * px + py


def _all_gather_small(x_shard, name):
    m_per, n = x_shard.shape

    def body(x_ref, out_ref, send_sems, recv_sems, local_sem):
        x, y, c, chips = _place()
        me, sibling = (x, y, c), (x, y, 1 - c)

        def rows(px, py, pc):
            return out_ref.at[pl.ds((4 * px + 2 * py + pc) * m_per, m_per), :]

        def copy(k, block, to, src=None):
            return pltpu.make_async_remote_copy(
                src_ref=rows(*block) if src is None else src, dst_ref=rows(*block),
                send_sem=send_sems.at[k], recv_sem=recv_sems.at[k], device_id=to, device_id_type=MESH)

        mine = pltpu.make_async_copy(x_ref, rows(*me), local_sem)
        mine.start()
        first = [copy(0, me, sibling, src=x_ref)]
        first += [copy(1 + j, me, (*chip, c), src=x_ref) for j, chip in enumerate(chips)]
        for cp in first:
            cp.start()
        passed = [copy(4 + j, (*chip, c), sibling) for j, chip in enumerate(chips)]
        for j, chip in enumerate(chips):
            copy(1 + j, (*chip, c), me).wait_recv()
            passed[j].start()
        copy(0, sibling, me).wait_recv()
        for j, chip in enumerate(chips):
            copy(4 + j, (*chip, 1 - c), me).wait_recv()
        for cp in first + passed:
            cp.wait_send()
        mine.wait()

    return pl.pallas_call(
        body, name=name,
        out_shape=jax.ShapeDtypeStruct((N_DEV * m_per, n), x_shard.dtype),
        in_specs=[pl.BlockSpec(memory_space=pltpu.VMEM)],
        out_specs=pl.BlockSpec(memory_space=pltpu.VMEM),
        scratch_shapes=[pltpu.SemaphoreType.DMA((7,)), pltpu.SemaphoreType.DMA((7,)), pltpu.SemaphoreType.DMA],
    )(x_shard)


def _gather_flat(vec, name):
    L = vec.shape[0]
    Lp = -(-L // 1024) * 1024
    g = _all_gather_small(jnp.pad(vec, (0, Lp - L)).reshape(8, Lp // 8), name)
    return g.reshape(N_DEV, Lp)[:, :L]


class _Item:
    def __init__(self, kind, rows, cols, arg, layer, swap=False):
        self.kind, self.rows, self.cols, self.arg, self.layer, self.swap = kind, rows, cols, arg, layer, swap

    def ref(self, refs):
        return refs[self.arg].at[self.layer]

    def pos(self, j):
        return 2 * (j % 2) + j // 2 if self.swap else j


def _block(ref, it, j, half):
    if it.kind == "col":
        ns = it.cols // N_CHIP
        return ref.at[pl.ds(half * (it.rows // 2), it.rows // 2), pl.ds(it.pos(j) * ns, ns)]
    rs = it.rows // N_CHIP
    return ref.at[pl.ds(j * rs + half * (rs // 2), rs // 2), :]


def _cast_place(w, layer, kind, pos, after, name):
    _, r, n = w.shape
    tr = _pick(r, max(16, (1 << 20) // n // 16 * 16), 16)
    bpr = r // tr

    def body(s_ref, w_ref, after_ref, o_ref):
        o_ref[...] = w_ref[...].astype(BF16)

    if kind == "col":
        full, out_idx = (1, r, N_CHIP * n), (lambda i, s: (0, i, s[0]))
    else:
        full, out_idx = (1, N_CHIP * r, n), (lambda i, s: (0, s[0] * bpr + i, 0))
    return pl.pallas_call(
        body, name=name,
        grid_spec=pltpu.PrefetchScalarGridSpec(
            num_scalar_prefetch=1, grid=(bpr,),
            in_specs=[pl.BlockSpec((None, tr, n), lambda i, s: (layer, i, 0)), pl.BlockSpec(memory_space=pl.ANY)],
            out_specs=pl.BlockSpec((None, tr, n), out_idx)),
        out_shape=jax.ShapeDtypeStruct(full, BF16),
        compiler_params=_params("parallel"),
    )(jnp.reshape(pos, (1,)).astype(jnp.int32), w, after)


HBM_SPEC = pl.BlockSpec(memory_space=pltpu.HBM)
SEM_SPEC = pl.BlockSpec(memory_space=pltpu.SEMAPHORE)
ANY_SPEC = pl.BlockSpec(memory_space=pl.ANY)
SPLIT_PARAMS = dict(has_side_effects=pltpu.SideEffectType.DATAFLOW_SIDE_EFFECTING)


def _in_hbm(a):
    return pltpu.with_memory_space_constraint(a, pltpu.HBM)


def _split_start(copies_of, bufs, n_sem, after, name):
    n = len(bufs)

    def body(*refs):
        ins, send, recv, token = refs[:n], refs[n + 1], refs[n + 2], refs[2 * n + 3]
        for cp in copies_of(ins, send, recv, False)[0]:
            cp.start()
        token[...] = jnp.zeros_like(token)

    outs = pl.pallas_call(
        body, name=name,
        out_shape=(pltpu.SemaphoreType.DMA(n_sem), pltpu.SemaphoreType.DMA(n_sem),
                   *[pltpu.HBM(b.shape, b.dtype) for b in bufs], jax.ShapeDtypeStruct((8, 128), F32)),
        in_specs=[HBM_SPEC] * n + [ANY_SPEC],
        out_specs=(SEM_SPEC, SEM_SPEC, *[HBM_SPEC] * n, pl.BlockSpec(memory_space=pltpu.VMEM)),
        input_output_aliases={t: 2 + t for t in range(n)},
        compiler_params=pltpu.CompilerParams(**SPLIT_PARAMS),
    )(*[_in_hbm(b) for b in bufs], after)
    return outs[0], outs[1], list(outs[2:2 + n]), outs[2 + n]


def _split_wait(copies_of, send, recv, bufs, after, name):
    n = len(bufs)
    after = list(after) if isinstance(after, (list, tuple)) else [after]

    def body(*refs):
        ins, send_ref, recv_ref = refs[:n], refs[n], refs[n + 1]
        sends, arrivals = copies_of(ins, send_ref, recv_ref, True)
        for cp in sends:
            cp.wait_send()
        for cp in arrivals:
            cp.wait_recv()

    return pl.pallas_call(
        body, name=name,
        out_shape=[pltpu.HBM(b.shape, b.dtype) for b in bufs],
        in_specs=[HBM_SPEC] * n + [SEM_SPEC, SEM_SPEC] + [ANY_SPEC] * len(after),
        out_specs=[HBM_SPEC] * n,
        input_output_aliases={t: t for t in range(n)},
        compiler_params=pltpu.CompilerParams(**SPLIT_PARAMS),
    )(*bufs, send, recv, *after)


def _gather_copies(items):
    def copies_of(refs, send, recv, with_arrivals):
        x, y, c, chips = _place()
        me = _chip_index(x, y)
        sends, arrivals = [], []
        for t, it in enumerate(items):
            for k, chip in enumerate(chips):
                for core in range(2):
                    mine = _block(it.ref(refs), it, me, c)
                    sends.append(pltpu.make_async_remote_copy(
                        src_ref=mine, dst_ref=mine, send_sem=send.at[6 * t + 2 * k + core],
                        recv_sem=recv.at[6 * t + 2 * k + c], device_id=(*chip, core), device_id_type=MESH))
                    if with_arrivals:
                        landed = _block(it.ref(refs), it, _chip_index(*chip), core)
                        arrivals.append(pltpu.make_async_remote_copy(
                            src_ref=landed, dst_ref=landed, send_sem=send.at[6 * t + 2 * k + core],
                            recv_sem=recv.at[6 * t + 2 * k + core], device_id=(*chip, core), device_id_type=MESH))
        return sends, arrivals

    return copies_of


def _owner_copies(items):
    n = len(items)

    def blk(ref, it, j):
        if it.kind == "col":
            ns = it.cols // N_CHIP
            return ref.at[:, pl.ds(it.pos(j) * ns, ns)]
        return ref.at[j]

    def copies_of(refs, send, recv, with_arrivals):
        x, y, c, chips = _place()
        sends, arrivals = [], []
        for t, it in enumerate(items):
            for k, chip in enumerate(chips):
                slot = refs[n + t].at[k]
                sends.append(pltpu.make_async_remote_copy(
                    src_ref=blk(refs[t], it, _chip_index(*chip)), dst_ref=slot, send_sem=send.at[3 * t + k],
                    recv_sem=recv.at[3 * t + k], device_id=(*chip, c), device_id_type=MESH))
                if with_arrivals:
                    arrivals.append(pltpu.make_async_remote_copy(
                        src_ref=slot, dst_ref=slot, send_sem=send.at[3 * t + k], recv_sem=recv.at[3 * t + k],
                        device_id=(*chip, c), device_id_type=MESH))
        return sends, arrivals

    return copies_of


def _owner_slot_shape(it):
    if it.kind == "col":
        return (3, it.rows // 2, it.cols // N_CHIP)
    return (3, it.rows // (2 * N_CHIP), it.cols)


def _pair_view(g, it):
    if it.kind == "col":
        return g.reshape(1, 2, it.rows // 2, it.cols)
    return g.reshape(N_CHIP, 2, it.rows // (2 * N_CHIP), it.cols)


def _pair_copies(n):
    def copies_of(refs, send, recv, with_arrivals):
        x, y, c, _ = _place()
        sends, arrivals = [], []
        for t in range(n):
            land = refs[n + t]
            sends.append(pltpu.make_async_remote_copy(
                src_ref=refs[t].at[:, pl.ds(1 - c, 1)], dst_ref=land, send_sem=send.at[t], recv_sem=recv.at[t],
                device_id=(x, y, 1 - c), device_id_type=MESH))
            if with_arrivals:
                arrivals.append(pltpu.make_async_remote_copy(
                    src_ref=land, dst_ref=land, send_sem=send.at[t], recv_sem=recv.at[t],
                    device_id=(x, y, 1 - c), device_id_type=MESH))
        return sends, arrivals

    return copies_of


def _half_copies(n):
    def half(ref, which):
        r2 = ref.shape[-2] // 2
        rows = pl.ds(which * r2, r2)
        return ref.at[rows, :] if len(ref.shape) == 2 else ref.at[:, rows, :]

    def copies_of(refs, send, recv, with_arrivals):
        x, y, c, _ = _place()
        sends, arrivals = [], []
        for t in range(n):
            mine = half(refs[t], c)
            sends.append(pltpu.make_async_remote_copy(
                src_ref=mine, dst_ref=mine, send_sem=send.at[t], recv_sem=recv.at[t],
                device_id=(x, y, 1 - c), device_id_type=MESH))
            if with_arrivals:
                theirs = half(refs[t], 1 - c)
                arrivals.append(pltpu.make_async_remote_copy(
                    src_ref=theirs, dst_ref=theirs, send_sem=send.at[t], recv_sem=recv.at[t],
                    device_id=(x, y, 1 - c), device_id_type=MESH))
        return sends, arrivals

    return copies_of


class _Reduction:
    pass


def _pair_start(grads, items, after, tag, names, layer=None):
    n = len(items)
    views = [_pair_view(g, it) for g, it in zip(grads, items)]
    lands = [lax.empty((v.shape[0], 1) + v.shape[2:], v.dtype) for v in views]
    r = _Reduction()
    r.items, r.tag, r.names, r.layer = items, tag, names, layer
    r.send, r.recv, r.bufs, r.token = _split_start(_pair_copies(n), views + lands, (n,), after, f"rs_pair_start_{tag}")
    return r


def _owner_start(r, after):
    x, y, c, _ = _place()
    n = len(r.items)
    bufs = _split_wait(_pair_copies(n), r.send, r.recv, r.bufs, after, f"rs_pair_wait_{r.tag}")
    pairs = [_pair_sum(bufs[t], bufs[n + t], c, f"rs_pair_sum_{r.tag}_{t}") for t in range(n)]
    shaped = [p if it.kind == "col" else p.reshape(N_CHIP, p.shape[0] // N_CHIP, p.shape[1])
              for p, it in zip(pairs, r.items)]
    lands = [lax.empty(_owner_slot_shape(it), BF16) for it in r.items]
    r.send, r.recv, r.bufs, r.token = _split_start(
        _owner_copies(r.items), shaped + lands, (3 * n,), r.token, f"rs_owner_start_{r.tag}")
    return r


def _reduce_finish(groups, after):
    x, y, c, _ = _place()
    me = _chip_index(x, y)
    halves = {}
    behind = [after]
    for r in groups:
        n = len(r.items)
        bufs = _split_wait(_owner_copies(r.items), r.send, r.recv, r.bufs, behind, f"rs_owner_wait_{r.tag}")
        for t, (it, nm) in enumerate(zip(r.items, r.names)):
            pair = bufs[t].reshape(-1, bufs[t].shape[-1])
            halves[nm] = _owner_sum(pair, bufs[n + t], me, c, it, f"rs_owner_sum_{r.tag}_{t}",
                                    layer=r.layer, into=halves.get(nm))
        behind = [after] + [halves[nm] for nm in r.names]
    n = len(halves)
    return list(halves), _split_start(_half_copies(n), list(halves.values()), (n,), after, "rs_half_start")


def _silu(v):
    return v * jax.nn.sigmoid(v)


def _sum8(p):
    return jnp.sum(p, axis=-2)


def kernel(x, c, mod_w, mod_b, norm_g, ffn_w_in, ffn_w_out, conv_w_in, conv_k, conv_w_out, kv_mod_w, kv_mod_b, kv_norm_g, w_kv, attn_w_q, attn_w_o, rel_bias, loss_target, m_mod_w, m_mod_b, m_norm_g, m_ffn_w_in, m_ffn_w_out, m_conv_w_in, m_conv_k, m_conv_w_out, m_kv_mod_w, m_kv_mod_b, m_kv_norm_g, m_w_kv, m_attn_w_q, m_attn_w_o, m_rel_bias, v_mod_w, v_mod_b, v_norm_g, v_ffn_w_in, v_ffn_w_out, v_conv_w_in, v_conv_k, v_conv_w_out, v_kv_mod_w, v_kv_mod_b, v_kv_norm_g, v_w_kv, v_attn_w_q, v_attn_w_o, v_rel_bias):
    xi, yi, ci = lax.axis_index("x"), lax.axis_index("y"), lax.axis_index("c")
    chip = 2 * xi + yi
    dev = 2 * chip + ci
    _, S, D = x.shape
    F = ffn_w_out.shape[1] * N_CHIP
    x0 = x.reshape(S, D)
    target = loss_target.reshape(S, D)
    n_mod = mod_w.shape[2]
    n_kvm = kv_mod_w.shape[1]
    dsh = D // N_CHIP
    TF = F // 2

    c_all = _all_gather_small(c.reshape(8, D // 8), "ag_c").reshape(N_DEV, D)
    sc16 = jnp.pad(_silu(c_all), ((0, 8), (0, 0)))
    part = [_mm(sc16, mod_w, "nn", F32, f"mod_fwd_{l}", b_layer=l)[:8] for l in range(2)]
    part.append(_mm(sc16, kv_mod_w, "nn", F32, "mod_fwd_kv")[:8])
    fwd_vec = jnp.concatenate([p.reshape(-1) for p in part] + [norm_g.reshape(-1), conv_k.reshape(-1)])
    fwd_all = _gather_flat(fwd_vec, "ag_fwd_small")[0::2]
    o = 0
    mods = []
    for n in (n_mod, n_mod, n_kvm):
        blk = fwd_all[:, o:o + 8 * n].reshape(N_CHIP, 8, n)
        mods.append(lax.dynamic_index_in_dim(blk, dev, axis=1, keepdims=False).reshape(N_CHIP * n))
        o += 8 * n
    ng = fwd_all[:, o:o + 8 * dsh].reshape(N_CHIP, 2, 4, dsh).transpose(1, 2, 0, 3).reshape(2, 4, D)
    o += 8 * dsh
    ck = fwd_all[:, o:o + 3 * dsh].reshape(N_CHIP, 3, dsh).transpose(1, 0, 2).reshape(3, D)
    ck8 = jnp.pad(ck, ((0, 5), (0, 0)))
    mod = [mods[l] + mod_b[l] for l in range(2)]
    sh1, sc1, g1, sh2, sc2, g2 = zip(*[jnp.split(m, 6) for m in mod])
    kv_sh, kv_sc = jnp.split(mods[2] + kv_mod_b, 2)
    row = lambda v: v.reshape(1, D)

    it_conv = [_Item("col", D, 3 * D, 0, 0), _Item("row", D, D, 1, 0)]
    it_ffn = [_Item("col", D, 2 * F, 0, 0, swap=True), _Item("row", F, D, 1, 0)]
    it_attn = [_Item("col", D, 2 * D, 0, 0, swap=True), _Item("row", D, D, 1, 0), _Item("row", D, D, 2, 0)]

    def placed(w, layer, it, nm, after=fwd_all):
        return _cast_place(w, layer, it.kind, it.pos(chip), after, f"place_{nm}")

    flying = {}

    def start(tag, its, bufs, after):
        send, recv, bufs, tok = _split_start(_gather_copies(its), bufs, (6 * len(its),), after, f"ag_start_{tag}")
        flying[tag] = (its, send, recv, bufs)
        return tok

    def arrived(tag, after):
        its, send, recv, bufs = flying[tag]
        return _split_wait(_gather_copies(its), send, recv, bufs, after, f"ag_wait_{tag}")

    one = lambda it: [_Item(it.kind, it.rows, it.cols, 0, 0, it.swap)]
    tok = start("conv_in", one(it_conv[0]), [placed(conv_w_in, 0, it_conv[0], "conv_w_in")], fwd_all)
    tok = start("conv_out", one(it_conv[1]), [placed(conv_w_out, 0, it_conv[1], "conv_w_out", tok)], tok)
    tok = start("ffn0_in", one(it_ffn[0]), [placed(ffn_w_in, 0, it_ffn[0], "ffn_w_in0", tok)], tok)
    tok = start("ffn0_out", one(it_ffn[1]), [placed(ffn_w_out, 0, it_ffn[1], "ffn_w_out0", tok)], tok)
    tok = start("attn", it_attn, [placed(w_kv[None], 0, it_attn[0], "w_kv", tok),
                                  placed(attn_w_q, 0, it_attn[1], "attn_w_q", tok),
                                  placed(attn_w_o, 0, it_attn[2], "attn_w_o", tok)], tok)
    token = start("ffn1", it_ffn, [placed(ffn_w_in, 1, it_ffn[0], "ffn_w_in1", tok),
                                   placed(ffn_w_out, 1, it_ffn[1], "ffn_w_out1", tok)], tok)

    a1 = row(ng[0, 0] * (1.0 + sc1[0])) + token[0, 0]
    (h1,) = _norm_mod(x0, a1, row(sh1[0]), "l0_norm1")
    tab = _bias_table(rel_bias[0], "l1_bias_table")
    h1, tab = lax.optimization_barrier((h1, tab))
    (W_cin,) = arrived("conv_in", h1)
    bcx, ug = _conv_in_gate(h1, W_cin, ck8, "l0_conv_in_gate")
    gt1 = row(g1[0] * ng[0, 1])
    a2 = row(ng[0, 2] * (1.0 + sc2[0]))
    (W_cout,) = arrived("conv_out", ug)
    y1, x1, h2 = _mm_post(ug, W_cout, x0, gt1, "l0_conv_out", scales=a2, shifts=row(sh2[0]))
    (W_fin0,) = arrived("ffn0_in", h2)
    gu0, act0 = _ffn_in_act(h2, W_fin0, 0, "l0_ffn_in")
    (W_fout0,) = arrived("ffn0_out", act0)
    gt2 = row(g2[0] * ng[0, 3])
    a3 = ng[1, 0] * (1.0 + sc1[1])
    akv = kv_norm_g * (1.0 + kv_sc)
    y2, x2, h3, hkv = _mm_post(act0, W_fout0, x1, gt2, "l0_ffn_out",
                               scales=jnp.stack([a3, akv]), shifts=jnp.stack([sh1[1], kv_sh]))
    W_kv, W_q, W_o = arrived("attn", hkv)
    kvp = _mm(hkv, W_kv, "nn", BF16, "l1_kv", b_layer=0, tm=512, tn=2 * D)
    att_scale = (D // N_HEADS) ** -0.5
    assert math.log2(att_scale) % 1 == 0, "scaling q before its bf16 cast is exact only for a power of two"
    qp = _mm(h3, W_q, "nn", BF16, "l1_q", b_layer=0, scale=att_scale)
    oh = _attn_fwd(qp, kvp, tab, "l1_attn")
    gt3 = row(g1[1] * ng[1, 1])
    a4 = row(ng[1, 2] * (1.0 + sc2[1]))
    y3, x3, h4 = _mm_post(oh, W_o, x2, gt3, "l1_attn_out", scales=a4, shifts=row(sh2[1]))
    W_fin1, W_fout1 = arrived("ffn1", h4)
    gu1, act1 = _ffn_in_act(h4, W_fin1, 0, "l1_ffn_in")
    gt4 = row(g2[1] * ng[1, 3])
    dx4, sq, dy4, dgt4 = _mm_post(act1, W_fout1, x3, gt4, "l1_ffn_out", target=target)
    loss_part = 0.5 * jnp.sum(sq) / D

    def ffn_bwd(dy, dxn, xin_, h, gu, act, a, w_in, w_out, post, tag):
        dgu, dx, ds, db, dyn, dgt = _ffn_bwd(dy, w_out, gu, w_in, xin_, dxn, a, post, f"{tag}_ffn_bwd")
        g_fout = _mm(act, dy, "tn", BF16, f"{tag}_ffn_out_dw", tm=TF)
        g_fin = _mm(h, dgu, "tn", BF16, f"{tag}_ffn_in_dw", tn=TF)
        return dx, ds, db, dyn, dgt, g_fin, g_fout

    dx3, ds4, db4, dy3, dgt3, G_fin1, G_fout1 = ffn_bwd(dy4, dx4, x3, h4, gu1, act1, a4, W_fin1, W_fout1,
                                                        (y3, gt3), "l1")
    red = [_pair_start([G_fin1, G_fout1], it_ffn, token, "ffn1", ["ffn_w_in", "ffn_w_out"], layer=1)]
    doh = _mm(dy3, W_o, "nt", BF16, "l1_attn_out_dx", b_layer=0, after=red[0].token)
    G_o = _mm(oh, dy3, "tn", BF16, "l1_attn_out_dw")
    _owner_start(red[0], G_o)
    dq, dkv, dtab = _attn_bwd(qp, kvp, tab, doh, "l1_attn_bwd")
    d_rel = _bias_table_grad(dtab)
    G_q = _mm(h3, dq, "tn", BF16, "l1_q_dw")
    G_kv = _mm(hkv, dkv, "tn", BF16, "l1_kv_dw")
    red.append(_pair_start([G_kv, G_q, G_o], it_attn, red[-1].token, "attn", ["w_kv", "attn_w_q", "attn_w_o"]))
    dx2, ds3, db3, dy2, dgt2 = _mm_pre_bwd([(dq, W_q), (dkv, W_kv)], x2, dx3,
                                           jnp.stack([a3, akv]) + red[1].token[0, 0], "l1_qkv_dx", post=(y2, gt2))
    _owner_start(red[1], dx2)

    dx1, ds2, db2, dy1, dgt1, G_fin0, G_fout0 = ffn_bwd(dy2, dx2, x1, h2, gu0, act0, a2, W_fin0, W_fout0,
                                                        (y1, gt1), "l0")
    red.append(_pair_start([G_fin0, G_fout0], it_ffn, red[-1].token, "ffn0", ["ffn_w_in", "ffn_w_out"], layer=0))
    dug = _mm(dy1, W_cout, "nt", BF16, "l0_conv_out_dx", b_layer=0, after=red[2].token)
    G_cout = _mm(ug, dy1, "tn", BF16, "l0_conv_out_dw")
    dbcx, dck = _conv_gate_bwd(dug, bcx, ck8, "l0_conv_gate_bwd")
    _owner_start(red[2], dbcx)
    G_cin = _mm(h1, dbcx, "tn", BF16, "l0_conv_in_dw")
    red.append(_pair_start([G_cin, G_cout], it_conv, red[-1].token, "conv", ["conv_w_in", "conv_w_out"]))
    dx0, ds1, db1 = _mm_pre_bwd([(dbcx, W_cin)], x0, dx1, a1 + red[3].token[0, 0], "l0_conv_in_dx")
    ds1, db1 = _sum8(ds1)[0], _sum8(db1)[0]
    da2, db2 = _sum8(ds2)[0], _sum8(db2)[0]
    ds3, db3 = _sum8(ds3), _sum8(db3)
    da4, db4 = _sum8(ds4)[0], _sum8(db4)[0]
    dgt1, dgt2, dgt3, dgt4 = _sum8(dgt1), _sum8(dgt2), _sum8(dgt3), _sum8(dgt4)

    def dmod_of(l, ds_a, db_a, dgt_a, ds_b, db_b, dgt_b):
        return jnp.concatenate([db_a, ds_a * ng[l, 0], dgt_a * ng[l, 1], db_b, ds_b * ng[l, 2], dgt_b * ng[l, 3]])

    dmod0 = dmod_of(0, ds1, db1, dgt1, da2, db2, dgt2)
    dmod1 = dmod_of(1, ds3[0], db3[0], dgt3, da4, db4, dgt4)
    dkvmod = jnp.concatenate([db3[1], ds3[1] * kv_norm_g])
    dng = jnp.stack([
        jnp.stack([ds1 * (1.0 + sc1[0]), dgt1 * g1[0], da2 * (1.0 + sc2[0]), dgt2 * g2[0]]),
        jnp.stack([ds3[0] * (1.0 + sc1[1]), dgt3 * g1[1], da4 * (1.0 + sc2[1]), dgt4 * g2[1]])])
    dkvng = ds3[1] * (1.0 + kv_sc)
    small = [dmod0, dmod1, dkvmod, dng.reshape(-1), dkvng, _sum8(dck).reshape(-1), d_rel.reshape(-1),
             loss_part.reshape(1)]
    sizes = [int(s.shape[0]) for s in small]
    offs = np.concatenate([[0], np.cumsum(sizes)])
    bwd_all = _gather_flat(jnp.concatenate(small), "ag_bwd_small")
    _owner_start(red[3], bwd_all)
    Lb = bwd_all.shape[1]
    Lp = -(-Lb // 128) * 128
    tot = _sum_rows(jnp.pad(bwd_all, ((0, 0), (0, Lp - Lb))), "sum_small")[0]
    seg = lambda i: tot[offs[i]:offs[i + 1]]
    g_mod_b = jnp.stack([seg(0), seg(1)])
    g_kv_mod_b = seg(2)
    g_norm_g = lax.dynamic_slice_in_dim(seg(3).reshape(2, 4, D), chip * dsh, dsh, axis=2)
    g_kv_norm_g = seg(4)
    g_conv_k = lax.dynamic_slice_in_dim(seg(5).reshape(1, 3, D), chip * dsh, dsh, axis=2)
    g_rel_bias = seg(6).reshape(rel_bias.shape)
    loss = seg(7)[0]

    def dmod_rows(i, n):
        rows_ = lax.dynamic_slice_in_dim(bwd_all[:, offs[i]:offs[i + 1]], chip * n, n, axis=1)
        return jnp.pad(rows_, ((0, 8), (0, 0)))

    g_mod_w = _mm(sc16, jnp.concatenate([dmod_rows(0, n_mod), dmod_rows(1, n_mod)], axis=1), "tn", F32,
                  "mod_bwd", out_layers=STACKED_LAYERS)
    g_kv_mod_w = _mm(sc16, dmod_rows(2, n_kvm), "tn", F32, "mod_bwd_kv")

    grads = {
        "mod_w": g_mod_w, "mod_b": g_mod_b, "norm_g": g_norm_g, "conv_k": g_conv_k,
        "kv_mod_w": g_kv_mod_w, "kv_mod_b": g_kv_mod_b, "kv_norm_g": g_kv_norm_g, "rel_bias": g_rel_bias,
    }
    weights = dict(mod_w=mod_w, mod_b=mod_b, norm_g=norm_g, ffn_w_in=ffn_w_in, ffn_w_out=ffn_w_out,
                   conv_w_in=conv_w_in, conv_k=conv_k, conv_w_out=conv_w_out, kv_mod_w=kv_mod_w,
                   kv_mod_b=kv_mod_b, kv_norm_g=kv_norm_g, w_kv=w_kv, attn_w_q=attn_w_q, attn_w_o=attn_w_o,
                   rel_bias=rel_bias)
    m_in = dict(mod_w=m_mod_w, mod_b=m_mod_b, norm_g=m_norm_g, ffn_w_in=m_ffn_w_in, ffn_w_out=m_ffn_w_out,
                conv_w_in=m_conv_w_in, conv_k=m_conv_k, conv_w_out=m_conv_w_out, kv_mod_w=m_kv_mod_w,
                kv_mod_b=m_kv_mod_b, kv_norm_g=m_kv_norm_g, w_kv=m_w_kv, attn_w_q=m_attn_w_q,
                attn_w_o=m_attn_w_o, rel_bias=m_rel_bias)
    v_in = dict(mod_w=v_mod_w, mod_b=v_mod_b, norm_g=v_norm_g, ffn_w_in=v_ffn_w_in, ffn_w_out=v_ffn_w_out,
                conv_w_in=v_conv_w_in, conv_k=v_conv_k, conv_w_out=v_conv_w_out, kv_mod_w=v_kv_mod_w,
                kv_mod_b=v_kv_mod_b, kv_norm_g=v_kv_norm_g, w_kv=v_w_kv, attn_w_q=v_attn_w_q,
                attn_w_o=v_attn_w_o, rel_bias=v_rel_bias)
    names = list(weights)
    step = {}

    def update(n, echo=False):
        g = grads[n].reshape(weights[n].shape)
        outs = _adamw(weights[n], g, m_in[n], v_in[n], f"adamw_{n}", echo=echo)
        step[n] = outs if echo else (g, *outs)

    update("mod_w")
    reduced, (half_send, half_recv, half_bufs, _) = _reduce_finish(red, step["mod_w"][1])
    local = [n for n in grads if n != "mod_w"]
    for n in local:
        update(n)
    grads.update(zip(reduced, _split_wait(
        _half_copies(len(half_bufs)), half_send, half_recv, half_bufs, [step[n][1] for n in local], "rs_half_wait")))
    for n in reduced:
        update(n, echo=True)
    return (loss, dx0.reshape(x.shape), *[step[n][k] for k in range(4) for n in names])
```

```python
import functools
import math

import numpy as np
import jax
import jax.numpy as jnp
from jax import lax
from jax.experimental import pallas as pl
from jax.experimental.pallas import tpu as pltpu

CHUNK = 64
N_LEFT_CHUNKS = 8
N_HEADS = 16
MAX_REL = 2 * CHUNK
N_REL = 2 * MAX_REL + 1
EPS = 1e-6
ADAM_LR = 0.001
ADAM_B1 = 0.9
ADAM_B2 = 0.999
ADAM_EPS = 1e-08
ADAM_WD = 0.01
ADAM_STEP = 10

Q_CHUNKS = 4
BQ = Q_CHUNKS * CHUNK
N_WIN = 1 + N_LEFT_CHUNKS // Q_CHUNKS
HEADS_PER_STEP = 8
NEG = -1e30
N_DEV = 8
N_CHIP = 4
SMALL_TENSOR_ELEMS = 1 << 16
PIECE_ROWS = 256

BF16 = jnp.bfloat16
F32 = jnp.float32
V7X_VMEM_LIMIT_BYTES = 56 * 1024 * 1024
MESH = pl.DeviceIdType.MESH


def _pick(n, pref, align):
    t = min(pref, n)
    t -= t % align
    while t >= align:
        if n % t == 0:
            return t
        t -= align
    return n


def _params(*sem):
    return pltpu.CompilerParams(dimension_semantics=sem, vmem_limit_bytes=V7X_VMEM_LIMIT_BYTES)


def _colsum8(v):
    r, d = v.shape
    return v.reshape(r // 8, 8, d).sum(axis=0)


_DIMS = {"nn": (((1,), (0,)), ((), ())), "nt": (((1,), (1,)), ((), ())), "tn": (((0,), (0,)), ((), ()))}


def _mm(a, b, mode, out_dtype, name, *, b_layer=None, tm=1024, tn=1024, tk=None, scale=None, after=None,
        out_layers=1):
    if tk is None:
        tk = 2048 if mode == "tn" else 3072
    bs = b.shape[1:] if b_layer is not None else b.shape
    if mode == "nn":
        (M, K), (K2, N) = a.shape, bs
    elif mode == "nt":
        (M, K), (N, K2) = a.shape, bs
    else:
        (K, M), (K2, N) = a.shape, bs
    assert K == K2, (name, a.shape, b.shape)
    tm = _pick(M, tm, 128 if mode == "tn" else 16)
    tn = _pick(N // out_layers, tn, 128)
    tk = _pick(K, tk, 128 if mode != "tn" else 16)
    nk = K // tk
    assert scale is None or nk == 1, name
    dims = _DIMS[mode]
    extra = [] if after is None else [after]

    def body(a_ref, b_ref, *rest):
        o_ref, acc = rest[len(extra)], rest[len(extra) + 1:]
        p = lax.dot_general(a_ref[...].astype(BF16), b_ref[...].astype(BF16), dims,
                            preferred_element_type=F32)
        if nk == 1:
            o_ref[...] = (p if scale is None else p * scale).astype(o_ref.dtype)
        else:
            k = pl.program_id(2)

            @pl.when(k == 0)
            def _():
                acc[0][...] = p

            @pl.when(k > 0)
            def _():
                acc[0][...] += p

            @pl.when(k == nk - 1)
            def _():
                o_ref[...] = acc[0][...].astype(o_ref.dtype)

    a_spec = (pl.BlockSpec((tk, tm), lambda i, j, k: (k, i)) if mode == "tn"
              else pl.BlockSpec((tm, tk), lambda i, j, k: (i, k)))
    if mode == "nt":
        b_blk, b_idx = (tn, tk), (lambda i, j, k: (j, k))
    else:
        b_blk, b_idx = (tk, tn), (lambda i, j, k: (k, j))
    if b_layer is not None:
        b_spec = pl.BlockSpec((None,) + b_blk, lambda i, j, k: (b_layer,) + b_idx(i, j, k))
    else:
        b_spec = pl.BlockSpec(b_blk, b_idx)
    if out_layers > 1:
        per_layer = N // out_layers // tn
        o_spec = pl.BlockSpec((None, tm, tn), lambda i, j, k: (j // per_layer, i, j % per_layer))
        o_shape = (out_layers, M, N // out_layers)
    else:
        o_spec, o_shape = pl.BlockSpec((tm, tn), lambda i, j, k: (i, j)), (M, N)
    return pl.pallas_call(
        body, name=name,
        grid=(M // tm, N // tn, nk),
        in_specs=[a_spec, b_spec] + [pl.BlockSpec(memory_space=pl.ANY)] * len(extra),
        out_specs=o_spec,
        out_shape=jax.ShapeDtypeStruct(o_shape, out_dtype),
        scratch_shapes=[pltpu.VMEM((tm, tn), F32)] if nk > 1 else [],
        compiler_params=_params("parallel", "parallel", "arbitrary"),
    )(a, b, *extra)


def _row_spec(tm, d):
    return pl.BlockSpec((tm, d), lambda i: (i, 0))


def _vec_spec(r, d):
    return pl.BlockSpec((r, d), lambda i: (0, 0))


def _norm_mod(x, scales, shifts, name):
    S, D = x.shape
    nb = scales.shape[0]
    tm = _pick(S, 1024, 16)

    def body(x_ref, a_ref, b_ref, *o_refs):
        xv = x_ref[...]
        xh = xv * lax.rsqrt(jnp.mean(xv * xv, axis=-1, keepdims=True) + EPS)
        for n in range(nb):
            o_refs[n][...] = (xh * a_ref[n:n + 1, :] + b_ref[n:n + 1, :]).astype(BF16)

    return pl.pallas_call(
        body, name=name, grid=(S // tm,),
        in_specs=[_row_spec(tm, D), _vec_spec(nb, D), _vec_spec(nb, D)],
        out_specs=[_row_spec(tm, D)] * nb,
        out_shape=[jax.ShapeDtypeStruct((S, D), BF16)] * nb,
        compiler_params=_params("parallel"),
    )(x, scales, shifts)


def _mm_post(a, w, x, gate, name, *, scales=None, shifts=None, target=None):
    M, K = a.shape
    D = w.shape[2]
    tm = _pick(M, 1024 if K <= D else 512, 16)
    sub = _pick(tm, PIECE_ROWS, 16)
    nb = 0 if scales is None else scales.shape[0]

    def body(a_ref, w_ref, x_ref, g_ref, *rest):
        if target is None:
            sc_ref, sh_ref, y_ref, xn_ref = rest[:4]
            h_refs = rest[4:]
        else:
            t_ref, dx_ref, sq_ref, dy_ref, dg_ref = rest

            @pl.when(pl.program_id(0) == 0)
            def _():
                sq_ref[...] = jnp.zeros_like(sq_ref)
                dg_ref[...] = jnp.zeros_like(dg_ref)

        def product(r):
            return jnp.dot(a_ref[pl.ds(r * sub, sub), :], w_ref[...], preferred_element_type=F32)

        y = product(0)
        for r in range(tm // sub):
            rows = pl.ds(r * sub, sub)
            yb = y.astype(BF16)
            if r + 1 < tm // sub:
                y = product(r + 1)
            yv = yb.astype(F32)
            yh = yv * lax.rsqrt(jnp.mean(yv * yv, axis=-1, keepdims=True) + EPS)
            xn = x_ref[rows, :] + yh * g_ref[...]
            if target is None:
                y_ref[rows, :] = yb
                xn_ref[rows, :] = xn
                xh = xn * lax.rsqrt(jnp.mean(xn * xn, axis=-1, keepdims=True) + EPS)
                for n in range(nb):
                    h_refs[n][rows, :] = (xh * sc_ref[n:n + 1, :] + sh_ref[n:n + 1, :]).astype(BF16)
            else:
                e = xn - t_ref[rows, :]
                dx = e / D
                dx_ref[rows, :] = dx
                sq_ref[...] += _colsum8(e * e)
                dy, dxy = _post_norm_grad(dx, yb, g_ref[...])
                dy_ref[rows, :] = dy.astype(BF16)
                dg_ref[...] += _colsum8(dxy)

    ins = [a, w, x, gate]
    in_specs = [_row_spec(tm, K), pl.BlockSpec((None, K, D), lambda i: (0, 0, 0)), _row_spec(tm, D), _vec_spec(1, D)]
    if target is None:
        ins += [scales, shifts]
        in_specs += [_vec_spec(nb, D), _vec_spec(nb, D)]
        out_specs = [_row_spec(tm, D)] * (2 + nb)
        out_shape = [jax.ShapeDtypeStruct((M, D), BF16), jax.ShapeDtypeStruct((M, D), F32)] \
            + [jax.ShapeDtypeStruct((M, D), BF16)] * nb
    else:
        ins += [target]
        in_specs += [_row_spec(tm, D)]
        out_specs = [_row_spec(tm, D), _vec_spec(8, D), _row_spec(tm, D), _vec_spec(8, D)]
        out_shape = [jax.ShapeDtypeStruct((M, D), F32), jax.ShapeDtypeStruct((8, D), F32),
                     jax.ShapeDtypeStruct((M, D), BF16), jax.ShapeDtypeStruct((8, D), F32)]
    return pl.pallas_call(
        body, name=name, grid=(M // tm,), in_specs=in_specs, out_specs=out_specs, out_shape=out_shape,
        compiler_params=_params("arbitrary" if target is not None else "parallel"),
    )(*ins)


def _post_norm_grad(dxn, yb, gate):
    yv = yb.astype(F32)
    r = lax.rsqrt(jnp.mean(yv * yv, axis=-1, keepdims=True) + EPS)
    yh = yv * r
    dyh = dxn * gate
    return r * (dyh - yh * jnp.mean(dyh * yh, axis=-1, keepdims=True)), dxn * yh


def _mm_pre_bwd(pairs, x, dxn, scales, name, post=None):
    S, D = x.shape
    nb = len(pairs)
    tm = _pick(S, 512, 16)
    sub = _pick(tm, PIECE_ROWS, 16)

    def body(*refs):
        a_refs, w_refs = refs[0:2 * nb:2], refs[1:2 * nb:2]
        x_ref, d_ref, sc_ref = refs[2 * nb:2 * nb + 3]
        rest = refs[2 * nb + 3:]
        if post is not None:
            y_ref, g_ref, dx_ref, ds_ref, db_ref, dy_ref, dg_ref = rest
        else:
            dx_ref, ds_ref, db_ref = rest

        @pl.when(pl.program_id(0) == 0)
        def _():
            ds_ref[...] = jnp.zeros_like(ds_ref)
            db_ref[...] = jnp.zeros_like(db_ref)
            if post is not None:
                dg_ref[...] = jnp.zeros_like(dg_ref)

        def products(r):
            return [lax.dot_general(a_refs[n][pl.ds(r * sub, sub), :], w_refs[n][...], _DIMS["nt"],
                                    preferred_element_type=F32) for n in range(nb)]

        nxt = products(0)
        for r in range(tm // sub):
            rows = pl.ds(r * sub, sub)
            dhs = nxt
            if r + 1 < tm // sub:
                nxt = products(r + 1)
            xv = x_ref[rows, :]
            rr = lax.rsqrt(jnp.mean(xv * xv, axis=-1, keepdims=True) + EPS)
            xh = xv * rr
            dxh = jnp.zeros_like(xv)
            for n in range(nb):
                dh = dhs[n]
                dxh = dxh + dh * sc_ref[n:n + 1, :]
                ds_ref[n] += _colsum8(dh * xh)
                db_ref[n] += _colsum8(dh)
            dx = d_ref[rows, :] + rr * (dxh - xh * jnp.mean(dxh * xh, axis=-1, keepdims=True))
            dx_ref[rows, :] = dx
            if post is not None:
                dy, dxy = _post_norm_grad(dx, y_ref[rows, :], g_ref[...])
                dy_ref[rows, :] = dy.astype(BF16)
                dg_ref[...] += _colsum8(dxy)

    ins, in_specs = [], []
    for a, w in pairs:
        ins += [a, w]
        in_specs += [_row_spec(tm, a.shape[1]),
                     pl.BlockSpec((None, D, a.shape[1]), lambda i: (0, 0, 0), pipeline_mode=pl.Buffered(1))]
    ins += [x, dxn, scales]
    in_specs += [_row_spec(tm, D), _row_spec(tm, D), _vec_spec(nb, D)]
    acc_spec = pl.BlockSpec((nb, 8, D), lambda i: (0, 0, 0))
    out_specs = [_row_spec(tm, D), acc_spec, acc_spec]
    out_shape = [jax.ShapeDtypeStruct((S, D), F32), jax.ShapeDtypeStruct((nb, 8, D), F32),
                 jax.ShapeDtypeStruct((nb, 8, D), F32)]
    if post is not None:
        ins += list(post)
        in_specs += [_row_spec(tm, D), _vec_spec(1, D)]
        out_specs += [_row_spec(tm, D), _vec_spec(8, D)]
        out_shape += [jax.ShapeDtypeStruct((S, D), BF16), jax.ShapeDtypeStruct((8, D), F32)]
    return pl.pallas_call(
        body, name=name, grid=(S // tm,), in_specs=in_specs, out_specs=out_specs, out_shape=out_shape,
        compiler_params=_params("arbitrary"),
    )(*ins)


FFN_PAIRS = 2


def _ffn_in_act(h, w, layer, name):
    S, D = h.shape
    F2 = w.shape[2]
    PW = F2 // (2 * FFN_PAIRS)
    tm = _pick(S, 1024, 16)
    sub = _pick(tm, PIECE_ROWS, 16)

    def body(h_ref, w_ref, gu_ref, a_ref):
        def product(r):
            return jnp.dot(h_ref[pl.ds(r * sub, sub), :], w_ref[...], preferred_element_type=F32)

        nxt = product(0)
        for r in range(tm // sub):
            rows = pl.ds(r * sub, sub)
            acc = nxt
            if r + 1 < tm // sub:
                nxt = product(r + 1)
            gu_ref[rows, :] = acc.astype(BF16)
            g = acc[:, :PW]
            a_ref[rows, :] = (g * jax.nn.sigmoid(g) * acc[:, PW:]).astype(BF16)

    return pl.pallas_call(
        body, name=name, grid=(FFN_PAIRS, S // tm),
        in_specs=[pl.BlockSpec((tm, D), lambda p, i: (i, 0)),
                  pl.BlockSpec((None, D, 2 * PW), lambda p, i: (layer, 0, p))],
        out_specs=[pl.BlockSpec((tm, 2 * PW), lambda p, i: (i, p)), pl.BlockSpec((tm, PW), lambda p, i: (i, p))],
        out_shape=[jax.ShapeDtypeStruct((S, F2), BF16), jax.ShapeDtypeStruct((S, F2 // 2), BF16)],
        compiler_params=_params("parallel", "parallel"),
    )(h, w)


def _ffn_bwd(dy, w_out, gu, w_in, x, dxn, scale, post, name):
    S, D = dy.shape
    F2 = gu.shape[1]
    PW = F2 // (2 * FFN_PAIRS)
    tm = _pick(S, 256, 16)

    def body(dy_ref, wo_ref, gu_ref, wi_ref, x_ref, d_ref, sc_ref, y_ref, g_ref,
             dgu_ref, dx_ref, ds_ref, db_ref, dyn_ref, dg_ref):
        @pl.when(pl.program_id(0) == 0)
        def _():
            ds_ref[...] = jnp.zeros_like(ds_ref)
            db_ref[...] = jnp.zeros_like(db_ref)
            dg_ref[...] = jnp.zeros_like(dg_ref)

        def first_product(p):
            return lax.dot_general(dy_ref[...], wo_ref[p * PW:(p + 1) * PW, :], _DIMS["nt"],
                                   preferred_element_type=F32)

        dh = jnp.zeros((tm, D), F32)
        nxt = first_product(0)
        for p in range(FFN_PAIRS):
            cols = slice(2 * p * PW, 2 * (p + 1) * PW)
            da = nxt
            if p + 1 < FFN_PAIRS:
                nxt = first_product(p + 1)
            g = gu_ref[:, 2 * p * PW:(2 * p + 1) * PW].astype(F32)
            u = gu_ref[:, (2 * p + 1) * PW:2 * (p + 1) * PW].astype(F32)
            sg = jax.nn.sigmoid(g)
            dgu_ref[:, 2 * p * PW:(2 * p + 1) * PW] = (da * u * (sg * (1.0 + g * (1.0 - sg)))).astype(BF16)
            dgu_ref[:, (2 * p + 1) * PW:2 * (p + 1) * PW] = (da * (g * sg)).astype(BF16)
            dh = dh + lax.dot_general(dgu_ref[:, cols], wi_ref[:, cols], _DIMS["nt"], preferred_element_type=F32)
        xv = x_ref[...]
        rr = lax.rsqrt(jnp.mean(xv * xv, axis=-1, keepdims=True) + EPS)
        xh = xv * rr
        dxh = dh * sc_ref[...]
        ds_ref[0] += _colsum8(dh * xh)
        db_ref[0] += _colsum8(dh)
        dx = d_ref[...] + rr * (dxh - xh * jnp.mean(dxh * xh, axis=-1, keepdims=True))
        dx_ref[...] = dx
        dyn, dxy = _post_norm_grad(dx, y_ref[...], g_ref[...])
        dyn_ref[...] = dyn.astype(BF16)
        dg_ref[...] += _colsum8(dxy)

    resident = dict(pipeline_mode=pl.Buffered(1))
    acc_spec = pl.BlockSpec((1, 8, D), lambda i: (0, 0, 0))
    return pl.pallas_call(
        body, name=name, grid=(S // tm,),
        in_specs=[_row_spec(tm, D), pl.BlockSpec((None, F2 // 2, D), lambda i: (0, 0, 0), **resident),
                  _row_spec(tm, F2), pl.BlockSpec((None, D, F2), lambda i: (0, 0, 0), **resident),
                  _row_spec(tm, D), _row_spec(tm, D), _vec_spec(1, D), _row_spec(tm, D), _vec_spec(1, D)],
        out_specs=[_row_spec(tm, F2), _row_spec(tm, D), acc_spec, acc_spec, _row_spec(tm, D), _vec_spec(8, D)],
        out_shape=[jax.ShapeDtypeStruct((S, F2), BF16), jax.ShapeDtypeStruct((S, D), F32),
                   jax.ShapeDtypeStruct((1, 8, D), F32), jax.ShapeDtypeStruct((1, 8, D), F32),
                   jax.ShapeDtypeStruct((S, D), BF16), jax.ShapeDtypeStruct((8, D), F32)],
        compiler_params=_params("arbitrary"),
    )(dy, w_out, gu, w_in, x, dxn, scale, *post)


HALO = 16


def _conv_terms(bcx_ref, prev_ref, i, tm, D):
    b = bcx_ref[:, 0:D].astype(F32)
    cg = bcx_ref[:, D:2 * D].astype(F32)
    xin = bcx_ref[:, 2 * D:3 * D].astype(F32)
    z = cg * xin
    zp = prev_ref[:, D:2 * D].astype(F32) * prev_ref[:, 2 * D:3 * D].astype(F32)
    zp = jnp.where(i > 0, zp, 0.0)
    z_ext = jnp.concatenate([zp, z], axis=0)
    z1 = pltpu.roll(z_ext, 1, 0)[HALO:, :]
    z2 = pltpu.roll(z_ext, 2, 0)[HALO:, :]
    return b, cg, xin, z, z1, z2


def _conv_in_gate(x, scale, shift, w, ck, name):
    S, D = x.shape
    D3 = w.shape[-1]
    assert w.dtype == BF16, name
    tm = _pick(S, 512, 16)

    def body(x_ref, a_ref, s_ref, w_ref, ck_ref, h_ref, bcx_ref, o_ref, tail):
        @pl.when(pl.program_id(0) == 0)
        def _():
            tail[...] = jnp.zeros_like(tail)

        xv = x_ref[...]
        xh = xv * lax.rsqrt(jnp.mean(xv * xv, axis=-1, keepdims=True) + EPS)
        h_ref[...] = (xh * a_ref[...] + s_ref[...]).astype(BF16)
        bcx_ref[...] = lax.dot_general(h_ref[...], w_ref[...], _DIMS["nn"],
                                       preferred_element_type=F32).astype(BF16)
        b = bcx_ref[:, 0:D].astype(F32)
        z = bcx_ref[:, D:2 * D].astype(F32) * bcx_ref[:, 2 * D:3 * D].astype(F32)
        z_ext = jnp.concatenate([tail[...], z], axis=0)
        z1 = pltpu.roll(z_ext, 1, 0)[HALO:, :]
        z2 = pltpu.roll(z_ext, 2, 0)[HALO:, :]
        conv = ck_ref[0:1, :] * z2 + ck_ref[1:2, :] * z1 + ck_ref[2:3, :] * z
        o_ref[...] = (b * conv).astype(BF16)
        tail[...] = z[tm - HALO:, :]

    return pl.pallas_call(
        body, name=name, grid=(S // tm,),
        in_specs=[_row_spec(tm, D), _vec_spec(1, D), _vec_spec(1, D),
                  pl.BlockSpec((None, D, D3), lambda i: (0, 0, 0), pipeline_mode=pl.Buffered(1)),
                  _vec_spec(8, D)],
        out_specs=[_row_spec(tm, D), _row_spec(tm, D3), _row_spec(tm, D)],
        out_shape=[jax.ShapeDtypeStruct((S, D), BF16), jax.ShapeDtypeStruct((S, D3), BF16),
                   jax.ShapeDtypeStruct((S, D), BF16)],
        scratch_shapes=[pltpu.VMEM((HALO, D), F32)],
        compiler_params=_params("arbitrary"),
    )(x, scale, shift, w, ck)


def _conv_gate_bwd(du, bcx, ck, name):
    S, D3 = bcx.shape
    D = D3 // 3
    tm = _pick(S, 512, 16)
    hb = tm // HALO
    nt = S // tm

    def body(du_ref, dun_ref, bcx_ref, prev_ref, next_ref, ck_ref, o_ref, dk_ref):
        i = pl.program_id(0)
        b, cg, xin, z, z1, z2 = _conv_terms(bcx_ref, prev_ref, i, tm, D)
        k0, k1, k2 = ck_ref[0:1, :], ck_ref[1:2, :], ck_ref[2:3, :]
        conv = k0 * z2 + k1 * z1 + k2 * z
        d = du_ref[...].astype(F32)
        dconv = d * b
        dcn = jnp.where(i < nt - 1, dun_ref[...].astype(F32) * next_ref[:, 0:D].astype(F32), 0.0)
        d_ext = jnp.concatenate([dconv, dcn], axis=0)
        d1 = pltpu.roll(d_ext, tm + HALO - 1, 0)[:tm, :]
        d2 = pltpu.roll(d_ext, tm + HALO - 2, 0)[:tm, :]
        dz = k2 * dconv + k1 * d1 + k0 * d2
        o_ref[:, 0:D] = (d * conv).astype(BF16)
        o_ref[:, D:2 * D] = (dz * xin).astype(BF16)
        o_ref[:, 2 * D:3 * D] = (dz * cg).astype(BF16)

        @pl.when(i == 0)
        def _():
            dk_ref[...] = jnp.zeros_like(dk_ref)

        dk_ref[0] += _colsum8(dconv * z2)
        dk_ref[1] += _colsum8(dconv * z1)
        dk_ref[2] += _colsum8(dconv * z)

    last = S // HALO - 1
    return pl.pallas_call(
        body, name=name, grid=(nt,),
        in_specs=[_row_spec(tm, D),
                  pl.BlockSpec((HALO, D), lambda i: (jnp.minimum((i + 1) * hb, last), 0)),
                  _row_spec(tm, D3),
                  pl.BlockSpec((HALO, D3), lambda i: (jnp.maximum(i * hb - 1, 0), 0)),
                  pl.BlockSpec((HALO, D3), lambda i: (jnp.minimum((i + 1) * hb, last), 0)),
                  _vec_spec(8, D)],
        out_specs=[_row_spec(tm, D3), pl.BlockSpec((3, 8, D), lambda i: (0, 0, 0))],
        out_shape=[jax.ShapeDtypeStruct((S, D3), BF16), jax.ShapeDtypeStruct((3, 8, D), F32)],
        compiler_params=_params("arbitrary"),
    )(du, du, bcx, bcx, bcx, ck)


def _rel_onehot():
    a = np.arange(CHUNK)[:, None]
    b = np.arange(CHUNK)[None, :]
    idx = np.stack([np.clip((N_LEFT_CHUNKS - dl) * CHUNK + a - b, -MAX_REL, MAX_REL) + MAX_REL
                    for dl in (6, 7, 8)]).reshape(-1)
    return (jnp.asarray(idx)[:, None] == jnp.arange(N_REL)[None, :]).astype(F32)


def _bias_table(rel_bias, name):
    H = rel_bias.shape[0]
    near = jnp.dot(rel_bias, _rel_onehot().T, precision=lax.Precision.HIGHEST).reshape(H, 3, CHUNK, CHUNK)
    far = jnp.broadcast_to(rel_bias[:, N_REL - 1][:, None, None], (H, CHUNK, CHUNK))

    def body(near_ref, far_ref, o_ref):
        neg = jnp.full((CHUNK, CHUNK), NEG, F32)
        for v in range(N_WIN):
            for ic in range(Q_CHUNKS):
                for jc in range(N_WIN * Q_CHUNKS):
                    dl = jc - ic
                    if dl < 0 or dl > N_LEFT_CHUNKS or jc < (N_WIN - 1 - v) * Q_CHUNKS:
                        blk = neg
                    else:
                        blk = far_ref[...] if dl <= 5 else near_ref[dl - 6]
                    o_ref[v, ic * CHUNK:(ic + 1) * CHUNK, jc * CHUNK:(jc + 1) * CHUNK] = blk

    return pl.pallas_call(
        body, name=name, grid=(H,),
        in_specs=[pl.BlockSpec((None, 3, CHUNK, CHUNK), lambda h: (h, 0, 0, 0)),
                  pl.BlockSpec((None, CHUNK, CHUNK), lambda h: (h, 0, 0))],
        out_specs=pl.BlockSpec((N_WIN, None, BQ, N_WIN * BQ), lambda h: (0, h, 0, 0)),
        out_shape=jax.ShapeDtypeStruct((N_WIN, H, BQ, N_WIN * BQ), F32),
        compiler_params=_params("parallel"),
    )(near, far)


NEAR_FIRST = 6
SLAB_ROWS = 2 * CHUNK
SLAB_COLS = 4 * CHUNK


def _slab(pair):
    c0 = (NEAR_FIRST + 2 * pair) * CHUNK
    return slice(pair * SLAB_ROWS, (pair + 1) * SLAB_ROWS), slice(c0, c0 + SLAB_COLS)


def _bias_table_grad(dslab):
    H = dslab.shape[0]

    def blk(ic, dl):
        pair, r, col = ic // 2, ic % 2, ic + dl - NEAR_FIRST - 2 * (ic // 2)
        return dslab[:, pair, r * CHUNK:(r + 1) * CHUNK, col * CHUNK:(col + 1) * CHUNK]

    by_dl = [sum(blk(ic, dl) for ic in range(Q_CHUNKS)) for dl in (6, 7, 8)]
    near = jnp.stack(by_dl, axis=1).reshape(H, 3 * CHUNK * CHUNK)
    g = jnp.dot(near, _rel_onehot(), precision=lax.Precision.HIGHEST)
    return g.at[:, N_REL - 1].add(-jnp.sum(near, axis=1))


def _attn_specs(nblk, W):
    last = nblk - 1
    q_spec = pl.BlockSpec((BQ, W), lambda g, i: (jnp.minimum(i, last), g))
    kv_specs = [pl.BlockSpec((BQ, 2 * W), functools.partial(
        lambda g, i, w: (jnp.maximum(jnp.minimum(i, last) - (N_WIN - 1) + w, 0), g), w=w)) for w in range(N_WIN)]
    tab_spec = pl.BlockSpec((None, HEADS_PER_STEP, BQ, N_WIN * BQ),
                            lambda g, i: (jnp.minimum(i, N_WIN - 1), g, 0, 0))
    dtab_spec = pl.BlockSpec((HEADS_PER_STEP, Q_CHUNKS // 2, SLAB_ROWS, SLAB_COLS), lambda g, i: (g, 0, 0, 0))
    return q_spec, kv_specs, tab_spec, dtab_spec


def _attn_scores(q_ref, kT, tab_ref, h, dh):
    return jnp.dot(q_ref[:, h * dh:(h + 1) * dh], kT[h * dh:(h + 1) * dh, :], preferred_element_type=F32) + tab_ref[h]


def _attn_fwd(q, kv, tab, name):
    S, D = q.shape
    dh = D // N_HEADS
    W = HEADS_PER_STEP * dh
    assert 2 * W == D, "the kv layout puts one head group's k beside its v: two head groups"
    q_spec, kv_specs, tab_spec, _ = _attn_specs(S // BQ, W)

    def body(q_ref, *rest):
        tab_ref, o_ref = rest[N_WIN], rest[N_WIN + 1]
        kvw = jnp.concatenate([r[...] for r in rest[:N_WIN]], axis=0)
        kT = kvw[:, :W].T
        vw = kvw[:, W:]
        outs = []
        s = _attn_scores(q_ref, kT, tab_ref, 0, dh)
        for h in range(HEADS_PER_STEP):
            s_next = _attn_scores(q_ref, kT, tab_ref, h + 1, dh) if h + 1 < HEADS_PER_STEP else None
            e = jnp.exp(s - jnp.max(s, axis=-1, keepdims=True))
            l = jnp.sum(e, axis=-1, keepdims=True)
            outs.append(jnp.dot(e.astype(BF16), vw[:, h * dh:(h + 1) * dh], preferred_element_type=F32) / l)
            s = s_next
        o_ref[...] = jnp.concatenate(outs, axis=1).astype(BF16)

    return pl.pallas_call(
        body, name=name, grid=(N_HEADS // HEADS_PER_STEP, S // BQ),
        in_specs=[q_spec] + kv_specs + [tab_spec],
        out_specs=q_spec,
        out_shape=jax.ShapeDtypeStruct((S, D), BF16),
        compiler_params=_params("parallel", "parallel"),
    )(q, *([kv] * N_WIN), tab)


def _attn_bwd(q, kv, tab, do, name):
    S, D = q.shape
    dh = D // N_HEADS
    W = HEADS_PER_STEP * dh
    nblk = S // BQ
    q_spec, kv_specs, tab_spec, dtab_spec = _attn_specs(nblk, W)

    def body(q_ref, *rest):
        tab_ref, do_ref, dq_ref, dkv_ref, dtab_ref, ring = rest[N_WIN:]
        i = pl.program_id(1)

        @pl.when(i == 0)
        def _():
            dtab_ref[...] = jnp.zeros_like(dtab_ref)
            ring[...] = jnp.zeros_like(ring)

        @pl.when(i < nblk)
        def _():
            kvw = jnp.concatenate([r[...] for r in rest[:N_WIN]], axis=0)
            kT = kvw[:, :W].T
            vw = kvw[:, W:]
            qT = q_ref[...].T
            dqs, dks, dvs = [], [], []

            s = _attn_scores(q_ref, kT, tab_ref, 0, dh)
            for h in range(HEADS_PER_STEP):
                hd = slice(h * dh, (h + 1) * dh)
                do_h = do_ref[:, hd]
                dp = lax.dot_general(do_h, vw[:, hd], _DIMS["nt"], preferred_element_type=F32)
                e = jnp.exp(s - jnp.max(s, axis=-1, keepdims=True))
                inv_l = 1.0 / jnp.sum(e, axis=-1, keepdims=True)
                if h + 1 < HEADS_PER_STEP:
                    s = _attn_scores(q_ref, kT, tab_ref, h + 1, dh)
                delta = jnp.sum(e * dp, axis=-1, keepdims=True) * inv_l
                ds = e * ((dp - delta) * inv_l)
                for pair in range(Q_CHUNKS // 2):
                    rows, cols = _slab(pair)
                    dtab_ref[h, pair] += ds[rows, cols]
                dsb = ds.astype(BF16)
                dqs.append(lax.dot_general(kT[hd, :], dsb, _DIMS["nt"], preferred_element_type=F32) * (dh ** -0.5))
                dks.append(jnp.dot(qT[hd, :], dsb, preferred_element_type=F32))
                do_s = (do_h.astype(F32) * inv_l).astype(BF16)
                dvs.append(jnp.dot(do_s.T, e.astype(BF16), preferred_element_type=F32))
            dq_ref[...] = jnp.concatenate(dqs, axis=0).T.astype(BF16)
            dkv = jnp.concatenate(dks + dvs, axis=0).T
            for w in range(N_WIN):
                slot = lax.rem(i + 1 + w, N_WIN)
                part = dkv[w * BQ:(w + 1) * BQ, :]
                if w == N_WIN - 1:
                    ring[slot] = part
                else:
                    ring[slot] += part

        dkv_ref[...] = ring[lax.rem(i + 1, N_WIN)].astype(BF16)

    done_spec = pl.BlockSpec((BQ, 2 * W), lambda g, i: (jnp.maximum(i - (N_WIN - 1), 0), g))
    return pl.pallas_call(
        body, name=name, grid=(N_HEADS // HEADS_PER_STEP, nblk + N_WIN - 1),
        in_specs=[q_spec] + kv_specs + [tab_spec, q_spec],
        out_specs=[q_spec, done_spec, dtab_spec],
        out_shape=[jax.ShapeDtypeStruct((S, D), BF16), jax.ShapeDtypeStruct((S, 2 * D), BF16),
                   jax.ShapeDtypeStruct((N_HEADS, Q_CHUNKS // 2, SLAB_ROWS, SLAB_COLS), F32)],
        scratch_shapes=[pltpu.VMEM((N_WIN, BQ, 2 * W), F32)],
        compiler_params=_params("parallel", "arbitrary"),
    )(q, *([kv] * N_WIN), tab, do)


def _adamw(w, g, m, v, name, echo=False):
    shape = w.shape
    C = shape[-1]
    R = int(np.prod(shape[:-1])) if len(shape) > 1 else 1
    whole = len(shape) >= 2 and R * C <= SMALL_TENSOR_ELEMS
    if whole:
        w2, g2, m2, v2 = w, g, m, v
    else:
        w2, g2, m2, v2 = (t.reshape(R, C) for t in (w, g, m, v))
    tr = _pick(R, max(8, (512 * 1024) // C // 8 * 8), 8)

    def body(w_ref, g_ref, m_ref, v_ref, *out_refs):
        d_ref, nm_ref, nv_ref = out_refs[-3:]
        gv = g_ref[...]
        if echo:
            out_refs[0][...] = gv
        nm = ADAM_B1 * m_ref[...] + (1.0 - ADAM_B1) * gv
        nv = ADAM_B2 * v_ref[...] + (1.0 - ADAM_B2) * jnp.square(gv)
        m_hat = nm / (1.0 - ADAM_B1 ** ADAM_STEP)
        v_hat = nv / (1.0 - ADAM_B2 ** ADAM_STEP)
        d_ref[...] = -ADAM_LR * (m_hat / (jnp.sqrt(v_hat) + ADAM_EPS) + ADAM_WD * w_ref[...])
        nm_ref[...] = nm
        nv_ref[...] = nv

    if whole:
        spec, grid = pl.BlockSpec(shape, lambda i: (0,) * len(shape)), (1,)
    else:
        spec, grid = pl.BlockSpec((tr, C), lambda i: (i, 0)), (R // tr,)
    outs = pl.pallas_call(
        body, name=name, grid=grid,
        in_specs=[spec] * 4, out_specs=[spec] * (3 + echo),
        out_shape=[jax.ShapeDtypeStruct(w2.shape, F32)] * (3 + echo),
        compiler_params=_params("parallel"),
    )(w2, g2, m2, v2)
    return tuple(o.reshape(shape) for o in outs)


def _sum_rows(a, name):
    n, L = a.shape

    def body(a_ref, o_ref):
        acc = a_ref[0:1, :]
        for r in range(1, n):
            acc = acc + a_ref[r:r + 1, :]
        o_ref[...] = acc

    return pl.pallas_call(
        body, name=name, grid=(1,),
        in_specs=[pl.BlockSpec((n, L), lambda i: (0, 0))],
        out_specs=pl.BlockSpec((1, L), lambda i: (0, 0)),
        out_shape=jax.ShapeDtypeStruct((1, L), F32),
        compiler_params=_params("arbitrary"),
    )(a)


def _scalar_call(body, name, scalar, grid, in_specs, out_spec, out_shape, args):
    return pl.pallas_call(
        body, name=name,
        grid_spec=pltpu.PrefetchScalarGridSpec(num_scalar_prefetch=1, grid=grid, in_specs=in_specs,
                                               out_specs=out_spec),
        out_shape=out_shape, compiler_params=_params("parallel"),
    )(jnp.reshape(scalar, (-1,)).astype(jnp.int32), *args)


def _pair_sum(view, got, c, name):
    nb, _, rh, cols = view.shape
    tr = _pick(rh, max(16, (1 << 20) // cols // 16 * 16), 16)
    bpr = rh // tr

    def body(s_ref, a_ref, b_ref, o_ref):
        o_ref[...] = (a_ref[...].astype(F32) + b_ref[...].astype(F32)).astype(BF16)

    spec = pl.BlockSpec((tr, cols), lambda i, s: (i, 0))
    mine = pl.BlockSpec((tr, cols), lambda i, s: ((2 * (i // bpr) + s[0]) * bpr + i % bpr, 0))
    return _scalar_call(body, name, c, (nb * bpr,), [mine, spec], spec,
                        jax.ShapeDtypeStruct((nb * rh, cols), BF16),
                        (view.reshape(nb * 2 * rh, cols), got.reshape(nb * rh, cols)))


STACKED_LAYERS = 2


def _owner_sum(pair, recv, me, c, it, name, layer=None, into=None):
    _, rh, bc = recv.shape
    tr = _pick(rh, max(16, (1 << 19) // bc // 16 * 16), 16)
    bpr = rh // tr

    def body(s_ref, a_ref, r0, r1, r2, *rest):
        rest[-1][...] = ((a_ref[...].astype(F32) + r0[...].astype(F32)) + r1[...].astype(F32)) + r2[...].astype(F32)

    if it.kind == "col":
        own = pl.BlockSpec((tr, bc), lambda i, s: (i, s[0]))
    else:
        own = pl.BlockSpec((tr, bc), lambda i, s: (s[0] * bpr + i, 0))
    slots = [pl.BlockSpec((None, tr, bc), functools.partial(lambda i, s, k: (k, i, 0), k=k)) for k in range(3)]
    in_specs, args, aliases = [own] + slots, [pair, recv, recv, recv], {}
    if layer is None:
        out_spec = pl.BlockSpec((tr, bc), lambda i, s: (s[1] * bpr + i, 0))
        out_shape = jax.ShapeDtypeStruct((2 * rh, bc), F32)
    else:
        out_spec = pl.BlockSpec((None, tr, bc), lambda i, s: (layer, s[1] * bpr + i, 0))
        out_shape = jax.ShapeDtypeStruct((STACKED_LAYERS, 2 * rh, bc), F32)
        if into is not None:
            in_specs.append(pl.BlockSpec(memory_space=pl.ANY))
            args.append(into)
            aliases = {len(args): 0}
    return pl.pallas_call(
        body, name=name,
        grid_spec=pltpu.PrefetchScalarGridSpec(num_scalar_prefetch=1, grid=(bpr,), in_specs=in_specs,
                                               out_specs=out_spec),
        out_shape=out_shape, input_output_aliases=aliases, compiler_params=_params("parallel"),
    )(jnp.stack([it.pos(me), c]).astype(jnp.int32), *args)


def _place():
    x, y, c = lax.axis_index("x"), lax.axis_index("y"), lax.axis_index("c")
    chips = [(1 - x, y), (x, 1 - y), (1 - x, 1 - y)]
    return x, y, c, chips


def _chip_index(px, py):
    return 2 * px + py


def _all_gather_small(x_shard, name):
    m_per, n = x_shard.shape

    def body(x_ref, out_ref, send_sems, recv_sems, local_sem):
        x, y, c, chips = _place()
        me, sibling = (x, y, c), (x, y, 1 - c)

        def rows(px, py, pc):
            return out_ref.at[pl.ds((4 * px + 2 * py + pc) * m_per, m_per), :]

        def copy(k, block, to, src=None):
            return pltpu.make_async_remote_copy(
                src_ref=rows(*block) if src is None else src, dst_ref=rows(*block),
                send_sem=send_sems.at[k], recv_sem=recv_sems.at[k], device_id=to, device_id_type=MESH)

        mine = pltpu.make_async_copy(x_ref, rows(*me), local_sem)
        mine.start()
        first = [copy(0, me, sibling, src=x_ref)]
        first += [copy(1 + j, me, (*chip, c), src=x_ref) for j, chip in enumerate(chips)]
        for cp in first:
            cp.start()
        passed = [copy(4 + j, (*chip, c), sibling) for j, chip in enumerate(chips)]
        for j, chip in enumerate(chips):
            copy(1 + j, (*chip, c), me).wait_recv()
            passed[j].start()
        copy(0, sibling, me).wait_recv()
        for j, chip in enumerate(chips):
            copy(4 + j, (*chip, 1 - c), me).wait_recv()
        for cp in first + passed:
            cp.wait_send()
        mine.wait()

    return pl.pallas_call(
        body, name=name,
        out_shape=jax.ShapeDtypeStruct((N_DEV * m_per, n), x_shard.dtype),
        in_specs=[pl.BlockSpec(memory_space=pltpu.VMEM)],
        out_specs=pl.BlockSpec(memory_space=pltpu.VMEM),
        scratch_shapes=[pltpu.SemaphoreType.DMA((7,)), pltpu.SemaphoreType.DMA((7,)), pltpu.SemaphoreType.DMA],
    )(x_shard)


def _gather_flat(vec, name):
    L = vec.shape[0]
    Lp = -(-L // 1024) * 1024
    g = _all_gather_small(jnp.pad(vec, (0, Lp - L)).reshape(8, Lp // 8), name)
    return g.reshape(N_DEV, Lp)[:, :L]


class _Item:
    def __init__(self, kind, rows, cols, arg, layer, swap=False):
        self.kind, self.rows, self.cols, self.arg, self.layer, self.swap = kind, rows, cols, arg, layer, swap

    def ref(self, refs):
        return refs[self.arg].at[self.layer]

    def pos(self, j):
        return 2 * (j % 2) + j // 2 if self.swap else j


def _block(ref, it, j, half):
    if it.kind == "col":
        ns = it.cols // N_CHIP
        return ref.at[pl.ds(half * (it.rows // 2), it.rows // 2), pl.ds(it.pos(j) * ns, ns)]
    rs = it.rows // N_CHIP
    return ref.at[pl.ds(j * rs + half * (rs // 2), rs // 2), :]


def _cast_place(w, layer, kind, pos, after, name):
    _, r, n = w.shape
    tr = _pick(r, max(16, (1 << 20) // n // 16 * 16), 16)
    bpr = r // tr

    def body(s_ref, w_ref, after_ref, o_ref):
        o_ref[...] = w_ref[...].astype(BF16)

    if kind == "col":
        full, out_idx = (1, r, N_CHIP * n), (lambda i, s: (0, i, s[0]))
    else:
        full, out_idx = (1, N_CHIP * r, n), (lambda i, s: (0, s[0] * bpr + i, 0))
    return pl.pallas_call(
        body, name=name,
        grid_spec=pltpu.PrefetchScalarGridSpec(
            num_scalar_prefetch=1, grid=(bpr,),
            in_specs=[pl.BlockSpec((None, tr, n), lambda i, s: (layer, i, 0)), pl.BlockSpec(memory_space=pl.ANY)],
            out_specs=pl.BlockSpec((None, tr, n), out_idx)),
        out_shape=jax.ShapeDtypeStruct(full, BF16),
        compiler_params=_params("parallel"),
    )(jnp.reshape(pos, (1,)).astype(jnp.int32), w, after)


HBM_SPEC = pl.BlockSpec(memory_space=pltpu.HBM)
SEM_SPEC = pl.BlockSpec(memory_space=pltpu.SEMAPHORE)
ANY_SPEC = pl.BlockSpec(memory_space=pl.ANY)
SPLIT_PARAMS = dict(has_side_effects=pltpu.SideEffectType.DATAFLOW_SIDE_EFFECTING)


def _in_hbm(a):
    return pltpu.with_memory_space_constraint(a, pltpu.HBM)


def _split_start(copies_of, bufs, n_sem, after, name):
    n = len(bufs)

    def body(*refs):
        ins, send, recv, token = refs[:n], refs[n + 1], refs[n + 2], refs[2 * n + 3]
        for cp in copies_of(ins, send, recv, False)[0]:
            cp.start()
        token[...] = jnp.zeros_like(token)

    outs = pl.pallas_call(
        body, name=name,
        out_shape=(pltpu.SemaphoreType.DMA(n_sem), pltpu.SemaphoreType.DMA(n_sem),
                   *[pltpu.HBM(b.shape, b.dtype) for b in bufs], jax.ShapeDtypeStruct((8, 128), F32)),
        in_specs=[HBM_SPEC] * n + [ANY_SPEC],
        out_specs=(SEM_SPEC, SEM_SPEC, *[HBM_SPEC] * n, pl.BlockSpec(memory_space=pltpu.VMEM)),
        input_output_aliases={t: 2 + t for t in range(n)},
        compiler_params=pltpu.CompilerParams(**SPLIT_PARAMS),
    )(*[_in_hbm(b) for b in bufs], after)
    return outs[0], outs[1], list(outs[2:2 + n]), outs[2 + n]


def _split_wait(copies_of, send, recv, bufs, after, name):
    n = len(bufs)
    after = list(after) if isinstance(after, (list, tuple)) else [after]

    def body(*refs):
        ins, send_ref, recv_ref = refs[:n], refs[n], refs[n + 1]
        sends, arrivals = copies_of(ins, send_ref, recv_ref, True)
        for cp in sends:
            cp.wait_send()
        for cp in arrivals:
            cp.wait_recv()

    return pl.pallas_call(
        body, name=name,
        out_shape=[pltpu.HBM(b.shape, b.dtype) for b in bufs],
        in_specs=[HBM_SPEC] * n + [SEM_SPEC, SEM_SPEC] + [ANY_SPEC] * len(after),
        out_specs=[HBM_SPEC] * n,
        input_output_aliases={t: t for t in range(n)},
        compiler_params=pltpu.CompilerParams(**SPLIT_PARAMS),
    )(*bufs, send, recv, *after)


def _gather_copies(items):
    def copies_of(refs, send, recv, with_arrivals):
        x, y, c, chips = _place()
        me = _chip_index(x, y)
        sends, arrivals = [], []
        for t, it in enumerate(items):
            for k, chip in enumerate(chips):
                for core in range(2):
                    mine = _block(it.ref(refs), it, me, c)
                    sends.append(pltpu.make_async_remote_copy(
                        src_ref=mine, dst_ref=mine, send_sem=send.at[6 * t + 2 * k + core],
                        recv_sem=recv.at[6 * t + 2 * k + c], device_id=(*chip, core), device_id_type=MESH))
                    if with_arrivals:
                        landed = _block(it.ref(refs), it, _chip_index(*chip), core)
                        arrivals.append(pltpu.make_async_remote_copy(
                            src_ref=landed, dst_ref=landed, send_sem=send.at[6 * t + 2 * k + core],
                            recv_sem=recv.at[6 * t + 2 * k + core], device_id=(*chip, core), device_id_type=MESH))
        return sends, arrivals

    return copies_of


def _owner_copies(items):
    n = len(items)

    def blk(ref, it, j):
        if it.kind == "col":
            ns = it.cols // N_CHIP
            return ref.at[:, pl.ds(it.pos(j) * ns, ns)]
        return ref.at[j]

    def copies_of(refs, send, recv, with_arrivals):
        x, y, c, chips = _place()
        sends, arrivals = [], []
        for t, it in enumerate(items):
            for k, chip in enumerate(chips):
                slot = refs[n + t].at[k]
                sends.append(pltpu.make_async_remote_copy(
                    src_ref=blk(refs[t], it, _chip_index(*chip)), dst_ref=slot, send_sem=send.at[3 * t + k],
                    recv_sem=recv.at[3 * t + k], device_id=(*chip, c), device_id_type=MESH))
                if with_arrivals:
                    arrivals.append(pltpu.make_async_remote_copy(
                        src_ref=slot, dst_ref=slot, send_sem=send.at[3 * t + k], recv_sem=recv.at[3 * t + k],
                        device_id=(*chip, c), device_id_type=MESH))
        return sends, arrivals

    return copies_of


def _owner_slot_shape(it):
    if it.kind == "col":
        return (3, it.rows // 2, it.cols // N_CHIP)
    return (3, it.rows // (2 * N_CHIP), it.cols)


def _pair_view(g, it):
    if it.kind == "col":
        return g.reshape(1, 2, it.rows // 2, it.cols)
    return g.reshape(N_CHIP, 2, it.rows // (2 * N_CHIP), it.cols)


def _pair_copies(n):
    def copies_of(refs, send, recv, with_arrivals):
        x, y, c, _ = _place()
        sends, arrivals = [], []
        for t in range(n):
            land = refs[n + t]
            sends.append(pltpu.make_async_remote_copy(
                src_ref=refs[t].at[:, pl.ds(1 - c, 1)], dst_ref=land, send_sem=send.at[t], recv_sem=recv.at[t],
                device_id=(x, y, 1 - c), device_id_type=MESH))
            if with_arrivals:
                arrivals.append(pltpu.make_async_remote_copy(
                    src_ref=land, dst_ref=land, send_sem=send.at[t], recv_sem=recv.at[t],
                    device_id=(x, y, 1 - c), device_id_type=MESH))
        return sends, arrivals

    return copies_of


def _half_copies(n):
    def half(ref, which):
        r2 = ref.shape[-2] // 2
        rows = pl.ds(which * r2, r2)
        return ref.at[rows, :] if len(ref.shape) == 2 else ref.at[:, rows, :]

    def copies_of(refs, send, recv, with_arrivals):
        x, y, c, _ = _place()
        sends, arrivals = [], []
        for t in range(n):
            mine = half(refs[t], c)
            sends.append(pltpu.make_async_remote_copy(
                src_ref=mine, dst_ref=mine, send_sem=send.at[t], recv_sem=recv.at[t],
                device_id=(x, y, 1 - c), device_id_type=MESH))
            if with_arrivals:
                theirs = half(refs[t], 1 - c)
                arrivals.append(pltpu.make_async_remote_copy(
                    src_ref=theirs, dst_ref=theirs, send_sem=send.at[t], recv_sem=recv.at[t],
                    device_id=(x, y, 1 - c), device_id_type=MESH))
        return sends, arrivals

    return copies_of


class _Reduction:
    pass


def _pair_start(grads, items, after, tag, names, layer=None):
    n = len(items)
    views = [_pair_view(g, it) for g, it in zip(grads, items)]
    lands = [lax.empty((v.shape[0], 1) + v.shape[2:], v.dtype) for v in views]
    r = _Reduction()
    r.items, r.tag, r.names, r.layer = items, tag, names, layer
    r.send, r.recv, r.bufs, r.token = _split_start(_pair_copies(n), views + lands, (n,), after, f"rs_pair_start_{tag}")
    return r


def _owner_start(r, after):
    x, y, c, _ = _place()
    n = len(r.items)
    bufs = _split_wait(_pair_copies(n), r.send, r.recv, r.bufs, after, f"rs_pair_wait_{r.tag}")
    pairs = [_pair_sum(bufs[t], bufs[n + t], c, f"rs_pair_sum_{r.tag}_{t}") for t in range(n)]
    shaped = [p if it.kind == "col" else p.reshape(N_CHIP, p.shape[0] // N_CHIP, p.shape[1])
              for p, it in zip(pairs, r.items)]
    lands = [lax.empty(_owner_slot_shape(it), BF16) for it in r.items]
    r.send, r.recv, r.bufs, r.token = _split_start(
        _owner_copies(r.items), shaped + lands, (3 * n,), r.token, f"rs_owner_start_{r.tag}")
    return r


def _reduce_finish(groups, after):
    x, y, c, _ = _place()
    me = _chip_index(x, y)
    halves = {}
    behind = [after]
    for r in groups:
        n = len(r.items)
        bufs = _split_wait(_owner_copies(r.items), r.send, r.recv, r.bufs, behind, f"rs_owner_wait_{r.tag}")
        for t, (it, nm) in enumerate(zip(r.items, r.names)):
            pair = bufs[t].reshape(-1, bufs[t].shape[-1])
            halves[nm] = _owner_sum(pair, bufs[n + t], me, c, it, f"rs_owner_sum_{r.tag}_{t}",
                                    layer=r.layer, into=halves.get(nm))
        behind = [after] + [halves[nm] for nm in r.names]
    n = len(halves)
    return list(halves), _split_start(_half_copies(n), list(halves.values()), (n,), after, "rs_half_start")


def _silu(v):
    return v * jax.nn.sigmoid(v)


def _sum8(p):
    return jnp.sum(p, axis=-2)


def kernel(x, c, mod_w, mod_b, norm_g, ffn_w_in, ffn_w_out, conv_w_in, conv_k, conv_w_out, kv_mod_w, kv_mod_b, kv_norm_g, w_kv, attn_w_q, attn_w_o, rel_bias, loss_target, m_mod_w, m_mod_b, m_norm_g, m_ffn_w_in, m_ffn_w_out, m_conv_w_in, m_conv_k, m_conv_w_out, m_kv_mod_w, m_kv_mod_b, m_kv_norm_g, m_w_kv, m_attn_w_q, m_attn_w_o, m_rel_bias, v_mod_w, v_mod_b, v_norm_g, v_ffn_w_in, v_ffn_w_out, v_conv_w_in, v_conv_k, v_conv_w_out, v_kv_mod_w, v_kv_mod_b, v_kv_norm_g, v_w_kv, v_attn_w_q, v_attn_w_o, v_rel_bias):
    xi, yi, ci = lax.axis_index("x"), lax.axis_index("y"), lax.axis_index("c")
    chip = 2 * xi + yi
    dev = 2 * chip + ci
    _, S, D = x.shape
    F = ffn_w_out.shape[1] * N_CHIP
    x0 = x.reshape(S, D)
    target = loss_target.reshape(S, D)
    n_mod = mod_w.shape[2]
    n_kvm = kv_mod_w.shape[1]
    dsh = D // N_CHIP
    TF = F // 2

    c_all = _all_gather_small(c.reshape(8, D // 8), "ag_c").reshape(N_DEV, D)
    sc16 = jnp.pad(_silu(c_all), ((0, 8), (0, 0)))
    part = [_mm(sc16, mod_w, "nn", F32, f"mod_fwd_{l}", b_layer=l)[:8] for l in range(2)]
    part.append(_mm(sc16, kv_mod_w, "nn", F32, "mod_fwd_kv")[:8])
    fwd_vec = jnp.concatenate([p.reshape(-1) for p in part] + [norm_g.reshape(-1), conv_k.reshape(-1)])
    fwd_all = _gather_flat(fwd_vec, "ag_fwd_small")[0::2]
    o = 0
    mods = []
    for n in (n_mod, n_mod, n_kvm):
        blk = fwd_all[:, o:o + 8 * n].reshape(N_CHIP, 8, n)
        mods.append(lax.dynamic_index_in_dim(blk, dev, axis=1, keepdims=False).reshape(N_CHIP * n))
        o += 8 * n
    ng = fwd_all[:, o:o + 8 * dsh].reshape(N_CHIP, 2, 4, dsh).transpose(1, 2, 0, 3).reshape(2, 4, D)
    o += 8 * dsh
    ck = fwd_all[:, o:o + 3 * dsh].reshape(N_CHIP, 3, dsh).transpose(1, 0, 2).reshape(3, D)
    ck8 = jnp.pad(ck, ((0, 5), (0, 0)))
    mod = [mods[l] + mod_b[l] for l in range(2)]
    sh1, sc1, g1, sh2, sc2, g2 = zip(*[jnp.split(m, 6) for m in mod])
    kv_sh, kv_sc = jnp.split(mods[2] + kv_mod_b, 2)
    row = lambda v: v.reshape(1, D)

    it_conv = [_Item("col", D, 3 * D, 0, 0), _Item("row", D, D, 1, 0)]
    it_ffn = [_Item("col", D, 2 * F, 0, 0, swap=True), _Item("row", F, D, 1, 0)]
    it_attn = [_Item("col", D, 2 * D, 0, 0, swap=True), _Item("row", D, D, 1, 0), _Item("row", D, D, 2, 0)]

    def placed(w, layer, it, nm, after=fwd_all):
        return _cast_place(w, layer, it.kind, it.pos(chip), after, f"place_{nm}")

    flying = {}

    def start(tag, its, bufs, after):
        send, recv, bufs, tok = _split_start(_gather_copies(its), bufs, (6 * len(its),), after, f"ag_start_{tag}")
        flying[tag] = (its, send, recv, bufs)
        return tok

    def arrived(tag, after):
        its, send, recv, bufs = flying[tag]
        return _split_wait(_gather_copies(its), send, recv, bufs, after, f"ag_wait_{tag}")

    one = lambda it: [_Item(it.kind, it.rows, it.cols, 0, 0, it.swap)]
    tok = start("conv_in", one(it_conv[0]), [placed(conv_w_in, 0, it_conv[0], "conv_w_in")], fwd_all)
    tok = start("conv_out", one(it_conv[1]), [placed(conv_w_out, 0, it_conv[1], "conv_w_out", tok)], tok)
    tok = start("ffn0_in", one(it_ffn[0]), [placed(ffn_w_in, 0, it_ffn[0], "ffn_w_in0", tok)], tok)
    tok = start("ffn0_out", one(it_ffn[1]), [placed(ffn_w_out, 0, it_ffn[1], "ffn_w_out0", tok)], tok)
    tok = start("attn", it_attn, [placed(w_kv[None], 0, it_attn[0], "w_kv", tok),
                                  placed(attn_w_q, 0, it_attn[1], "attn_w_q", tok),
                                  placed(attn_w_o, 0, it_attn[2], "attn_w_o", tok)], tok)
    token = start("ffn1", it_ffn, [placed(ffn_w_in, 1, it_ffn[0], "ffn_w_in1", tok),
                                   placed(ffn_w_out, 1, it_ffn[1], "ffn_w_out1", tok)], tok)

    a1 = row(ng[0, 0] * (1.0 + sc1[0])) + token[0, 0]
    tab = _bias_table(rel_bias[0] + token[0, 0], "l1_bias_table")
    (W_cin,) = arrived("conv_in", tab)
    h1, bcx, ug = _conv_in_gate(x0, a1, row(sh1[0]), W_cin, ck8, "l0_norm_conv_in_gate")
    gt1 = row(g1[0] * ng[0, 1])
    a2 = row(ng[0, 2] * (1.0 + sc2[0]))
    (W_cout,) = arrived("conv_out", ug)
    y1, x1, h2 = _mm_post(ug, W_cout, x0, gt1, "l0_conv_out", scales=a2, shifts=row(sh2[0]))
    (W_fin0,) = arrived("ffn0_in", h2)
    gu0, act0 = _ffn_in_act(h2, W_fin0, 0, "l0_ffn_in")
    (W_fout0,) = arrived("ffn0_out", act0)
    gt2 = row(g2[0] * ng[0, 3])
    a3 = ng[1, 0] * (1.0 + sc1[1])
    akv = kv_norm_g * (1.0 + kv_sc)
    y2, x2, h3, hkv = _mm_post(act0, W_fout0, x1, gt2, "l0_ffn_out",
                               scales=jnp.stack([a3, akv]), shifts=jnp.stack([sh1[1], kv_sh]))
    W_kv, W_q, W_o = arrived("attn", hkv)
    kvp = _mm(hkv, W_kv, "nn", BF16, "l1_kv", b_layer=0, tm=512, tn=2 * D)
    att_scale = (D // N_HEADS) ** -0.5
    assert math.log2(att_scale) % 1 == 0, "scaling q before its bf16 cast is exact only for a power of two"
    qp = _mm(h3, W_q, "nn", BF16, "l1_q", b_layer=0, scale=att_scale)
    oh = _attn_fwd(qp, kvp, tab, "l1_attn")
    gt3 = row(g1[1] * ng[1, 1])
    a4 = row(ng[1, 2] * (1.0 + sc2[1]))
    y3, x3, h4 = _mm_post(oh, W_o, x2, gt3, "l1_attn_out", scales=a4, shifts=row(sh2[1]))
    W_fin1, W_fout1 = arrived("ffn1", h4)
    gu1, act1 = _ffn_in_act(h4, W_fin1, 0, "l1_ffn_in")
    gt4 = row(g2[1] * ng[1, 3])
    dx4, sq, dy4, dgt4 = _mm_post(act1, W_fout1, x3, gt4, "l1_ffn_out", target=target)
    loss_part = 0.5 * jnp.sum(sq) / D

    def ffn_bwd(dy, dxn, xin_, h, gu, act, a, w_in, w_out, post, tag):
        dgu, dx, ds, db, dyn, dgt = _ffn_bwd(dy, w_out, gu, w_in, xin_, dxn, a, post, f"{tag}_ffn_bwd")
        g_fout = _mm(act, dy, "tn", BF16, f"{tag}_ffn_out_dw", tm=TF)
        g_fin = _mm(h, dgu, "tn", BF16, f"{tag}_ffn_in_dw", tn=TF)
        return dx, ds, db, dyn, dgt, g_fin, g_fout

    dx3, ds4, db4, dy3, dgt3, G_fin1, G_fout1 = ffn_bwd(dy4, dx4, x3, h4, gu1, act1, a4, W_fin1, W_fout1,
                                                        (y3, gt3), "l1")
    red = [_pair_start([G_fin1, G_fout1], it_ffn, token, "ffn1", ["ffn_w_in", "ffn_w_out"], layer=1)]
    doh = _mm(dy3, W_o, "nt", BF16, "l1_attn_out_dx", b_layer=0, after=red[0].token)
    G_o = _mm(oh, dy3, "tn", BF16, "l1_attn_out_dw")
    _owner_start(red[0], G_o)
    dq, dkv, dtab = _attn_bwd(qp, kvp, tab, doh, "l1_attn_bwd")
    d_rel = _bias_table_grad(dtab)
    G_q = _mm(h3, dq, "tn", BF16, "l1_q_dw")
    G_kv = _mm(hkv, dkv, "tn", BF16, "l1_kv_dw")
    red.append(_pair_start([G_kv, G_q, G_o], it_attn, red[-1].token, "attn", ["w_kv", "attn_w_q", "attn_w_o"]))
    dx2, ds3, db3, dy2, dgt2 = _mm_pre_bwd([(dq, W_q), (dkv, W_kv)], x2, dx3,
                                           jnp.stack([a3, akv]) + red[1].token[0, 0], "l1_qkv_dx", post=(y2, gt2))
    _owner_start(red[1], dx2)

    dx1, ds2, db2, dy1, dgt1, G_fin0, G_fout0 = ffn_bwd(dy2, dx2, x1, h2, gu0, act0, a2, W_fin0, W_fout0,
                                                        (y1, gt1), "l0")
    red.append(_pair_start([G_fin0, G_fout0], it_ffn, red[-1].token, "ffn0", ["ffn_w_in", "ffn_w_out"], layer=0))
    dug = _mm(dy1, W_cout, "nt", BF16, "l0_conv_out_dx", b_layer=0, after=red[2].token)
    G_cout = _mm(ug, dy1, "tn", BF16, "l0_conv_out_dw")
    dbcx, dck = _conv_gate_bwd(dug, bcx, ck8, "l0_conv_gate_bwd")
    _owner_start(red[2], dbcx)
    G_cin = _mm(h1, dbcx, "tn", BF16, "l0_conv_in_dw")
    red.append(_pair_start([G_cin, G_cout], it_conv, red[-1].token, "conv", ["conv_w_in", "conv_w_out"]))
    dx0, ds1, db1 = _mm_pre_bwd([(dbcx, W_cin)], x0, dx1, a1 + red[3].token[0, 0], "l0_conv_in_dx")
    ds1, db1 = _sum8(ds1)[0], _sum8(db1)[0]
    da2, db2 = _sum8(ds2)[0], _sum8(db2)[0]
    ds3, db3 = _sum8(ds3), _sum8(db3)
    da4, db4 = _sum8(ds4)[0], _sum8(db4)[0]
    dgt1, dgt2, dgt3, dgt4 = _sum8(dgt1), _sum8(dgt2), _sum8(dgt3), _sum8(dgt4)

    def dmod_of(l, ds_a, db_a, dgt_a, ds_b, db_b, dgt_b):
        return jnp.concatenate([db_a, ds_a * ng[l, 0], dgt_a * ng[l, 1], db_b, ds_b * ng[l, 2], dgt_b * ng[l, 3]])

    dmod0 = dmod_of(0, ds1, db1, dgt1, da2, db2, dgt2)
    dmod1 = dmod_of(1, ds3[0], db3[0], dgt3, da4, db4, dgt4)
    dkvmod = jnp.concatenate([db3[1], ds3[1] * kv_norm_g])
    dng = jnp.stack([
        jnp.stack([ds1 * (1.0 + sc1[0]), dgt1 * g1[0], da2 * (1.0 + sc2[0]), dgt2 * g2[0]]),
        jnp.stack([ds3[0] * (1.0 + sc1[1]), dgt3 * g1[1], da4 * (1.0 + sc2[1]), dgt4 * g2[1]])])
    dkvng = ds3[1] * (1.0 + kv_sc)
    small = [dmod0, dmod1, dkvmod, dng.reshape(-1), dkvng, _sum8(dck).reshape(-1), d_rel.reshape(-1),
             loss_part.reshape(1)]
    sizes = [int(s.shape[0]) for s in small]
    offs = np.concatenate([[0], np.cumsum(sizes)])
    bwd_all = _gather_flat(jnp.concatenate(small), "ag_bwd_small")
    _owner_start(red[3], bwd_all)
    Lb = bwd_all.shape[1]
    Lp = -(-Lb // 128) * 128
    tot = _sum_rows(jnp.pad(bwd_all, ((0, 0), (0, Lp - Lb))), "sum_small")[0]
    seg = lambda i: tot[offs[i]:offs[i + 1]]
    g_mod_b = jnp.stack([seg(0), seg(1)])
    g_kv_mod_b = seg(2)
    g_norm_g = lax.dynamic_slice_in_dim(seg(3).reshape(2, 4, D), chip * dsh, dsh, axis=2)
    g_kv_norm_g = seg(4)
    g_conv_k = lax.dynamic_slice_in_dim(seg(5).reshape(1, 3, D), chip * dsh, dsh, axis=2)
    g_rel_bias = seg(6).reshape(rel_bias.shape)
    loss = seg(7)[0]

    def dmod_rows(i, n):
        rows_ = lax.dynamic_slice_in_dim(bwd_all[:, offs[i]:offs[i + 1]], chip * n, n, axis=1)
        return jnp.pad(rows_, ((0, 8), (0, 0)))

    g_mod_w = _mm(sc16, jnp.concatenate([dmod_rows(0, n_mod), dmod_rows(1, n_mod)], axis=1), "tn", F32,
                  "mod_bwd", out_layers=STACKED_LAYERS)
    g_kv_mod_w = _mm(sc16, dmod_rows(2, n_kvm), "tn", F32, "mod_bwd_kv")

    grads = {
        "mod_w": g_mod_w, "mod_b": g_mod_b, "norm_g": g_norm_g, "conv_k": g_conv_k,
        "kv_mod_w": g_kv_mod_w, "kv_mod_b": g_kv_mod_b, "kv_norm_g": g_kv_norm_g, "rel_bias": g_rel_bias,
    }
    weights = dict(mod_w=mod_w, mod_b=mod_b, norm_g=norm_g, ffn_w_in=ffn_w_in, ffn_w_out=ffn_w_out,
                   conv_w_in=conv_w_in, conv_k=conv_k, conv_w_out=conv_w_out, kv_mod_w=kv_mod_w,
                   kv_mod_b=kv_mod_b, kv_norm_g=kv_norm_g, w_kv=w_kv, attn_w_q=attn_w_q, attn_w_o=attn_w_o,
                   rel_bias=rel_bias)
    m_in = dict(mod_w=m_mod_w, mod_b=m_mod_b, norm_g=m_norm_g, ffn_w_in=m_ffn_w_in, ffn_w_out=m_ffn_w_out,
                conv_w_in=m_conv_w_in, conv_k=m_conv_k, conv_w_out=m_conv_w_out, kv_mod_w=m_kv_mod_w,
                kv_mod_b=m_kv_mod_b, kv_norm_g=m_kv_norm_g, w_kv=m_w_kv, attn_w_q=m_attn_w_q,
                attn_w_o=m_attn_w_o, rel_bias=m_rel_bias)
    v_in = dict(mod_w=v_mod_w, mod_b=v_mod_b, norm_g=v_norm_g, ffn_w_in=v_ffn_w_in, ffn_w_out=v_ffn_w_out,
                conv_w_in=v_conv_w_in, conv_k=v_conv_k, conv_w_out=v_conv_w_out, kv_mod_w=v_kv_mod_w,
                kv_mod_b=v_kv_mod_b, kv_norm_g=v_kv_norm_g, w_kv=v_w_kv, attn_w_q=v_attn_w_q,
                attn_w_o=v_attn_w_o, rel_bias=v_rel_bias)
    names = list(weights)
    step = {}

    def update(n, echo=False):
        g = grads[n].reshape(weights[n].shape)
        outs = _adamw(weights[n], g, m_in[n], v_in[n], f"adamw_{n}", echo=echo)
        step[n] = outs if echo else (g, *outs)

    update("mod_w")
    reduced, (half_send, half_recv, half_bufs, _) = _reduce_finish(red, step["mod_w"][1])
    local = [n for n in grads if n != "mod_w"]
    for n in local:
        update(n)
    grads.update(zip(reduced, _split_wait(
        _half_copies(len(half_bufs)), half_send, half_recv, half_bufs, [step[n][1] for n in local], "rs_half_wait")))
    for n in reduced:
        update(n, echo=True)
    return (loss, dx0.reshape(x.shape), *[step[n][k] for k in range(4) for n in names])
```

```python
import functools
import math

import numpy as np
import jax
import jax.numpy as jnp
from jax import lax
from jax.experimental import pallas as pl
from jax.experimental.pallas import tpu as pltpu

CHUNK = 64
N_LEFT_CHUNKS = 8
N_HEADS = 16
MAX_REL = 2 * CHUNK
N_REL = 2 * MAX_REL + 1
EPS = 1e-6
ADAM_LR = 0.001
ADAM_B1 = 0.9
ADAM_B2 = 0.999
ADAM_EPS = 1e-08
ADAM_WD = 0.01
ADAM_STEP = 10

Q_CHUNKS = 4
BQ = Q_CHUNKS * CHUNK
N_WIN = 1 + N_LEFT_CHUNKS // Q_CHUNKS
HEADS_PER_STEP = 8
NEG = -1e30
N_DEV = 8
N_CHIP = 4
SMALL_TENSOR_ELEMS = 1 << 16
PIECE_ROWS = 256

BF16 = jnp.bfloat16
F32 = jnp.float32
V7X_VMEM_LIMIT_BYTES = 56 * 1024 * 1024
MESH = pl.DeviceIdType.MESH


def _pick(n, pref, align):
    t = min(pref, n)
    t -= t % align
    while t >= align:
        if n % t == 0:
            return t
        t -= align
    return n


def _params(*sem):
    return pltpu.CompilerParams(dimension_semantics=sem, vmem_limit_bytes=V7X_VMEM_LIMIT_BYTES)


def _colsum8(v):
    r, d = v.shape
    return v.reshape(r // 8, 8, d).sum(axis=0)


_DIMS = {"nn": (((1,), (0,)), ((), ())), "nt": (((1,), (1,)), ((), ())), "tn": (((0,), (0,)), ((), ()))}


def _mm(a, b, mode, out_dtype, name, *, b_layer=None, tm=1024, tn=1024, tk=None, scale=None, after=None,
        out_layers=1):
    if tk is None:
        tk = 2048 if mode == "tn" else 3072
    bs = b.shape[1:] if b_layer is not None else b.shape
    if mode == "nn":
        (M, K), (K2, N) = a.shape, bs
    elif mode == "nt":
        (M, K), (N, K2) = a.shape, bs
    else:
        (K, M), (K2, N) = a.shape, bs
    assert K == K2, (name, a.shape, b.shape)
    tm = _pick(M, tm, 128 if mode == "tn" else 16)
    tn = _pick(N // out_layers, tn, 128)
    tk = _pick(K, tk, 128 if mode != "tn" else 16)
    nk = K // tk
    assert scale is None or nk == 1, name
    dims = _DIMS[mode]
    extra = [] if after is None else [after]

    def body(a_ref, b_ref, *rest):
        o_ref, acc = rest[len(extra)], rest[len(extra) + 1:]
        p = lax.dot_general(a_ref[...].astype(BF16), b_ref[...].astype(BF16), dims,
                            preferred_element_type=F32)
        if nk == 1:
            o_ref[...] = (p if scale is None else p * scale).astype(o_ref.dtype)
        else:
            k = pl.program_id(2)

            @pl.when(k == 0)
            def _():
                acc[0][...] = p

            @pl.when(k > 0)
            def _():
                acc[0][...] += p

            @pl.when(k == nk - 1)
            def _():
                o_ref[...] = acc[0][...].astype(o_ref.dtype)

    a_spec = (pl.BlockSpec((tk, tm), lambda i, j, k: (k, i)) if mode == "tn"
              else pl.BlockSpec((tm, tk), lambda i, j, k: (i, k)))
    if mode == "nt":
        b_blk, b_idx = (tn, tk), (lambda i, j, k: (j, k))
    else:
        b_blk, b_idx = (tk, tn), (lambda i, j, k: (k, j))
    if b_layer is not None:
        b_spec = pl.BlockSpec((None,) + b_blk, lambda i, j, k: (b_layer,) + b_idx(i, j, k))
    else:
        b_spec = pl.BlockSpec(b_blk, b_idx)
    if out_layers > 1:
        per_layer = N // out_layers // tn
        o_spec = pl.BlockSpec((None, tm, tn), lambda i, j, k: (j // per_layer, i, j % per_layer))
        o_shape = (out_layers, M, N // out_layers)
    else:
        o_spec, o_shape = pl.BlockSpec((tm, tn), lambda i, j, k: (i, j)), (M, N)
    return pl.pallas_call(
        body, name=name,
        grid=(M // tm, N // tn, nk),
        in_specs=[a_spec, b_spec] + [pl.BlockSpec(memory_space=pl.ANY)] * len(extra),
        out_specs=o_spec,
        out_shape=jax.ShapeDtypeStruct(o_shape, out_dtype),
        scratch_shapes=[pltpu.VMEM((tm, tn), F32)] if nk > 1 else [],
        compiler_params=_params("parallel", "parallel", "arbitrary"),
    )(a, b, *extra)


def _row_spec(tm, d):
    return pl.BlockSpec((tm, d), lambda i: (i, 0))


def _vec_spec(r, d):
    return pl.BlockSpec((r, d), lambda i: (0, 0))


def _norm_mod(x, scales, shifts, name):
    S, D = x.shape
    nb = scales.shape[0]
    tm = _pick(S, 1024, 16)

    def body(x_ref, a_ref, b_ref, *o_refs):
        xv = x_ref[...]
        xh = xv * lax.rsqrt(jnp.mean(xv * xv, axis=-1, keepdims=True) + EPS)
        for n in range(nb):
            o_refs[n][...] = (xh * a_ref[n:n + 1, :] + b_ref[n:n + 1, :]).astype(BF16)

    return pl.pallas_call(
        body, name=name, grid=(S // tm,),
        in_specs=[_row_spec(tm, D), _vec_spec(nb, D), _vec_spec(nb, D)],
        out_specs=[_row_spec(tm, D)] * nb,
        out_shape=[jax.ShapeDtypeStruct((S, D), BF16)] * nb,
        compiler_params=_params("parallel"),
    )(x, scales, shifts)


def _mm_post(a, w, x, gate, name, *, scales=None, shifts=None, target=None):
    M, K = a.shape
    D = w.shape[2]
    tm = _pick(M, 1024 if K <= D else 512, 16)
    sub = _pick(tm, PIECE_ROWS, 16)
    nb = 0 if scales is None else scales.shape[0]

    def body(a_ref, w_ref, x_ref, g_ref, *rest):
        if target is None:
            sc_ref, sh_ref, y_ref, xn_ref = rest[:4]
            h_refs = rest[4:]
        else:
            t_ref, dx_ref, sq_ref, dy_ref, dg_ref = rest

            @pl.when(pl.program_id(0) == 0)
            def _():
                sq_ref[...] = jnp.zeros_like(sq_ref)
                dg_ref[...] = jnp.zeros_like(dg_ref)

        def product(r):
            return jnp.dot(a_ref[pl.ds(r * sub, sub), :], w_ref[...], preferred_element_type=F32)

        y = product(0)
        for r in range(tm // sub):
            rows = pl.ds(r * sub, sub)
            yb = y.astype(BF16)
            if r + 1 < tm // sub:
                y = product(r + 1)
            yv = yb.astype(F32)
            yh = yv * lax.rsqrt(jnp.mean(yv * yv, axis=-1, keepdims=True) + EPS)
            xn = x_ref[rows, :] + yh * g_ref[...]
            if target is None:
                y_ref[rows, :] = yb
                xn_ref[rows, :] = xn
                xh = xn * lax.rsqrt(jnp.mean(xn * xn, axis=-1, keepdims=True) + EPS)
                for n in range(nb):
                    h_refs[n][rows, :] = (xh * sc_ref[n:n + 1, :] + sh_ref[n:n + 1, :]).astype(BF16)
            else:
                e = xn - t_ref[rows, :]
                dx = e / D
                dx_ref[rows, :] = dx
                sq_ref[...] += _colsum8(e * e)
                dy, dxy = _post_norm_grad(dx, yb, g_ref[...])
                dy_ref[rows, :] = dy.astype(BF16)
                dg_ref[...] += _colsum8(dxy)

    ins = [a, w, x, gate]
    in_specs = [_row_spec(tm, K), pl.BlockSpec((None, K, D), lambda i: (0, 0, 0)), _row_spec(tm, D), _vec_spec(1, D)]
    if target is None:
        ins += [scales, shifts]
        in_specs += [_vec_spec(nb, D), _vec_spec(nb, D)]
        out_specs = [_row_spec(tm, D)] * (2 + nb)
        out_shape = [jax.ShapeDtypeStruct((M, D), BF16), jax.ShapeDtypeStruct((M, D), F32)] \
            + [jax.ShapeDtypeStruct((M, D), BF16)] * nb
    else:
        ins += [target]
        in_specs += [_row_spec(tm, D)]
        out_specs = [_row_spec(tm, D), _vec_spec(8, D), _row_spec(tm, D), _vec_spec(8, D)]
        out_shape = [jax.ShapeDtypeStruct((M, D), F32), jax.ShapeDtypeStruct((8, D), F32),
                     jax.ShapeDtypeStruct((M, D), BF16), jax.ShapeDtypeStruct((8, D), F32)]
    return pl.pallas_call(
        body, name=name, grid=(M // tm,), in_specs=in_specs, out_specs=out_specs, out_shape=out_shape,
        compiler_params=_params("arbitrary" if target is not None else "parallel"),
    )(*ins)


def _post_norm_grad(dxn, yb, gate):
    yv = yb.astype(F32)
    r = lax.rsqrt(jnp.mean(yv * yv, axis=-1, keepdims=True) + EPS)
    yh = yv * r
    dyh = dxn * gate
    return r * (dyh - yh * jnp.mean(dyh * yh, axis=-1, keepdims=True)), dxn * yh


def _mm_pre_bwd(pairs, x, dxn, scales, name, post=None):
    S, D = x.shape
    nb = len(pairs)
    tm = _pick(S, 512, 16)
    sub = _pick(tm, PIECE_ROWS, 16)

    def body(*refs):
        a_refs, w_refs = refs[0:2 * nb:2], refs[1:2 * nb:2]
        x_ref, d_ref, sc_ref = refs[2 * nb:2 * nb + 3]
        rest = refs[2 * nb + 3:]
        if post is not None:
            y_ref, g_ref, dx_ref, ds_ref, db_ref, dy_ref, dg_ref = rest
        else:
            dx_ref, ds_ref, db_ref = rest

        @pl.when(pl.program_id(0) == 0)
        def _():
            ds_ref[...] = jnp.zeros_like(ds_ref)
            db_ref[...] = jnp.zeros_like(db_ref)
            if post is not None:
                dg_ref[...] = jnp.zeros_like(dg_ref)

        def products(r):
            return [lax.dot_general(a_refs[n][pl.ds(r * sub, sub), :], w_refs[n][...], _DIMS["nt"],
                                    preferred_element_type=F32) for n in range(nb)]

        nxt = products(0)
        for r in range(tm // sub):
            rows = pl.ds(r * sub, sub)
            dhs = nxt
            if r + 1 < tm // sub:
                nxt = products(r + 1)
            xv = x_ref[rows, :]
            rr = lax.rsqrt(jnp.mean(xv * xv, axis=-1, keepdims=True) + EPS)
            xh = xv * rr
            dxh = jnp.zeros_like(xv)
            for n in range(nb):
                dh = dhs[n]
                dxh = dxh + dh * sc_ref[n:n + 1, :]
                ds_ref[n] += _colsum8(dh * xh)
                db_ref[n] += _colsum8(dh)
            dx = d_ref[rows, :] + rr * (dxh - xh * jnp.mean(dxh * xh, axis=-1, keepdims=True))
            dx_ref[rows, :] = dx
            if post is not None:
                dy, dxy = _post_norm_grad(dx, y_ref[rows, :], g_ref[...])
                dy_ref[rows, :] = dy.astype(BF16)
                dg_ref[...] += _colsum8(dxy)

    ins, in_specs = [], []
    for a, w in pairs:
        ins += [a, w]
        in_specs += [_row_spec(tm, a.shape[1]),
                     pl.BlockSpec((None, D, a.shape[1]), lambda i: (0, 0, 0), pipeline_mode=pl.Buffered(1))]
    ins += [x, dxn, scales]
    in_specs += [_row_spec(tm, D), _row_spec(tm, D), _vec_spec(nb, D)]
    acc_spec = pl.BlockSpec((nb, 8, D), lambda i: (0, 0, 0))
    out_specs = [_row_spec(tm, D), acc_spec, acc_spec]
    out_shape = [jax.ShapeDtypeStruct((S, D), F32), jax.ShapeDtypeStruct((nb, 8, D), F32),
                 jax.ShapeDtypeStruct((nb, 8, D), F32)]
    if post is not None:
        ins += list(post)
        in_specs += [_row_spec(tm, D), _vec_spec(1, D)]
        out_specs += [_row_spec(tm, D), _vec_spec(8, D)]
        out_shape += [jax.ShapeDtypeStruct((S, D), BF16), jax.ShapeDtypeStruct((8, D), F32)]
    return pl.pallas_call(
        body, name=name, grid=(S // tm,), in_specs=in_specs, out_specs=out_specs, out_shape=out_shape,
        compiler_params=_params("arbitrary"),
    )(*ins)


FFN_PAIRS = 2


def _ffn_in_act(h, w, layer, name):
    S, D = h.shape
    F2 = w.shape[2]
    PW = F2 // (2 * FFN_PAIRS)
    tm = _pick(S, 1024, 16)
    sub = _pick(tm, PIECE_ROWS, 16)

    def body(h_ref, w_ref, gu_ref, a_ref):
        def product(r):
            return jnp.dot(h_ref[pl.ds(r * sub, sub), :], w_ref[...], preferred_element_type=F32)

        nxt = product(0)
        for r in range(tm // sub):
            rows = pl.ds(r * sub, sub)
            acc = nxt
            if r + 1 < tm // sub:
                nxt = product(r + 1)
            gu_ref[rows, :] = acc.astype(BF16)
            g = acc[:, :PW]
            a_ref[rows, :] = (g * jax.nn.sigmoid(g) * acc[:, PW:]).astype(BF16)

    return pl.pallas_call(
        body, name=name, grid=(FFN_PAIRS, S // tm),
        in_specs=[pl.BlockSpec((tm, D), lambda p, i: (i, 0)),
                  pl.BlockSpec((None, D, 2 * PW), lambda p, i: (layer, 0, p))],
        out_specs=[pl.BlockSpec((tm, 2 * PW), lambda p, i: (i, p)), pl.BlockSpec((tm, PW), lambda p, i: (i, p))],
        out_shape=[jax.ShapeDtypeStruct((S, F2), BF16), jax.ShapeDtypeStruct((S, F2 // 2), BF16)],
        compiler_params=_params("parallel", "parallel"),
    )(h, w)


def _ffn_bwd(dy, w_out, gu, w_in, x, dxn, scale, post, name):
    S, D = dy.shape
    F2 = gu.shape[1]
    PW = F2 // (2 * FFN_PAIRS)
    tm = _pick(S, 256, 16)

    def body(dy_ref, wo_ref, gu_ref, wi_ref, x_ref, d_ref, sc_ref, y_ref, g_ref,
             dgu_ref, dx_ref, ds_ref, db_ref, dyn_ref, dg_ref):
        @pl.when(pl.program_id(0) == 0)
        def _():
            ds_ref[...] = jnp.zeros_like(ds_ref)
            db_ref[...] = jnp.zeros_like(db_ref)
            dg_ref[...] = jnp.zeros_like(dg_ref)

        def first_product(p):
            return lax.dot_general(dy_ref[...], wo_ref[p * PW:(p + 1) * PW, :], _DIMS["nt"],
                                   preferred_element_type=F32)

        dh = jnp.zeros((tm, D), F32)
        nxt = first_product(0)
        for p in range(FFN_PAIRS):
            cols = slice(2 * p * PW, 2 * (p + 1) * PW)
            da = nxt
            if p + 1 < FFN_PAIRS:
                nxt = first_product(p + 1)
            g = gu_ref[:, 2 * p * PW:(2 * p + 1) * PW].astype(F32)
            u = gu_ref[:, (2 * p + 1) * PW:2 * (p + 1) * PW].astype(F32)
            sg = jax.nn.sigmoid(g)
            dgu_ref[:, 2 * p * PW:(2 * p + 1) * PW] = (da * u * (sg * (1.0 + g * (1.0 - sg)))).astype(BF16)
            dgu_ref[:, (2 * p + 1) * PW:2 * (p + 1) * PW] = (da * (g * sg)).astype(BF16)
            dh = dh + lax.dot_general(dgu_ref[:, cols], wi_ref[:, cols], _DIMS["nt"], preferred_element_type=F32)
        xv = x_ref[...]
        rr = lax.rsqrt(jnp.mean(xv * xv, axis=-1, keepdims=True) + EPS)
        xh = xv * rr
        dxh = dh * sc_ref[...]
        ds_ref[0] += _colsum8(dh * xh)
        db_ref[0] += _colsum8(dh)
        dx = d_ref[...] + rr * (dxh - xh * jnp.mean(dxh * xh, axis=-1, keepdims=True))
        dx_ref[...] = dx
        dyn, dxy = _post_norm_grad(dx, y_ref[...], g_ref[...])
        dyn_ref[...] = dyn.astype(BF16)
        dg_ref[...] += _colsum8(dxy)

    resident = dict(pipeline_mode=pl.Buffered(1))
    acc_spec = pl.BlockSpec((1, 8, D), lambda i: (0, 0, 0))
    return pl.pallas_call(
        body, name=name, grid=(S // tm,),
        in_specs=[_row_spec(tm, D), pl.BlockSpec((None, F2 // 2, D), lambda i: (0, 0, 0), **resident),
                  _row_spec(tm, F2), pl.BlockSpec((None, D, F2), lambda i: (0, 0, 0), **resident),
                  _row_spec(tm, D), _row_spec(tm, D), _vec_spec(1, D), _row_spec(tm, D), _vec_spec(1, D)],
        out_specs=[_row_spec(tm, F2), _row_spec(tm, D), acc_spec, acc_spec, _row_spec(tm, D), _vec_spec(8, D)],
        out_shape=[jax.ShapeDtypeStruct((S, F2), BF16), jax.ShapeDtypeStruct((S, D), F32),
                   jax.ShapeDtypeStruct((1, 8, D), F32), jax.ShapeDtypeStruct((1, 8, D), F32),
                   jax.ShapeDtypeStruct((S, D), BF16), jax.ShapeDtypeStruct((8, D), F32)],
        compiler_params=_params("arbitrary"),
    )(dy, w_out, gu, w_in, x, dxn, scale, *post)


HALO = 16


def _conv_terms(bcx_ref, prev_ref, i, tm, D):
    b = bcx_ref[:, 0:D].astype(F32)
    cg = bcx_ref[:, D:2 * D].astype(F32)
    xin = bcx_ref[:, 2 * D:3 * D].astype(F32)
    z = cg * xin
    zp = prev_ref[:, D:2 * D].astype(F32) * prev_ref[:, 2 * D:3 * D].astype(F32)
    zp = jnp.where(i > 0, zp, 0.0)
    z_ext = jnp.concatenate([zp, z], axis=0)
    z1 = pltpu.roll(z_ext, 1, 0)[HALO:, :]
    z2 = pltpu.roll(z_ext, 2, 0)[HALO:, :]
    return b, cg, xin, z, z1, z2


def _conv_in_gate(x, scale, shift, w, ck, name):
    S, D = x.shape
    D3 = w.shape[-1]
    assert w.dtype == BF16, name
    tm = _pick(S, 512, 16)

    def body(x_ref, a_ref, s_ref, w_ref, ck_ref, h_ref, bcx_ref, o_ref, tail):
        @pl.when(pl.program_id(0) == 0)
        def _():
            tail[...] = jnp.zeros_like(tail)

        xv = x_ref[...]
        xh = xv * lax.rsqrt(jnp.mean(xv * xv, axis=-1, keepdims=True) + EPS)
        h_ref[...] = (xh * a_ref[...] + s_ref[...]).astype(BF16)
        bcx_ref[...] = lax.dot_general(h_ref[...], w_ref[...], _DIMS["nn"],
                                       preferred_element_type=F32).astype(BF16)
        b = bcx_ref[:, 0:D].astype(F32)
        z = bcx_ref[:, D:2 * D].astype(F32) * bcx_ref[:, 2 * D:3 * D].astype(F32)
        z_ext = jnp.concatenate([tail[...], z], axis=0)
        z1 = pltpu.roll(z_ext, 1, 0)[HALO:, :]
        z2 = pltpu.roll(z_ext, 2, 0)[HALO:, :]
        conv = ck_ref[0:1, :] * z2 + ck_ref[1:2, :] * z1 + ck_ref[2:3, :] * z
        o_ref[...] = (b * conv).astype(BF16)
        tail[...] = z[tm - HALO:, :]

    return pl.pallas_call(
        body, name=name, grid=(S // tm,),
        in_specs=[_row_spec(tm, D), _vec_spec(1, D), _vec_spec(1, D),
                  pl.BlockSpec((None, D, D3), lambda i: (0, 0, 0), pipeline_mode=pl.Buffered(1)),
                  _vec_spec(8, D)],
        out_specs=[_row_spec(tm, D), _row_spec(tm, D3), _row_spec(tm, D)],
        out_shape=[jax.ShapeDtypeStruct((S, D), BF16), jax.ShapeDtypeStruct((S, D3), BF16),
                   jax.ShapeDtypeStruct((S, D), BF16)],
        scratch_shapes=[pltpu.VMEM((HALO, D), F32)],
        compiler_params=_params("arbitrary"),
    )(x, scale, shift, w, ck)


def _conv_gate_bwd(du, bcx, ck, name):
    S, D3 = bcx.shape
    D = D3 // 3
    tm = _pick(S, 512, 16)
    hb = tm // HALO
    nt = S // tm

    def body(du_ref, dun_ref, bcx_ref, prev_ref, next_ref, ck_ref, o_ref, dk_ref):
        i = pl.program_id(0)
        b, cg, xin, z, z1, z2 = _conv_terms(bcx_ref, prev_ref, i, tm, D)
        k0, k1, k2 = ck_ref[0:1, :], ck_ref[1:2, :], ck_ref[2:3, :]
        conv = k0 * z2 + k1 * z1 + k2 * z
        d = du_ref[...].astype(F32)
        dconv = d * b
        dcn = jnp.where(i < nt - 1, dun_ref[...].astype(F32) * next_ref[:, 0:D].astype(F32), 0.0)
        d_ext = jnp.concatenate([dconv, dcn], axis=0)
        d1 = pltpu.roll(d_ext, tm + HALO - 1, 0)[:tm, :]
        d2 = pltpu.roll(d_ext, tm + HALO - 2, 0)[:tm, :]
        dz = k2 * dconv + k1 * d1 + k0 * d2
        o_ref[:, 0:D] = (d * conv).astype(BF16)
        o_ref[:, D:2 * D] = (dz * xin).astype(BF16)
        o_ref[:, 2 * D:3 * D] = (dz * cg).astype(BF16)

        @pl.when(i == 0)
        def _():
            dk_ref[...] = jnp.zeros_like(dk_ref)

        dk_ref[0] += _colsum8(dconv * z2)
        dk_ref[1] += _colsum8(dconv * z1)
        dk_ref[2] += _colsum8(dconv * z)

    last = S // HALO - 1
    return pl.pallas_call(
        body, name=name, grid=(nt,),
        in_specs=[_row_spec(tm, D),
                  pl.BlockSpec((HALO, D), lambda i: (jnp.minimum((i + 1) * hb, last), 0)),
                  _row_spec(tm, D3),
                  pl.BlockSpec((HALO, D3), lambda i: (jnp.maximum(i * hb - 1, 0), 0)),
                  pl.BlockSpec((HALO, D3), lambda i: (jnp.minimum((i + 1) * hb, last), 0)),
                  _vec_spec(8, D)],
        out_specs=[_row_spec(tm, D3), pl.BlockSpec((3, 8, D), lambda i: (0, 0, 0))],
        out_shape=[jax.ShapeDtypeStruct((S, D3), BF16), jax.ShapeDtypeStruct((3, 8, D), F32)],
        compiler_params=_params("arbitrary"),
    )(du, du, bcx, bcx, bcx, ck)


def _rel_onehot():
    a = np.arange(CHUNK)[:, None]
    b = np.arange(CHUNK)[None, :]
    idx = np.stack([np.clip((N_LEFT_CHUNKS - dl) * CHUNK + a - b, -MAX_REL, MAX_REL) + MAX_REL
                    for dl in (6, 7, 8)]).reshape(-1)
    return (jnp.asarray(idx)[:, None] == jnp.arange(N_REL)[None, :]).astype(F32)


def _bias_table(rel_bias, name):
    H = rel_bias.shape[0]
    near = jnp.dot(rel_bias, _rel_onehot().T, precision=lax.Precision.HIGHEST).reshape(H, 3, CHUNK, CHUNK)
    far = jnp.broadcast_to(rel_bias[:, N_REL - 1][:, None, None], (H, CHUNK, CHUNK))

    def body(near_ref, far_ref, o_ref):
        neg = jnp.full((CHUNK, CHUNK), NEG, F32)
        for v in range(N_WIN):
            for ic in range(Q_CHUNKS):
                for jc in range(N_WIN * Q_CHUNKS):
                    dl = jc - ic
                    if dl < 0 or dl > N_LEFT_CHUNKS or jc < (N_WIN - 1 - v) * Q_CHUNKS:
                        blk = neg
                    else:
                        blk = far_ref[...] if dl <= 5 else near_ref[dl - 6]
                    o_ref[v, ic * CHUNK:(ic + 1) * CHUNK, jc * CHUNK:(jc + 1) * CHUNK] = blk

    return pl.pallas_call(
        body, name=name, grid=(H,),
        in_specs=[pl.BlockSpec((None, 3, CHUNK, CHUNK), lambda h: (h, 0, 0, 0)),
                  pl.BlockSpec((None, CHUNK, CHUNK), lambda h: (h, 0, 0))],
        out_specs=pl.BlockSpec((N_WIN, None, BQ, N_WIN * BQ), lambda h: (0, h, 0, 0)),
        out_shape=jax.ShapeDtypeStruct((N_WIN, H, BQ, N_WIN * BQ), F32),
        compiler_params=_params("parallel"),
    )(near, far)


NEAR_FIRST = 6
SLAB_ROWS = 2 * CHUNK
SLAB_COLS = 4 * CHUNK


def _slab(pair):
    c0 = (NEAR_FIRST + 2 * pair) * CHUNK
    return slice(pair * SLAB_ROWS, (pair + 1) * SLAB_ROWS), slice(c0, c0 + SLAB_COLS)


def _bias_table_grad(dslab):
    H = dslab.shape[0]

    def blk(ic, dl):
        pair, r, col = ic // 2, ic % 2, ic + dl - NEAR_FIRST - 2 * (ic // 2)
        return dslab[:, pair, r * CHUNK:(r + 1) * CHUNK, col * CHUNK:(col + 1) * CHUNK]

    by_dl = [sum(blk(ic, dl) for ic in range(Q_CHUNKS)) for dl in (6, 7, 8)]
    near = jnp.stack(by_dl, axis=1).reshape(H, 3 * CHUNK * CHUNK)
    g = jnp.dot(near, _rel_onehot(), precision=lax.Precision.HIGHEST)
    return g.at[:, N_REL - 1].add(-jnp.sum(near, axis=1))


def _attn_specs(nblk, W):
    last = nblk - 1
    q_spec = pl.BlockSpec((BQ, W), lambda g, i: (jnp.minimum(i, last), g))
    kv_specs = [pl.BlockSpec((BQ, 2 * W), functools.partial(
        lambda g, i, w: (jnp.maximum(jnp.minimum(i, last) - (N_WIN - 1) + w, 0), g), w=w)) for w in range(N_WIN)]
    tab_spec = pl.BlockSpec((None, HEADS_PER_STEP, BQ, N_WIN * BQ),
                            lambda g, i: (jnp.minimum(i, N_WIN - 1), g, 0, 0))
    dtab_spec = pl.BlockSpec((HEADS_PER_STEP, Q_CHUNKS // 2, SLAB_ROWS, SLAB_COLS), lambda g, i: (g, 0, 0, 0))
    return q_spec, kv_specs, tab_spec, dtab_spec


def _attn_scores(q_ref, kT, tab_ref, h, dh):
    return jnp.dot(q_ref[:, h * dh:(h + 1) * dh], kT[h * dh:(h + 1) * dh, :], preferred_element_type=F32) + tab_ref[h]


def _attn_fwd(q, kv, tab, name):
    S, D = q.shape
    dh = D // N_HEADS
    W = HEADS_PER_STEP * dh
    assert 2 * W == D, "the kv layout puts one head group's k beside its v: two head groups"
    q_spec, kv_specs, tab_spec, _ = _attn_specs(S // BQ, W)

    def body(q_ref, *rest):
        tab_ref, o_ref = rest[N_WIN], rest[N_WIN + 1]
        kvw = jnp.concatenate([r[...] for r in rest[:N_WIN]], axis=0)
        kT = kvw[:, :W].T
        vw = kvw[:, W:]
        outs = []
        s = _attn_scores(q_ref, kT, tab_ref, 0, dh)
        for h in range(HEADS_PER_STEP):
            s_next = _attn_scores(q_ref, kT, tab_ref, h + 1, dh) if h + 1 < HEADS_PER_STEP else None
            e = jnp.exp(s - jnp.max(s, axis=-1, keepdims=True))
            l = jnp.sum(e, axis=-1, keepdims=True)
            outs.append(jnp.dot(e.astype(BF16), vw[:, h * dh:(h + 1) * dh], preferred_element_type=F32) / l)
            s = s_next
        o_ref[...] = jnp.concatenate(outs, axis=1).astype(BF16)

    return pl.pallas_call(
        body, name=name, grid=(N_HEADS // HEADS_PER_STEP, S // BQ),
        in_specs=[q_spec] + kv_specs + [tab_spec],
        out_specs=q_spec,
        out_shape=jax.ShapeDtypeStruct((S, D), BF16),
        compiler_params=_params("parallel", "parallel"),
    )(q, *([kv] * N_WIN), tab)


def _attn_bwd(q, kv, tab, do, name):
    S, D = q.shape
    dh = D // N_HEADS
    W = HEADS_PER_STEP * dh
    nblk = S // BQ
    q_spec, kv_specs, tab_spec, dtab_spec = _attn_specs(nblk, W)

    def body(q_ref, *rest):
        tab_ref, do_ref, dq_ref, dkv_ref, dtab_ref, ring = rest[N_WIN:]
        i = pl.program_id(1)

        @pl.when(i == 0)
        def _():
            dtab_ref[...] = jnp.zeros_like(dtab_ref)
            ring[...] = jnp.zeros_like(ring)

        @pl.when(i < nblk)
        def _():
            kvw = jnp.concatenate([r[...] for r in rest[:N_WIN]], axis=0)
            kT = kvw[:, :W].T
            vw = kvw[:, W:]
            qT = q_ref[...].T
            dqs, dks, dvs = [], [], []

            s = _attn_scores(q_ref, kT, tab_ref, 0, dh)
            for h in range(HEADS_PER_STEP):
                hd = slice(h * dh, (h + 1) * dh)
                do_h = do_ref[:, hd]
                dp = lax.dot_general(do_h, vw[:, hd], _DIMS["nt"], preferred_element_type=F32)
                e = jnp.exp(s - jnp.max(s, axis=-1, keepdims=True))
                inv_l = 1.0 / jnp.sum(e, axis=-1, keepdims=True)
                if h + 1 < HEADS_PER_STEP:
                    s = _attn_scores(q_ref, kT, tab_ref, h + 1, dh)
                delta = jnp.sum(e * dp, axis=-1, keepdims=True) * inv_l
                ds = e * ((dp - delta) * inv_l)
                for pair in range(Q_CHUNKS // 2):
                    rows, cols = _slab(pair)
                    dtab_ref[h, pair] += ds[rows, cols]
                dsb = ds.astype(BF16)
                dqs.append(lax.dot_general(kT[hd, :], dsb, _DIMS["nt"], preferred_element_type=F32) * (dh ** -0.5))
                dks.append(jnp.dot(qT[hd, :], dsb, preferred_element_type=F32))
                do_s = (do_h.astype(F32) * inv_l).astype(BF16)
                dvs.append(jnp.dot(do_s.T, e.astype(BF16), preferred_element_type=F32))
            dq_ref[...] = jnp.concatenate(dqs, axis=0).T.astype(BF16)
            dkv = jnp.concatenate(dks + dvs, axis=0).T
            for w in range(N_WIN):
                slot = lax.rem(i + 1 + w, N_WIN)
                part = dkv[w * BQ:(w + 1) * BQ, :]
                if w == N_WIN - 1:
                    ring[slot] = part
                else:
                    ring[slot] += part

        dkv_ref[...] = ring[lax.rem(i + 1, N_WIN)].astype(BF16)

    done_spec = pl.BlockSpec((BQ, 2 * W), lambda g, i: (jnp.maximum(i - (N_WIN - 1), 0), g))
    return pl.pallas_call(
        body, name=name, grid=(N_HEADS // HEADS_PER_STEP, nblk + N_WIN - 1),
        in_specs=[q_spec] + kv_specs + [tab_spec, q_spec],
        out_specs=[q_spec, done_spec, dtab_spec],
        out_shape=[jax.ShapeDtypeStruct((S, D), BF16), jax.ShapeDtypeStruct((S, 2 * D), BF16),
                   jax.ShapeDtypeStruct((N_HEADS, Q_CHUNKS // 2, SLAB_ROWS, SLAB_COLS), F32)],
        scratch_shapes=[pltpu.VMEM((N_WIN, BQ, 2 * W), F32)],
        compiler_params=_params("parallel", "arbitrary"),
    )(q, *([kv] * N_WIN), tab, do)


def _adamw(w, g, m, v, name, echo=False):
    shape = w.shape
    C = shape[-1]
    R = int(np.prod(shape[:-1])) if len(shape) > 1 else 1
    whole = len(shape) >= 2 and R * C <= SMALL_TENSOR_ELEMS
    if whole:
        w2, g2, m2, v2 = w, g, m, v
    else:
        w2, g2, m2, v2 = (t.reshape(R, C) for t in (w, g, m, v))
    tr = _pick(R, max(8, (512 * 1024) // C // 8 * 8), 8)

    def body(w_ref, g_ref, m_ref, v_ref, *out_refs):
        d_ref, nm_ref, nv_ref = out_refs[-3:]
        gv = g_ref[...]
        if echo:
            out_refs[0][...] = gv
        nm = ADAM_B1 * m_ref[...] + (1.0 - ADAM_B1) * gv
        nv = ADAM_B2 * v_ref[...] + (1.0 - ADAM_B2) * jnp.square(gv)
        m_hat = nm / (1.0 - ADAM_B1 ** ADAM_STEP)
        v_hat = nv / (1.0 - ADAM_B2 ** ADAM_STEP)
        d_ref[...] = -ADAM_LR * (m_hat / (jnp.sqrt(v_hat) + ADAM_EPS) + ADAM_WD * w_ref[...])
        nm_ref[...] = nm
        nv_ref[...] = nv

    if whole:
        spec, grid = pl.BlockSpec(shape, lambda i: (0,) * len(shape)), (1,)
    else:
        spec, grid = pl.BlockSpec((tr, C), lambda i: (i, 0)), (R // tr,)
    outs = pl.pallas_call(
        body, name=name, grid=grid,
        in_specs=[spec] * 4, out_specs=[spec] * (3 + echo),
        out_shape=[jax.ShapeDtypeStruct(w2.shape, F32)] * (3 + echo),
        compiler_params=_params("parallel"),
    )(w2, g2, m2, v2)
    return tuple(o.reshape(shape) for o in outs)


def _sum_rows(a, name):
    n, L = a.shape

    def body(a_ref, o_ref):
        acc = a_ref[0:1, :]
        for r in range(1, n):
            acc = acc + a_ref[r:r + 1, :]
        o_ref[...] = acc

    return pl.pallas_call(
        body, name=name, grid=(1,),
        in_specs=[pl.BlockSpec((n, L), lambda i: (0, 0))],
        out_specs=pl.BlockSpec((1, L), lambda i: (0, 0)),
        out_shape=jax.ShapeDtypeStruct((1, L), F32),
        compiler_params=_params("arbitrary"),
    )(a)


def _scalar_call(body, name, scalar, grid, in_specs, out_spec, out_shape, args):
    return pl.pallas_call(
        body, name=name,
        grid_spec=pltpu.PrefetchScalarGridSpec(num_scalar_prefetch=1, grid=grid, in_specs=in_specs,
                                               out_specs=out_spec),
        out_shape=out_shape, compiler_params=_params("parallel"),
    )(jnp.reshape(scalar, (-1,)).astype(jnp.int32), *args)


def _pair_sum(view, got, c, name):
    nb, _, rh, cols = view.shape
    tr = _pick(rh, max(16, (1 << 20) // cols // 16 * 16), 16)
    bpr = rh // tr

    def body(s_ref, a_ref, b_ref, o_ref):
        o_ref[...] = (a_ref[...].astype(F32) + b_ref[...].astype(F32)).astype(BF16)

    spec = pl.BlockSpec((tr, cols), lambda i, s: (i, 0))
    mine = pl.BlockSpec((tr, cols), lambda i, s: ((2 * (i // bpr) + s[0]) * bpr + i % bpr, 0))
    return _scalar_call(body, name, c, (nb * bpr,), [mine, spec], spec,
                        jax.ShapeDtypeStruct((nb * rh, cols), BF16),
                        (view.reshape(nb * 2 * rh, cols), got.reshape(nb * rh, cols)))


STACKED_LAYERS = 2


def _owner_sum(pair, recv, me, c, it, name, layer=None, into=None):
    _, rh, bc = recv.shape
    tr = _pick(rh, max(16, (1 << 19) // bc // 16 * 16), 16)
    bpr = rh // tr

    def body(s_ref, a_ref, r0, r1, r2, *rest):
        rest[-1][...] = ((a_ref[...].astype(F32) + r0[...].astype(F32)) + r1[...].astype(F32)) + r2[...].astype(F32)

    if it.kind == "col":
        own = pl.BlockSpec((tr, bc), lambda i, s: (i, s[0]))
    else:
        own = pl.BlockSpec((tr, bc), lambda i, s: (s[0] * bpr + i, 0))
    slots = [pl.BlockSpec((None, tr, bc), functools.partial(lambda i, s, k: (k, i, 0), k=k)) for k in range(3)]
    in_specs, args, aliases = [own] + slots, [pair, recv, recv, recv], {}
    if layer is None:
        out_spec = pl.BlockSpec((tr, bc), lambda i, s: (s[1] * bpr + i, 0))
        out_shape = jax.ShapeDtypeStruct((2 * rh, bc), F32)
    else:
        out_spec = pl.BlockSpec((None, tr, bc), lambda i, s: (layer, s[1] * bpr + i, 0))
        out_shape = jax.ShapeDtypeStruct((STACKED_LAYERS, 2 * rh, bc), F32)
        if into is not None:
            in_specs.append(pl.BlockSpec(memory_space=pl.ANY))
            args.append(into)
            aliases = {len(args): 0}
    return pl.pallas_call(
        body, name=name,
        grid_spec=pltpu.PrefetchScalarGridSpec(num_scalar_prefetch=1, grid=(bpr,), in_specs=in_specs,
                                               out_specs=out_spec),
        out_shape=out_shape, input_output_aliases=aliases, compiler_params=_params("parallel"),
    )(jnp.stack([it.pos(me), c]).astype(jnp.int32), *args)


def _place():
    x, y, c = lax.axis_index("x"), lax.axis_index("y"), lax.axis_index("c")
    chips = [(1 - x, y), (x, 1 - y), (1 - x, 1 - y)]
    return x, y, c, chips


def _chip_index(px, py):
    return 2 * px + py


def _all_gather_small(x_shard, name):
    m_per, n = x_shard.shape

    def body(x_ref, out_ref, send_sems, recv_sems, local_sem):
        x, y, c, chips = _place()
        me, sibling = (x, y, c), (x, y, 1 - c)

        def rows(px, py, pc):
            return out_ref.at[pl.ds((4 * px + 2 * py + pc) * m_per, m_per), :]

        def copy(k, block, to, src=None):
            return pltpu.make_async_remote_copy(
                src_ref=rows(*block) if src is None else src, dst_ref=rows(*block),
                send_sem=send_sems.at[k], recv_sem=recv_sems.at[k], device_id=to, device_id_type=MESH)

        mine = pltpu.make_async_copy(x_ref, rows(*me), local_sem)
        mine.start()
        first = [copy(0, me, sibling, src=x_ref)]
        first += [copy(1 + j, me, (*chip, c), src=x_ref) for j, chip in enumerate(chips)]
        for cp in first:
            cp.start()
        passed = [copy(4 + j, (*chip, c), sibling) for j, chip in enumerate(chips)]
        for j, chip in enumerate(chips):
            copy(1 + j, (*chip, c), me).wait_recv()
            passed[j].start()
        copy(0, sibling, me).wait_recv()
        for j, chip in enumerate(chips):
            copy(4 + j, (*chip, 1 - c), me).wait_recv()
        for cp in first + passed:
            cp.wait_send()
        mine.wait()

    return pl.pallas_call(
        body, name=name,
        out_shape=jax.ShapeDtypeStruct((N_DEV * m_per, n), x_shard.dtype),
        in_specs=[pl.BlockSpec(memory_space=pltpu.VMEM)],
        out_specs=pl.BlockSpec(memory_space=pltpu.VMEM),
        scratch_shapes=[pltpu.SemaphoreType.DMA((7,)), pltpu.SemaphoreType.DMA((7,)), pltpu.SemaphoreType.DMA],
    )(x_shard)


def _gather_flat(vec, name):
    L = vec.shape[0]
    Lp = -(-L // 1024) * 1024
    g = _all_gather_small(jnp.pad(vec, (0, Lp - L)).reshape(8, Lp // 8), name)
    return g.reshape(N_DEV, Lp)[:, :L]


class _Item:
    def __init__(self, kind, rows, cols, arg, layer, swap=False):
        self.kind, self.rows, self.cols, self.arg, self.layer, self.swap = kind, rows, cols, arg, layer, swap

    def ref(self, refs):
        return refs[self.arg].at[self.layer]

    def pos(self, j):
        return 2 * (j % 2) + j // 2 if self.swap else j


def _block(ref, it, j, half):
    if it.kind == "col":
        ns = it.cols // N_CHIP
        return ref.at[pl.ds(half * (it.rows // 2), it.rows // 2), pl.ds(it.pos(j) * ns, ns)]
    rs = it.rows // N_CHIP
    return ref.at[pl.ds(j * rs + half * (rs // 2), rs // 2), :]


def _cast_place(w, layer, kind, pos, after, name):
    _, r, n = w.shape
    tr = _pick(r, max(16, (1 << 20) // n // 16 * 16), 16)
    bpr = r // tr

    def body(s_ref, w_ref, after_ref, o_ref):
        o_ref[...] = w_ref[...].astype(BF16)

    if kind == "col":
        full, out_idx = (1, r, N_CHIP * n), (lambda i, s: (0, i, s[0]))
    else:
        full, out_idx = (1, N_CHIP * r, n), (lambda i, s: (0, s[0] * bpr + i, 0))
    return pl.pallas_call(
        body, name=name,
        grid_spec=pltpu.PrefetchScalarGridSpec(
            num_scalar_prefetch=1, grid=(bpr,),
            in_specs=[pl.BlockSpec((None, tr, n), lambda i, s: (layer, i, 0)), pl.BlockSpec(memory_space=pl.ANY)],
            out_specs=pl.BlockSpec((None, tr, n), out_idx)),
        out_shape=jax.ShapeDtypeStruct(full, BF16),
        compiler_params=_params("parallel"),
    )(jnp.reshape(pos, (1,)).astype(jnp.int32), w, after)


HBM_SPEC = pl.BlockSpec(memory_space=pltpu.HBM)
SEM_SPEC = pl.BlockSpec(memory_space=pltpu.SEMAPHORE)
ANY_SPEC = pl.BlockSpec(memory_space=pl.ANY)
SPLIT_PARAMS = dict(has_side_effects=pltpu.SideEffectType.DATAFLOW_SIDE_EFFECTING)


def _in_hbm(a):
    return pltpu.with_memory_space_constraint(a, pltpu.HBM)


def _split_start(copies_of, bufs, n_sem, after, name):
    n = len(bufs)

    def body(*refs):
        ins, send, recv, token = refs[:n], refs[n + 1], refs[n + 2], refs[2 * n + 3]
        for cp in copies_of(ins, send, recv, False)[0]:
            cp.start()
        token[...] = jnp.zeros_like(token)

    outs = pl.pallas_call(
        body, name=name,
        out_shape=(pltpu.SemaphoreType.DMA(n_sem), pltpu.SemaphoreType.DMA(n_sem),
                   *[pltpu.HBM(b.shape, b.dtype) for b in bufs], jax.ShapeDtypeStruct((8, 128), F32)),
        in_specs=[HBM_SPEC] * n + [ANY_SPEC],
        out_specs=(SEM_SPEC, SEM_SPEC, *[HBM_SPEC] * n, pl.BlockSpec(memory_space=pltpu.VMEM)),
        input_output_aliases={t: 2 + t for t in range(n)},
        compiler_params=pltpu.CompilerParams(**SPLIT_PARAMS),
    )(*[_in_hbm(b) for b in bufs], after)
    return outs[0], outs[1], list(outs[2:2 + n]), outs[2 + n]


def _split_wait(copies_of, send, recv, bufs, after, name):
    n = len(bufs)
    after = list(after) if isinstance(after, (list, tuple)) else [after]

    def body(*refs):
        ins, send_ref, recv_ref = refs[:n], refs[n], refs[n + 1]
        sends, arrivals = copies_of(ins, send_ref, recv_ref, True)
        for cp in sends:
            cp.wait_send()
        for cp in arrivals:
            cp.wait_recv()

    return pl.pallas_call(
        body, name=name,
        out_shape=[pltpu.HBM(b.shape, b.dtype) for b in bufs],
        in_specs=[HBM_SPEC] * n + [SEM_SPEC, SEM_SPEC] + [ANY_SPEC] * len(after),
        out_specs=[HBM_SPEC] * n,
        input_output_aliases={t: t for t in range(n)},
        compiler_params=pltpu.CompilerParams(**SPLIT_PARAMS),
    )(*bufs, send, recv, *after)


def _gather_copies(items):
    def copies_of(refs, send, recv, with_arrivals):
        x, y, c, chips = _place()
        me = _chip_index(x, y)
        sends, arrivals = [], []
        for t, it in enumerate(items):
            for k, chip in enumerate(chips):
                for core in range(2):
                    mine = _block(it.ref(refs), it, me, c)
                    sends.append(pltpu.make_async_remote_copy(
                        src_ref=mine, dst_ref=mine, send_sem=send.at[6 * t + 2 * k + core],
                        recv_sem=recv.at[6 * t + 2 * k + c], device_id=(*chip, core), device_id_type=MESH))
                    if with_arrivals:
                        landed = _block(it.ref(refs), it, _chip_index(*chip), core)
                        arrivals.append(pltpu.make_async_remote_copy(
                            src_ref=landed, dst_ref=landed, send_sem=send.at[6 * t + 2 * k + core],
                            recv_sem=recv.at[6 * t + 2 * k + core], device_id=(*chip, core), device_id_type=MESH))
        return sends, arrivals

    return copies_of


def _owner_copies(items):
    n = len(items)

    def blk(ref, it, j):
        if it.kind == "col":
            ns = it.cols // N_CHIP
            return ref.at[:, pl.ds(it.pos(j) * ns, ns)]
        return ref.at[j]

    def copies_of(refs, send, recv, with_arrivals):
        x, y, c, chips = _place()
        sends, arrivals = [], []
        for t, it in enumerate(items):
            for k, chip in enumerate(chips):
                slot = refs[n + t].at[k]
                sends.append(pltpu.make_async_remote_copy(
                    src_ref=blk(refs[t], it, _chip_index(*chip)), dst_ref=slot, send_sem=send.at[3 * t + k],
                    recv_sem=recv.at[3 * t + k], device_id=(*chip, c), device_id_type=MESH))
                if with_arrivals:
                    arrivals.append(pltpu.make_async_remote_copy(
                        src_ref=slot, dst_ref=slot, send_sem=send.at[3 * t + k], recv_sem=recv.at[3 * t + k],
                        device_id=(*chip, c), device_id_type=MESH))
        return sends, arrivals

    return copies_of


def _owner_slot_shape(it):
    if it.kind == "col":
        return (3, it.rows // 2, it.cols // N_CHIP)
    return (3, it.rows // (2 * N_CHIP), it.cols)


def _pair_view(g, it):
    if it.kind == "col":
        return g.reshape(1, 2, it.rows // 2, it.cols)
    return g.reshape(N_CHIP, 2, it.rows // (2 * N_CHIP), it.cols)


def _pair_copies(n):
    def copies_of(refs, send, recv, with_arrivals):
        x, y, c, _ = _place()
        sends, arrivals = [], []
        for t in range(n):
            land = refs[n + t]
            sends.append(pltpu.make_async_remote_copy(
                src_ref=refs[t].at[:, pl.ds(1 - c, 1)], dst_ref=land, send_sem=send.at[t], recv_sem=recv.at[t],
                device_id=(x, y, 1 - c), device_id_type=MESH))
            if with_arrivals:
                arrivals.append(pltpu.make_async_remote_copy(
                    src_ref=land, dst_ref=land, send_sem=send.at[t], recv_sem=recv.at[t],
                    device_id=(x, y, 1 - c), device_id_type=MESH))
        return sends, arrivals

    return copies_of


def _half_copies(n):
    def half(ref, which):
        r2 = ref.shape[-2] // 2
        rows = pl.ds(which * r2, r2)
        return ref.at[rows, :] if len(ref.shape) == 2 else ref.at[:, rows, :]

    def copies_of(refs, send, recv, with_arrivals):
        x, y, c, _ = _place()
        sends, arrivals = [], []
        for t in range(n):
            mine = half(refs[t], c)
            sends.append(pltpu.make_async_remote_copy(
                src_ref=mine, dst_ref=mine, send_sem=send.at[t], recv_sem=recv.at[t],
                device_id=(x, y, 1 - c), device_id_type=MESH))
            if with_arrivals:
                theirs = half(refs[t], 1 - c)
                arrivals.append(pltpu.make_async_remote_copy(
                    src_ref=theirs, dst_ref=theirs, send_sem=send.at[t], recv_sem=recv.at[t],
                    device_id=(x, y, 1 - c), device_id_type=MESH))
        return sends, arrivals

    return copies_of


class _Reduction:
    pass


def _pair_start(grads, items, after, tag, names, layer=None):
    n = len(items)
    views = [_pair_view(g, it) for g, it in zip(grads, items)]
    lands = [lax.empty((v.shape[0], 1) + v.shape[2:], v.dtype) for v in views]
    r = _Reduction()
    r.items, r.tag, r.names, r.layer = items, tag, names, layer
    r.send, r.recv, r.bufs, r.token = _split_start(_pair_copies(n), views + lands, (n,), after, f"rs_pair_start_{tag}")
    return r


def _owner_start(r, after):
    x, y, c, _ = _place()
    n = len(r.items)
    bufs = _split_wait(_pair_copies(n), r.send, r.recv, r.bufs, after, f"rs_pair_wait_{r.tag}")
    pairs = [_pair_sum(bufs[t], bufs[n + t], c, f"rs_pair_sum_{r.tag}_{t}") for t in range(n)]
    shaped = [p if it.kind == "col" else p.reshape(N_CHIP, p.shape[0] // N_CHIP, p.shape[1])
              for p, it in zip(pairs, r.items)]
    lands = [lax.empty(_owner_slot_shape(it), BF16) for it in r.items]
    r.send, r.recv, r.bufs, r.token = _split_start(
        _owner_copies(r.items), shaped + lands, (3 * n,), r.token, f"rs_owner_start_{r.tag}")
    return r


def _reduce_finish(groups, after):
    x, y, c, _ = _place()
    me = _chip_index(x, y)
    halves = {}
    behind = [after]
    for r in groups:
        n = len(r.items)
        bufs = _split_wait(_owner_copies(r.items), r.send, r.recv, r.bufs, behind, f"rs_owner_wait_{r.tag}")
        for t, (it, nm) in enumerate(zip(r.items, r.names)):
            pair = bufs[t].reshape(-1, bufs[t].shape[-1])
            halves[nm] = _owner_sum(pair, bufs[n + t], me, c, it, f"rs_owner_sum_{r.tag}_{t}",
                                    layer=r.layer, into=halves.get(nm))
        behind = [after] + [halves[nm] for nm in r.names]
    n = len(halves)
    return list(halves), _split_start(_half_copies(n), list(halves.values()), (n,), after, "rs_half_start")


def _silu(v):
    return v * jax.nn.sigmoid(v)


def _sum8(p):
    return jnp.sum(p, axis=-2)


def kernel(x, c, mod_w, mod_b, norm_g, ffn_w_in, ffn_w_out, conv_w_in, conv_k, conv_w_out, kv_mod_w, kv_mod_b, kv_norm_g, w_kv, attn_w_q, attn_w_o, rel_bias, loss_target, m_mod_w, m_mod_b, m_norm_g, m_ffn_w_in, m_ffn_w_out, m_conv_w_in, m_conv_k, m_conv_w_out, m_kv_mod_w, m_kv_mod_b, m_kv_norm_g, m_w_kv, m_attn_w_q, m_attn_w_o, m_rel_bias, v_mod_w, v_mod_b, v_norm_g, v_ffn_w_in, v_ffn_w_out, v_conv_w_in, v_conv_k, v_conv_w_out, v_kv_mod_w, v_kv_mod_b, v_kv_norm_g, v_w_kv, v_attn_w_q, v_attn_w_o, v_rel_bias):
    xi, yi, ci = lax.axis_index("x"), lax.axis_index("y"), lax.axis_index("c")
    chip = 2 * xi + yi
    dev = 2 * chip + ci
    _, S, D = x.shape
    F = ffn_w_out.shape[1] * N_CHIP
    x0 = x.reshape(S, D)
    target = loss_target.reshape(S, D)
    n_mod = mod_w.shape[2]
    n_kvm = kv_mod_w.shape[1]
    dsh = D // N_CHIP
    TF = F // 2

    c_all = _all_gather_small(c.reshape(8, D // 8), "ag_c").reshape(N_DEV, D)
    sc16 = jnp.pad(_silu(c_all), ((0, 8), (0, 0)))
    part = [_mm(sc16, mod_w, "nn", F32, f"mod_fwd_{l}", b_layer=l)[:8] for l in range(2)]
    part.append(_mm(sc16, kv_mod_w, "nn", F32, "mod_fwd_kv")[:8])
    fwd_vec = jnp.concatenate([p.reshape(-1) for p in part] + [norm_g.reshape(-1), conv_k.reshape(-1)])
    fwd_all = _gather_flat(fwd_vec, "ag_fwd_small")[0::2]
    o = 0
    mods = []
    for n in (n_mod, n_mod, n_kvm):
        blk = fwd_all[:, o:o + 8 * n].reshape(N_CHIP, 8, n)
        mods.append(lax.dynamic_index_in_dim(blk, dev, axis=1, keepdims=False).reshape(N_CHIP * n))
        o += 8 * n
    ng = fwd_all[:, o:o + 8 * dsh].reshape(N_CHIP, 2, 4, dsh).transpose(1, 2, 0, 3).reshape(2, 4, D)
    o += 8 * dsh
    ck = fwd_all[:, o:o + 3 * dsh].reshape(N_CHIP, 3, dsh).transpose(1, 0, 2).reshape(3, D)
    ck8 = jnp.pad(ck, ((0, 5), (0, 0)))
    mod = [mods[l] + mod_b[l] for l in range(2)]
    sh1, sc1, g1, sh2, sc2, g2 = zip(*[jnp.split(m, 6) for m in mod])
    kv_sh, kv_sc = jnp.split(mods[2] + kv_mod_b, 2)
    row = lambda v: v.reshape(1, D)

    it_conv = [_Item("col", D, 3 * D, 0, 0), _Item("row", D, D, 1, 0)]
    it_ffn = [_Item("col", D, 2 * F, 0, 0, swap=True), _Item("row", F, D, 1, 0)]
    it_attn = [_Item("col", D, 2 * D, 0, 0, swap=True), _Item("row", D, D, 1, 0), _Item("row", D, D, 2, 0)]

    def placed(w, layer, it, nm, after=fwd_all):
        return _cast_place(w, layer, it.kind, it.pos(chip), after, f"place_{nm}")

    flying = {}

    def start(tag, its, bufs, after):
        send, recv, bufs, tok = _split_start(_gather_copies(its), bufs, (6 * len(its),), after, f"ag_start_{tag}")
        flying[tag] = (its, send, recv, bufs)
        return tok

    def arrived(tag, after):
        its, send, recv, bufs = flying[tag]
        return _split_wait(_gather_copies(its), send, recv, bufs, after, f"ag_wait_{tag}")

    one = lambda it: [_Item(it.kind, it.rows, it.cols, 0, 0, it.swap)]
    tok = start("conv_in", one(it_conv[0]), [placed(conv_w_in, 0, it_conv[0], "conv_w_in")], fwd_all)
    tok = start("conv_out", one(it_conv[1]), [placed(conv_w_out, 0, it_conv[1], "conv_w_out", tok)], tok)
    tok = start("ffn0_in", one(it_ffn[0]), [placed(ffn_w_in, 0, it_ffn[0], "ffn_w_in0", tok)], tok)
    tok = start("ffn0_out", one(it_ffn[1]), [placed(ffn_w_out, 0, it_ffn[1], "ffn_w_out0", tok)], tok)
    tok = start("attn", it_attn, [placed(w_kv[None], 0, it_attn[0], "w_kv", tok),
                                  placed(attn_w_q, 0, it_attn[1], "attn_w_q", tok),
                                  placed(attn_w_o, 0, it_attn[2], "attn_w_o", tok)], tok)
    token = start("ffn1", it_ffn, [placed(ffn_w_in, 1, it_ffn[0], "ffn_w_in1", tok),
                                   placed(ffn_w_out, 1, it_ffn[1], "ffn_w_out1", tok)], tok)

    a1 = row(ng[0, 0] * (1.0 + sc1[0])) + token[0, 0]
    behind_starts = lax.optimization_barrier(rel_bias[0] + token[0, 0])
    tab = _bias_table(behind_starts, "l1_bias_table")
    (W_cin,) = arrived("conv_in", tab)
    h1, bcx, ug = _conv_in_gate(x0, a1, row(sh1[0]), W_cin, ck8, "l0_norm_conv_in_gate")
    gt1 = row(g1[0] * ng[0, 1])
    a2 = row(ng[0, 2] * (1.0 + sc2[0]))
    (W_cout,) = arrived("conv_out", ug)
    y1, x1, h2 = _mm_post(ug, W_cout, x0, gt1, "l0_conv_out", scales=a2, shifts=row(sh2[0]))
    (W_fin0,) = arrived("ffn0_in", h2)
    gu0, act0 = _ffn_in_act(h2, W_fin0, 0, "l0_ffn_in")
    (W_fout0,) = arrived("ffn0_out", act0)
    gt2 = row(g2[0] * ng[0, 3])
    a3 = ng[1, 0] * (1.0 + sc1[1])
    akv = kv_norm_g * (1.0 + kv_sc)
    y2, x2, h3, hkv = _mm_post(act0, W_fout0, x1, gt2, "l0_ffn_out",
                               scales=jnp.stack([a3, akv]), shifts=jnp.stack([sh1[1], kv_sh]))
    W_kv, W_q, W_o = arrived("attn", hkv)
    kvp = _mm(hkv, W_kv, "nn", BF16, "l1_kv", b_layer=0, tm=512, tn=2 * D)
    att_scale = (D // N_HEADS) ** -0.5
    assert math.log2(att_scale) % 1 == 0, "scaling q before its bf16 cast is exact only for a power of two"
    qp = _mm(h3, W_q, "nn", BF16, "l1_q", b_layer=0, scale=att_scale)
    oh = _attn_fwd(qp, kvp, tab, "l1_attn")
    gt3 = row(g1[1] * ng[1, 1])
    a4 = row(ng[1, 2] * (1.0 + sc2[1]))
    y3, x3, h4 = _mm_post(oh, W_o, x2, gt3, "l1_attn_out", scales=a4, shifts=row(sh2[1]))
    W_fin1, W_fout1 = arrived("ffn1", h4)
    gu1, act1 = _ffn_in_act(h4, W_fin1, 0, "l1_ffn_in")
    gt4 = row(g2[1] * ng[1, 3])
    dx4, sq, dy4, dgt4 = _mm_post(act1, W_fout1, x3, gt4, "l1_ffn_out", target=target)
    loss_part = 0.5 * jnp.sum(sq) / D

    def ffn_bwd(dy, dxn, xin_, h, gu, act, a, w_in, w_out, post, tag):
        dgu, dx, ds, db, dyn, dgt = _ffn_bwd(dy, w_out, gu, w_in, xin_, dxn, a, post, f"{tag}_ffn_bwd")
        g_fout = _mm(act, dy, "tn", BF16, f"{tag}_ffn_out_dw", tm=TF)
        g_fin = _mm(h, dgu, "tn", BF16, f"{tag}_ffn_in_dw", tn=TF)
        return dx, ds, db, dyn, dgt, g_fin, g_fout

    dx3, ds4, db4, dy3, dgt3, G_fin1, G_fout1 = ffn_bwd(dy4, dx4, x3, h4, gu1, act1, a4, W_fin1, W_fout1,
                                                        (y3, gt3), "l1")
    red = [_pair_start([G_fin1, G_fout1], it_ffn, token, "ffn1", ["ffn_w_in", "ffn_w_out"], layer=1)]
    doh = _mm(dy3, W_o, "nt", BF16, "l1_attn_out_dx", b_layer=0, after=red[0].token)
    G_o = _mm(oh, dy3, "tn", BF16, "l1_attn_out_dw")
    _owner_start(red[0], G_o)
    dq, dkv, dtab = _attn_bwd(qp, kvp, tab, doh, "l1_attn_bwd")
    d_rel = _bias_table_grad(dtab)
    G_q = _mm(h3, dq, "tn", BF16, "l1_q_dw")
    G_kv = _mm(hkv, dkv, "tn", BF16, "l1_kv_dw")
    red.append(_pair_start([G_kv, G_q, G_o], it_attn, red[-1].token, "attn", ["w_kv", "attn_w_q", "attn_w_o"]))
    dx2, ds3, db3, dy2, dgt2 = _mm_pre_bwd([(dq, W_q), (dkv, W_kv)], x2, dx3,
                                           jnp.stack([a3, akv]) + red[1].token[0, 0], "l1_qkv_dx", post=(y2, gt2))
    _owner_start(red[1], dx2)

    dx1, ds2, db2, dy1, dgt1, G_fin0, G_fout0 = ffn_bwd(dy2, dx2, x1, h2, gu0, act0, a2, W_fin0, W_fout0,
                                                        (y1, gt1), "l0")
    red.append(_pair_start([G_fin0, G_fout0], it_ffn, red[-1].token, "ffn0", ["ffn_w_in", "ffn_w_out"], layer=0))
    dug = _mm(dy1, W_cout, "nt", BF16, "l0_conv_out_dx", b_layer=0, after=red[2].token)
    G_cout = _mm(ug, dy1, "tn", BF16, "l0_conv_out_dw")
    dbcx, dck = _conv_gate_bwd(dug, bcx, ck8, "l0_conv_gate_bwd")
    _owner_start(red[2], dbcx)
    G_cin = _mm(h1, dbcx, "tn", BF16, "l0_conv_in_dw")
    red.append(_pair_start([G_cin, G_cout], it_conv, red[-1].token, "conv", ["conv_w_in", "conv_w_out"]))
    dx0, ds1, db1 = _mm_pre_bwd([(dbcx, W_cin)], x0, dx1, a1 + red[3].token[0, 0], "l0_conv_in_dx")
    ds1, db1 = _sum8(ds1)[0], _sum8(db1)[0]
    da2, db2 = _sum8(ds2)[0], _sum8(db2)[0]
    ds3, db3 = _sum8(ds3), _sum8(db3)
    da4, db4 = _sum8(ds4)[0], _sum8(db4)[0]
    dgt1, dgt2, dgt3, dgt4 = _sum8(dgt1), _sum8(dgt2), _sum8(dgt3), _sum8(dgt4)

    def dmod_of(l, ds_a, db_a, dgt_a, ds_b, db_b, dgt_b):
        return jnp.concatenate([db_a, ds_a * ng[l, 0], dgt_a * ng[l, 1], db_b, ds_b * ng[l, 2], dgt_b * ng[l, 3]])

    dmod0 = dmod_of(0, ds1, db1, dgt1, da2, db2, dgt2)
    dmod1 = dmod_of(1, ds3[0], db3[0], dgt3, da4, db4, dgt4)
    dkvmod = jnp.concatenate([db3[1], ds3[1] * kv_norm_g])
    dng = jnp.stack([
        jnp.stack([ds1 * (1.0 + sc1[0]), dgt1 * g1[0], da2 * (1.0 + sc2[0]), dgt2 * g2[0]]),
        jnp.stack([ds3[0] * (1.0 + sc1[1]), dgt3 * g1[1], da4 * (1.0 + sc2[1]), dgt4 * g2[1]])])
    dkvng = ds3[1] * (1.0 + kv_sc)
    small = [dmod0, dmod1, dkvmod, dng.reshape(-1), dkvng, _sum8(dck).reshape(-1), d_rel.reshape(-1),
             loss_part.reshape(1)]
    sizes = [int(s.shape[0]) for s in small]
    offs = np.concatenate([[0], np.cumsum(sizes)])
    bwd_all = _gather_flat(jnp.concatenate(small), "ag_bwd_small")
    _owner_start(red[3], bwd_all)
    Lb = bwd_all.shape[1]
    Lp = -(-Lb // 128) * 128
    tot = _sum_rows(jnp.pad(bwd_all, ((0, 0), (0, Lp - Lb))), "sum_small")[0]
    seg = lambda i: tot[offs[i]:offs[i + 1]]
    g_mod_b = jnp.stack([seg(0), seg(1)])
    g_kv_mod_b = seg(2)
    g_norm_g = lax.dynamic_slice_in_dim(seg(3).reshape(2, 4, D), chip * dsh, dsh, axis=2)
    g_kv_norm_g = seg(4)
    g_conv_k = lax.dynamic_slice_in_dim(seg(5).reshape(1, 3, D), chip * dsh, dsh, axis=2)
    g_rel_bias = seg(6).reshape(rel_bias.shape)
    loss = seg(7)[0]

    def dmod_rows(i, n):
        rows_ = lax.dynamic_slice_in_dim(bwd_all[:, offs[i]:offs[i + 1]], chip * n, n, axis=1)
        return jnp.pad(rows_, ((0, 8), (0, 0)))

    g_mod_w = _mm(sc16, jnp.concatenate([dmod_rows(0, n_mod), dmod_rows(1, n_mod)], axis=1), "tn", F32,
                  "mod_bwd", out_layers=STACKED_LAYERS)
    g_kv_mod_w = _mm(sc16, dmod_rows(2, n_kvm), "tn", F32, "mod_bwd_kv")

    grads = {
        "mod_w": g_mod_w, "mod_b": g_mod_b, "norm_g": g_norm_g, "conv_k": g_conv_k,
        "kv_mod_w": g_kv_mod_w, "kv_mod_b": g_kv_mod_b, "kv_norm_g": g_kv_norm_g, "rel_bias": g_rel_bias,
    }
    weights = dict(mod_w=mod_w, mod_b=mod_b, norm_g=norm_g, ffn_w_in=ffn_w_in, ffn_w_out=ffn_w_out,
                   conv_w_in=conv_w_in, conv_k=conv_k, conv_w_out=conv_w_out, kv_mod_w=kv_mod_w,
                   kv_mod_b=kv_mod_b, kv_norm_g=kv_norm_g, w_kv=w_kv, attn_w_q=attn_w_q, attn_w_o=attn_w_o,
                   rel_bias=rel_bias)
    m_in = dict(mod_w=m_mod_w, mod_b=m_mod_b, norm_g=m_norm_g, ffn_w_in=m_ffn_w_in, ffn_w_out=m_ffn_w_out,
                conv_w_in=m_conv_w_in, conv_k=m_conv_k, conv_w_out=m_conv_w_out, kv_mod_w=m_kv_mod_w,
                kv_mod_b=m_kv_mod_b, kv_norm_g=m_kv_norm_g, w_kv=m_w_kv, attn_w_q=m_attn_w_q,
                attn_w_o=m_attn_w_o, rel_bias=m_rel_bias)
    v_in = dict(mod_w=v_mod_w, mod_b=v_mod_b, norm_g=v_norm_g, ffn_w_in=v_ffn_w_in, ffn_w_out=v_ffn_w_out,
                conv_w_in=v_conv_w_in, conv_k=v_conv_k, conv_w_out=v_conv_w_out, kv_mod_w=v_kv_mod_w,
                kv_mod_b=v_kv_mod_b, kv_norm_g=v_kv_norm_g, w_kv=v_w_kv, attn_w_q=v_attn_w_q,
                attn_w_o=v_attn_w_o, rel_bias=v_rel_bias)
    names = list(weights)
    step = {}

    def update(n, echo=False):
        g = grads[n].reshape(weights[n].shape)
        outs = _adamw(weights[n], g, m_in[n], v_in[n], f"adamw_{n}", echo=echo)
        step[n] = outs if echo else (g, *outs)

    update("mod_w")
    reduced, (half_send, half_recv, half_bufs, _) = _reduce_finish(red, step["mod_w"][1])
    local = [n for n in grads if n != "mod_w"]
    for n in local:
        update(n)
    grads.update(zip(reduced, _split_wait(
        _half_copies(len(half_bufs)), half_send, half_recv, half_bufs, [step[n][1] for n in local], "rs_half_wait")))
    for n in reduced:
        update(n, echo=True)
    return (loss, dx0.reshape(x.shape), *[step[n][k] for k in range(4) for n in names])
```
